```python
import jax, jax.numpy as jnp
from jax import lax
import numpy as np

D_MODEL = 1024
BATCH = 32
SEQ = 2048
DEPTH = 2

GRID_W = 64
CTX_LEN = 256
HEAD_DIM = 64
AXIS_DIM = HEAD_DIM // 2
ROPE_THETA = 10000.0
A_HEADS = 6
A_KV = 2
A_GROUP = A_HEADS // A_KV
C_HEADS = 6
C_KV = 2
C_GROUP = C_HEADS // C_KV
POOL_GROUPS = 4
POOL_CH = 64
POOL_WIDTH = POOL_GROUPS * POOL_CH
POOL_WINDOWS = (2, 4, 8, 16)
WINDOW = 128
Q_BLOCK = 128
BAND = Q_BLOCK + 2 * WINDOW
D_FF = 4 * D_MODEL
N_BRANCH = 3
A_QW = A_HEADS * HEAD_DIM
A_KVW = A_KV * HEAD_DIM
C_QW = C_HEADS * HEAD_DIM
C_KVW = C_KV * HEAD_DIM
IN_SPLITS = (A_QW, A_KVW, A_KVW, C_QW, C_KVW, C_KVW, POOL_WIDTH, D_MODEL, D_MODEL, D_MODEL)
IN_WIDTH = A_QW + 2 * A_KVW + C_QW + 2 * C_KVW + POOL_WIDTH + N_BRANCH * D_MODEL
EPS = 1e-6
NEG = -1e30

kernel_name = "hybrid_prefix_gqa_pool_window_block"


def rmsnorm(x, g):
    xf = x.astype(jnp.float32)
    y = xf * lax.rsqrt(jnp.mean(xf * xf, axis=-1, keepdims=True) + EPS)
    return (y * g.astype(jnp.float32)).astype(x.dtype)


def modulate(x, g, shift, scale):
    return rmsnorm(x, g) * (1 + scale) + shift


def adaln(v, w, b):
    m = jax.nn.silu(v) @ w + b
    return jnp.split(m, 6, axis=-1)


def split_in(z):
    idx = []
    acc = 0
    for s in IN_SPLITS[:-1]:
        acc += s
        idx.append(acc)
    return jnp.split(z, idx, axis=-1)


def heads_q(t, n_kv, n_group):
    b, l, _ = t.shape
    return t.reshape(b, l, n_kv, n_group, HEAD_DIM)


def heads_kv(t, n_kv):
    b, l, _ = t.shape
    return t.reshape(b, l, n_kv, HEAD_DIM)


def rope_tables(n_tok):
    rows = n_tok // GRID_W
    r = jnp.repeat(jnp.arange(rows, dtype=jnp.float32), GRID_W)
    col = jnp.tile(jnp.arange(GRID_W, dtype=jnp.float32), rows)
    inv = 1.0 / (ROPE_THETA ** (jnp.arange(0, AXIS_DIM, 2, dtype=jnp.float32) / AXIS_DIM))
    ang = jnp.concatenate([r[:, None] * inv, col[:, None] * inv], axis=-1)
    return jnp.cos(ang), jnp.sin(ang)


def apply_rope(x, cos, sin):
    shp = x.shape
    xr = x.reshape(shp[:-1] + (shp[-1] // 2, 2))
    x0, x1 = xr[..., 0], xr[..., 1]
    bshape = (shp[1],) + (1,) * (x.ndim - 3) + (shp[-1] // 2,)
    cs = cos.reshape(bshape).astype(x.dtype)
    sn = sin.reshape(bshape).astype(x.dtype)
    return jnp.stack([x0 * cs - x1 * sn, x0 * sn + x1 * cs], axis=-1).reshape(shp)


def global_attn(q, k, v):
    b, s, hk, g, dh = q.shape
    nblk = s // Q_BLOCK
    qb = q.reshape(b, nblk, Q_BLOCK, hk, g, dh).transpose(1, 0, 2, 3, 4, 5)
    scale = dh ** -0.5

    def one_block(qi):
        sc = jnp.einsum('bqhgd,bkhd->bhgqk', qi, k).astype(jnp.float32) * scale
        p = jax.nn.softmax(sc, axis=-1).astype(v.dtype)
        return jnp.einsum('bhgqk,bkhd->bqhgd', p, v)

    o = lax.map(one_block, qb)
    return o.transpose(1, 0, 2, 3, 4, 5).reshape(b, s, hk * g * dh)


def window_attn(q, k, v, kc, vc, sink):
    b, s, hk, g, dh = q.shape
    nctx = kc.shape[1]
    nblk = s // Q_BLOCK
    scale = dh ** -0.5
    k_pad = jnp.pad(k, ((0, 0), (WINDOW, WINDOW), (0, 0), (0, 0)))
    v_pad = jnp.pad(v, ((0, 0), (WINDOW, WINDOW), (0, 0), (0, 0)))
    qb = q.reshape(b, nblk, Q_BLOCK, hk, g, dh).transpose(1, 0, 2, 3, 4, 5)
    sink_f = sink.astype(jnp.float32).reshape(1, hk, g, 1, 1)

    def one_block(args):
        qi, bi = args
        start = bi * Q_BLOCK
        kb = lax.dynamic_slice_in_dim(k_pad, start, BAND, axis=1)
        vb = lax.dynamic_slice_in_dim(v_pad, start, BAND, axis=1)
        qpos = start + jnp.arange(Q_BLOCK)
        kpos = start - WINDOW + jnp.arange(BAND)
        valid = (jnp.abs(qpos[:, None] - kpos[None, :]) <= WINDOW) & (kpos[None, :] >= 0) & (kpos[None, :] < s)
        s_loc = jnp.einsum('bqhgd,bkhd->bhgqk', qi, kb).astype(jnp.float32) * scale
        s_loc = jnp.where(valid, s_loc, NEG)
        s_ctx = jnp.einsum('bqhgd,bkhd->bhgqk', qi, kc).astype(jnp.float32) * scale
        s_sink = jnp.broadcast_to(sink_f, s_ctx.shape[:-1] + (1,))
        p = jax.nn.softmax(jnp.concatenate([s_ctx, s_loc, s_sink], axis=-1), axis=-1).astype(v.dtype)
        o = jnp.einsum('bhgqk,bkhd->bqhgd', p[..., :nctx], vc)
        return o + jnp.einsum('bhgqk,bkhd->bqhgd', p[..., nctx:nctx + BAND], vb)

    o = lax.map(one_block, (qb, jnp.arange(nblk)))
    return o.transpose(1, 0, 2, 3, 4, 5).reshape(b, s, hk * g * dh)


def ctx_attn(q, k, v, sink):
    b, n, hk, g, dh = q.shape
    sc = jnp.einsum('bqhgd,bkhd->bhgqk', q, k).astype(jnp.float32) * (dh ** -0.5)
    if sink is not None:
        sk = jnp.broadcast_to(sink.astype(jnp.float32).reshape(1, hk, g, 1, 1), sc.shape[:-1] + (1,))
        p = jax.nn.softmax(jnp.concatenate([sc, sk], axis=-1), axis=-1)[..., :-1]
    else:
        p = jax.nn.softmax(sc, axis=-1)
    o = jnp.einsum('bhgqk,bkhd->bqhgd', p.astype(v.dtype), v)
    return o.reshape(b, n, hk * g * dh)


def pool_mix(u, w_pool, pool_scale):
    b, n, _ = u.shape
    uf = u.astype(jnp.float32).reshape(b, n, POOL_GROUPS, POOL_CH)
    cs = jnp.concatenate([jnp.zeros((b, 1, POOL_GROUPS, POOL_CH), jnp.float32), jnp.cumsum(uf, axis=1)], axis=1)
    t = jnp.arange(n)
    outs = []
    for gi, w in enumerate(POOL_WINDOWS):
        lo = jnp.clip(t - w // 2, 0, n)
        hi = jnp.clip(t + w - w // 2, 0, n)
        tot = cs[:, hi, gi] - cs[:, lo, gi]
        cnt = (hi - lo).astype(jnp.float32)
        outs.append(tot / cnt[None, :, None] - uf[:, :, gi])
    pooled = jnp.stack(outs, axis=2).astype(u.dtype)
    mixed = jnp.einsum('blgc,gcd->blgd', pooled, w_pool).reshape(b, n, POOL_WIDTH)
    return mixed * pool_scale


def project(h, w_in, qn_a, kn_a, qn_c, kn_c):
    qa, ka, va, qc, kc, vc, u, ga, gb, gc = split_in(h @ w_in)
    qa = rmsnorm(heads_q(qa, A_KV, A_GROUP), qn_a)
    ka = rmsnorm(heads_kv(ka, A_KV), kn_a)
    va = heads_kv(va, A_KV)
    qc = rmsnorm(heads_q(qc, C_KV, C_GROUP), qn_c)
    kc = rmsnorm(heads_kv(kc, C_KV), kn_c)
    vc = heads_kv(vc, C_KV)
    return qa, ka, va, qc, kc, vc, u, ga, gb, gc


def merge(oa, ob, oc, ga, gb, gc, w_br_a, w_br_b, w_br_c, w_out):
    y = (jax.nn.sigmoid(ga) * (oa @ w_br_a)
         + jax.nn.sigmoid(gb) * (ob @ w_br_b)
         + jax.nn.sigmoid(gc) * (oc @ w_br_c))
    return y @ w_out


def mlp(h, w1, w2):
    return jnp.square(jax.nn.relu(h @ w1)) @ w2


def _fwd_setup_inputs(seed: int = 0) -> dict:
    key = jax.random.key(seed)
    ks = jax.random.split(key, 24)
    f32 = jnp.float32
    nrm = lambda k, shp, s: jax.random.normal(k, shp, f32) * s
    return {
        "x": nrm(ks[0], (BATCH, SEQ, D_MODEL), 1.0),
        "c": nrm(ks[1], (BATCH, D_MODEL), 1.0),
        "ctx": nrm(ks[2], (BATCH, CTX_LEN, D_MODEL), 1.0),
        "c_ctx": nrm(ks[3], (D_MODEL,), 1.0),
        "w_ada": nrm(ks[4], (DEPTH, D_MODEL, 6 * D_MODEL), 0.5 * D_MODEL ** -0.5),
        "b_ada": nrm(ks[5], (DEPTH, 6 * D_MODEL), 0.02),
        "norm1": 1.0 + nrm(ks[6], (DEPTH, D_MODEL), 0.1),
        "norm2": 1.0 + nrm(ks[7], (DEPTH, D_MODEL), 0.1),
        "w_in": nrm(ks[8], (DEPTH, D_MODEL, IN_WIDTH), D_MODEL ** -0.5),
        "q_norm_a": 1.0 + nrm(ks[9], (DEPTH, HEAD_DIM), 0.1),
        "k_norm_a": 1.0 + nrm(ks[10], (DEPTH, HEAD_DIM), 0.1),
        "q_norm_c": 1.0 + nrm(ks[11], (DEPTH, HEAD_DIM), 0.1),
        "k_norm_c": 1.0 + nrm(ks[12], (DEPTH, HEAD_DIM), 0.1),
        "sink_c": nrm(ks[13], (DEPTH, C_HEADS), 0.5),
        "w_pool": nrm(ks[14], (DEPTH, POOL_GROUPS, POOL_CH, POOL_CH), POOL_CH ** -0.5),
        "pool_scale": 1.0 + nrm(ks[15], (DEPTH, POOL_WIDTH), 0.1),
        "w_br_a": nrm(ks[16], (DEPTH, A_QW, D_MODEL), A_QW ** -0.5),
        "w_br_b": nrm(ks[17], (DEPTH, POOL_WIDTH, D_MODEL), POOL_WIDTH ** -0.5),
        "w_br_c": nrm(ks[18], (DEPTH, C_QW, D_MODEL), C_QW ** -0.5),
        "w_out": nrm(ks[19], (DEPTH, D_MODEL, D_MODEL), D_MODEL ** -0.5),
        "w_mlp1": nrm(ks[20], (DEPTH, D_MODEL, D_FF), D_MODEL ** -0.5),
        "w_mlp2": nrm(ks[21], (DEPTH, D_FF, D_MODEL), D_FF ** -0.5),
    }


def _fwd_reference(x, c, ctx, c_ctx, w_ada, b_ada, norm1, norm2, w_in, q_norm_a, k_norm_a, q_norm_c, k_norm_c,
              sink_c, w_pool, pool_scale, w_br_a, w_br_b, w_br_c, w_out, w_mlp1, w_mlp2):
    n_tok = x.shape[1]
    cos, sin = rope_tables(n_tok)
    xc = ctx
    for l in range(DEPTH):
        last = l == DEPTH - 1
        sh1, sc1, g1, sh2, sc2, g2 = [m[:, None, :] for m in adaln(c, w_ada[l], b_ada[l])]
        csh1, csc1, cg1, csh2, csc2, cg2 = adaln(c_ctx, w_ada[l], b_ada[l])

        hc = modulate(xc, norm1[l], csh1, csc1)
        cqa, cka, cva, cqc, ckc, cvc, cu, cga, cgb, cgc = project(
            hc, w_in[l], q_norm_a[l], k_norm_a[l], q_norm_c[l], k_norm_c[l])

        h = modulate(x, norm1[l], sh1, sc1)
        qa, ka, va, qc, kc, vc, u, ga, gb, gc = project(
            h, w_in[l], q_norm_a[l], k_norm_a[l], q_norm_c[l], k_norm_c[l])
        qa, ka = apply_rope(qa, cos, sin), apply_rope(ka, cos, sin)
        qc, kc = apply_rope(qc, cos, sin), apply_rope(kc, cos, sin)

        oa = global_attn(qa, jnp.concatenate([cka, ka], axis=1), jnp.concatenate([cva, va], axis=1))
        ob = pool_mix(u, w_pool[l], pool_scale[l])
        oc = window_attn(qc, kc, vc, ckc, cvc, sink_c[l])
        x = x + g1 * merge(oa, ob, oc, ga, gb, gc, w_br_a[l], w_br_b[l], w_br_c[l], w_out[l])
        x = x + g2 * mlp(modulate(x, norm2[l], sh2, sc2), w_mlp1[l], w_mlp2[l])

        if not last:
            coa = ctx_attn(cqa, cka, cva, None)
            cob = pool_mix(cu, w_pool[l], pool_scale[l])
            coc = ctx_attn(cqc, ckc, cvc, sink_c[l])
            xc = xc + cg1 * merge(coa, cob, coc, cga, cgb, cgc, w_br_a[l], w_br_b[l], w_br_c[l], w_out[l])
            xc = xc + cg2 * mlp(modulate(xc, norm2[l], csh2, csc2), w_mlp1[l], w_mlp2[l])
    return x


import jax as _jax
import jax.numpy as _jnp

TWIN_FORMAT = 'train_step'
FWD_PARAMS = ['x', 'c', 'ctx', 'c_ctx', 'w_ada', 'b_ada', 'norm1', 'norm2', 'w_in', 'q_norm_a', 'k_norm_a', 'q_norm_c', 'k_norm_c', 'sink_c', 'w_pool', 'pool_scale', 'w_br_a', 'w_br_b', 'w_br_c', 'w_out', 'w_mlp1', 'w_mlp2']
TWIN_WEIGHTS = ['c_ctx', 'w_ada', 'b_ada', 'norm1', 'norm2', 'w_in', 'q_norm_a', 'k_norm_a', 'q_norm_c', 'k_norm_c', 'sink_c', 'w_pool', 'pool_scale', 'w_br_a', 'w_br_b', 'w_br_c', 'w_out', 'w_mlp1', 'w_mlp2']
TWIN_DIFF_INPUT = 'x'
TWIN_INPUTS = ['x', 'c', 'ctx', 'c_ctx', 'w_ada', 'b_ada', 'norm1', 'norm2', 'w_in', 'q_norm_a', 'k_norm_a', 'q_norm_c', 'k_norm_c', 'sink_c', 'w_pool', 'pool_scale', 'w_br_a', 'w_br_b', 'w_br_c', 'w_out', 'w_mlp1', 'w_mlp2', 'loss_target', 'm_c_ctx', 'm_w_ada', 'm_b_ada', 'm_norm1', 'm_norm2', 'm_w_in', 'm_q_norm_a', 'm_k_norm_a', 'm_q_norm_c', 'm_k_norm_c', 'm_sink_c', 'm_w_pool', 'm_pool_scale', 'm_w_br_a', 'm_w_br_b', 'm_w_br_c', 'm_w_out', 'm_w_mlp1', 'm_w_mlp2', 'v_c_ctx', 'v_w_ada', 'v_b_ada', 'v_norm1', 'v_norm2', 'v_w_in', 'v_q_norm_a', 'v_k_norm_a', 'v_q_norm_c', 'v_k_norm_c', 'v_sink_c', 'v_w_pool', 'v_pool_scale', 'v_w_br_a', 'v_w_br_b', 'v_w_br_c', 'v_w_out', 'v_w_mlp1', 'v_w_mlp2']
TWIN_OUTPUTS = ['loss', 'grad_x', 'grad_c_ctx', 'grad_w_ada', 'grad_b_ada', 'grad_norm1', 'grad_norm2', 'grad_w_in', 'grad_q_norm_a', 'grad_k_norm_a', 'grad_q_norm_c', 'grad_k_norm_c', 'grad_sink_c', 'grad_w_pool', 'grad_pool_scale', 'grad_w_br_a', 'grad_w_br_b', 'grad_w_br_c', 'grad_w_out', 'grad_w_mlp1', 'grad_w_mlp2', 'delta_c_ctx', 'delta_w_ada', 'delta_b_ada', 'delta_norm1', 'delta_norm2', 'delta_w_in', 'delta_q_norm_a', 'delta_k_norm_a', 'delta_q_norm_c', 'delta_k_norm_c', 'delta_sink_c', 'delta_w_pool', 'delta_pool_scale', 'delta_w_br_a', 'delta_w_br_b', 'delta_w_br_c', 'delta_w_out', 'delta_w_mlp1', 'delta_w_mlp2', 'new_m_c_ctx', 'new_m_w_ada', 'new_m_b_ada', 'new_m_norm1', 'new_m_norm2', 'new_m_w_in', 'new_m_q_norm_a', 'new_m_k_norm_a', 'new_m_q_norm_c', 'new_m_k_norm_c', 'new_m_sink_c', 'new_m_w_pool', 'new_m_pool_scale', 'new_m_w_br_a', 'new_m_w_br_b', 'new_m_w_br_c', 'new_m_w_out', 'new_m_w_mlp1', 'new_m_w_mlp2', 'new_v_c_ctx', 'new_v_w_ada', 'new_v_b_ada', 'new_v_norm1', 'new_v_norm2', 'new_v_w_in', 'new_v_q_norm_a', 'new_v_k_norm_a', 'new_v_q_norm_c', 'new_v_k_norm_c', 'new_v_sink_c', 'new_v_w_pool', 'new_v_pool_scale', 'new_v_w_br_a', 'new_v_w_br_b', 'new_v_w_br_c', 'new_v_w_out', 'new_v_w_mlp1', 'new_v_w_mlp2']
TWIN_LEAF_KINDS = {'loss': 'loss', 'grad_x': 'grad_x', 'grad_c_ctx': 'grad_w', 'grad_w_ada': 'grad_w', 'grad_b_ada': 'grad_w', 'grad_norm1': 'grad_w', 'grad_norm2': 'grad_w', 'grad_w_in': 'grad_w', 'grad_q_norm_a': 'grad_w', 'grad_k_norm_a': 'grad_w', 'grad_q_norm_c': 'grad_w', 'grad_k_norm_c': 'grad_w', 'grad_sink_c': 'grad_w', 'grad_w_pool': 'grad_w', 'grad_pool_scale': 'grad_w', 'grad_w_br_a': 'grad_w', 'grad_w_br_b': 'grad_w', 'grad_w_br_c': 'grad_w', 'grad_w_out': 'grad_w', 'grad_w_mlp1': 'grad_w', 'grad_w_mlp2': 'grad_w', 'delta_c_ctx': 'delta_w', 'delta_w_ada': 'delta_w', 'delta_b_ada': 'delta_w', 'delta_norm1': 'delta_w', 'delta_norm2': 'delta_w', 'delta_w_in': 'delta_w', 'delta_q_norm_a': 'delta_w', 'delta_k_norm_a': 'delta_w', 'delta_q_norm_c': 'delta_w', 'delta_k_norm_c': 'delta_w', 'delta_sink_c': 'delta_w', 'delta_w_pool': 'delta_w', 'delta_pool_scale': 'delta_w', 'delta_w_br_a': 'delta_w', 'delta_w_br_b': 'delta_w', 'delta_w_br_c': 'delta_w', 'delta_w_out': 'delta_w', 'delta_w_mlp1': 'delta_w', 'delta_w_mlp2': 'delta_w', 'new_m_c_ctx': 'new_m', 'new_m_w_ada': 'new_m', 'new_m_b_ada': 'new_m', 'new_m_norm1': 'new_m', 'new_m_norm2': 'new_m', 'new_m_w_in': 'new_m', 'new_m_q_norm_a': 'new_m', 'new_m_k_norm_a': 'new_m', 'new_m_q_norm_c': 'new_m', 'new_m_k_norm_c': 'new_m', 'new_m_sink_c': 'new_m', 'new_m_w_pool': 'new_m', 'new_m_pool_scale': 'new_m', 'new_m_w_br_a': 'new_m', 'new_m_w_br_b': 'new_m', 'new_m_w_br_c': 'new_m', 'new_m_w_out': 'new_m', 'new_m_w_mlp1': 'new_m', 'new_m_w_mlp2': 'new_m', 'new_v_c_ctx': 'new_v', 'new_v_w_ada': 'new_v', 'new_v_b_ada': 'new_v', 'new_v_norm1': 'new_v', 'new_v_norm2': 'new_v', 'new_v_w_in': 'new_v', 'new_v_q_norm_a': 'new_v', 'new_v_k_norm_a': 'new_v', 'new_v_q_norm_c': 'new_v', 'new_v_k_norm_c': 'new_v', 'new_v_sink_c': 'new_v', 'new_v_w_pool': 'new_v', 'new_v_pool_scale': 'new_v', 'new_v_w_br_a': 'new_v', 'new_v_w_br_b': 'new_v', 'new_v_w_br_c': 'new_v', 'new_v_w_out': 'new_v', 'new_v_w_mlp1': 'new_v', 'new_v_w_mlp2': 'new_v'}


def _forward(args):
    return _fwd_reference(*[args[k] for k in FWD_PARAMS])


def _output_shape():
    out = _jax.eval_shape(lambda: _forward(_fwd_setup_inputs(0)))
    return out.shape, out.dtype

N_MICROBATCH = 1
ADAM_LR = 0.001
ADAM_B1 = 0.9
ADAM_B2 = 0.999
ADAM_EPS = 1e-08
ADAM_WD = 0.01
ADAM_STEP = 10
PER_EXAMPLE_BATCH_AXIS = {'x': 0, 'c': 0, 'ctx': 0, 'loss_target': 0}
SHARED_INPUTS = []
_WEIGHT_DTYPES = {'c_ctx': _jnp.float32, 'w_ada': _jnp.float32, 'b_ada': _jnp.float32, 'norm1': _jnp.float32, 'norm2': _jnp.float32, 'w_in': _jnp.float32, 'q_norm_a': _jnp.float32, 'k_norm_a': _jnp.float32, 'q_norm_c': _jnp.float32, 'k_norm_c': _jnp.float32, 'sink_c': _jnp.float32, 'w_pool': _jnp.float32, 'pool_scale': _jnp.float32, 'w_br_a': _jnp.float32, 'w_br_b': _jnp.float32, 'w_br_c': _jnp.float32, 'w_out': _jnp.float32, 'w_mlp1': _jnp.float32, 'w_mlp2': _jnp.float32}
MOMENT_SCALE = {'c_ctx': 3.603215e-01, 'w_ada': 7.371397e+00, 'b_ada': 1.556553e+01, 'norm1': 1.790324e+00, 'norm2': 2.641115e+01, 'w_in': 3.607746e-01, 'q_norm_a': 4.926178e-02, 'k_norm_a': 5.010779e-02, 'q_norm_c': 2.091288e-01, 'k_norm_c': 2.071992e-01, 'sink_c': 3.538918e-02, 'w_pool': 6.107401e-01, 'pool_scale': 6.063314e+00, 'w_br_a': 5.699407e-01, 'w_br_b': 1.640220e-01, 'w_br_c': 5.129322e-01, 'w_out': 7.175394e-01, 'w_mlp1': 8.739332e-01, 'w_mlp2': 3.694587e+00}


def _to_microbatches(a, axis):
    t = _jnp.moveaxis(a, axis, 0)
    t = t.reshape((N_MICROBATCH, t.shape[0] // N_MICROBATCH) + t.shape[1:])
    return _jnp.moveaxis(t, 1, axis + 1)


def setup_inputs(seed: int = 0) -> dict:
    inp = _fwd_setup_inputs(seed)
    key = _jax.random.fold_in(_jax.random.key(seed), 7919)
    shape, _ = _output_shape()
    out = dict(inp)
    out["loss_target"] = _jax.random.normal(_jax.random.fold_in(key, 0), shape, _jnp.float32)
    for i, name in enumerate(TWIN_WEIGHTS):
        w = inp[name].astype(_jnp.float32)
        if MOMENT_SCALE is None:
            s = _jnp.sqrt(_jnp.mean(_jnp.square(w)) + 1e-30)
        else:
            s = MOMENT_SCALE[name]
        km, kv = _jax.random.split(_jax.random.fold_in(key, i + 1))
        out[name] = w
        out["m_" + name] = s * _jax.random.normal(km, w.shape, _jnp.float32)
        out["v_" + name] = (s * s) * _jax.random.uniform(kv, w.shape, _jnp.float32, 0.5, 1.5)
    if N_MICROBATCH > 1:
        for name, axis in PER_EXAMPLE_BATCH_AXIS.items():
            out[name] = _to_microbatches(out[name], axis)
    return {'x': out['x'], 'c': out['c'], 'ctx': out['ctx'], 'c_ctx': out['c_ctx'], 'w_ada': out['w_ada'], 'b_ada': out['b_ada'], 'norm1': out['norm1'], 'norm2': out['norm2'], 'w_in': out['w_in'], 'q_norm_a': out['q_norm_a'], 'k_norm_a': out['k_norm_a'], 'q_norm_c': out['q_norm_c'], 'k_norm_c': out['k_norm_c'], 'sink_c': out['sink_c'], 'w_pool': out['w_pool'], 'pool_scale': out['pool_scale'], 'w_br_a': out['w_br_a'], 'w_br_b': out['w_br_b'], 'w_br_c': out['w_br_c'], 'w_out': out['w_out'], 'w_mlp1': out['w_mlp1'], 'w_mlp2': out['w_mlp2'], 'loss_target': out['loss_target'], 'm_c_ctx': out['m_c_ctx'], 'm_w_ada': out['m_w_ada'], 'm_b_ada': out['m_b_ada'], 'm_norm1': out['m_norm1'], 'm_norm2': out['m_norm2'], 'm_w_in': out['m_w_in'], 'm_q_norm_a': out['m_q_norm_a'], 'm_k_norm_a': out['m_k_norm_a'], 'm_q_norm_c': out['m_q_norm_c'], 'm_k_norm_c': out['m_k_norm_c'], 'm_sink_c': out['m_sink_c'], 'm_w_pool': out['m_w_pool'], 'm_pool_scale': out['m_pool_scale'], 'm_w_br_a': out['m_w_br_a'], 'm_w_br_b': out['m_w_br_b'], 'm_w_br_c': out['m_w_br_c'], 'm_w_out': out['m_w_out'], 'm_w_mlp1': out['m_w_mlp1'], 'm_w_mlp2': out['m_w_mlp2'], 'v_c_ctx': out['v_c_ctx'], 'v_w_ada': out['v_w_ada'], 'v_b_ada': out['v_b_ada'], 'v_norm1': out['v_norm1'], 'v_norm2': out['v_norm2'], 'v_w_in': out['v_w_in'], 'v_q_norm_a': out['v_q_norm_a'], 'v_k_norm_a': out['v_k_norm_a'], 'v_q_norm_c': out['v_q_norm_c'], 'v_k_norm_c': out['v_k_norm_c'], 'v_sink_c': out['v_sink_c'], 'v_w_pool': out['v_w_pool'], 'v_pool_scale': out['v_pool_scale'], 'v_w_br_a': out['v_w_br_a'], 'v_w_br_b': out['v_w_br_b'], 'v_w_br_c': out['v_w_br_c'], 'v_w_out': out['v_w_out'], 'v_w_mlp1': out['v_w_mlp1'], 'v_w_mlp2': out['v_w_mlp2']}


def _loss(weights, diff, rest, loss_target):
    with _jax.named_scope("forward"):
        args = {**rest, TWIN_DIFF_INPUT: diff, **{k: w.astype(_WEIGHT_DTYPES[k]) for k, w in weights.items()}}
        y = _forward(args)
    with _jax.named_scope("loss_head"):
        err = _jnp.square(y.astype(_jnp.float32) - loss_target)
        return 0.5 * _jnp.sum(_jnp.mean(err, axis=-1)) if err.ndim else 0.5 * err


def _adamw(w, g, m, v):
    m = ADAM_B1 * m + (1.0 - ADAM_B1) * g
    v = ADAM_B2 * v + (1.0 - ADAM_B2) * _jnp.square(g)
    m_hat = m / (1.0 - ADAM_B1 ** ADAM_STEP)
    v_hat = v / (1.0 - ADAM_B2 ** ADAM_STEP)
    delta = -ADAM_LR * (m_hat / (_jnp.sqrt(v_hat) + ADAM_EPS) + ADAM_WD * w)
    return delta, m, v


def reference(x, c, ctx, c_ctx, w_ada, b_ada, norm1, norm2, w_in, q_norm_a, k_norm_a, q_norm_c, k_norm_c, sink_c, w_pool, pool_scale, w_br_a, w_br_b, w_br_c, w_out, w_mlp1, w_mlp2, loss_target, m_c_ctx, m_w_ada, m_b_ada, m_norm1, m_norm2, m_w_in, m_q_norm_a, m_k_norm_a, m_q_norm_c, m_k_norm_c, m_sink_c, m_w_pool, m_pool_scale, m_w_br_a, m_w_br_b, m_w_br_c, m_w_out, m_w_mlp1, m_w_mlp2, v_c_ctx, v_w_ada, v_b_ada, v_norm1, v_norm2, v_w_in, v_q_norm_a, v_k_norm_a, v_q_norm_c, v_k_norm_c, v_sink_c, v_w_pool, v_pool_scale, v_w_br_a, v_w_br_b, v_w_br_c, v_w_out, v_w_mlp1, v_w_mlp2):
    given = dict(x=x, c=c, ctx=ctx, c_ctx=c_ctx, w_ada=w_ada, b_ada=b_ada, norm1=norm1, norm2=norm2, w_in=w_in, q_norm_a=q_norm_a, k_norm_a=k_norm_a, q_norm_c=q_norm_c, k_norm_c=k_norm_c, sink_c=sink_c, w_pool=w_pool, pool_scale=pool_scale, w_br_a=w_br_a, w_br_b=w_br_b, w_br_c=w_br_c, w_out=w_out, w_mlp1=w_mlp1, w_mlp2=w_mlp2, loss_target=loss_target, m_c_ctx=m_c_ctx, m_w_ada=m_w_ada, m_b_ada=m_b_ada, m_norm1=m_norm1, m_norm2=m_norm2, m_w_in=m_w_in, m_q_norm_a=m_q_norm_a, m_k_norm_a=m_k_norm_a, m_q_norm_c=m_q_norm_c, m_k_norm_c=m_k_norm_c, m_sink_c=m_sink_c, m_w_pool=m_w_pool, m_pool_scale=m_pool_scale, m_w_br_a=m_w_br_a, m_w_br_b=m_w_br_b, m_w_br_c=m_w_br_c, m_w_out=m_w_out, m_w_mlp1=m_w_mlp1, m_w_mlp2=m_w_mlp2, v_c_ctx=v_c_ctx, v_w_ada=v_w_ada, v_b_ada=v_b_ada, v_norm1=v_norm1, v_norm2=v_norm2, v_w_in=v_w_in, v_q_norm_a=v_q_norm_a, v_k_norm_a=v_k_norm_a, v_q_norm_c=v_q_norm_c, v_k_norm_c=v_k_norm_c, v_sink_c=v_sink_c, v_w_pool=v_w_pool, v_pool_scale=v_pool_scale, v_w_br_a=v_w_br_a, v_w_br_b=v_w_br_b, v_w_br_c=v_w_br_c, v_w_out=v_w_out, v_w_mlp1=v_w_mlp1, v_w_mlp2=v_w_mlp2)
    weights = {n: given[n] for n in TWIN_WEIGHTS}
    shared = {n: given[n] for n in SHARED_INPUTS}
    per_example = {n: given[n] for n in ['x', 'c', 'ctx']}
    grad_fn = _jax.value_and_grad(_loss, argnums=(0, 1))

    def one_microbatch(ex, loss_target):
        ex = dict(ex)
        diff = ex.pop(TWIN_DIFF_INPUT)
        return grad_fn(weights, diff, {**shared, **ex}, loss_target)

    if N_MICROBATCH == 1:
        loss, (grad_w, grad_x) = one_microbatch(per_example, given["loss_target"])
    else:
        def body(carry, xs):
            loss_sum, grad_sum = carry
            l_k, (gw_k, gx_k) = one_microbatch(xs[0], xs[1])
            with _jax.named_scope("update"):
                return (loss_sum + l_k, _jax.tree.map(_jnp.add, grad_sum, gw_k)), gx_k

        init = (_jnp.zeros((), _jnp.float32), _jax.tree.map(_jnp.zeros_like, weights))
        (loss, grad_w), grad_x = _jax.lax.scan(body, init, (per_example, given["loss_target"]))
    with _jax.named_scope("update"):
        delta_w, new_m, new_v = {}, {}, {}
        for n in TWIN_WEIGHTS:
            delta_w[n], new_m[n], new_v[n] = _adamw(weights[n], grad_w[n], given["m_" + n], given["v_" + n])
    return (loss, grad_x, *[grad_w[n] for n in TWIN_WEIGHTS], *[delta_w[n] for n in TWIN_WEIGHTS],
            *[new_m[n] for n in TWIN_WEIGHTS], *[new_v[n] for n in TWIN_WEIGHTS])
```

```python
import functools

import jax
import jax.numpy as jnp
from jax import lax
from jax.experimental import pallas as pl
from jax.experimental.pallas import tpu as pltpu

F32 = jnp.float32
BF16 = jnp.bfloat16
HIGHEST = lax.Precision.HIGHEST

N_DEV = 8
HEAD = 64
N_QHEADS = 6
N_KV = 2
GROUP = 3
QW = N_QHEADS * HEAD
KVW = N_KV * HEAD
QKV_W = 2 * (QW + 2 * KVW)
POOL_W = 256
POOL_WINDOWS = (2, 4, 8, 16)
GATE0 = QKV_W + POOL_W
WINDOW = 128
GRID_W = 64
ROPE_THETA = 10000.0
EPS = 1e-6
NEG = -1e30
QSCALE = HEAD ** -0.5
LANES = 128
PACK_W = 1024
VMEM_LIMIT = 56 * 1024 * 1024

ADAM_LR = 0.001
ADAM_B1 = 0.9
ADAM_B2 = 0.999
ADAM_EPS = 1e-08
ADAM_WD = 0.01
ADAM_STEP = 10

NT_DIMS = (((1,), (1,)), ((), ()))
TN_DIMS = (((0,), (0,)), ((), ()))


def _dot(a, b):
    return jnp.dot(a, b, preferred_element_type=F32)


def _dot_nt(a, b):
    return lax.dot_general(a, b, NT_DIMS, preferred_element_type=F32)


def _dot_tn(a, b):
    return lax.dot_general(a, b, TN_DIMS, preferred_element_type=F32)


def _params(n_grid, vmem=None):
    return pltpu.CompilerParams(dimension_semantics=("arbitrary",) * n_grid, vmem_limit_bytes=vmem or VMEM_LIMIT)


def _full(shape):
    nd = len(shape)
    return pl.BlockSpec(shape, lambda *_: (0,) * nd)


def _modulate(x, gn, shift, scale):
    rstd = lax.rsqrt(jnp.mean(x * x, axis=-1, keepdims=True) + EPS)
    xhat = x * rstd
    return xhat * gn * (1.0 + scale) + shift, xhat, rstd


def _modulate_bwd(dh, xhat, rstd, gn, scale):
    d_shift = jnp.sum(dh, axis=0, keepdims=True)
    d_scale = jnp.sum(dh * xhat * gn, axis=0, keepdims=True)
    dy = dh * (1.0 + scale)
    d_gn = jnp.sum(dy * xhat, axis=0, keepdims=True)
    dxh = dy * gn
    dx = rstd * (dxh - xhat * jnp.mean(dxh * xhat, axis=-1, keepdims=True))
    return dx, d_shift, d_scale, d_gn


def _mod_row(mod_ref, row, k, d):
    return mod_ref[pl.ds(row, 1), k * d:(k + 1) * d]


class _Cfg:
    def __init__(self, b, s, n, d):
        self.B, self.S, self.N, self.D = b, s, n, d
        self.T = n + s
        self.F = 4 * d
        self.IN = GATE0 + 3 * d
        self.tm = 256 if (n % 256 == 0 and s % 256 == 0) else 128
        self.nT = self.T // self.tm
        self.nC = n // self.tm
        self.gw = 512 if d % 512 == 0 else 256
        assert GATE0 % self.gw == 0 and d % self.gw == 0 and b < 8 and self.T >= 3 * self.tm
        assert s % GRID_W == 0 and n % self.tm == 0 and s % self.tm == 0 and s >= self.tm + 2 * WINDOW


def _adaln_fwd(cfg, cc8, w_ada, b_ada):
    d = cfg.D
    tn = 6 * d // 4

    def body(c_ref, w_ref, b_ref, o_ref):
        c = c_ref[...]
        a = (c * jax.nn.sigmoid(c)).astype(BF16)
        o_ref[...] = _dot(a, w_ref[...]) + b_ref[...]

    return pl.pallas_call(
        body, name="adaln_fwd", grid=(4,),
        in_specs=[_full((8, d)), pl.BlockSpec((d, tn), lambda j: (0, j)), pl.BlockSpec((1, tn), lambda j: (0, j))],
        out_specs=pl.BlockSpec((8, tn), lambda j: (0, j)),
        out_shape=jax.ShapeDtypeStruct((8, 6 * d), F32), compiler_params=_params(1),
    )(cc8, w_ada, b_ada)


def _in_proj_fwd(cfg, x, gn, mod, w_in):
    B, T, D, IN, tm, nC = cfg.B, cfg.T, cfg.D, cfg.IN, cfg.tm, cfg.nC

    def body(x_ref, gn_ref, mod_ref, w_ref, z_ref, h_ref):
        b, t = pl.program_id(0), pl.program_id(1)
        row = jnp.where(t < nC, B, b)
        h, _, _ = _modulate(x_ref[0], gn_ref[...], _mod_row(mod_ref, row, 0, D), _mod_row(mod_ref, row, 1, D))
        hb = h.astype(BF16)
        h_ref[0] = hb
        z_ref[0] = _dot(hb, w_ref[...])

    return pl.pallas_call(
        body, name="in_proj_fwd", grid=(B, cfg.nT),
        in_specs=[pl.BlockSpec((1, tm, D), lambda b, t: (b, t, 0)), _full((1, D)), _full((8, 6 * D)), _full((D, IN))],
        out_specs=[pl.BlockSpec((1, tm, IN), lambda b, t: (b, t, 0)), pl.BlockSpec((1, tm, D), lambda b, t: (b, t, 0))],
        out_shape=[jax.ShapeDtypeStruct((B, T, IN), F32), jax.ShapeDtypeStruct((B, T, D), BF16)],
        compiler_params=_params(2),
    )(x, gn, mod, w_in)


def _head_indicator():
    r = lax.broadcasted_iota(jnp.int32, (LANES, LANES), 0) // HEAD
    c = lax.broadcasted_iota(jnp.int32, (LANES, LANES), 1) // HEAD
    return jnp.where(r == c, 1.0, 0.0).astype(F32)


def _pair_swap(y):
    lane = lax.broadcasted_iota(jnp.int32, y.shape, 1)
    return jnp.where(lane % 2 == 0, pltpu.roll(y, LANES - 1, 1), pltpu.roll(y, 1, 1))


_QK_CHUNKS = (0, 1, 2, 3, 5, 6, 7, 8)
_Q_CHUNKS = (0, 1, 2, 5, 6, 7)


def _qknorm_fwd(cfg, z, gvec, cosf, sins):
    B, T, tm = cfg.B, cfg.T, cfg.tm

    def body(z_ref, g_ref, cos_ref, sin_ref, qa_ref, ka_ref, qc_ref, kc_ref):
        ind = _head_indicator()
        cos, sin = cos_ref[...], sin_ref[...]

        def chunk(c):
            x = z_ref[0, :, c * LANES:(c + 1) * LANES]
            ss = jnp.dot(x * x, ind, precision=HIGHEST, preferred_element_type=F32)
            y = x * lax.rsqrt(ss * (1.0 / HEAD) + EPS) * g_ref[:, c * LANES:(c + 1) * LANES]
            out = y * cos + _pair_swap(y) * sin
            return (out * QSCALE if c in _Q_CHUNKS else out).astype(BF16)

        qa_ref[0] = jnp.concatenate([chunk(0), chunk(1), chunk(2)], axis=-1)
        ka_ref[0] = chunk(3)
        qc_ref[0] = jnp.concatenate([chunk(5), chunk(6), chunk(7)], axis=-1)
        kc_ref[0] = chunk(8)

    row = lambda w: pl.BlockSpec((1, tm, w), lambda b, t: (b, t, 0))
    tab = pl.BlockSpec((tm, LANES), lambda b, t: (t, 0))
    return pl.pallas_call(
        body, name="qknorm_fwd", grid=(B, cfg.nT),
        in_specs=[row(QKV_W), _full((1, QKV_W)), tab, tab],
        out_specs=[row(QW), row(KVW), row(QW), row(KVW)],
        out_shape=[jax.ShapeDtypeStruct((B, T, w), BF16) for w in (QW, KVW, QW, KVW)],
        compiler_params=_params(2),
    )(z, gvec, cosf, sins)


def _attn_scores(cfg, t, q, k_ref, v_ref, sink_ref, h, loc, window, sink):
    S, N, tq = cfg.S, cfg.N, cfg.tm
    hs = slice(h * HEAD, (h + 1) * HEAD)
    qs = jnp.concatenate([q[:, (GROUP * h + g) * HEAD:(GROUP * h + g + 1) * HEAD] for g in range(GROUP)], axis=0)
    k_ctx = k_ref[0, 0:N, :][:, hs]
    v_ctx = v_ref[0, 0:N, :].astype(BF16)[:, hs]
    s_c = _dot_nt(qs, k_ctx)
    m = jnp.max(s_c, axis=-1, keepdims=True)
    k_loc = v_loc = s_l = lo = None
    if loc:
        if window:
            W = tq + 2 * WINDOW
            lo = pl.multiple_of(jnp.clip(t * tq - WINDOW, 0, S - W), LANES)
        else:
            W, lo = S, 0
        k_loc = k_ref[0, pl.ds(N + lo, W), :][:, hs]
        v_loc = v_ref[0, pl.ds(N + lo, W), :].astype(BF16)[:, hs]
        s_l = _dot_nt(qs, k_loc)
        if window:
            qpos = t * tq + lax.broadcasted_iota(jnp.int32, (GROUP * tq, W), 0) % tq
            kpos = lo + lax.broadcasted_iota(jnp.int32, (GROUP * tq, W), 1)
            s_l = jnp.where(jnp.abs(qpos - kpos) <= WINDOW, s_l, NEG)
        m = jnp.maximum(m, jnp.max(s_l, axis=-1, keepdims=True))
    if sink:
        rowg = lax.broadcasted_iota(jnp.int32, (GROUP * tq, 1), 0) // tq
        sk = jnp.zeros((GROUP * tq, 1), F32)
        for g in range(GROUP):
            j = GROUP * h + g
            sk = jnp.where(rowg == g, sink_ref[0:1, j:j + 1], sk)
        m = jnp.maximum(m, sk)
    e_c = jnp.exp(s_c - m)
    l = jnp.sum(e_c, axis=-1, keepdims=True)
    e_l = e_s = None
    if loc:
        e_l = jnp.exp(s_l - m)
        l = l + jnp.sum(e_l, axis=-1, keepdims=True)
    if sink:
        e_s = jnp.exp(sk - m)
        l = l + e_s
    inv = 1.0 / l
    p_c = e_c * inv
    p_l = e_l * inv if loc else None
    p_s = e_s * inv if sink else None
    return qs, p_c, p_l, p_s, k_ctx, v_ctx, k_loc, v_loc, lo


def _attn_fwd(cfg, q, k, z, vblock, sink8, *, loc, window, sink, name):
    B, S, N, T, tq, nC = cfg.B, cfg.S, cfg.N, cfg.T, cfg.tm, cfg.nC
    nk = T if loc else N
    nq = S if loc else N
    qoff = nC if loc else 0

    def body(q_ref, k_ref, v_ref, sink_ref, o_ref):
        t = pl.program_id(1)
        q_t = q_ref[0]
        outs = [None] * N_QHEADS
        for h in range(N_KV):
            _, p_c, p_l, _, _, v_ctx, _, v_loc, _ = _attn_scores(cfg, t, q_t, k_ref, v_ref, sink_ref, h, loc, window, sink)
            o = _dot(p_c.astype(BF16), v_ctx)
            if loc:
                o = o + _dot(p_l.astype(BF16), v_loc)
            for g in range(GROUP):
                outs[GROUP * h + g] = o[g * tq:(g + 1) * tq]
        o_ref[0] = jnp.concatenate(outs, axis=-1).astype(BF16)

    return pl.pallas_call(
        body, name=name, grid=(B, nq // tq),
        in_specs=[pl.BlockSpec((1, tq, QW), lambda b, t: (b, t + qoff, 0)),
                  pl.BlockSpec((1, nk, KVW), lambda b, t: (b, 0, 0)),
                  pl.BlockSpec((1, nk, KVW), lambda b, t: (b, 0, vblock)),
                  _full((8, LANES))],
        out_specs=pl.BlockSpec((1, tq, QW), lambda b, t: (b, t, 0)),
        out_shape=jax.ShapeDtypeStruct((B, nq, QW), BF16), compiler_params=_params(2),
    )(q, k, z, sink8)


def _pool_geometry(cfg, t):
    tm, N, T, nC = cfg.tm, cfg.N, cfg.T, cfg.nC
    r0 = pl.multiple_of(t * tm, tm)
    isctx = t < nC
    seg_lo = jnp.where(isctx, 0, N)
    seg_hi = jnp.where(isctx, N, T)
    k0 = pl.multiple_of(jnp.clip(t * tm - tm, 0, T - 3 * tm), tm)
    return r0, seg_lo, seg_hi, k0


def _pool_count(pos, h, seg_lo, seg_hi):
    return jnp.maximum(jnp.minimum(pos + h, seg_hi) - jnp.maximum(pos - h, seg_lo), 1).astype(F32)


def _split_bf16(x):
    hi = x.astype(BF16)
    return hi, (x - hi.astype(F32)).astype(BF16)


def _pool_fwd(cfg, z, wp, ps):
    B, T, tm = cfg.B, cfg.T, cfg.tm

    def body(u_ref, wp_ref, ps_ref, ob_ref, pooled_ref):
        t = pl.program_id(1)
        r0, seg_lo, seg_hi, k0 = _pool_geometry(cfg, t)
        hi, lo = _split_bf16(u_ref[0, pl.ds(k0, 3 * tm), :])
        rr = r0 + lax.broadcasted_iota(jnp.int32, (tm, 3 * tm), 0)
        cc = k0 + lax.broadcasted_iota(jnp.int32, (tm, 3 * tm), 1)
        diff = cc - rr
        inseg = (cc >= seg_lo) & (cc < seg_hi)
        rcol = r0 + lax.broadcasted_iota(jnp.int32, (tm, 1), 0)
        group = lax.broadcasted_iota(jnp.int32, (tm, POOL_W), 1) // HEAD
        acc = jnp.zeros((tm, POOL_W), F32)
        for gi, w in enumerate(POOL_WINDOWS):
            h = w // 2
            band = jnp.where((diff >= -h) & (diff <= h - 1) & inseg, 1.0, 0.0).astype(BF16)
            tot = _dot(band, hi) + _dot(band, lo)
            acc = jnp.where(group == gi, tot / _pool_count(rcol, h, seg_lo, seg_hi), acc)
        pooled = (acc - u_ref[0, pl.ds(r0, tm), :]).astype(BF16)
        pooled_ref[0] = pooled
        ob_ref[0] = (_dot(pooled, wp_ref[...]) * ps_ref[...]).astype(BF16)

    row = pl.BlockSpec((1, tm, POOL_W), lambda b, t: (b, t, 0))
    return pl.pallas_call(
        body, name="pool_fwd", grid=(B, cfg.nT),
        in_specs=[pl.BlockSpec((1, T, POOL_W), lambda b, t: (b, 0, QKV_W // POOL_W)), _full((POOL_W, POOL_W)), _full((1, POOL_W))],
        out_specs=[row, row],
        out_shape=[jax.ShapeDtypeStruct((B, T, POOL_W), BF16)] * 2, compiler_params=_params(2),
    )(z, wp, ps)


def _gate_specs(cfg):
    tm, gw = cfg.tm, cfg.gw
    first = GATE0 // gw
    return [pl.BlockSpec((1, tm, gw), functools.partial(lambda b, t, j: (b, t, j), j=first + i)) for i in range(3 * cfg.D // gw)]


def _read_gates(cfg, gate_refs):
    per = cfg.D // cfg.gw
    return [jnp.concatenate([gate_refs[k * per + i][0] for i in range(per)], axis=-1) for k in range(3)]


def _merge_fwd(cfg, x, oa, ob, oc, z, mod, wa, wb, wc, wo, *, ctx_active):
    B, T, D, tm, nC = cfg.B, cfg.T, cfg.D, cfg.tm, cfg.nC
    ng = 3 * D // cfg.gw

    def body(x_ref, oa_ref, ob_ref, oc_ref, *rest):
        gate_refs = rest[:ng]
        mod_ref, wa_ref, wb_ref, wc_ref, wo_ref, x1_ref, mgo_ref = rest[ng:]
        b, t = pl.program_id(0), pl.program_id(1)

        def compute():
            row = jnp.where(t < nC, B, b)
            ga, gb, gc = _read_gates(cfg, gate_refs)
            y = (jax.nn.sigmoid(ga) * _dot(oa_ref[0], wa_ref[...])
                 + jax.nn.sigmoid(gb) * _dot(ob_ref[0], wb_ref[...])
                 + jax.nn.sigmoid(gc) * _dot(oc_ref[0], wc_ref[...]))
            mo = _dot(y.astype(BF16), wo_ref[...])
            mgo_ref[0] = mo.astype(BF16)
            x1_ref[0] = x_ref[0] + _mod_row(mod_ref, row, 2, D) * mo

        if ctx_active:
            compute()
        else:
            pl.when(t >= nC)(compute)

            @pl.when(t < nC)
            def _():
                mgo_ref[0] = jnp.zeros((tm, D), BF16)
                x1_ref[0] = x_ref[0]

    row = lambda w: pl.BlockSpec((1, tm, w), lambda b, t: (b, t, 0))
    return pl.pallas_call(
        body, name="merge_fwd", grid=(B, cfg.nT),
        in_specs=[row(D), row(QW), row(POOL_W), row(QW)] + _gate_specs(cfg)
        + [_full((8, 6 * D)), _full((QW, D)), _full((POOL_W, D)), _full((QW, D)), _full((D, D))],
        out_specs=[row(D), row(D)],
        out_shape=[jax.ShapeDtypeStruct((B, T, D), F32), jax.ShapeDtypeStruct((B, T, D), BF16)],
        compiler_params=_params(2),
    )(x, oa, ob, oc, *([z] * ng), mod, wa, wb, wc, wo)


def _mlp_fwd(cfg, x1, gn, mod, w1, w2, *, ctx_active):
    B, T, D, F, tm, nC = cfg.B, cfg.T, cfg.D, cfg.F, cfg.tm, cfg.nC

    def body(x_ref, gn_ref, mod_ref, w1_ref, w2_ref, x2_ref, mo_ref):
        b, t = pl.program_id(0), pl.program_id(1)

        def compute():
            row = jnp.where(t < nC, B, b)
            x = x_ref[0]
            h, _, _ = _modulate(x, gn_ref[...], _mod_row(mod_ref, row, 3, D), _mod_row(mod_ref, row, 4, D))
            a = jnp.maximum(_dot(h.astype(BF16), w1_ref[...]), 0.0)
            mo = _dot((a * a).astype(BF16), w2_ref[...])
            mo_ref[0] = mo.astype(BF16)
            x2_ref[0] = x + _mod_row(mod_ref, row, 5, D) * mo

        if ctx_active:
            compute()
        else:
            pl.when(t >= nC)(compute)

            @pl.when(t < nC)
            def _():
                mo_ref[0] = jnp.zeros((tm, D), BF16)
                x2_ref[0] = x_ref[0]

    row = pl.BlockSpec((1, tm, D), lambda b, t: (b, t, 0))
    return pl.pallas_call(
        body, name="mlp_fwd", grid=(B, cfg.nT),
        in_specs=[row, _full((1, D)), _full((8, 6 * D)), _full((D, F)), _full((F, D))],
        out_specs=[row, row],
        out_shape=[jax.ShapeDtypeStruct((B, T, D), F32), jax.ShapeDtypeStruct((B, T, D), BF16)],
        compiler_params=_params(2),
    )(x1, gn, mod, w1, w2)


def _loss_fwd_bwd(cfg, x2, target):
    B, T, D, tm, nC = cfg.B, cfg.T, cfg.D, cfg.tm, cfg.nC

    def body(x_ref, tgt_ref, dx_ref, sse_ref):
        b, t = pl.program_id(0), pl.program_id(1)

        @pl.when((b == 0) & (t == 0))
        def _():
            sse_ref[...] = jnp.zeros((8, LANES), F32)

        @pl.when(t < nC)
        def _():
            dx_ref[0] = jnp.zeros((tm, D), F32)

        @pl.when(t >= nC)
        def _():
            err = x_ref[0] - tgt_ref[0]
            dx_ref[0] = err * (1.0 / D)
            sse_ref[...] += jnp.sum(err * err)

    return pl.pallas_call(
        body, name="loss", grid=(B, cfg.nT),
        in_specs=[pl.BlockSpec((1, tm, D), lambda b, t: (b, t, 0)),
                  pl.BlockSpec((1, tm, D), lambda b, t: (b, jnp.maximum(t - nC, 0), 0))],
        out_specs=[pl.BlockSpec((1, tm, D), lambda b, t: (b, t, 0)), _full((8, LANES))],
        out_shape=[jax.ShapeDtypeStruct((B, T, D), F32), jax.ShapeDtypeStruct((8, LANES), F32)],
        compiler_params=_params(2),
    )(x2, target)


def _acc_init(refs_shapes):
    b, t = pl.program_id(0), pl.program_id(1)

    @pl.when((b == 0) & (t == 0))
    def _():
        for ref in refs_shapes:
            ref[...] = jnp.zeros(ref.shape, ref.dtype)


def _mlp_bwd(cfg, x1, dx2, mo, gn, mod, w1, w2, *, ctx_active):
    B, T, D, F, tm, nC = cfg.B, cfg.T, cfg.D, cfg.F, cfg.tm, cfg.nC

    def body(x_ref, dx_ref, mo_ref, gn_ref, mod_ref, w1_ref, w2_ref, dx1_ref, h_ref, r_ref, da_ref, dout_ref, dmod_ref, dgn_ref):
        b, t = pl.program_id(0), pl.program_id(1)
        _acc_init([dmod_ref, dgn_ref])

        def compute():
            row = jnp.where(t < nC, B, b)
            gn = gn_ref[...]
            scale = _mod_row(mod_ref, row, 4, D)
            h, xhat, rstd = _modulate(x_ref[0], gn, _mod_row(mod_ref, row, 3, D), scale)
            hb = h.astype(BF16)
            a = jnp.maximum(_dot(hb, w1_ref[...]), 0.0)
            dx = dx_ref[0]
            dout = (dx * _mod_row(mod_ref, row, 5, D)).astype(BF16)
            da = (_dot_nt(dout, w2_ref[...]) * (2.0 * a)).astype(BF16)
            dh = _dot_nt(da, w1_ref[...])
            dxn, d_shift, d_scale, d_gn = _modulate_bwd(dh, xhat, rstd, gn, scale)
            dx1_ref[0] = dx + dxn
            h_ref[0] = hb
            r_ref[0] = (a * a).astype(BF16)
            da_ref[0] = da
            dout_ref[0] = dout
            d_gate = jnp.sum(dx * mo_ref[0].astype(F32), axis=0, keepdims=True)
            dmod_ref[pl.ds(row, 1), :] += jnp.concatenate([d_shift, d_scale, d_gate], axis=-1)
            dgn_ref[0:1, :] += d_gn

        if ctx_active:
            compute()
        else:
            pl.when(t >= nC)(compute)

            @pl.when(t < nC)
            def _():
                dx1_ref[0] = dx_ref[0]
                h_ref[0] = jnp.zeros((tm, D), BF16)
                r_ref[0] = jnp.zeros((tm, F), BF16)
                da_ref[0] = jnp.zeros((tm, F), BF16)
                dout_ref[0] = jnp.zeros((tm, D), BF16)

    row = lambda w: pl.BlockSpec((1, tm, w), lambda b, t: (b, t, 0))
    sds = lambda w, dt: jax.ShapeDtypeStruct((B, T, w), dt)
    return pl.pallas_call(
        body, name="mlp_bwd", grid=(B, cfg.nT),
        in_specs=[row(D), row(D), row(D), _full((1, D)), _full((8, 6 * D)), _full((D, F)), _full((F, D))],
        out_specs=[row(D), row(D), row(F), row(F), row(D), _full((8, 3 * D)), _full((8, D))],
        out_shape=[sds(D, F32), sds(D, BF16), sds(F, BF16), sds(F, BF16), sds(D, BF16),
                   jax.ShapeDtypeStruct((8, 3 * D), F32), jax.ShapeDtypeStruct((8, D), F32)],
        compiler_params=_params(2),
    )(x1, dx2, mo, gn, mod, w1, w2)


def _matmul_tn(a, g, name):
    R, Ka = a.shape
    Ng = g.shape[1]
    tr = next(c for c in (512, 256, 128, 64, 32, 16, 8) if R % c == 0)
    tka = Ka if Ka <= 1024 else 1024
    tn = next(c for c in (1152, 1024, 768, 512, 384, 256, 128) if Ng % c == 0) if Ng % LANES == 0 else Ng
    assert Ka % tka == 0
    nr = R // tr

    def body(a_ref, g_ref, o_ref):
        @pl.when(pl.program_id(2) == 0)
        def _():
            o_ref[...] = jnp.zeros(o_ref.shape, F32)
        o_ref[...] += _dot_tn(a_ref[...], g_ref[...])

    return pl.pallas_call(
        body, name=name, grid=(Ka // tka, Ng // tn, nr),
        in_specs=[pl.BlockSpec((tr, tka), lambda i, j, r: (r, i)), pl.BlockSpec((tr, tn), lambda i, j, r: (r, j))],
        out_specs=pl.BlockSpec((tka, tn), lambda i, j, r: (i, j)),
        out_shape=jax.ShapeDtypeStruct((Ka, Ng), F32), compiler_params=_params(3),
    )(a, g)


def _merge_bwd(cfg, dx1, mgo, oa, ob, oc, z, mod, wa, wb, wc, wo, *, ctx_active):
    B, T, D, tm, nC = cfg.B, cfg.T, cfg.D, cfg.tm, cfg.nC
    ng = 3 * D // cfg.gw

    def body(dx_ref, mgo_ref, oa_ref, ob_ref, oc_ref, *rest):
        gate_refs = rest[:ng]
        (mod_ref, wa_ref, wb_ref, wc_ref, wo_ref,
         doa_ref, dob_ref, doc_ref, dpa_ref, dpb_ref, dpc_ref, y_ref, dmo_ref, dzg_ref, dg1_ref) = rest[ng:]
        b, t = pl.program_id(0), pl.program_id(1)
        _acc_init([dg1_ref])

        def compute():
            row = jnp.where(t < nC, B, b)
            dx = dx_ref[0]
            dg1_ref[pl.ds(row, 1), :] += jnp.sum(dx * mgo_ref[0].astype(F32), axis=0, keepdims=True)
            dmo = (dx * _mod_row(mod_ref, row, 2, D)).astype(BF16)
            dmo_ref[0] = dmo
            dy = _dot_nt(dmo, wo_ref[...])
            gates = _read_gates(cfg, gate_refs)
            y = jnp.zeros((tm, D), F32)
            dgs = []
            for gate, o_ref, w_ref, do_ref, dp_ref in ((gates[0], oa_ref, wa_ref, doa_ref, dpa_ref),
                                                      (gates[1], ob_ref, wb_ref, dob_ref, dpb_ref),
                                                      (gates[2], oc_ref, wc_ref, doc_ref, dpc_ref)):
                s = jax.nn.sigmoid(gate)
                p = _dot(o_ref[0], w_ref[...])
                y = y + s * p
                dp = (dy * s).astype(BF16)
                dp_ref[0] = dp
                do_ref[0] = _dot_nt(dp, w_ref[...]).astype(BF16)
                dgs.append((dy * p * s * (1.0 - s)).astype(BF16))
            y_ref[0] = y.astype(BF16)
            dzg_ref[0] = jnp.concatenate(dgs, axis=-1)

        if ctx_active:
            compute()
        else:
            pl.when(t >= nC)(compute)

            @pl.when(t < nC)
            def _():
                for ref in (doa_ref, dob_ref, doc_ref, dpa_ref, dpb_ref, dpc_ref, y_ref, dmo_ref, dzg_ref):
                    ref[...] = jnp.zeros(ref.shape, ref.dtype)

    row = lambda w: pl.BlockSpec((1, tm, w), lambda b, t: (b, t, 0))
    sds = lambda w: jax.ShapeDtypeStruct((B, T, w), BF16)
    return pl.pallas_call(
        body, name="merge_bwd", grid=(B, cfg.nT),
        in_specs=[row(D), row(D), row(QW), row(POOL_W), row(QW)] + _gate_specs(cfg)
        + [_full((8, 6 * D)), _full((QW, D)), _full((POOL_W, D)), _full((QW, D)), _full((D, D))],
        out_specs=[row(QW), row(POOL_W), row(QW), row(D), row(D), row(D), row(D), row(D), row(3 * D), _full((8, D))],
        out_shape=[sds(QW), sds(POOL_W), sds(QW), sds(D), sds(D), sds(D), sds(D), sds(D), sds(3 * D),
                   jax.ShapeDtypeStruct((8, D), F32)],
        compiler_params=_params(2),
    )(dx1, mgo, oa, ob, oc, *([z] * ng), mod, wa, wb, wc, wo)


def _attn_bwd(cfg, q, k, z, vblock, sink8, do, *, loc, window, sink, name):
    B, S, N, T, tq, nC = cfg.B, cfg.S, cfg.N, cfg.T, cfg.tm, cfg.nC
    nk = T if loc else N
    nq = S if loc else N
    qoff = nC if loc else 0

    def body(q_ref, k_ref, v_ref, sink_ref, do_ref, dq_ref, dk_ref, dv_ref, dsink_ref):
        b, t = pl.program_id(0), pl.program_id(1)
        _acc_init([dsink_ref])

        @pl.when(t == 0)
        def _():
            dk_ref[...] = jnp.zeros(dk_ref.shape, F32)
            dv_ref[...] = jnp.zeros(dv_ref.shape, F32)

        q_t = q_ref[0]
        do_t = do_ref[0]
        dqs = [None] * N_QHEADS
        dk_c, dv_c, dk_l, dv_l = [], [], [], []
        dsink_row = jnp.zeros((1, LANES), F32)
        lane = lax.broadcasted_iota(jnp.int32, (1, LANES), 1)
        lo = 0
        for h in range(N_KV):
            qs, p_c, p_l, p_s, k_ctx, v_ctx, k_loc, v_loc, lo = _attn_scores(cfg, t, q_t, k_ref, v_ref, sink_ref, h, loc, window, sink)
            dos = jnp.concatenate([do_t[:, (GROUP * h + g) * HEAD:(GROUP * h + g + 1) * HEAD] for g in range(GROUP)], axis=0)
            dp_c = _dot_nt(dos, v_ctx)
            delta = jnp.sum(p_c * dp_c, axis=-1, keepdims=True)
            if loc:
                dp_l = _dot_nt(dos, v_loc)
                delta = delta + jnp.sum(p_l * dp_l, axis=-1, keepdims=True)
            ds_c = (p_c * (dp_c - delta)).astype(BF16)
            dq = _dot(ds_c, k_ctx)
            dk_c.append(_dot_tn(ds_c, qs))
            dv_c.append(_dot_tn(p_c.astype(BF16), dos))
            if loc:
                ds_l = (p_l * (dp_l - delta)).astype(BF16)
                dq = dq + _dot(ds_l, k_loc)
                dk_l.append(_dot_tn(ds_l, qs))
                dv_l.append(_dot_tn(p_l.astype(BF16), dos))
            if sink:
                dsk = -p_s * delta
                for g in range(GROUP):
                    tot = jnp.sum(dsk[g * tq:(g + 1) * tq], axis=0, keepdims=True)
                    dsink_row = dsink_row + jnp.where(lane == GROUP * h + g, tot, 0.0)
            for g in range(GROUP):
                dqs[GROUP * h + g] = dq[g * tq:(g + 1) * tq] * QSCALE
        dq_ref[0] = jnp.concatenate(dqs, axis=-1)
        dk_ref[0, 0:N, :] += jnp.concatenate(dk_c, axis=-1)
        dv_ref[0, 0:N, :] += jnp.concatenate(dv_c, axis=-1)
        if loc:
            W = tq + 2 * WINDOW if window else S
            dk_ref[0, pl.ds(N + lo, W), :] += jnp.concatenate(dk_l, axis=-1)
            dv_ref[0, pl.ds(N + lo, W), :] += jnp.concatenate(dv_l, axis=-1)
        if sink:
            dsink_ref[0:1, :] += dsink_row

    kv = pl.BlockSpec((1, nk, KVW), lambda b, t: (b, 0, 0))
    return pl.pallas_call(
        body, name=name, grid=(B, nq // tq),
        in_specs=[pl.BlockSpec((1, tq, QW), lambda b, t: (b, t + qoff, 0)), kv,
                  pl.BlockSpec((1, nk, KVW), lambda b, t: (b, 0, vblock)), _full((8, LANES)),
                  pl.BlockSpec((1, tq, QW), lambda b, t: (b, t + qoff, 0))],
        out_specs=[pl.BlockSpec((1, tq, QW), lambda b, t: (b, t, 0)), kv, kv, _full((8, LANES))],
        out_shape=[jax.ShapeDtypeStruct((B, nq, QW), F32), jax.ShapeDtypeStruct((B, nk, KVW), F32),
                   jax.ShapeDtypeStruct((B, nk, KVW), F32), jax.ShapeDtypeStruct((8, LANES), F32)],
        compiler_params=_params(2),
    )(q, k, z, sink8, do)


def _qknorm_bwd(cfg, z, gvec, cosf, sins, lat, ctx):
    B, S, N, T, tm, nC = cfg.B, cfg.S, cfg.N, cfg.T, cfg.tm, cfg.nC

    def body(z_ref, g_ref, cos_ref, sin_ref, lqa, lka, lva, lqc, lkc, lvc, cqa, cka, cva, cqc, ckc, cvc, dz_ref, dg_ref):
        t = pl.program_id(1)
        _acc_init([dg_ref])
        ind = _head_indicator()
        cos, sin = cos_ref[...], sin_ref[...]
        isctx = t < nC

        def q_grad(lat_ref, ctx_ref):
            return jnp.where(isctx, ctx_ref[0], lat_ref[0])

        def k_grad(lat_ref, ctx_ref):
            return lat_ref[0] + jnp.where(isctx, ctx_ref[0], 0.0)

        dqa, dqc = q_grad(lqa, cqa), q_grad(lqc, cqc)
        douts = {0: dqa[:, 0:128], 1: dqa[:, 128:256], 2: dqa[:, 256:384], 3: k_grad(lka, cka),
                 5: dqc[:, 0:128], 6: dqc[:, 128:256], 7: dqc[:, 256:384], 8: k_grad(lkc, ckc)}
        pieces = []
        dgs = []
        for c in range(QKV_W // LANES):
            if c not in douts:
                pieces.append(k_grad(lva, cva) if c == 4 else k_grad(lvc, cvc))
                dgs.append(jnp.zeros((1, LANES), F32))
                continue
            x = z_ref[0, :, c * LANES:(c + 1) * LANES]
            g = g_ref[:, c * LANES:(c + 1) * LANES]
            ss = jnp.dot(x * x, ind, precision=HIGHEST, preferred_element_type=F32)
            rstd = lax.rsqrt(ss * (1.0 / HEAD) + EPS)
            n = x * rstd
            dout = douts[c]
            dy = dout * cos + _pair_swap(dout * sin)
            dgs.append(jnp.sum(dy * n, axis=0, keepdims=True))
            dn = dy * g
            mean = jnp.dot(dn * n, ind, precision=HIGHEST, preferred_element_type=F32) * (1.0 / HEAD)
            pieces.append(rstd * (dn - n * mean))
        dz_ref[0] = jnp.concatenate(pieces, axis=-1).astype(BF16)
        dg_ref[0:1, :] += jnp.concatenate(dgs, axis=-1)

    row = lambda w: pl.BlockSpec((1, tm, w), lambda b, t: (b, t, 0))
    latq = pl.BlockSpec((1, tm, QW), lambda b, t: (b, jnp.maximum(t - nC, 0), 0))
    ctxq = pl.BlockSpec((1, tm, QW), lambda b, t: (b, jnp.minimum(t, nC - 1), 0))
    ctxk = pl.BlockSpec((1, tm, KVW), lambda b, t: (b, jnp.minimum(t, nC - 1), 0))
    tab = pl.BlockSpec((tm, LANES), lambda b, t: (t, 0))
    return pl.pallas_call(
        body, name="qknorm_bwd", grid=(B, cfg.nT),
        in_specs=[row(QKV_W), _full((1, QKV_W)), tab, tab,
                  latq, row(KVW), row(KVW), latq, row(KVW), row(KVW),
                  ctxq, ctxk, ctxk, ctxq, ctxk, ctxk],
        out_specs=[row(QKV_W), _full((8, QKV_W))],
        out_shape=[jax.ShapeDtypeStruct((B, T, QKV_W), BF16), jax.ShapeDtypeStruct((8, QKV_W), F32)],
        compiler_params=_params(2),
    )(z, gvec, cosf, sins, *lat, *ctx)


def _pool_bwd(cfg, dob, pooled, wp, ps):
    B, T, tm = cfg.B, cfg.T, cfg.tm

    def body(dob_ref, pooled_ref, wp_ref, ps_ref, du_ref, dwp_ref, dps_ref):
        t = pl.program_id(1)
        _acc_init([dwp_ref, dps_ref])
        r0, seg_lo, seg_hi, k0 = _pool_geometry(cfg, t)
        ps = ps_ref[...]
        wp = wp_ref[...]
        dmix = dob_ref[0, pl.ds(r0, tm), :].astype(F32)
        pooled = pooled_ref[0]
        dps_ref[0:1, :] += jnp.sum(dmix * _dot(pooled, wp), axis=0, keepdims=True)
        dpm = (dmix * ps).astype(BF16)
        dwp_ref[...] += _dot_tn(pooled, dpm)
        dpooled_t = _dot_nt(dpm, wp)
        dpm_w = (dob_ref[0, pl.ds(k0, 3 * tm), :].astype(F32) * ps).astype(BF16)
        dpooled_w = _dot_nt(dpm_w, wp)
        rr = r0 + lax.broadcasted_iota(jnp.int32, (tm, 3 * tm), 0)
        cc = k0 + lax.broadcasted_iota(jnp.int32, (tm, 3 * tm), 1)
        diff = rr - cc
        inseg = (cc >= seg_lo) & (cc < seg_hi)
        ccol = k0 + lax.broadcasted_iota(jnp.int32, (3 * tm, 1), 0)
        group = lax.broadcasted_iota(jnp.int32, (tm, POOL_W), 1) // HEAD
        acc = jnp.zeros((tm, POOL_W), F32)
        for gi, w in enumerate(POOL_WINDOWS):
            h = w // 2
            band_t = jnp.where((diff >= -h) & (diff <= h - 1) & inseg, 1.0, 0.0).astype(BF16)
            hi, lo = _split_bf16(dpooled_w / _pool_count(ccol, h, seg_lo, seg_hi))
            acc = jnp.where(group == gi, _dot(band_t, hi) + _dot(band_t, lo), acc)
        du_ref[0] = (acc - dpooled_t).astype(BF16)

    row = pl.BlockSpec((1, tm, POOL_W), lambda b, t: (b, t, 0))
    return pl.pallas_call(
        body, name="pool_bwd", grid=(B, cfg.nT),
        in_specs=[pl.BlockSpec((1, T, POOL_W), lambda b, t: (b, 0, 0)), row, _full((POOL_W, POOL_W)), _full((1, POOL_W))],
        out_specs=[row, _full((POOL_W, POOL_W)), _full((8, POOL_W))],
        out_shape=[jax.ShapeDtypeStruct((B, T, POOL_W), BF16), jax.ShapeDtypeStruct((POOL_W, POOL_W), F32),
                   jax.ShapeDtypeStruct((8, POOL_W), F32)],
        compiler_params=_params(2),
    )(dob, pooled, wp, ps)


def _in_proj_bwd(cfg, dzq, du, dzg, w_in, x, dx1, gn, mod):
    B, T, D, IN, tm, nC = cfg.B, cfg.T, cfg.D, cfg.IN, cfg.tm, cfg.nC

    def body(dzq_ref, du_ref, dzg_ref, w_ref, x_ref, dx1_ref, gn_ref, mod_ref, dx0_ref, dz_ref, dmod_ref, dgn_ref):
        b, t = pl.program_id(0), pl.program_id(1)
        _acc_init([dmod_ref, dgn_ref])
        row = jnp.where(t < nC, B, b)
        dz = jnp.concatenate([dzq_ref[0], du_ref[0], dzg_ref[0]], axis=-1)
        dz_ref[0] = dz
        dh = _dot_nt(dz, w_ref[...])
        gn = gn_ref[...]
        scale = _mod_row(mod_ref, row, 1, D)
        _, xhat, rstd = _modulate(x_ref[0], gn, _mod_row(mod_ref, row, 0, D), scale)
        dxn, d_shift, d_scale, d_gn = _modulate_bwd(dh, xhat, rstd, gn, scale)
        dx0_ref[0] = dx1_ref[0] + dxn
        dmod_ref[pl.ds(row, 1), :] += jnp.concatenate([d_shift, d_scale], axis=-1)
        dgn_ref[0:1, :] += d_gn

    row = lambda w: pl.BlockSpec((1, tm, w), lambda b, t: (b, t, 0))
    return pl.pallas_call(
        body, name="in_proj_bwd", grid=(B, cfg.nT),
        in_specs=[row(QKV_W), row(POOL_W), row(3 * D), _full((D, IN)), row(D), row(D), _full((1, D)), _full((8, 6 * D))],
        out_specs=[row(D), row(IN), _full((8, 2 * D)), _full((8, D))],
        out_shape=[jax.ShapeDtypeStruct((B, T, D), F32), jax.ShapeDtypeStruct((B, T, IN), BF16),
                   jax.ShapeDtypeStruct((8, 2 * D), F32), jax.ShapeDtypeStruct((8, D), F32)],
        compiler_params=_params(2),
    )(dzq, du, dzg, w_in, x, dx1, gn, mod)


def _adaln_bwd(cfg, cc8, dmod, w_ada):
    d = cfg.D
    tn = 6 * d // 4

    def body(c_ref, dm_ref, w_ref, dc_ref, a_ref, dmb_ref, db_ref):
        j = pl.program_id(0)

        @pl.when(j == 0)
        def _():
            dc_ref[...] = jnp.zeros((8, d), F32)
            c = c_ref[...]
            a_ref[...] = (c * jax.nn.sigmoid(c)).astype(BF16)

        dm = dm_ref[...]
        dmb = dm.astype(BF16)
        dmb_ref[...] = dmb
        db_ref[...] = jnp.broadcast_to(jnp.sum(dm, axis=0, keepdims=True), (8, tn))
        dc_ref[...] += _dot_nt(dmb, w_ref[...])

        @pl.when(j == 3)
        def _():
            c = c_ref[...]
            s = jax.nn.sigmoid(c)
            dc_ref[...] = dc_ref[...] * (s * (1.0 + c * (1.0 - s)))

    return pl.pallas_call(
        body, name="adaln_bwd", grid=(4,),
        in_specs=[_full((8, d)), pl.BlockSpec((8, tn), lambda j: (0, j)), pl.BlockSpec((d, tn), lambda j: (0, j))],
        out_specs=[_full((8, d)), _full((8, d)), pl.BlockSpec((8, tn), lambda j: (0, j)), pl.BlockSpec((8, tn), lambda j: (0, j))],
        out_shape=[jax.ShapeDtypeStruct((8, d), F32), jax.ShapeDtypeStruct((8, d), BF16),
                   jax.ShapeDtypeStruct((8, 6 * d), BF16), jax.ShapeDtypeStruct((8, 6 * d), F32)],
        compiler_params=_params(1),
    )(cc8, dmod, w_ada)


def _peer(k):
    x, y, c = lax.axis_index("x"), lax.axis_index("y"), lax.axis_index("c")
    px = x ^ ((k >> 2) & 1)
    py = y ^ ((k >> 1) & 1)
    pc = c ^ (k & 1)
    return (px, py, pc), 4 * px + 2 * py + pc


def _exchange(x, *, scatter, name):
    shape = x.shape[1:] if scatter else x.shape

    def body(x_ref, out_ref, send_sems, recv_sems, local_sem):
        _, me = _peer(0)
        src_of = (lambda d: x_ref.at[d]) if scatter else (lambda d: x_ref)
        mine = pltpu.make_async_copy(src_of(me), out_ref.at[me], local_sem)
        mine.start()
        copies = []
        for k in range(1, N_DEV):
            pos, idx = _peer(k)
            cp = pltpu.make_async_remote_copy(src_ref=src_of(idx), dst_ref=out_ref.at[me], send_sem=send_sems.at[k - 1],
                                              recv_sem=recv_sems.at[k - 1], device_id=pos, device_id_type=pl.DeviceIdType.MESH)
            cp.start()
            copies.append(cp)
        for k in range(1, N_DEV):
            pos, idx = _peer(k)
            pltpu.make_async_remote_copy(src_ref=src_of(me), dst_ref=out_ref.at[idx], send_sem=send_sems.at[k - 1],
                                         recv_sem=recv_sems.at[k - 1], device_id=pos, device_id_type=pl.DeviceIdType.MESH).wait_recv()
        for cp in copies:
            cp.wait_send()
        mine.wait()

    return pl.pallas_call(
        body, name=name,
        in_specs=[pl.BlockSpec(memory_space=pl.ANY)], out_specs=pl.BlockSpec(memory_space=pl.ANY),
        out_shape=jax.ShapeDtypeStruct((N_DEV,) + tuple(shape), x.dtype),
        scratch_shapes=[pltpu.SemaphoreType.DMA((N_DEV - 1,)), pltpu.SemaphoreType.DMA((N_DEV - 1,)), pltpu.SemaphoreType.DMA],
    )(x)


def _adamw(parts, w, m, v, name):
    rows = w.shape[0]
    tr = next(c for c in (256, 128, 64, 32, 16, 8) if rows % c == 0)
    bc1 = 1.0 - ADAM_B1 ** ADAM_STEP
    bc2 = 1.0 - ADAM_B2 ** ADAM_STEP

    def body(p_ref, w_ref, m_ref, v_ref, g_ref, d_ref, m2_ref, v2_ref):
        g = p_ref[0].astype(F32)
        for d in range(1, N_DEV):
            g = g + p_ref[d].astype(F32)
        m2 = ADAM_B1 * m_ref[...] + (1.0 - ADAM_B1) * g
        v2 = ADAM_B2 * v_ref[...] + (1.0 - ADAM_B2) * (g * g)
        g_ref[...] = g
        m2_ref[...] = m2
        v2_ref[...] = v2
        d_ref[...] = -ADAM_LR * ((m2 / bc1) / (jnp.sqrt(v2 / bc2) + ADAM_EPS) + ADAM_WD * w_ref[...])

    blk = pl.BlockSpec((tr, PACK_W), lambda i: (i, 0))
    return pl.pallas_call(
        body, name=name, grid=(rows // tr,),
        in_specs=[pl.BlockSpec((N_DEV, tr, PACK_W), lambda i: (0, i, 0)), blk, blk, blk],
        out_specs=[blk] * 4, out_shape=[jax.ShapeDtypeStruct((rows, PACK_W), F32)] * 4,
        compiler_params=_params(1),
    )(parts, w, m, v)


_SHARDED = (("w_ada", True), ("w_in", True), ("w_br_a", True), ("w_br_b", True), ("w_br_c", True),
            ("w_out", False), ("w_mlp1", True), ("w_mlp2", False))
_SMALL = ("c_ctx", "b_ada", "norm1", "norm2", "q_norm_a", "k_norm_a", "q_norm_c", "k_norm_c", "sink_c", "w_pool", "pool_scale")


def _pack_shards(shards):
    return jnp.concatenate([shards[n].reshape(-1, PACK_W) for n, _ in _SHARDED], axis=0)


def _unpack_gathered(packed, shard_shapes):
    out, r = {}, 0
    for n, cols in _SHARDED:
        k, w = shard_shapes[n]
        nr = k * w // PACK_W
        blk = packed[:, r:r + nr].reshape(N_DEV, k, w)
        out[n] = blk.transpose(1, 0, 2).reshape(k, N_DEV * w) if cols else blk.reshape(N_DEV * k, w)
        r += nr
    return out


def _pack_full_grads(grads, shard_shapes):
    parts = []
    for n, cols in _SHARDED:
        k, w = shard_shapes[n]
        g = grads[n]
        g = g.reshape(k, N_DEV, w).transpose(1, 0, 2) if cols else g.reshape(N_DEV, k, w)
        parts.append(g.reshape(N_DEV, -1, PACK_W))
    return jnp.concatenate(parts, axis=1)


def _unpack_shards(packed, shard_shapes):
    out, r = {}, 0
    for n, _ in _SHARDED:
        k, w = shard_shapes[n]
        nr = k * w // PACK_W
        out[n] = packed[r:r + nr].reshape(k, w)
        r += nr
    return out


def _pack_small(vals):
    flat = jnp.concatenate([vals[n].reshape(-1) for n in _SMALL])
    rows = -(-flat.shape[0] // (8 * PACK_W)) * 8
    return jnp.pad(flat, (0, rows * PACK_W - flat.shape[0])).reshape(rows, PACK_W)


def _unpack_small(packed, like):
    flat, out, r = packed.reshape(-1), {}, 0
    for n in _SMALL:
        sz = like[n].size
        out[n] = flat[r:r + sz].reshape(like[n].shape)
        r += sz
    return out


def _rope_tables(cfg):
    pos = jnp.arange(cfg.S, dtype=F32)
    r = jnp.floor(pos / GRID_W)
    col = pos - r * GRID_W
    half = HEAD // 4
    inv = 1.0 / (ROPE_THETA ** (jnp.arange(0, HEAD // 2, 2, dtype=F32) / (HEAD // 2)))
    ang = jnp.concatenate([r[:, None] * inv, col[:, None] * inv], axis=-1)
    assert ang.shape[1] == 2 * half
    cos = jnp.repeat(jnp.cos(ang), 2, axis=-1)
    sin = jnp.repeat(jnp.sin(ang), 2, axis=-1) * jnp.tile(jnp.array([-1.0, 1.0], F32), HEAD // 2)
    cos = jnp.concatenate([jnp.ones((cfg.N, HEAD), F32), cos], axis=0)
    sin = jnp.concatenate([jnp.zeros((cfg.N, HEAD), F32), sin], axis=0)
    return jnp.tile(cos, (1, 2)), jnp.tile(sin, (1, 2))


def _gvec(qa, ka, qc, kc):
    one = jnp.ones((KVW,), F32)
    return jnp.concatenate([jnp.tile(qa, N_QHEADS), jnp.tile(ka, N_KV), one, jnp.tile(qc, N_QHEADS), jnp.tile(kc, N_KV), one])[None, :]


def _block_diag(wp):
    g, c, _ = wp.shape
    out = jnp.zeros((g * c, g * c), wp.dtype)
    for i in range(g):
        out = out.at[i * c:(i + 1) * c, i * c:(i + 1) * c].set(wp[i])
    return out


def _pad8(a):
    return jnp.pad(a, ((0, 8 - a.shape[0]), (0, 0)))


def kernel(x, c, ctx, c_ctx, w_ada, b_ada, norm1, norm2, w_in, q_norm_a, k_norm_a, q_norm_c, k_norm_c, sink_c, w_pool, pool_scale, w_br_a, w_br_b, w_br_c, w_out, w_mlp1, w_mlp2, loss_target, m_c_ctx, m_w_ada, m_b_ada, m_norm1, m_norm2, m_w_in, m_q_norm_a, m_k_norm_a, m_q_norm_c, m_k_norm_c, m_sink_c, m_w_pool, m_pool_scale, m_w_br_a, m_w_br_b, m_w_br_c, m_w_out, m_w_mlp1, m_w_mlp2, v_c_ctx, v_w_ada, v_b_ada, v_norm1, v_norm2, v_w_in, v_q_norm_a, v_k_norm_a, v_q_norm_c, v_k_norm_c, v_sink_c, v_w_pool, v_pool_scale, v_w_br_a, v_w_br_b, v_w_br_c, v_w_out, v_w_mlp1, v_w_mlp2):
    B, S, D = x.shape
    N = ctx.shape[1]
    L = w_ada.shape[0]
    cfg = _Cfg(B, S, N, D)
    T = cfg.T
    weights = dict(c_ctx=c_ctx, w_ada=w_ada, b_ada=b_ada, norm1=norm1, norm2=norm2, w_in=w_in, q_norm_a=q_norm_a,
                   k_norm_a=k_norm_a, q_norm_c=q_norm_c, k_norm_c=k_norm_c, sink_c=sink_c, w_pool=w_pool,
                   pool_scale=pool_scale, w_br_a=w_br_a, w_br_b=w_br_b, w_br_c=w_br_c, w_out=w_out, w_mlp1=w_mlp1, w_mlp2=w_mlp2)
    mom_m = dict(c_ctx=m_c_ctx, w_ada=m_w_ada, b_ada=m_b_ada, norm1=m_norm1, norm2=m_norm2, w_in=m_w_in, q_norm_a=m_q_norm_a,
                 k_norm_a=m_k_norm_a, q_norm_c=m_q_norm_c, k_norm_c=m_k_norm_c, sink_c=m_sink_c, w_pool=m_w_pool,
                 pool_scale=m_pool_scale, w_br_a=m_w_br_a, w_br_b=m_w_br_b, w_br_c=m_w_br_c, w_out=m_w_out, w_mlp1=m_w_mlp1, w_mlp2=m_w_mlp2)
    mom_v = dict(c_ctx=v_c_ctx, w_ada=v_w_ada, b_ada=v_b_ada, norm1=v_norm1, norm2=v_norm2, w_in=v_w_in, q_norm_a=v_q_norm_a,
                 k_norm_a=v_k_norm_a, q_norm_c=v_q_norm_c, k_norm_c=v_k_norm_c, sink_c=v_sink_c, w_pool=v_w_pool,
                 pool_scale=v_pool_scale, w_br_a=v_w_br_a, w_br_b=v_w_br_b, w_br_c=v_w_br_c, w_out=v_w_out, w_mlp1=v_w_mlp1, w_mlp2=v_w_mlp2)
    shard_shapes = {n: weights[n].shape[1:] for n, _ in _SHARDED}

    packed_w = jnp.concatenate([_pack_shards({n: weights[n][l] for n, _ in _SHARDED}) for l in range(L)], axis=0)
    rows_l = packed_w.shape[0] // L
    gathered = _exchange(packed_w.astype(BF16), scatter=False, name="gather_weights")
    full = [_unpack_gathered(gathered[:, l * rows_l:(l + 1) * rows_l], shard_shapes) for l in range(L)]

    cosf, sins = _rope_tables(cfg)
    xs = jnp.concatenate([ctx, x], axis=1)
    cc8 = _pad8(jnp.concatenate([c, c_ctx[None, :]], axis=0))
    zeros_q = jnp.zeros((B, N, QW), F32)
    zeros_k = jnp.zeros((B, N, KVW), F32)
    va_blk, vc_blk = (QW + KVW) // KVW, (2 * QW + 3 * KVW) // KVW

    saved = []
    for l in range(L):
        fw = full[l]
        ctx_active = l < L - 1
        mod = _adaln_fwd(cfg, cc8, fw["w_ada"], b_ada[l][None, :])
        z, h = _in_proj_fwd(cfg, xs, norm1[l][None, :], mod, fw["w_in"])
        gvec = _gvec(q_norm_a[l], k_norm_a[l], q_norm_c[l], k_norm_c[l])
        qa, ka, qc, kc = _qknorm_fwd(cfg, z, gvec, cosf, sins)
        sink8 = jnp.zeros((8, LANES), F32).at[0, :N_QHEADS].set(sink_c[l])
        oa_l = _attn_fwd(cfg, qa, ka, z, va_blk, sink8, loc=True, window=False, sink=False, name="attn_a_fwd")
        oc_l = _attn_fwd(cfg, qc, kc, z, vc_blk, sink8, loc=True, window=True, sink=True, name="attn_c_fwd")
        if ctx_active:
            oa_c = _attn_fwd(cfg, qa, ka, z, va_blk, sink8, loc=False, window=False, sink=False, name="attn_a_ctx_fwd")
            oc_c = _attn_fwd(cfg, qc, kc, z, vc_blk, sink8, loc=False, window=False, sink=True, name="attn_c_ctx_fwd")
        else:
            oa_c = oc_c = jnp.zeros((B, N, QW), BF16)
        oa = jnp.concatenate([oa_c, oa_l], axis=1)
        oc = jnp.concatenate([oc_c, oc_l], axis=1)
        wp = _block_diag(w_pool[l]).astype(BF16)
        ps = pool_scale[l][None, :]
        ob, pooled = _pool_fwd(cfg, z, wp, ps)
        x1, mgo = _merge_fwd(cfg, xs, oa, ob, oc, z, mod, fw["w_br_a"], fw["w_br_b"], fw["w_br_c"], fw["w_out"], ctx_active=ctx_active)
        x2, mo = _mlp_fwd(cfg, x1, norm2[l][None, :], mod, fw["w_mlp1"], fw["w_mlp2"], ctx_active=ctx_active)
        saved.append(dict(xs=xs, mod=mod, z=z, h=h, gvec=gvec, qa=qa, ka=ka, qc=qc, kc=kc, sink8=sink8, oa=oa, oc=oc, ob=ob,
                          pooled=pooled, wp=wp, ps=ps, x1=x1, mgo=mgo, mo=mo))
        xs = x2

    dxs, sse = _loss_fwd_bwd(cfg, xs, loss_target)
    loss = lax.psum(0.5 * sse[0, 0] / D, ("x", "y", "c"))

    grads_full = [None] * L
    small = {n: [None] * L for n in _SMALL if n != "c_ctx"}
    d_c_ctx = jnp.zeros((D,), F32)
    flat2 = lambda a: a.reshape(B * T, a.shape[-1])
    for l in reversed(range(L)):
        fw, sv = full[l], saved[l]
        ctx_active = l < L - 1
        mod = sv["mod"]
        dx1, h2, r, da, dout, dmod_mlp, dgn2 = _mlp_bwd(cfg, sv["x1"], dxs, sv["mo"], norm2[l][None, :], mod, fw["w_mlp1"], fw["w_mlp2"], ctx_active=ctx_active)
        g = {}
        g["w_mlp1"] = _matmul_tn(flat2(h2), flat2(da), "dw_mlp1")
        g["w_mlp2"] = _matmul_tn(flat2(r), flat2(dout), "dw_mlp2")
        doa, dob, doc, dpa, dpb, dpc, y, dmo, dzg, dg1 = _merge_bwd(
            cfg, dx1, sv["mgo"], sv["oa"], sv["ob"], sv["oc"], sv["z"], mod, fw["w_br_a"], fw["w_br_b"], fw["w_br_c"], fw["w_out"], ctx_active=ctx_active)
        g["w_out"] = _matmul_tn(flat2(y), flat2(dmo), "dw_out")
        g["w_br_a"] = _matmul_tn(flat2(sv["oa"]), flat2(dpa), "dw_br_a")
        g["w_br_b"] = _matmul_tn(flat2(sv["ob"]), flat2(dpb), "dw_br_b")
        g["w_br_c"] = _matmul_tn(flat2(sv["oc"]), flat2(dpc), "dw_br_c")
        z = sv["z"]
        dqa, dka, dva, _ = _attn_bwd(cfg, sv["qa"], sv["ka"], z, va_blk, sv["sink8"], doa, loc=True, window=False, sink=False, name="attn_a_bwd")
        dqc, dkc, dvc, dsink = _attn_bwd(cfg, sv["qc"], sv["kc"], z, vc_blk, sv["sink8"], doc, loc=True, window=True, sink=True, name="attn_c_bwd")
        if ctx_active:
            cqa, cka, cva, _ = _attn_bwd(cfg, sv["qa"], sv["ka"], z, va_blk, sv["sink8"], doa, loc=False, window=False, sink=False, name="attn_a_ctx_bwd")
            cqc, ckc, cvc, dsink_c = _attn_bwd(cfg, sv["qc"], sv["kc"], z, vc_blk, sv["sink8"], doc, loc=False, window=False, sink=True, name="attn_c_ctx_bwd")
            dsink = dsink + dsink_c
        else:
            cqa, cka, cva, cqc, ckc, cvc = zeros_q, zeros_k, zeros_k, zeros_q, zeros_k, zeros_k
        dzq, dgvec = _qknorm_bwd(cfg, z, sv["gvec"], cosf, sins, (dqa, dka, dva, dqc, dkc, dvc), (cqa, cka, cva, cqc, ckc, cvc))
        du, dwp, dps = _pool_bwd(cfg, dob, sv["pooled"], sv["wp"], sv["ps"])
        dxs, dz, dmod_in, dgn1 = _in_proj_bwd(cfg, dzq, du, dzg, fw["w_in"], sv["xs"], dx1, norm1[l][None, :], mod)
        g["w_in"] = _matmul_tn(flat2(sv["h"]), flat2(dz), "dw_in")
        dmod = jnp.concatenate([dmod_in, dg1, dmod_mlp], axis=-1)
        dcc, act, dmod_b, dbias = _adaln_bwd(cfg, cc8, dmod, fw["w_ada"])
        g["w_ada"] = _matmul_tn(act, dmod_b, "dw_ada")
        grads_full[l] = g
        d_c_ctx = d_c_ctx + dcc[B]
        gv = dgvec[0]
        heads = lambda v, n: v.reshape(n, HEAD).sum(axis=0)
        small["b_ada"][l] = dbias[0]
        small["norm1"][l] = dgn1[0]
        small["norm2"][l] = dgn2[0]
        small["q_norm_a"][l] = heads(gv[0:QW], N_QHEADS)
        small["k_norm_a"][l] = heads(gv[QW:QW + KVW], N_KV)
        small["q_norm_c"][l] = heads(gv[QW + 2 * KVW:2 * QW + 2 * KVW], N_QHEADS)
        small["k_norm_c"][l] = heads(gv[2 * QW + 2 * KVW:2 * QW + 3 * KVW], N_KV)
        small["sink_c"][l] = dsink[0, :N_QHEADS]
        small["w_pool"][l] = jnp.stack([dwp[i * HEAD:(i + 1) * HEAD, i * HEAD:(i + 1) * HEAD] for i in range(len(POOL_WINDOWS))])
        small["pool_scale"][l] = dps[0]
    grad_x = dxs[:, N:, :]

    packed_g = jnp.concatenate([_pack_full_grads(grads_full[l], shard_shapes) for l in range(L)], axis=1)
    parts = _exchange(packed_g, scatter=True, name="scatter_grads")
    pack_local = lambda src: jnp.concatenate([_pack_shards({n: src[n][l] for n, _ in _SHARDED}) for l in range(L)], axis=0)
    g_sh, d_sh, m_sh, v_sh = _adamw(parts, packed_w, pack_local(mom_m), pack_local(mom_v), "adamw_sharded")

    small_vals = {n: jnp.stack(v) for n, v in small.items()}
    small_vals["c_ctx"] = d_c_ctx
    small_parts = _exchange(_pack_small(small_vals), scatter=False, name="gather_small_grads")
    g_sm, d_sm, m_sm, v_sm = _adamw(small_parts, _pack_small(weights), _pack_small(mom_m), _pack_small(mom_v), "adamw_small")

    outs = {}
    for tag, sh, sm in (("grad", g_sh, g_sm), ("delta", d_sh, d_sm), ("new_m", m_sh, m_sm), ("new_v", v_sh, v_sm)):
        per_layer = [_unpack_shards(sh[l * rows_l:(l + 1) * rows_l], shard_shapes) for l in range(L)]
        res = {n: jnp.stack([per_layer[l][n] for l in range(L)]) for n, _ in _SHARDED}
        res.update(_unpack_small(sm, weights))
        outs[tag] = res
    order = ("c_ctx", "w_ada", "b_ada", "norm1", "norm2", "w_in", "q_norm_a", "k_norm_a", "q_norm_c", "k_norm_c", "sink_c",
             "w_pool", "pool_scale", "w_br_a", "w_br_b", "w_br_c", "w_out", "w_mlp1", "w_mlp2")
    return (loss, grad_x, *[outs[tag][n] for tag in ("grad", "delta", "new_m", "new_v") for n in order])
```

```python
import functools

import jax
import jax.numpy as jnp
from jax import lax
from jax.experimental import pallas as pl
from jax.experimental.pallas import tpu as pltpu

F32 = jnp.float32
BF16 = jnp.bfloat16
HIGHEST = lax.Precision.HIGHEST

N_DEV = 8
HEAD = 64
N_QHEADS = 6
N_KV = 2
GROUP = 3
QW = N_QHEADS * HEAD
KVW = N_KV * HEAD
QKV_W = 2 * (QW + 2 * KVW)
POOL_W = 256
POOL_WINDOWS = (2, 4, 8, 16)
GATE0 = QKV_W + POOL_W
WINDOW = 128
GRID_W = 64
ROPE_THETA = 10000.0
EPS = 1e-6
NEG = -1e30
QSCALE = HEAD ** -0.5
LANES = 128
PACK_W = 1024
VMEM_LIMIT = 56 * 1024 * 1024

ADAM_LR = 0.001
ADAM_B1 = 0.9
ADAM_B2 = 0.999
ADAM_EPS = 1e-08
ADAM_WD = 0.01
ADAM_STEP = 10

NT_DIMS = (((1,), (1,)), ((), ()))
TN_DIMS = (((0,), (0,)), ((), ()))


def _dot(a, b):
    return jnp.dot(a, b, preferred_element_type=F32)


def _dot_nt(a, b):
    return lax.dot_general(a, b, NT_DIMS, preferred_element_type=F32)


def _dot_tn(a, b):
    return lax.dot_general(a, b, TN_DIMS, preferred_element_type=F32)


def _params(n_grid, vmem=None):
    return pltpu.CompilerParams(dimension_semantics=("arbitrary",) * n_grid, vmem_limit_bytes=vmem or VMEM_LIMIT)


def _full(shape):
    nd = len(shape)
    return pl.BlockSpec(shape, lambda *_: (0,) * nd)


def _modulate(x, gn, shift, scale):
    rstd = lax.rsqrt(jnp.mean(x * x, axis=-1, keepdims=True) + EPS)
    xhat = x * rstd
    return xhat * gn * (1.0 + scale) + shift, xhat, rstd


def _modulate_bwd(dh, xhat, rstd, gn, scale):
    d_shift = jnp.sum(dh, axis=0, keepdims=True)
    d_scale = jnp.sum(dh * xhat * gn, axis=0, keepdims=True)
    dy = dh * (1.0 + scale)
    d_gn = jnp.sum(dy * xhat, axis=0, keepdims=True)
    dxh = dy * gn
    dx = rstd * (dxh - xhat * jnp.mean(dxh * xhat, axis=-1, keepdims=True))
    return dx, d_shift, d_scale, d_gn


def _mod_row(mod_ref, row, k, d):
    return mod_ref[pl.ds(row, 1), k * d:(k + 1) * d]


class _Cfg:
    def __init__(self, b, s, n, d):
        self.B, self.S, self.N, self.D = b, s, n, d
        self.T = n + s
        self.F = 4 * d
        self.IN = GATE0 + 3 * d
        self.tm = 256 if (n % 256 == 0 and s % 256 == 0) else 128
        self.nT = self.T // self.tm
        self.nC = n // self.tm
        self.gw = 512 if d % 512 == 0 else 256
        assert GATE0 % self.gw == 0 and d % self.gw == 0 and b < 8 and self.T >= 3 * self.tm
        assert s % GRID_W == 0 and n % self.tm == 0 and s % self.tm == 0 and s >= self.tm + 2 * WINDOW


def _adaln_fwd(cfg, cc8, w_ada, b_ada):
    d = cfg.D
    tn = 6 * d // 4

    def body(c_ref, w_ref, b_ref, o_ref):
        c = c_ref[...]
        a = (c * jax.nn.sigmoid(c)).astype(BF16)
        o_ref[...] = _dot(a, w_ref[...]) + b_ref[...]

    return pl.pallas_call(
        body, name="adaln_fwd", grid=(4,),
        in_specs=[_full((8, d)), pl.BlockSpec((d, tn), lambda j: (0, j)), pl.BlockSpec((1, tn), lambda j: (0, j))],
        out_specs=pl.BlockSpec((8, tn), lambda j: (0, j)),
        out_shape=jax.ShapeDtypeStruct((8, 6 * d), F32), compiler_params=_params(1),
    )(cc8, w_ada, b_ada)


def _in_proj_fwd(cfg, x, gn, mod, w_in, exch=None):
    B, T, D, IN, tm, nC = cfg.B, cfg.T, cfg.D, cfg.IN, cfg.tm, cfg.nC

    def body(x_ref, gn_ref, mod_ref, w_ref, z_ref, h_ref):
        b, t = pl.program_id(0), pl.program_id(1)
        row = jnp.where(t < nC, B, b)
        h, _, _ = _modulate(x_ref[0], gn_ref[...], _mod_row(mod_ref, row, 0, D), _mod_row(mod_ref, row, 1, D))
        hb = h.astype(BF16)
        h_ref[0] = hb
        z_ref[0] = _dot(hb, w_ref[...])

    return _pcall(exch)(
        body, name="in_proj_fwd", grid=(B, cfg.nT),
        in_specs=[pl.BlockSpec((1, tm, D), lambda b, t: (b, t, 0)), _full((1, D)), _full((8, 6 * D)), _full((D, IN))],
        out_specs=[pl.BlockSpec((1, tm, IN), lambda b, t: (b, t, 0)), pl.BlockSpec((1, tm, D), lambda b, t: (b, t, 0))],
        out_shape=[jax.ShapeDtypeStruct((B, T, IN), F32), jax.ShapeDtypeStruct((B, T, D), BF16)],
        compiler_params=_params(2),
    )(x, gn, mod, w_in)


def _head_indicator():
    r = lax.broadcasted_iota(jnp.int32, (LANES, LANES), 0) // HEAD
    c = lax.broadcasted_iota(jnp.int32, (LANES, LANES), 1) // HEAD
    return jnp.where(r == c, 1.0, 0.0).astype(F32)


def _pair_swap(y):
    lane = lax.broadcasted_iota(jnp.int32, y.shape, 1)
    return jnp.where(lane % 2 == 0, pltpu.roll(y, LANES - 1, 1), pltpu.roll(y, 1, 1))


_QK_CHUNKS = (0, 1, 2, 3, 5, 6, 7, 8)
_Q_CHUNKS = (0, 1, 2, 5, 6, 7)


def _qknorm_fwd(cfg, z, gvec, cosf, sins):
    B, T, tm = cfg.B, cfg.T, cfg.tm

    def body(z_ref, g_ref, cos_ref, sin_ref, qa_ref, ka_ref, qc_ref, kc_ref):
        ind = _head_indicator()
        cos, sin = cos_ref[...], sin_ref[...]

        def chunk(c):
            x = z_ref[0, :, c * LANES:(c + 1) * LANES]
            ss = jnp.dot(x * x, ind, precision=HIGHEST, preferred_element_type=F32)
            y = x * lax.rsqrt(ss * (1.0 / HEAD) + EPS) * g_ref[:, c * LANES:(c + 1) * LANES]
            out = y * cos + _pair_swap(y) * sin
            return (out * QSCALE if c in _Q_CHUNKS else out).astype(BF16)

        qa_ref[0] = jnp.concatenate([chunk(0), chunk(1), chunk(2)], axis=-1)
        ka_ref[0] = chunk(3)
        qc_ref[0] = jnp.concatenate([chunk(5), chunk(6), chunk(7)], axis=-1)
        kc_ref[0] = chunk(8)

    row = lambda w: pl.BlockSpec((1, tm, w), lambda b, t: (b, t, 0))
    tab = pl.BlockSpec((tm, LANES), lambda b, t: (t, 0))
    return pl.pallas_call(
        body, name="qknorm_fwd", grid=(B, cfg.nT),
        in_specs=[row(QKV_W), _full((1, QKV_W)), tab, tab],
        out_specs=[row(QW), row(KVW), row(QW), row(KVW)],
        out_shape=[jax.ShapeDtypeStruct((B, T, w), BF16) for w in (QW, KVW, QW, KVW)],
        compiler_params=_params(2),
    )(z, gvec, cosf, sins)


def _attn_scores(cfg, t, q, k_ref, v_ref, sink_ref, h, loc, window, sink):
    S, N, tq = cfg.S, cfg.N, cfg.tm
    hs = slice(h * HEAD, (h + 1) * HEAD)
    qs = jnp.concatenate([q[:, (GROUP * h + g) * HEAD:(GROUP * h + g + 1) * HEAD] for g in range(GROUP)], axis=0)
    k_ctx = k_ref[0, 0:N, :][:, hs]
    v_ctx = v_ref[0, 0:N, :].astype(BF16)[:, hs]
    s_c = _dot_nt(qs, k_ctx)
    m = jnp.max(s_c, axis=-1, keepdims=True)
    k_loc = v_loc = s_l = lo = None
    if loc:
        if window:
            W = tq + 2 * WINDOW
            lo = pl.multiple_of(jnp.clip(t * tq - WINDOW, 0, S - W), LANES)
        else:
            W, lo = S, 0
        k_loc = k_ref[0, pl.ds(N + lo, W), :][:, hs]
        v_loc = v_ref[0, pl.ds(N + lo, W), :].astype(BF16)[:, hs]
        s_l = _dot_nt(qs, k_loc)
        if window:
            qpos = t * tq + lax.broadcasted_iota(jnp.int32, (GROUP * tq, W), 0) % tq
            kpos = lo + lax.broadcasted_iota(jnp.int32, (GROUP * tq, W), 1)
            s_l = jnp.where(jnp.abs(qpos - kpos) <= WINDOW, s_l, NEG)
        m = jnp.maximum(m, jnp.max(s_l, axis=-1, keepdims=True))
    if sink:
        rowg = lax.broadcasted_iota(jnp.int32, (GROUP * tq, 1), 0) // tq
        sk = jnp.zeros((GROUP * tq, 1), F32)
        for g in range(GROUP):
            j = GROUP * h + g
            sk = jnp.where(rowg == g, sink_ref[0:1, j:j + 1], sk)
        m = jnp.maximum(m, sk)
    e_c = jnp.exp(s_c - m)
    l = jnp.sum(e_c, axis=-1, keepdims=True)
    e_l = e_s = None
    if loc:
        e_l = jnp.exp(s_l - m)
        l = l + jnp.sum(e_l, axis=-1, keepdims=True)
    if sink:
        e_s = jnp.exp(sk - m)
        l = l + e_s
    inv = 1.0 / l
    p_c = e_c * inv
    p_l = e_l * inv if loc else None
    p_s = e_s * inv if sink else None
    return qs, p_c, p_l, p_s, k_ctx, v_ctx, k_loc, v_loc, lo


def _attn_fwd(cfg, q, k, z, vblock, sink8, *, loc, window, sink, name, exch=None):
    B, S, N, T, tq, nC = cfg.B, cfg.S, cfg.N, cfg.T, cfg.tm, cfg.nC
    nk = T if loc else N
    nq = S if loc else N
    qoff = nC if loc else 0

    def body(q_ref, k_ref, v_ref, sink_ref, o_ref):
        t = pl.program_id(1)
        q_t = q_ref[0]
        outs = [None] * N_QHEADS
        for h in range(N_KV):
            _, p_c, p_l, _, _, v_ctx, _, v_loc, _ = _attn_scores(cfg, t, q_t, k_ref, v_ref, sink_ref, h, loc, window, sink)
            o = _dot(p_c.astype(BF16), v_ctx)
            if loc:
                o = o + _dot(p_l.astype(BF16), v_loc)
            for g in range(GROUP):
                outs[GROUP * h + g] = o[g * tq:(g + 1) * tq]
        o_ref[0] = jnp.concatenate(outs, axis=-1).astype(BF16)

    return _pcall(exch)(
        body, name=name, grid=(B, nq // tq),
        in_specs=[pl.BlockSpec((1, tq, QW), lambda b, t: (b, t + qoff, 0)),
                  pl.BlockSpec((1, nk, KVW), lambda b, t: (b, 0, 0)),
                  pl.BlockSpec((1, nk, KVW), lambda b, t: (b, 0, vblock)),
                  _full((8, LANES))],
        out_specs=pl.BlockSpec((1, tq, QW), lambda b, t: (b, t, 0)),
        out_shape=jax.ShapeDtypeStruct((B, nq, QW), BF16), compiler_params=_params(2),
    )(q, k, z, sink8)


def _pool_geometry(cfg, t):
    tm, N, T, nC = cfg.tm, cfg.N, cfg.T, cfg.nC
    r0 = pl.multiple_of(t * tm, tm)
    isctx = t < nC
    seg_lo = jnp.where(isctx, 0, N)
    seg_hi = jnp.where(isctx, N, T)
    k0 = pl.multiple_of(jnp.clip(t * tm - tm, 0, T - 3 * tm), tm)
    return r0, seg_lo, seg_hi, k0


def _pool_count(pos, h, seg_lo, seg_hi):
    return jnp.maximum(jnp.minimum(pos + h, seg_hi) - jnp.maximum(pos - h, seg_lo), 1).astype(F32)


def _split_bf16(x):
    hi = x.astype(BF16)
    return hi, (x - hi.astype(F32)).astype(BF16)


def _pool_fwd(cfg, z, wp, ps):
    B, T, tm = cfg.B, cfg.T, cfg.tm

    def body(u_ref, wp_ref, ps_ref, ob_ref, pooled_ref):
        t = pl.program_id(1)
        r0, seg_lo, seg_hi, k0 = _pool_geometry(cfg, t)
        hi, lo = _split_bf16(u_ref[0, pl.ds(k0, 3 * tm), :])
        rr = r0 + lax.broadcasted_iota(jnp.int32, (tm, 3 * tm), 0)
        cc = k0 + lax.broadcasted_iota(jnp.int32, (tm, 3 * tm), 1)
        diff = cc - rr
        inseg = (cc >= seg_lo) & (cc < seg_hi)
        rcol = r0 + lax.broadcasted_iota(jnp.int32, (tm, 1), 0)
        group = lax.broadcasted_iota(jnp.int32, (tm, POOL_W), 1) // HEAD
        acc = jnp.zeros((tm, POOL_W), F32)
        for gi, w in enumerate(POOL_WINDOWS):
            h = w // 2
            band = jnp.where((diff >= -h) & (diff <= h - 1) & inseg, 1.0, 0.0).astype(BF16)
            tot = _dot(band, hi) + _dot(band, lo)
            acc = jnp.where(group == gi, tot / _pool_count(rcol, h, seg_lo, seg_hi), acc)
        pooled = (acc - u_ref[0, pl.ds(r0, tm), :]).astype(BF16)
        pooled_ref[0] = pooled
        ob_ref[0] = (_dot(pooled, wp_ref[...]) * ps_ref[...]).astype(BF16)

    row = pl.BlockSpec((1, tm, POOL_W), lambda b, t: (b, t, 0))
    return pl.pallas_call(
        body, name="pool_fwd", grid=(B, cfg.nT),
        in_specs=[pl.BlockSpec((1, T, POOL_W), lambda b, t: (b, 0, QKV_W // POOL_W)), _full((POOL_W, POOL_W)), _full((1, POOL_W))],
        out_specs=[row, row],
        out_shape=[jax.ShapeDtypeStruct((B, T, POOL_W), BF16)] * 2, compiler_params=_params(2),
    )(z, wp, ps)


def _gate_specs(cfg):
    tm, gw = cfg.tm, cfg.gw
    first = GATE0 // gw
    return [pl.BlockSpec((1, tm, gw), functools.partial(lambda b, t, j: (b, t, j), j=first + i)) for i in range(3 * cfg.D // gw)]


def _read_gates(cfg, gate_refs):
    per = cfg.D // cfg.gw
    return [jnp.concatenate([gate_refs[k * per + i][0] for i in range(per)], axis=-1) for k in range(3)]


def _merge_fwd(cfg, x, oa, ob, oc, z, mod, wa, wb, wc, wo, *, ctx_active, exch=None):
    B, T, D, tm, nC = cfg.B, cfg.T, cfg.D, cfg.tm, cfg.nC
    ng = 3 * D // cfg.gw

    def body(x_ref, oa_ref, ob_ref, oc_ref, *rest):
        gate_refs = rest[:ng]
        mod_ref, wa_ref, wb_ref, wc_ref, wo_ref, x1_ref, mgo_ref = rest[ng:]
        b, t = pl.program_id(0), pl.program_id(1)

        def compute():
            row = jnp.where(t < nC, B, b)
            ga, gb, gc = _read_gates(cfg, gate_refs)
            y = (jax.nn.sigmoid(ga) * _dot(oa_ref[0], wa_ref[...])
                 + jax.nn.sigmoid(gb) * _dot(ob_ref[0], wb_ref[...])
                 + jax.nn.sigmoid(gc) * _dot(oc_ref[0], wc_ref[...]))
            mo = _dot(y.astype(BF16), wo_ref[...])
            mgo_ref[0] = mo.astype(BF16)
            x1_ref[0] = x_ref[0] + _mod_row(mod_ref, row, 2, D) * mo

        if ctx_active:
            compute()
        else:
            pl.when(t >= nC)(compute)

            @pl.when(t < nC)
            def _():
                mgo_ref[0] = jnp.zeros((tm, D), BF16)
                x1_ref[0] = x_ref[0]

    row = lambda w: pl.BlockSpec((1, tm, w), lambda b, t: (b, t, 0))
    return _pcall(exch)(
        body, name="merge_fwd", grid=(B, cfg.nT),
        in_specs=[row(D), row(QW), row(POOL_W), row(QW)] + _gate_specs(cfg)
        + [_full((8, 6 * D)), _full((QW, D)), _full((POOL_W, D)), _full((QW, D)), _full((D, D))],
        out_specs=[row(D), row(D)],
        out_shape=[jax.ShapeDtypeStruct((B, T, D), F32), jax.ShapeDtypeStruct((B, T, D), BF16)],
        compiler_params=_params(2),
    )(x, oa, ob, oc, *([z] * ng), mod, wa, wb, wc, wo)


def _mlp_fwd(cfg, x1, gn, mod, w1, w2, *, ctx_active, exch=None):
    B, T, D, F, tm, nC = cfg.B, cfg.T, cfg.D, cfg.F, cfg.tm, cfg.nC

    def body(x_ref, gn_ref, mod_ref, w1_ref, w2_ref, x2_ref, mo_ref):
        b, t = pl.program_id(0), pl.program_id(1)

        def compute():
            row = jnp.where(t < nC, B, b)
            x = x_ref[0]
            h, _, _ = _modulate(x, gn_ref[...], _mod_row(mod_ref, row, 3, D), _mod_row(mod_ref, row, 4, D))
            a = jnp.maximum(_dot(h.astype(BF16), w1_ref[...]), 0.0)
            mo = _dot((a * a).astype(BF16), w2_ref[...])
            mo_ref[0] = mo.astype(BF16)
            x2_ref[0] = x + _mod_row(mod_ref, row, 5, D) * mo

        if ctx_active:
            compute()
        else:
            pl.when(t >= nC)(compute)

            @pl.when(t < nC)
            def _():
                mo_ref[0] = jnp.zeros((tm, D), BF16)
                x2_ref[0] = x_ref[0]

    row = pl.BlockSpec((1, tm, D), lambda b, t: (b, t, 0))
    return _pcall(exch)(
        body, name="mlp_fwd", grid=(B, cfg.nT),
        in_specs=[row, _full((1, D)), _full((8, 6 * D)), _full((D, F)), _full((F, D))],
        out_specs=[row, row],
        out_shape=[jax.ShapeDtypeStruct((B, T, D), F32), jax.ShapeDtypeStruct((B, T, D), BF16)],
        compiler_params=_params(2),
    )(x1, gn, mod, w1, w2)


def _loss_fwd_bwd(cfg, x2, target):
    B, T, D, tm, nC = cfg.B, cfg.T, cfg.D, cfg.tm, cfg.nC

    def body(x_ref, tgt_ref, dx_ref, sse_ref):
        b, t = pl.program_id(0), pl.program_id(1)

        @pl.when((b == 0) & (t == 0))
        def _():
            sse_ref[...] = jnp.zeros((8, LANES), F32)

        @pl.when(t < nC)
        def _():
            dx_ref[0] = jnp.zeros((tm, D), F32)

        @pl.when(t >= nC)
        def _():
            err = x_ref[0] - tgt_ref[0]
            dx_ref[0] = err * (1.0 / D)
            sse_ref[...] += jnp.sum(err * err)

    return pl.pallas_call(
        body, name="loss", grid=(B, cfg.nT),
        in_specs=[pl.BlockSpec((1, tm, D), lambda b, t: (b, t, 0)),
                  pl.BlockSpec((1, tm, D), lambda b, t: (b, jnp.maximum(t - nC, 0), 0))],
        out_specs=[pl.BlockSpec((1, tm, D), lambda b, t: (b, t, 0)), _full((8, LANES))],
        out_shape=[jax.ShapeDtypeStruct((B, T, D), F32), jax.ShapeDtypeStruct((8, LANES), F32)],
        compiler_params=_params(2),
    )(x2, target)


def _acc_init(refs_shapes):
    b, t = pl.program_id(0), pl.program_id(1)

    @pl.when((b == 0) & (t == 0))
    def _():
        for ref in refs_shapes:
            ref[...] = jnp.zeros(ref.shape, ref.dtype)


def _mlp_bwd(cfg, x1, dx2, mo, gn, mod, w1, w2, *, ctx_active, exch=None):
    B, T, D, F, tm, nC = cfg.B, cfg.T, cfg.D, cfg.F, cfg.tm, cfg.nC

    def body(x_ref, dx_ref, mo_ref, gn_ref, mod_ref, w1_ref, w2_ref, dx1_ref, h_ref, r_ref, da_ref, dout_ref, dmod_ref, dgn_ref):
        b, t = pl.program_id(0), pl.program_id(1)
        _acc_init([dmod_ref, dgn_ref])

        def compute():
            row = jnp.where(t < nC, B, b)
            gn = gn_ref[...]
            scale = _mod_row(mod_ref, row, 4, D)
            h, xhat, rstd = _modulate(x_ref[0], gn, _mod_row(mod_ref, row, 3, D), scale)
            hb = h.astype(BF16)
            a = jnp.maximum(_dot(hb, w1_ref[...]), 0.0)
            dx = dx_ref[0]
            dout = (dx * _mod_row(mod_ref, row, 5, D)).astype(BF16)
            da = (_dot_nt(dout, w2_ref[...]) * (2.0 * a)).astype(BF16)
            dh = _dot_nt(da, w1_ref[...])
            dxn, d_shift, d_scale, d_gn = _modulate_bwd(dh, xhat, rstd, gn, scale)
            dx1_ref[0] = dx + dxn
            h_ref[0] = hb
            r_ref[0] = (a * a).astype(BF16)
            da_ref[0] = da
            dout_ref[0] = dout
            d_gate = jnp.sum(dx * mo_ref[0].astype(F32), axis=0, keepdims=True)
            dmod_ref[pl.ds(row, 1), :] += jnp.concatenate([d_shift, d_scale, d_gate], axis=-1)
            dgn_ref[0:1, :] += d_gn

        if ctx_active:
            compute()
        else:
            pl.when(t >= nC)(compute)

            @pl.when(t < nC)
            def _():
                dx1_ref[0] = dx_ref[0]
                h_ref[0] = jnp.zeros((tm, D), BF16)
                r_ref[0] = jnp.zeros((tm, F), BF16)
                da_ref[0] = jnp.zeros((tm, F), BF16)
                dout_ref[0] = jnp.zeros((tm, D), BF16)

    row = lambda w: pl.BlockSpec((1, tm, w), lambda b, t: (b, t, 0))
    sds = lambda w, dt: jax.ShapeDtypeStruct((B, T, w), dt)
    return _pcall(exch)(
        body, name="mlp_bwd", grid=(B, cfg.nT),
        in_specs=[row(D), row(D), row(D), _full((1, D)), _full((8, 6 * D)), _full((D, F)), _full((F, D))],
        out_specs=[row(D), row(D), row(F), row(F), row(D), _full((8, 3 * D)), _full((8, D))],
        out_shape=[sds(D, F32), sds(D, BF16), sds(F, BF16), sds(F, BF16), sds(D, BF16),
                   jax.ShapeDtypeStruct((8, 3 * D), F32), jax.ShapeDtypeStruct((8, D), F32)],
        compiler_params=_params(2),
    )(x1, dx2, mo, gn, mod, w1, w2)


def _matmul_tn(a, g, name):
    R, Ka = a.shape
    Ng = g.shape[1]
    tr = next(c for c in (512, 256, 128, 64, 32, 16, 8) if R % c == 0)
    tka = Ka if Ka <= 1024 else 1024
    tn = next(c for c in (1152, 1024, 768, 512, 384, 256, 128) if Ng % c == 0) if Ng % LANES == 0 else Ng
    assert Ka % tka == 0
    nr = R // tr

    def body(a_ref, g_ref, o_ref):
        @pl.when(pl.program_id(2) == 0)
        def _():
            o_ref[...] = jnp.zeros(o_ref.shape, F32)
        o_ref[...] += _dot_tn(a_ref[...], g_ref[...])

    return pl.pallas_call(
        body, name=name, grid=(Ka // tka, Ng // tn, nr),
        in_specs=[pl.BlockSpec((tr, tka), lambda i, j, r: (r, i)), pl.BlockSpec((tr, tn), lambda i, j, r: (r, j))],
        out_specs=pl.BlockSpec((tka, tn), lambda i, j, r: (i, j)),
        out_shape=jax.ShapeDtypeStruct((Ka, Ng), F32), compiler_params=_params(3),
    )(a, g)


def _merge_bwd(cfg, dx1, mgo, oa, ob, oc, z, mod, wa, wb, wc, wo, *, ctx_active, exch=None):
    B, T, D, tm, nC = cfg.B, cfg.T, cfg.D, cfg.tm, cfg.nC
    ng = 3 * D // cfg.gw

    def body(dx_ref, mgo_ref, oa_ref, ob_ref, oc_ref, *rest):
        gate_refs = rest[:ng]
        (mod_ref, wa_ref, wb_ref, wc_ref, wo_ref,
         doa_ref, dob_ref, doc_ref, dpa_ref, dpb_ref, dpc_ref, y_ref, dmo_ref, dzg_ref, dg1_ref) = rest[ng:]
        b, t = pl.program_id(0), pl.program_id(1)
        _acc_init([dg1_ref])

        def compute():
            row = jnp.where(t < nC, B, b)
            dx = dx_ref[0]
            dg1_ref[pl.ds(row, 1), :] += jnp.sum(dx * mgo_ref[0].astype(F32), axis=0, keepdims=True)
            dmo = (dx * _mod_row(mod_ref, row, 2, D)).astype(BF16)
            dmo_ref[0] = dmo
            dy = _dot_nt(dmo, wo_ref[...])
            gates = _read_gates(cfg, gate_refs)
            y = jnp.zeros((tm, D), F32)
            dgs = []
            for gate, o_ref, w_ref, do_ref, dp_ref in ((gates[0], oa_ref, wa_ref, doa_ref, dpa_ref),
                                                      (gates[1], ob_ref, wb_ref, dob_ref, dpb_ref),
                                                      (gates[2], oc_ref, wc_ref, doc_ref, dpc_ref)):
                s = jax.nn.sigmoid(gate)
                p = _dot(o_ref[0], w_ref[...])
                y = y + s * p
                dp = (dy * s).astype(BF16)
                dp_ref[0] = dp
                do_ref[0] = _dot_nt(dp, w_ref[...]).astype(BF16)
                dgs.append((dy * p * s * (1.0 - s)).astype(BF16))
            y_ref[0] = y.astype(BF16)
            dzg_ref[0] = jnp.concatenate(dgs, axis=-1)

        if ctx_active:
            compute()
        else:
            pl.when(t >= nC)(compute)

            @pl.when(t < nC)
            def _():
                for ref in (doa_ref, dob_ref, doc_ref, dpa_ref, dpb_ref, dpc_ref, y_ref, dmo_ref, dzg_ref):
                    ref[...] = jnp.zeros(ref.shape, ref.dtype)

    row = lambda w: pl.BlockSpec((1, tm, w), lambda b, t: (b, t, 0))
    sds = lambda w: jax.ShapeDtypeStruct((B, T, w), BF16)
    return _pcall(exch)(
        body, name="merge_bwd", grid=(B, cfg.nT),
        in_specs=[row(D), row(D), row(QW), row(POOL_W), row(QW)] + _gate_specs(cfg)
        + [_full((8, 6 * D)), _full((QW, D)), _full((POOL_W, D)), _full((QW, D)), _full((D, D))],
        out_specs=[row(QW), row(POOL_W), row(QW), row(D), row(D), row(D), row(D), row(D), row(3 * D), _full((8, D))],
        out_shape=[sds(QW), sds(POOL_W), sds(QW), sds(D), sds(D), sds(D), sds(D), sds(D), sds(3 * D),
                   jax.ShapeDtypeStruct((8, D), F32)],
        compiler_params=_params(2),
    )(dx1, mgo, oa, ob, oc, *([z] * ng), mod, wa, wb, wc, wo)


def _attn_bwd(cfg, q, k, z, vblock, sink8, do, *, loc, window, sink, name, exch=None):
    B, S, N, T, tq, nC = cfg.B, cfg.S, cfg.N, cfg.T, cfg.tm, cfg.nC
    nk = T if loc else N
    nq = S if loc else N
    qoff = nC if loc else 0

    def body(q_ref, k_ref, v_ref, sink_ref, do_ref, dq_ref, dk_ref, dv_ref, dsink_ref):
        b, t = pl.program_id(0), pl.program_id(1)
        _acc_init([dsink_ref])

        @pl.when(t == 0)
        def _():
            dk_ref[...] = jnp.zeros(dk_ref.shape, F32)
            dv_ref[...] = jnp.zeros(dv_ref.shape, F32)

        q_t = q_ref[0]
        do_t = do_ref[0]
        dqs = [None] * N_QHEADS
        dk_c, dv_c, dk_l, dv_l = [], [], [], []
        dsink_row = jnp.zeros((1, LANES), F32)
        lane = lax.broadcasted_iota(jnp.int32, (1, LANES), 1)
        lo = 0
        for h in range(N_KV):
            qs, p_c, p_l, p_s, k_ctx, v_ctx, k_loc, v_loc, lo = _attn_scores(cfg, t, q_t, k_ref, v_ref, sink_ref, h, loc, window, sink)
            dos = jnp.concatenate([do_t[:, (GROUP * h + g) * HEAD:(GROUP * h + g + 1) * HEAD] for g in range(GROUP)], axis=0)
            dp_c = _dot_nt(dos, v_ctx)
            delta = jnp.sum(p_c * dp_c, axis=-1, keepdims=True)
            if loc:
                dp_l = _dot_nt(dos, v_loc)
                delta = delta + jnp.sum(p_l * dp_l, axis=-1, keepdims=True)
            ds_c = (p_c * (dp_c - delta)).astype(BF16)
            dq = _dot(ds_c, k_ctx)
            dk_c.append(_dot_tn(ds_c, qs))
            dv_c.append(_dot_tn(p_c.astype(BF16), dos))
            if loc:
                ds_l = (p_l * (dp_l - delta)).astype(BF16)
                dq = dq + _dot(ds_l, k_loc)
                dk_l.append(_dot_tn(ds_l, qs))
                dv_l.append(_dot_tn(p_l.astype(BF16), dos))
            if sink:
                dsk = -p_s * delta
                for g in range(GROUP):
                    tot = jnp.sum(dsk[g * tq:(g + 1) * tq], axis=0, keepdims=True)
                    dsink_row = dsink_row + jnp.where(lane == GROUP * h + g, tot, 0.0)
            for g in range(GROUP):
                dqs[GROUP * h + g] = dq[g * tq:(g + 1) * tq] * QSCALE
        dq_ref[0] = jnp.concatenate(dqs, axis=-1)
        dk_ref[0, 0:N, :] += jnp.concatenate(dk_c, axis=-1)
        dv_ref[0, 0:N, :] += jnp.concatenate(dv_c, axis=-1)
        if loc:
            W = tq + 2 * WINDOW if window else S
            dk_ref[0, pl.ds(N + lo, W), :] += jnp.concatenate(dk_l, axis=-1)
            dv_ref[0, pl.ds(N + lo, W), :] += jnp.concatenate(dv_l, axis=-1)
        if sink:
            dsink_ref[0:1, :] += dsink_row

    kv = pl.BlockSpec((1, nk, KVW), lambda b, t: (b, 0, 0))
    return _pcall(exch)(
        body, name=name, grid=(B, nq // tq),
        in_specs=[pl.BlockSpec((1, tq, QW), lambda b, t: (b, t + qoff, 0)), kv,
                  pl.BlockSpec((1, nk, KVW), lambda b, t: (b, 0, vblock)), _full((8, LANES)),
                  pl.BlockSpec((1, tq, QW), lambda b, t: (b, t + qoff, 0))],
        out_specs=[pl.BlockSpec((1, tq, QW), lambda b, t: (b, t, 0)), kv, kv, _full((8, LANES))],
        out_shape=[jax.ShapeDtypeStruct((B, nq, QW), F32), jax.ShapeDtypeStruct((B, nk, KVW), F32),
                   jax.ShapeDtypeStruct((B, nk, KVW), F32), jax.ShapeDtypeStruct((8, LANES), F32)],
        compiler_params=_params(2),
    )(q, k, z, sink8, do)


def _qknorm_bwd(cfg, z, gvec, cosf, sins, lat, ctx):
    B, S, N, T, tm, nC = cfg.B, cfg.S, cfg.N, cfg.T, cfg.tm, cfg.nC

    def body(z_ref, g_ref, cos_ref, sin_ref, lqa, lka, lva, lqc, lkc, lvc, cqa, cka, cva, cqc, ckc, cvc, dz_ref, dg_ref):
        t = pl.program_id(1)
        _acc_init([dg_ref])
        ind = _head_indicator()
        cos, sin = cos_ref[...], sin_ref[...]
        isctx = t < nC

        def q_grad(lat_ref, ctx_ref):
            return jnp.where(isctx, ctx_ref[0], lat_ref[0])

        def k_grad(lat_ref, ctx_ref):
            return lat_ref[0] + jnp.where(isctx, ctx_ref[0], 0.0)

        dqa, dqc = q_grad(lqa, cqa), q_grad(lqc, cqc)
        douts = {0: dqa[:, 0:128], 1: dqa[:, 128:256], 2: dqa[:, 256:384], 3: k_grad(lka, cka),
                 5: dqc[:, 0:128], 6: dqc[:, 128:256], 7: dqc[:, 256:384], 8: k_grad(lkc, ckc)}
        pieces = []
        dgs = []
        for c in range(QKV_W // LANES):
            if c not in douts:
                pieces.append(k_grad(lva, cva) if c == 4 else k_grad(lvc, cvc))
                dgs.append(jnp.zeros((1, LANES), F32))
                continue
            x = z_ref[0, :, c * LANES:(c + 1) * LANES]
            g = g_ref[:, c * LANES:(c + 1) * LANES]
            ss = jnp.dot(x * x, ind, precision=HIGHEST, preferred_element_type=F32)
            rstd = lax.rsqrt(ss * (1.0 / HEAD) + EPS)
            n = x * rstd
            dout = douts[c]
            dy = dout * cos + _pair_swap(dout * sin)
            dgs.append(jnp.sum(dy * n, axis=0, keepdims=True))
            dn = dy * g
            mean = jnp.dot(dn * n, ind, precision=HIGHEST, preferred_element_type=F32) * (1.0 / HEAD)
            pieces.append(rstd * (dn - n * mean))
        dz_ref[0] = jnp.concatenate(pieces, axis=-1).astype(BF16)
        dg_ref[0:1, :] += jnp.concatenate(dgs, axis=-1)

    row = lambda w: pl.BlockSpec((1, tm, w), lambda b, t: (b, t, 0))
    latq = pl.BlockSpec((1, tm, QW), lambda b, t: (b, jnp.maximum(t - nC, 0), 0))
    ctxq = pl.BlockSpec((1, tm, QW), lambda b, t: (b, jnp.minimum(t, nC - 1), 0))
    ctxk = pl.BlockSpec((1, tm, KVW), lambda b, t: (b, jnp.minimum(t, nC - 1), 0))
    tab = pl.BlockSpec((tm, LANES), lambda b, t: (t, 0))
    return pl.pallas_call(
        body, name="qknorm_bwd", grid=(B, cfg.nT),
        in_specs=[row(QKV_W), _full((1, QKV_W)), tab, tab,
                  latq, row(KVW), row(KVW), latq, row(KVW), row(KVW),
                  ctxq, ctxk, ctxk, ctxq, ctxk, ctxk],
        out_specs=[row(QKV_W), _full((8, QKV_W))],
        out_shape=[jax.ShapeDtypeStruct((B, T, QKV_W), BF16), jax.ShapeDtypeStruct((8, QKV_W), F32)],
        compiler_params=_params(2),
    )(z, gvec, cosf, sins, *lat, *ctx)


def _pool_bwd(cfg, dob, pooled, wp, ps):
    B, T, tm = cfg.B, cfg.T, cfg.tm

    def body(dob_ref, pooled_ref, wp_ref, ps_ref, du_ref, dwp_ref, dps_ref):
        t = pl.program_id(1)
        _acc_init([dwp_ref, dps_ref])
        r0, seg_lo, seg_hi, k0 = _pool_geometry(cfg, t)
        ps = ps_ref[...]
        wp = wp_ref[...]
        dmix = dob_ref[0, pl.ds(r0, tm), :].astype(F32)
        pooled = pooled_ref[0]
        dps_ref[0:1, :] += jnp.sum(dmix * _dot(pooled, wp), axis=0, keepdims=True)
        dpm = (dmix * ps).astype(BF16)
        dwp_ref[...] += _dot_tn(pooled, dpm)
        dpooled_t = _dot_nt(dpm, wp)
        dpm_w = (dob_ref[0, pl.ds(k0, 3 * tm), :].astype(F32) * ps).astype(BF16)
        dpooled_w = _dot_nt(dpm_w, wp)
        rr = r0 + lax.broadcasted_iota(jnp.int32, (tm, 3 * tm), 0)
        cc = k0 + lax.broadcasted_iota(jnp.int32, (tm, 3 * tm), 1)
        diff = rr - cc
        inseg = (cc >= seg_lo) & (cc < seg_hi)
        ccol = k0 + lax.broadcasted_iota(jnp.int32, (3 * tm, 1), 0)
        group = lax.broadcasted_iota(jnp.int32, (tm, POOL_W), 1) // HEAD
        acc = jnp.zeros((tm, POOL_W), F32)
        for gi, w in enumerate(POOL_WINDOWS):
            h = w // 2
            band_t = jnp.where((diff >= -h) & (diff <= h - 1) & inseg, 1.0, 0.0).astype(BF16)
            hi, lo = _split_bf16(dpooled_w / _pool_count(ccol, h, seg_lo, seg_hi))
            acc = jnp.where(group == gi, _dot(band_t, hi) + _dot(band_t, lo), acc)
        du_ref[0] = (acc - dpooled_t).astype(BF16)

    row = pl.BlockSpec((1, tm, POOL_W), lambda b, t: (b, t, 0))
    return pl.pallas_call(
        body, name="pool_bwd", grid=(B, cfg.nT),
        in_specs=[pl.BlockSpec((1, T, POOL_W), lambda b, t: (b, 0, 0)), row, _full((POOL_W, POOL_W)), _full((1, POOL_W))],
        out_specs=[row, _full((POOL_W, POOL_W)), _full((8, POOL_W))],
        out_shape=[jax.ShapeDtypeStruct((B, T, POOL_W), BF16), jax.ShapeDtypeStruct((POOL_W, POOL_W), F32),
                   jax.ShapeDtypeStruct((8, POOL_W), F32)],
        compiler_params=_params(2),
    )(dob, pooled, wp, ps)


def _in_proj_bwd(cfg, dzq, du, dzg, w_in, x, dx1, gn, mod, exch=None):
    B, T, D, IN, tm, nC = cfg.B, cfg.T, cfg.D, cfg.IN, cfg.tm, cfg.nC

    def body(dzq_ref, du_ref, dzg_ref, w_ref, x_ref, dx1_ref, gn_ref, mod_ref, dx0_ref, dz_ref, dmod_ref, dgn_ref):
        b, t = pl.program_id(0), pl.program_id(1)
        _acc_init([dmod_ref, dgn_ref])
        row = jnp.where(t < nC, B, b)
        dz = jnp.concatenate([dzq_ref[0], du_ref[0], dzg_ref[0]], axis=-1)
        dz_ref[0] = dz
        dh = _dot_nt(dz, w_ref[...])
        gn = gn_ref[...]
        scale = _mod_row(mod_ref, row, 1, D)
        _, xhat, rstd = _modulate(x_ref[0], gn, _mod_row(mod_ref, row, 0, D), scale)
        dxn, d_shift, d_scale, d_gn = _modulate_bwd(dh, xhat, rstd, gn, scale)
        dx0_ref[0] = dx1_ref[0] + dxn
        dmod_ref[pl.ds(row, 1), :] += jnp.concatenate([d_shift, d_scale], axis=-1)
        dgn_ref[0:1, :] += d_gn

    row = lambda w: pl.BlockSpec((1, tm, w), lambda b, t: (b, t, 0))
    return _pcall(exch)(
        body, name="in_proj_bwd", grid=(B, cfg.nT),
        in_specs=[row(QKV_W), row(POOL_W), row(3 * D), _full((D, IN)), row(D), row(D), _full((1, D)), _full((8, 6 * D))],
        out_specs=[row(D), row(IN), _full((8, 2 * D)), _full((8, D))],
        out_shape=[jax.ShapeDtypeStruct((B, T, D), F32), jax.ShapeDtypeStruct((B, T, IN), BF16),
                   jax.ShapeDtypeStruct((8, 2 * D), F32), jax.ShapeDtypeStruct((8, D), F32)],
        compiler_params=_params(2),
    )(dzq, du, dzg, w_in, x, dx1, gn, mod)


def _adaln_bwd(cfg, cc8, dmod, w_ada):
    d = cfg.D
    tn = 6 * d // 4

    def body(c_ref, dm_ref, w_ref, dc_ref, a_ref, dmb_ref, db_ref):
        j = pl.program_id(0)

        @pl.when(j == 0)
        def _():
            dc_ref[...] = jnp.zeros((8, d), F32)
            c = c_ref[...]
            a_ref[...] = (c * jax.nn.sigmoid(c)).astype(BF16)

        dm = dm_ref[...]
        dmb = dm.astype(BF16)
        dmb_ref[...] = dmb
        db_ref[...] = jnp.broadcast_to(jnp.sum(dm, axis=0, keepdims=True), (8, tn))
        dc_ref[...] += _dot_nt(dmb, w_ref[...])

        @pl.when(j == 3)
        def _():
            c = c_ref[...]
            s = jax.nn.sigmoid(c)
            dc_ref[...] = dc_ref[...] * (s * (1.0 + c * (1.0 - s)))

    return pl.pallas_call(
        body, name="adaln_bwd", grid=(4,),
        in_specs=[_full((8, d)), pl.BlockSpec((8, tn), lambda j: (0, j)), pl.BlockSpec((d, tn), lambda j: (0, j))],
        out_specs=[_full((8, d)), _full((8, d)), pl.BlockSpec((8, tn), lambda j: (0, j)), pl.BlockSpec((8, tn), lambda j: (0, j))],
        out_shape=[jax.ShapeDtypeStruct((8, d), F32), jax.ShapeDtypeStruct((8, d), BF16),
                   jax.ShapeDtypeStruct((8, 6 * d), BF16), jax.ShapeDtypeStruct((8, 6 * d), F32)],
        compiler_params=_params(1),
    )(cc8, dmod, w_ada)


def _peer(k):
    x, y, c = lax.axis_index("x"), lax.axis_index("y"), lax.axis_index("c")
    px = x ^ ((k >> 2) & 1)
    py = y ^ ((k >> 1) & 1)
    pc = c ^ (k & 1)
    return (px, py, pc), 4 * px + 2 * py + pc


def _exchange_copies(x_ref, out_ref, send_sems, recv_sems, local_sem, scatter):
    _, me = _peer(0)
    src_of = (lambda d: x_ref.at[d]) if scatter else (lambda d: x_ref)
    mine = pltpu.make_async_copy(src_of(me), out_ref.at[me], local_sem)
    sends, recvs = [], []
    for k in range(1, N_DEV):
        pos, idx = _peer(k)
        common = dict(send_sem=send_sems.at[k - 1], recv_sem=recv_sems.at[k - 1], device_id=pos, device_id_type=pl.DeviceIdType.MESH)
        sends.append(pltpu.make_async_remote_copy(src_ref=src_of(idx), dst_ref=out_ref.at[me], **common))
        recvs.append(pltpu.make_async_remote_copy(src_ref=src_of(me), dst_ref=out_ref.at[idx], **common))
    return mine, sends, recvs


def _exchange_start(*refs, scatter):
    mine, sends, _ = _exchange_copies(*refs, scatter)
    mine.start()
    for cp in sends:
        cp.start()


def _exchange_wait(*refs, scatter):
    mine, sends, recvs = _exchange_copies(*refs, scatter)
    for cp in recvs:
        cp.wait_recv()
    for cp in sends:
        cp.wait_send()
    mine.wait()


def _exchange_scratch():
    return [pltpu.SemaphoreType.DMA((N_DEV - 1,)), pltpu.SemaphoreType.DMA((N_DEV - 1,)), pltpu.SemaphoreType.DMA]


def _exchange_shape(x, scatter):
    return jax.ShapeDtypeStruct((N_DEV,) + tuple(x.shape[1:] if scatter else x.shape), x.dtype)


def _exchange(x, *, scatter, name):
    def body(x_ref, out_ref, send_sems, recv_sems, local_sem):
        _exchange_start(x_ref, out_ref, send_sems, recv_sems, local_sem, scatter=scatter)
        _exchange_wait(x_ref, out_ref, send_sems, recv_sems, local_sem, scatter=scatter)

    return pl.pallas_call(
        body, name=name,
        in_specs=[pl.BlockSpec(memory_space=pl.ANY)], out_specs=pl.BlockSpec(memory_space=pl.ANY),
        out_shape=_exchange_shape(x, scatter), scratch_shapes=_exchange_scratch(),
    )(x)


def _pcall(exch):
    if exch is None:
        return pl.pallas_call
    x, scatter = exch

    def make(body, *, name, grid, in_specs, out_specs, out_shape, compiler_params):
        multi = isinstance(out_shape, (list, tuple))
        out_specs_l = list(out_specs) if multi else [out_specs]
        out_shape_l = list(out_shape) if multi else [out_shape]
        n_in, n_out = len(in_specs), len(out_specs_l)

        def hosted(*refs):
            ins, x_ref = refs[:n_in], refs[n_in]
            outs, xo_ref = refs[n_in + 1:n_in + 1 + n_out], refs[n_in + 1 + n_out]
            sems = refs[n_in + 2 + n_out:]
            ids = [pl.program_id(i) for i in range(len(grid))]
            first = functools.reduce(jnp.logical_and, [i == 0 for i in ids])
            last = functools.reduce(jnp.logical_and, [i == g - 1 for i, g in zip(ids, grid)])

            @pl.when(first)
            def _():
                _exchange_start(x_ref, xo_ref, *sems, scatter=scatter)

            body(*ins, *outs)

            @pl.when(last)
            def _():
                _exchange_wait(x_ref, xo_ref, *sems, scatter=scatter)

        any_spec = pl.BlockSpec(memory_space=pl.ANY)
        call = pl.pallas_call(
            hosted, name=name, grid=grid, in_specs=list(in_specs) + [any_spec], out_specs=out_specs_l + [any_spec],
            out_shape=out_shape_l + [_exchange_shape(x, scatter)], scratch_shapes=_exchange_scratch(),
            compiler_params=compiler_params)
        return lambda *args: call(*args, x)

    return make


def _adamw(parts, w, m, v, name):
    rows = w.shape[0]
    tr = next(c for c in (256, 128, 64, 32, 16, 8) if rows % c == 0)
    bc1 = 1.0 - ADAM_B1 ** ADAM_STEP
    bc2 = 1.0 - ADAM_B2 ** ADAM_STEP

    def body(p_ref, w_ref, m_ref, v_ref, g_ref, d_ref, m2_ref, v2_ref):
        g = p_ref[0].astype(F32)
        for d in range(1, N_DEV):
            g = g + p_ref[d].astype(F32)
        m2 = ADAM_B1 * m_ref[...] + (1.0 - ADAM_B1) * g
        v2 = ADAM_B2 * v_ref[...] + (1.0 - ADAM_B2) * (g * g)
        g_ref[...] = g
        m2_ref[...] = m2
        v2_ref[...] = v2
        d_ref[...] = -ADAM_LR * ((m2 / bc1) / (jnp.sqrt(v2 / bc2) + ADAM_EPS) + ADAM_WD * w_ref[...])

    blk = pl.BlockSpec((tr, PACK_W), lambda i: (i, 0))
    return pl.pallas_call(
        body, name=name, grid=(rows // tr,),
        in_specs=[pl.BlockSpec((N_DEV, tr, PACK_W), lambda i: (0, i, 0)), blk, blk, blk],
        out_specs=[blk] * 4, out_shape=[jax.ShapeDtypeStruct((rows, PACK_W), F32)] * 4,
        compiler_params=_params(1),
    )(parts, w, m, v)


_SHARDED = dict(w_ada=True, w_in=True, w_br_a=True, w_br_b=True, w_br_c=True, w_out=False, w_mlp1=True, w_mlp2=False)
_GROUPS = (("w_ada", "w_in"), ("w_br_a", "w_br_b", "w_br_c", "w_out"), ("w_mlp1", "w_mlp2"))
_SMALL = ("c_ctx", "b_ada", "norm1", "norm2", "q_norm_a", "k_norm_a", "q_norm_c", "k_norm_c", "sink_c", "w_pool", "pool_scale")


def _pack_shards(shards, names):
    return jnp.concatenate([shards[n].reshape(-1, PACK_W) for n in names], axis=0)


def _unpack_gathered(packed, names, shard_shapes):
    out, r = {}, 0
    for n in names:
        k, w = shard_shapes[n]
        nr = k * w // PACK_W
        blk = packed[:, r:r + nr].reshape(N_DEV, k, w)
        out[n] = blk.transpose(1, 0, 2).reshape(k, N_DEV * w) if _SHARDED[n] else blk.reshape(N_DEV * k, w)
        r += nr
    return out


def _pack_full_grads(grads, names, shard_shapes):
    parts = []
    for n in names:
        k, w = shard_shapes[n]
        g = grads[n].astype(BF16)
        g = g.reshape(k, N_DEV, w).transpose(1, 0, 2) if _SHARDED[n] else g.reshape(N_DEV, k, w)
        parts.append(g.reshape(N_DEV, -1, PACK_W))
    return jnp.concatenate(parts, axis=1)


def _unpack_shards(packed, names, shard_shapes):
    out, r = {}, 0
    for n in names:
        k, w = shard_shapes[n]
        nr = k * w // PACK_W
        out[n] = packed[r:r + nr].reshape(k, w)
        r += nr
    return out


def _pack_small(vals):
    flat = jnp.concatenate([vals[n].reshape(-1) for n in _SMALL])
    rows = -(-flat.shape[0] // (8 * PACK_W)) * 8
    return jnp.pad(flat, (0, rows * PACK_W - flat.shape[0])).reshape(rows, PACK_W)


def _unpack_small(packed, like):
    flat, out, r = packed.reshape(-1), {}, 0
    for n in _SMALL:
        sz = like[n].size
        out[n] = flat[r:r + sz].reshape(like[n].shape)
        r += sz
    return out


def _rope_tables(cfg):
    pos = jnp.arange(cfg.S, dtype=F32)
    r = jnp.floor(pos / GRID_W)
    col = pos - r * GRID_W
    half = HEAD // 4
    inv = 1.0 / (ROPE_THETA ** (jnp.arange(0, HEAD // 2, 2, dtype=F32) / (HEAD // 2)))
    ang = jnp.concatenate([r[:, None] * inv, col[:, None] * inv], axis=-1)
    assert ang.shape[1] == 2 * half
    cos = jnp.repeat(jnp.cos(ang), 2, axis=-1)
    sin = jnp.repeat(jnp.sin(ang), 2, axis=-1) * jnp.tile(jnp.array([-1.0, 1.0], F32), HEAD // 2)
    cos = jnp.concatenate([jnp.ones((cfg.N, HEAD), F32), cos], axis=0)
    sin = jnp.concatenate([jnp.zeros((cfg.N, HEAD), F32), sin], axis=0)
    return jnp.tile(cos, (1, 2)), jnp.tile(sin, (1, 2))


def _gvec(qa, ka, qc, kc):
    one = jnp.ones((KVW,), F32)
    return jnp.concatenate([jnp.tile(qa, N_QHEADS), jnp.tile(ka, N_KV), one, jnp.tile(qc, N_QHEADS), jnp.tile(kc, N_KV), one])[None, :]


def _block_diag(wp):
    g, c, _ = wp.shape
    out = jnp.zeros((g * c, g * c), wp.dtype)
    for i in range(g):
        out = out.at[i * c:(i + 1) * c, i * c:(i + 1) * c].set(wp[i])
    return out


def _pad8(a):
    return jnp.pad(a, ((0, 8 - a.shape[0]), (0, 0)))


def kernel(x, c, ctx, c_ctx, w_ada, b_ada, norm1, norm2, w_in, q_norm_a, k_norm_a, q_norm_c, k_norm_c, sink_c, w_pool, pool_scale, w_br_a, w_br_b, w_br_c, w_out, w_mlp1, w_mlp2, loss_target, m_c_ctx, m_w_ada, m_b_ada, m_norm1, m_norm2, m_w_in, m_q_norm_a, m_k_norm_a, m_q_norm_c, m_k_norm_c, m_sink_c, m_w_pool, m_pool_scale, m_w_br_a, m_w_br_b, m_w_br_c, m_w_out, m_w_mlp1, m_w_mlp2, v_c_ctx, v_w_ada, v_b_ada, v_norm1, v_norm2, v_w_in, v_q_norm_a, v_k_norm_a, v_q_norm_c, v_k_norm_c, v_sink_c, v_w_pool, v_pool_scale, v_w_br_a, v_w_br_b, v_w_br_c, v_w_out, v_w_mlp1, v_w_mlp2):
    B, S, D = x.shape
    N = ctx.shape[1]
    L = w_ada.shape[0]
    cfg = _Cfg(B, S, N, D)
    T = cfg.T
    weights = dict(c_ctx=c_ctx, w_ada=w_ada, b_ada=b_ada, norm1=norm1, norm2=norm2, w_in=w_in, q_norm_a=q_norm_a,
                   k_norm_a=k_norm_a, q_norm_c=q_norm_c, k_norm_c=k_norm_c, sink_c=sink_c, w_pool=w_pool,
                   pool_scale=pool_scale, w_br_a=w_br_a, w_br_b=w_br_b, w_br_c=w_br_c, w_out=w_out, w_mlp1=w_mlp1, w_mlp2=w_mlp2)
    mom_m = dict(c_ctx=m_c_ctx, w_ada=m_w_ada, b_ada=m_b_ada, norm1=m_norm1, norm2=m_norm2, w_in=m_w_in, q_norm_a=m_q_norm_a,
                 k_norm_a=m_k_norm_a, q_norm_c=m_q_norm_c, k_norm_c=m_k_norm_c, sink_c=m_sink_c, w_pool=m_w_pool,
                 pool_scale=m_pool_scale, w_br_a=m_w_br_a, w_br_b=m_w_br_b, w_br_c=m_w_br_c, w_out=m_w_out, w_mlp1=m_w_mlp1, w_mlp2=m_w_mlp2)
    mom_v = dict(c_ctx=v_c_ctx, w_ada=v_w_ada, b_ada=v_b_ada, norm1=v_norm1, norm2=v_norm2, w_in=v_w_in, q_norm_a=v_q_norm_a,
                 k_norm_a=v_k_norm_a, q_norm_c=v_q_norm_c, k_norm_c=v_k_norm_c, sink_c=v_sink_c, w_pool=v_w_pool,
                 pool_scale=v_pool_scale, w_br_a=v_w_br_a, w_br_b=v_w_br_b, w_br_c=v_w_br_c, w_out=v_w_out, w_mlp1=v_w_mlp1, w_mlp2=v_w_mlp2)
    shard_shapes = {n: weights[n].shape[1:] for n in _SHARDED}
    units = [(l, gi) for l in range(L) for gi in range(len(_GROUPS))]
    pack_local = lambda src, l, gi: _pack_shards({n: src[n][l] for n in _GROUPS[gi]}, _GROUPS[gi])
    packed_w = {u: pack_local(weights, *u) for u in units}

    full = [dict() for _ in range(L)]

    def gather_spec(l, gi):
        return (packed_w[(l, gi)].astype(BF16), False)

    def gathered(l, gi, arr):
        full[l].update(_unpack_gathered(arr, _GROUPS[gi], shard_shapes))

    gathered(0, 0, _exchange(gather_spec(0, 0)[0], scatter=False, name="gather_first_weights"))

    cosf, sins = _rope_tables(cfg)
    xs = jnp.concatenate([ctx, x], axis=1)
    cc8 = _pad8(jnp.concatenate([c, c_ctx[None, :]], axis=0))
    zeros_q = jnp.zeros((B, N, QW), F32)
    zeros_k = jnp.zeros((B, N, KVW), F32)
    va_blk, vc_blk = (QW + KVW) // KVW, (2 * QW + 3 * KVW) // KVW

    def hosting(fn, *a, spec=None, done=None, **kw):
        if spec is None:
            return fn(*a, **kw)
        res = fn(*a, exch=spec, **kw)
        done(res[-1])
        return res[0] if len(res) == 2 else res[:-1]

    def gather_behind(l, gi):
        if l >= L:
            return {}
        return dict(spec=gather_spec(l, gi), done=functools.partial(gathered, l, gi))

    saved = []
    for l in range(L):
        fw = full[l]
        ctx_active = l < L - 1
        mod = _adaln_fwd(cfg, cc8, fw["w_ada"], b_ada[l][None, :])
        z, h = hosting(_in_proj_fwd, cfg, xs, norm1[l][None, :], mod, fw["w_in"], **(gather_behind(0, 1) if l == 0 else {}))
        gvec = _gvec(q_norm_a[l], k_norm_a[l], q_norm_c[l], k_norm_c[l])
        qa, ka, qc, kc = _qknorm_fwd(cfg, z, gvec, cosf, sins)
        sink8 = jnp.zeros((8, LANES), F32).at[0, :N_QHEADS].set(sink_c[l])
        oa_l = hosting(_attn_fwd, cfg, qa, ka, z, va_blk, sink8, loc=True, window=False, sink=False, name="attn_a_fwd",
                       **gather_behind(l + 1, 0))
        oc_l = hosting(_attn_fwd, cfg, qc, kc, z, vc_blk, sink8, loc=True, window=True, sink=True, name="attn_c_fwd",
                       **(gather_behind(0, 2) if l == 0 else {}))
        if ctx_active:
            oa_c = _attn_fwd(cfg, qa, ka, z, va_blk, sink8, loc=False, window=False, sink=False, name="attn_a_ctx_fwd")
            oc_c = _attn_fwd(cfg, qc, kc, z, vc_blk, sink8, loc=False, window=False, sink=True, name="attn_c_ctx_fwd")
        else:
            oa_c = oc_c = jnp.zeros((B, N, QW), BF16)
        oa = jnp.concatenate([oa_c, oa_l], axis=1)
        oc = jnp.concatenate([oc_c, oc_l], axis=1)
        wp = _block_diag(w_pool[l]).astype(BF16)
        ps = pool_scale[l][None, :]
        ob, pooled = _pool_fwd(cfg, z, wp, ps)
        x1, mgo = hosting(_merge_fwd, cfg, xs, oa, ob, oc, z, mod, fw["w_br_a"], fw["w_br_b"], fw["w_br_c"], fw["w_out"],
                          ctx_active=ctx_active, **gather_behind(l + 1, 1))
        x2, mo = hosting(_mlp_fwd, cfg, x1, norm2[l][None, :], mod, fw["w_mlp1"], fw["w_mlp2"], ctx_active=ctx_active,
                         **gather_behind(l + 1, 2))
        saved.append(dict(xs=xs, mod=mod, z=z, h=h, gvec=gvec, qa=qa, ka=ka, qc=qc, kc=kc, sink8=sink8, oa=oa, oc=oc, ob=ob,
                          pooled=pooled, wp=wp, ps=ps, x1=x1, mgo=mgo, mo=mo))
        xs = x2

    dxs, sse = _loss_fwd_bwd(cfg, xs, loss_target)
    loss = lax.psum(0.5 * sse[0, 0] / D, ("x", "y", "c"))

    grads_full = [None] * L
    small = {n: [None] * L for n in _SMALL if n != "c_ctx"}
    d_c_ctx = jnp.zeros((D,), F32)
    flat2 = lambda a: a.reshape(B * T, a.shape[-1])
    parts = {}

    def scatter_behind(l, gi):
        if l >= L:
            return {}
        spec = (_pack_full_grads(grads_full[l], _GROUPS[gi], shard_shapes), True)
        return dict(spec=spec, done=functools.partial(parts.__setitem__, (l, gi)))

    for l in reversed(range(L)):
        fw, sv = full[l], saved[l]
        ctx_active = l < L - 1
        mod = sv["mod"]
        dx1, h2, r, da, dout, dmod_mlp, dgn2 = hosting(_mlp_bwd, cfg, sv["x1"], dxs, sv["mo"], norm2[l][None, :], mod, fw["w_mlp1"], fw["w_mlp2"],
                                                       ctx_active=ctx_active, **scatter_behind(l + 1, 2))
        g = {}
        grads_full[l] = g
        g["w_mlp1"] = _matmul_tn(flat2(h2), flat2(da), "dw_mlp1")
        g["w_mlp2"] = _matmul_tn(flat2(r), flat2(dout), "dw_mlp2")
        doa, dob, doc, dpa, dpb, dpc, y, dmo, dzg, dg1 = hosting(
            _merge_bwd, cfg, dx1, sv["mgo"], sv["oa"], sv["ob"], sv["oc"], sv["z"], mod, fw["w_br_a"], fw["w_br_b"], fw["w_br_c"], fw["w_out"],
            ctx_active=ctx_active, **scatter_behind(l + 1, 1))
        g["w_out"] = _matmul_tn(flat2(y), flat2(dmo), "dw_out")
        g["w_br_a"] = _matmul_tn(flat2(sv["oa"]), flat2(dpa), "dw_br_a")
        g["w_br_b"] = _matmul_tn(flat2(sv["ob"]), flat2(dpb), "dw_br_b")
        g["w_br_c"] = _matmul_tn(flat2(sv["oc"]), flat2(dpc), "dw_br_c")
        z = sv["z"]
        dqa, dka, dva, _ = hosting(_attn_bwd, cfg, sv["qa"], sv["ka"], z, va_blk, sv["sink8"], doa, loc=True, window=False, sink=False,
                                   name="attn_a_bwd", **scatter_behind(l + 1, 0))
        dqc, dkc, dvc, dsink = hosting(_attn_bwd, cfg, sv["qc"], sv["kc"], z, vc_blk, sv["sink8"], doc, loc=True, window=True, sink=True,
                                       name="attn_c_bwd", **(scatter_behind(0, 2) if l == 0 else {}))
        if ctx_active:
            cqa, cka, cva, _ = _attn_bwd(cfg, sv["qa"], sv["ka"], z, va_blk, sv["sink8"], doa, loc=False, window=False, sink=False, name="attn_a_ctx_bwd")
            cqc, ckc, cvc, dsink_c = _attn_bwd(cfg, sv["qc"], sv["kc"], z, vc_blk, sv["sink8"], doc, loc=False, window=False, sink=True, name="attn_c_ctx_bwd")
            dsink = dsink + dsink_c
        else:
            cqa, cka, cva, cqc, ckc, cvc = zeros_q, zeros_k, zeros_k, zeros_q, zeros_k, zeros_k
        dzq, dgvec = _qknorm_bwd(cfg, z, sv["gvec"], cosf, sins, (dqa, dka, dva, dqc, dkc, dvc), (cqa, cka, cva, cqc, ckc, cvc))
        du, dwp, dps = _pool_bwd(cfg, dob, sv["pooled"], sv["wp"], sv["ps"])
        dxs, dz, dmod_in, dgn1 = hosting(_in_proj_bwd, cfg, dzq, du, dzg, fw["w_in"], sv["xs"], dx1, norm1[l][None, :], mod,
                                         **(scatter_behind(0, 1) if l == 0 else {}))
        g["w_in"] = _matmul_tn(flat2(sv["h"]), flat2(dz), "dw_in")
        dmod = jnp.concatenate([dmod_in, dg1, dmod_mlp], axis=-1)
        dcc, act, dmod_b, dbias = _adaln_bwd(cfg, cc8, dmod, fw["w_ada"])
        g["w_ada"] = _matmul_tn(act, dmod_b, "dw_ada")
        d_c_ctx = d_c_ctx + dcc[B]
        gv = dgvec[0]
        heads = lambda v, n: v.reshape(n, HEAD).sum(axis=0)
        small["b_ada"][l] = dbias[0]
        small["norm1"][l] = dgn1[0]
        small["norm2"][l] = dgn2[0]
        small["q_norm_a"][l] = heads(gv[0:QW], N_QHEADS)
        small["k_norm_a"][l] = heads(gv[QW:QW + KVW], N_KV)
        small["q_norm_c"][l] = heads(gv[QW + 2 * KVW:2 * QW + 2 * KVW], N_QHEADS)
        small["k_norm_c"][l] = heads(gv[2 * QW + 2 * KVW:2 * QW + 3 * KVW], N_KV)
        small["sink_c"][l] = dsink[0, :N_QHEADS]
        small["w_pool"][l] = jnp.stack([dwp[i * HEAD:(i + 1) * HEAD, i * HEAD:(i + 1) * HEAD] for i in range(len(POOL_WINDOWS))])
        small["pool_scale"][l] = dps[0]
    grad_x = dxs[:, N:, :]

    parts[(0, 0)] = _exchange(_pack_full_grads(grads_full[0], _GROUPS[0], shard_shapes), scatter=True, name="scatter_last_grads")
    stepped = {u: _adamw(parts[u], packed_w[u], pack_local(mom_m, *u), pack_local(mom_v, *u), "adamw_l%d_g%d" % u) for u in units}

    small_vals = {n: jnp.stack(v) for n, v in small.items()}
    small_vals["c_ctx"] = d_c_ctx
    small_parts = _exchange(_pack_small(small_vals), scatter=False, name="gather_small_grads")
    g_sm, d_sm, m_sm, v_sm = _adamw(small_parts, _pack_small(weights), _pack_small(mom_m), _pack_small(mom_v), "adamw_small")

    outs = {}
    for i, (tag, sm) in enumerate((("grad", g_sm), ("delta", d_sm), ("new_m", m_sm), ("new_v", v_sm))):
        per_layer = [dict() for _ in range(L)]
        for l, gi in units:
            per_layer[l].update(_unpack_shards(stepped[(l, gi)][i], _GROUPS[gi], shard_shapes))
        res = {n: jnp.stack([per_layer[l][n] for l in range(L)]) for n in _SHARDED}
        res.update(_unpack_small(sm, weights))
        outs[tag] = res
    order = ("c_ctx", "w_ada", "b_ada", "norm1", "norm2", "w_in", "q_norm_a", "k_norm_a", "q_norm_c", "k_norm_c", "sink_c",
             "w_pool", "pool_scale", "w_br_a", "w_br_b", "w_br_c", "w_out", "w_mlp1", "w_mlp2")
    return (loss, grad_x, *[outs[tag][n] for tag in ("grad", "delta", "new_m", "new_v") for n in order])
```

```python
import functools

import jax
import jax.numpy as jnp
from jax import lax
from jax.experimental import pallas as pl
from jax.experimental.pallas import tpu as pltpu

F32 = jnp.float32
BF16 = jnp.bfloat16
HIGHEST = lax.Precision.HIGHEST

N_DEV = 8
HEAD = 64
N_QHEADS = 6
N_KV = 2
GROUP = 3
QW = N_QHEADS * HEAD
KVW = N_KV * HEAD
QKV_W = 2 * (QW + 2 * KVW)
POOL_W = 256
POOL_WINDOWS = (2, 4, 8, 16)
GATE0 = QKV_W + POOL_W
WINDOW = 128
GRID_W = 64
ROPE_THETA = 10000.0
EPS = 1e-6
NEG = -1e30
QSCALE = HEAD ** -0.5
LANES = 128
PACK_W = 1024
VMEM_LIMIT = 56 * 1024 * 1024

ADAM_LR = 0.001
ADAM_B1 = 0.9
ADAM_B2 = 0.999
ADAM_EPS = 1e-08
ADAM_WD = 0.01
ADAM_STEP = 10

NT_DIMS = (((1,), (1,)), ((), ()))
TN_DIMS = (((0,), (0,)), ((), ()))


def _dot(a, b):
    return jnp.dot(a, b, preferred_element_type=F32)


def _dot_nt(a, b):
    return lax.dot_general(a, b, NT_DIMS, preferred_element_type=F32)


def _dot_tn(a, b):
    return lax.dot_general(a, b, TN_DIMS, preferred_element_type=F32)


def _params(n_grid):
    return pltpu.CompilerParams(dimension_semantics=("arbitrary",) * n_grid, vmem_limit_bytes=VMEM_LIMIT)


def _full(shape):
    nd = len(shape)
    return pl.BlockSpec(shape, lambda *_: (0,) * nd)


def _layer(l, width):
    return pl.BlockSpec((1, 1, width), lambda *_: (l, 0, 0))


def _modulate(x, gn, shift, scale):
    rstd = lax.rsqrt(jnp.mean(x * x, axis=-1, keepdims=True) + EPS)
    xhat = x * rstd
    return xhat * gn * (1.0 + scale) + shift, xhat, rstd


def _modulate_bwd(dh, xhat, rstd, gn, scale):
    d_shift = jnp.sum(dh, axis=0, keepdims=True)
    d_scale = jnp.sum(dh * xhat * gn, axis=0, keepdims=True)
    dy = dh * (1.0 + scale)
    d_gn = jnp.sum(dy * xhat, axis=0, keepdims=True)
    dxh = dy * gn
    dx = rstd * (dxh - xhat * jnp.mean(dxh * xhat, axis=-1, keepdims=True))
    return dx, d_shift, d_scale, d_gn


def _mod_row(mod_ref, row, k, d):
    return mod_ref[pl.ds(row, 1), k * d:(k + 1) * d]


class _Cfg:
    def __init__(self, b, s, n, d):
        self.B, self.S, self.N, self.D = b, s, n, d
        self.T = n + s
        self.F = 4 * d
        self.IN = GATE0 + 3 * d
        self.tm = 256 if (n % 256 == 0 and s % 256 == 0) else 128
        self.nT = self.T // self.tm
        self.nC = n // self.tm
        self.gw = 512 if d % 512 == 0 else 256
        assert GATE0 % self.gw == 0 and d % self.gw == 0 and b < 8 and self.T >= 3 * self.tm
        assert s % GRID_W == 0 and n % self.tm == 0 and s % self.tm == 0 and s >= self.tm + 2 * WINDOW
        assert d % (N_DEV * LANES) == 0


def _peer(k):
    x, y, c = lax.axis_index("x"), lax.axis_index("y"), lax.axis_index("c")
    px = x ^ ((k >> 2) & 1)
    py = y ^ ((k >> 1) & 1)
    pc = c ^ (k & 1)
    return (px, py, pc), 4 * px + 2 * py + pc


class _Exchange:
    def __init__(self, arrays, scatter):
        self.arrays = [a if isinstance(a, tuple) else (a, None) for a in arrays]
        self.scatter = scatter
        self.n = len(self.arrays)

    def operands(self):
        return [a for a, _ in self.arrays]

    def out_shapes(self):
        res = []
        for a, layer in self.arrays:
            shape = a.shape[1:] if (self.scatter or layer is not None) else a.shape
            res.append(jax.ShapeDtypeStruct((N_DEV,) + tuple(shape), a.dtype))
        return res

    def scratch(self):
        n = self.n * (N_DEV - 1)
        return [pltpu.SemaphoreType.DMA((n,)), pltpu.SemaphoreType.DMA((n,)), pltpu.SemaphoreType.DMA((self.n,))]

    def _copies(self, x_refs, out_refs, send_sems, recv_sems, local_sems, want):
        _, me = _peer(0)
        res = []
        for i, ((_, layer), x_ref, out_ref) in enumerate(zip(self.arrays, x_refs, out_refs)):
            if self.scatter:
                src_of = lambda d, x_ref=x_ref: x_ref.at[d]
            elif layer is not None:
                src_of = lambda d, x_ref=x_ref, layer=layer: x_ref.at[layer]
            else:
                src_of = lambda d, x_ref=x_ref: x_ref
            if want == "local":
                res.append(pltpu.make_async_copy(src_of(me), out_ref.at[me], local_sems.at[i]))
                continue
            for k in range(1, N_DEV):
                pos, idx = _peer(k)
                j = i * (N_DEV - 1) + k - 1
                common = dict(send_sem=send_sems.at[j], recv_sem=recv_sems.at[j], device_id=pos, device_id_type=pl.DeviceIdType.MESH)
                if want == "send":
                    res.append(pltpu.make_async_remote_copy(src_ref=src_of(idx), dst_ref=out_ref.at[me], **common))
                else:
                    res.append(pltpu.make_async_remote_copy(src_ref=src_of(me), dst_ref=out_ref.at[idx], **common))
        return res

    def start(self, *refs):
        for cp in self._copies(*refs, "local") + self._copies(*refs, "send"):
            cp.start()

    def wait(self, *refs):
        for cp in self._copies(*refs, "recv"):
            cp.wait_recv()
        for cp in self._copies(*refs, "send"):
            cp.wait_send()
        for cp in self._copies(*refs, "local"):
            cp.wait()

    def alone(self, name):
        n = self.n

        def body(*refs):
            args = (refs[:n], refs[n:2 * n], *refs[2 * n:])
            self.start(*args)
            self.wait(*args)

        any_spec = pl.BlockSpec(memory_space=pl.ANY)
        return pl.pallas_call(body, name=name, in_specs=[any_spec] * n, out_specs=[any_spec] * n,
                              out_shape=self.out_shapes(), scratch_shapes=self.scratch())(*self.operands())


def _pcall(exch):
    if exch is None:
        return pl.pallas_call

    def make(body, *, name, grid, in_specs, out_specs, out_shape, compiler_params, scratch_shapes=()):
        multi = isinstance(out_shape, (list, tuple))
        out_specs_l = list(out_specs) if multi else [out_specs]
        out_shape_l = list(out_shape) if multi else [out_shape]
        n_in, n_out, n_x, n_s = len(in_specs), len(out_specs_l), exch.n, len(scratch_shapes)

        def hosted(*refs):
            ins, x_refs = refs[:n_in], refs[n_in:n_in + n_x]
            o0 = n_in + n_x
            outs, xo_refs = refs[o0:o0 + n_out], refs[o0 + n_out:o0 + n_out + n_x]
            s0 = o0 + n_out + n_x
            own_scratch, sems = refs[s0:s0 + n_s], refs[s0 + n_s:]
            ids = [pl.program_id(i) for i in range(len(grid))]
            first = functools.reduce(jnp.logical_and, [i == 0 for i in ids])
            last = functools.reduce(jnp.logical_and, [i == g - 1 for i, g in zip(ids, grid)])

            @pl.when(first)
            def _():
                exch.start(x_refs, xo_refs, *sems)

            body(*ins, *outs, *own_scratch)

            @pl.when(last)
            def _():
                exch.wait(x_refs, xo_refs, *sems)

        any_spec = pl.BlockSpec(memory_space=pl.ANY)
        call = pl.pallas_call(
            hosted, name=name, grid=grid, in_specs=list(in_specs) + [any_spec] * n_x, out_specs=out_specs_l + [any_spec] * n_x,
            out_shape=out_shape_l + exch.out_shapes(), scratch_shapes=list(scratch_shapes) + exch.scratch(),
            compiler_params=compiler_params)
        return lambda *args: call(*args, *exch.operands())

    return make


def _adaln_fwd(cfg, l, cc8, w_ada, b_ada):
    d = cfg.D
    wa = 6 * d // N_DEV

    def body(c_ref, w_ref, b_ref, o_ref):
        c = c_ref[...]
        a = (c * jax.nn.sigmoid(c)).astype(BF16)
        for j in range(N_DEV):
            o_ref[:, j * wa:(j + 1) * wa] = _dot(a, w_ref[j]) + b_ref[0, :, j * wa:(j + 1) * wa]

    return pl.pallas_call(
        body, name="adaln_fwd", grid=(1,),
        in_specs=[_full((8, d)), _full((N_DEV, d, wa)), _layer(l, 6 * d)],
        out_specs=_full((8, 6 * d)),
        out_shape=jax.ShapeDtypeStruct((8, 6 * d), F32), compiler_params=_params(1),
    )(cc8, w_ada, b_ada)


def _in_proj_fwd(cfg, l, x, gn, mod, w_in, exch=None):
    B, T, D, IN, tm, nC = cfg.B, cfg.T, cfg.D, cfg.IN, cfg.tm, cfg.nC

    def body(x_ref, gn_ref, mod_ref, w_ref, z_ref, h_ref):
        b, t = pl.program_id(0), pl.program_id(1)
        row = jnp.where(t < nC, B, b)
        h, _, _ = _modulate(x_ref[0], gn_ref[0], _mod_row(mod_ref, row, 0, D), _mod_row(mod_ref, row, 1, D))
        hb = h.astype(BF16)
        h_ref[0] = hb
        z_ref[0] = _dot(hb, w_ref[...])

    return _pcall(exch)(
        body, name="in_proj_fwd", grid=(B, cfg.nT),
        in_specs=[pl.BlockSpec((1, tm, D), lambda b, t: (b, t, 0)), _layer(l, D), _full((8, 6 * D)), _full((D, IN))],
        out_specs=[pl.BlockSpec((1, tm, IN), lambda b, t: (b, t, 0)), pl.BlockSpec((1, tm, D), lambda b, t: (b, t, 0))],
        out_shape=[jax.ShapeDtypeStruct((B, T, IN), F32), jax.ShapeDtypeStruct((B, T, D), BF16)],
        compiler_params=_params(2),
    )(x, gn, mod, w_in)


def _head_indicator():
    r = lax.broadcasted_iota(jnp.int32, (LANES, LANES), 0) // HEAD
    c = lax.broadcasted_iota(jnp.int32, (LANES, LANES), 1) // HEAD
    return jnp.where(r == c, 1.0, 0.0).astype(F32)


def _pair_swap(y):
    lane = lax.broadcasted_iota(jnp.int32, y.shape, 1)
    return jnp.where(lane % 2 == 0, pltpu.roll(y, LANES - 1, 1), pltpu.roll(y, 1, 1))


_QK_CHUNKS = (0, 1, 2, 3, 5, 6, 7, 8)
_Q_CHUNKS = (0, 1, 2, 5, 6, 7)


def _qknorm_fwd(cfg, l, z, gvec, cosf, sins):
    B, T, tm = cfg.B, cfg.T, cfg.tm

    def body(z_ref, g_ref, cos_ref, sin_ref, qa_ref, ka_ref, qc_ref, kc_ref):
        ind = _head_indicator()
        cos, sin = cos_ref[...], sin_ref[...]

        def chunk(c):
            x = z_ref[0, :, c * LANES:(c + 1) * LANES]
            ss = jnp.dot(x * x, ind, precision=HIGHEST, preferred_element_type=F32)
            y = x * lax.rsqrt(ss * (1.0 / HEAD) + EPS) * g_ref[0, :, c * LANES:(c + 1) * LANES]
            out = y * cos + _pair_swap(y) * sin
            return (out * QSCALE if c in _Q_CHUNKS else out).astype(BF16)

        qa_ref[0] = jnp.concatenate([chunk(0), chunk(1), chunk(2)], axis=-1)
        ka_ref[0] = chunk(3)
        qc_ref[0] = jnp.concatenate([chunk(5), chunk(6), chunk(7)], axis=-1)
        kc_ref[0] = chunk(8)

    row = lambda w: pl.BlockSpec((1, tm, w), lambda b, t: (b, t, 0))
    tab = pl.BlockSpec((tm, LANES), lambda b, t: (t, 0))
    return pl.pallas_call(
        body, name="qknorm_fwd", grid=(B, cfg.nT),
        in_specs=[row(QKV_W), _layer(l, QKV_W), tab, tab],
        out_specs=[row(QW), row(KVW), row(QW), row(KVW)],
        out_shape=[jax.ShapeDtypeStruct((B, T, w), BF16) for w in (QW, KVW, QW, KVW)],
        compiler_params=_params(2),
    )(z, gvec, cosf, sins)


def _attn_scores(cfg, tl, q, k_ref, v_ref, sink_ref, h, loc, window, sink):
    S, N, tq = cfg.S, cfg.N, cfg.tm
    hs = slice(h * HEAD, (h + 1) * HEAD)
    qs = jnp.concatenate([q[:, (GROUP * h + g) * HEAD:(GROUP * h + g + 1) * HEAD] for g in range(GROUP)], axis=0)
    k_ctx = k_ref[0, 0:N, :][:, hs]
    v_ctx = v_ref[0, 0:N, :].astype(BF16)[:, hs]
    s_c = _dot_nt(qs, k_ctx)
    m = jnp.max(s_c, axis=-1, keepdims=True)
    k_loc = v_loc = s_l = lo = None
    if loc:
        if window:
            W = tq + 2 * WINDOW
            lo = pl.multiple_of(jnp.clip(tl * tq - WINDOW, 0, S - W), LANES)
        else:
            W, lo = S, 0
        k_loc = k_ref[0, pl.ds(N + lo, W), :][:, hs]
        v_loc = v_ref[0, pl.ds(N + lo, W), :].astype(BF16)[:, hs]
        s_l = _dot_nt(qs, k_loc)
        if window:
            qpos = tl * tq + lax.broadcasted_iota(jnp.int32, (GROUP * tq, W), 0) % tq
            kpos = lo + lax.broadcasted_iota(jnp.int32, (GROUP * tq, W), 1)
            s_l = jnp.where(jnp.abs(qpos - kpos) <= WINDOW, s_l, NEG)
        m = jnp.maximum(m, jnp.max(s_l, axis=-1, keepdims=True))
    if sink:
        rowg = lax.broadcasted_iota(jnp.int32, (GROUP * tq, 1), 0) // tq
        sk = jnp.zeros((GROUP * tq, 1), F32)
        for g in range(GROUP):
            j = GROUP * h + g
            sk = jnp.where(rowg == g, sink_ref[0, 0:1, j:j + 1], sk)
        m = jnp.maximum(m, sk)
    e_c = jnp.exp(s_c - m)
    l = jnp.sum(e_c, axis=-1, keepdims=True)
    e_l = e_s = None
    if loc:
        e_l = jnp.exp(s_l - m)
        l = l + jnp.sum(e_l, axis=-1, keepdims=True)
    if sink:
        e_s = jnp.exp(sk - m)
        l = l + e_s
    inv = 1.0 / l
    p_c = e_c * inv
    p_l = e_l * inv if loc else None
    p_s = e_s * inv if sink else None
    return qs, p_c, p_l, p_s, k_ctx, v_ctx, k_loc, v_loc, lo


def _attn_fwd(cfg, l, q, k, z, vblock, sink8, *, window, sink, ctx_attend, name, exch=None):
    B, T, tq, nC = cfg.B, cfg.T, cfg.tm, cfg.nC

    def body(q_ref, k_ref, v_ref, sink_ref, o_ref):
        t = pl.program_id(1)

        def run(loc):
            q_t = q_ref[0]
            outs = [None] * N_QHEADS
            for h in range(N_KV):
                _, p_c, p_l, _, _, v_ctx, _, v_loc, _ = _attn_scores(cfg, t - nC, q_t, k_ref, v_ref, sink_ref, h, loc, window and loc, sink)
                o = _dot(p_c.astype(BF16), v_ctx)
                if loc:
                    o = o + _dot(p_l.astype(BF16), v_loc)
                for g in range(GROUP):
                    outs[GROUP * h + g] = o[g * tq:(g + 1) * tq]
            o_ref[0] = jnp.concatenate(outs, axis=-1).astype(BF16)

        pl.when(t >= nC)(functools.partial(run, True))
        if ctx_attend:
            pl.when(t < nC)(functools.partial(run, False))
        else:
            @pl.when(t < nC)
            def _():
                o_ref[0] = jnp.zeros((tq, QW), BF16)

    return _pcall(exch)(
        body, name=name, grid=(B, cfg.nT),
        in_specs=[pl.BlockSpec((1, tq, QW), lambda b, t: (b, t, 0)),
                  pl.BlockSpec((1, T, KVW), lambda b, t: (b, 0, 0)),
                  pl.BlockSpec((1, T, KVW), lambda b, t: (b, 0, vblock)),
                  pl.BlockSpec((1, 8, LANES), lambda b, t: (l, 0, 0))],
        out_specs=pl.BlockSpec((1, tq, QW), lambda b, t: (b, t, 0)),
        out_shape=jax.ShapeDtypeStruct((B, T, QW), BF16), compiler_params=_params(2),
    )(q, k, z, sink8)


def _pool_geometry(cfg, t):
    tm, N, T, nC = cfg.tm, cfg.N, cfg.T, cfg.nC
    r0 = pl.multiple_of(t * tm, tm)
    isctx = t < nC
    seg_lo = jnp.where(isctx, 0, N)
    seg_hi = jnp.where(isctx, N, T)
    k0 = pl.multiple_of(jnp.clip(t * tm - tm, 0, T - 3 * tm), tm)
    return r0, seg_lo, seg_hi, k0


def _pool_count(pos, h, seg_lo, seg_hi):
    return jnp.maximum(jnp.minimum(pos + h, seg_hi) - jnp.maximum(pos - h, seg_lo), 1).astype(F32)


def _split_bf16(x):
    hi = x.astype(BF16)
    return hi, (x - hi.astype(F32)).astype(BF16)


def _pool_fwd(cfg, l, z, wp, ps):
    B, T, tm = cfg.B, cfg.T, cfg.tm

    def body(u_ref, wp_ref, ps_ref, ob_ref, pooled_ref):
        t = pl.program_id(1)
        r0, seg_lo, seg_hi, k0 = _pool_geometry(cfg, t)
        hi, lo = _split_bf16(u_ref[0, pl.ds(k0, 3 * tm), :])
        rr = r0 + lax.broadcasted_iota(jnp.int32, (tm, 3 * tm), 0)
        cc = k0 + lax.broadcasted_iota(jnp.int32, (tm, 3 * tm), 1)
        diff = cc - rr
        inseg = (cc >= seg_lo) & (cc < seg_hi)
        rcol = r0 + lax.broadcasted_iota(jnp.int32, (tm, 1), 0)
        group = lax.broadcasted_iota(jnp.int32, (tm, POOL_W), 1) // HEAD
        acc = jnp.zeros((tm, POOL_W), F32)
        for gi, w in enumerate(POOL_WINDOWS):
            h = w // 2
            band = jnp.where((diff >= -h) & (diff <= h - 1) & inseg, 1.0, 0.0).astype(BF16)
            tot = _dot(band, hi) + _dot(band, lo)
            acc = jnp.where(group == gi, tot / _pool_count(rcol, h, seg_lo, seg_hi), acc)
        pooled = (acc - u_ref[0, pl.ds(r0, tm), :]).astype(BF16)
        pooled_ref[0] = pooled
        ob_ref[0] = (_dot(pooled, wp_ref[0]) * ps_ref[0]).astype(BF16)

    row = pl.BlockSpec((1, tm, POOL_W), lambda b, t: (b, t, 0))
    return pl.pallas_call(
        body, name="pool_fwd", grid=(B, cfg.nT),
        in_specs=[pl.BlockSpec((1, T, POOL_W), lambda b, t: (b, 0, QKV_W // POOL_W)),
                  pl.BlockSpec((1, POOL_W, POOL_W), lambda b, t: (l, 0, 0)), _layer(l, POOL_W)],
        out_specs=[row, row],
        out_shape=[jax.ShapeDtypeStruct((B, T, POOL_W), BF16)] * 2, compiler_params=_params(2),
    )(z, wp, ps)


def _gate_specs(cfg):
    tm, gw = cfg.tm, cfg.gw
    first = GATE0 // gw
    return [pl.BlockSpec((1, tm, gw), functools.partial(lambda b, t, j: (b, t, j), j=first + i)) for i in range(3 * cfg.D // gw)]


def _read_gates(cfg, gate_refs):
    per = cfg.D // cfg.gw
    return [jnp.concatenate([gate_refs[k * per + i][0] for i in range(per)], axis=-1) for k in range(3)]


def _merge_fwd(cfg, x, oa, ob, oc, z, mod, wa, wb, wc, wo, *, ctx_active, exch=None):
    B, T, D, tm, nC = cfg.B, cfg.T, cfg.D, cfg.tm, cfg.nC
    ng = 3 * D // cfg.gw

    def body(x_ref, oa_ref, ob_ref, oc_ref, *rest):
        gate_refs = rest[:ng]
        mod_ref, wa_ref, wb_ref, wc_ref, wo_ref, x1_ref, mgo_ref = rest[ng:]
        b, t = pl.program_id(0), pl.program_id(1)

        def compute():
            row = jnp.where(t < nC, B, b)
            ga, gb, gc = _read_gates(cfg, gate_refs)
            y = (jax.nn.sigmoid(ga) * _dot(oa_ref[0], wa_ref[...])
                 + jax.nn.sigmoid(gb) * _dot(ob_ref[0], wb_ref[...])
                 + jax.nn.sigmoid(gc) * _dot(oc_ref[0], wc_ref[...]))
            mo = _dot(y.astype(BF16), wo_ref[...])
            mgo_ref[0] = mo.astype(BF16)
            x1_ref[0] = x_ref[0] + _mod_row(mod_ref, row, 2, D) * mo

        if ctx_active:
            compute()
        else:
            pl.when(t >= nC)(compute)

            @pl.when(t < nC)
            def _():
                mgo_ref[0] = jnp.zeros((tm, D), BF16)
                x1_ref[0] = x_ref[0]

    row = lambda w: pl.BlockSpec((1, tm, w), lambda b, t: (b, t, 0))
    return _pcall(exch)(
        body, name="merge_fwd", grid=(B, cfg.nT),
        in_specs=[row(D), row(QW), row(POOL_W), row(QW)] + _gate_specs(cfg)
        + [_full((8, 6 * D)), _full((QW, D)), _full((POOL_W, D)), _full((QW, D)), _full((D, D))],
        out_specs=[row(D), row(D)],
        out_shape=[jax.ShapeDtypeStruct((B, T, D), F32), jax.ShapeDtypeStruct((B, T, D), BF16)],
        compiler_params=_params(2),
    )(x, oa, ob, oc, *([z] * ng), mod, wa, wb, wc, wo)


def _w1_apply(hb, w1_ref):
    return jnp.concatenate([_dot(hb, w1_ref[d]) for d in range(N_DEV)], axis=-1)


def _mlp_fwd(cfg, l, x1, gn, mod, w1, w2, *, ctx_active, exch=None):
    B, T, D, F, tm, nC = cfg.B, cfg.T, cfg.D, cfg.F, cfg.tm, cfg.nC

    def body(x_ref, gn_ref, mod_ref, w1_ref, w2_ref, x2_ref, mo_ref):
        b, t = pl.program_id(0), pl.program_id(1)

        def compute():
            row = jnp.where(t < nC, B, b)
            x = x_ref[0]
            h, _, _ = _modulate(x, gn_ref[0], _mod_row(mod_ref, row, 3, D), _mod_row(mod_ref, row, 4, D))
            a = jnp.maximum(_w1_apply(h.astype(BF16), w1_ref), 0.0)
            mo = _dot((a * a).astype(BF16), w2_ref[...])
            mo_ref[0] = mo.astype(BF16)
            x2_ref[0] = x + _mod_row(mod_ref, row, 5, D) * mo

        if ctx_active:
            compute()
        else:
            pl.when(t >= nC)(compute)

            @pl.when(t < nC)
            def _():
                mo_ref[0] = jnp.zeros((tm, D), BF16)
                x2_ref[0] = x_ref[0]

    row = pl.BlockSpec((1, tm, D), lambda b, t: (b, t, 0))
    return _pcall(exch)(
        body, name="mlp_fwd", grid=(B, cfg.nT),
        in_specs=[row, _layer(l, D), _full((8, 6 * D)), _full((N_DEV, D, F // N_DEV)), _full((F, D))],
        out_specs=[row, row],
        out_shape=[jax.ShapeDtypeStruct((B, T, D), F32), jax.ShapeDtypeStruct((B, T, D), BF16)],
        compiler_params=_params(2),
    )(x1, gn, mod, w1, w2)


def _loss_fwd_bwd(cfg, x2, target):
    B, T, D, tm, nC = cfg.B, cfg.T, cfg.D, cfg.tm, cfg.nC

    def body(x_ref, tgt_ref, dx_ref, sse_ref):
        b, t = pl.program_id(0), pl.program_id(1)

        @pl.when((b == 0) & (t == 0))
        def _():
            sse_ref[...] = jnp.zeros((8, LANES), F32)

        @pl.when(t < nC)
        def _():
            dx_ref[0] = jnp.zeros((tm, D), F32)

        @pl.when(t >= nC)
        def _():
            err = x_ref[0] - tgt_ref[0]
            dx_ref[0] = err * (1.0 / D)
            sse_ref[...] += jnp.sum(err * err)

    return pl.pallas_call(
        body, name="loss", grid=(B, cfg.nT),
        in_specs=[pl.BlockSpec((1, tm, D), lambda b, t: (b, t, 0)),
                  pl.BlockSpec((1, tm, D), lambda b, t: (b, jnp.maximum(t - nC, 0), 0))],
        out_specs=[pl.BlockSpec((1, tm, D), lambda b, t: (b, t, 0)), _full((8, LANES))],
        out_shape=[jax.ShapeDtypeStruct((B, T, D), F32), jax.ShapeDtypeStruct((8, LANES), F32)],
        compiler_params=_params(2),
    )(x2, target)


def _acc_init(refs):
    b, t = pl.program_id(0), pl.program_id(1)

    @pl.when((b == 0) & (t == 0))
    def _():
        for ref in refs:
            ref[...] = jnp.zeros(ref.shape, ref.dtype)


def _mlp_bwd(cfg, l, x1, dx2, mo, gn, mod, w1, w2, *, ctx_active, exch=None):
    B, T, D, F, tm, nC = cfg.B, cfg.T, cfg.D, cfg.F, cfg.tm, cfg.nC
    ws = F // N_DEV

    def body(x_ref, dx_ref, mo_ref, gn_ref, mod_ref, w1_ref, w2_ref, dx1_ref, h_ref, r_ref, da_ref, dout_ref, dmod_ref, dgn_ref):
        b, t = pl.program_id(0), pl.program_id(1)
        _acc_init([dmod_ref, dgn_ref])

        def compute():
            row = jnp.where(t < nC, B, b)
            gn = gn_ref[0]
            scale = _mod_row(mod_ref, row, 4, D)
            h, xhat, rstd = _modulate(x_ref[0], gn, _mod_row(mod_ref, row, 3, D), scale)
            hb = h.astype(BF16)
            a = jnp.maximum(_w1_apply(hb, w1_ref), 0.0)
            dx = dx_ref[0]
            dout = (dx * _mod_row(mod_ref, row, 5, D)).astype(BF16)
            da = (_dot_nt(dout, w2_ref[...]) * (2.0 * a)).astype(BF16)
            dh = _dot_nt(da[:, 0:ws], w1_ref[0])
            for d in range(1, N_DEV):
                dh = dh + _dot_nt(da[:, d * ws:(d + 1) * ws], w1_ref[d])
            dxn, d_shift, d_scale, d_gn = _modulate_bwd(dh, xhat, rstd, gn, scale)
            dx1_ref[0] = dx + dxn
            h_ref[0] = hb
            r_ref[0] = (a * a).astype(BF16)
            da_ref[0] = da
            dout_ref[0] = dout
            d_gate = jnp.sum(dx * mo_ref[0].astype(F32), axis=0, keepdims=True)
            dmod_ref[pl.ds(row, 1), :] += jnp.concatenate([d_shift, d_scale, d_gate], axis=-1)
            dgn_ref[0:1, :] += d_gn

        if ctx_active:
            compute()
        else:
            pl.when(t >= nC)(compute)

            @pl.when(t < nC)
            def _():
                dx1_ref[0] = dx_ref[0]
                h_ref[0] = jnp.zeros((tm, D), BF16)
                r_ref[0] = jnp.zeros((tm, F), BF16)
                da_ref[0] = jnp.zeros((tm, F), BF16)
                dout_ref[0] = jnp.zeros((tm, D), BF16)

    row = lambda w: pl.BlockSpec((1, tm, w), lambda b, t: (b, t, 0))
    sds = lambda w, dt: jax.ShapeDtypeStruct((B, T, w), dt)
    return _pcall(exch)(
        body, name="mlp_bwd", grid=(B, cfg.nT),
        in_specs=[row(D), row(D), row(D), _layer(l, D), _full((8, 6 * D)), _full((N_DEV, D, ws)), _full((F, D))],
        out_specs=[row(D), row(D), row(F), row(F), row(D), _full((8, 3 * D)), _full((8, D))],
        out_shape=[sds(D, F32), sds(D, BF16), sds(F, BF16), sds(F, BF16), sds(D, BF16),
                   jax.ShapeDtypeStruct((8, 3 * D), F32), jax.ShapeDtypeStruct((8, D), F32)],
        compiler_params=_params(2),
    )(x1, dx2, mo, gn, mod, w1, w2)


def _matmul_tn(a, g, name, *, by_shard):
    R, Ka = a.shape
    Ng = g.shape[1]
    tr = next(c for c in (512, 256, 128, 64, 32, 16, 8) if R % c == 0)
    tka = Ka if Ka <= 1024 else 1024
    if by_shard:
        tn = Ng // N_DEV
    else:
        tn = next(c for c in (1152, 1024, 768, 512, 384, 256, 128) if Ng % c == 0)
    assert Ka % tka == 0 and tn % LANES == 0
    nr = R // tr

    def body(a_ref, g_ref, o_ref, acc_ref):
        r = pl.program_id(2)

        @pl.when(r == 0)
        def _():
            acc_ref[...] = jnp.zeros(acc_ref.shape, F32)

        acc_ref[...] += _dot_tn(a_ref[...], g_ref[...])

        @pl.when(r == nr - 1)
        def _():
            o_ref[...] = acc_ref[...].astype(BF16).reshape(o_ref.shape)

    if by_shard:
        out_spec = pl.BlockSpec((1, tka, tn), lambda i, j, r: (j, i, 0))
        out_shape = jax.ShapeDtypeStruct((N_DEV, Ka, tn), BF16)
    else:
        out_spec = pl.BlockSpec((tka, tn), lambda i, j, r: (i, j))
        out_shape = jax.ShapeDtypeStruct((Ka, Ng), BF16)
    return pl.pallas_call(
        body, name=name, grid=(Ka // tka, Ng // tn, nr),
        in_specs=[pl.BlockSpec((tr, tka), lambda i, j, r: (r, i)), pl.BlockSpec((tr, tn), lambda i, j, r: (r, j))],
        out_specs=out_spec, out_shape=out_shape, scratch_shapes=[pltpu.VMEM((tka, tn), F32)], compiler_params=_params(3),
    )(a, g)


def _merge_bwd(cfg, dx1, mgo, oa, ob, oc, z, mod, wa, wb, wc, wo, *, ctx_active, exch=None):
    B, T, D, tm, nC = cfg.B, cfg.T, cfg.D, cfg.tm, cfg.nC
    ng = 3 * D // cfg.gw

    def body(dx_ref, mgo_ref, oa_ref, ob_ref, oc_ref, *rest):
        gate_refs = rest[:ng]
        (mod_ref, wa_ref, wb_ref, wc_ref, wo_ref,
         doa_ref, dob_ref, doc_ref, dpa_ref, dpb_ref, dpc_ref, y_ref, dmo_ref, dzg_ref, dg1_ref) = rest[ng:]
        b, t = pl.program_id(0), pl.program_id(1)
        _acc_init([dg1_ref])

        def compute():
            row = jnp.where(t < nC, B, b)
            dx = dx_ref[0]
            dg1_ref[pl.ds(row, 1), :] += jnp.sum(dx * mgo_ref[0].astype(F32), axis=0, keepdims=True)
            dmo = (dx * _mod_row(mod_ref, row, 2, D)).astype(BF16)
            dmo_ref[0] = dmo
            dy = _dot_nt(dmo, wo_ref[...])
            gates = _read_gates(cfg, gate_refs)
            y = jnp.zeros((tm, D), F32)
            dgs = []
            for gate, o_ref, w_ref, do_ref, dp_ref in ((gates[0], oa_ref, wa_ref, doa_ref, dpa_ref),
                                                      (gates[1], ob_ref, wb_ref, dob_ref, dpb_ref),
                                                      (gates[2], oc_ref, wc_ref, doc_ref, dpc_ref)):
                s = jax.nn.sigmoid(gate)
                p = _dot(o_ref[0], w_ref[...])
                y = y + s * p
                dp = (dy * s).astype(BF16)
                dp_ref[0] = dp
                do_ref[0] = _dot_nt(dp, w_ref[...]).astype(BF16)
                dgs.append((dy * p * s * (1.0 - s)).astype(BF16))
            y_ref[0] = y.astype(BF16)
            dzg_ref[0] = jnp.concatenate(dgs, axis=-1)

        if ctx_active:
            compute()
        else:
            pl.when(t >= nC)(compute)

            @pl.when(t < nC)
            def _():
                for ref in (doa_ref, dob_ref, doc_ref, dpa_ref, dpb_ref, dpc_ref, y_ref, dmo_ref, dzg_ref):
                    ref[...] = jnp.zeros(ref.shape, ref.dtype)

    row = lambda w: pl.BlockSpec((1, tm, w), lambda b, t: (b, t, 0))
    sds = lambda w: jax.ShapeDtypeStruct((B, T, w), BF16)
    return _pcall(exch)(
        body, name="merge_bwd", grid=(B, cfg.nT),
        in_specs=[row(D), row(D), row(QW), row(POOL_W), row(QW)] + _gate_specs(cfg)
        + [_full((8, 6 * D)), _full((QW, D)), _full((POOL_W, D)), _full((QW, D)), _full((D, D))],
        out_specs=[row(QW), row(POOL_W), row(QW), row(D), row(D), row(D), row(D), row(D), row(3 * D), _full((8, D))],
        out_shape=[sds(QW), sds(POOL_W), sds(QW), sds(D), sds(D), sds(D), sds(D), sds(D), sds(3 * D),
                   jax.ShapeDtypeStruct((8, D), F32)],
        compiler_params=_params(2),
    )(dx1, mgo, oa, ob, oc, *([z] * ng), mod, wa, wb, wc, wo)


def _attn_bwd(cfg, l, q, k, z, vblock, sink8, do, *, window, sink, ctx_attend, name, exch=None):
    B, S, N, T, tq, nC = cfg.B, cfg.S, cfg.N, cfg.T, cfg.tm, cfg.nC

    def body(q_ref, k_ref, v_ref, sink_ref, do_ref, dq_ref, dk_ref, dv_ref, dsink_ref):
        b, t = pl.program_id(0), pl.program_id(1)
        _acc_init([dsink_ref])

        @pl.when(t == 0)
        def _():
            dk_ref[...] = jnp.zeros(dk_ref.shape, F32)
            dv_ref[...] = jnp.zeros(dv_ref.shape, F32)

        def run(loc):
            q_t = q_ref[0]
            do_t = do_ref[0]
            dqs = [None] * N_QHEADS
            dk_c, dv_c, dk_l, dv_l = [], [], [], []
            dsink_row = jnp.zeros((1, LANES), F32)
            lane = lax.broadcasted_iota(jnp.int32, (1, LANES), 1)
            lo = 0
            for h in range(N_KV):
                qs, p_c, p_l, p_s, k_ctx, v_ctx, k_loc, v_loc, lo = _attn_scores(
                    cfg, t - nC, q_t, k_ref, v_ref, sink_ref, h, loc, window and loc, sink)
                dos = jnp.concatenate([do_t[:, (GROUP * h + g) * HEAD:(GROUP * h + g + 1) * HEAD] for g in range(GROUP)], axis=0)
                dp_c = _dot_nt(dos, v_ctx)
                delta = jnp.sum(p_c * dp_c, axis=-1, keepdims=True)
                if loc:
                    dp_l = _dot_nt(dos, v_loc)
                    delta = delta + jnp.sum(p_l * dp_l, axis=-1, keepdims=True)
                ds_c = (p_c * (dp_c - delta)).astype(BF16)
                dq = _dot(ds_c, k_ctx)
                dk_c.append(_dot_tn(ds_c, qs))
                dv_c.append(_dot_tn(p_c.astype(BF16), dos))
                if loc:
                    ds_l = (p_l * (dp_l - delta)).astype(BF16)
                    dq = dq + _dot(ds_l, k_loc)
                    dk_l.append(_dot_tn(ds_l, qs))
                    dv_l.append(_dot_tn(p_l.astype(BF16), dos))
                if sink:
                    dsk = -p_s * delta
                    for g in range(GROUP):
                        tot = jnp.sum(dsk[g * tq:(g + 1) * tq], axis=0, keepdims=True)
                        dsink_row = dsink_row + jnp.where(lane == GROUP * h + g, tot, 0.0)
                for g in range(GROUP):
                    dqs[GROUP * h + g] = dq[g * tq:(g + 1) * tq] * QSCALE
            dq_ref[0] = jnp.concatenate(dqs, axis=-1)
            dk_ref[0, 0:N, :] += jnp.concatenate(dk_c, axis=-1)
            dv_ref[0, 0:N, :] += jnp.concatenate(dv_c, axis=-1)
            if loc:
                W = tq + 2 * WINDOW if window else S
                dk_ref[0, pl.ds(N + lo, W), :] += jnp.concatenate(dk_l, axis=-1)
                dv_ref[0, pl.ds(N + lo, W), :] += jnp.concatenate(dv_l, axis=-1)
            if sink:
                dsink_ref[0:1, :] += dsink_row

        pl.when(t >= nC)(functools.partial(run, True))
        if ctx_attend:
            pl.when(t < nC)(functools.partial(run, False))
        else:
            @pl.when(t < nC)
            def _():
                dq_ref[0] = jnp.zeros((tq, QW), F32)

    kv = pl.BlockSpec((1, T, KVW), lambda b, t: (b, 0, 0))
    qrow = pl.BlockSpec((1, tq, QW), lambda b, t: (b, t, 0))
    return _pcall(exch)(
        body, name=name, grid=(B, cfg.nT),
        in_specs=[qrow, kv, pl.BlockSpec((1, T, KVW), lambda b, t: (b, 0, vblock)),
                  pl.BlockSpec((1, 8, LANES), lambda b, t: (l, 0, 0)), qrow],
        out_specs=[qrow, kv, kv, _full((8, LANES))],
        out_shape=[jax.ShapeDtypeStruct((B, T, QW), F32), jax.ShapeDtypeStruct((B, T, KVW), F32),
                   jax.ShapeDtypeStruct((B, T, KVW), F32), jax.ShapeDtypeStruct((8, LANES), F32)],
        compiler_params=_params(2),
    )(q, k, z, sink8, do)


def _qknorm_bwd(cfg, l, z, gvec, cosf, sins, dqa, dka, dva, dqc, dkc, dvc):
    B, T, tm = cfg.B, cfg.T, cfg.tm

    def body(z_ref, g_ref, cos_ref, sin_ref, dqa_ref, dka_ref, dva_ref, dqc_ref, dkc_ref, dvc_ref, dz_ref, dg_ref):
        _acc_init([dg_ref])
        ind = _head_indicator()
        cos, sin = cos_ref[...], sin_ref[...]
        dqa_t, dqc_t = dqa_ref[0], dqc_ref[0]
        douts = {0: dqa_t[:, 0:128], 1: dqa_t[:, 128:256], 2: dqa_t[:, 256:384], 3: dka_ref[0],
                 5: dqc_t[:, 0:128], 6: dqc_t[:, 128:256], 7: dqc_t[:, 256:384], 8: dkc_ref[0]}
        pieces = []
        dgs = []
        for c in range(QKV_W // LANES):
            if c not in douts:
                pieces.append(dva_ref[0] if c == 4 else dvc_ref[0])
                dgs.append(jnp.zeros((1, LANES), F32))
                continue
            x = z_ref[0, :, c * LANES:(c + 1) * LANES]
            g = g_ref[0, :, c * LANES:(c + 1) * LANES]
            ss = jnp.dot(x * x, ind, precision=HIGHEST, preferred_element_type=F32)
            rstd = lax.rsqrt(ss * (1.0 / HEAD) + EPS)
            n = x * rstd
            dout = douts[c]
            dy = dout * cos + _pair_swap(dout * sin)
            dgs.append(jnp.sum(dy * n, axis=0, keepdims=True))
            dn = dy * g
            mean = jnp.dot(dn * n, ind, precision=HIGHEST, preferred_element_type=F32) * (1.0 / HEAD)
            pieces.append(rstd * (dn - n * mean))
        dz_ref[0] = jnp.concatenate(pieces, axis=-1).astype(BF16)
        dg_ref[0:1, :] += jnp.concatenate(dgs, axis=-1)

    row = lambda w: pl.BlockSpec((1, tm, w), lambda b, t: (b, t, 0))
    tab = pl.BlockSpec((tm, LANES), lambda b, t: (t, 0))
    return pl.pallas_call(
        body, name="qknorm_bwd", grid=(B, cfg.nT),
        in_specs=[row(QKV_W), _layer(l, QKV_W), tab, tab, row(QW), row(KVW), row(KVW), row(QW), row(KVW), row(KVW)],
        out_specs=[row(QKV_W), _full((8, QKV_W))],
        out_shape=[jax.ShapeDtypeStruct((B, T, QKV_W), BF16), jax.ShapeDtypeStruct((8, QKV_W), F32)],
        compiler_params=_params(2),
    )(z, gvec, cosf, sins, dqa, dka, dva, dqc, dkc, dvc)


def _pool_bwd(cfg, l, dob, pooled, wp, ps):
    B, T, tm = cfg.B, cfg.T, cfg.tm

    def body(dob_ref, pooled_ref, wp_ref, ps_ref, du_ref, dwp_ref, dps_ref):
        t = pl.program_id(1)
        _acc_init([dwp_ref, dps_ref])
        r0, seg_lo, seg_hi, k0 = _pool_geometry(cfg, t)
        ps = ps_ref[0]
        wp = wp_ref[0]
        dmix = dob_ref[0, pl.ds(r0, tm), :].astype(F32)
        pooled = pooled_ref[0]
        dps_ref[0:1, :] += jnp.sum(dmix * _dot(pooled, wp), axis=0, keepdims=True)
        dpm = (dmix * ps).astype(BF16)
        dwp_ref[...] += _dot_tn(pooled, dpm)
        dpooled_t = _dot_nt(dpm, wp)
        dpm_w = (dob_ref[0, pl.ds(k0, 3 * tm), :].astype(F32) * ps).astype(BF16)
        dpooled_w = _dot_nt(dpm_w, wp)
        rr = r0 + lax.broadcasted_iota(jnp.int32, (tm, 3 * tm), 0)
        cc = k0 + lax.broadcasted_iota(jnp.int32, (tm, 3 * tm), 1)
        diff = rr - cc
        inseg = (cc >= seg_lo) & (cc < seg_hi)
        ccol = k0 + lax.broadcasted_iota(jnp.int32, (3 * tm, 1), 0)
        group = lax.broadcasted_iota(jnp.int32, (tm, POOL_W), 1) // HEAD
        acc = jnp.zeros((tm, POOL_W), F32)
        for gi, w in enumerate(POOL_WINDOWS):
            h = w // 2
            band_t = jnp.where((diff >= -h) & (diff <= h - 1) & inseg, 1.0, 0.0).astype(BF16)
            hi, lo = _split_bf16(dpooled_w / _pool_count(ccol, h, seg_lo, seg_hi))
            acc = jnp.where(group == gi, _dot(band_t, hi) + _dot(band_t, lo), acc)
        du_ref[0] = (acc - dpooled_t).astype(BF16)

    row = pl.BlockSpec((1, tm, POOL_W), lambda b, t: (b, t, 0))
    return pl.pallas_call(
        body, name="pool_bwd", grid=(B, cfg.nT),
        in_specs=[pl.BlockSpec((1, T, POOL_W), lambda b, t: (b, 0, 0)), row,
                  pl.BlockSpec((1, POOL_W, POOL_W), lambda b, t: (l, 0, 0)), _layer(l, POOL_W)],
        out_specs=[row, _full((POOL_W, POOL_W)), _full((8, POOL_W))],
        out_shape=[jax.ShapeDtypeStruct((B, T, POOL_W), BF16), jax.ShapeDtypeStruct((POOL_W, POOL_W), F32),
                   jax.ShapeDtypeStruct((8, POOL_W), F32)],
        compiler_params=_params(2),
    )(dob, pooled, wp, ps)


def _in_proj_bwd(cfg, l, dzq, du, dzg, w_in, x, dx1, gn, mod, *, latent_only, exch=None):
    B, S, T, D, IN, tm, nC = cfg.B, cfg.S, cfg.T, cfg.D, cfg.IN, cfg.tm, cfg.nC

    def body(dzq_ref, du_ref, dzg_ref, w_ref, x_ref, dx1_ref, gn_ref, mod_ref, dx0_ref, dz_ref, dmod_ref, dgn_ref):
        b, t = pl.program_id(0), pl.program_id(1)
        _acc_init([dmod_ref, dgn_ref])
        row = jnp.where(t < nC, B, b)
        dz = jnp.concatenate([dzq_ref[0], du_ref[0], dzg_ref[0]], axis=-1)
        dz_ref[0] = dz
        dh = _dot_nt(dz, w_ref[...])
        gn = gn_ref[0]
        scale = _mod_row(mod_ref, row, 1, D)
        _, xhat, rstd = _modulate(x_ref[0], gn, _mod_row(mod_ref, row, 0, D), scale)
        dxn, d_shift, d_scale, d_gn = _modulate_bwd(dh, xhat, rstd, gn, scale)
        dx0_ref[0] = dx1_ref[0] + dxn
        dmod_ref[pl.ds(row, 1), :] += jnp.concatenate([d_shift, d_scale], axis=-1)
        dgn_ref[0:1, :] += d_gn

    row = lambda w: pl.BlockSpec((1, tm, w), lambda b, t: (b, t, 0))
    if latent_only:
        dx0_spec = pl.BlockSpec((1, tm, D), lambda b, t: (b, jnp.maximum(t - nC, 0), 0))
        dx0_shape = jax.ShapeDtypeStruct((B, S, D), F32)
    else:
        dx0_spec, dx0_shape = row(D), jax.ShapeDtypeStruct((B, T, D), F32)
    return _pcall(exch)(
        body, name="in_proj_bwd", grid=(B, cfg.nT),
        in_specs=[row(QKV_W), row(POOL_W), row(3 * D), _full((D, IN)), row(D), row(D), _layer(l, D), _full((8, 6 * D))],
        out_specs=[dx0_spec, row(IN), _full((8, 2 * D)), _full((8, D))],
        out_shape=[dx0_shape, jax.ShapeDtypeStruct((B, T, IN), BF16),
                   jax.ShapeDtypeStruct((8, 2 * D), F32), jax.ShapeDtypeStruct((8, D), F32)],
        compiler_params=_params(2),
    )(dzq, du, dzg, w_in, x, dx1, gn, mod)


def _adaln_bwd(cfg, cc8, dmod_in, dg1, dmod_mlp, w_ada):
    d = cfg.D
    wa = 6 * d // N_DEV

    def body(c_ref, din_ref, dg1_ref, dmlp_ref, w_ref, dc_ref, a_ref, dmb_ref, db_ref):
        c = c_ref[...]
        s = jax.nn.sigmoid(c)
        a_ref[...] = (c * s).astype(BF16)
        dm_all = jnp.concatenate([din_ref[...], dg1_ref[...], dmlp_ref[...]], axis=-1)
        dmb = dm_all.astype(BF16)
        dmb_ref[...] = dmb
        db_ref[...] = jnp.broadcast_to(jnp.sum(dm_all, axis=0, keepdims=True), (8, 6 * d))
        dc = _dot_nt(dmb[:, 0:wa], w_ref[0])
        for j in range(1, N_DEV):
            dc = dc + _dot_nt(dmb[:, j * wa:(j + 1) * wa], w_ref[j])
        dc_ref[...] = dc * (s * (1.0 + c * (1.0 - s)))

    return pl.pallas_call(
        body, name="adaln_bwd", grid=(1,),
        in_specs=[_full((8, d)), _full((8, 2 * d)), _full((8, d)), _full((8, 3 * d)), _full((N_DEV, d, wa))],
        out_specs=[_full((8, d)), _full((8, d)), _full((8, 6 * d)), _full((8, 6 * d))],
        out_shape=[jax.ShapeDtypeStruct((8, d), F32), jax.ShapeDtypeStruct((8, d), BF16),
                   jax.ShapeDtypeStruct((8, 6 * d), BF16), jax.ShapeDtypeStruct((8, 6 * d), F32)],
        compiler_params=_params(1),
    )(cc8, dmod_in, dg1, dmod_mlp, w_ada)


def _adam_update(g, w, m, v):
    bc1 = 1.0 - ADAM_B1 ** ADAM_STEP
    bc2 = 1.0 - ADAM_B2 ** ADAM_STEP
    m2 = ADAM_B1 * m + (1.0 - ADAM_B1) * g
    v2 = ADAM_B2 * v + (1.0 - ADAM_B2) * (g * g)
    delta = -ADAM_LR * ((m2 / bc1) / (jnp.sqrt(v2 / bc2) + ADAM_EPS) + ADAM_WD * w)
    return delta, m2, v2


def _sum_parts(p_ref):
    g = p_ref[0].astype(F32)
    for d in range(1, N_DEV):
        g = g + p_ref[d].astype(F32)
    return g


def _adamw_sharded(parts, w, m, v, name):
    L, K, W = w.shape
    tk = next(c for c in (256, 128, 64, 32, 16, 8) if K % c == 0)

    def body(*refs):
        p_refs = refs[:L]
        w_ref, m_ref, v_ref, g_ref, d_ref, m2_ref, v2_ref = refs[L:]
        layer = pl.program_id(0)

        def run(p_ref):
            g = _sum_parts(p_ref)
            delta, m2, v2 = _adam_update(g, w_ref[0], m_ref[0], v_ref[0])
            g_ref[0] = g
            d_ref[0] = delta
            m2_ref[0] = m2
            v2_ref[0] = v2

        for li in range(L):
            pl.when(layer == li)(functools.partial(run, p_refs[li]))

    blk = pl.BlockSpec((1, tk, W), lambda l, i: (l, i, 0))
    part_spec = lambda li: pl.BlockSpec((N_DEV, tk, W), lambda l, i: (0, jnp.where(l == li, i, 0), 0))
    return pl.pallas_call(
        body, name=name, grid=(L, K // tk),
        in_specs=[part_spec(li) for li in range(L)] + [blk, blk, blk],
        out_specs=[blk] * 4, out_shape=[jax.ShapeDtypeStruct((L, K, W), F32)] * 4,
        compiler_params=_params(2),
    )(*parts, w, m, v)


def _adamw_packed(parts, w, m, v, name):
    rows = w.shape[0]
    tr = next(c for c in (256, 128, 64, 32, 16, 8) if rows % c == 0)

    def body(p_ref, w_ref, m_ref, v_ref, g_ref, d_ref, m2_ref, v2_ref):
        g = _sum_parts(p_ref)
        delta, m2, v2 = _adam_update(g, w_ref[...], m_ref[...], v_ref[...])
        g_ref[...] = g
        d_ref[...] = delta
        m2_ref[...] = m2
        v2_ref[...] = v2

    blk = pl.BlockSpec((tr, PACK_W), lambda i: (i, 0))
    return pl.pallas_call(
        body, name=name, grid=(rows // tr,),
        in_specs=[pl.BlockSpec((N_DEV, tr, PACK_W), lambda i: (0, i, 0)), blk, blk, blk],
        out_specs=[blk] * 4, out_shape=[jax.ShapeDtypeStruct((rows, PACK_W), F32)] * 4,
        compiler_params=_params(1),
    )(parts, w, m, v)


_SHARDED = dict(w_ada=True, w_in=True, w_br_a=True, w_br_b=True, w_br_c=True, w_out=False, w_mlp1=True, w_mlp2=False)
_GROUPS = (("w_ada", "w_in"), ("w_br_a", "w_br_b", "w_br_c", "w_out"), ("w_mlp1", "w_mlp2"))
_KEEP_SHARDS = ("w_ada", "w_mlp1")
_SMALL = ("c_ctx", "b_ada", "norm1", "norm2", "q_norm_a", "k_norm_a", "q_norm_c", "k_norm_c", "sink_c", "w_pool", "pool_scale")


def _from_shards(name, g):
    n, k, w = g.shape
    if name in _KEEP_SHARDS:
        return g
    if _SHARDED[name]:
        return g.transpose(1, 0, 2).reshape(k, n * w)
    return g.reshape(n * k, w)


def _to_shards(name, g):
    if g.ndim == 3:
        return g
    if _SHARDED[name]:
        k, nw = g.shape
        return g.reshape(k, N_DEV, nw // N_DEV).transpose(1, 0, 2)
    nk, w = g.shape
    return g.reshape(N_DEV, nk // N_DEV, w)


def _pack_small(vals):
    flat = jnp.concatenate([vals[n].reshape(-1) for n in _SMALL])
    rows = -(-flat.shape[0] // (8 * PACK_W)) * 8
    return jnp.pad(flat, (0, rows * PACK_W - flat.shape[0])).reshape(rows, PACK_W)


def _unpack_small(packed, like):
    flat, out, r = packed.reshape(-1), {}, 0
    for n in _SMALL:
        sz = like[n].size
        out[n] = flat[r:r + sz].reshape(like[n].shape)
        r += sz
    return out


def _rope_tables(cfg):
    pos = jnp.arange(cfg.S, dtype=F32)
    r = jnp.floor(pos / GRID_W)
    col = pos - r * GRID_W
    inv = 1.0 / (ROPE_THETA ** (jnp.arange(0, HEAD // 2, 2, dtype=F32) / (HEAD // 2)))
    ang = jnp.concatenate([r[:, None] * inv, col[:, None] * inv], axis=-1)
    cos = jnp.repeat(jnp.cos(ang), 2, axis=-1)
    sin = jnp.repeat(jnp.sin(ang), 2, axis=-1) * jnp.tile(jnp.array([-1.0, 1.0], F32), HEAD // 2)
    cos = jnp.concatenate([jnp.ones((cfg.N, HEAD), F32), cos], axis=0)
    sin = jnp.concatenate([jnp.zeros((cfg.N, HEAD), F32), sin], axis=0)
    return jnp.tile(cos, (1, 2)), jnp.tile(sin, (1, 2))


def _gvec(qa, ka, qc, kc):
    one = jnp.ones((qa.shape[0], KVW), F32)
    t = lambda a, n: jnp.tile(a, (1, n))
    return jnp.concatenate([t(qa, N_QHEADS), t(ka, N_KV), one, t(qc, N_QHEADS), t(kc, N_KV), one], axis=-1)[:, None, :]


def _block_diag(wp):
    L, g, c, _ = wp.shape
    eye = jnp.eye(g, dtype=wp.dtype)
    return (wp[:, :, :, None, :] * eye[None, :, None, :, None]).reshape(L, g * c, g * c)


def _pad8(a):
    return jnp.pad(a, ((0, 8 - a.shape[0]), (0, 0)))


def kernel(x, c, ctx, c_ctx, w_ada, b_ada, norm1, norm2, w_in, q_norm_a, k_norm_a, q_norm_c, k_norm_c, sink_c, w_pool, pool_scale, w_br_a, w_br_b, w_br_c, w_out, w_mlp1, w_mlp2, loss_target, m_c_ctx, m_w_ada, m_b_ada, m_norm1, m_norm2, m_w_in, m_q_norm_a, m_k_norm_a, m_q_norm_c, m_k_norm_c, m_sink_c, m_w_pool, m_pool_scale, m_w_br_a, m_w_br_b, m_w_br_c, m_w_out, m_w_mlp1, m_w_mlp2, v_c_ctx, v_w_ada, v_b_ada, v_norm1, v_norm2, v_w_in, v_q_norm_a, v_k_norm_a, v_q_norm_c, v_k_norm_c, v_sink_c, v_w_pool, v_pool_scale, v_w_br_a, v_w_br_b, v_w_br_c, v_w_out, v_w_mlp1, v_w_mlp2):
    B, S, D = x.shape
    N = ctx.shape[1]
    L = w_ada.shape[0]
    cfg = _Cfg(B, S, N, D)
    T = cfg.T
    weights = dict(c_ctx=c_ctx, w_ada=w_ada, b_ada=b_ada, norm1=norm1, norm2=norm2, w_in=w_in, q_norm_a=q_norm_a,
                   k_norm_a=k_norm_a, q_norm_c=q_norm_c, k_norm_c=k_norm_c, sink_c=sink_c, w_pool=w_pool,
                   pool_scale=pool_scale, w_br_a=w_br_a, w_br_b=w_br_b, w_br_c=w_br_c, w_out=w_out, w_mlp1=w_mlp1, w_mlp2=w_mlp2)
    mom_m = dict(c_ctx=m_c_ctx, w_ada=m_w_ada, b_ada=m_b_ada, norm1=m_norm1, norm2=m_norm2, w_in=m_w_in, q_norm_a=m_q_norm_a,
                 k_norm_a=m_k_norm_a, q_norm_c=m_q_norm_c, k_norm_c=m_k_norm_c, sink_c=m_sink_c, w_pool=m_w_pool,
                 pool_scale=m_pool_scale, w_br_a=m_w_br_a, w_br_b=m_w_br_b, w_br_c=m_w_br_c, w_out=m_w_out, w_mlp1=m_w_mlp1, w_mlp2=m_w_mlp2)
    mom_v = dict(c_ctx=v_c_ctx, w_ada=v_w_ada, b_ada=v_b_ada, norm1=v_norm1, norm2=v_norm2, w_in=v_w_in, q_norm_a=v_q_norm_a,
                 k_norm_a=v_k_norm_a, q_norm_c=v_q_norm_c, k_norm_c=v_k_norm_c, sink_c=v_sink_c, w_pool=v_w_pool,
                 pool_scale=v_pool_scale, w_br_a=v_w_br_a, w_br_b=v_w_br_b, w_br_c=v_w_br_c, w_out=v_w_out, w_mlp1=v_w_mlp1, w_mlp2=v_w_mlp2)

    shards_bf16 = {n: weights[n].astype(BF16) for n in _SHARDED}
    full = [dict() for _ in range(L)]

    def gather_of(l, gi):
        return _Exchange([(shards_bf16[n], l) for n in _GROUPS[gi]], scatter=False)

    def gathered(l, gi, arrs):
        for n, a in zip(_GROUPS[gi], arrs):
            full[l][n] = _from_shards(n, a)

    gathered(0, 0, gather_of(0, 0).alone("gather_first_weights"))

    def hosting(fn, *a, exch=None, done=None, **kw):
        if exch is None:
            return fn(*a, **kw)
        res = fn(*a, exch=exch, **kw)
        done(res[-exch.n:])
        own = res[:-exch.n]
        return own[0] if len(own) == 1 else own

    def gather_behind(l, gi):
        if l >= L:
            return {}
        return dict(exch=gather_of(l, gi), done=functools.partial(gathered, l, gi))

    cosf, sins = _rope_tables(cfg)
    xs = jnp.concatenate([ctx, x], axis=1)
    cc8 = _pad8(jnp.concatenate([c, c_ctx[None, :]], axis=0))
    va_blk, vc_blk = (QW + KVW) // KVW, (2 * QW + 3 * KVW) // KVW
    per_layer = lambda a: a[:, None, :]
    b_ada3, norm1_3, norm2_3, ps3 = per_layer(b_ada), per_layer(norm1), per_layer(norm2), per_layer(pool_scale)
    gvec = _gvec(q_norm_a, k_norm_a, q_norm_c, k_norm_c)
    sink8 = jnp.pad(sink_c[:, None, :], ((0, 0), (0, 7), (0, LANES - N_QHEADS)))
    wp = _block_diag(w_pool).astype(BF16)

    saved = []
    for l in range(L):
        fw = full[l]
        ctx_active = l < L - 1
        mod = _adaln_fwd(cfg, l, cc8, fw["w_ada"], b_ada3)
        z, h = hosting(_in_proj_fwd, cfg, l, xs, norm1_3, mod, fw["w_in"], **(gather_behind(0, 1) if l == 0 else {}))
        qa, ka, qc, kc = _qknorm_fwd(cfg, l, z, gvec, cosf, sins)
        oa = hosting(_attn_fwd, cfg, l, qa, ka, z, va_blk, sink8, window=False, sink=False, ctx_attend=ctx_active, name="attn_a_fwd",
                     **gather_behind(l + 1, 0))
        oc = hosting(_attn_fwd, cfg, l, qc, kc, z, vc_blk, sink8, window=True, sink=True, ctx_attend=ctx_active, name="attn_c_fwd",
                     **(gather_behind(0, 2) if l == 0 else {}))
        ob, pooled = _pool_fwd(cfg, l, z, wp, ps3)
        x1, mgo = hosting(_merge_fwd, cfg, xs, oa, ob, oc, z, mod, fw["w_br_a"], fw["w_br_b"], fw["w_br_c"], fw["w_out"],
                          ctx_active=ctx_active, **gather_behind(l + 1, 1))
        x2, mo = hosting(_mlp_fwd, cfg, l, x1, norm2_3, mod, fw["w_mlp1"], fw["w_mlp2"], ctx_active=ctx_active,
                         **gather_behind(l + 1, 2))
        saved.append(dict(xs=xs, mod=mod, z=z, h=h, qa=qa, ka=ka, qc=qc, kc=kc, oa=oa, oc=oc, ob=ob, pooled=pooled, x1=x1, mgo=mgo, mo=mo))
        xs = x2

    dxs, sse = _loss_fwd_bwd(cfg, xs, loss_target)
    loss = lax.psum(0.5 * sse[0, 0] / D, ("x", "y", "c"))

    grads = [dict() for _ in range(L)]
    parts = {}
    small = {n: [None] * L for n in _SMALL if n != "c_ctx"}
    d_c_ctx = jnp.zeros((D,), F32)
    flat2 = lambda a: a.reshape(B * T, a.shape[-1])

    def scatter_of(l, gi):
        return _Exchange([_to_shards(n, grads[l][n]) for n in _GROUPS[gi]], scatter=True)

    def scattered(l, gi, arrs):
        for n, a in zip(_GROUPS[gi], arrs):
            parts[(l, n)] = a

    def scatter_behind(l, gi):
        if l >= L:
            return {}
        return dict(exch=scatter_of(l, gi), done=functools.partial(scattered, l, gi))

    for l in reversed(range(L)):
        fw, sv, g = full[l], saved[l], grads[l]
        ctx_active = l < L - 1
        mod = sv["mod"]
        dx1, h2, r, da, dout, dmod_mlp, dgn2 = hosting(_mlp_bwd, cfg, l, sv["x1"], dxs, sv["mo"], norm2_3, mod, fw["w_mlp1"], fw["w_mlp2"],
                                                       ctx_active=ctx_active, **scatter_behind(l + 1, 2))
        g["w_mlp1"] = _matmul_tn(flat2(h2), flat2(da), "dw_mlp1", by_shard=True)
        g["w_mlp2"] = _matmul_tn(flat2(r), flat2(dout), "dw_mlp2", by_shard=False)
        doa, dob, doc, dpa, dpb, dpc, y, dmo, dzg, dg1 = hosting(
            _merge_bwd, cfg, dx1, sv["mgo"], sv["oa"], sv["ob"], sv["oc"], sv["z"], mod, fw["w_br_a"], fw["w_br_b"], fw["w_br_c"], fw["w_out"],
            ctx_active=ctx_active, **scatter_behind(l + 1, 1))
        g["w_out"] = _matmul_tn(flat2(y), flat2(dmo), "dw_out", by_shard=False)
        g["w_br_a"] = _matmul_tn(flat2(sv["oa"]), flat2(dpa), "dw_br_a", by_shard=True)
        g["w_br_b"] = _matmul_tn(flat2(sv["ob"]), flat2(dpb), "dw_br_b", by_shard=True)
        g["w_br_c"] = _matmul_tn(flat2(sv["oc"]), flat2(dpc), "dw_br_c", by_shard=True)
        z = sv["z"]
        dqa, dka, dva, _ = hosting(_attn_bwd, cfg, l, sv["qa"], sv["ka"], z, va_blk, sink8, doa, window=False, sink=False,
                                   ctx_attend=ctx_active, name="attn_a_bwd", **scatter_behind(l + 1, 0))
        dqc, dkc, dvc, dsink = hosting(_attn_bwd, cfg, l, sv["qc"], sv["kc"], z, vc_blk, sink8, doc, window=True, sink=True,
                                       ctx_attend=ctx_active, name="attn_c_bwd", **(scatter_behind(0, 2) if l == 0 else {}))
        dzq, dgvec = _qknorm_bwd(cfg, l, z, gvec, cosf, sins, dqa, dka, dva, dqc, dkc, dvc)
        du, dwp, dps = _pool_bwd(cfg, l, dob, sv["pooled"], wp, ps3)
        dxs, dz, dmod_in, dgn1 = hosting(_in_proj_bwd, cfg, l, dzq, du, dzg, fw["w_in"], sv["xs"], dx1, norm1_3, mod,
                                         latent_only=(l == 0), **(scatter_behind(0, 1) if l == 0 else {}))
        g["w_in"] = _matmul_tn(flat2(sv["h"]), flat2(dz), "dw_in", by_shard=False)
        dcc, act, dmod_b, dbias = _adaln_bwd(cfg, cc8, dmod_in, dg1, dmod_mlp, fw["w_ada"])
        g["w_ada"] = _matmul_tn(act, dmod_b, "dw_ada", by_shard=True)
        d_c_ctx = d_c_ctx + dcc[B]
        gv = dgvec[0]
        heads = lambda v, n: v.reshape(n, HEAD).sum(axis=0)
        small["b_ada"][l] = dbias[0]
        small["norm1"][l] = dgn1[0]
        small["norm2"][l] = dgn2[0]
        small["q_norm_a"][l] = heads(gv[0:QW], N_QHEADS)
        small["k_norm_a"][l] = heads(gv[QW:QW + KVW], N_KV)
        small["q_norm_c"][l] = heads(gv[QW + 2 * KVW:2 * QW + 2 * KVW], N_QHEADS)
        small["k_norm_c"][l] = heads(gv[2 * QW + 2 * KVW:2 * QW + 3 * KVW], N_KV)
        small["sink_c"][l] = dsink[0, :N_QHEADS]
        small["w_pool"][l] = jnp.stack([dwp[i * HEAD:(i + 1) * HEAD, i * HEAD:(i + 1) * HEAD] for i in range(len(POOL_WINDOWS))])
        small["pool_scale"][l] = dps[0]
    grad_x = dxs

    scattered(0, 0, scatter_of(0, 0).alone("scatter_last_grads"))
    stepped = {n: _adamw_sharded([parts[(l, n)] for l in range(L)], weights[n], mom_m[n], mom_v[n], "adamw_" + n) for n in _SHARDED}

    small_vals = {n: jnp.stack(v) for n, v in small.items()}
    small_vals["c_ctx"] = d_c_ctx
    small_parts = _Exchange([_pack_small(small_vals)], scatter=False).alone("gather_small_grads")[0]
    stepped_small = _adamw_packed(small_parts, _pack_small(weights), _pack_small(mom_m), _pack_small(mom_v), "adamw_small")

    outs = []
    for i in range(4):
        res = {n: stepped[n][i] for n in _SHARDED}
        res.update(_unpack_small(stepped_small[i], weights))
        outs.append(res)
    order = ("c_ctx", "w_ada", "b_ada", "norm1", "norm2", "w_in", "q_norm_a", "k_norm_a", "q_norm_c", "k_norm_c", "sink_c",
             "w_pool", "pool_scale", "w_br_a", "w_br_b", "w_br_c", "w_out", "w_mlp1", "w_mlp2")
    return (loss, grad_x, *[res[n] for res in outs for n in order])
```

```python
import functools

import jax
import jax.numpy as jnp
from jax import lax
from jax.experimental import pallas as pl
from jax.experimental.pallas import tpu as pltpu

F32 = jnp.float32
BF16 = jnp.bfloat16
HIGHEST = lax.Precision.HIGHEST

N_DEV = 8
HEAD = 64
N_QHEADS = 6
N_KV = 2
GROUP = 3
QW = N_QHEADS * HEAD
KVW = N_KV * HEAD
QKV_W = 2 * (QW + 2 * KVW)
POOL_W = 256
POOL_WINDOWS = (2, 4, 8, 16)
GATE0 = QKV_W + POOL_W
WINDOW = 128
GRID_W = 64
ROPE_THETA = 10000.0
EPS = 1e-6
NEG = -1e30
QSCALE = HEAD ** -0.5
LANES = 128
PACK_W = 1024
VMEM_LIMIT = 56 * 1024 * 1024

ADAM_LR = 0.001
ADAM_B1 = 0.9
ADAM_B2 = 0.999
ADAM_EPS = 1e-08
ADAM_WD = 0.01
ADAM_STEP = 10

NT_DIMS = (((1,), (1,)), ((), ()))
TN_DIMS = (((0,), (0,)), ((), ()))


def _dot(a, b):
    return jnp.dot(a, b, preferred_element_type=F32)


def _dot_nt(a, b):
    return lax.dot_general(a, b, NT_DIMS, preferred_element_type=F32)


def _dot_tn(a, b):
    return lax.dot_general(a, b, TN_DIMS, preferred_element_type=F32)


def _params(n_grid):
    return pltpu.CompilerParams(dimension_semantics=("arbitrary",) * n_grid, vmem_limit_bytes=VMEM_LIMIT)


def _full(shape):
    nd = len(shape)
    return pl.BlockSpec(shape, lambda *_: (0,) * nd)


def _layer(l, width):
    return pl.BlockSpec((1, 1, width), lambda *_: (l, 0, 0))


def _modulate(x, gn, shift, scale):
    rstd = lax.rsqrt(jnp.mean(x * x, axis=-1, keepdims=True) + EPS)
    xhat = x * rstd
    return xhat * gn * (1.0 + scale) + shift, xhat, rstd


def _modulate_bwd(dh, xhat, rstd, gn, scale):
    d_shift = jnp.sum(dh, axis=0, keepdims=True)
    d_scale = jnp.sum(dh * xhat * gn, axis=0, keepdims=True)
    dy = dh * (1.0 + scale)
    d_gn = jnp.sum(dy * xhat, axis=0, keepdims=True)
    dxh = dy * gn
    dx = rstd * (dxh - xhat * jnp.mean(dxh * xhat, axis=-1, keepdims=True))
    return dx, d_shift, d_scale, d_gn


def _mod_row(mod_ref, row, k, d):
    return mod_ref[pl.ds(row, 1), k * d:(k + 1) * d]


class _Cfg:
    def __init__(self, b, s, n, d):
        self.B, self.S, self.N, self.D = b, s, n, d
        self.T = n + s
        self.F = 4 * d
        self.IN = GATE0 + 3 * d
        self.tm = 256 if (n % 256 == 0 and s % 256 == 0) else 128
        self.nT = self.T // self.tm
        self.nC = n // self.tm
        self.gw = 512 if d % 512 == 0 else 256
        assert GATE0 % self.gw == 0 and d % self.gw == 0 and b < 8 and self.T >= 3 * self.tm
        assert s % GRID_W == 0 and n % self.tm == 0 and s % self.tm == 0 and s >= self.tm + 2 * WINDOW
        assert d % (N_DEV * LANES) == 0


def _peer(k):
    x, y, c = lax.axis_index("x"), lax.axis_index("y"), lax.axis_index("c")
    px = x ^ ((k >> 2) & 1)
    py = y ^ ((k >> 1) & 1)
    pc = c ^ (k & 1)
    return (px, py, pc), 4 * px + 2 * py + pc


class _Exchange:
    def __init__(self, arrays, scatter):
        self.arrays = [a if isinstance(a, tuple) else (a, None) for a in arrays]
        self.scatter = scatter
        self.n = len(self.arrays)

    def operands(self):
        return [a for a, _ in self.arrays]

    def out_shapes(self):
        res = []
        for a, layer in self.arrays:
            shape = a.shape[1:] if (self.scatter or layer is not None) else a.shape
            res.append(jax.ShapeDtypeStruct((N_DEV,) + tuple(shape), a.dtype))
        return res

    def scratch(self):
        n = self.n * (N_DEV - 1)
        return [pltpu.SemaphoreType.DMA((n,)), pltpu.SemaphoreType.DMA((n,)), pltpu.SemaphoreType.DMA((self.n,))]

    def _copies(self, x_refs, out_refs, send_sems, recv_sems, local_sems, want):
        _, me = _peer(0)
        res = []
        for i, ((_, layer), x_ref, out_ref) in enumerate(zip(self.arrays, x_refs, out_refs)):
            if self.scatter:
                src_of = lambda d, x_ref=x_ref: x_ref.at[d]
            elif layer is not None:
                src_of = lambda d, x_ref=x_ref, layer=layer: x_ref.at[layer]
            else:
                src_of = lambda d, x_ref=x_ref: x_ref
            if want == "local":
                res.append(pltpu.make_async_copy(src_of(me), out_ref.at[me], local_sems.at[i]))
                continue
            for k in range(1, N_DEV):
                pos, idx = _peer(k)
                j = i * (N_DEV - 1) + k - 1
                common = dict(send_sem=send_sems.at[j], recv_sem=recv_sems.at[j], device_id=pos, device_id_type=pl.DeviceIdType.MESH)
                if want == "send":
                    res.append(pltpu.make_async_remote_copy(src_ref=src_of(idx), dst_ref=out_ref.at[me], **common))
                else:
                    res.append(pltpu.make_async_remote_copy(src_ref=src_of(me), dst_ref=out_ref.at[idx], **common))
        return res

    def start(self, *refs):
        for cp in self._copies(*refs, "local") + self._copies(*refs, "send"):
            cp.start()

    def wait(self, *refs):
        for cp in self._copies(*refs, "recv"):
            cp.wait_recv()
        for cp in self._copies(*refs, "send"):
            cp.wait_send()
        for cp in self._copies(*refs, "local"):
            cp.wait()

    def alone(self, name):
        n = self.n

        def body(*refs):
            args = (refs[:n], refs[n:2 * n], *refs[2 * n:])
            self.start(*args)
            self.wait(*args)

        any_spec = pl.BlockSpec(memory_space=pl.ANY)
        return pl.pallas_call(body, name=name, in_specs=[any_spec] * n, out_specs=[any_spec] * n,
                              out_shape=self.out_shapes(), scratch_shapes=self.scratch())(*self.operands())


def _pcall(exch):
    if exch is None:
        return pl.pallas_call

    def make(body, *, name, grid, in_specs, out_specs, out_shape, compiler_params, scratch_shapes=()):
        multi = isinstance(out_shape, (list, tuple))
        out_specs_l = list(out_specs) if multi else [out_specs]
        out_shape_l = list(out_shape) if multi else [out_shape]
        n_in, n_out, n_x, n_s = len(in_specs), len(out_specs_l), exch.n, len(scratch_shapes)

        def hosted(*refs):
            ins, x_refs = refs[:n_in], refs[n_in:n_in + n_x]
            o0 = n_in + n_x
            outs, xo_refs = refs[o0:o0 + n_out], refs[o0 + n_out:o0 + n_out + n_x]
            s0 = o0 + n_out + n_x
            own_scratch, sems = refs[s0:s0 + n_s], refs[s0 + n_s:]
            ids = [pl.program_id(i) for i in range(len(grid))]
            first = functools.reduce(jnp.logical_and, [i == 0 for i in ids])
            last = functools.reduce(jnp.logical_and, [i == g - 1 for i, g in zip(ids, grid)])

            @pl.when(first)
            def _():
                exch.start(x_refs, xo_refs, *sems)

            body(*ins, *outs, *own_scratch)

            @pl.when(last)
            def _():
                exch.wait(x_refs, xo_refs, *sems)

        any_spec = pl.BlockSpec(memory_space=pl.ANY)
        call = pl.pallas_call(
            hosted, name=name, grid=grid, in_specs=list(in_specs) + [any_spec] * n_x, out_specs=out_specs_l + [any_spec] * n_x,
            out_shape=out_shape_l + exch.out_shapes(), scratch_shapes=list(scratch_shapes) + exch.scratch(),
            compiler_params=compiler_params)
        return lambda *args: call(*args, *exch.operands())

    return make


def _adaln_fwd(cfg, cc_all, w_ada):
    d = cfg.D
    L, _, wa = w_ada.shape

    def body(c_ref, w_ref, o_ref):
        c = c_ref[...]
        a = (c * jax.nn.sigmoid(c)).astype(BF16)
        for l in range(L):
            m = _dot(a, w_ref[l].astype(BF16))
            for p in range(N_DEV):
                o_ref[p, l] = m[8 * p:8 * (p + 1)]

    return pl.pallas_call(
        body, name="adaln_fwd", grid=(1,),
        in_specs=[_full((8 * N_DEV, d)), _full((L, d, wa))],
        out_specs=_full((N_DEV, L, 8, wa)),
        out_shape=jax.ShapeDtypeStruct((N_DEV, L, 8, wa), F32), compiler_params=_params(1),
    )(cc_all, w_ada)


def _adaln_join(cfg, parts, b_ada):
    d = cfg.D
    _, L, _, wa = parts.shape

    def body(p_ref, b_ref, o_ref):
        for l in range(L):
            for j in range(N_DEV):
                o_ref[l, :, j * wa:(j + 1) * wa] = p_ref[j, l] + b_ref[l, :, j * wa:(j + 1) * wa]

    return pl.pallas_call(
        body, name="adaln_join", grid=(1,),
        in_specs=[_full((N_DEV, L, 8, wa)), _full((L, 1, 6 * d))],
        out_specs=_full((L, 8, 6 * d)),
        out_shape=jax.ShapeDtypeStruct((L, 8, 6 * d), F32), compiler_params=_params(1),
    )(parts, b_ada)


def _in_proj_fwd(cfg, l, x, gn, mod, w_in, exch=None):
    B, T, D, IN, tm, nC = cfg.B, cfg.T, cfg.D, cfg.IN, cfg.tm, cfg.nC

    def body(x_ref, gn_ref, mod_ref, w_ref, z_ref, h_ref):
        b, t = pl.program_id(0), pl.program_id(1)
        row = jnp.where(t < nC, B, b)
        h, _, _ = _modulate(x_ref[0], gn_ref[0], _mod_row(mod_ref, row, 0, D), _mod_row(mod_ref, row, 1, D))
        hb = h.astype(BF16)
        h_ref[0] = hb
        z_ref[0] = _dot(hb, w_ref[...])

    return _pcall(exch)(
        body, name="in_proj_fwd", grid=(B, cfg.nT),
        in_specs=[pl.BlockSpec((1, tm, D), lambda b, t: (b, t, 0)), _layer(l, D), _full((8, 6 * D)), _full((D, IN))],
        out_specs=[pl.BlockSpec((1, tm, IN), lambda b, t: (b, t, 0)), pl.BlockSpec((1, tm, D), lambda b, t: (b, t, 0))],
        out_shape=[jax.ShapeDtypeStruct((B, T, IN), F32), jax.ShapeDtypeStruct((B, T, D), BF16)],
        compiler_params=_params(2),
    )(x, gn, mod, w_in)


def _head_indicator():
    r = lax.broadcasted_iota(jnp.int32, (LANES, LANES), 0) // HEAD
    c = lax.broadcasted_iota(jnp.int32, (LANES, LANES), 1) // HEAD
    return jnp.where(r == c, 1.0, 0.0).astype(F32)


def _pair_swap(y):
    lane = lax.broadcasted_iota(jnp.int32, y.shape, 1)
    return jnp.where(lane % 2 == 0, pltpu.roll(y, LANES - 1, 1), pltpu.roll(y, 1, 1))


_QK_CHUNKS = (0, 1, 2, 3, 5, 6, 7, 8)
_Q_CHUNKS = (0, 1, 2, 5, 6, 7)


def _qknorm_fwd(cfg, l, z, gvec, cosf, sins):
    B, T, tm = cfg.B, cfg.T, cfg.tm

    def body(z_ref, g_ref, cos_ref, sin_ref, qa_ref, ka_ref, qc_ref, kc_ref):
        ind = _head_indicator()
        cos, sin = cos_ref[...], sin_ref[...]

        def chunk(c):
            x = z_ref[0, :, c * LANES:(c + 1) * LANES]
            ss = jnp.dot(x * x, ind, precision=HIGHEST, preferred_element_type=F32)
            y = x * lax.rsqrt(ss * (1.0 / HEAD) + EPS) * g_ref[0, :, c * LANES:(c + 1) * LANES]
            out = y * cos + _pair_swap(y) * sin
            return (out * QSCALE if c in _Q_CHUNKS else out).astype(BF16)

        qa_ref[0] = jnp.concatenate([chunk(0), chunk(1), chunk(2)], axis=-1)
        ka_ref[0] = chunk(3)
        qc_ref[0] = jnp.concatenate([chunk(5), chunk(6), chunk(7)], axis=-1)
        kc_ref[0] = chunk(8)

    row = lambda w: pl.BlockSpec((1, tm, w), lambda b, t: (b, t, 0))
    tab = pl.BlockSpec((tm, LANES), lambda b, t: (t, 0))
    return pl.pallas_call(
        body, name="qknorm_fwd", grid=(B, cfg.nT),
        in_specs=[row(QKV_W), _layer(l, QKV_W), tab, tab],
        out_specs=[row(QW), row(KVW), row(QW), row(KVW)],
        out_shape=[jax.ShapeDtypeStruct((B, T, w), BF16) for w in (QW, KVW, QW, KVW)],
        compiler_params=_params(2),
    )(z, gvec, cosf, sins)


def _attn_scores(cfg, tl, q, k_ref, v_ref, sink_ref, h, loc, window, sink):
    S, N, tq = cfg.S, cfg.N, cfg.tm
    hs = slice(h * HEAD, (h + 1) * HEAD)
    qs = jnp.concatenate([q[:, (GROUP * h + g) * HEAD:(GROUP * h + g + 1) * HEAD] for g in range(GROUP)], axis=0)
    k_ctx = k_ref[0, 0:N, :][:, hs]
    v_ctx = v_ref[0, 0:N, :].astype(BF16)[:, hs]
    s_c = _dot_nt(qs, k_ctx)
    m = jnp.max(s_c, axis=-1, keepdims=True)
    k_loc = v_loc = s_l = lo = None
    if loc:
        if window:
            W = tq + 2 * WINDOW
            lo = pl.multiple_of(jnp.clip(tl * tq - WINDOW, 0, S - W), LANES)
        else:
            W, lo = S, 0
        k_loc = k_ref[0, pl.ds(N + lo, W), :][:, hs]
        v_loc = v_ref[0, pl.ds(N + lo, W), :].astype(BF16)[:, hs]
        s_l = _dot_nt(qs, k_loc)
        if window:
            qpos = tl * tq + lax.broadcasted_iota(jnp.int32, (GROUP * tq, W), 0) % tq
            kpos = lo + lax.broadcasted_iota(jnp.int32, (GROUP * tq, W), 1)
            s_l = jnp.where(jnp.abs(qpos - kpos) <= WINDOW, s_l, NEG)
        m = jnp.maximum(m, jnp.max(s_l, axis=-1, keepdims=True))
    if sink:
        rowg = lax.broadcasted_iota(jnp.int32, (GROUP * tq, 1), 0) // tq
        sk = jnp.zeros((GROUP * tq, 1), F32)
        for g in range(GROUP):
            j = GROUP * h + g
            sk = jnp.where(rowg == g, sink_ref[0, 0:1, j:j + 1], sk)
        m = jnp.maximum(m, sk)
    e_c = jnp.exp(s_c - m)
    l = jnp.sum(e_c, axis=-1, keepdims=True)
    e_l = e_s = None
    if loc:
        e_l = jnp.exp(s_l - m)
        l = l + jnp.sum(e_l, axis=-1, keepdims=True)
    if sink:
        e_s = jnp.exp(sk - m)
        l = l + e_s
    inv = 1.0 / l
    p_c = e_c * inv
    p_l = e_l * inv if loc else None
    p_s = e_s * inv if sink else None
    return qs, p_c, p_l, p_s, k_ctx, v_ctx, k_loc, v_loc, lo


def _attn_fwd(cfg, l, q, k, z, vblock, sink8, *, window, sink, ctx_attend, name, exch=None):
    B, T, tq, nC = cfg.B, cfg.T, cfg.tm, cfg.nC

    def body(q_ref, k_ref, v_ref, sink_ref, o_ref):
        t = pl.program_id(1)

        def run(loc):
            q_t = q_ref[0]
            outs = [None] * N_QHEADS
            for h in range(N_KV):
                _, p_c, p_l, _, _, v_ctx, _, v_loc, _ = _attn_scores(cfg, t - nC, q_t, k_ref, v_ref, sink_ref, h, loc, window and loc, sink)
                o = _dot(p_c.astype(BF16), v_ctx)
                if loc:
                    o = o + _dot(p_l.astype(BF16), v_loc)
                for g in range(GROUP):
                    outs[GROUP * h + g] = o[g * tq:(g + 1) * tq]
            o_ref[0] = jnp.concatenate(outs, axis=-1).astype(BF16)

        pl.when(t >= nC)(functools.partial(run, True))
        if ctx_attend:
            pl.when(t < nC)(functools.partial(run, False))
        else:
            @pl.when(t < nC)
            def _():
                o_ref[0] = jnp.zeros((tq, QW), BF16)

    return _pcall(exch)(
        body, name=name, grid=(B, cfg.nT),
        in_specs=[pl.BlockSpec((1, tq, QW), lambda b, t: (b, t, 0)),
                  pl.BlockSpec((1, T, KVW), lambda b, t: (b, 0, 0)),
                  pl.BlockSpec((1, T, KVW), lambda b, t: (b, 0, vblock)),
                  pl.BlockSpec((1, 8, LANES), lambda b, t: (l, 0, 0))],
        out_specs=pl.BlockSpec((1, tq, QW), lambda b, t: (b, t, 0)),
        out_shape=jax.ShapeDtypeStruct((B, T, QW), BF16), compiler_params=_params(2),
    )(q, k, z, sink8)


def _pool_geometry(cfg, t):
    tm, N, T, nC = cfg.tm, cfg.N, cfg.T, cfg.nC
    r0 = pl.multiple_of(t * tm, tm)
    isctx = t < nC
    seg_lo = jnp.where(isctx, 0, N)
    seg_hi = jnp.where(isctx, N, T)
    k0 = pl.multiple_of(jnp.clip(t * tm - tm, 0, T - 3 * tm), tm)
    return r0, seg_lo, seg_hi, k0


def _pool_count(pos, h, seg_lo, seg_hi):
    return jnp.maximum(jnp.minimum(pos + h, seg_hi) - jnp.maximum(pos - h, seg_lo), 1).astype(F32)


def _split_bf16(x):
    hi = x.astype(BF16)
    return hi, (x - hi.astype(F32)).astype(BF16)


def _pool_fwd(cfg, l, z, wp, ps):
    B, T, tm = cfg.B, cfg.T, cfg.tm

    def body(u_ref, wp_ref, ps_ref, ob_ref, pooled_ref):
        t = pl.program_id(1)
        r0, seg_lo, seg_hi, k0 = _pool_geometry(cfg, t)
        hi, lo = _split_bf16(u_ref[0, pl.ds(k0, 3 * tm), :])
        rr = r0 + lax.broadcasted_iota(jnp.int32, (tm, 3 * tm), 0)
        cc = k0 + lax.broadcasted_iota(jnp.int32, (tm, 3 * tm), 1)
        diff = cc - rr
        inseg = (cc >= seg_lo) & (cc < seg_hi)
        rcol = r0 + lax.broadcasted_iota(jnp.int32, (tm, 1), 0)
        group = lax.broadcasted_iota(jnp.int32, (tm, POOL_W), 1) // HEAD
        acc = jnp.zeros((tm, POOL_W), F32)
        for gi, w in enumerate(POOL_WINDOWS):
            h = w // 2
            band = jnp.where((diff >= -h) & (diff <= h - 1) & inseg, 1.0, 0.0).astype(BF16)
            tot = _dot(band, hi) + _dot(band, lo)
            acc = jnp.where(group == gi, tot / _pool_count(rcol, h, seg_lo, seg_hi), acc)
        pooled = (acc - u_ref[0, pl.ds(r0, tm), :]).astype(BF16)
        pooled_ref[0] = pooled
        ob_ref[0] = (_dot(pooled, wp_ref[0]) * ps_ref[0]).astype(BF16)

    row = pl.BlockSpec((1, tm, POOL_W), lambda b, t: (b, t, 0))
    return pl.pallas_call(
        body, name="pool_fwd", grid=(B, cfg.nT),
        in_specs=[pl.BlockSpec((1, T, POOL_W), lambda b, t: (b, 0, QKV_W // POOL_W)),
                  pl.BlockSpec((1, POOL_W, POOL_W), lambda b, t: (l, 0, 0)), _layer(l, POOL_W)],
        out_specs=[row, row],
        out_shape=[jax.ShapeDtypeStruct((B, T, POOL_W), BF16)] * 2, compiler_params=_params(2),
    )(z, wp, ps)


def _gate_specs(cfg):
    tm, gw = cfg.tm, cfg.gw
    first = GATE0 // gw
    return [pl.BlockSpec((1, tm, gw), functools.partial(lambda b, t, j: (b, t, j), j=first + i)) for i in range(3 * cfg.D // gw)]


def _read_gates(cfg, gate_refs):
    per = cfg.D // cfg.gw
    return [jnp.concatenate([gate_refs[k * per + i][0] for i in range(per)], axis=-1) for k in range(3)]


def _merge_fwd(cfg, x, oa, ob, oc, z, mod, wa, wb, wc, wo, *, ctx_active, exch=None):
    B, T, D, tm, nC = cfg.B, cfg.T, cfg.D, cfg.tm, cfg.nC
    ng = 3 * D // cfg.gw

    def body(x_ref, oa_ref, ob_ref, oc_ref, *rest):
        gate_refs = rest[:ng]
        mod_ref, wa_ref, wb_ref, wc_ref, wo_ref, x1_ref, mgo_ref = rest[ng:]
        b, t = pl.program_id(0), pl.program_id(1)

        def compute():
            row = jnp.where(t < nC, B, b)
            ga, gb, gc = _read_gates(cfg, gate_refs)
            y = (jax.nn.sigmoid(ga) * _dot(oa_ref[0], wa_ref[...])
                 + jax.nn.sigmoid(gb) * _dot(ob_ref[0], wb_ref[...])
                 + jax.nn.sigmoid(gc) * _dot(oc_ref[0], wc_ref[...]))
            mo = _dot(y.astype(BF16), wo_ref[...])
            mgo_ref[0] = mo.astype(BF16)
            x1_ref[0] = x_ref[0] + _mod_row(mod_ref, row, 2, D) * mo

        if ctx_active:
            compute()
        else:
            pl.when(t >= nC)(compute)

            @pl.when(t < nC)
            def _():
                mgo_ref[0] = jnp.zeros((tm, D), BF16)
                x1_ref[0] = x_ref[0]

    row = lambda w: pl.BlockSpec((1, tm, w), lambda b, t: (b, t, 0))
    return _pcall(exch)(
        body, name="merge_fwd", grid=(B, cfg.nT),
        in_specs=[row(D), row(QW), row(POOL_W), row(QW)] + _gate_specs(cfg)
        + [_full((8, 6 * D)), _full((QW, D)), _full((POOL_W, D)), _full((QW, D)), _full((D, D))],
        out_specs=[row(D), row(D)],
        out_shape=[jax.ShapeDtypeStruct((B, T, D), F32), jax.ShapeDtypeStruct((B, T, D), BF16)],
        compiler_params=_params(2),
    )(x, oa, ob, oc, *([z] * ng), mod, wa, wb, wc, wo)


def _w1_apply(hb, w1_ref):
    return jnp.concatenate([_dot(hb, w1_ref[d]) for d in range(N_DEV)], axis=-1)


def _mlp_fwd(cfg, l, x1, gn, mod, w1, w2, *, ctx_active, exch=None):
    B, T, D, F, tm, nC = cfg.B, cfg.T, cfg.D, cfg.F, cfg.tm, cfg.nC

    def body(x_ref, gn_ref, mod_ref, w1_ref, w2_ref, x2_ref, mo_ref):
        b, t = pl.program_id(0), pl.program_id(1)

        def compute():
            row = jnp.where(t < nC, B, b)
            x = x_ref[0]
            h, _, _ = _modulate(x, gn_ref[0], _mod_row(mod_ref, row, 3, D), _mod_row(mod_ref, row, 4, D))
            a = jnp.maximum(_w1_apply(h.astype(BF16), w1_ref), 0.0)
            mo = _dot((a * a).astype(BF16), w2_ref[...])
            mo_ref[0] = mo.astype(BF16)
            x2_ref[0] = x + _mod_row(mod_ref, row, 5, D) * mo

        if ctx_active:
            compute()
        else:
            pl.when(t >= nC)(compute)

            @pl.when(t < nC)
            def _():
                mo_ref[0] = jnp.zeros((tm, D), BF16)
                x2_ref[0] = x_ref[0]

    row = pl.BlockSpec((1, tm, D), lambda b, t: (b, t, 0))
    return _pcall(exch)(
        body, name="mlp_fwd", grid=(B, cfg.nT),
        in_specs=[row, _layer(l, D), _full((8, 6 * D)), _full((N_DEV, D, F // N_DEV)), _full((F, D))],
        out_specs=[row, row],
        out_shape=[jax.ShapeDtypeStruct((B, T, D), F32), jax.ShapeDtypeStruct((B, T, D), BF16)],
        compiler_params=_params(2),
    )(x1, gn, mod, w1, w2)


def _loss_fwd_bwd(cfg, x2, target):
    B, T, D, tm, nC = cfg.B, cfg.T, cfg.D, cfg.tm, cfg.nC

    def body(x_ref, tgt_ref, dx_ref, sse_ref):
        b, t = pl.program_id(0), pl.program_id(1)

        @pl.when((b == 0) & (t == 0))
        def _():
            sse_ref[...] = jnp.zeros((8, LANES), F32)

        @pl.when(t < nC)
        def _():
            dx_ref[0] = jnp.zeros((tm, D), F32)

        @pl.when(t >= nC)
        def _():
            err = x_ref[0] - tgt_ref[0]
            dx_ref[0] = err * (1.0 / D)
            sse_ref[...] += jnp.sum(err * err)

    return pl.pallas_call(
        body, name="loss", grid=(B, cfg.nT),
        in_specs=[pl.BlockSpec((1, tm, D), lambda b, t: (b, t, 0)),
                  pl.BlockSpec((1, tm, D), lambda b, t: (b, jnp.maximum(t - nC, 0), 0))],
        out_specs=[pl.BlockSpec((1, tm, D), lambda b, t: (b, t, 0)), _full((8, LANES))],
        out_shape=[jax.ShapeDtypeStruct((B, T, D), F32), jax.ShapeDtypeStruct((8, LANES), F32)],
        compiler_params=_params(2),
    )(x2, target)


def _acc_init(refs):
    b, t = pl.program_id(0), pl.program_id(1)

    @pl.when((b == 0) & (t == 0))
    def _():
        for ref in refs:
            ref[...] = jnp.zeros(ref.shape, ref.dtype)


def _mlp_bwd(cfg, l, x1, dx2, mo, gn, mod, w1, w2, *, ctx_active, exch=None):
    B, T, D, F, tm, nC = cfg.B, cfg.T, cfg.D, cfg.F, cfg.tm, cfg.nC
    ws = F // N_DEV

    def body(x_ref, dx_ref, mo_ref, gn_ref, mod_ref, w1_ref, w2_ref, dx1_ref, h_ref, r_ref, da_ref, dout_ref, dmod_ref, dgn_ref):
        b, t = pl.program_id(0), pl.program_id(1)
        _acc_init([dmod_ref, dgn_ref])

        def compute():
            row = jnp.where(t < nC, B, b)
            gn = gn_ref[0]
            scale = _mod_row(mod_ref, row, 4, D)
            h, xhat, rstd = _modulate(x_ref[0], gn, _mod_row(mod_ref, row, 3, D), scale)
            hb = h.astype(BF16)
            a = jnp.maximum(_w1_apply(hb, w1_ref), 0.0)
            dx = dx_ref[0]
            dout = (dx * _mod_row(mod_ref, row, 5, D)).astype(BF16)
            da = (_dot_nt(dout, w2_ref[...]) * (2.0 * a)).astype(BF16)
            dh = _dot_nt(da[:, 0:ws], w1_ref[0])
            for d in range(1, N_DEV):
                dh = dh + _dot_nt(da[:, d * ws:(d + 1) * ws], w1_ref[d])
            dxn, d_shift, d_scale, d_gn = _modulate_bwd(dh, xhat, rstd, gn, scale)
            dx1_ref[0] = dx + dxn
            h_ref[0] = hb
            r_ref[0] = (a * a).astype(BF16)
            da_ref[0] = da
            dout_ref[0] = dout
            d_gate = jnp.sum(dx * mo_ref[0].astype(F32), axis=0, keepdims=True)
            dmod_ref[pl.ds(row, 1), :] += jnp.concatenate([d_shift, d_scale, d_gate], axis=-1)
            dgn_ref[0:1, :] += d_gn

        if ctx_active:
            compute()
        else:
            pl.when(t >= nC)(compute)

            @pl.when(t < nC)
            def _():
                dx1_ref[0] = dx_ref[0]
                h_ref[0] = jnp.zeros((tm, D), BF16)
                r_ref[0] = jnp.zeros((tm, F), BF16)
                da_ref[0] = jnp.zeros((tm, F), BF16)
                dout_ref[0] = jnp.zeros((tm, D), BF16)

    row = lambda w: pl.BlockSpec((1, tm, w), lambda b, t: (b, t, 0))
    sds = lambda w, dt: jax.ShapeDtypeStruct((B, T, w), dt)
    return _pcall(exch)(
        body, name="mlp_bwd", grid=(B, cfg.nT),
        in_specs=[row(D), row(D), row(D), _layer(l, D), _full((8, 6 * D)), _full((N_DEV, D, ws)), _full((F, D))],
        out_specs=[row(D), row(D), row(F), row(F), row(D), _full((8, 3 * D)), _full((8, D))],
        out_shape=[sds(D, F32), sds(D, BF16), sds(F, BF16), sds(F, BF16), sds(D, BF16),
                   jax.ShapeDtypeStruct((8, 3 * D), F32), jax.ShapeDtypeStruct((8, D), F32)],
        compiler_params=_params(2),
    )(x1, dx2, mo, gn, mod, w1, w2)


def _matmul_tn(a, g, name, *, by_shard):
    R, Ka = a.shape
    Ng = g.shape[1]
    tr = next(c for c in (512, 256, 128, 64, 32, 16, 8) if R % c == 0)
    tka = Ka if Ka <= 1024 else 1024
    if by_shard:
        ws = Ng // N_DEV
        per = next(c for c in (8, 4, 2, 1) if c * ws <= 1152 or c == 1)
        tn = per * ws
    else:
        tn = next(c for c in (1152, 1024, 768, 512, 384, 256, 128) if Ng % c == 0)
    assert Ka % tka == 0 and tn % LANES == 0
    nr = R // tr

    def body(a_ref, g_ref, o_ref, acc_ref):
        r = pl.program_id(2)

        @pl.when(r == 0)
        def _():
            acc_ref[...] = jnp.zeros(acc_ref.shape, F32)

        acc_ref[...] += _dot_tn(a_ref[...], g_ref[...])

        @pl.when(r == nr - 1)
        def _():
            if by_shard:
                for d in range(per):
                    o_ref[d] = acc_ref[:, d * ws:(d + 1) * ws].astype(BF16)
            else:
                o_ref[...] = acc_ref[...].astype(BF16)

    if by_shard:
        out_spec = pl.BlockSpec((per, tka, ws), lambda i, j, r: (j, i, 0))
        out_shape = jax.ShapeDtypeStruct((N_DEV, Ka, ws), BF16)
    else:
        out_spec = pl.BlockSpec((tka, tn), lambda i, j, r: (i, j))
        out_shape = jax.ShapeDtypeStruct((Ka, Ng), BF16)
    return pl.pallas_call(
        body, name=name, grid=(Ka // tka, Ng // tn, nr),
        in_specs=[pl.BlockSpec((tr, tka), lambda i, j, r: (r, i)), pl.BlockSpec((tr, tn), lambda i, j, r: (r, j))],
        out_specs=out_spec, out_shape=out_shape, scratch_shapes=[pltpu.VMEM((tka, tn), F32)], compiler_params=_params(3),
    )(a, g)


def _merge_bwd(cfg, dx1, mgo, oa, ob, oc, z, mod, wa, wb, wc, wo, *, ctx_active, exch=None):
    B, T, D, tm, nC = cfg.B, cfg.T, cfg.D, cfg.tm, cfg.nC
    ng = 3 * D // cfg.gw

    def body(dx_ref, mgo_ref, oa_ref, ob_ref, oc_ref, *rest):
        gate_refs = rest[:ng]
        (mod_ref, wa_ref, wb_ref, wc_ref, wo_ref,
         doa_ref, dob_ref, doc_ref, dpa_ref, dpb_ref, dpc_ref, y_ref, dmo_ref, dzg_ref, dg1_ref) = rest[ng:]
        b, t = pl.program_id(0), pl.program_id(1)
        _acc_init([dg1_ref])

        def compute():
            row = jnp.where(t < nC, B, b)
            dx = dx_ref[0]
            dg1_ref[pl.ds(row, 1), :] += jnp.sum(dx * mgo_ref[0].astype(F32), axis=0, keepdims=True)
            dmo = (dx * _mod_row(mod_ref, row, 2, D)).astype(BF16)
            dmo_ref[0] = dmo
            dy = _dot_nt(dmo, wo_ref[...])
            gates = _read_gates(cfg, gate_refs)
            y = jnp.zeros((tm, D), F32)
            dgs = []
            for gate, o_ref, w_ref, do_ref, dp_ref in ((gates[0], oa_ref, wa_ref, doa_ref, dpa_ref),
                                                      (gates[1], ob_ref, wb_ref, dob_ref, dpb_ref),
                                                      (gates[2], oc_ref, wc_ref, doc_ref, dpc_ref)):
                s = jax.nn.sigmoid(gate)
                p = _dot(o_ref[0], w_ref[...])
                y = y + s * p
                dp = (dy * s).astype(BF16)
                dp_ref[0] = dp
                do_ref[0] = _dot_nt(dp, w_ref[...]).astype(BF16)
                dgs.append((dy * p * s * (1.0 - s)).astype(BF16))
            y_ref[0] = y.astype(BF16)
            dzg_ref[0] = jnp.concatenate(dgs, axis=-1)

        if ctx_active:
            compute()
        else:
            pl.when(t >= nC)(compute)

            @pl.when(t < nC)
            def _():
                for ref in (doa_ref, dob_ref, doc_ref, dpa_ref, dpb_ref, dpc_ref, y_ref, dmo_ref, dzg_ref):
                    ref[...] = jnp.zeros(ref.shape, ref.dtype)

    row = lambda w: pl.BlockSpec((1, tm, w), lambda b, t: (b, t, 0))
    sds = lambda w: jax.ShapeDtypeStruct((B, T, w), BF16)
    return _pcall(exch)(
        body, name="merge_bwd", grid=(B, cfg.nT),
        in_specs=[row(D), row(D), row(QW), row(POOL_W), row(QW)] + _gate_specs(cfg)
        + [_full((8, 6 * D)), _full((QW, D)), _full((POOL_W, D)), _full((QW, D)), _full((D, D))],
        out_specs=[row(QW), row(POOL_W), row(QW), row(D), row(D), row(D), row(D), row(D), row(3 * D), _full((8, D))],
        out_shape=[sds(QW), sds(POOL_W), sds(QW), sds(D), sds(D), sds(D), sds(D), sds(D), sds(3 * D),
                   jax.ShapeDtypeStruct((8, D), F32)],
        compiler_params=_params(2),
    )(dx1, mgo, oa, ob, oc, *([z] * ng), mod, wa, wb, wc, wo)


def _attn_bwd(cfg, l, q, k, z, vblock, sink8, do, *, window, sink, ctx_attend, name, exch=None):
    B, S, N, T, tq, nC = cfg.B, cfg.S, cfg.N, cfg.T, cfg.tm, cfg.nC

    def body(q_ref, k_ref, v_ref, sink_ref, do_ref, dq_ref, dk_ref, dv_ref, dsink_ref):
        b, t = pl.program_id(0), pl.program_id(1)
        _acc_init([dsink_ref])

        @pl.when(t == 0)
        def _():
            dk_ref[...] = jnp.zeros(dk_ref.shape, F32)
            dv_ref[...] = jnp.zeros(dv_ref.shape, F32)

        def run(loc):
            q_t = q_ref[0]
            do_t = do_ref[0]
            dqs = [None] * N_QHEADS
            dk_c, dv_c, dk_l, dv_l = [], [], [], []
            dsink_row = jnp.zeros((1, LANES), F32)
            lane = lax.broadcasted_iota(jnp.int32, (1, LANES), 1)
            lo = 0
            for h in range(N_KV):
                qs, p_c, p_l, p_s, k_ctx, v_ctx, k_loc, v_loc, lo = _attn_scores(
                    cfg, t - nC, q_t, k_ref, v_ref, sink_ref, h, loc, window and loc, sink)
                dos = jnp.concatenate([do_t[:, (GROUP * h + g) * HEAD:(GROUP * h + g + 1) * HEAD] for g in range(GROUP)], axis=0)
                dp_c = _dot_nt(dos, v_ctx)
                delta = jnp.sum(p_c * dp_c, axis=-1, keepdims=True)
                if loc:
                    dp_l = _dot_nt(dos, v_loc)
                    delta = delta + jnp.sum(p_l * dp_l, axis=-1, keepdims=True)
                ds_c = (p_c * (dp_c - delta)).astype(BF16)
                dq = _dot(ds_c, k_ctx)
                dk_c.append(_dot_tn(ds_c, qs))
                dv_c.append(_dot_tn(p_c.astype(BF16), dos))
                if loc:
                    ds_l = (p_l * (dp_l - delta)).astype(BF16)
                    dq = dq + _dot(ds_l, k_loc)
                    dk_l.append(_dot_tn(ds_l, qs))
                    dv_l.append(_dot_tn(p_l.astype(BF16), dos))
                if sink:
                    dsk = -p_s * delta
                    for g in range(GROUP):
                        tot = jnp.sum(dsk[g * tq:(g + 1) * tq], axis=0, keepdims=True)
                        dsink_row = dsink_row + jnp.where(lane == GROUP * h + g, tot, 0.0)
                for g in range(GROUP):
                    dqs[GROUP * h + g] = dq[g * tq:(g + 1) * tq] * QSCALE
            dq_ref[0] = jnp.concatenate(dqs, axis=-1)
            dk_ref[0, 0:N, :] += jnp.concatenate(dk_c, axis=-1)
            dv_ref[0, 0:N, :] += jnp.concatenate(dv_c, axis=-1)
            if loc:
                W = tq + 2 * WINDOW if window else S
                dk_ref[0, pl.ds(N + lo, W), :] += jnp.concatenate(dk_l, axis=-1)
                dv_ref[0, pl.ds(N + lo, W), :] += jnp.concatenate(dv_l, axis=-1)
            if sink:
                dsink_ref[0:1, :] += dsink_row

        pl.when(t >= nC)(functools.partial(run, True))
        if ctx_attend:
            pl.when(t < nC)(functools.partial(run, False))
        else:
            @pl.when(t < nC)
            def _():
                dq_ref[0] = jnp.zeros((tq, QW), F32)

    kv = pl.BlockSpec((1, T, KVW), lambda b, t: (b, 0, 0))
    qrow = pl.BlockSpec((1, tq, QW), lambda b, t: (b, t, 0))
    return _pcall(exch)(
        body, name=name, grid=(B, cfg.nT),
        in_specs=[qrow, kv, pl.BlockSpec((1, T, KVW), lambda b, t: (b, 0, vblock)),
                  pl.BlockSpec((1, 8, LANES), lambda b, t: (l, 0, 0)), qrow],
        out_specs=[qrow, kv, kv, _full((8, LANES))],
        out_shape=[jax.ShapeDtypeStruct((B, T, QW), F32), jax.ShapeDtypeStruct((B, T, KVW), F32),
                   jax.ShapeDtypeStruct((B, T, KVW), F32), jax.ShapeDtypeStruct((8, LANES), F32)],
        compiler_params=_params(2),
    )(q, k, z, sink8, do)


def _qknorm_bwd(cfg, l, z, gvec, cosf, sins, dqa, dka, dva, dqc, dkc, dvc):
    B, T, tm = cfg.B, cfg.T, cfg.tm

    def body(z_ref, g_ref, cos_ref, sin_ref, dqa_ref, dka_ref, dva_ref, dqc_ref, dkc_ref, dvc_ref, dz_ref, dg_ref):
        _acc_init([dg_ref])
        ind = _head_indicator()
        cos, sin = cos_ref[...], sin_ref[...]
        dqa_t, dqc_t = dqa_ref[0], dqc_ref[0]
        douts = {0: dqa_t[:, 0:128], 1: dqa_t[:, 128:256], 2: dqa_t[:, 256:384], 3: dka_ref[0],
                 5: dqc_t[:, 0:128], 6: dqc_t[:, 128:256], 7: dqc_t[:, 256:384], 8: dkc_ref[0]}
        pieces = []
        dgs = []
        for c in range(QKV_W // LANES):
            if c not in douts:
                pieces.append(dva_ref[0] if c == 4 else dvc_ref[0])
                dgs.append(jnp.zeros((1, LANES), F32))
                continue
            x = z_ref[0, :, c * LANES:(c + 1) * LANES]
            g = g_ref[0, :, c * LANES:(c + 1) * LANES]
            ss = jnp.dot(x * x, ind, precision=HIGHEST, preferred_element_type=F32)
            rstd = lax.rsqrt(ss * (1.0 / HEAD) + EPS)
            n = x * rstd
            dout = douts[c]
            dy = dout * cos + _pair_swap(dout * sin)
            dgs.append(jnp.sum(dy * n, axis=0, keepdims=True))
            dn = dy * g
            mean = jnp.dot(dn * n, ind, precision=HIGHEST, preferred_element_type=F32) * (1.0 / HEAD)
            pieces.append(rstd * (dn - n * mean))
        dz_ref[0] = jnp.concatenate(pieces, axis=-1).astype(BF16)
        dg_ref[0:1, :] += jnp.concatenate(dgs, axis=-1)

    row = lambda w: pl.BlockSpec((1, tm, w), lambda b, t: (b, t, 0))
    tab = pl.BlockSpec((tm, LANES), lambda b, t: (t, 0))
    return pl.pallas_call(
        body, name="qknorm_bwd", grid=(B, cfg.nT),
        in_specs=[row(QKV_W), _layer(l, QKV_W), tab, tab, row(QW), row(KVW), row(KVW), row(QW), row(KVW), row(KVW)],
        out_specs=[row(QKV_W), _full((8, QKV_W))],
        out_shape=[jax.ShapeDtypeStruct((B, T, QKV_W), BF16), jax.ShapeDtypeStruct((8, QKV_W), F32)],
        compiler_params=_params(2),
    )(z, gvec, cosf, sins, dqa, dka, dva, dqc, dkc, dvc)


def _pool_bwd(cfg, l, dob, pooled, wp, ps):
    B, T, tm = cfg.B, cfg.T, cfg.tm

    def body(dob_ref, pooled_ref, wp_ref, ps_ref, du_ref, dwp_ref, dps_ref):
        t = pl.program_id(1)
        _acc_init([dwp_ref, dps_ref])
        r0, seg_lo, seg_hi, k0 = _pool_geometry(cfg, t)
        ps = ps_ref[0]
        wp = wp_ref[0]
        dmix = dob_ref[0, pl.ds(r0, tm), :].astype(F32)
        pooled = pooled_ref[0]
        dps_ref[0:1, :] += jnp.sum(dmix * _dot(pooled, wp), axis=0, keepdims=True)
        dpm = (dmix * ps).astype(BF16)
        dwp_ref[...] += _dot_tn(pooled, dpm)
        dpooled_t = _dot_nt(dpm, wp)
        dpm_w = (dob_ref[0, pl.ds(k0, 3 * tm), :].astype(F32) * ps).astype(BF16)
        dpooled_w = _dot_nt(dpm_w, wp)
        rr = r0 + lax.broadcasted_iota(jnp.int32, (tm, 3 * tm), 0)
        cc = k0 + lax.broadcasted_iota(jnp.int32, (tm, 3 * tm), 1)
        diff = rr - cc
        inseg = (cc >= seg_lo) & (cc < seg_hi)
        ccol = k0 + lax.broadcasted_iota(jnp.int32, (3 * tm, 1), 0)
        group = lax.broadcasted_iota(jnp.int32, (tm, POOL_W), 1) // HEAD
        acc = jnp.zeros((tm, POOL_W), F32)
        for gi, w in enumerate(POOL_WINDOWS):
            h = w // 2
            band_t = jnp.where((diff >= -h) & (diff <= h - 1) & inseg, 1.0, 0.0).astype(BF16)
            hi, lo = _split_bf16(dpooled_w / _pool_count(ccol, h, seg_lo, seg_hi))
            acc = jnp.where(group == gi, _dot(band_t, hi) + _dot(band_t, lo), acc)
        du_ref[0] = (acc - dpooled_t).astype(BF16)

    row = pl.BlockSpec((1, tm, POOL_W), lambda b, t: (b, t, 0))
    return pl.pallas_call(
        body, name="pool_bwd", grid=(B, cfg.nT),
        in_specs=[pl.BlockSpec((1, T, POOL_W), lambda b, t: (b, 0, 0)), row,
                  pl.BlockSpec((1, POOL_W, POOL_W), lambda b, t: (l, 0, 0)), _layer(l, POOL_W)],
        out_specs=[row, _full((POOL_W, POOL_W)), _full((8, POOL_W))],
        out_shape=[jax.ShapeDtypeStruct((B, T, POOL_W), BF16), jax.ShapeDtypeStruct((POOL_W, POOL_W), F32),
                   jax.ShapeDtypeStruct((8, POOL_W), F32)],
        compiler_params=_params(2),
    )(dob, pooled, wp, ps)


def _in_proj_bwd(cfg, l, dzq, du, dzg, w_in, x, dx1, gn, mod, *, latent_only, exch=None):
    B, S, T, D, IN, tm, nC = cfg.B, cfg.S, cfg.T, cfg.D, cfg.IN, cfg.tm, cfg.nC

    def body(dzq_ref, du_ref, dzg_ref, w_ref, x_ref, dx1_ref, gn_ref, mod_ref, dx0_ref, dz_ref, dmod_ref, dgn_ref):
        b, t = pl.program_id(0), pl.program_id(1)
        _acc_init([dmod_ref, dgn_ref])
        row = jnp.where(t < nC, B, b)
        dz = jnp.concatenate([dzq_ref[0], du_ref[0], dzg_ref[0]], axis=-1)
        dz_ref[0] = dz
        dh = _dot_nt(dz, w_ref[...])
        gn = gn_ref[0]
        scale = _mod_row(mod_ref, row, 1, D)
        _, xhat, rstd = _modulate(x_ref[0], gn, _mod_row(mod_ref, row, 0, D), scale)
        dxn, d_shift, d_scale, d_gn = _modulate_bwd(dh, xhat, rstd, gn, scale)
        dx0_ref[0] = dx1_ref[0] + dxn
        dmod_ref[pl.ds(row, 1), :] += jnp.concatenate([d_shift, d_scale], axis=-1)
        dgn_ref[0:1, :] += d_gn

    row = lambda w: pl.BlockSpec((1, tm, w), lambda b, t: (b, t, 0))
    if latent_only:
        dx0_spec = pl.BlockSpec((1, tm, D), lambda b, t: (b, jnp.maximum(t - nC, 0), 0))
        dx0_shape = jax.ShapeDtypeStruct((B, S, D), F32)
    else:
        dx0_spec, dx0_shape = row(D), jax.ShapeDtypeStruct((B, T, D), F32)
    return _pcall(exch)(
        body, name="in_proj_bwd", grid=(B, cfg.nT),
        in_specs=[row(QKV_W), row(POOL_W), row(3 * D), _full((D, IN)), row(D), row(D), _layer(l, D), _full((8, 6 * D))],
        out_specs=[dx0_spec, row(IN), _full((8, 2 * D)), _full((8, D))],
        out_shape=[dx0_shape, jax.ShapeDtypeStruct((B, T, IN), BF16),
                   jax.ShapeDtypeStruct((8, 2 * D), F32), jax.ShapeDtypeStruct((8, D), F32)],
        compiler_params=_params(2),
    )(dzq, du, dzg, w_in, x, dx1, gn, mod)


def _adaln_bwd(cfg, l, cc_all, dm_all, w_ada):
    d, B = cfg.D, cfg.B
    wa = w_ada.shape[2]

    def body(c_ref, dm_ref, w_ref, dw_ref, dc_ref):
        c = c_ref[...]
        s = jax.nn.sigmoid(c)
        dmb = dm_ref[...].astype(BF16)
        dw_ref[...] = _dot_tn((c * s).astype(BF16), dmb)
        dc = _dot_nt(dmb, w_ref[0].astype(BF16)) * (s * (1.0 + c * (1.0 - s)))
        is_ctx = lax.broadcasted_iota(jnp.int32, (8 * N_DEV, 1), 0) % 8 == B
        dc_ref[...] = jnp.broadcast_to(jnp.sum(jnp.where(is_ctx, dc, 0.0), axis=0, keepdims=True), (8, d))

    return pl.pallas_call(
        body, name="adaln_bwd", grid=(1,),
        in_specs=[_full((8 * N_DEV, d)), _full((8 * N_DEV, wa)), pl.BlockSpec((1, d, wa), lambda *_: (l, 0, 0))],
        out_specs=[_full((d, wa)), _full((8, d))],
        out_shape=[jax.ShapeDtypeStruct((d, wa), F32), jax.ShapeDtypeStruct((8, d), F32)],
        compiler_params=_params(1),
    )(cc_all, dm_all, w_ada)


def _dmod_pack(cfg, dmod_in, dg1, dmod_mlp):
    d = cfg.D
    wa = 6 * d // N_DEV

    def body(din_ref, dg1_ref, dmlp_ref, o_ref, db_ref):
        dm = jnp.concatenate([din_ref[...], dg1_ref[...], dmlp_ref[...]], axis=-1)
        for j in range(N_DEV):
            o_ref[j] = dm[:, j * wa:(j + 1) * wa]
        db_ref[...] = jnp.broadcast_to(jnp.sum(dm, axis=0, keepdims=True), (8, 6 * d))

    return pl.pallas_call(
        body, name="dmod_pack", grid=(1,),
        in_specs=[_full((8, 2 * d)), _full((8, d)), _full((8, 3 * d))],
        out_specs=[_full((N_DEV, 8, wa)), _full((8, 6 * d))],
        out_shape=[jax.ShapeDtypeStruct((N_DEV, 8, wa), F32), jax.ShapeDtypeStruct((8, 6 * d), F32)],
        compiler_params=_params(1),
    )(dmod_in, dg1, dmod_mlp)


def _adam_update(g, w, m, v):
    bc1 = 1.0 - ADAM_B1 ** ADAM_STEP
    bc2 = 1.0 - ADAM_B2 ** ADAM_STEP
    m2 = ADAM_B1 * m + (1.0 - ADAM_B1) * g
    v2 = ADAM_B2 * v + (1.0 - ADAM_B2) * (g * g)
    delta = -ADAM_LR * ((m2 / bc1) / (jnp.sqrt(v2 / bc2) + ADAM_EPS) + ADAM_WD * w)
    return delta, m2, v2


def _sum_parts(p_ref):
    g = p_ref[0].astype(F32)
    for d in range(1, p_ref.shape[0]):
        g = g + p_ref[d].astype(F32)
    return g


def _adamw_sharded(parts, w, m, v, name):
    L, K, W = w.shape
    P = parts[0].shape[0]
    tk = next(c for c in (256, 128, 64, 32, 16, 8) if K % c == 0)

    def body(*refs):
        p_refs = refs[:L]
        w_ref, m_ref, v_ref, g_ref, d_ref, m2_ref, v2_ref = refs[L:]
        layer = pl.program_id(0)

        def run(p_ref):
            g = _sum_parts(p_ref)
            delta, m2, v2 = _adam_update(g, w_ref[0], m_ref[0], v_ref[0])
            g_ref[0] = g
            d_ref[0] = delta
            m2_ref[0] = m2
            v2_ref[0] = v2

        for li in range(L):
            pl.when(layer == li)(functools.partial(run, p_refs[li]))

    blk = pl.BlockSpec((1, tk, W), lambda l, i: (l, i, 0))
    part_spec = lambda li: pl.BlockSpec((P, tk, W), lambda l, i: (0, jnp.where(l == li, i, 0), 0))
    return pl.pallas_call(
        body, name=name, grid=(L, K // tk),
        in_specs=[part_spec(li) for li in range(L)] + [blk, blk, blk],
        out_specs=[blk] * 4, out_shape=[jax.ShapeDtypeStruct((L, K, W), F32)] * 4,
        compiler_params=_params(2),
    )(*parts, w, m, v)


def _adamw_packed(parts, w, m, v, name):
    rows = w.shape[0]
    tr = next(c for c in (256, 128, 64, 32, 16, 8) if rows % c == 0)

    def body(p_ref, w_ref, m_ref, v_ref, g_ref, d_ref, m2_ref, v2_ref):
        g = _sum_parts(p_ref)
        delta, m2, v2 = _adam_update(g, w_ref[...], m_ref[...], v_ref[...])
        g_ref[...] = g
        d_ref[...] = delta
        m2_ref[...] = m2
        v2_ref[...] = v2

    blk = pl.BlockSpec((tr, PACK_W), lambda i: (i, 0))
    return pl.pallas_call(
        body, name=name, grid=(rows // tr,),
        in_specs=[pl.BlockSpec((N_DEV, tr, PACK_W), lambda i: (0, i, 0)), blk, blk, blk],
        out_specs=[blk] * 4, out_shape=[jax.ShapeDtypeStruct((rows, PACK_W), F32)] * 4,
        compiler_params=_params(1),
    )(parts, w, m, v)


_SHARDED = dict(w_ada=True, w_in=True, w_br_a=True, w_br_b=True, w_br_c=True, w_out=False, w_mlp1=True, w_mlp2=False)
_GROUPS = (("w_in",), ("w_br_a", "w_br_b", "w_br_c", "w_out"), ("w_mlp1", "w_mlp2"))
_KEEP_SHARDS = ("w_mlp1",)
_SMALL = ("c_ctx", "b_ada", "norm1", "norm2", "q_norm_a", "k_norm_a", "q_norm_c", "k_norm_c", "sink_c", "w_pool", "pool_scale")


def _from_shards(name, g):
    n, k, w = g.shape
    if name in _KEEP_SHARDS:
        return g
    if _SHARDED[name]:
        return g.transpose(1, 0, 2).reshape(k, n * w)
    return g.reshape(n * k, w)


def _to_shards(name, g):
    if g.ndim == 3:
        return g
    if _SHARDED[name]:
        k, nw = g.shape
        return g.reshape(k, N_DEV, nw // N_DEV).transpose(1, 0, 2)
    nk, w = g.shape
    return g.reshape(N_DEV, nk // N_DEV, w)


def _pack_small(vals):
    flat = jnp.concatenate([vals[n].reshape(-1) for n in _SMALL])
    rows = -(-flat.shape[0] // (8 * PACK_W)) * 8
    return jnp.pad(flat, (0, rows * PACK_W - flat.shape[0])).reshape(rows, PACK_W)


def _unpack_small(packed, like):
    flat, out, r = packed.reshape(-1), {}, 0
    for n in _SMALL:
        sz = like[n].size
        out[n] = flat[r:r + sz].reshape(like[n].shape)
        r += sz
    return out


def _rope_tables(cfg):
    pos = jnp.arange(cfg.S, dtype=F32)
    r = jnp.floor(pos / GRID_W)
    col = pos - r * GRID_W
    inv = 1.0 / (ROPE_THETA ** (jnp.arange(0, HEAD // 2, 2, dtype=F32) / (HEAD // 2)))
    ang = jnp.concatenate([r[:, None] * inv, col[:, None] * inv], axis=-1)
    cos = jnp.repeat(jnp.cos(ang), 2, axis=-1)
    sin = jnp.repeat(jnp.sin(ang), 2, axis=-1) * jnp.tile(jnp.array([-1.0, 1.0], F32), HEAD // 2)
    cos = jnp.concatenate([jnp.ones((cfg.N, HEAD), F32), cos], axis=0)
    sin = jnp.concatenate([jnp.zeros((cfg.N, HEAD), F32), sin], axis=0)
    return jnp.tile(cos, (1, 2)), jnp.tile(sin, (1, 2))


def _gvec(qa, ka, qc, kc):
    one = jnp.ones((qa.shape[0], KVW), F32)
    t = lambda a, n: jnp.tile(a, (1, n))
    return jnp.concatenate([t(qa, N_QHEADS), t(ka, N_KV), one, t(qc, N_QHEADS), t(kc, N_KV), one], axis=-1)[:, None, :]


def _block_diag(wp):
    L, g, c, _ = wp.shape
    eye = jnp.eye(g, dtype=wp.dtype)
    return (wp[:, :, :, None, :] * eye[None, :, None, :, None]).reshape(L, g * c, g * c)


def _pad8(a):
    return jnp.pad(a, ((0, 8 - a.shape[0]), (0, 0)))


def kernel(x, c, ctx, c_ctx, w_ada, b_ada, norm1, norm2, w_in, q_norm_a, k_norm_a, q_norm_c, k_norm_c, sink_c, w_pool, pool_scale, w_br_a, w_br_b, w_br_c, w_out, w_mlp1, w_mlp2, loss_target, m_c_ctx, m_w_ada, m_b_ada, m_norm1, m_norm2, m_w_in, m_q_norm_a, m_k_norm_a, m_q_norm_c, m_k_norm_c, m_sink_c, m_w_pool, m_pool_scale, m_w_br_a, m_w_br_b, m_w_br_c, m_w_out, m_w_mlp1, m_w_mlp2, v_c_ctx, v_w_ada, v_b_ada, v_norm1, v_norm2, v_w_in, v_q_norm_a, v_k_norm_a, v_q_norm_c, v_k_norm_c, v_sink_c, v_w_pool, v_pool_scale, v_w_br_a, v_w_br_b, v_w_br_c, v_w_out, v_w_mlp1, v_w_mlp2):
    B, S, D = x.shape
    N = ctx.shape[1]
    L = w_ada.shape[0]
    cfg = _Cfg(B, S, N, D)
    T = cfg.T
    weights = dict(c_ctx=c_ctx, w_ada=w_ada, b_ada=b_ada, norm1=norm1, norm2=norm2, w_in=w_in, q_norm_a=q_norm_a,
                   k_norm_a=k_norm_a, q_norm_c=q_norm_c, k_norm_c=k_norm_c, sink_c=sink_c, w_pool=w_pool,
                   pool_scale=pool_scale, w_br_a=w_br_a, w_br_b=w_br_b, w_br_c=w_br_c, w_out=w_out, w_mlp1=w_mlp1, w_mlp2=w_mlp2)
    mom_m = dict(c_ctx=m_c_ctx, w_ada=m_w_ada, b_ada=m_b_ada, norm1=m_norm1, norm2=m_norm2, w_in=m_w_in, q_norm_a=m_q_norm_a,
                 k_norm_a=m_k_norm_a, q_norm_c=m_q_norm_c, k_norm_c=m_k_norm_c, sink_c=m_sink_c, w_pool=m_w_pool,
                 pool_scale=m_pool_scale, w_br_a=m_w_br_a, w_br_b=m_w_br_b, w_br_c=m_w_br_c, w_out=m_w_out, w_mlp1=m_w_mlp1, w_mlp2=m_w_mlp2)
    mom_v = dict(c_ctx=v_c_ctx, w_ada=v_w_ada, b_ada=v_b_ada, norm1=v_norm1, norm2=v_norm2, w_in=v_w_in, q_norm_a=v_q_norm_a,
                 k_norm_a=v_k_norm_a, q_norm_c=v_q_norm_c, k_norm_c=v_k_norm_c, sink_c=v_sink_c, w_pool=v_w_pool,
                 pool_scale=v_pool_scale, w_br_a=v_w_br_a, w_br_b=v_w_br_b, w_br_c=v_w_br_c, w_out=v_w_out, w_mlp1=v_w_mlp1, w_mlp2=v_w_mlp2)

    shards_bf16 = {n: weights[n].astype(BF16) for group in _GROUPS for n in group}
    full = [dict() for _ in range(L)]

    def gather_of(l, gi):
        return _Exchange([(shards_bf16[n], l) for n in _GROUPS[gi]], scatter=False)

    def gathered(l, gi, arrs):
        for n, a in zip(_GROUPS[gi], arrs):
            full[l][n] = _from_shards(n, a)

    gathered(0, 0, gather_of(0, 0).alone("gather_first_weights"))

    def hosting(fn, *a, exch=None, done=None, **kw):
        if exch is None:
            return fn(*a, **kw)
        res = fn(*a, exch=exch, **kw)
        done(res[-exch.n:])
        own = res[:-exch.n]
        return own[0] if len(own) == 1 else own

    def gather_behind(l, gi):
        if l >= L:
            return {}
        return dict(exch=gather_of(l, gi), done=functools.partial(gathered, l, gi))

    cosf, sins = _rope_tables(cfg)
    xs = jnp.concatenate([ctx, x], axis=1)
    cc8 = _pad8(jnp.concatenate([c, c_ctx[None, :]], axis=0))
    va_blk, vc_blk = (QW + KVW) // KVW, (2 * QW + 3 * KVW) // KVW
    per_layer = lambda a: a[:, None, :]
    b_ada3, norm1_3, norm2_3, ps3 = per_layer(b_ada), per_layer(norm1), per_layer(norm2), per_layer(pool_scale)
    gvec = _gvec(q_norm_a, k_norm_a, q_norm_c, k_norm_c)
    sink8 = jnp.pad(sink_c[:, None, :], ((0, 0), (0, 7), (0, LANES - N_QHEADS)))
    wp = _block_diag(w_pool).astype(BF16)

    cc_all = _Exchange([cc8], scatter=False).alone("gather_cond")[0].reshape(8 * N_DEV, D)
    mod_cols = _Exchange([_adaln_fwd(cfg, cc_all, w_ada)], scatter=True).alone("scatter_mod")[0]
    mod_all = _adaln_join(cfg, mod_cols, b_ada3)

    saved = []
    for l in range(L):
        fw = full[l]
        ctx_active = l < L - 1
        mod = mod_all[l]
        z, h = hosting(_in_proj_fwd, cfg, l, xs, norm1_3, mod, fw["w_in"], **(gather_behind(0, 1) if l == 0 else {}))
        qa, ka, qc, kc = _qknorm_fwd(cfg, l, z, gvec, cosf, sins)
        oa = hosting(_attn_fwd, cfg, l, qa, ka, z, va_blk, sink8, window=False, sink=False, ctx_attend=ctx_active, name="attn_a_fwd",
                     **gather_behind(l + 1, 0))
        oc = hosting(_attn_fwd, cfg, l, qc, kc, z, vc_blk, sink8, window=True, sink=True, ctx_attend=ctx_active, name="attn_c_fwd",
                     **(gather_behind(0, 2) if l == 0 else {}))
        ob, pooled = _pool_fwd(cfg, l, z, wp, ps3)
        x1, mgo = hosting(_merge_fwd, cfg, xs, oa, ob, oc, z, mod, fw["w_br_a"], fw["w_br_b"], fw["w_br_c"], fw["w_out"],
                          ctx_active=ctx_active, **gather_behind(l + 1, 1))
        x2, mo = hosting(_mlp_fwd, cfg, l, x1, norm2_3, mod, fw["w_mlp1"], fw["w_mlp2"], ctx_active=ctx_active,
                         **gather_behind(l + 1, 2))
        saved.append(dict(xs=xs, mod=mod, z=z, h=h, qa=qa, ka=ka, qc=qc, kc=kc, oa=oa, oc=oc, ob=ob, pooled=pooled, x1=x1, mgo=mgo, mo=mo))
        xs = x2

    dxs, sse = _loss_fwd_bwd(cfg, xs, loss_target)
    loss = lax.psum(0.5 * sse[0, 0] / D, ("x", "y", "c"))

    grads = [dict() for _ in range(L)]
    parts = {}
    small = {n: [None] * L for n in _SMALL if n != "c_ctx"}
    d_c_ctx = jnp.zeros((D,), F32)
    flat2 = lambda a: a.reshape(B * T, a.shape[-1])

    def scatter_of(l, gi):
        return _Exchange([_to_shards(n, grads[l][n]) for n in _GROUPS[gi]], scatter=True)

    def scattered(l, gi, arrs):
        for n, a in zip(_GROUPS[gi], arrs):
            parts[(l, n)] = a

    def scatter_behind(l, gi):
        if l >= L:
            return {}
        return dict(exch=scatter_of(l, gi), done=functools.partial(scattered, l, gi))

    for l in reversed(range(L)):
        fw, sv, g = full[l], saved[l], grads[l]
        ctx_active = l < L - 1
        mod = sv["mod"]
        dx1, h2, r, da, dout, dmod_mlp, dgn2 = hosting(_mlp_bwd, cfg, l, sv["x1"], dxs, sv["mo"], norm2_3, mod, fw["w_mlp1"], fw["w_mlp2"],
                                                       ctx_active=ctx_active, **scatter_behind(l + 1, 2))
        g["w_mlp1"] = _matmul_tn(flat2(h2), flat2(da), "dw_mlp1", by_shard=True)
        g["w_mlp2"] = _matmul_tn(flat2(r), flat2(dout), "dw_mlp2", by_shard=False)
        doa, dob, doc, dpa, dpb, dpc, y, dmo, dzg, dg1 = hosting(
            _merge_bwd, cfg, dx1, sv["mgo"], sv["oa"], sv["ob"], sv["oc"], sv["z"], mod, fw["w_br_a"], fw["w_br_b"], fw["w_br_c"], fw["w_out"],
            ctx_active=ctx_active, **scatter_behind(l + 1, 1))
        g["w_out"] = _matmul_tn(flat2(y), flat2(dmo), "dw_out", by_shard=False)
        g["w_br_a"] = _matmul_tn(flat2(sv["oa"]), flat2(dpa), "dw_br_a", by_shard=True)
        g["w_br_b"] = _matmul_tn(flat2(sv["ob"]), flat2(dpb), "dw_br_b", by_shard=True)
        g["w_br_c"] = _matmul_tn(flat2(sv["oc"]), flat2(dpc), "dw_br_c", by_shard=True)
        z = sv["z"]
        dqa, dka, dva, _ = hosting(_attn_bwd, cfg, l, sv["qa"], sv["ka"], z, va_blk, sink8, doa, window=False, sink=False,
                                   ctx_attend=ctx_active, name="attn_a_bwd", **scatter_behind(l + 1, 0))
        dqc, dkc, dvc, dsink = hosting(_attn_bwd, cfg, l, sv["qc"], sv["kc"], z, vc_blk, sink8, doc, window=True, sink=True,
                                       ctx_attend=ctx_active, name="attn_c_bwd", **(scatter_behind(0, 2) if l == 0 else {}))
        dzq, dgvec = _qknorm_bwd(cfg, l, z, gvec, cosf, sins, dqa, dka, dva, dqc, dkc, dvc)
        du, dwp, dps = _pool_bwd(cfg, l, dob, sv["pooled"], wp, ps3)
        dxs, dz, dmod_in, dgn1 = hosting(_in_proj_bwd, cfg, l, dzq, du, dzg, fw["w_in"], sv["xs"], dx1, norm1_3, mod,
                                         latent_only=(l == 0), **(scatter_behind(0, 1) if l == 0 else {}))
        g["w_in"] = _matmul_tn(flat2(sv["h"]), flat2(dz), "dw_in", by_shard=False)
        dmod_cols, dbias = _dmod_pack(cfg, dmod_in, dg1, dmod_mlp)
        dm_all = _Exchange([dmod_cols], scatter=True).alone("scatter_dmod")[0].reshape(8 * N_DEV, -1)
        g["w_ada"], dcc = _adaln_bwd(cfg, l, cc_all, dm_all, w_ada)
        d_c_ctx = d_c_ctx + dcc[0]
        gv = dgvec[0]
        heads = lambda v, n: v.reshape(n, HEAD).sum(axis=0)
        small["b_ada"][l] = dbias[0]
        small["norm1"][l] = dgn1[0]
        small["norm2"][l] = dgn2[0]
        small["q_norm_a"][l] = heads(gv[0:QW], N_QHEADS)
        small["k_norm_a"][l] = heads(gv[QW:QW + KVW], N_KV)
        small["q_norm_c"][l] = heads(gv[QW + 2 * KVW:2 * QW + 2 * KVW], N_QHEADS)
        small["k_norm_c"][l] = heads(gv[2 * QW + 2 * KVW:2 * QW + 3 * KVW], N_KV)
        small["sink_c"][l] = dsink[0, :N_QHEADS]
        small["w_pool"][l] = jnp.stack([dwp[i * HEAD:(i + 1) * HEAD, i * HEAD:(i + 1) * HEAD] for i in range(len(POOL_WINDOWS))])
        small["pool_scale"][l] = dps[0]
    grad_x = dxs

    scattered(0, 0, scatter_of(0, 0).alone("scatter_last_grads"))
    for l in range(L):
        parts[(l, "w_ada")] = grads[l]["w_ada"][None]
    stepped = {n: _adamw_sharded([parts[(l, n)] for l in range(L)], weights[n], mom_m[n], mom_v[n], "adamw_" + n) for n in _SHARDED}

    small_vals = {n: jnp.stack(v) for n, v in small.items()}
    small_vals["c_ctx"] = d_c_ctx
    small_parts = _Exchange([_pack_small(small_vals)], scatter=False).alone("gather_small_grads")[0]
    stepped_small = _adamw_packed(small_parts, _pack_small(weights), _pack_small(mom_m), _pack_small(mom_v), "adamw_small")

    outs = []
    for i in range(4):
        res = {n: stepped[n][i] for n in _SHARDED}
        res.update(_unpack_small(stepped_small[i], weights))
        outs.append(res)
    order = ("c_ctx", "w_ada", "b_ada", "norm1", "norm2", "w_in", "q_norm_a", "k_norm_a", "q_norm_c", "k_norm_c", "sink_c",
             "w_pool", "pool_scale", "w_br_a", "w_br_b", "w_br_c", "w_out", "w_mlp1", "w_mlp2")
    return (loss, grad_x, *[res[n] for res in outs for n in order])
```

```python
import functools

import jax
import jax.numpy as jnp
from jax import lax
from jax.experimental import pallas as pl
from jax.experimental.pallas import tpu as pltpu

F32 = jnp.float32
BF16 = jnp.bfloat16
HIGHEST = lax.Precision.HIGHEST

N_DEV = 8
HEAD = 64
N_QHEADS = 6
N_KV = 2
GROUP = 3
QW = N_QHEADS * HEAD
KVW = N_KV * HEAD
QKV_W = 2 * (QW + 2 * KVW)
POOL_W = 256
POOL_WINDOWS = (2, 4, 8, 16)
GATE0 = QKV_W + POOL_W
WINDOW = 128
GRID_W = 64
ROPE_THETA = 10000.0
EPS = 1e-6
NEG = -1e30
QSCALE = HEAD ** -0.5
LANES = 128
PACK_W = 1024
VMEM_LIMIT = 56 * 1024 * 1024

ADAM_LR = 0.001
ADAM_B1 = 0.9
ADAM_B2 = 0.999
ADAM_EPS = 1e-08
ADAM_WD = 0.01
ADAM_STEP = 10

NT_DIMS = (((1,), (1,)), ((), ()))
TN_DIMS = (((0,), (0,)), ((), ()))


def _dot(a, b):
    return jnp.dot(a, b, preferred_element_type=F32)


def _dot_nt(a, b):
    return lax.dot_general(a, b, NT_DIMS, preferred_element_type=F32)


def _dot_tn(a, b):
    return lax.dot_general(a, b, TN_DIMS, preferred_element_type=F32)


def _params(n_grid):
    return pltpu.CompilerParams(dimension_semantics=("arbitrary",) * n_grid, vmem_limit_bytes=VMEM_LIMIT)


def _full(shape):
    nd = len(shape)
    return pl.BlockSpec(shape, lambda *_: (0,) * nd)


def _layer(l, width):
    return pl.BlockSpec((1, 1, width), lambda *_: (l, 0, 0))


def _modulate(x, gn, shift, scale):
    rstd = lax.rsqrt(jnp.mean(x * x, axis=-1, keepdims=True) + EPS)
    xhat = x * rstd
    return xhat * gn * (1.0 + scale) + shift, xhat, rstd


def _modulate_bwd(dh, xhat, rstd, gn, scale):
    d_shift = jnp.sum(dh, axis=0, keepdims=True)
    d_scale = jnp.sum(dh * xhat * gn, axis=0, keepdims=True)
    dy = dh * (1.0 + scale)
    d_gn = jnp.sum(dy * xhat, axis=0, keepdims=True)
    dxh = dy * gn
    dx = rstd * (dxh - xhat * jnp.mean(dxh * xhat, axis=-1, keepdims=True))
    return dx, d_shift, d_scale, d_gn


def _mod_row(mod_ref, row, k, d):
    return mod_ref[pl.ds(row, 1), k * d:(k + 1) * d]


class _Cfg:
    def __init__(self, b, s, n, d):
        self.B, self.S, self.N, self.D = b, s, n, d
        self.T = n + s
        self.F = 4 * d
        self.IN = GATE0 + 3 * d
        self.tm = 256 if (n % 256 == 0 and s % 256 == 0) else 128
        self.nT = self.T // self.tm
        self.nC = n // self.tm
        self.gw = 512 if d % 512 == 0 else 256
        self.kw = self.tm + 2 * LANES
        assert GATE0 % self.gw == 0 and d % self.gw == 0 and b < 8 and self.T >= self.kw and max(POOL_WINDOWS) // 2 <= LANES
        assert s % GRID_W == 0 and n % self.tm == 0 and s % self.tm == 0 and s >= self.tm + 2 * WINDOW
        assert d % (N_DEV * LANES) == 0


def _peer(k):
    x, y, c = lax.axis_index("x"), lax.axis_index("y"), lax.axis_index("c")
    px = x ^ ((k >> 2) & 1)
    py = y ^ ((k >> 1) & 1)
    pc = c ^ (k & 1)
    return (px, py, pc), 4 * px + 2 * py + pc


class _Exchange:
    def __init__(self, arrays, scatter):
        self.arrays = [a if isinstance(a, tuple) else (a, None) for a in arrays]
        self.scatter = scatter
        self.n = len(self.arrays)

    def operands(self):
        return [a for a, _ in self.arrays]

    def out_shapes(self):
        res = []
        for a, layer in self.arrays:
            shape = a.shape[1:] if (self.scatter or layer is not None) else a.shape
            res.append(jax.ShapeDtypeStruct((N_DEV,) + tuple(shape), a.dtype))
        return res

    def scratch(self):
        n = self.n * (N_DEV - 1)
        return [pltpu.SemaphoreType.DMA((n,)), pltpu.SemaphoreType.DMA((n,)), pltpu.SemaphoreType.DMA((self.n,))]

    def _copies(self, x_refs, out_refs, send_sems, recv_sems, local_sems, want):
        _, me = _peer(0)
        res = []
        for i, ((_, layer), x_ref, out_ref) in enumerate(zip(self.arrays, x_refs, out_refs)):
            if self.scatter:
                src_of = lambda d, x_ref=x_ref: x_ref.at[d]
            elif layer is not None:
                src_of = lambda d, x_ref=x_ref, layer=layer: x_ref.at[layer]
            else:
                src_of = lambda d, x_ref=x_ref: x_ref
            if want == "local":
                res.append(pltpu.make_async_copy(src_of(me), out_ref.at[me], local_sems.at[i]))
                continue
            for k in range(1, N_DEV):
                pos, idx = _peer(k)
                j = i * (N_DEV - 1) + k - 1
                common = dict(send_sem=send_sems.at[j], recv_sem=recv_sems.at[j], device_id=pos, device_id_type=pl.DeviceIdType.MESH)
                if want == "send":
                    res.append(pltpu.make_async_remote_copy(src_ref=src_of(idx), dst_ref=out_ref.at[me], **common))
                else:
                    res.append(pltpu.make_async_remote_copy(src_ref=src_of(me), dst_ref=out_ref.at[idx], **common))
        return res

    def start(self, *refs):
        for cp in self._copies(*refs, "local") + self._copies(*refs, "send"):
            cp.start()

    def wait(self, *refs):
        for cp in self._copies(*refs, "recv"):
            cp.wait_recv()
        for cp in self._copies(*refs, "send"):
            cp.wait_send()
        for cp in self._copies(*refs, "local"):
            cp.wait()

    def alone(self, name):
        n = self.n

        def body(*refs):
            args = (refs[:n], refs[n:2 * n], *refs[2 * n:])
            self.start(*args)
            self.wait(*args)

        any_spec = pl.BlockSpec(memory_space=pl.ANY)
        return pl.pallas_call(body, name=name, in_specs=[any_spec] * n, out_specs=[any_spec] * n,
                              out_shape=self.out_shapes(), scratch_shapes=self.scratch())(*self.operands())


def _pcall(exch):
    if exch is None:
        return pl.pallas_call

    def make(body, *, name, grid, in_specs, out_specs, out_shape, compiler_params, scratch_shapes=()):
        multi = isinstance(out_shape, (list, tuple))
        out_specs_l = list(out_specs) if multi else [out_specs]
        out_shape_l = list(out_shape) if multi else [out_shape]
        n_in, n_out, n_x, n_s = len(in_specs), len(out_specs_l), exch.n, len(scratch_shapes)

        def hosted(*refs):
            ins, x_refs = refs[:n_in], refs[n_in:n_in + n_x]
            o0 = n_in + n_x
            outs, xo_refs = refs[o0:o0 + n_out], refs[o0 + n_out:o0 + n_out + n_x]
            s0 = o0 + n_out + n_x
            own_scratch, sems = refs[s0:s0 + n_s], refs[s0 + n_s:]
            ids = [pl.program_id(i) for i in range(len(grid))]
            first = functools.reduce(jnp.logical_and, [i == 0 for i in ids])
            last = functools.reduce(jnp.logical_and, [i == g - 1 for i, g in zip(ids, grid)])

            @pl.when(first)
            def _():
                exch.start(x_refs, xo_refs, *sems)

            body(*ins, *outs, *own_scratch)

            @pl.when(last)
            def _():
                exch.wait(x_refs, xo_refs, *sems)

        any_spec = pl.BlockSpec(memory_space=pl.ANY)
        call = pl.pallas_call(
            hosted, name=name, grid=grid, in_specs=list(in_specs) + [any_spec] * n_x, out_specs=out_specs_l + [any_spec] * n_x,
            out_shape=out_shape_l + exch.out_shapes(), scratch_shapes=list(scratch_shapes) + exch.scratch(),
            compiler_params=compiler_params)
        return lambda *args: call(*args, *exch.operands())

    return make


def _adaln_fwd(cfg, cc_all, w_ada):
    d = cfg.D
    L, _, wa = w_ada.shape

    def body(c_ref, w_ref, o_ref):
        c = c_ref[...]
        a = (c * jax.nn.sigmoid(c)).astype(BF16)
        for l in range(L):
            m = _dot(a, w_ref[l].astype(BF16))
            for p in range(N_DEV):
                o_ref[p, l] = m[8 * p:8 * (p + 1)]

    return pl.pallas_call(
        body, name="adaln_fwd", grid=(1,),
        in_specs=[_full((8 * N_DEV, d)), _full((L, d, wa))],
        out_specs=_full((N_DEV, L, 8, wa)),
        out_shape=jax.ShapeDtypeStruct((N_DEV, L, 8, wa), F32), compiler_params=_params(1),
    )(cc_all, w_ada)


def _adaln_join(cfg, parts, b_ada):
    d = cfg.D
    _, L, _, wa = parts.shape

    def body(p_ref, b_ref, o_ref):
        for l in range(L):
            for j in range(N_DEV):
                o_ref[l, :, j * wa:(j + 1) * wa] = p_ref[j, l] + b_ref[l, :, j * wa:(j + 1) * wa]

    return pl.pallas_call(
        body, name="adaln_join", grid=(1,),
        in_specs=[_full((N_DEV, L, 8, wa)), _full((L, 1, 6 * d))],
        out_specs=_full((L, 8, 6 * d)),
        out_shape=jax.ShapeDtypeStruct((L, 8, 6 * d), F32), compiler_params=_params(1),
    )(parts, b_ada)


def _in_proj_fwd(cfg, l, x, gn, mod, w_in, exch=None):
    B, T, D, IN, tm, nC = cfg.B, cfg.T, cfg.D, cfg.IN, cfg.tm, cfg.nC

    def body(x_ref, gn_ref, mod_ref, w_ref, z_ref, h_ref):
        b, t = pl.program_id(0), pl.program_id(1)
        row = jnp.where(t < nC, B, b)
        h, _, _ = _modulate(x_ref[0], gn_ref[0], _mod_row(mod_ref, row, 0, D), _mod_row(mod_ref, row, 1, D))
        hb = h.astype(BF16)
        h_ref[0] = hb
        z_ref[0] = _dot(hb, w_ref[...])

    return _pcall(exch)(
        body, name="in_proj_fwd", grid=(B, cfg.nT),
        in_specs=[pl.BlockSpec((1, tm, D), lambda b, t: (b, t, 0)), _layer(l, D), _full((8, 6 * D)), _full((D, IN))],
        out_specs=[pl.BlockSpec((1, tm, IN), lambda b, t: (b, t, 0)), pl.BlockSpec((1, tm, D), lambda b, t: (b, t, 0))],
        out_shape=[jax.ShapeDtypeStruct((B, T, IN), F32), jax.ShapeDtypeStruct((B, T, D), BF16)],
        compiler_params=_params(2),
    )(x, gn, mod, w_in)


def _head_indicator():
    r = lax.broadcasted_iota(jnp.int32, (LANES, LANES), 0) // HEAD
    c = lax.broadcasted_iota(jnp.int32, (LANES, LANES), 1) // HEAD
    return jnp.where(r == c, 1.0, 0.0).astype(BF16)


def _head_sum(x, ind):
    hi = x.astype(BF16)
    lo = (x - hi.astype(F32)).astype(BF16)
    return _dot(hi, ind) + _dot(lo, ind)


def _pair_swap(y):
    lane = lax.broadcasted_iota(jnp.int32, y.shape, 1)
    return jnp.where(lane % 2 == 0, pltpu.roll(y, LANES - 1, 1), pltpu.roll(y, 1, 1))


_QK_CHUNKS = (0, 1, 2, 3, 5, 6, 7, 8)
_Q_CHUNKS = (0, 1, 2, 5, 6, 7)


def _qknorm_fwd(cfg, l, z, gvec, cosf, sins):
    B, T, tm = cfg.B, cfg.T, cfg.tm

    def body(z_ref, g_ref, cos_ref, sin_ref, qa_ref, ka_ref, qc_ref, kc_ref):
        ind = _head_indicator()
        cos, sin = cos_ref[...], sin_ref[...]

        def chunk(c):
            x = z_ref[0, :, c * LANES:(c + 1) * LANES]
            ss = _head_sum(x * x, ind)
            y = x * lax.rsqrt(ss * (1.0 / HEAD) + EPS) * g_ref[0, :, c * LANES:(c + 1) * LANES]
            out = y * cos + _pair_swap(y) * sin
            return (out * QSCALE if c in _Q_CHUNKS else out).astype(BF16)

        qa_ref[0] = jnp.concatenate([chunk(0), chunk(1), chunk(2)], axis=-1)
        ka_ref[0] = chunk(3)
        qc_ref[0] = jnp.concatenate([chunk(5), chunk(6), chunk(7)], axis=-1)
        kc_ref[0] = chunk(8)

    row = lambda w: pl.BlockSpec((1, tm, w), lambda b, t: (b, t, 0))
    tab = pl.BlockSpec((tm, LANES), lambda b, t: (t, 0))
    return pl.pallas_call(
        body, name="qknorm_fwd", grid=(B, cfg.nT),
        in_specs=[row(QKV_W), _layer(l, QKV_W), tab, tab],
        out_specs=[row(QW), row(KVW), row(QW), row(KVW)],
        out_shape=[jax.ShapeDtypeStruct((B, T, w), BF16) for w in (QW, KVW, QW, KVW)],
        compiler_params=_params(2),
    )(z, gvec, cosf, sins)


def _attn_scores(cfg, tl, q, k_ref, v_ref, sink_ref, h, loc, window, sink):
    S, N, tq = cfg.S, cfg.N, cfg.tm
    hs = slice(h * HEAD, (h + 1) * HEAD)
    qs = jnp.concatenate([q[:, (GROUP * h + g) * HEAD:(GROUP * h + g + 1) * HEAD] for g in range(GROUP)], axis=0)
    lo = None
    if not loc:
        kk = k_ref[0, 0:N, :][:, hs]
        vv = v_ref[0, 0:N, :].astype(BF16)[:, hs]
    elif not window:
        kk = k_ref[0][:, hs]
        vv = v_ref[0].astype(BF16)[:, hs]
    else:
        W = tq + 2 * WINDOW
        lo = pl.multiple_of(jnp.clip(tl * tq - WINDOW, 0, S - W), LANES)
        kk = jnp.concatenate([k_ref[0, 0:N, :], k_ref[0, pl.ds(N + lo, W), :]], axis=0)[:, hs]
        vv = jnp.concatenate([v_ref[0, 0:N, :], v_ref[0, pl.ds(N + lo, W), :]], axis=0).astype(BF16)[:, hs]
    st = _dot_nt(kk, qs)
    if window:
        krow = lax.broadcasted_iota(jnp.int32, st.shape, 0)
        qpos = tl * tq + lax.broadcasted_iota(jnp.int32, st.shape, 1) % tq
        st = jnp.where((krow < N) | (jnp.abs(qpos - (lo + krow - N)) <= WINDOW), st, NEG)
    m = jnp.max(st, axis=0, keepdims=True)
    if sink:
        colg = lax.broadcasted_iota(jnp.int32, (1, GROUP * tq), 1) // tq
        sk = jnp.zeros((1, GROUP * tq), F32)
        for g in range(GROUP):
            j = GROUP * h + g
            sk = jnp.where(colg == g, sink_ref[0, 0:1, j:j + 1], sk)
        m = jnp.maximum(m, sk)
    e = jnp.exp(st - m)
    l = jnp.sum(e, axis=0, keepdims=True)
    e_s = None
    if sink:
        e_s = jnp.exp(sk - m)
        l = l + e_s
    return qs, kk, vv, e, 1.0 / l, e_s, lo


def _attn_fwd(cfg, l, q, k, z, vblock, sink8, *, window, sink, ctx_attend, name, exch=None):
    B, T, tq, nC = cfg.B, cfg.T, cfg.tm, cfg.nC

    def body(q_ref, k_ref, v_ref, sink_ref, o_ref):
        t = pl.program_id(1)

        def run(loc):
            q_t = q_ref[0]
            outs = [None] * N_QHEADS
            for h in range(N_KV):
                _, _, vv, e, inv, _, _ = _attn_scores(cfg, t - nC, q_t, k_ref, v_ref, sink_ref, h, loc, window and loc, sink)
                o = (_dot_tn(vv, e.astype(BF16)) * inv).T
                for g in range(GROUP):
                    outs[GROUP * h + g] = o[g * tq:(g + 1) * tq]
            o_ref[0] = jnp.concatenate(outs, axis=-1).astype(BF16)

        pl.when(t >= nC)(functools.partial(run, True))
        if ctx_attend:
            pl.when(t < nC)(functools.partial(run, False))
        else:
            @pl.when(t < nC)
            def _():
                o_ref[0] = jnp.zeros((tq, QW), BF16)

    return _pcall(exch)(
        body, name=name, grid=(B, cfg.nT),
        in_specs=[pl.BlockSpec((1, tq, QW), lambda b, t: (b, t, 0)),
                  pl.BlockSpec((1, T, KVW), lambda b, t: (b, 0, 0)),
                  pl.BlockSpec((1, T, KVW), lambda b, t: (b, 0, vblock)),
                  pl.BlockSpec((1, 8, LANES), lambda b, t: (l, 0, 0))],
        out_specs=pl.BlockSpec((1, tq, QW), lambda b, t: (b, t, 0)),
        out_shape=jax.ShapeDtypeStruct((B, T, QW), BF16), compiler_params=_params(2),
    )(q, k, z, sink8)


def _pool_geometry(cfg, t):
    tm, N, T, nC = cfg.tm, cfg.N, cfg.T, cfg.nC
    r0 = pl.multiple_of(t * tm, tm)
    isctx = t < nC
    seg_lo = jnp.where(isctx, 0, N)
    seg_hi = jnp.where(isctx, N, T)
    k0 = pl.multiple_of(jnp.clip(t * tm - LANES, 0, T - cfg.kw), LANES)
    return r0, seg_lo, seg_hi, k0


def _pool_count(pos, h, seg_lo, seg_hi):
    return jnp.maximum(jnp.minimum(pos + h, seg_hi) - jnp.maximum(pos - h, seg_lo), 1).astype(F32)


def _split_bf16(x):
    hi = x.astype(BF16)
    return hi, (x - hi.astype(F32)).astype(BF16)


def _pool_fwd(cfg, l, z, wp, ps):
    B, T, tm, kw = cfg.B, cfg.T, cfg.tm, cfg.kw

    def body(u_ref, wp_ref, ps_ref, ob_ref, pooled_ref):
        t = pl.program_id(1)
        r0, seg_lo, seg_hi, k0 = _pool_geometry(cfg, t)
        hi, lo = _split_bf16(u_ref[0, pl.ds(k0, kw), :])
        rr = r0 + lax.broadcasted_iota(jnp.int32, (tm, kw), 0)
        cc = k0 + lax.broadcasted_iota(jnp.int32, (tm, kw), 1)
        diff = cc - rr
        inseg = (cc >= seg_lo) & (cc < seg_hi)
        rcol = r0 + lax.broadcasted_iota(jnp.int32, (tm, 1), 0)
        group = lax.broadcasted_iota(jnp.int32, (tm, POOL_W), 1) // HEAD
        acc = jnp.zeros((tm, POOL_W), F32)
        for gi, w in enumerate(POOL_WINDOWS):
            h = w // 2
            band = jnp.where((diff >= -h) & (diff <= h - 1) & inseg, 1.0, 0.0).astype(BF16)
            tot = _dot(band, hi) + _dot(band, lo)
            acc = jnp.where(group == gi, tot / _pool_count(rcol, h, seg_lo, seg_hi), acc)
        pooled = (acc - u_ref[0, pl.ds(r0, tm), :]).astype(BF16)
        pooled_ref[0] = pooled
        ob_ref[0] = (_dot(pooled, wp_ref[0]) * ps_ref[0]).astype(BF16)

    row = pl.BlockSpec((1, tm, POOL_W), lambda b, t: (b, t, 0))
    return pl.pallas_call(
        body, name="pool_fwd", grid=(B, cfg.nT),
        in_specs=[pl.BlockSpec((1, T, POOL_W), lambda b, t: (b, 0, QKV_W // POOL_W)),
                  pl.BlockSpec((1, POOL_W, POOL_W), lambda b, t: (l, 0, 0)), _layer(l, POOL_W)],
        out_specs=[row, row],
        out_shape=[jax.ShapeDtypeStruct((B, T, POOL_W), BF16)] * 2, compiler_params=_params(2),
    )(z, wp, ps)


def _gate_specs(cfg):
    tm, gw = cfg.tm, cfg.gw
    first = GATE0 // gw
    return [pl.BlockSpec((1, tm, gw), functools.partial(lambda b, t, j: (b, t, j), j=first + i)) for i in range(3 * cfg.D // gw)]


def _read_gates(cfg, gate_refs):
    per = cfg.D // cfg.gw
    return [jnp.concatenate([gate_refs[k * per + i][0] for i in range(per)], axis=-1) for k in range(3)]


def _merge_fwd(cfg, x, oa, ob, oc, z, mod, wa, wb, wc, wo, *, ctx_active, exch=None):
    B, T, D, tm, nC = cfg.B, cfg.T, cfg.D, cfg.tm, cfg.nC
    ng = 3 * D // cfg.gw

    def body(x_ref, oa_ref, ob_ref, oc_ref, *rest):
        gate_refs = rest[:ng]
        mod_ref, wa_ref, wb_ref, wc_ref, wo_ref, x1_ref, mgo_ref = rest[ng:]
        b, t = pl.program_id(0), pl.program_id(1)

        def compute():
            row = jnp.where(t < nC, B, b)
            ga, gb, gc = _read_gates(cfg, gate_refs)
            y = (jax.nn.sigmoid(ga) * _dot(oa_ref[0], wa_ref[...])
                 + jax.nn.sigmoid(gb) * _dot(ob_ref[0], wb_ref[...])
                 + jax.nn.sigmoid(gc) * _dot(oc_ref[0], wc_ref[...]))
            mo = _dot(y.astype(BF16), wo_ref[...])
            mgo_ref[0] = mo.astype(BF16)
            x1_ref[0] = x_ref[0] + _mod_row(mod_ref, row, 2, D) * mo

        if ctx_active:
            compute()
        else:
            pl.when(t >= nC)(compute)

            @pl.when(t < nC)
            def _():
                mgo_ref[0] = jnp.zeros((tm, D), BF16)
                x1_ref[0] = x_ref[0]

    row = lambda w: pl.BlockSpec((1, tm, w), lambda b, t: (b, t, 0))
    return _pcall(exch)(
        body, name="merge_fwd", grid=(B, cfg.nT),
        in_specs=[row(D), row(QW), row(POOL_W), row(QW)] + _gate_specs(cfg)
        + [_full((8, 6 * D)), _full((QW, D)), _full((POOL_W, D)), _full((QW, D)), _full((D, D))],
        out_specs=[row(D), row(D)],
        out_shape=[jax.ShapeDtypeStruct((B, T, D), F32), jax.ShapeDtypeStruct((B, T, D), BF16)],
        compiler_params=_params(2),
    )(x, oa, ob, oc, *([z] * ng), mod, wa, wb, wc, wo)


def _w1_apply(hb, w1_ref):
    return jnp.concatenate([_dot(hb, w1_ref[d]) for d in range(N_DEV)], axis=-1)


def _mlp_fwd(cfg, l, x1, gn, mod, w1, w2, *, ctx_active, exch=None):
    B, T, D, F, tm, nC = cfg.B, cfg.T, cfg.D, cfg.F, cfg.tm, cfg.nC

    def body(x_ref, gn_ref, mod_ref, w1_ref, w2_ref, x2_ref, mo_ref):
        b, t = pl.program_id(0), pl.program_id(1)

        def compute():
            row = jnp.where(t < nC, B, b)
            x = x_ref[0]
            h, _, _ = _modulate(x, gn_ref[0], _mod_row(mod_ref, row, 3, D), _mod_row(mod_ref, row, 4, D))
            a = jnp.maximum(_w1_apply(h.astype(BF16), w1_ref), 0.0)
            mo = _dot((a * a).astype(BF16), w2_ref[...])
            mo_ref[0] = mo.astype(BF16)
            x2_ref[0] = x + _mod_row(mod_ref, row, 5, D) * mo

        if ctx_active:
            compute()
        else:
            pl.when(t >= nC)(compute)

            @pl.when(t < nC)
            def _():
                mo_ref[0] = jnp.zeros((tm, D), BF16)
                x2_ref[0] = x_ref[0]

    row = pl.BlockSpec((1, tm, D), lambda b, t: (b, t, 0))
    return _pcall(exch)(
        body, name="mlp_fwd", grid=(B, cfg.nT),
        in_specs=[row, _layer(l, D), _full((8, 6 * D)), _full((N_DEV, D, F // N_DEV)), _full((F, D))],
        out_specs=[row, row],
        out_shape=[jax.ShapeDtypeStruct((B, T, D), F32), jax.ShapeDtypeStruct((B, T, D), BF16)],
        compiler_params=_params(2),
    )(x1, gn, mod, w1, w2)


def _loss_fwd_bwd(cfg, x2, target):
    B, T, D, tm, nC = cfg.B, cfg.T, cfg.D, cfg.tm, cfg.nC

    def body(x_ref, tgt_ref, dx_ref, sse_ref):
        b, t = pl.program_id(0), pl.program_id(1)

        @pl.when((b == 0) & (t == 0))
        def _():
            sse_ref[...] = jnp.zeros((8, LANES), F32)

        @pl.when(t < nC)
        def _():
            dx_ref[0] = jnp.zeros((tm, D), F32)

        @pl.when(t >= nC)
        def _():
            err = x_ref[0] - tgt_ref[0]
            dx_ref[0] = err * (1.0 / D)
            sse_ref[...] += jnp.sum(err * err)

    return pl.pallas_call(
        body, name="loss", grid=(B, cfg.nT),
        in_specs=[pl.BlockSpec((1, tm, D), lambda b, t: (b, t, 0)),
                  pl.BlockSpec((1, tm, D), lambda b, t: (b, jnp.maximum(t - nC, 0), 0))],
        out_specs=[pl.BlockSpec((1, tm, D), lambda b, t: (b, t, 0)), _full((8, LANES))],
        out_shape=[jax.ShapeDtypeStruct((B, T, D), F32), jax.ShapeDtypeStruct((8, LANES), F32)],
        compiler_params=_params(2),
    )(x2, target)


def _acc_init(refs):
    b, t = pl.program_id(0), pl.program_id(1)

    @pl.when((b == 0) & (t == 0))
    def _():
        for ref in refs:
            ref[...] = jnp.zeros(ref.shape, ref.dtype)


def _mlp_bwd(cfg, l, x1, dx2, mo, gn, mod, w1, w2, *, ctx_active, exch=None):
    B, T, D, F, tm, nC = cfg.B, cfg.T, cfg.D, cfg.F, cfg.tm, cfg.nC
    ws = F // N_DEV

    def body(x_ref, dx_ref, mo_ref, gn_ref, mod_ref, w1_ref, w2_ref, dx1_ref, h_ref, r_ref, da_ref, dout_ref, dmod_ref, dgn_ref):
        b, t = pl.program_id(0), pl.program_id(1)
        _acc_init([dmod_ref, dgn_ref])

        def compute():
            row = jnp.where(t < nC, B, b)
            gn = gn_ref[0]
            scale = _mod_row(mod_ref, row, 4, D)
            h, xhat, rstd = _modulate(x_ref[0], gn, _mod_row(mod_ref, row, 3, D), scale)
            hb = h.astype(BF16)
            a = jnp.maximum(_w1_apply(hb, w1_ref), 0.0)
            dx = dx_ref[0]
            dout = (dx * _mod_row(mod_ref, row, 5, D)).astype(BF16)
            da = (_dot_nt(dout, w2_ref[...]) * (2.0 * a)).astype(BF16)
            dh = _dot_nt(da[:, 0:ws], w1_ref[0])
            for d in range(1, N_DEV):
                dh = dh + _dot_nt(da[:, d * ws:(d + 1) * ws], w1_ref[d])
            dxn, d_shift, d_scale, d_gn = _modulate_bwd(dh, xhat, rstd, gn, scale)
            dx1_ref[0] = dx + dxn
            h_ref[0] = hb
            r_ref[0] = (a * a).astype(BF16)
            da_ref[0] = da
            dout_ref[0] = dout
            d_gate = jnp.sum(dx * mo_ref[0].astype(F32), axis=0, keepdims=True)
            dmod_ref[pl.ds(row, 1), :] += jnp.concatenate([d_shift, d_scale, d_gate], axis=-1)
            dgn_ref[0:1, :] += d_gn

        if ctx_active:
            compute()
        else:
            pl.when(t >= nC)(compute)

            @pl.when(t < nC)
            def _():
                dx1_ref[0] = dx_ref[0]
                h_ref[0] = jnp.zeros((tm, D), BF16)
                r_ref[0] = jnp.zeros((tm, F), BF16)
                da_ref[0] = jnp.zeros((tm, F), BF16)
                dout_ref[0] = jnp.zeros((tm, D), BF16)

    row = lambda w: pl.BlockSpec((1, tm, w), lambda b, t: (b, t, 0))
    sds = lambda w, dt: jax.ShapeDtypeStruct((B, T, w), dt)
    return _pcall(exch)(
        body, name="mlp_bwd", grid=(B, cfg.nT),
        in_specs=[row(D), row(D), row(D), _layer(l, D), _full((8, 6 * D)), _full((N_DEV, D, ws)), _full((F, D))],
        out_specs=[row(D), row(D), row(F), row(F), row(D), _full((8, 3 * D)), _full((8, D))],
        out_shape=[sds(D, F32), sds(D, BF16), sds(F, BF16), sds(F, BF16), sds(D, BF16),
                   jax.ShapeDtypeStruct((8, 3 * D), F32), jax.ShapeDtypeStruct((8, D), F32)],
        compiler_params=_params(2),
    )(x1, dx2, mo, gn, mod, w1, w2)


def _matmul_tn(a, g, name, *, by_shard):
    R, Ka = a.shape
    Ng = g.shape[1]
    tr = next(c for c in (512, 256, 128, 64, 32, 16, 8) if R % c == 0)
    tka = Ka if Ka <= 1024 else 1024
    if by_shard:
        ws = Ng // N_DEV
        per = next(c for c in (8, 4, 2, 1) if c * ws <= 1152 or c == 1)
        tn = per * ws
    else:
        tn = next(c for c in (1152, 1024, 768, 512, 384, 256, 128) if Ng % c == 0)
    assert Ka % tka == 0 and tn % LANES == 0
    nr = R // tr

    def body(a_ref, g_ref, o_ref, acc_ref):
        r = pl.program_id(2)

        @pl.when(r == 0)
        def _():
            acc_ref[...] = jnp.zeros(acc_ref.shape, F32)

        acc_ref[...] += _dot_tn(a_ref[...], g_ref[...])

        @pl.when(r == nr - 1)
        def _():
            if by_shard:
                for d in range(per):
                    o_ref[d] = acc_ref[:, d * ws:(d + 1) * ws].astype(BF16)
            else:
                o_ref[...] = acc_ref[...].astype(BF16)

    if by_shard:
        out_spec = pl.BlockSpec((per, tka, ws), lambda i, j, r: (j, i, 0))
        out_shape = jax.ShapeDtypeStruct((N_DEV, Ka, ws), BF16)
    else:
        out_spec = pl.BlockSpec((tka, tn), lambda i, j, r: (i, j))
        out_shape = jax.ShapeDtypeStruct((Ka, Ng), BF16)
    return pl.pallas_call(
        body, name=name, grid=(Ka // tka, Ng // tn, nr),
        in_specs=[pl.BlockSpec((tr, tka), lambda i, j, r: (r, i)), pl.BlockSpec((tr, tn), lambda i, j, r: (r, j))],
        out_specs=out_spec, out_shape=out_shape, scratch_shapes=[pltpu.VMEM((tka, tn), F32)], compiler_params=_params(3),
    )(a, g)


def _merge_bwd(cfg, dx1, mgo, oa, ob, oc, z, mod, wa, wb, wc, wo, *, ctx_active, exch=None):
    B, T, D, tm, nC = cfg.B, cfg.T, cfg.D, cfg.tm, cfg.nC
    ng = 3 * D // cfg.gw

    def body(dx_ref, mgo_ref, oa_ref, ob_ref, oc_ref, *rest):
        gate_refs = rest[:ng]
        (mod_ref, wa_ref, wb_ref, wc_ref, wo_ref,
         doa_ref, dob_ref, doc_ref, dpa_ref, dpb_ref, dpc_ref, y_ref, dmo_ref, dzg_ref, dg1_ref) = rest[ng:]
        b, t = pl.program_id(0), pl.program_id(1)
        _acc_init([dg1_ref])

        def compute():
            row = jnp.where(t < nC, B, b)
            dx = dx_ref[0]
            dg1_ref[pl.ds(row, 1), :] += jnp.sum(dx * mgo_ref[0].astype(F32), axis=0, keepdims=True)
            dmo = (dx * _mod_row(mod_ref, row, 2, D)).astype(BF16)
            dmo_ref[0] = dmo
            dy = _dot_nt(dmo, wo_ref[...])
            gates = _read_gates(cfg, gate_refs)
            y = jnp.zeros((tm, D), F32)
            dgs = []
            for gate, o_ref, w_ref, do_ref, dp_ref in ((gates[0], oa_ref, wa_ref, doa_ref, dpa_ref),
                                                      (gates[1], ob_ref, wb_ref, dob_ref, dpb_ref),
                                                      (gates[2], oc_ref, wc_ref, doc_ref, dpc_ref)):
                s = jax.nn.sigmoid(gate)
                p = _dot(o_ref[0], w_ref[...])
                y = y + s * p
                dp = (dy * s).astype(BF16)
                dp_ref[0] = dp
                do_ref[0] = _dot_nt(dp, w_ref[...]).astype(BF16)
                dgs.append((dy * p * s * (1.0 - s)).astype(BF16))
            y_ref[0] = y.astype(BF16)
            dzg_ref[0] = jnp.concatenate(dgs, axis=-1)

        if ctx_active:
            compute()
        else:
            pl.when(t >= nC)(compute)

            @pl.when(t < nC)
            def _():
                for ref in (doa_ref, dob_ref, doc_ref, dpa_ref, dpb_ref, dpc_ref, y_ref, dmo_ref, dzg_ref):
                    ref[...] = jnp.zeros(ref.shape, ref.dtype)

    row = lambda w: pl.BlockSpec((1, tm, w), lambda b, t: (b, t, 0))
    sds = lambda w: jax.ShapeDtypeStruct((B, T, w), BF16)
    return _pcall(exch)(
        body, name="merge_bwd", grid=(B, cfg.nT),
        in_specs=[row(D), row(D), row(QW), row(POOL_W), row(QW)] + _gate_specs(cfg)
        + [_full((8, 6 * D)), _full((QW, D)), _full((POOL_W, D)), _full((QW, D)), _full((D, D))],
        out_specs=[row(QW), row(POOL_W), row(QW), row(D), row(D), row(D), row(D), row(D), row(3 * D), _full((8, D))],
        out_shape=[sds(QW), sds(POOL_W), sds(QW), sds(D), sds(D), sds(D), sds(D), sds(D), sds(3 * D),
                   jax.ShapeDtypeStruct((8, D), F32)],
        compiler_params=_params(2),
    )(dx1, mgo, oa, ob, oc, *([z] * ng), mod, wa, wb, wc, wo)


def _attn_bwd(cfg, l, q, k, z, vblock, sink8, do, *, window, sink, ctx_attend, name, exch=None):
    B, S, N, T, tq, nC = cfg.B, cfg.S, cfg.N, cfg.T, cfg.tm, cfg.nC

    def body(q_ref, k_ref, v_ref, sink_ref, do_ref, dq_ref, dk_ref, dv_ref, dsink_ref):
        b, t = pl.program_id(0), pl.program_id(1)
        _acc_init([dsink_ref])

        @pl.when(t == 0)
        def _():
            dk_ref[...] = jnp.zeros(dk_ref.shape, F32)
            dv_ref[...] = jnp.zeros(dv_ref.shape, F32)

        def run(loc):
            q_t = q_ref[0]
            do_t = do_ref[0]
            dqs = [None] * N_QHEADS
            dks, dvs = [], []
            dsink_row = jnp.zeros((1, LANES), F32)
            lane = lax.broadcasted_iota(jnp.int32, (1, LANES), 1)
            lo = None
            for h in range(N_KV):
                qs, kk, vv, e, inv, e_s, lo = _attn_scores(cfg, t - nC, q_t, k_ref, v_ref, sink_ref, h, loc, window and loc, sink)
                dos = jnp.concatenate([do_t[:, (GROUP * h + g) * HEAD:(GROUP * h + g + 1) * HEAD] for g in range(GROUP)], axis=0)
                p = e * inv
                dp = _dot_nt(vv, dos)
                delta = jnp.sum(p * dp, axis=0, keepdims=True)
                ds = (p * (dp - delta)).astype(BF16)
                dq = _dot_tn(kk, ds).T
                dks.append(_dot(ds, qs))
                dvs.append(_dot(p.astype(BF16), dos))
                if sink:
                    dsk = -(e_s * inv) * delta
                    for g in range(GROUP):
                        tot = jnp.sum(dsk[:, g * tq:(g + 1) * tq], axis=1, keepdims=True)
                        dsink_row = dsink_row + jnp.where(lane == GROUP * h + g, tot, 0.0)
                for g in range(GROUP):
                    dqs[GROUP * h + g] = dq[g * tq:(g + 1) * tq] * QSCALE
            dq_ref[0] = jnp.concatenate(dqs, axis=-1)
            dk = jnp.concatenate(dks, axis=-1)
            dv = jnp.concatenate(dvs, axis=-1)
            if loc and not window:
                dk_ref[0] += dk
                dv_ref[0] += dv
            else:
                dk_ref[0, 0:N, :] += dk[0:N]
                dv_ref[0, 0:N, :] += dv[0:N]
                if loc:
                    W = tq + 2 * WINDOW
                    dk_ref[0, pl.ds(N + lo, W), :] += dk[N:]
                    dv_ref[0, pl.ds(N + lo, W), :] += dv[N:]
            if sink:
                dsink_ref[0:1, :] += dsink_row

        pl.when(t >= nC)(functools.partial(run, True))
        if ctx_attend:
            pl.when(t < nC)(functools.partial(run, False))
        else:
            @pl.when(t < nC)
            def _():
                dq_ref[0] = jnp.zeros((tq, QW), F32)

    kv = pl.BlockSpec((1, T, KVW), lambda b, t: (b, 0, 0))
    qrow = pl.BlockSpec((1, tq, QW), lambda b, t: (b, t, 0))
    return _pcall(exch)(
        body, name=name, grid=(B, cfg.nT),
        in_specs=[qrow, kv, pl.BlockSpec((1, T, KVW), lambda b, t: (b, 0, vblock)),
                  pl.BlockSpec((1, 8, LANES), lambda b, t: (l, 0, 0)), qrow],
        out_specs=[qrow, kv, kv, _full((8, LANES))],
        out_shape=[jax.ShapeDtypeStruct((B, T, QW), F32), jax.ShapeDtypeStruct((B, T, KVW), F32),
                   jax.ShapeDtypeStruct((B, T, KVW), F32), jax.ShapeDtypeStruct((8, LANES), F32)],
        compiler_params=_params(2),
    )(q, k, z, sink8, do)


def _qknorm_bwd(cfg, l, z, gvec, cosf, sins, dqa, dka, dva, dqc, dkc, dvc):
    B, T, tm = cfg.B, cfg.T, cfg.tm

    def body(z_ref, g_ref, cos_ref, sin_ref, dqa_ref, dka_ref, dva_ref, dqc_ref, dkc_ref, dvc_ref, dz_ref, dg_ref):
        _acc_init([dg_ref])
        ind = _head_indicator()
        cos, sin = cos_ref[...], sin_ref[...]
        dqa_t, dqc_t = dqa_ref[0], dqc_ref[0]
        douts = {0: dqa_t[:, 0:128], 1: dqa_t[:, 128:256], 2: dqa_t[:, 256:384], 3: dka_ref[0],
                 5: dqc_t[:, 0:128], 6: dqc_t[:, 128:256], 7: dqc_t[:, 256:384], 8: dkc_ref[0]}
        pieces = []
        dgs = []
        for c in range(QKV_W // LANES):
            if c not in douts:
                pieces.append(dva_ref[0] if c == 4 else dvc_ref[0])
                dgs.append(jnp.zeros((1, LANES), F32))
                continue
            x = z_ref[0, :, c * LANES:(c + 1) * LANES]
            g = g_ref[0, :, c * LANES:(c + 1) * LANES]
            ss = _head_sum(x * x, ind)
            rstd = lax.rsqrt(ss * (1.0 / HEAD) + EPS)
            n = x * rstd
            dout = douts[c]
            dy = dout * cos + _pair_swap(dout * sin)
            dgs.append(jnp.sum(dy * n, axis=0, keepdims=True))
            dn = dy * g
            mean = _head_sum(dn * n, ind) * (1.0 / HEAD)
            pieces.append(rstd * (dn - n * mean))
        dz_ref[0] = jnp.concatenate(pieces, axis=-1).astype(BF16)
        dg_ref[0:1, :] += jnp.concatenate(dgs, axis=-1)

    row = lambda w: pl.BlockSpec((1, tm, w), lambda b, t: (b, t, 0))
    tab = pl.BlockSpec((tm, LANES), lambda b, t: (t, 0))
    return pl.pallas_call(
        body, name="qknorm_bwd", grid=(B, cfg.nT),
        in_specs=[row(QKV_W), _layer(l, QKV_W), tab, tab, row(QW), row(KVW), row(KVW), row(QW), row(KVW), row(KVW)],
        out_specs=[row(QKV_W), _full((8, QKV_W))],
        out_shape=[jax.ShapeDtypeStruct((B, T, QKV_W), BF16), jax.ShapeDtypeStruct((8, QKV_W), F32)],
        compiler_params=_params(2),
    )(z, gvec, cosf, sins, dqa, dka, dva, dqc, dkc, dvc)


def _pool_bwd(cfg, l, dob, pooled, wp, ps):
    B, T, tm, kw = cfg.B, cfg.T, cfg.tm, cfg.kw

    def body(dob_ref, pooled_ref, wp_ref, ps_ref, du_ref, dwp_ref, dps_ref):
        t = pl.program_id(1)
        _acc_init([dwp_ref, dps_ref])
        r0, seg_lo, seg_hi, k0 = _pool_geometry(cfg, t)
        ps = ps_ref[0]
        wp = wp_ref[0]
        dmix = dob_ref[0, pl.ds(r0, tm), :].astype(F32)
        pooled = pooled_ref[0]
        dps_ref[0:1, :] += jnp.sum(dmix * _dot(pooled, wp), axis=0, keepdims=True)
        dpm = (dmix * ps).astype(BF16)
        dwp_ref[...] += _dot_tn(pooled, dpm)
        dpooled_t = _dot_nt(dpm, wp)
        dpm_w = (dob_ref[0, pl.ds(k0, kw), :].astype(F32) * ps).astype(BF16)
        dpooled_w = _dot_nt(dpm_w, wp)
        rr = r0 + lax.broadcasted_iota(jnp.int32, (tm, kw), 0)
        cc = k0 + lax.broadcasted_iota(jnp.int32, (tm, kw), 1)
        diff = rr - cc
        inseg = (cc >= seg_lo) & (cc < seg_hi)
        ccol = k0 + lax.broadcasted_iota(jnp.int32, (kw, 1), 0)
        group = lax.broadcasted_iota(jnp.int32, (tm, POOL_W), 1) // HEAD
        acc = jnp.zeros((tm, POOL_W), F32)
        for gi, w in enumerate(POOL_WINDOWS):
            h = w // 2
            band_t = jnp.where((diff >= -h) & (diff <= h - 1) & inseg, 1.0, 0.0).astype(BF16)
            hi, lo = _split_bf16(dpooled_w / _pool_count(ccol, h, seg_lo, seg_hi))
            acc = jnp.where(group == gi, _dot(band_t, hi) + _dot(band_t, lo), acc)
        du_ref[0] = (acc - dpooled_t).astype(BF16)

    row = pl.BlockSpec((1, tm, POOL_W), lambda b, t: (b, t, 0))
    return pl.pallas_call(
        body, name="pool_bwd", grid=(B, cfg.nT),
        in_specs=[pl.BlockSpec((1, T, POOL_W), lambda b, t: (b, 0, 0)), row,
                  pl.BlockSpec((1, POOL_W, POOL_W), lambda b, t: (l, 0, 0)), _layer(l, POOL_W)],
        out_specs=[row, _full((POOL_W, POOL_W)), _full((8, POOL_W))],
        out_shape=[jax.ShapeDtypeStruct((B, T, POOL_W), BF16), jax.ShapeDtypeStruct((POOL_W, POOL_W), F32),
                   jax.ShapeDtypeStruct((8, POOL_W), F32)],
        compiler_params=_params(2),
    )(dob, pooled, wp, ps)


def _in_proj_bwd(cfg, l, dzq, du, dzg, w_in, x, dx1, gn, mod, *, latent_only, exch=None):
    B, S, T, D, IN, tm, nC = cfg.B, cfg.S, cfg.T, cfg.D, cfg.IN, cfg.tm, cfg.nC

    def body(dzq_ref, du_ref, dzg_ref, w_ref, x_ref, dx1_ref, gn_ref, mod_ref, dx0_ref, dz_ref, dmod_ref, dgn_ref):
        b, t = pl.program_id(0), pl.program_id(1)
        _acc_init([dmod_ref, dgn_ref])
        row = jnp.where(t < nC, B, b)
        dz = jnp.concatenate([dzq_ref[0], du_ref[0], dzg_ref[0]], axis=-1)
        dz_ref[0] = dz
        dh = _dot_nt(dz, w_ref[...])
        gn = gn_ref[0]
        scale = _mod_row(mod_ref, row, 1, D)
        _, xhat, rstd = _modulate(x_ref[0], gn, _mod_row(mod_ref, row, 0, D), scale)
        dxn, d_shift, d_scale, d_gn = _modulate_bwd(dh, xhat, rstd, gn, scale)
        dx0_ref[0] = dx1_ref[0] + dxn
        dmod_ref[pl.ds(row, 1), :] += jnp.concatenate([d_shift, d_scale], axis=-1)
        dgn_ref[0:1, :] += d_gn

    row = lambda w: pl.BlockSpec((1, tm, w), lambda b, t: (b, t, 0))
    if latent_only:
        dx0_spec = pl.BlockSpec((1, tm, D), lambda b, t: (b, jnp.maximum(t - nC, 0), 0))
        dx0_shape = jax.ShapeDtypeStruct((B, S, D), F32)
    else:
        dx0_spec, dx0_shape = row(D), jax.ShapeDtypeStruct((B, T, D), F32)
    return _pcall(exch)(
        body, name="in_proj_bwd", grid=(B, cfg.nT),
        in_specs=[row(QKV_W), row(POOL_W), row(3 * D), _full((D, IN)), row(D), row(D), _layer(l, D), _full((8, 6 * D))],
        out_specs=[dx0_spec, row(IN), _full((8, 2 * D)), _full((8, D))],
        out_shape=[dx0_shape, jax.ShapeDtypeStruct((B, T, IN), BF16),
                   jax.ShapeDtypeStruct((8, 2 * D), F32), jax.ShapeDtypeStruct((8, D), F32)],
        compiler_params=_params(2),
    )(dzq, du, dzg, w_in, x, dx1, gn, mod)


def _adaln_bwd(cfg, l, cc_all, dm_all, w_ada):
    d, B = cfg.D, cfg.B
    wa = w_ada.shape[2]

    def body(c_ref, dm_ref, w_ref, dw_ref, dc_ref):
        c = c_ref[...]
        s = jax.nn.sigmoid(c)
        dmb = dm_ref[...].astype(BF16)
        dw_ref[...] = _dot_tn((c * s).astype(BF16), dmb)
        dc = _dot_nt(dmb, w_ref[0].astype(BF16)) * (s * (1.0 + c * (1.0 - s)))
        is_ctx = lax.broadcasted_iota(jnp.int32, (8 * N_DEV, 1), 0) % 8 == B
        dc_ref[...] = jnp.broadcast_to(jnp.sum(jnp.where(is_ctx, dc, 0.0), axis=0, keepdims=True), (8, d))

    return pl.pallas_call(
        body, name="adaln_bwd", grid=(1,),
        in_specs=[_full((8 * N_DEV, d)), _full((8 * N_DEV, wa)), pl.BlockSpec((1, d, wa), lambda *_: (l, 0, 0))],
        out_specs=[_full((d, wa)), _full((8, d))],
        out_shape=[jax.ShapeDtypeStruct((d, wa), F32), jax.ShapeDtypeStruct((8, d), F32)],
        compiler_params=_params(1),
    )(cc_all, dm_all, w_ada)


def _dmod_pack(cfg, dmod_in, dg1, dmod_mlp):
    d = cfg.D
    wa = 6 * d // N_DEV

    def body(din_ref, dg1_ref, dmlp_ref, o_ref, db_ref):
        dm = jnp.concatenate([din_ref[...], dg1_ref[...], dmlp_ref[...]], axis=-1)
        for j in range(N_DEV):
            o_ref[j] = dm[:, j * wa:(j + 1) * wa]
        db_ref[...] = jnp.broadcast_to(jnp.sum(dm, axis=0, keepdims=True), (8, 6 * d))

    return pl.pallas_call(
        body, name="dmod_pack", grid=(1,),
        in_specs=[_full((8, 2 * d)), _full((8, d)), _full((8, 3 * d))],
        out_specs=[_full((N_DEV, 8, wa)), _full((8, 6 * d))],
        out_shape=[jax.ShapeDtypeStruct((N_DEV, 8, wa), F32), jax.ShapeDtypeStruct((8, 6 * d), F32)],
        compiler_params=_params(1),
    )(dmod_in, dg1, dmod_mlp)


def _adam_update(g, w, m, v):
    bc1 = 1.0 - ADAM_B1 ** ADAM_STEP
    bc2 = 1.0 - ADAM_B2 ** ADAM_STEP
    m2 = ADAM_B1 * m + (1.0 - ADAM_B1) * g
    v2 = ADAM_B2 * v + (1.0 - ADAM_B2) * (g * g)
    delta = -ADAM_LR * ((m2 / bc1) / (jnp.sqrt(v2 / bc2) + ADAM_EPS) + ADAM_WD * w)
    return delta, m2, v2


def _sum_parts(p_ref):
    g = p_ref[0].astype(F32)
    for d in range(1, p_ref.shape[0]):
        g = g + p_ref[d].astype(F32)
    return g


def _adamw_sharded(parts, w, m, v, name):
    L, K, W = w.shape
    P = parts[0].shape[0]
    tk = next(c for c in (256, 128, 64, 32, 16, 8) if K % c == 0)

    def body(*refs):
        p_refs = refs[:L]
        w_ref, m_ref, v_ref, g_ref, d_ref, m2_ref, v2_ref = refs[L:]
        layer = pl.program_id(0)

        def run(p_ref):
            g = _sum_parts(p_ref)
            delta, m2, v2 = _adam_update(g, w_ref[0], m_ref[0], v_ref[0])
            g_ref[0] = g
            d_ref[0] = delta
            m2_ref[0] = m2
            v2_ref[0] = v2

        for li in range(L):
            pl.when(layer == li)(functools.partial(run, p_refs[li]))

    blk = pl.BlockSpec((1, tk, W), lambda l, i: (l, i, 0))
    part_spec = lambda li: pl.BlockSpec((P, tk, W), lambda l, i: (0, jnp.where(l == li, i, 0), 0))
    return pl.pallas_call(
        body, name=name, grid=(L, K // tk),
        in_specs=[part_spec(li) for li in range(L)] + [blk, blk, blk],
        out_specs=[blk] * 4, out_shape=[jax.ShapeDtypeStruct((L, K, W), F32)] * 4,
        compiler_params=_params(2),
    )(*parts, w, m, v)


def _adamw_packed(parts, w, m, v, name):
    rows = w.shape[0]
    tr = next(c for c in (256, 128, 64, 32, 16, 8) if rows % c == 0)

    def body(p_ref, w_ref, m_ref, v_ref, g_ref, d_ref, m2_ref, v2_ref):
        g = _sum_parts(p_ref)
        delta, m2, v2 = _adam_update(g, w_ref[...], m_ref[...], v_ref[...])
        g_ref[...] = g
        d_ref[...] = delta
        m2_ref[...] = m2
        v2_ref[...] = v2

    blk = pl.BlockSpec((tr, PACK_W), lambda i: (i, 0))
    return pl.pallas_call(
        body, name=name, grid=(rows // tr,),
        in_specs=[pl.BlockSpec((N_DEV, tr, PACK_W), lambda i: (0, i, 0)), blk, blk, blk],
        out_specs=[blk] * 4, out_shape=[jax.ShapeDtypeStruct((rows, PACK_W), F32)] * 4,
        compiler_params=_params(1),
    )(parts, w, m, v)


_SHARDED = dict(w_ada=True, w_in=True, w_br_a=True, w_br_b=True, w_br_c=True, w_out=False, w_mlp1=True, w_mlp2=False)
_GROUPS = (("w_in",), ("w_br_a", "w_br_b", "w_br_c", "w_out"), ("w_mlp1", "w_mlp2"))
_KEEP_SHARDS = ("w_mlp1",)
_SMALL = ("c_ctx", "b_ada", "norm1", "norm2", "q_norm_a", "k_norm_a", "q_norm_c", "k_norm_c", "sink_c", "w_pool", "pool_scale")


def _from_shards(name, g):
    n, k, w = g.shape
    if name in _KEEP_SHARDS:
        return g
    if _SHARDED[name]:
        return g.transpose(1, 0, 2).reshape(k, n * w)
    return g.reshape(n * k, w)


def _to_shards(name, g):
    if g.ndim == 3:
        return g
    if _SHARDED[name]:
        k, nw = g.shape
        return g.reshape(k, N_DEV, nw // N_DEV).transpose(1, 0, 2)
    nk, w = g.shape
    return g.reshape(N_DEV, nk // N_DEV, w)


def _pack_small(vals):
    flat = jnp.concatenate([vals[n].reshape(-1) for n in _SMALL])
    rows = -(-flat.shape[0] // (8 * PACK_W)) * 8
    return jnp.pad(flat, (0, rows * PACK_W - flat.shape[0])).reshape(rows, PACK_W)


def _unpack_small(packed, like):
    flat, out, r = packed.reshape(-1), {}, 0
    for n in _SMALL:
        sz = like[n].size
        out[n] = flat[r:r + sz].reshape(like[n].shape)
        r += sz
    return out


def _rope_tables(cfg):
    pos = jnp.arange(cfg.S, dtype=F32)
    r = jnp.floor(pos / GRID_W)
    col = pos - r * GRID_W
    inv = 1.0 / (ROPE_THETA ** (jnp.arange(0, HEAD // 2, 2, dtype=F32) / (HEAD // 2)))
    ang = jnp.concatenate([r[:, None] * inv, col[:, None] * inv], axis=-1)
    cos = jnp.repeat(jnp.cos(ang), 2, axis=-1)
    sin = jnp.repeat(jnp.sin(ang), 2, axis=-1) * jnp.tile(jnp.array([-1.0, 1.0], F32), HEAD // 2)
    cos = jnp.concatenate([jnp.ones((cfg.N, HEAD), F32), cos], axis=0)
    sin = jnp.concatenate([jnp.zeros((cfg.N, HEAD), F32), sin], axis=0)
    return jnp.tile(cos, (1, 2)), jnp.tile(sin, (1, 2))


def _gvec(qa, ka, qc, kc):
    one = jnp.ones((qa.shape[0], KVW), F32)
    t = lambda a, n: jnp.tile(a, (1, n))
    return jnp.concatenate([t(qa, N_QHEADS), t(ka, N_KV), one, t(qc, N_QHEADS), t(kc, N_KV), one], axis=-1)[:, None, :]


def _block_diag(wp):
    L, g, c, _ = wp.shape
    eye = jnp.eye(g, dtype=wp.dtype)
    return (wp[:, :, :, None, :] * eye[None, :, None, :, None]).reshape(L, g * c, g * c)


def _pad8(a):
    return jnp.pad(a, ((0, 8 - a.shape[0]), (0, 0)))


def kernel(x, c, ctx, c_ctx, w_ada, b_ada, norm1, norm2, w_in, q_norm_a, k_norm_a, q_norm_c, k_norm_c, sink_c, w_pool, pool_scale, w_br_a, w_br_b, w_br_c, w_out, w_mlp1, w_mlp2, loss_target, m_c_ctx, m_w_ada, m_b_ada, m_norm1, m_norm2, m_w_in, m_q_norm_a, m_k_norm_a, m_q_norm_c, m_k_norm_c, m_sink_c, m_w_pool, m_pool_scale, m_w_br_a, m_w_br_b, m_w_br_c, m_w_out, m_w_mlp1, m_w_mlp2, v_c_ctx, v_w_ada, v_b_ada, v_norm1, v_norm2, v_w_in, v_q_norm_a, v_k_norm_a, v_q_norm_c, v_k_norm_c, v_sink_c, v_w_pool, v_pool_scale, v_w_br_a, v_w_br_b, v_w_br_c, v_w_out, v_w_mlp1, v_w_mlp2):
    B, S, D = x.shape
    N = ctx.shape[1]
    L = w_ada.shape[0]
    cfg = _Cfg(B, S, N, D)
    T = cfg.T
    weights = dict(c_ctx=c_ctx, w_ada=w_ada, b_ada=b_ada, norm1=norm1, norm2=norm2, w_in=w_in, q_norm_a=q_norm_a,
                   k_norm_a=k_norm_a, q_norm_c=q_norm_c, k_norm_c=k_norm_c, sink_c=sink_c, w_pool=w_pool,
                   pool_scale=pool_scale, w_br_a=w_br_a, w_br_b=w_br_b, w_br_c=w_br_c, w_out=w_out, w_mlp1=w_mlp1, w_mlp2=w_mlp2)
    mom_m = dict(c_ctx=m_c_ctx, w_ada=m_w_ada, b_ada=m_b_ada, norm1=m_norm1, norm2=m_norm2, w_in=m_w_in, q_norm_a=m_q_norm_a,
                 k_norm_a=m_k_norm_a, q_norm_c=m_q_norm_c, k_norm_c=m_k_norm_c, sink_c=m_sink_c, w_pool=m_w_pool,
                 pool_scale=m_pool_scale, w_br_a=m_w_br_a, w_br_b=m_w_br_b, w_br_c=m_w_br_c, w_out=m_w_out, w_mlp1=m_w_mlp1, w_mlp2=m_w_mlp2)
    mom_v = dict(c_ctx=v_c_ctx, w_ada=v_w_ada, b_ada=v_b_ada, norm1=v_norm1, norm2=v_norm2, w_in=v_w_in, q_norm_a=v_q_norm_a,
                 k_norm_a=v_k_norm_a, q_norm_c=v_q_norm_c, k_norm_c=v_k_norm_c, sink_c=v_sink_c, w_pool=v_w_pool,
                 pool_scale=v_pool_scale, w_br_a=v_w_br_a, w_br_b=v_w_br_b, w_br_c=v_w_br_c, w_out=v_w_out, w_mlp1=v_w_mlp1, w_mlp2=v_w_mlp2)

    shards_bf16 = {n: weights[n].astype(BF16) for group in _GROUPS for n in group}
    full = [dict() for _ in range(L)]

    def gather_of(l, gi):
        return _Exchange([(shards_bf16[n], l) for n in _GROUPS[gi]], scatter=False)

    def gathered(l, gi, arrs):
        for n, a in zip(_GROUPS[gi], arrs):
            full[l][n] = _from_shards(n, a)

    gathered(0, 0, gather_of(0, 0).alone("gather_first_weights"))

    def hosting(fn, *a, exch=None, done=None, **kw):
        if exch is None:
            return fn(*a, **kw)
        res = fn(*a, exch=exch, **kw)
        done(res[-exch.n:])
        own = res[:-exch.n]
        return own[0] if len(own) == 1 else own

    def gather_behind(l, gi):
        if l >= L:
            return {}
        return dict(exch=gather_of(l, gi), done=functools.partial(gathered, l, gi))

    cosf, sins = _rope_tables(cfg)
    xs = jnp.concatenate([ctx, x], axis=1)
    cc8 = _pad8(jnp.concatenate([c, c_ctx[None, :]], axis=0))
    va_blk, vc_blk = (QW + KVW) // KVW, (2 * QW + 3 * KVW) // KVW
    per_layer = lambda a: a[:, None, :]
    b_ada3, norm1_3, norm2_3, ps3 = per_layer(b_ada), per_layer(norm1), per_layer(norm2), per_layer(pool_scale)
    gvec = _gvec(q_norm_a, k_norm_a, q_norm_c, k_norm_c)
    sink8 = jnp.pad(sink_c[:, None, :], ((0, 0), (0, 7), (0, LANES - N_QHEADS)))
    wp = _block_diag(w_pool).astype(BF16)

    cc_all = _Exchange([cc8], scatter=False).alone("gather_cond")[0].reshape(8 * N_DEV, D)
    mod_cols = _Exchange([_adaln_fwd(cfg, cc_all, w_ada)], scatter=True).alone("scatter_mod")[0]
    mod_all = _adaln_join(cfg, mod_cols, b_ada3)

    saved = []
    for l in range(L):
        fw = full[l]
        ctx_active = l < L - 1
        mod = mod_all[l]
        z, h = hosting(_in_proj_fwd, cfg, l, xs, norm1_3, mod, fw["w_in"], **(gather_behind(0, 1) if l == 0 else {}))
        qa, ka, qc, kc = _qknorm_fwd(cfg, l, z, gvec, cosf, sins)
        oa = hosting(_attn_fwd, cfg, l, qa, ka, z, va_blk, sink8, window=False, sink=False, ctx_attend=ctx_active, name="attn_a_fwd",
                     **gather_behind(l + 1, 0))
        oc = hosting(_attn_fwd, cfg, l, qc, kc, z, vc_blk, sink8, window=True, sink=True, ctx_attend=ctx_active, name="attn_c_fwd",
                     **(gather_behind(0, 2) if l == 0 else {}))
        ob, pooled = _pool_fwd(cfg, l, z, wp, ps3)
        x1, mgo = hosting(_merge_fwd, cfg, xs, oa, ob, oc, z, mod, fw["w_br_a"], fw["w_br_b"], fw["w_br_c"], fw["w_out"],
                          ctx_active=ctx_active, **gather_behind(l + 1, 1))
        x2, mo = hosting(_mlp_fwd, cfg, l, x1, norm2_3, mod, fw["w_mlp1"], fw["w_mlp2"], ctx_active=ctx_active,
                         **gather_behind(l + 1, 2))
        saved.append(dict(xs=xs, mod=mod, z=z, h=h, qa=qa, ka=ka, qc=qc, kc=kc, oa=oa, oc=oc, ob=ob, pooled=pooled, x1=x1, mgo=mgo, mo=mo))
        xs = x2

    dxs, sse = _loss_fwd_bwd(cfg, xs, loss_target)
    loss = lax.psum(0.5 * sse[0, 0] / D, ("x", "y", "c"))

    grads = [dict() for _ in range(L)]
    parts = {}
    small = {n: [None] * L for n in _SMALL if n != "c_ctx"}
    d_c_ctx = jnp.zeros((D,), F32)
    flat2 = lambda a: a.reshape(B * T, a.shape[-1])

    def scatter_of(l, gi):
        return _Exchange([_to_shards(n, grads[l][n]) for n in _GROUPS[gi]], scatter=True)

    def scattered(l, gi, arrs):
        for n, a in zip(_GROUPS[gi], arrs):
            parts[(l, n)] = a

    def scatter_behind(l, gi):
        if l >= L:
            return {}
        return dict(exch=scatter_of(l, gi), done=functools.partial(scattered, l, gi))

    for l in reversed(range(L)):
        fw, sv, g = full[l], saved[l], grads[l]
        ctx_active = l < L - 1
        mod = sv["mod"]
        dx1, h2, r, da, dout, dmod_mlp, dgn2 = hosting(_mlp_bwd, cfg, l, sv["x1"], dxs, sv["mo"], norm2_3, mod, fw["w_mlp1"], fw["w_mlp2"],
                                                       ctx_active=ctx_active, **scatter_behind(l + 1, 2))
        g["w_mlp1"] = _matmul_tn(flat2(h2), flat2(da), "dw_mlp1", by_shard=True)
        g["w_mlp2"] = _matmul_tn(flat2(r), flat2(dout), "dw_mlp2", by_shard=False)
        doa, dob, doc, dpa, dpb, dpc, y, dmo, dzg, dg1 = hosting(
            _merge_bwd, cfg, dx1, sv["mgo"], sv["oa"], sv["ob"], sv["oc"], sv["z"], mod, fw["w_br_a"], fw["w_br_b"], fw["w_br_c"], fw["w_out"],
            ctx_active=ctx_active, **scatter_behind(l + 1, 1))
        g["w_out"] = _matmul_tn(flat2(y), flat2(dmo), "dw_out", by_shard=False)
        g["w_br_a"] = _matmul_tn(flat2(sv["oa"]), flat2(dpa), "dw_br_a", by_shard=True)
        g["w_br_b"] = _matmul_tn(flat2(sv["ob"]), flat2(dpb), "dw_br_b", by_shard=True)
        g["w_br_c"] = _matmul_tn(flat2(sv["oc"]), flat2(dpc), "dw_br_c", by_shard=True)
        z = sv["z"]
        dqa, dka, dva, _ = hosting(_attn_bwd, cfg, l, sv["qa"], sv["ka"], z, va_blk, sink8, doa, window=False, sink=False,
                                   ctx_attend=ctx_active, name="attn_a_bwd", **scatter_behind(l + 1, 0))
        dqc, dkc, dvc, dsink = hosting(_attn_bwd, cfg, l, sv["qc"], sv["kc"], z, vc_blk, sink8, doc, window=True, sink=True,
                                       ctx_attend=ctx_active, name="attn_c_bwd", **(scatter_behind(0, 2) if l == 0 else {}))
        dzq, dgvec = _qknorm_bwd(cfg, l, z, gvec, cosf, sins, dqa, dka, dva, dqc, dkc, dvc)
        du, dwp, dps = _pool_bwd(cfg, l, dob, sv["pooled"], wp, ps3)
        dxs, dz, dmod_in, dgn1 = hosting(_in_proj_bwd, cfg, l, dzq, du, dzg, fw["w_in"], sv["xs"], dx1, norm1_3, mod,
                                         latent_only=(l == 0), **(scatter_behind(0, 1) if l == 0 else {}))
        g["w_in"] = _matmul_tn(flat2(sv["h"]), flat2(dz), "dw_in", by_shard=False)
        dmod_cols, dbias = _dmod_pack(cfg, dmod_in, dg1, dmod_mlp)
        dm_all = _Exchange([dmod_cols], scatter=True).alone("scatter_dmod")[0].reshape(8 * N_DEV, -1)
        g["w_ada"], dcc = _adaln_bwd(cfg, l, cc_all, dm_all, w_ada)
        d_c_ctx = d_c_ctx + dcc[0]
        gv = dgvec[0]
        heads = lambda v, n: v.reshape(n, HEAD).sum(axis=0)
        small["b_ada"][l] = dbias[0]
        small["norm1"][l] = dgn1[0]
        small["norm2"][l] = dgn2[0]
        small["q_norm_a"][l] = heads(gv[0:QW], N_QHEADS)
        small["k_norm_a"][l] = heads(gv[QW:QW + KVW], N_KV)
        small["q_norm_c"][l] = heads(gv[QW + 2 * KVW:2 * QW + 2 * KVW], N_QHEADS)
        small["k_norm_c"][l] = heads(gv[2 * QW + 2 * KVW:2 * QW + 3 * KVW], N_KV)
        small["sink_c"][l] = dsink[0, :N_QHEADS]
        small["w_pool"][l] = jnp.stack([dwp[i * HEAD:(i + 1) * HEAD, i * HEAD:(i + 1) * HEAD] for i in range(len(POOL_WINDOWS))])
        small["pool_scale"][l] = dps[0]
    grad_x = dxs

    scattered(0, 0, scatter_of(0, 0).alone("scatter_last_grads"))
    for l in range(L):
        parts[(l, "w_ada")] = grads[l]["w_ada"][None]
    stepped = {n: _adamw_sharded([parts[(l, n)] for l in range(L)], weights[n], mom_m[n], mom_v[n], "adamw_" + n) for n in _SHARDED}

    small_vals = {n: jnp.stack(v) for n, v in small.items()}
    small_vals["c_ctx"] = d_c_ctx
    small_parts = _Exchange([_pack_small(small_vals)], scatter=False).alone("gather_small_grads")[0]
    stepped_small = _adamw_packed(small_parts, _pack_small(weights), _pack_small(mom_m), _pack_small(mom_v), "adamw_small")

    outs = []
    for i in range(4):
        res = {n: stepped[n][i] for n in _SHARDED}
        res.update(_unpack_small(stepped_small[i], weights))
        outs.append(res)
    order = ("c_ctx", "w_ada", "b_ada", "norm1", "norm2", "w_in", "q_norm_a", "k_norm_a", "q_norm_c", "k_norm_c", "sink_c",
             "w_pool", "pool_scale", "w_br_a", "w_br_b", "w_br_c", "w_out", "w_mlp1", "w_mlp2")
    return (loss, grad_x, *[res[n] for res in outs for n in order])
```

```python
import functools

import jax
import jax.numpy as jnp
from jax import lax
from jax.experimental import pallas as pl
from jax.experimental.pallas import tpu as pltpu

F32 = jnp.float32
BF16 = jnp.bfloat16
HIGHEST = lax.Precision.HIGHEST

N_DEV = 8
HEAD = 64
N_QHEADS = 6
N_KV = 2
GROUP = 3
QW = N_QHEADS * HEAD
KVW = N_KV * HEAD
QKV_W = 2 * (QW + 2 * KVW)
POOL_W = 256
POOL_WINDOWS = (2, 4, 8, 16)
GATE0 = QKV_W + POOL_W
WINDOW = 128
GRID_W = 64
ROPE_THETA = 10000.0
EPS = 1e-6
NEG = -1e30
QSCALE = HEAD ** -0.5
LANES = 128
PACK_W = 1024
VMEM_LIMIT = 56 * 1024 * 1024

ADAM_LR = 0.001
ADAM_B1 = 0.9
ADAM_B2 = 0.999
ADAM_EPS = 1e-08
ADAM_WD = 0.01
ADAM_STEP = 10

NT_DIMS = (((1,), (1,)), ((), ()))
TN_DIMS = (((0,), (0,)), ((), ()))


def _dot(a, b):
    return jnp.dot(a, b, preferred_element_type=F32)


def _dot_nt(a, b):
    return lax.dot_general(a, b, NT_DIMS, preferred_element_type=F32)


def _dot_tn(a, b):
    return lax.dot_general(a, b, TN_DIMS, preferred_element_type=F32)


def _params(n_grid):
    return pltpu.CompilerParams(dimension_semantics=("arbitrary",) * n_grid, vmem_limit_bytes=VMEM_LIMIT)


def _full(shape):
    nd = len(shape)
    return pl.BlockSpec(shape, lambda *_: (0,) * nd)


def _layer(l, width):
    return pl.BlockSpec((1, 1, width), lambda *_: (l, 0, 0))


def _modulate(x, gn, shift, scale):
    rstd = lax.rsqrt(jnp.mean(x * x, axis=-1, keepdims=True) + EPS)
    xhat = x * rstd
    return xhat * gn * (1.0 + scale) + shift, xhat, rstd


def _modulate_bwd(dh, xhat, rstd, gn, scale):
    d_shift = jnp.sum(dh, axis=0, keepdims=True)
    d_scale = jnp.sum(dh * xhat * gn, axis=0, keepdims=True)
    dy = dh * (1.0 + scale)
    d_gn = jnp.sum(dy * xhat, axis=0, keepdims=True)
    dxh = dy * gn
    dx = rstd * (dxh - xhat * jnp.mean(dxh * xhat, axis=-1, keepdims=True))
    return dx, d_shift, d_scale, d_gn


def _mod_row(mod_ref, row, k, d):
    return mod_ref[pl.ds(row, 1), k * d:(k + 1) * d]


class _Cfg:
    def __init__(self, b, s, n, d):
        self.B, self.S, self.N, self.D = b, s, n, d
        self.T = n + s
        self.F = 4 * d
        self.IN = GATE0 + 3 * d
        self.tm = 256 if (n % 256 == 0 and s % 256 == 0) else 128
        self.nT = self.T // self.tm
        self.nC = n // self.tm
        self.gw = 512 if d % 512 == 0 else 256
        self.kw = self.tm + 2 * LANES
        assert GATE0 % self.gw == 0 and d % self.gw == 0 and b < 8 and self.T >= self.kw and max(POOL_WINDOWS) // 2 <= LANES
        assert s % GRID_W == 0 and n % self.tm == 0 and s % self.tm == 0 and s >= self.tm + 2 * WINDOW
        assert d % (N_DEV * LANES) == 0


def _peer(k):
    x, y, c = lax.axis_index("x"), lax.axis_index("y"), lax.axis_index("c")
    px = x ^ ((k >> 2) & 1)
    py = y ^ ((k >> 1) & 1)
    pc = c ^ (k & 1)
    return (px, py, pc), 4 * px + 2 * py + pc


class _Exchange:
    def __init__(self, arrays, scatter):
        self.arrays = [a if isinstance(a, tuple) else (a, None) for a in arrays]
        self.scatter = scatter
        self.n = len(self.arrays)

    def operands(self):
        return [a for a, _ in self.arrays]

    def out_shapes(self):
        res = []
        for a, layer in self.arrays:
            shape = a.shape[1:] if (self.scatter or layer is not None) else a.shape
            res.append(jax.ShapeDtypeStruct((N_DEV,) + tuple(shape), a.dtype))
        return res

    def scratch(self):
        n = self.n * (N_DEV - 1)
        return [pltpu.SemaphoreType.DMA((n,)), pltpu.SemaphoreType.DMA((n,)), pltpu.SemaphoreType.DMA((self.n,))]

    def _copies(self, x_refs, out_refs, send_sems, recv_sems, local_sems, want):
        _, me = _peer(0)
        res = []
        for i, ((_, layer), x_ref, out_ref) in enumerate(zip(self.arrays, x_refs, out_refs)):
            if self.scatter:
                src_of = lambda d, x_ref=x_ref: x_ref.at[d]
            elif layer is not None:
                src_of = lambda d, x_ref=x_ref, layer=layer: x_ref.at[layer]
            else:
                src_of = lambda d, x_ref=x_ref: x_ref
            if want == "local":
                res.append(pltpu.make_async_copy(src_of(me), out_ref.at[me], local_sems.at[i]))
                continue
            for k in range(1, N_DEV):
                pos, idx = _peer(k)
                j = i * (N_DEV - 1) + k - 1
                common = dict(send_sem=send_sems.at[j], recv_sem=recv_sems.at[j], device_id=pos, device_id_type=pl.DeviceIdType.MESH)
                if want == "send":
                    res.append(pltpu.make_async_remote_copy(src_ref=src_of(idx), dst_ref=out_ref.at[me], **common))
                else:
                    res.append(pltpu.make_async_remote_copy(src_ref=src_of(me), dst_ref=out_ref.at[idx], **common))
        return res

    def start(self, *refs):
        for cp in self._copies(*refs, "local") + self._copies(*refs, "send"):
            cp.start()

    def wait(self, *refs):
        for cp in self._copies(*refs, "recv"):
            cp.wait_recv()
        for cp in self._copies(*refs, "send"):
            cp.wait_send()
        for cp in self._copies(*refs, "local"):
            cp.wait()

    def alone(self, name):
        n = self.n

        def body(*refs):
            args = (refs[:n], refs[n:2 * n], *refs[2 * n:])
            self.start(*args)
            self.wait(*args)

        any_spec = pl.BlockSpec(memory_space=pl.ANY)
        return pl.pallas_call(body, name=name, in_specs=[any_spec] * n, out_specs=[any_spec] * n,
                              out_shape=self.out_shapes(), scratch_shapes=self.scratch())(*self.operands())


def _pcall(exch):
    if exch is None:
        return pl.pallas_call

    def make(body, *, name, grid, in_specs, out_specs, out_shape, compiler_params, scratch_shapes=()):
        multi = isinstance(out_shape, (list, tuple))
        out_specs_l = list(out_specs) if multi else [out_specs]
        out_shape_l = list(out_shape) if multi else [out_shape]
        n_in, n_out, n_x, n_s = len(in_specs), len(out_specs_l), exch.n, len(scratch_shapes)

        def hosted(*refs):
            ins, x_refs = refs[:n_in], refs[n_in:n_in + n_x]
            o0 = n_in + n_x
            outs, xo_refs = refs[o0:o0 + n_out], refs[o0 + n_out:o0 + n_out + n_x]
            s0 = o0 + n_out + n_x
            own_scratch, sems = refs[s0:s0 + n_s], refs[s0 + n_s:]
            ids = [pl.program_id(i) for i in range(len(grid))]
            first = functools.reduce(jnp.logical_and, [i == 0 for i in ids])
            last = functools.reduce(jnp.logical_and, [i == g - 1 for i, g in zip(ids, grid)])

            @pl.when(first)
            def _():
                exch.start(x_refs, xo_refs, *sems)

            body(*ins, *outs, *own_scratch)

            @pl.when(last)
            def _():
                exch.wait(x_refs, xo_refs, *sems)

        any_spec = pl.BlockSpec(memory_space=pl.ANY)
        call = pl.pallas_call(
            hosted, name=name, grid=grid, in_specs=list(in_specs) + [any_spec] * n_x, out_specs=out_specs_l + [any_spec] * n_x,
            out_shape=out_shape_l + exch.out_shapes(), scratch_shapes=list(scratch_shapes) + exch.scratch(),
            compiler_params=compiler_params)
        return lambda *args: call(*args, *exch.operands())

    return make


def _adaln_fwd(cfg, cc_all, w_ada):
    d = cfg.D
    L, _, wa = w_ada.shape

    def body(c_ref, w_ref, o_ref):
        c = c_ref[...]
        a = (c * jax.nn.sigmoid(c)).astype(BF16)
        for l in range(L):
            m = _dot(a, w_ref[l].astype(BF16))
            for p in range(N_DEV):
                o_ref[p, l] = m[8 * p:8 * (p + 1)]

    return pl.pallas_call(
        body, name="adaln_fwd", grid=(1,),
        in_specs=[_full((8 * N_DEV, d)), _full((L, d, wa))],
        out_specs=_full((N_DEV, L, 8, wa)),
        out_shape=jax.ShapeDtypeStruct((N_DEV, L, 8, wa), F32), compiler_params=_params(1),
    )(cc_all, w_ada)


def _adaln_join(cfg, parts, b_ada):
    d = cfg.D
    _, L, _, wa = parts.shape

    def body(p_ref, b_ref, o_ref):
        for l in range(L):
            for j in range(N_DEV):
                o_ref[l, :, j * wa:(j + 1) * wa] = p_ref[j, l] + b_ref[l, :, j * wa:(j + 1) * wa]

    return pl.pallas_call(
        body, name="adaln_join", grid=(1,),
        in_specs=[_full((N_DEV, L, 8, wa)), _full((L, 1, 6 * d))],
        out_specs=_full((L, 8, 6 * d)),
        out_shape=jax.ShapeDtypeStruct((L, 8, 6 * d), F32), compiler_params=_params(1),
    )(parts, b_ada)


def _in_proj_fwd(cfg, l, x, gn, mod, w_in, exch=None):
    B, T, D, IN, tm, nC = cfg.B, cfg.T, cfg.D, cfg.IN, cfg.tm, cfg.nC

    def body(x_ref, gn_ref, mod_ref, w_ref, z_ref, h_ref):
        b, t = pl.program_id(0), pl.program_id(1)
        row = jnp.where(t < nC, B, b)
        h, _, _ = _modulate(x_ref[0], gn_ref[0], _mod_row(mod_ref, row, 0, D), _mod_row(mod_ref, row, 1, D))
        hb = h.astype(BF16)
        h_ref[0] = hb
        z_ref[0] = _dot(hb, w_ref[...])

    return _pcall(exch)(
        body, name="in_proj_fwd", grid=(B, cfg.nT),
        in_specs=[pl.BlockSpec((1, tm, D), lambda b, t: (b, t, 0)), _layer(l, D), _full((8, 6 * D)), _full((D, IN))],
        out_specs=[pl.BlockSpec((1, tm, IN), lambda b, t: (b, t, 0)), pl.BlockSpec((1, tm, D), lambda b, t: (b, t, 0))],
        out_shape=[jax.ShapeDtypeStruct((B, T, IN), F32), jax.ShapeDtypeStruct((B, T, D), BF16)],
        compiler_params=_params(2),
    )(x, gn, mod, w_in)


def _head_indicator():
    r = lax.broadcasted_iota(jnp.int32, (LANES, LANES), 0) // HEAD
    c = lax.broadcasted_iota(jnp.int32, (LANES, LANES), 1) // HEAD
    return jnp.where(r == c, 1.0, 0.0).astype(BF16)


def _head_sum(x, ind):
    hi = x.astype(BF16)
    lo = (x - hi.astype(F32)).astype(BF16)
    return _dot(hi, ind) + _dot(lo, ind)


def _pair_swap(y):
    lane = lax.broadcasted_iota(jnp.int32, y.shape, 1)
    return jnp.where(lane % 2 == 0, pltpu.roll(y, LANES - 1, 1), pltpu.roll(y, 1, 1))


_QK_CHUNKS = (0, 1, 2, 3, 5, 6, 7, 8)
_Q_CHUNKS = (0, 1, 2, 5, 6, 7)


def _qknorm_fwd(cfg, l, z, gvec, cosf, sins):
    B, T, tm = cfg.B, cfg.T, cfg.tm

    def body(z_ref, g_ref, cos_ref, sin_ref, qa_ref, ka_ref, qc_ref, kc_ref):
        ind = _head_indicator()
        cos, sin = cos_ref[...], sin_ref[...]

        def chunk(c):
            x = z_ref[0, :, c * LANES:(c + 1) * LANES]
            ss = _head_sum(x * x, ind)
            y = x * lax.rsqrt(ss * (1.0 / HEAD) + EPS) * g_ref[0, :, c * LANES:(c + 1) * LANES]
            out = y * cos + _pair_swap(y) * sin
            return (out * QSCALE if c in _Q_CHUNKS else out).astype(BF16)

        qa_ref[0] = jnp.concatenate([chunk(0), chunk(1), chunk(2)], axis=-1)
        ka_ref[0] = chunk(3)
        qc_ref[0] = jnp.concatenate([chunk(5), chunk(6), chunk(7)], axis=-1)
        kc_ref[0] = chunk(8)

    row = lambda w: pl.BlockSpec((1, tm, w), lambda b, t: (b, t, 0))
    tab = pl.BlockSpec((tm, LANES), lambda b, t: (t, 0))
    return pl.pallas_call(
        body, name="qknorm_fwd", grid=(B, cfg.nT),
        in_specs=[row(QKV_W), _layer(l, QKV_W), tab, tab],
        out_specs=[row(QW), row(KVW), row(QW), row(KVW)],
        out_shape=[jax.ShapeDtypeStruct((B, T, w), BF16) for w in (QW, KVW, QW, KVW)],
        compiler_params=_params(2),
    )(z, gvec, cosf, sins)


def _attn_scores(cfg, tl, q, k_ref, v_ref, sink_ref, h, loc, window, sink):
    S, N, tq = cfg.S, cfg.N, cfg.tm
    hs = slice(h * HEAD, (h + 1) * HEAD)
    qs = jnp.concatenate([q[:, (GROUP * h + g) * HEAD:(GROUP * h + g + 1) * HEAD] for g in range(GROUP)], axis=0)
    lo = None
    if not loc:
        kk = k_ref[0, 0:N, :][:, hs]
        vv = v_ref[0, 0:N, :].astype(BF16)[:, hs]
    elif not window:
        kk = k_ref[0][:, hs]
        vv = v_ref[0].astype(BF16)[:, hs]
    else:
        W = tq + 2 * WINDOW
        lo = pl.multiple_of(jnp.clip(tl * tq - WINDOW, 0, S - W), LANES)
        kk = jnp.concatenate([k_ref[0, 0:N, :], k_ref[0, pl.ds(N + lo, W), :]], axis=0)[:, hs]
        vv = jnp.concatenate([v_ref[0, 0:N, :], v_ref[0, pl.ds(N + lo, W), :]], axis=0).astype(BF16)[:, hs]
    st = _dot_nt(kk, qs)
    if window:
        krow = lax.broadcasted_iota(jnp.int32, st.shape, 0)
        qpos = tl * tq + lax.broadcasted_iota(jnp.int32, st.shape, 1) % tq
        st = jnp.where((krow < N) | (jnp.abs(qpos - (lo + krow - N)) <= WINDOW), st, NEG)
    m = jnp.max(st, axis=0, keepdims=True)
    if sink:
        colg = lax.broadcasted_iota(jnp.int32, (1, GROUP * tq), 1) // tq
        sk = jnp.zeros((1, GROUP * tq), F32)
        for g in range(GROUP):
            j = GROUP * h + g
            sk = jnp.where(colg == g, sink_ref[0, 0:1, j:j + 1], sk)
        m = jnp.maximum(m, sk)
    e = jnp.exp(st - m)
    l = jnp.sum(e, axis=0, keepdims=True)
    e_s = None
    if sink:
        e_s = jnp.exp(sk - m)
        l = l + e_s
    return qs, kk, vv, e, 1.0 / l, e_s, lo


def _attn_fwd(cfg, l, q, k, z, vblock, sink8, *, window, sink, ctx_attend, name, exch=None):
    B, T, tq, nC = cfg.B, cfg.T, cfg.tm, cfg.nC

    def body(q_ref, k_ref, v_ref, sink_ref, o_ref):
        t = pl.program_id(1)

        def run(loc):
            q_t = q_ref[0]
            outs = [None] * N_QHEADS
            for h in range(N_KV):
                _, _, vv, e, inv, _, _ = _attn_scores(cfg, t - nC, q_t, k_ref, v_ref, sink_ref, h, loc, window and loc, sink)
                o = (_dot_tn(vv, e.astype(BF16)) * inv).T
                for g in range(GROUP):
                    outs[GROUP * h + g] = o[g * tq:(g + 1) * tq]
            o_ref[0] = jnp.concatenate(outs, axis=-1).astype(BF16)

        pl.when(t >= nC)(functools.partial(run, True))
        if ctx_attend:
            pl.when(t < nC)(functools.partial(run, False))
        else:
            @pl.when(t < nC)
            def _():
                o_ref[0] = jnp.zeros((tq, QW), BF16)

    return _pcall(exch)(
        body, name=name, grid=(B, cfg.nT),
        in_specs=[pl.BlockSpec((1, tq, QW), lambda b, t: (b, t, 0)),
                  pl.BlockSpec((1, T, KVW), lambda b, t: (b, 0, 0)),
                  pl.BlockSpec((1, T, KVW), lambda b, t: (b, 0, vblock)),
                  pl.BlockSpec((1, 8, LANES), lambda b, t: (l, 0, 0))],
        out_specs=pl.BlockSpec((1, tq, QW), lambda b, t: (b, t, 0)),
        out_shape=jax.ShapeDtypeStruct((B, T, QW), BF16), compiler_params=_params(2),
    )(q, k, z, sink8)


def _pool_geometry(cfg, t):
    tm, N, T, nC = cfg.tm, cfg.N, cfg.T, cfg.nC
    r0 = pl.multiple_of(t * tm, tm)
    isctx = t < nC
    seg_lo = jnp.where(isctx, 0, N)
    seg_hi = jnp.where(isctx, N, T)
    k0 = pl.multiple_of(jnp.clip(t * tm - LANES, 0, T - cfg.kw), LANES)
    return r0, seg_lo, seg_hi, k0


def _pool_count(pos, h, seg_lo, seg_hi):
    return jnp.maximum(jnp.minimum(pos + h, seg_hi) - jnp.maximum(pos - h, seg_lo), 1).astype(F32)


def _split_bf16(x):
    hi = x.astype(BF16)
    return hi, (x - hi.astype(F32)).astype(BF16)


def _pool_fwd(cfg, l, z, wp, ps):
    B, T, tm, kw = cfg.B, cfg.T, cfg.tm, cfg.kw

    def body(u_ref, wp_ref, ps_ref, ob_ref, pooled_ref):
        t = pl.program_id(1)
        r0, seg_lo, seg_hi, k0 = _pool_geometry(cfg, t)
        hi, lo = _split_bf16(u_ref[0, pl.ds(k0, kw), :])
        rr = r0 + lax.broadcasted_iota(jnp.int32, (tm, kw), 0)
        cc = k0 + lax.broadcasted_iota(jnp.int32, (tm, kw), 1)
        diff = cc - rr
        inseg = (cc >= seg_lo) & (cc < seg_hi)
        rcol = r0 + lax.broadcasted_iota(jnp.int32, (tm, 1), 0)
        group = lax.broadcasted_iota(jnp.int32, (tm, POOL_W), 1) // HEAD
        acc = jnp.zeros((tm, POOL_W), F32)
        for gi, w in enumerate(POOL_WINDOWS):
            h = w // 2
            band = jnp.where((diff >= -h) & (diff <= h - 1) & inseg, 1.0, 0.0).astype(BF16)
            tot = _dot(band, hi) + _dot(band, lo)
            acc = jnp.where(group == gi, tot / _pool_count(rcol, h, seg_lo, seg_hi), acc)
        pooled = (acc - u_ref[0, pl.ds(r0, tm), :]).astype(BF16)
        pooled_ref[0] = pooled
        ob_ref[0] = (_dot(pooled, wp_ref[0]) * ps_ref[0]).astype(BF16)

    row = pl.BlockSpec((1, tm, POOL_W), lambda b, t: (b, t, 0))
    return pl.pallas_call(
        body, name="pool_fwd", grid=(B, cfg.nT),
        in_specs=[pl.BlockSpec((1, T, POOL_W), lambda b, t: (b, 0, QKV_W // POOL_W)),
                  pl.BlockSpec((1, POOL_W, POOL_W), lambda b, t: (l, 0, 0)), _layer(l, POOL_W)],
        out_specs=[row, row],
        out_shape=[jax.ShapeDtypeStruct((B, T, POOL_W), BF16)] * 2, compiler_params=_params(2),
    )(z, wp, ps)


def _gate_specs(cfg):
    tm, gw = cfg.tm, cfg.gw
    first = GATE0 // gw
    return [pl.BlockSpec((1, tm, gw), functools.partial(lambda b, t, j: (b, t, j), j=first + i)) for i in range(3 * cfg.D // gw)]


def _read_gates(cfg, gate_refs):
    per = cfg.D // cfg.gw
    return [jnp.concatenate([gate_refs[k * per + i][0] for i in range(per)], axis=-1) for k in range(3)]


def _merge_fwd(cfg, x, oa, ob, oc, z, mod, wa, wb, wc, wo, *, ctx_active, exch=None):
    B, T, D, tm, nC = cfg.B, cfg.T, cfg.D, cfg.tm, cfg.nC
    ng = 3 * D // cfg.gw

    def body(x_ref, oa_ref, ob_ref, oc_ref, *rest):
        gate_refs = rest[:ng]
        mod_ref, wa_ref, wb_ref, wc_ref, wo_ref, x1_ref, mgo_ref = rest[ng:]
        b, t = pl.program_id(0), pl.program_id(1)

        def compute():
            row = jnp.where(t < nC, B, b)
            ga, gb, gc = _read_gates(cfg, gate_refs)
            y = (jax.nn.sigmoid(ga) * _dot(oa_ref[0], wa_ref[...])
                 + jax.nn.sigmoid(gb) * _dot(ob_ref[0], wb_ref[...])
                 + jax.nn.sigmoid(gc) * _dot(oc_ref[0], wc_ref[...]))
            mo = _dot(y.astype(BF16), wo_ref[...])
            mgo_ref[0] = mo.astype(BF16)
            x1_ref[0] = x_ref[0] + _mod_row(mod_ref, row, 2, D) * mo

        if ctx_active:
            compute()
        else:
            pl.when(t >= nC)(compute)

            @pl.when(t < nC)
            def _():
                mgo_ref[0] = jnp.zeros((tm, D), BF16)
                x1_ref[0] = x_ref[0]

    row = lambda w: pl.BlockSpec((1, tm, w), lambda b, t: (b, t, 0))
    return _pcall(exch)(
        body, name="merge_fwd", grid=(B, cfg.nT),
        in_specs=[row(D), row(QW), row(POOL_W), row(QW)] + _gate_specs(cfg)
        + [_full((8, 6 * D)), _full((QW, D)), _full((POOL_W, D)), _full((QW, D)), _full((D, D))],
        out_specs=[row(D), row(D)],
        out_shape=[jax.ShapeDtypeStruct((B, T, D), F32), jax.ShapeDtypeStruct((B, T, D), BF16)],
        compiler_params=_params(2),
    )(x, oa, ob, oc, *([z] * ng), mod, wa, wb, wc, wo)


def _w1_apply(hb, w1_ref):
    return jnp.concatenate([_dot(hb, w1_ref[d]) for d in range(N_DEV)], axis=-1)


def _mlp_fwd(cfg, l, x1, gn, mod, w1, w2, *, ctx_active, target=None, exch=None):
    B, T, D, F, tm, nC = cfg.B, cfg.T, cfg.D, cfg.F, cfg.tm, cfg.nC
    assert target is None or not ctx_active

    def body(x_ref, gn_ref, mod_ref, w1_ref, w2_ref, *rest):
        if target is None:
            x2_ref, mo_ref = rest
        else:
            tgt_ref, x2_ref, mo_ref, sse_ref = rest
            _acc_init([sse_ref])
        b, t = pl.program_id(0), pl.program_id(1)

        def compute():
            row = jnp.where(t < nC, B, b)
            x = x_ref[0]
            h, _, _ = _modulate(x, gn_ref[0], _mod_row(mod_ref, row, 3, D), _mod_row(mod_ref, row, 4, D))
            a = jnp.maximum(_w1_apply(h.astype(BF16), w1_ref), 0.0)
            mo = _dot((a * a).astype(BF16), w2_ref[...])
            mo_ref[0] = mo.astype(BF16)
            x2 = x + _mod_row(mod_ref, row, 5, D) * mo
            if target is None:
                x2_ref[0] = x2
            else:
                err = x2 - tgt_ref[0]
                x2_ref[0] = err * (1.0 / D)
                sse_ref[...] += jnp.sum(err * err)

        if ctx_active:
            compute()
        else:
            pl.when(t >= nC)(compute)

            @pl.when(t < nC)
            def _():
                mo_ref[0] = jnp.zeros((tm, D), BF16)
                x2_ref[0] = x_ref[0] if target is None else jnp.zeros((tm, D), F32)

    row = pl.BlockSpec((1, tm, D), lambda b, t: (b, t, 0))
    in_specs = [row, _layer(l, D), _full((8, 6 * D)), _full((N_DEV, D, F // N_DEV)), _full((F, D))]
    out_specs = [row, row]
    out_shape = [jax.ShapeDtypeStruct((B, T, D), F32), jax.ShapeDtypeStruct((B, T, D), BF16)]
    args = [x1, gn, mod, w1, w2]
    if target is not None:
        in_specs.append(pl.BlockSpec((1, tm, D), lambda b, t: (b, jnp.maximum(t - nC, 0), 0)))
        out_specs.append(_full((8, LANES)))
        out_shape.append(jax.ShapeDtypeStruct((8, LANES), F32))
        args.append(target)
    return _pcall(exch)(
        body, name="mlp_fwd", grid=(B, cfg.nT), in_specs=in_specs, out_specs=out_specs, out_shape=out_shape,
        compiler_params=_params(2),
    )(*args)


def _acc_init(refs):
    b, t = pl.program_id(0), pl.program_id(1)

    @pl.when((b == 0) & (t == 0))
    def _():
        for ref in refs:
            ref[...] = jnp.zeros(ref.shape, ref.dtype)


def _mlp_bwd(cfg, l, x1, dx2, mo, gn, mod, w1, w2, *, ctx_active, exch=None):
    B, T, D, F, tm, nC = cfg.B, cfg.T, cfg.D, cfg.F, cfg.tm, cfg.nC
    ws = F // N_DEV

    def body(x_ref, dx_ref, mo_ref, gn_ref, mod_ref, w1_ref, w2_ref, dx1_ref, h_ref, r_ref, da_ref, dout_ref, dmod_ref, dgn_ref):
        b, t = pl.program_id(0), pl.program_id(1)
        _acc_init([dmod_ref, dgn_ref])

        def compute():
            row = jnp.where(t < nC, B, b)
            gn = gn_ref[0]
            scale = _mod_row(mod_ref, row, 4, D)
            h, xhat, rstd = _modulate(x_ref[0], gn, _mod_row(mod_ref, row, 3, D), scale)
            hb = h.astype(BF16)
            a = jnp.maximum(_w1_apply(hb, w1_ref), 0.0)
            dx = dx_ref[0]
            dout = (dx * _mod_row(mod_ref, row, 5, D)).astype(BF16)
            da = (_dot_nt(dout, w2_ref[...]) * (2.0 * a)).astype(BF16)
            dh = _dot_nt(da[:, 0:ws], w1_ref[0])
            for d in range(1, N_DEV):
                dh = dh + _dot_nt(da[:, d * ws:(d + 1) * ws], w1_ref[d])
            dxn, d_shift, d_scale, d_gn = _modulate_bwd(dh, xhat, rstd, gn, scale)
            dx1_ref[0] = dx + dxn
            h_ref[0] = hb
            r_ref[0] = (a * a).astype(BF16)
            da_ref[0] = da
            dout_ref[0] = dout
            d_gate = jnp.sum(dx * mo_ref[0].astype(F32), axis=0, keepdims=True)
            dmod_ref[pl.ds(row, 1), :] += jnp.concatenate([d_shift, d_scale, d_gate], axis=-1)
            dgn_ref[0:1, :] += d_gn

        if ctx_active:
            compute()
        else:
            pl.when(t >= nC)(compute)

            @pl.when(t < nC)
            def _():
                dx1_ref[0] = dx_ref[0]
                h_ref[0] = jnp.zeros((tm, D), BF16)
                r_ref[0] = jnp.zeros((tm, F), BF16)
                da_ref[0] = jnp.zeros((tm, F), BF16)
                dout_ref[0] = jnp.zeros((tm, D), BF16)

    row = lambda w: pl.BlockSpec((1, tm, w), lambda b, t: (b, t, 0))
    sds = lambda w, dt: jax.ShapeDtypeStruct((B, T, w), dt)
    return _pcall(exch)(
        body, name="mlp_bwd", grid=(B, cfg.nT),
        in_specs=[row(D), row(D), row(D), _layer(l, D), _full((8, 6 * D)), _full((N_DEV, D, ws)), _full((F, D))],
        out_specs=[row(D), row(D), row(F), row(F), row(D), _full((8, 3 * D)), _full((8, D))],
        out_shape=[sds(D, F32), sds(D, BF16), sds(F, BF16), sds(F, BF16), sds(D, BF16),
                   jax.ShapeDtypeStruct((8, 3 * D), F32), jax.ShapeDtypeStruct((8, D), F32)],
        compiler_params=_params(2),
    )(x1, dx2, mo, gn, mod, w1, w2)


def _matmul_tn(a, g, name, *, by_shard):
    R, Ka = a.shape
    Ng = g.shape[1]
    tr = next(c for c in (512, 256, 128, 64, 32, 16, 8) if R % c == 0)
    tka = Ka if Ka <= 1024 else 1024
    if by_shard:
        ws = Ng // N_DEV
        per = next(c for c in (8, 4, 2, 1) if c * ws <= 1152 or c == 1)
        tn = per * ws
    else:
        tn = next(c for c in (1152, 1024, 768, 512, 384, 256, 128) if Ng % c == 0)
    assert Ka % tka == 0 and tn % LANES == 0
    nr = R // tr

    def body(a_ref, g_ref, o_ref, acc_ref):
        r = pl.program_id(2)

        @pl.when(r == 0)
        def _():
            acc_ref[...] = jnp.zeros(acc_ref.shape, F32)

        acc_ref[...] += _dot_tn(a_ref[...], g_ref[...])

        @pl.when(r == nr - 1)
        def _():
            if by_shard:
                for d in range(per):
                    o_ref[d] = acc_ref[:, d * ws:(d + 1) * ws].astype(BF16)
            else:
                o_ref[...] = acc_ref[...].astype(BF16)

    if by_shard:
        out_spec = pl.BlockSpec((per, tka, ws), lambda i, j, r: (j, i, 0))
        out_shape = jax.ShapeDtypeStruct((N_DEV, Ka, ws), BF16)
    else:
        out_spec = pl.BlockSpec((tka, tn), lambda i, j, r: (i, j))
        out_shape = jax.ShapeDtypeStruct((Ka, Ng), BF16)
    return pl.pallas_call(
        body, name=name, grid=(Ka // tka, Ng // tn, nr),
        in_specs=[pl.BlockSpec((tr, tka), lambda i, j, r: (r, i)), pl.BlockSpec((tr, tn), lambda i, j, r: (r, j))],
        out_specs=out_spec, out_shape=out_shape, scratch_shapes=[pltpu.VMEM((tka, tn), F32)], compiler_params=_params(3),
    )(a, g)


def _merge_bwd(cfg, dx1, mgo, oa, ob, oc, z, mod, wa, wb, wc, wo, *, ctx_active, exch=None):
    B, T, D, tm, nC = cfg.B, cfg.T, cfg.D, cfg.tm, cfg.nC
    ng = 3 * D // cfg.gw

    def body(dx_ref, mgo_ref, oa_ref, ob_ref, oc_ref, *rest):
        gate_refs = rest[:ng]
        (mod_ref, wa_ref, wb_ref, wc_ref, wo_ref,
         doa_ref, dob_ref, doc_ref, dpa_ref, dpb_ref, dpc_ref, y_ref, dmo_ref, dzg_ref, dg1_ref) = rest[ng:]
        b, t = pl.program_id(0), pl.program_id(1)
        _acc_init([dg1_ref])

        def compute():
            row = jnp.where(t < nC, B, b)
            dx = dx_ref[0]
            dg1_ref[pl.ds(row, 1), :] += jnp.sum(dx * mgo_ref[0].astype(F32), axis=0, keepdims=True)
            dmo = (dx * _mod_row(mod_ref, row, 2, D)).astype(BF16)
            dmo_ref[0] = dmo
            dy = _dot_nt(dmo, wo_ref[...])
            gates = _read_gates(cfg, gate_refs)
            y = jnp.zeros((tm, D), F32)
            dgs = []
            for gate, o_ref, w_ref, do_ref, dp_ref in ((gates[0], oa_ref, wa_ref, doa_ref, dpa_ref),
                                                      (gates[1], ob_ref, wb_ref, dob_ref, dpb_ref),
                                                      (gates[2], oc_ref, wc_ref, doc_ref, dpc_ref)):
                s = jax.nn.sigmoid(gate)
                p = _dot(o_ref[0], w_ref[...])
                y = y + s * p
                dp = (dy * s).astype(BF16)
                dp_ref[0] = dp
                do_ref[0] = _dot_nt(dp, w_ref[...]).astype(BF16)
                dgs.append((dy * p * s * (1.0 - s)).astype(BF16))
            y_ref[0] = y.astype(BF16)
            dzg_ref[0] = jnp.concatenate(dgs, axis=-1)

        if ctx_active:
            compute()
        else:
            pl.when(t >= nC)(compute)

            @pl.when(t < nC)
            def _():
                for ref in (doa_ref, dob_ref, doc_ref, dpa_ref, dpb_ref, dpc_ref, y_ref, dmo_ref, dzg_ref):
                    ref[...] = jnp.zeros(ref.shape, ref.dtype)

    row = lambda w: pl.BlockSpec((1, tm, w), lambda b, t: (b, t, 0))
    sds = lambda w: jax.ShapeDtypeStruct((B, T, w), BF16)
    return _pcall(exch)(
        body, name="merge_bwd", grid=(B, cfg.nT),
        in_specs=[row(D), row(D), row(QW), row(POOL_W), row(QW)] + _gate_specs(cfg)
        + [_full((8, 6 * D)), _full((QW, D)), _full((POOL_W, D)), _full((QW, D)), _full((D, D))],
        out_specs=[row(QW), row(POOL_W), row(QW), row(D), row(D), row(D), row(D), row(D), row(3 * D), _full((8, D))],
        out_shape=[sds(QW), sds(POOL_W), sds(QW), sds(D), sds(D), sds(D), sds(D), sds(D), sds(3 * D),
                   jax.ShapeDtypeStruct((8, D), F32)],
        compiler_params=_params(2),
    )(dx1, mgo, oa, ob, oc, *([z] * ng), mod, wa, wb, wc, wo)


def _attn_bwd(cfg, l, q, k, z, vblock, sink8, do, *, window, sink, ctx_attend, name, exch=None):
    B, S, N, T, tq, nC = cfg.B, cfg.S, cfg.N, cfg.T, cfg.tm, cfg.nC

    def body(q_ref, k_ref, v_ref, sink_ref, do_ref, dq_ref, dk_ref, dv_ref, dsink_ref):
        b, t = pl.program_id(0), pl.program_id(1)
        _acc_init([dsink_ref])

        @pl.when(t == 0)
        def _():
            dk_ref[...] = jnp.zeros(dk_ref.shape, F32)
            dv_ref[...] = jnp.zeros(dv_ref.shape, F32)

        def run(loc):
            q_t = q_ref[0]
            do_t = do_ref[0]
            dqs = [None] * N_QHEADS
            dks, dvs = [], []
            dsink_row = jnp.zeros((1, LANES), F32)
            lane = lax.broadcasted_iota(jnp.int32, (1, LANES), 1)
            lo = None
            for h in range(N_KV):
                qs, kk, vv, e, inv, e_s, lo = _attn_scores(cfg, t - nC, q_t, k_ref, v_ref, sink_ref, h, loc, window and loc, sink)
                dos = jnp.concatenate([do_t[:, (GROUP * h + g) * HEAD:(GROUP * h + g + 1) * HEAD] for g in range(GROUP)], axis=0)
                p = e * inv
                dp = _dot_nt(vv, dos)
                delta = jnp.sum(p * dp, axis=0, keepdims=True)
                ds = (p * (dp - delta)).astype(BF16)
                dq = _dot_tn(kk, ds).T
                dks.append(_dot(ds, qs))
                dvs.append(_dot(p.astype(BF16), dos))
                if sink:
                    dsk = -(e_s * inv) * delta
                    for g in range(GROUP):
                        tot = jnp.sum(dsk[:, g * tq:(g + 1) * tq], axis=1, keepdims=True)
                        dsink_row = dsink_row + jnp.where(lane == GROUP * h + g, tot, 0.0)
                for g in range(GROUP):
                    dqs[GROUP * h + g] = dq[g * tq:(g + 1) * tq] * QSCALE
            dq_ref[0] = jnp.concatenate(dqs, axis=-1)
            dk = jnp.concatenate(dks, axis=-1)
            dv = jnp.concatenate(dvs, axis=-1)
            if loc and not window:
                dk_ref[0] += dk
                dv_ref[0] += dv
            else:
                dk_ref[0, 0:N, :] += dk[0:N]
                dv_ref[0, 0:N, :] += dv[0:N]
                if loc:
                    W = tq + 2 * WINDOW
                    dk_ref[0, pl.ds(N + lo, W), :] += dk[N:]
                    dv_ref[0, pl.ds(N + lo, W), :] += dv[N:]
            if sink:
                dsink_ref[0:1, :] += dsink_row

        pl.when(t >= nC)(functools.partial(run, True))
        if ctx_attend:
            pl.when(t < nC)(functools.partial(run, False))
        else:
            @pl.when(t < nC)
            def _():
                dq_ref[0] = jnp.zeros((tq, QW), F32)

    kv = pl.BlockSpec((1, T, KVW), lambda b, t: (b, 0, 0))
    qrow = pl.BlockSpec((1, tq, QW), lambda b, t: (b, t, 0))
    return _pcall(exch)(
        body, name=name, grid=(B, cfg.nT),
        in_specs=[qrow, kv, pl.BlockSpec((1, T, KVW), lambda b, t: (b, 0, vblock)),
                  pl.BlockSpec((1, 8, LANES), lambda b, t: (l, 0, 0)), qrow],
        out_specs=[qrow, kv, kv, _full((8, LANES))],
        out_shape=[jax.ShapeDtypeStruct((B, T, QW), F32), jax.ShapeDtypeStruct((B, T, KVW), F32),
                   jax.ShapeDtypeStruct((B, T, KVW), F32), jax.ShapeDtypeStruct((8, LANES), F32)],
        compiler_params=_params(2),
    )(q, k, z, sink8, do)


def _qknorm_bwd(cfg, l, z, gvec, cosf, sins, dqa, dka, dva, dqc, dkc, dvc):
    B, T, tm = cfg.B, cfg.T, cfg.tm

    def body(z_ref, g_ref, cos_ref, sin_ref, dqa_ref, dka_ref, dva_ref, dqc_ref, dkc_ref, dvc_ref, dz_ref, dg_ref):
        _acc_init([dg_ref])
        ind = _head_indicator()
        cos, sin = cos_ref[...], sin_ref[...]
        dqa_t, dqc_t = dqa_ref[0], dqc_ref[0]
        douts = {0: dqa_t[:, 0:128], 1: dqa_t[:, 128:256], 2: dqa_t[:, 256:384], 3: dka_ref[0],
                 5: dqc_t[:, 0:128], 6: dqc_t[:, 128:256], 7: dqc_t[:, 256:384], 8: dkc_ref[0]}
        pieces = []
        dgs = []
        for c in range(QKV_W // LANES):
            if c not in douts:
                pieces.append(dva_ref[0] if c == 4 else dvc_ref[0])
                dgs.append(jnp.zeros((1, LANES), F32))
                continue
            x = z_ref[0, :, c * LANES:(c + 1) * LANES]
            g = g_ref[0, :, c * LANES:(c + 1) * LANES]
            ss = _head_sum(x * x, ind)
            rstd = lax.rsqrt(ss * (1.0 / HEAD) + EPS)
            n = x * rstd
            dout = douts[c]
            dy = dout * cos + _pair_swap(dout * sin)
            dgs.append(jnp.sum(dy * n, axis=0, keepdims=True))
            dn = dy * g
            mean = _head_sum(dn * n, ind) * (1.0 / HEAD)
            pieces.append(rstd * (dn - n * mean))
        dz_ref[0] = jnp.concatenate(pieces, axis=-1).astype(BF16)
        dg_ref[0:1, :] += jnp.concatenate(dgs, axis=-1)

    row = lambda w: pl.BlockSpec((1, tm, w), lambda b, t: (b, t, 0))
    tab = pl.BlockSpec((tm, LANES), lambda b, t: (t, 0))
    return pl.pallas_call(
        body, name="qknorm_bwd", grid=(B, cfg.nT),
        in_specs=[row(QKV_W), _layer(l, QKV_W), tab, tab, row(QW), row(KVW), row(KVW), row(QW), row(KVW), row(KVW)],
        out_specs=[row(QKV_W), _full((8, QKV_W))],
        out_shape=[jax.ShapeDtypeStruct((B, T, QKV_W), BF16), jax.ShapeDtypeStruct((8, QKV_W), F32)],
        compiler_params=_params(2),
    )(z, gvec, cosf, sins, dqa, dka, dva, dqc, dkc, dvc)


def _pool_bwd(cfg, l, dob, pooled, wp, ps):
    B, T, tm, kw = cfg.B, cfg.T, cfg.tm, cfg.kw

    def body(dob_ref, pooled_ref, wp_ref, ps_ref, du_ref, dwp_ref, dps_ref):
        t = pl.program_id(1)
        _acc_init([dwp_ref, dps_ref])
        r0, seg_lo, seg_hi, k0 = _pool_geometry(cfg, t)
        ps = ps_ref[0]
        wp = wp_ref[0]
        dmix = dob_ref[0, pl.ds(r0, tm), :].astype(F32)
        pooled = pooled_ref[0]
        dps_ref[0:1, :] += jnp.sum(dmix * _dot(pooled, wp), axis=0, keepdims=True)
        dpm = (dmix * ps).astype(BF16)
        dwp_ref[...] += _dot_tn(pooled, dpm)
        dpooled_t = _dot_nt(dpm, wp)
        dpm_w = (dob_ref[0, pl.ds(k0, kw), :].astype(F32) * ps).astype(BF16)
        dpooled_w = _dot_nt(dpm_w, wp)
        rr = r0 + lax.broadcasted_iota(jnp.int32, (tm, kw), 0)
        cc = k0 + lax.broadcasted_iota(jnp.int32, (tm, kw), 1)
        diff = rr - cc
        inseg = (cc >= seg_lo) & (cc < seg_hi)
        ccol = k0 + lax.broadcasted_iota(jnp.int32, (kw, 1), 0)
        group = lax.broadcasted_iota(jnp.int32, (tm, POOL_W), 1) // HEAD
        acc = jnp.zeros((tm, POOL_W), F32)
        for gi, w in enumerate(POOL_WINDOWS):
            h = w // 2
            band_t = jnp.where((diff >= -h) & (diff <= h - 1) & inseg, 1.0, 0.0).astype(BF16)
            hi, lo = _split_bf16(dpooled_w / _pool_count(ccol, h, seg_lo, seg_hi))
            acc = jnp.where(group == gi, _dot(band_t, hi) + _dot(band_t, lo), acc)
        du_ref[0] = (acc - dpooled_t).astype(BF16)

    row = pl.BlockSpec((1, tm, POOL_W), lambda b, t: (b, t, 0))
    return pl.pallas_call(
        body, name="pool_bwd", grid=(B, cfg.nT),
        in_specs=[pl.BlockSpec((1, T, POOL_W), lambda b, t: (b, 0, 0)), row,
                  pl.BlockSpec((1, POOL_W, POOL_W), lambda b, t: (l, 0, 0)), _layer(l, POOL_W)],
        out_specs=[row, _full((POOL_W, POOL_W)), _full((8, POOL_W))],
        out_shape=[jax.ShapeDtypeStruct((B, T, POOL_W), BF16), jax.ShapeDtypeStruct((POOL_W, POOL_W), F32),
                   jax.ShapeDtypeStruct((8, POOL_W), F32)],
        compiler_params=_params(2),
    )(dob, pooled, wp, ps)


def _in_proj_bwd(cfg, l, dzq, du, dzg, w_in, x, dx1, gn, mod, *, latent_only, exch=None):
    B, S, T, D, IN, tm, nC = cfg.B, cfg.S, cfg.T, cfg.D, cfg.IN, cfg.tm, cfg.nC

    def body(dzq_ref, du_ref, dzg_ref, w_ref, x_ref, dx1_ref, gn_ref, mod_ref, dx0_ref, dz_ref, dmod_ref, dgn_ref):
        b, t = pl.program_id(0), pl.program_id(1)
        _acc_init([dmod_ref, dgn_ref])
        row = jnp.where(t < nC, B, b)
        dz = jnp.concatenate([dzq_ref[0], du_ref[0], dzg_ref[0]], axis=-1)
        dz_ref[0] = dz
        dh = _dot_nt(dz, w_ref[...])
        gn = gn_ref[0]
        scale = _mod_row(mod_ref, row, 1, D)
        _, xhat, rstd = _modulate(x_ref[0], gn, _mod_row(mod_ref, row, 0, D), scale)
        dxn, d_shift, d_scale, d_gn = _modulate_bwd(dh, xhat, rstd, gn, scale)
        dx0_ref[0] = dx1_ref[0] + dxn
        dmod_ref[pl.ds(row, 1), :] += jnp.concatenate([d_shift, d_scale], axis=-1)
        dgn_ref[0:1, :] += d_gn

    row = lambda w: pl.BlockSpec((1, tm, w), lambda b, t: (b, t, 0))
    if latent_only:
        dx0_spec = pl.BlockSpec((1, tm, D), lambda b, t: (b, jnp.maximum(t - nC, 0), 0))
        dx0_shape = jax.ShapeDtypeStruct((B, S, D), F32)
    else:
        dx0_spec, dx0_shape = row(D), jax.ShapeDtypeStruct((B, T, D), F32)
    return _pcall(exch)(
        body, name="in_proj_bwd", grid=(B, cfg.nT),
        in_specs=[row(QKV_W), row(POOL_W), row(3 * D), _full((D, IN)), row(D), row(D), _layer(l, D), _full((8, 6 * D))],
        out_specs=[dx0_spec, row(IN), _full((8, 2 * D)), _full((8, D))],
        out_shape=[dx0_shape, jax.ShapeDtypeStruct((B, T, IN), BF16),
                   jax.ShapeDtypeStruct((8, 2 * D), F32), jax.ShapeDtypeStruct((8, D), F32)],
        compiler_params=_params(2),
    )(dzq, du, dzg, w_in, x, dx1, gn, mod)


def _adaln_bwd(cfg, l, cc_all, dm_all, w_ada):
    d, B = cfg.D, cfg.B
    wa = w_ada.shape[2]

    def body(c_ref, dm_ref, w_ref, dw_ref, dc_ref):
        c = c_ref[...]
        s = jax.nn.sigmoid(c)
        dmb = dm_ref[...].astype(BF16)
        dw_ref[...] = _dot_tn((c * s).astype(BF16), dmb)
        dc = _dot_nt(dmb, w_ref[0].astype(BF16)) * (s * (1.0 + c * (1.0 - s)))
        is_ctx = lax.broadcasted_iota(jnp.int32, (8 * N_DEV, 1), 0) % 8 == B
        dc_ref[...] = jnp.broadcast_to(jnp.sum(jnp.where(is_ctx, dc, 0.0), axis=0, keepdims=True), (8, d))

    return pl.pallas_call(
        body, name="adaln_bwd", grid=(1,),
        in_specs=[_full((8 * N_DEV, d)), _full((8 * N_DEV, wa)), pl.BlockSpec((1, d, wa), lambda *_: (l, 0, 0))],
        out_specs=[_full((d, wa)), _full((8, d))],
        out_shape=[jax.ShapeDtypeStruct((d, wa), F32), jax.ShapeDtypeStruct((8, d), F32)],
        compiler_params=_params(1),
    )(cc_all, dm_all, w_ada)


def _dmod_pack(cfg, dmod_in, dg1, dmod_mlp):
    d = cfg.D
    wa = 6 * d // N_DEV

    def body(din_ref, dg1_ref, dmlp_ref, o_ref, db_ref):
        dm = jnp.concatenate([din_ref[...], dg1_ref[...], dmlp_ref[...]], axis=-1)
        for j in range(N_DEV):
            o_ref[j] = dm[:, j * wa:(j + 1) * wa]
        db_ref[...] = jnp.broadcast_to(jnp.sum(dm, axis=0, keepdims=True), (8, 6 * d))

    return pl.pallas_call(
        body, name="dmod_pack", grid=(1,),
        in_specs=[_full((8, 2 * d)), _full((8, d)), _full((8, 3 * d))],
        out_specs=[_full((N_DEV, 8, wa)), _full((8, 6 * d))],
        out_shape=[jax.ShapeDtypeStruct((N_DEV, 8, wa), F32), jax.ShapeDtypeStruct((8, 6 * d), F32)],
        compiler_params=_params(1),
    )(dmod_in, dg1, dmod_mlp)


def _adam_update(g, w, m, v):
    bc1 = 1.0 - ADAM_B1 ** ADAM_STEP
    bc2 = 1.0 - ADAM_B2 ** ADAM_STEP
    m2 = ADAM_B1 * m + (1.0 - ADAM_B1) * g
    v2 = ADAM_B2 * v + (1.0 - ADAM_B2) * (g * g)
    delta = -ADAM_LR * ((m2 / bc1) / (jnp.sqrt(v2 / bc2) + ADAM_EPS) + ADAM_WD * w)
    return delta, m2, v2


def _sum_parts(p_ref):
    g = p_ref[0].astype(F32)
    for d in range(1, p_ref.shape[0]):
        g = g + p_ref[d].astype(F32)
    return g


def _adamw_sharded(parts, w, m, v, name):
    L, K, W = w.shape
    P = parts[0].shape[0]
    tk = next(c for c in (256, 128, 64, 32, 16, 8) if K % c == 0)

    def body(*refs):
        p_refs = refs[:L]
        w_ref, m_ref, v_ref, g_ref, d_ref, m2_ref, v2_ref = refs[L:]
        layer = pl.program_id(0)

        def run(p_ref):
            g = _sum_parts(p_ref)
            delta, m2, v2 = _adam_update(g, w_ref[0], m_ref[0], v_ref[0])
            g_ref[0] = g
            d_ref[0] = delta
            m2_ref[0] = m2
            v2_ref[0] = v2

        for li in range(L):
            pl.when(layer == li)(functools.partial(run, p_refs[li]))

    blk = pl.BlockSpec((1, tk, W), lambda l, i: (l, i, 0))
    part_spec = lambda li: pl.BlockSpec((P, tk, W), lambda l, i: (0, jnp.where(l == li, i, 0), 0))
    return pl.pallas_call(
        body, name=name, grid=(L, K // tk),
        in_specs=[part_spec(li) for li in range(L)] + [blk, blk, blk],
        out_specs=[blk] * 4, out_shape=[jax.ShapeDtypeStruct((L, K, W), F32)] * 4,
        compiler_params=_params(2),
    )(*parts, w, m, v)


def _adamw_packed(parts, w, m, v, name):
    rows = w.shape[0]
    tr = next(c for c in (256, 128, 64, 32, 16, 8) if rows % c == 0)

    def body(p_ref, w_ref, m_ref, v_ref, g_ref, d_ref, m2_ref, v2_ref):
        g = _sum_parts(p_ref)
        delta, m2, v2 = _adam_update(g, w_ref[...], m_ref[...], v_ref[...])
        g_ref[...] = g
        d_ref[...] = delta
        m2_ref[...] = m2
        v2_ref[...] = v2

    blk = pl.BlockSpec((tr, PACK_W), lambda i: (i, 0))
    return pl.pallas_call(
        body, name=name, grid=(rows // tr,),
        in_specs=[pl.BlockSpec((N_DEV, tr, PACK_W), lambda i: (0, i, 0)), blk, blk, blk],
        out_specs=[blk] * 4, out_shape=[jax.ShapeDtypeStruct((rows, PACK_W), F32)] * 4,
        compiler_params=_params(1),
    )(parts, w, m, v)


_SHARDED = dict(w_ada=True, w_in=True, w_br_a=True, w_br_b=True, w_br_c=True, w_out=False, w_mlp1=True, w_mlp2=False)
_MERGE_WEIGHTS = ("w_br_a", "w_br_b", "w_br_c", "w_out")
_GATHERED = ("w_in",) + _MERGE_WEIGHTS + ("w_mlp1", "w_mlp2")
_KEEP_SHARDS = ("w_mlp1",)
_SMALL = ("c_ctx", "b_ada", "norm1", "norm2", "q_norm_a", "k_norm_a", "q_norm_c", "k_norm_c", "sink_c", "w_pool", "pool_scale")


def _from_shards(name, g):
    n, k, w = g.shape
    if name in _KEEP_SHARDS:
        return g
    if _SHARDED[name]:
        return g.transpose(1, 0, 2).reshape(k, n * w)
    return g.reshape(n * k, w)


def _to_shards(name, g):
    if g.ndim == 3:
        return g
    if _SHARDED[name]:
        k, nw = g.shape
        return g.reshape(k, N_DEV, nw // N_DEV).transpose(1, 0, 2)
    nk, w = g.shape
    return g.reshape(N_DEV, nk // N_DEV, w)


def _pack_small(vals):
    flat = jnp.concatenate([vals[n].reshape(-1) for n in _SMALL])
    rows = -(-flat.shape[0] // (8 * PACK_W)) * 8
    return jnp.pad(flat, (0, rows * PACK_W - flat.shape[0])).reshape(rows, PACK_W)


def _unpack_small(packed, like):
    flat, out, r = packed.reshape(-1), {}, 0
    for n in _SMALL:
        sz = like[n].size
        out[n] = flat[r:r + sz].reshape(like[n].shape)
        r += sz
    return out


def _rope_tables(cfg):
    pos = jnp.arange(cfg.S, dtype=F32)
    r = jnp.floor(pos / GRID_W)
    col = pos - r * GRID_W
    inv = 1.0 / (ROPE_THETA ** (jnp.arange(0, HEAD // 2, 2, dtype=F32) / (HEAD // 2)))
    ang = jnp.concatenate([r[:, None] * inv, col[:, None] * inv], axis=-1)
    cos = jnp.repeat(jnp.cos(ang), 2, axis=-1)
    sin = jnp.repeat(jnp.sin(ang), 2, axis=-1) * jnp.tile(jnp.array([-1.0, 1.0], F32), HEAD // 2)
    cos = jnp.concatenate([jnp.ones((cfg.N, HEAD), F32), cos], axis=0)
    sin = jnp.concatenate([jnp.zeros((cfg.N, HEAD), F32), sin], axis=0)
    return jnp.tile(cos, (1, 2)), jnp.tile(sin, (1, 2))


def _gvec(qa, ka, qc, kc):
    one = jnp.ones((qa.shape[0], KVW), F32)
    t = lambda a, n: jnp.tile(a, (1, n))
    return jnp.concatenate([t(qa, N_QHEADS), t(ka, N_KV), one, t(qc, N_QHEADS), t(kc, N_KV), one], axis=-1)[:, None, :]


def _block_diag(wp):
    L, g, c, _ = wp.shape
    eye = jnp.eye(g, dtype=wp.dtype)
    return (wp[:, :, :, None, :] * eye[None, :, None, :, None]).reshape(L, g * c, g * c)


def _pad8(a):
    return jnp.pad(a, ((0, 8 - a.shape[0]), (0, 0)))


def kernel(x, c, ctx, c_ctx, w_ada, b_ada, norm1, norm2, w_in, q_norm_a, k_norm_a, q_norm_c, k_norm_c, sink_c, w_pool, pool_scale, w_br_a, w_br_b, w_br_c, w_out, w_mlp1, w_mlp2, loss_target, m_c_ctx, m_w_ada, m_b_ada, m_norm1, m_norm2, m_w_in, m_q_norm_a, m_k_norm_a, m_q_norm_c, m_k_norm_c, m_sink_c, m_w_pool, m_pool_scale, m_w_br_a, m_w_br_b, m_w_br_c, m_w_out, m_w_mlp1, m_w_mlp2, v_c_ctx, v_w_ada, v_b_ada, v_norm1, v_norm2, v_w_in, v_q_norm_a, v_k_norm_a, v_q_norm_c, v_k_norm_c, v_sink_c, v_w_pool, v_pool_scale, v_w_br_a, v_w_br_b, v_w_br_c, v_w_out, v_w_mlp1, v_w_mlp2):
    B, S, D = x.shape
    N = ctx.shape[1]
    L = w_ada.shape[0]
    cfg = _Cfg(B, S, N, D)
    T = cfg.T
    weights = dict(c_ctx=c_ctx, w_ada=w_ada, b_ada=b_ada, norm1=norm1, norm2=norm2, w_in=w_in, q_norm_a=q_norm_a,
                   k_norm_a=k_norm_a, q_norm_c=q_norm_c, k_norm_c=k_norm_c, sink_c=sink_c, w_pool=w_pool,
                   pool_scale=pool_scale, w_br_a=w_br_a, w_br_b=w_br_b, w_br_c=w_br_c, w_out=w_out, w_mlp1=w_mlp1, w_mlp2=w_mlp2)
    mom_m = dict(c_ctx=m_c_ctx, w_ada=m_w_ada, b_ada=m_b_ada, norm1=m_norm1, norm2=m_norm2, w_in=m_w_in, q_norm_a=m_q_norm_a,
                 k_norm_a=m_k_norm_a, q_norm_c=m_q_norm_c, k_norm_c=m_k_norm_c, sink_c=m_sink_c, w_pool=m_w_pool,
                 pool_scale=m_pool_scale, w_br_a=m_w_br_a, w_br_b=m_w_br_b, w_br_c=m_w_br_c, w_out=m_w_out, w_mlp1=m_w_mlp1, w_mlp2=m_w_mlp2)
    mom_v = dict(c_ctx=v_c_ctx, w_ada=v_w_ada, b_ada=v_b_ada, norm1=v_norm1, norm2=v_norm2, w_in=v_w_in, q_norm_a=v_q_norm_a,
                 k_norm_a=v_k_norm_a, q_norm_c=v_q_norm_c, k_norm_c=v_k_norm_c, sink_c=v_sink_c, w_pool=v_w_pool,
                 pool_scale=v_pool_scale, w_br_a=v_w_br_a, w_br_b=v_w_br_b, w_br_c=v_w_br_c, w_out=v_w_out, w_mlp1=v_w_mlp1, w_mlp2=v_w_mlp2)

    shards_bf16 = {n: weights[n].astype(BF16) for n in _GATHERED}
    full = [dict() for _ in range(L)]

    def gather_of(items):
        return _Exchange([(shards_bf16[n], l) for l, n in items], scatter=False)

    def gathered(items, arrs):
        for (l, n), a in zip(items, arrs):
            full[l][n] = _from_shards(n, a)

    gathered([(0, "w_in")], gather_of([(0, "w_in")]).alone("gather_first_weights"))

    def hosting(fn, *a, exch=None, done=None, **kw):
        if exch is None:
            return fn(*a, **kw)
        res = fn(*a, exch=exch, **kw)
        done(res[-exch.n:])
        own = res[:-exch.n]
        return own[0] if len(own) == 1 else own

    def gather_behind(l, names):
        if l >= L:
            return {}
        items = [(l, n) for n in names]
        return dict(exch=gather_of(items), done=functools.partial(gathered, items))

    cosf, sins = _rope_tables(cfg)
    xs = jnp.concatenate([ctx, x], axis=1)
    cc8 = _pad8(jnp.concatenate([c, c_ctx[None, :]], axis=0))
    va_blk, vc_blk = (QW + KVW) // KVW, (2 * QW + 3 * KVW) // KVW
    per_layer = lambda a: a[:, None, :]
    b_ada3, norm1_3, norm2_3, ps3 = per_layer(b_ada), per_layer(norm1), per_layer(norm2), per_layer(pool_scale)
    gvec = _gvec(q_norm_a, k_norm_a, q_norm_c, k_norm_c)
    sink8 = jnp.pad(sink_c[:, None, :], ((0, 0), (0, 7), (0, LANES - N_QHEADS)))
    wp = _block_diag(w_pool).astype(BF16)

    cc_all = _Exchange([cc8], scatter=False).alone("gather_cond")[0].reshape(8 * N_DEV, D)
    mod_cols = _Exchange([_adaln_fwd(cfg, cc_all, w_ada)], scatter=True).alone("scatter_mod")[0]
    mod_all = _adaln_join(cfg, mod_cols, b_ada3)

    saved = []
    for l in range(L):
        fw = full[l]
        ctx_active = l < L - 1
        mod = mod_all[l]
        z, h = hosting(_in_proj_fwd, cfg, l, xs, norm1_3, mod, fw["w_in"], **gather_behind(l, _MERGE_WEIGHTS if l == 0 else ("w_mlp2",)))
        qa, ka, qc, kc = _qknorm_fwd(cfg, l, z, gvec, cosf, sins)
        oa = hosting(_attn_fwd, cfg, l, qa, ka, z, va_blk, sink8, window=False, sink=False, ctx_attend=ctx_active, name="attn_a_fwd",
                     **gather_behind(l, ("w_mlp1",)))
        oc = hosting(_attn_fwd, cfg, l, qc, kc, z, vc_blk, sink8, window=True, sink=True, ctx_attend=ctx_active, name="attn_c_fwd",
                     **(gather_behind(0, ("w_mlp2",)) if l == 0 else {}))
        ob, pooled = _pool_fwd(cfg, l, z, wp, ps3)
        x1, mgo = hosting(_merge_fwd, cfg, xs, oa, ob, oc, z, mod, fw["w_br_a"], fw["w_br_b"], fw["w_br_c"], fw["w_out"],
                          ctx_active=ctx_active, **gather_behind(l + 1, _MERGE_WEIGHTS))
        if l < L - 1:
            x2, mo = hosting(_mlp_fwd, cfg, l, x1, norm2_3, mod, fw["w_mlp1"], fw["w_mlp2"], ctx_active=ctx_active,
                             **gather_behind(l + 1, ("w_in",)))
        else:
            x2, mo, sse = _mlp_fwd(cfg, l, x1, norm2_3, mod, fw["w_mlp1"], fw["w_mlp2"], ctx_active=ctx_active, target=loss_target)
        saved.append(dict(xs=xs, mod=mod, z=z, h=h, qa=qa, ka=ka, qc=qc, kc=kc, oa=oa, oc=oc, ob=ob, pooled=pooled, x1=x1, mgo=mgo, mo=mo))
        xs = x2

    dxs = xs
    loss = lax.psum(0.5 * sse[0, 0] / D, ("x", "y", "c"))

    grads = [dict() for _ in range(L)]
    parts = {}
    small = {n: [None] * L for n in _SMALL if n != "c_ctx"}
    d_c_ctx = jnp.zeros((D,), F32)
    flat2 = lambda a: a.reshape(B * T, a.shape[-1])

    def scatter_of(l, names):
        return _Exchange([_to_shards(n, grads[l][n]) for n in names], scatter=True)

    def scattered(l, names, arrs):
        for n, a in zip(names, arrs):
            parts[(l, n)] = a

    def scatter_behind(l, names):
        if l >= L:
            return {}
        return dict(exch=scatter_of(l, names), done=functools.partial(scattered, l, names))

    for l in reversed(range(L)):
        fw, sv, g = full[l], saved[l], grads[l]
        ctx_active = l < L - 1
        mod = sv["mod"]
        dx1, h2, r, da, dout, dmod_mlp, dgn2 = hosting(_mlp_bwd, cfg, l, sv["x1"], dxs, sv["mo"], norm2_3, mod, fw["w_mlp1"], fw["w_mlp2"],
                                                       ctx_active=ctx_active, **scatter_behind(l + 1, ("w_in",)))
        g["w_mlp1"] = _matmul_tn(flat2(h2), flat2(da), "dw_mlp1", by_shard=True)
        g["w_mlp2"] = _matmul_tn(flat2(r), flat2(dout), "dw_mlp2", by_shard=False)
        doa, dob, doc, dpa, dpb, dpc, y, dmo, dzg, dg1 = _merge_bwd(
            cfg, dx1, sv["mgo"], sv["oa"], sv["ob"], sv["oc"], sv["z"], mod, fw["w_br_a"], fw["w_br_b"], fw["w_br_c"], fw["w_out"],
            ctx_active=ctx_active)
        g["w_out"] = _matmul_tn(flat2(y), flat2(dmo), "dw_out", by_shard=False)
        g["w_br_a"] = _matmul_tn(flat2(sv["oa"]), flat2(dpa), "dw_br_a", by_shard=True)
        g["w_br_b"] = _matmul_tn(flat2(sv["ob"]), flat2(dpb), "dw_br_b", by_shard=True)
        g["w_br_c"] = _matmul_tn(flat2(sv["oc"]), flat2(dpc), "dw_br_c", by_shard=True)
        z = sv["z"]
        dqa, dka, dva, _ = hosting(_attn_bwd, cfg, l, sv["qa"], sv["ka"], z, va_blk, sink8, doa, window=False, sink=False,
                                   ctx_attend=ctx_active, name="attn_a_bwd", **scatter_behind(l, ("w_mlp1", "w_mlp2")))
        dqc, dkc, dvc, dsink = hosting(_attn_bwd, cfg, l, sv["qc"], sv["kc"], z, vc_blk, sink8, doc, window=True, sink=True,
                                       ctx_attend=ctx_active, name="attn_c_bwd", **scatter_behind(l, _MERGE_WEIGHTS))
        dzq, dgvec = _qknorm_bwd(cfg, l, z, gvec, cosf, sins, dqa, dka, dva, dqc, dkc, dvc)
        du, dwp, dps = _pool_bwd(cfg, l, dob, sv["pooled"], wp, ps3)
        dxs, dz, dmod_in, dgn1 = _in_proj_bwd(cfg, l, dzq, du, dzg, fw["w_in"], sv["xs"], dx1, norm1_3, mod, latent_only=(l == 0))
        g["w_in"] = _matmul_tn(flat2(sv["h"]), flat2(dz), "dw_in", by_shard=False)
        dmod_cols, dbias = _dmod_pack(cfg, dmod_in, dg1, dmod_mlp)
        dm_all = _Exchange([dmod_cols], scatter=True).alone("scatter_dmod")[0].reshape(8 * N_DEV, -1)
        g["w_ada"], dcc = _adaln_bwd(cfg, l, cc_all, dm_all, w_ada)
        d_c_ctx = d_c_ctx + dcc[0]
        gv = dgvec[0]
        heads = lambda v, n: v.reshape(n, HEAD).sum(axis=0)
        small["b_ada"][l] = dbias[0]
        small["norm1"][l] = dgn1[0]
        small["norm2"][l] = dgn2[0]
        small["q_norm_a"][l] = heads(gv[0:QW], N_QHEADS)
        small["k_norm_a"][l] = heads(gv[QW:QW + KVW], N_KV)
        small["q_norm_c"][l] = heads(gv[QW + 2 * KVW:2 * QW + 2 * KVW], N_QHEADS)
        small["k_norm_c"][l] = heads(gv[2 * QW + 2 * KVW:2 * QW + 3 * KVW], N_KV)
        small["sink_c"][l] = dsink[0, :N_QHEADS]
        small["w_pool"][l] = jnp.stack([dwp[i * HEAD:(i + 1) * HEAD, i * HEAD:(i + 1) * HEAD] for i in range(len(POOL_WINDOWS))])
        small["pool_scale"][l] = dps[0]
    grad_x = dxs

    scattered(0, ("w_in",), scatter_of(0, ("w_in",)).alone("scatter_last_grads"))
    for l in range(L):
        parts[(l, "w_ada")] = grads[l]["w_ada"][None]
    stepped = {n: _adamw_sharded([parts[(l, n)] for l in range(L)], weights[n], mom_m[n], mom_v[n], "adamw_" + n) for n in _SHARDED}

    small_vals = {n: jnp.stack(v) for n, v in small.items()}
    small_vals["c_ctx"] = d_c_ctx
    small_parts = _Exchange([_pack_small(small_vals)], scatter=False).alone("gather_small_grads")[0]
    stepped_small = _adamw_packed(small_parts, _pack_small(weights), _pack_small(mom_m), _pack_small(mom_v), "adamw_small")

    outs = []
    for i in range(4):
        res = {n: stepped[n][i] for n in _SHARDED}
        res.update(_unpack_small(stepped_small[i], weights))
        outs.append(res)
    order = ("c_ctx", "w_ada", "b_ada", "norm1", "norm2", "w_in", "q_norm_a", "k_norm_a", "q_norm_c", "k_norm_c", "sink_c",
             "w_pool", "pool_scale", "w_br_a", "w_br_b", "w_br_c", "w_out", "w_mlp1", "w_mlp2")
    return (loss, grad_x, *[res[n] for res in outs for n in order])
```

```python
import functools

import jax
import jax.numpy as jnp
from jax import lax
from jax.experimental import pallas as pl
from jax.experimental.pallas import tpu as pltpu

F32 = jnp.float32
BF16 = jnp.bfloat16
HIGHEST = lax.Precision.HIGHEST

N_DEV = 8
HEAD = 64
N_QHEADS = 6
N_KV = 2
GROUP = 3
QW = N_QHEADS * HEAD
KVW = N_KV * HEAD
QKV_W = 2 * (QW + 2 * KVW)
POOL_W = 256
POOL_WINDOWS = (2, 4, 8, 16)
GATE0 = QKV_W + POOL_W
WINDOW = 128
GRID_W = 64
ROPE_THETA = 10000.0
EPS = 1e-6
NEG = -1e30
QSCALE = HEAD ** -0.5
LANES = 128
PACK_W = 1024
VMEM_LIMIT = 56 * 1024 * 1024

ADAM_LR = 0.001
ADAM_B1 = 0.9
ADAM_B2 = 0.999
ADAM_EPS = 1e-08
ADAM_WD = 0.01
ADAM_STEP = 10

NT_DIMS = (((1,), (1,)), ((), ()))
TN_DIMS = (((0,), (0,)), ((), ()))


def _dot(a, b):
    return jnp.dot(a, b, preferred_element_type=F32)


def _dot_nt(a, b):
    return lax.dot_general(a, b, NT_DIMS, preferred_element_type=F32)


def _dot_tn(a, b):
    return lax.dot_general(a, b, TN_DIMS, preferred_element_type=F32)


def _params(n_grid):
    return pltpu.CompilerParams(dimension_semantics=("arbitrary",) * n_grid, vmem_limit_bytes=VMEM_LIMIT)


def _full(shape):
    nd = len(shape)
    return pl.BlockSpec(shape, lambda *_: (0,) * nd)


def _layer(l, width):
    return pl.BlockSpec((1, 1, width), lambda *_: (l, 0, 0))


def _modulate(x, gn, shift, scale):
    rstd = lax.rsqrt(jnp.mean(x * x, axis=-1, keepdims=True) + EPS)
    xhat = x * rstd
    return xhat * gn * (1.0 + scale) + shift, xhat, rstd


def _modulate_bwd(dh, xhat, rstd, gn, scale):
    d_shift = jnp.sum(dh, axis=0, keepdims=True)
    d_scale = jnp.sum(dh * xhat * gn, axis=0, keepdims=True)
    dy = dh * (1.0 + scale)
    d_gn = jnp.sum(dy * xhat, axis=0, keepdims=True)
    dxh = dy * gn
    dx = rstd * (dxh - xhat * jnp.mean(dxh * xhat, axis=-1, keepdims=True))
    return dx, d_shift, d_scale, d_gn


def _mod_row(mod_ref, row, k, d):
    return mod_ref[pl.ds(row, 1), k * d:(k + 1) * d]


class _Cfg:
    def __init__(self, b, s, n, d):
        self.B, self.S, self.N, self.D = b, s, n, d
        self.T = n + s
        self.F = 4 * d
        self.IN = GATE0 + 3 * d
        self.tm = 256 if (n % 256 == 0 and s % 256 == 0) else 128
        self.nT = self.T // self.tm
        self.nC = n // self.tm
        self.gw = 512 if d % 512 == 0 else 256
        self.kw = self.tm + 2 * LANES
        assert GATE0 % self.gw == 0 and d % self.gw == 0 and b < 8 and self.T >= self.kw and max(POOL_WINDOWS) // 2 <= LANES
        assert s % GRID_W == 0 and n % self.tm == 0 and s % self.tm == 0 and s >= self.tm + 2 * WINDOW
        assert d % (N_DEV * LANES) == 0


def _peer(k):
    x, y, c = lax.axis_index("x"), lax.axis_index("y"), lax.axis_index("c")
    px = x ^ ((k >> 2) & 1)
    py = y ^ ((k >> 1) & 1)
    pc = c ^ (k & 1)
    return (px, py, pc), 4 * px + 2 * py + pc


class _Exchange:
    def __init__(self, arrays, scatter):
        self.arrays = [a if isinstance(a, tuple) else (a, None) for a in arrays]
        self.scatter = scatter
        self.n = len(self.arrays)

    def operands(self):
        return [a for a, _ in self.arrays]

    def out_shapes(self):
        res = []
        for a, layer in self.arrays:
            shape = a.shape[1:] if (self.scatter or layer is not None) else a.shape
            res.append(jax.ShapeDtypeStruct((N_DEV,) + tuple(shape), a.dtype))
        return res

    def scratch(self):
        n = self.n * (N_DEV - 1)
        return [pltpu.SemaphoreType.DMA((n,)), pltpu.SemaphoreType.DMA((n,)), pltpu.SemaphoreType.DMA((self.n,))]

    def _copies(self, x_refs, out_refs, send_sems, recv_sems, local_sems, want):
        _, me = _peer(0)
        res = []
        for i, ((_, layer), x_ref, out_ref) in enumerate(zip(self.arrays, x_refs, out_refs)):
            if self.scatter:
                src_of = lambda d, x_ref=x_ref: x_ref.at[d]
            elif layer is not None:
                src_of = lambda d, x_ref=x_ref, layer=layer: x_ref.at[layer]
            else:
                src_of = lambda d, x_ref=x_ref: x_ref
            if want == "local":
                res.append(pltpu.make_async_copy(src_of(me), out_ref.at[me], local_sems.at[i]))
                continue
            for k in range(1, N_DEV):
                pos, idx = _peer(k)
                j = i * (N_DEV - 1) + k - 1
                common = dict(send_sem=send_sems.at[j], recv_sem=recv_sems.at[j], device_id=pos, device_id_type=pl.DeviceIdType.MESH)
                if want == "send":
                    res.append(pltpu.make_async_remote_copy(src_ref=src_of(idx), dst_ref=out_ref.at[me], **common))
                else:
                    res.append(pltpu.make_async_remote_copy(src_ref=src_of(me), dst_ref=out_ref.at[idx], **common))
        return res

    def start(self, *refs):
        for cp in self._copies(*refs, "local") + self._copies(*refs, "send"):
            cp.start()

    def wait(self, *refs):
        for cp in self._copies(*refs, "recv"):
            cp.wait_recv()
        for cp in self._copies(*refs, "send"):
            cp.wait_send()
        for cp in self._copies(*refs, "local"):
            cp.wait()

    def alone(self, name):
        n = self.n

        def body(*refs):
            args = (refs[:n], refs[n:2 * n], *refs[2 * n:])
            self.start(*args)
            self.wait(*args)

        any_spec = pl.BlockSpec(memory_space=pl.ANY)
        return pl.pallas_call(body, name=name, in_specs=[any_spec] * n, out_specs=[any_spec] * n,
                              out_shape=self.out_shapes(), scratch_shapes=self.scratch())(*self.operands())


def _pcall(exch):
    if exch is None:
        return pl.pallas_call

    def make(body, *, name, grid, in_specs, out_specs, out_shape, compiler_params, scratch_shapes=()):
        multi = isinstance(out_shape, (list, tuple))
        out_specs_l = list(out_specs) if multi else [out_specs]
        out_shape_l = list(out_shape) if multi else [out_shape]
        n_in, n_out, n_x, n_s = len(in_specs), len(out_specs_l), exch.n, len(scratch_shapes)

        def hosted(*refs):
            ins, x_refs = refs[:n_in], refs[n_in:n_in + n_x]
            o0 = n_in + n_x
            outs, xo_refs = refs[o0:o0 + n_out], refs[o0 + n_out:o0 + n_out + n_x]
            s0 = o0 + n_out + n_x
            own_scratch, sems = refs[s0:s0 + n_s], refs[s0 + n_s:]
            ids = [pl.program_id(i) for i in range(len(grid))]
            first = functools.reduce(jnp.logical_and, [i == 0 for i in ids])
            last = functools.reduce(jnp.logical_and, [i == g - 1 for i, g in zip(ids, grid)])

            @pl.when(first)
            def _():
                exch.start(x_refs, xo_refs, *sems)

            body(*ins, *outs, *own_scratch)

            @pl.when(last)
            def _():
                exch.wait(x_refs, xo_refs, *sems)

        any_spec = pl.BlockSpec(memory_space=pl.ANY)
        call = pl.pallas_call(
            hosted, name=name, grid=grid, in_specs=list(in_specs) + [any_spec] * n_x, out_specs=out_specs_l + [any_spec] * n_x,
            out_shape=out_shape_l + exch.out_shapes(), scratch_shapes=list(scratch_shapes) + exch.scratch(),
            compiler_params=compiler_params)
        return lambda *args: call(*args, *exch.operands())

    return make


def _adaln_fwd(cfg, cc_all, w_ada):
    d = cfg.D
    L, _, wa = w_ada.shape

    def body(c_ref, w_ref, o_ref):
        c = c_ref[...]
        a = (c * jax.nn.sigmoid(c)).astype(BF16)
        for l in range(L):
            m = _dot(a, w_ref[l].astype(BF16))
            for p in range(N_DEV):
                o_ref[p, l] = m[8 * p:8 * (p + 1)]

    return pl.pallas_call(
        body, name="adaln_fwd", grid=(1,),
        in_specs=[_full((8 * N_DEV, d)), _full((L, d, wa))],
        out_specs=_full((N_DEV, L, 8, wa)),
        out_shape=jax.ShapeDtypeStruct((N_DEV, L, 8, wa), F32), compiler_params=_params(1),
    )(cc_all, w_ada)


def _adaln_join(cfg, parts, b_ada):
    d = cfg.D
    _, L, _, wa = parts.shape

    def body(p_ref, b_ref, o_ref):
        for l in range(L):
            for j in range(N_DEV):
                o_ref[l, :, j * wa:(j + 1) * wa] = p_ref[j, l] + b_ref[l, :, j * wa:(j + 1) * wa]

    return pl.pallas_call(
        body, name="adaln_join", grid=(1,),
        in_specs=[_full((N_DEV, L, 8, wa)), _full((L, 1, 6 * d))],
        out_specs=_full((L, 8, 6 * d)),
        out_shape=jax.ShapeDtypeStruct((L, 8, 6 * d), F32), compiler_params=_params(1),
    )(parts, b_ada)


def _in_proj_fwd(cfg, l, x, gn, mod, w_in, exch=None):
    B, T, D, IN, tm, nC = cfg.B, cfg.T, cfg.D, cfg.IN, cfg.tm, cfg.nC

    def body(x_ref, gn_ref, mod_ref, w_ref, z_ref, h_ref):
        b, t = pl.program_id(0), pl.program_id(1)
        row = jnp.where(t < nC, B, b)
        h, _, _ = _modulate(x_ref[0], gn_ref[0], _mod_row(mod_ref, row, 0, D), _mod_row(mod_ref, row, 1, D))
        hb = h.astype(BF16)
        h_ref[0] = hb
        z_ref[0] = _dot(hb, w_ref[...])

    return _pcall(exch)(
        body, name="in_proj_fwd", grid=(B, cfg.nT),
        in_specs=[pl.BlockSpec((1, tm, D), lambda b, t: (b, t, 0)), _layer(l, D), _full((8, 6 * D)), _full((D, IN))],
        out_specs=[pl.BlockSpec((1, tm, IN), lambda b, t: (b, t, 0)), pl.BlockSpec((1, tm, D), lambda b, t: (b, t, 0))],
        out_shape=[jax.ShapeDtypeStruct((B, T, IN), F32), jax.ShapeDtypeStruct((B, T, D), BF16)],
        compiler_params=_params(2),
    )(x, gn, mod, w_in)


def _head_indicator():
    r = lax.broadcasted_iota(jnp.int32, (LANES, LANES), 0) // HEAD
    c = lax.broadcasted_iota(jnp.int32, (LANES, LANES), 1) // HEAD
    return jnp.where(r == c, 1.0, 0.0).astype(BF16)


def _head_sum(x, ind):
    hi = x.astype(BF16)
    lo = (x - hi.astype(F32)).astype(BF16)
    return _dot(hi, ind) + _dot(lo, ind)


def _pair_swap(y):
    lane = lax.broadcasted_iota(jnp.int32, y.shape, 1)
    return jnp.where(lane % 2 == 0, pltpu.roll(y, LANES - 1, 1), pltpu.roll(y, 1, 1))


_QK_CHUNKS = (0, 1, 2, 3, 5, 6, 7, 8)
_Q_CHUNKS = (0, 1, 2, 5, 6, 7)


def _qknorm_fwd(cfg, l, z, gvec, cosf, sins):
    B, T, tm = cfg.B, cfg.T, cfg.tm

    def body(z_ref, g_ref, cos_ref, sin_ref, qa_ref, ka_ref, qc_ref, kc_ref):
        ind = _head_indicator()
        cos, sin = cos_ref[...], sin_ref[...]

        def chunk(c):
            x = z_ref[0, :, c * LANES:(c + 1) * LANES]
            ss = _head_sum(x * x, ind)
            y = x * lax.rsqrt(ss * (1.0 / HEAD) + EPS) * g_ref[0, :, c * LANES:(c + 1) * LANES]
            out = y * cos + _pair_swap(y) * sin
            return (out * QSCALE if c in _Q_CHUNKS else out).astype(BF16)

        qa_ref[0] = jnp.concatenate([chunk(0), chunk(1), chunk(2)], axis=-1)
        ka_ref[0] = chunk(3)
        qc_ref[0] = jnp.concatenate([chunk(5), chunk(6), chunk(7)], axis=-1)
        kc_ref[0] = chunk(8)

    row = lambda w: pl.BlockSpec((1, tm, w), lambda b, t: (b, t, 0))
    tab = pl.BlockSpec((tm, LANES), lambda b, t: (t, 0))
    return pl.pallas_call(
        body, name="qknorm_fwd", grid=(B, cfg.nT),
        in_specs=[row(QKV_W), _layer(l, QKV_W), tab, tab],
        out_specs=[row(QW), row(KVW), row(QW), row(KVW)],
        out_shape=[jax.ShapeDtypeStruct((B, T, w), BF16) for w in (QW, KVW, QW, KVW)],
        compiler_params=_params(2),
    )(z, gvec, cosf, sins)


def _attn_scores(cfg, tl, q, k_ref, v_ref, sink_ref, h, loc, window, sink, lse=None):
    S, N, tq = cfg.S, cfg.N, cfg.tm
    hs = slice(h * HEAD, (h + 1) * HEAD)
    qs = jnp.concatenate([q[:, (GROUP * h + g) * HEAD:(GROUP * h + g + 1) * HEAD] for g in range(GROUP)], axis=0)
    lo = None
    if not loc:
        kk = k_ref[0, 0:N, :][:, hs]
        vv = v_ref[0, 0:N, :].astype(BF16)[:, hs]
    elif not window:
        kk = k_ref[0][:, hs]
        vv = v_ref[0].astype(BF16)[:, hs]
    else:
        W = tq + 2 * WINDOW
        lo = pl.multiple_of(jnp.clip(tl * tq - WINDOW, 0, S - W), LANES)
        kk = jnp.concatenate([k_ref[0, 0:N, :], k_ref[0, pl.ds(N + lo, W), :]], axis=0)[:, hs]
        vv = jnp.concatenate([v_ref[0, 0:N, :], v_ref[0, pl.ds(N + lo, W), :]], axis=0).astype(BF16)[:, hs]
    st = _dot_nt(kk, qs)
    if window:
        krow = lax.broadcasted_iota(jnp.int32, st.shape, 0)
        qpos = tl * tq + lax.broadcasted_iota(jnp.int32, st.shape, 1) % tq
        st = jnp.where((krow < N) | (jnp.abs(qpos - (lo + krow - N)) <= WINDOW), st, NEG)
    sk = None
    if sink:
        colg = lax.broadcasted_iota(jnp.int32, (1, GROUP * tq), 1) // tq
        sk = jnp.zeros((1, GROUP * tq), F32)
        for g in range(GROUP):
            j = GROUP * h + g
            sk = jnp.where(colg == g, sink_ref[0, 0:1, j:j + 1], sk)
    if lse is not None:
        return qs, kk, vv, jnp.exp(st - lse), None, (jnp.exp(sk - lse) if sink else None), lo, lse
    m = jnp.max(st, axis=0, keepdims=True)
    if sink:
        m = jnp.maximum(m, sk)
    e = jnp.exp(st - m)
    l = jnp.sum(e, axis=0, keepdims=True)
    e_s = None
    if sink:
        e_s = jnp.exp(sk - m)
        l = l + e_s
    return qs, kk, vv, e, 1.0 / l, e_s, lo, m + jnp.log(l)


def _attn_fwd(cfg, l, q, k, z, vblock, sink8, *, window, sink, ctx_attend, name, exch=None):
    B, T, tq, nC = cfg.B, cfg.T, cfg.tm, cfg.nC

    def body(q_ref, k_ref, v_ref, sink_ref, o_ref, lse_ref):
        t = pl.program_id(1)

        def run(loc):
            q_t = q_ref[0]
            outs = [None] * N_QHEADS
            lses = [None] * N_QHEADS
            for h in range(N_KV):
                _, _, vv, e, inv, _, _, lse = _attn_scores(cfg, t - nC, q_t, k_ref, v_ref, sink_ref, h, loc, window and loc, sink)
                o = (_dot_tn(vv, e.astype(BF16)) * inv).T
                for g in range(GROUP):
                    outs[GROUP * h + g] = o[g * tq:(g + 1) * tq]
                    lses[GROUP * h + g] = lse[:, g * tq:(g + 1) * tq]
            o_ref[0] = jnp.concatenate(outs, axis=-1).astype(BF16)
            lse_ref[0] = jnp.concatenate(lses + [jnp.zeros((8 - N_QHEADS, tq), F32)], axis=0)

        pl.when(t >= nC)(functools.partial(run, True))
        if ctx_attend:
            pl.when(t < nC)(functools.partial(run, False))
        else:
            @pl.when(t < nC)
            def _():
                o_ref[0] = jnp.zeros((tq, QW), BF16)
                lse_ref[0] = jnp.zeros((8, tq), F32)

    return _pcall(exch)(
        body, name=name, grid=(B, cfg.nT),
        in_specs=[pl.BlockSpec((1, tq, QW), lambda b, t: (b, t, 0)),
                  pl.BlockSpec((1, T, KVW), lambda b, t: (b, 0, 0)),
                  pl.BlockSpec((1, T, KVW), lambda b, t: (b, 0, vblock)),
                  pl.BlockSpec((1, 8, LANES), lambda b, t: (l, 0, 0))],
        out_specs=[pl.BlockSpec((1, tq, QW), lambda b, t: (b, t, 0)), pl.BlockSpec((1, 8, tq), lambda b, t: (b, 0, t))],
        out_shape=[jax.ShapeDtypeStruct((B, T, QW), BF16), jax.ShapeDtypeStruct((B, 8, T), F32)], compiler_params=_params(2),
    )(q, k, z, sink8)


def _pool_geometry(cfg, t):
    tm, N, T, nC = cfg.tm, cfg.N, cfg.T, cfg.nC
    r0 = pl.multiple_of(t * tm, tm)
    isctx = t < nC
    seg_lo = jnp.where(isctx, 0, N)
    seg_hi = jnp.where(isctx, N, T)
    k0 = pl.multiple_of(jnp.clip(t * tm - LANES, 0, T - cfg.kw), LANES)
    return r0, seg_lo, seg_hi, k0


def _pool_count(pos, h, seg_lo, seg_hi):
    return jnp.maximum(jnp.minimum(pos + h, seg_hi) - jnp.maximum(pos - h, seg_lo), 1).astype(F32)


def _split_bf16(x):
    hi = x.astype(BF16)
    return hi, (x - hi.astype(F32)).astype(BF16)


def _pool_fwd(cfg, l, z, wp, ps):
    B, T, tm, kw = cfg.B, cfg.T, cfg.tm, cfg.kw

    def body(u_ref, wp_ref, ps_ref, ob_ref, pooled_ref):
        t = pl.program_id(1)
        r0, seg_lo, seg_hi, k0 = _pool_geometry(cfg, t)
        hi, lo = _split_bf16(u_ref[0, pl.ds(k0, kw), :])
        rr = r0 + lax.broadcasted_iota(jnp.int32, (tm, kw), 0)
        cc = k0 + lax.broadcasted_iota(jnp.int32, (tm, kw), 1)
        diff = cc - rr
        inseg = (cc >= seg_lo) & (cc < seg_hi)
        rcol = r0 + lax.broadcasted_iota(jnp.int32, (tm, 1), 0)
        group = lax.broadcasted_iota(jnp.int32, (tm, POOL_W), 1) // HEAD
        acc = jnp.zeros((tm, POOL_W), F32)
        for gi, w in enumerate(POOL_WINDOWS):
            h = w // 2
            band = jnp.where((diff >= -h) & (diff <= h - 1) & inseg, 1.0, 0.0).astype(BF16)
            tot = _dot(band, hi) + _dot(band, lo)
            acc = jnp.where(group == gi, tot / _pool_count(rcol, h, seg_lo, seg_hi), acc)
        pooled = (acc - u_ref[0, pl.ds(r0, tm), :]).astype(BF16)
        pooled_ref[0] = pooled
        ob_ref[0] = (_dot(pooled, wp_ref[0]) * ps_ref[0]).astype(BF16)

    row = pl.BlockSpec((1, tm, POOL_W), lambda b, t: (b, t, 0))
    return pl.pallas_call(
        body, name="pool_fwd", grid=(B, cfg.nT),
        in_specs=[pl.BlockSpec((1, T, POOL_W), lambda b, t: (b, 0, QKV_W // POOL_W)),
                  pl.BlockSpec((1, POOL_W, POOL_W), lambda b, t: (l, 0, 0)), _layer(l, POOL_W)],
        out_specs=[row, row],
        out_shape=[jax.ShapeDtypeStruct((B, T, POOL_W), BF16)] * 2, compiler_params=_params(2),
    )(z, wp, ps)


def _gate_specs(cfg):
    tm, gw = cfg.tm, cfg.gw
    first = GATE0 // gw
    return [pl.BlockSpec((1, tm, gw), functools.partial(lambda b, t, j: (b, t, j), j=first + i)) for i in range(3 * cfg.D // gw)]


def _read_gates(cfg, gate_refs):
    per = cfg.D // cfg.gw
    return [jnp.concatenate([gate_refs[k * per + i][0] for i in range(per)], axis=-1) for k in range(3)]


def _merge_fwd(cfg, x, oa, ob, oc, z, mod, wa, wb, wc, wo, *, ctx_active, exch=None):
    B, T, D, tm, nC = cfg.B, cfg.T, cfg.D, cfg.tm, cfg.nC
    ng = 3 * D // cfg.gw

    def body(x_ref, oa_ref, ob_ref, oc_ref, *rest):
        gate_refs = rest[:ng]
        mod_ref, wa_ref, wb_ref, wc_ref, wo_ref, x1_ref, mgo_ref = rest[ng:]
        b, t = pl.program_id(0), pl.program_id(1)

        def compute():
            row = jnp.where(t < nC, B, b)
            ga, gb, gc = _read_gates(cfg, gate_refs)
            y = (jax.nn.sigmoid(ga) * _dot(oa_ref[0], wa_ref[...])
                 + jax.nn.sigmoid(gb) * _dot(ob_ref[0], wb_ref[...])
                 + jax.nn.sigmoid(gc) * _dot(oc_ref[0], wc_ref[...]))
            mo = _dot(y.astype(BF16), wo_ref[...])
            mgo_ref[0] = mo.astype(BF16)
            x1_ref[0] = x_ref[0] + _mod_row(mod_ref, row, 2, D) * mo

        if ctx_active:
            compute()
        else:
            pl.when(t >= nC)(compute)

            @pl.when(t < nC)
            def _():
                mgo_ref[0] = jnp.zeros((tm, D), BF16)
                x1_ref[0] = x_ref[0]

    row = lambda w: pl.BlockSpec((1, tm, w), lambda b, t: (b, t, 0))
    return _pcall(exch)(
        body, name="merge_fwd", grid=(B, cfg.nT),
        in_specs=[row(D), row(QW), row(POOL_W), row(QW)] + _gate_specs(cfg)
        + [_full((8, 6 * D)), _full((QW, D)), _full((POOL_W, D)), _full((QW, D)), _full((D, D))],
        out_specs=[row(D), row(D)],
        out_shape=[jax.ShapeDtypeStruct((B, T, D), F32), jax.ShapeDtypeStruct((B, T, D), BF16)],
        compiler_params=_params(2),
    )(x, oa, ob, oc, *([z] * ng), mod, wa, wb, wc, wo)


def _w1_apply(hb, w1_ref):
    return jnp.concatenate([_dot(hb, w1_ref[d]) for d in range(N_DEV)], axis=-1)


def _mlp_fwd(cfg, l, x1, gn, mod, w1, w2, *, ctx_active, target=None, exch=None):
    B, T, D, F, tm, nC = cfg.B, cfg.T, cfg.D, cfg.F, cfg.tm, cfg.nC
    assert target is None or not ctx_active

    def body(x_ref, gn_ref, mod_ref, w1_ref, w2_ref, *rest):
        if target is None:
            x2_ref, mo_ref = rest
        else:
            tgt_ref, x2_ref, mo_ref, sse_ref = rest
            _acc_init([sse_ref])
        b, t = pl.program_id(0), pl.program_id(1)

        def compute():
            row = jnp.where(t < nC, B, b)
            x = x_ref[0]
            h, _, _ = _modulate(x, gn_ref[0], _mod_row(mod_ref, row, 3, D), _mod_row(mod_ref, row, 4, D))
            a = jnp.maximum(_w1_apply(h.astype(BF16), w1_ref), 0.0)
            mo = _dot((a * a).astype(BF16), w2_ref[...])
            mo_ref[0] = mo.astype(BF16)
            x2 = x + _mod_row(mod_ref, row, 5, D) * mo
            if target is None:
                x2_ref[0] = x2
            else:
                err = x2 - tgt_ref[0]
                x2_ref[0] = err * (1.0 / D)
                sse_ref[...] += jnp.sum(err * err)

        if ctx_active:
            compute()
        else:
            pl.when(t >= nC)(compute)

            @pl.when(t < nC)
            def _():
                mo_ref[0] = jnp.zeros((tm, D), BF16)
                x2_ref[0] = x_ref[0] if target is None else jnp.zeros((tm, D), F32)

    row = pl.BlockSpec((1, tm, D), lambda b, t: (b, t, 0))
    in_specs = [row, _layer(l, D), _full((8, 6 * D)), _full((N_DEV, D, F // N_DEV)), _full((F, D))]
    out_specs = [row, row]
    out_shape = [jax.ShapeDtypeStruct((B, T, D), F32), jax.ShapeDtypeStruct((B, T, D), BF16)]
    args = [x1, gn, mod, w1, w2]
    if target is not None:
        in_specs.append(pl.BlockSpec((1, tm, D), lambda b, t: (b, jnp.maximum(t - nC, 0), 0)))
        out_specs.append(_full((8, LANES)))
        out_shape.append(jax.ShapeDtypeStruct((8, LANES), F32))
        args.append(target)
    return _pcall(exch)(
        body, name="mlp_fwd", grid=(B, cfg.nT), in_specs=in_specs, out_specs=out_specs, out_shape=out_shape,
        compiler_params=_params(2),
    )(*args)


def _acc_init(refs):
    b, t = pl.program_id(0), pl.program_id(1)

    @pl.when((b == 0) & (t == 0))
    def _():
        for ref in refs:
            ref[...] = jnp.zeros(ref.shape, ref.dtype)


def _mlp_bwd(cfg, l, x1, dx2, mo, gn, mod, w1, w2, *, ctx_active, exch=None):
    B, T, D, F, tm, nC = cfg.B, cfg.T, cfg.D, cfg.F, cfg.tm, cfg.nC
    ws = F // N_DEV

    def body(x_ref, dx_ref, mo_ref, gn_ref, mod_ref, w1_ref, w2_ref, dx1_ref, h_ref, r_ref, da_ref, dout_ref, dmod_ref, dgn_ref):
        b, t = pl.program_id(0), pl.program_id(1)
        _acc_init([dmod_ref, dgn_ref])

        def compute():
            row = jnp.where(t < nC, B, b)
            gn = gn_ref[0]
            scale = _mod_row(mod_ref, row, 4, D)
            h, xhat, rstd = _modulate(x_ref[0], gn, _mod_row(mod_ref, row, 3, D), scale)
            hb = h.astype(BF16)
            a = jnp.maximum(_w1_apply(hb, w1_ref), 0.0)
            dx = dx_ref[0]
            dout = (dx * _mod_row(mod_ref, row, 5, D)).astype(BF16)
            da = (_dot_nt(dout, w2_ref[...]) * (2.0 * a)).astype(BF16)
            dh = _dot_nt(da[:, 0:ws], w1_ref[0])
            for d in range(1, N_DEV):
                dh = dh + _dot_nt(da[:, d * ws:(d + 1) * ws], w1_ref[d])
            dxn, d_shift, d_scale, d_gn = _modulate_bwd(dh, xhat, rstd, gn, scale)
            dx1_ref[0] = dx + dxn
            h_ref[0] = hb
            r_ref[0] = (a * a).astype(BF16)
            da_ref[0] = da
            dout_ref[0] = dout
            d_gate = jnp.sum(dx * mo_ref[0].astype(F32), axis=0, keepdims=True)
            dmod_ref[pl.ds(row, 1), :] += jnp.concatenate([d_shift, d_scale, d_gate], axis=-1)
            dgn_ref[0:1, :] += d_gn

        if ctx_active:
            compute()
        else:
            pl.when(t >= nC)(compute)

            @pl.when(t < nC)
            def _():
                dx1_ref[0] = dx_ref[0]
                h_ref[0] = jnp.zeros((tm, D), BF16)
                r_ref[0] = jnp.zeros((tm, F), BF16)
                da_ref[0] = jnp.zeros((tm, F), BF16)
                dout_ref[0] = jnp.zeros((tm, D), BF16)

    row = lambda w: pl.BlockSpec((1, tm, w), lambda b, t: (b, t, 0))
    sds = lambda w, dt: jax.ShapeDtypeStruct((B, T, w), dt)
    return _pcall(exch)(
        body, name="mlp_bwd", grid=(B, cfg.nT),
        in_specs=[row(D), row(D), row(D), _layer(l, D), _full((8, 6 * D)), _full((N_DEV, D, ws)), _full((F, D))],
        out_specs=[row(D), row(D), row(F), row(F), row(D), _full((8, 3 * D)), _full((8, D))],
        out_shape=[sds(D, F32), sds(D, BF16), sds(F, BF16), sds(F, BF16), sds(D, BF16),
                   jax.ShapeDtypeStruct((8, 3 * D), F32), jax.ShapeDtypeStruct((8, D), F32)],
        compiler_params=_params(2),
    )(x1, dx2, mo, gn, mod, w1, w2)


def _matmul_tn(a, g, name, *, by_shard):
    R, Ka = a.shape
    Ng = g.shape[1]
    tr = next(c for c in (2304, 1024, 512, 256, 128, 64, 32, 16, 8) if R % c == 0)
    tka = Ka if Ka <= 1024 else 1024
    if by_shard:
        ws = Ng // N_DEV
        per = next(c for c in (8, 4, 2, 1) if c * ws <= 1152 or c == 1)
        tn = per * ws
    else:
        tn = next(c for c in (1152, 1024, 768, 512, 384, 256, 128) if Ng % c == 0)
    assert Ka % tka == 0 and tn % LANES == 0
    nr = R // tr

    def body(a_ref, g_ref, o_ref, acc_ref):
        r = pl.program_id(2)

        @pl.when(r == 0)
        def _():
            acc_ref[...] = jnp.zeros(acc_ref.shape, F32)

        acc_ref[...] += _dot_tn(a_ref[...], g_ref[...])

        @pl.when(r == nr - 1)
        def _():
            if by_shard:
                for d in range(per):
                    o_ref[d] = acc_ref[:, d * ws:(d + 1) * ws].astype(BF16)
            else:
                o_ref[...] = acc_ref[...].astype(BF16)

    if by_shard:
        out_spec = pl.BlockSpec((per, tka, ws), lambda i, j, r: (j, i, 0))
        out_shape = jax.ShapeDtypeStruct((N_DEV, Ka, ws), BF16)
    else:
        out_spec = pl.BlockSpec((tka, tn), lambda i, j, r: (i, j))
        out_shape = jax.ShapeDtypeStruct((Ka, Ng), BF16)
    return pl.pallas_call(
        body, name=name, grid=(Ka // tka, Ng // tn, nr),
        in_specs=[pl.BlockSpec((tr, tka), lambda i, j, r: (r, i)), pl.BlockSpec((tr, tn), lambda i, j, r: (r, j))],
        out_specs=out_spec, out_shape=out_shape, scratch_shapes=[pltpu.VMEM((tka, tn), F32)], compiler_params=_params(3),
    )(a, g)


def _merge_bwd(cfg, dx1, mgo, oa, ob, oc, z, mod, wa, wb, wc, wo, *, ctx_active, exch=None):
    B, T, D, tm, nC = cfg.B, cfg.T, cfg.D, cfg.tm, cfg.nC
    ng = 3 * D // cfg.gw

    def body(dx_ref, mgo_ref, oa_ref, ob_ref, oc_ref, *rest):
        gate_refs = rest[:ng]
        (mod_ref, wa_ref, wb_ref, wc_ref, wo_ref,
         doa_ref, dob_ref, doc_ref, dpa_ref, dpb_ref, dpc_ref, y_ref, dmo_ref, dzg_ref, dg1_ref) = rest[ng:]
        b, t = pl.program_id(0), pl.program_id(1)
        _acc_init([dg1_ref])

        def compute():
            row = jnp.where(t < nC, B, b)
            dx = dx_ref[0]
            dg1_ref[pl.ds(row, 1), :] += jnp.sum(dx * mgo_ref[0].astype(F32), axis=0, keepdims=True)
            dmo = (dx * _mod_row(mod_ref, row, 2, D)).astype(BF16)
            dmo_ref[0] = dmo
            dy = _dot_nt(dmo, wo_ref[...])
            gates = _read_gates(cfg, gate_refs)
            y = jnp.zeros((tm, D), F32)
            dgs = []
            for gate, o_ref, w_ref, do_ref, dp_ref in ((gates[0], oa_ref, wa_ref, doa_ref, dpa_ref),
                                                      (gates[1], ob_ref, wb_ref, dob_ref, dpb_ref),
                                                      (gates[2], oc_ref, wc_ref, doc_ref, dpc_ref)):
                s = jax.nn.sigmoid(gate)
                p = _dot(o_ref[0], w_ref[...])
                y = y + s * p
                dp = (dy * s).astype(BF16)
                dp_ref[0] = dp
                do_ref[0] = _dot_nt(dp, w_ref[...]).astype(BF16)
                dgs.append((dy * p * s * (1.0 - s)).astype(BF16))
            y_ref[0] = y.astype(BF16)
            dzg_ref[0] = jnp.concatenate(dgs, axis=-1)

        if ctx_active:
            compute()
        else:
            pl.when(t >= nC)(compute)

            @pl.when(t < nC)
            def _():
                for ref in (doa_ref, dob_ref, doc_ref, dpa_ref, dpb_ref, dpc_ref, y_ref, dmo_ref, dzg_ref):
                    ref[...] = jnp.zeros(ref.shape, ref.dtype)

    row = lambda w: pl.BlockSpec((1, tm, w), lambda b, t: (b, t, 0))
    sds = lambda w: jax.ShapeDtypeStruct((B, T, w), BF16)
    return _pcall(exch)(
        body, name="merge_bwd", grid=(B, cfg.nT),
        in_specs=[row(D), row(D), row(QW), row(POOL_W), row(QW)] + _gate_specs(cfg)
        + [_full((8, 6 * D)), _full((QW, D)), _full((POOL_W, D)), _full((QW, D)), _full((D, D))],
        out_specs=[row(QW), row(POOL_W), row(QW), row(D), row(D), row(D), row(D), row(D), row(3 * D), _full((8, D))],
        out_shape=[sds(QW), sds(POOL_W), sds(QW), sds(D), sds(D), sds(D), sds(D), sds(D), sds(3 * D),
                   jax.ShapeDtypeStruct((8, D), F32)],
        compiler_params=_params(2),
    )(dx1, mgo, oa, ob, oc, *([z] * ng), mod, wa, wb, wc, wo)


def _attn_bwd(cfg, l, q, k, z, vblock, sink8, do, lse, *, window, sink, ctx_attend, name, exch=None):
    B, S, N, T, tq, nC = cfg.B, cfg.S, cfg.N, cfg.T, cfg.tm, cfg.nC

    def body(q_ref, k_ref, v_ref, sink_ref, do_ref, lse_ref, dq_ref, dk_ref, dv_ref, dsink_ref):
        b, t = pl.program_id(0), pl.program_id(1)
        _acc_init([dsink_ref])

        @pl.when(t == 0)
        def _():
            dk_ref[...] = jnp.zeros(dk_ref.shape, F32)
            dv_ref[...] = jnp.zeros(dv_ref.shape, F32)

        def run(loc):
            q_t = q_ref[0]
            do_t = do_ref[0]
            dqs = [None] * N_QHEADS
            dks, dvs = [], []
            dsink_row = jnp.zeros((1, LANES), F32)
            lane = lax.broadcasted_iota(jnp.int32, (1, LANES), 1)
            lo = None
            for h in range(N_KV):
                lse = jnp.concatenate([lse_ref[0, GROUP * h + g:GROUP * h + g + 1, :] for g in range(GROUP)], axis=1)
                qs, kk, vv, p, _, p_s, lo, _ = _attn_scores(cfg, t - nC, q_t, k_ref, v_ref, sink_ref, h, loc, window and loc, sink, lse=lse)
                dos = jnp.concatenate([do_t[:, (GROUP * h + g) * HEAD:(GROUP * h + g + 1) * HEAD] for g in range(GROUP)], axis=0)
                dp = _dot_nt(vv, dos)
                delta = jnp.sum(p * dp, axis=0, keepdims=True)
                ds = (p * (dp - delta)).astype(BF16)
                dq = _dot_tn(kk, ds).T
                dks.append(_dot(ds, qs))
                dvs.append(_dot(p.astype(BF16), dos))
                if sink:
                    dsk = -p_s * delta
                    for g in range(GROUP):
                        tot = jnp.sum(dsk[:, g * tq:(g + 1) * tq], axis=1, keepdims=True)
                        dsink_row = dsink_row + jnp.where(lane == GROUP * h + g, tot, 0.0)
                for g in range(GROUP):
                    dqs[GROUP * h + g] = dq[g * tq:(g + 1) * tq] * QSCALE
            dq_ref[0] = jnp.concatenate(dqs, axis=-1)
            dk = jnp.concatenate(dks, axis=-1)
            dv = jnp.concatenate(dvs, axis=-1)
            if loc and not window:
                dk_ref[0] += dk
                dv_ref[0] += dv
            else:
                dk_ref[0, 0:N, :] += dk[0:N]
                dv_ref[0, 0:N, :] += dv[0:N]
                if loc:
                    W = tq + 2 * WINDOW
                    dk_ref[0, pl.ds(N + lo, W), :] += dk[N:]
                    dv_ref[0, pl.ds(N + lo, W), :] += dv[N:]
            if sink:
                dsink_ref[0:1, :] += dsink_row

        pl.when(t >= nC)(functools.partial(run, True))
        if ctx_attend:
            pl.when(t < nC)(functools.partial(run, False))
        else:
            @pl.when(t < nC)
            def _():
                dq_ref[0] = jnp.zeros((tq, QW), F32)

    kv = pl.BlockSpec((1, T, KVW), lambda b, t: (b, 0, 0))
    qrow = pl.BlockSpec((1, tq, QW), lambda b, t: (b, t, 0))
    return _pcall(exch)(
        body, name=name, grid=(B, cfg.nT),
        in_specs=[qrow, kv, pl.BlockSpec((1, T, KVW), lambda b, t: (b, 0, vblock)),
                  pl.BlockSpec((1, 8, LANES), lambda b, t: (l, 0, 0)), qrow, pl.BlockSpec((1, 8, tq), lambda b, t: (b, 0, t))],
        out_specs=[qrow, kv, kv, _full((8, LANES))],
        out_shape=[jax.ShapeDtypeStruct((B, T, QW), F32), jax.ShapeDtypeStruct((B, T, KVW), F32),
                   jax.ShapeDtypeStruct((B, T, KVW), F32), jax.ShapeDtypeStruct((8, LANES), F32)],
        compiler_params=_params(2),
    )(q, k, z, sink8, do, lse)


def _qknorm_bwd(cfg, l, z, gvec, cosf, sins, dqa, dka, dva, dqc, dkc, dvc):
    B, T, tm = cfg.B, cfg.T, cfg.tm

    def body(z_ref, g_ref, cos_ref, sin_ref, dqa_ref, dka_ref, dva_ref, dqc_ref, dkc_ref, dvc_ref, dz_ref, dg_ref):
        _acc_init([dg_ref])
        ind = _head_indicator()
        cos, sin = cos_ref[...], sin_ref[...]
        dqa_t, dqc_t = dqa_ref[0], dqc_ref[0]
        douts = {0: dqa_t[:, 0:128], 1: dqa_t[:, 128:256], 2: dqa_t[:, 256:384], 3: dka_ref[0],
                 5: dqc_t[:, 0:128], 6: dqc_t[:, 128:256], 7: dqc_t[:, 256:384], 8: dkc_ref[0]}
        pieces = []
        dgs = []
        for c in range(QKV_W // LANES):
            if c not in douts:
                pieces.append(dva_ref[0] if c == 4 else dvc_ref[0])
                dgs.append(jnp.zeros((1, LANES), F32))
                continue
            x = z_ref[0, :, c * LANES:(c + 1) * LANES]
            g = g_ref[0, :, c * LANES:(c + 1) * LANES]
            ss = _head_sum(x * x, ind)
            rstd = lax.rsqrt(ss * (1.0 / HEAD) + EPS)
            n = x * rstd
            dout = douts[c]
            dy = dout * cos + _pair_swap(dout * sin)
            dgs.append(jnp.sum(dy * n, axis=0, keepdims=True))
            dn = dy * g
            mean = _head_sum(dn * n, ind) * (1.0 / HEAD)
            pieces.append(rstd * (dn - n * mean))
        dz_ref[0] = jnp.concatenate(pieces, axis=-1).astype(BF16)
        dg_ref[0:1, :] += jnp.concatenate(dgs, axis=-1)

    row = lambda w: pl.BlockSpec((1, tm, w), lambda b, t: (b, t, 0))
    tab = pl.BlockSpec((tm, LANES), lambda b, t: (t, 0))
    return pl.pallas_call(
        body, name="qknorm_bwd", grid=(B, cfg.nT),
        in_specs=[row(QKV_W), _layer(l, QKV_W), tab, tab, row(QW), row(KVW), row(KVW), row(QW), row(KVW), row(KVW)],
        out_specs=[row(QKV_W), _full((8, QKV_W))],
        out_shape=[jax.ShapeDtypeStruct((B, T, QKV_W), BF16), jax.ShapeDtypeStruct((8, QKV_W), F32)],
        compiler_params=_params(2),
    )(z, gvec, cosf, sins, dqa, dka, dva, dqc, dkc, dvc)


def _pool_bwd(cfg, l, dob, pooled, wp, ps):
    B, T, tm, kw = cfg.B, cfg.T, cfg.tm, cfg.kw

    def body(dob_ref, pooled_ref, wp_ref, ps_ref, du_ref, dwp_ref, dps_ref):
        t = pl.program_id(1)
        _acc_init([dwp_ref, dps_ref])
        r0, seg_lo, seg_hi, k0 = _pool_geometry(cfg, t)
        ps = ps_ref[0]
        wp = wp_ref[0]
        dmix = dob_ref[0, pl.ds(r0, tm), :].astype(F32)
        pooled = pooled_ref[0]
        dps_ref[0:1, :] += jnp.sum(dmix * _dot(pooled, wp), axis=0, keepdims=True)
        dpm = (dmix * ps).astype(BF16)
        dwp_ref[...] += _dot_tn(pooled, dpm)
        dpooled_t = _dot_nt(dpm, wp)
        dpm_w = (dob_ref[0, pl.ds(k0, kw), :].astype(F32) * ps).astype(BF16)
        dpooled_w = _dot_nt(dpm_w, wp)
        rr = r0 + lax.broadcasted_iota(jnp.int32, (tm, kw), 0)
        cc = k0 + lax.broadcasted_iota(jnp.int32, (tm, kw), 1)
        diff = rr - cc
        inseg = (cc >= seg_lo) & (cc < seg_hi)
        ccol = k0 + lax.broadcasted_iota(jnp.int32, (kw, 1), 0)
        group = lax.broadcasted_iota(jnp.int32, (tm, POOL_W), 1) // HEAD
        acc = jnp.zeros((tm, POOL_W), F32)
        for gi, w in enumerate(POOL_WINDOWS):
            h = w // 2
            band_t = jnp.where((diff >= -h) & (diff <= h - 1) & inseg, 1.0, 0.0).astype(BF16)
            hi, lo = _split_bf16(dpooled_w / _pool_count(ccol, h, seg_lo, seg_hi))
            acc = jnp.where(group == gi, _dot(band_t, hi) + _dot(band_t, lo), acc)
        du_ref[0] = (acc - dpooled_t).astype(BF16)

    row = pl.BlockSpec((1, tm, POOL_W), lambda b, t: (b, t, 0))
    return pl.pallas_call(
        body, name="pool_bwd", grid=(B, cfg.nT),
        in_specs=[pl.BlockSpec((1, T, POOL_W), lambda b, t: (b, 0, 0)), row,
                  pl.BlockSpec((1, POOL_W, POOL_W), lambda b, t: (l, 0, 0)), _layer(l, POOL_W)],
        out_specs=[row, _full((POOL_W, POOL_W)), _full((8, POOL_W))],
        out_shape=[jax.ShapeDtypeStruct((B, T, POOL_W), BF16), jax.ShapeDtypeStruct((POOL_W, POOL_W), F32),
                   jax.ShapeDtypeStruct((8, POOL_W), F32)],
        compiler_params=_params(2),
    )(dob, pooled, wp, ps)


def _in_proj_bwd(cfg, l, dzq, du, dzg, w_in, x, dx1, gn, mod, *, latent_only, exch=None):
    B, S, T, D, IN, tm, nC = cfg.B, cfg.S, cfg.T, cfg.D, cfg.IN, cfg.tm, cfg.nC

    def body(dzq_ref, du_ref, dzg_ref, w_ref, x_ref, dx1_ref, gn_ref, mod_ref, dx0_ref, dz_ref, dmod_ref, dgn_ref):
        b, t = pl.program_id(0), pl.program_id(1)
        _acc_init([dmod_ref, dgn_ref])
        row = jnp.where(t < nC, B, b)
        dz = jnp.concatenate([dzq_ref[0], du_ref[0], dzg_ref[0]], axis=-1)
        dz_ref[0] = dz
        dh = _dot_nt(dz, w_ref[...])
        gn = gn_ref[0]
        scale = _mod_row(mod_ref, row, 1, D)
        _, xhat, rstd = _modulate(x_ref[0], gn, _mod_row(mod_ref, row, 0, D), scale)
        dxn, d_shift, d_scale, d_gn = _modulate_bwd(dh, xhat, rstd, gn, scale)
        dx0_ref[0] = dx1_ref[0] + dxn
        dmod_ref[pl.ds(row, 1), :] += jnp.concatenate([d_shift, d_scale], axis=-1)
        dgn_ref[0:1, :] += d_gn

    row = lambda w: pl.BlockSpec((1, tm, w), lambda b, t: (b, t, 0))
    if latent_only:
        dx0_spec = pl.BlockSpec((1, tm, D), lambda b, t: (b, jnp.maximum(t - nC, 0), 0))
        dx0_shape = jax.ShapeDtypeStruct((B, S, D), F32)
    else:
        dx0_spec, dx0_shape = row(D), jax.ShapeDtypeStruct((B, T, D), F32)
    return _pcall(exch)(
        body, name="in_proj_bwd", grid=(B, cfg.nT),
        in_specs=[row(QKV_W), row(POOL_W), row(3 * D), _full((D, IN)), row(D), row(D), _layer(l, D), _full((8, 6 * D))],
        out_specs=[dx0_spec, row(IN), _full((8, 2 * D)), _full((8, D))],
        out_shape=[dx0_shape, jax.ShapeDtypeStruct((B, T, IN), BF16),
                   jax.ShapeDtypeStruct((8, 2 * D), F32), jax.ShapeDtypeStruct((8, D), F32)],
        compiler_params=_params(2),
    )(dzq, du, dzg, w_in, x, dx1, gn, mod)


def _adaln_bwd(cfg, l, cc_all, dm_all, w_ada):
    d, B = cfg.D, cfg.B
    wa = w_ada.shape[2]

    def body(c_ref, dm_ref, w_ref, dw_ref, dc_ref):
        c = c_ref[...]
        s = jax.nn.sigmoid(c)
        dmb = dm_ref[...].astype(BF16)
        dw_ref[...] = _dot_tn((c * s).astype(BF16), dmb)
        dc = _dot_nt(dmb, w_ref[0].astype(BF16)) * (s * (1.0 + c * (1.0 - s)))
        is_ctx = lax.broadcasted_iota(jnp.int32, (8 * N_DEV, 1), 0) % 8 == B
        dc_ref[...] = jnp.broadcast_to(jnp.sum(jnp.where(is_ctx, dc, 0.0), axis=0, keepdims=True), (8, d))

    return pl.pallas_call(
        body, name="adaln_bwd", grid=(1,),
        in_specs=[_full((8 * N_DEV, d)), _full((8 * N_DEV, wa)), pl.BlockSpec((1, d, wa), lambda *_: (l, 0, 0))],
        out_specs=[_full((d, wa)), _full((8, d))],
        out_shape=[jax.ShapeDtypeStruct((d, wa), F32), jax.ShapeDtypeStruct((8, d), F32)],
        compiler_params=_params(1),
    )(cc_all, dm_all, w_ada)


def _dmod_pack(cfg, dmod_in, dg1, dmod_mlp):
    d = cfg.D
    wa = 6 * d // N_DEV

    def body(din_ref, dg1_ref, dmlp_ref, o_ref, db_ref):
        dm = jnp.concatenate([din_ref[...], dg1_ref[...], dmlp_ref[...]], axis=-1)
        for j in range(N_DEV):
            o_ref[j] = dm[:, j * wa:(j + 1) * wa]
        db_ref[...] = jnp.broadcast_to(jnp.sum(dm, axis=0, keepdims=True), (8, 6 * d))

    return pl.pallas_call(
        body, name="dmod_pack", grid=(1,),
        in_specs=[_full((8, 2 * d)), _full((8, d)), _full((8, 3 * d))],
        out_specs=[_full((N_DEV, 8, wa)), _full((8, 6 * d))],
        out_shape=[jax.ShapeDtypeStruct((N_DEV, 8, wa), F32), jax.ShapeDtypeStruct((8, 6 * d), F32)],
        compiler_params=_params(1),
    )(dmod_in, dg1, dmod_mlp)


def _adam_update(g, w, m, v):
    bc1 = 1.0 - ADAM_B1 ** ADAM_STEP
    bc2 = 1.0 - ADAM_B2 ** ADAM_STEP
    m2 = ADAM_B1 * m + (1.0 - ADAM_B1) * g
    v2 = ADAM_B2 * v + (1.0 - ADAM_B2) * (g * g)
    delta = -ADAM_LR * ((m2 / bc1) / (jnp.sqrt(v2 / bc2) + ADAM_EPS) + ADAM_WD * w)
    return delta, m2, v2


def _sum_parts(p_ref):
    g = p_ref[0].astype(F32)
    for d in range(1, p_ref.shape[0]):
        g = g + p_ref[d].astype(F32)
    return g


def _adamw_sharded(parts, w, m, v, name):
    L, K, W = w.shape
    P = parts[0].shape[0]
    tk = next(c for c in (256, 128, 64, 32, 16, 8) if K % c == 0)

    def body(*refs):
        p_refs = refs[:L]
        w_ref, m_ref, v_ref, g_ref, d_ref, m2_ref, v2_ref = refs[L:]
        layer = pl.program_id(0)

        def run(p_ref):
            g = _sum_parts(p_ref)
            delta, m2, v2 = _adam_update(g, w_ref[0], m_ref[0], v_ref[0])
            g_ref[0] = g
            d_ref[0] = delta
            m2_ref[0] = m2
            v2_ref[0] = v2

        for li in range(L):
            pl.when(layer == li)(functools.partial(run, p_refs[li]))

    blk = pl.BlockSpec((1, tk, W), lambda l, i: (l, i, 0))
    part_spec = lambda li: pl.BlockSpec((P, tk, W), lambda l, i: (0, jnp.where(l == li, i, 0), 0))
    return pl.pallas_call(
        body, name=name, grid=(L, K // tk),
        in_specs=[part_spec(li) for li in range(L)] + [blk, blk, blk],
        out_specs=[blk] * 4, out_shape=[jax.ShapeDtypeStruct((L, K, W), F32)] * 4,
        compiler_params=_params(2),
    )(*parts, w, m, v)


def _adamw_packed(parts, w, m, v, name):
    rows = w.shape[0]
    tr = next(c for c in (256, 128, 64, 32, 16, 8) if rows % c == 0)

    def body(p_ref, w_ref, m_ref, v_ref, g_ref, d_ref, m2_ref, v2_ref):
        g = _sum_parts(p_ref)
        delta, m2, v2 = _adam_update(g, w_ref[...], m_ref[...], v_ref[...])
        g_ref[...] = g
        d_ref[...] = delta
        m2_ref[...] = m2
        v2_ref[...] = v2

    blk = pl.BlockSpec((tr, PACK_W), lambda i: (i, 0))
    return pl.pallas_call(
        body, name=name, grid=(rows // tr,),
        in_specs=[pl.BlockSpec((N_DEV, tr, PACK_W), lambda i: (0, i, 0)), blk, blk, blk],
        out_specs=[blk] * 4, out_shape=[jax.ShapeDtypeStruct((rows, PACK_W), F32)] * 4,
        compiler_params=_params(1),
    )(parts, w, m, v)


_SHARDED = dict(w_ada=True, w_in=True, w_br_a=True, w_br_b=True, w_br_c=True, w_out=False, w_mlp1=True, w_mlp2=False)
_MERGE_WEIGHTS = ("w_br_a", "w_br_b", "w_br_c", "w_out")
_GATHERED = ("w_in",) + _MERGE_WEIGHTS + ("w_mlp1", "w_mlp2")
_KEEP_SHARDS = ("w_mlp1",)
_SMALL = ("c_ctx", "b_ada", "norm1", "norm2", "q_norm_a", "k_norm_a", "q_norm_c", "k_norm_c", "sink_c", "w_pool", "pool_scale")


def _from_shards(name, g):
    n, k, w = g.shape
    if name in _KEEP_SHARDS:
        return g
    if _SHARDED[name]:
        return g.transpose(1, 0, 2).reshape(k, n * w)
    return g.reshape(n * k, w)


def _to_shards(name, g):
    if g.ndim == 3:
        return g
    if _SHARDED[name]:
        k, nw = g.shape
        return g.reshape(k, N_DEV, nw // N_DEV).transpose(1, 0, 2)
    nk, w = g.shape
    return g.reshape(N_DEV, nk // N_DEV, w)


def _pack_small(vals):
    flat = jnp.concatenate([vals[n].reshape(-1) for n in _SMALL])
    rows = -(-flat.shape[0] // (8 * PACK_W)) * 8
    return jnp.pad(flat, (0, rows * PACK_W - flat.shape[0])).reshape(rows, PACK_W)


def _unpack_small(packed, like):
    flat, out, r = packed.reshape(-1), {}, 0
    for n in _SMALL:
        sz = like[n].size
        out[n] = flat[r:r + sz].reshape(like[n].shape)
        r += sz
    return out


def _rope_tables(cfg):
    pos = jnp.arange(cfg.S, dtype=F32)
    r = jnp.floor(pos / GRID_W)
    col = pos - r * GRID_W
    inv = 1.0 / (ROPE_THETA ** (jnp.arange(0, HEAD // 2, 2, dtype=F32) / (HEAD // 2)))
    ang = jnp.concatenate([r[:, None] * inv, col[:, None] * inv], axis=-1)
    cos = jnp.repeat(jnp.cos(ang), 2, axis=-1)
    sin = jnp.repeat(jnp.sin(ang), 2, axis=-1) * jnp.tile(jnp.array([-1.0, 1.0], F32), HEAD // 2)
    cos = jnp.concatenate([jnp.ones((cfg.N, HEAD), F32), cos], axis=0)
    sin = jnp.concatenate([jnp.zeros((cfg.N, HEAD), F32), sin], axis=0)
    return jnp.tile(cos, (1, 2)), jnp.tile(sin, (1, 2))


def _gvec(qa, ka, qc, kc):
    one = jnp.ones((qa.shape[0], KVW), F32)
    t = lambda a, n: jnp.tile(a, (1, n))
    return jnp.concatenate([t(qa, N_QHEADS), t(ka, N_KV), one, t(qc, N_QHEADS), t(kc, N_KV), one], axis=-1)[:, None, :]


def _block_diag(wp):
    L, g, c, _ = wp.shape
    eye = jnp.eye(g, dtype=wp.dtype)
    return (wp[:, :, :, None, :] * eye[None, :, None, :, None]).reshape(L, g * c, g * c)


def _pad8(a):
    return jnp.pad(a, ((0, 8 - a.shape[0]), (0, 0)))


def kernel(x, c, ctx, c_ctx, w_ada, b_ada, norm1, norm2, w_in, q_norm_a, k_norm_a, q_norm_c, k_norm_c, sink_c, w_pool, pool_scale, w_br_a, w_br_b, w_br_c, w_out, w_mlp1, w_mlp2, loss_target, m_c_ctx, m_w_ada, m_b_ada, m_norm1, m_norm2, m_w_in, m_q_norm_a, m_k_norm_a, m_q_norm_c, m_k_norm_c, m_sink_c, m_w_pool, m_pool_scale, m_w_br_a, m_w_br_b, m_w_br_c, m_w_out, m_w_mlp1, m_w_mlp2, v_c_ctx, v_w_ada, v_b_ada, v_norm1, v_norm2, v_w_in, v_q_norm_a, v_k_norm_a, v_q_norm_c, v_k_norm_c, v_sink_c, v_w_pool, v_pool_scale, v_w_br_a, v_w_br_b, v_w_br_c, v_w_out, v_w_mlp1, v_w_mlp2):
    B, S, D = x.shape
    N = ctx.shape[1]
    L = w_ada.shape[0]
    cfg = _Cfg(B, S, N, D)
    T = cfg.T
    weights = dict(c_ctx=c_ctx, w_ada=w_ada, b_ada=b_ada, norm1=norm1, norm2=norm2, w_in=w_in, q_norm_a=q_norm_a,
                   k_norm_a=k_norm_a, q_norm_c=q_norm_c, k_norm_c=k_norm_c, sink_c=sink_c, w_pool=w_pool,
                   pool_scale=pool_scale, w_br_a=w_br_a, w_br_b=w_br_b, w_br_c=w_br_c, w_out=w_out, w_mlp1=w_mlp1, w_mlp2=w_mlp2)
    mom_m = dict(c_ctx=m_c_ctx, w_ada=m_w_ada, b_ada=m_b_ada, norm1=m_norm1, norm2=m_norm2, w_in=m_w_in, q_norm_a=m_q_norm_a,
                 k_norm_a=m_k_norm_a, q_norm_c=m_q_norm_c, k_norm_c=m_k_norm_c, sink_c=m_sink_c, w_pool=m_w_pool,
                 pool_scale=m_pool_scale, w_br_a=m_w_br_a, w_br_b=m_w_br_b, w_br_c=m_w_br_c, w_out=m_w_out, w_mlp1=m_w_mlp1, w_mlp2=m_w_mlp2)
    mom_v = dict(c_ctx=v_c_ctx, w_ada=v_w_ada, b_ada=v_b_ada, norm1=v_norm1, norm2=v_norm2, w_in=v_w_in, q_norm_a=v_q_norm_a,
                 k_norm_a=v_k_norm_a, q_norm_c=v_q_norm_c, k_norm_c=v_k_norm_c, sink_c=v_sink_c, w_pool=v_w_pool,
                 pool_scale=v_pool_scale, w_br_a=v_w_br_a, w_br_b=v_w_br_b, w_br_c=v_w_br_c, w_out=v_w_out, w_mlp1=v_w_mlp1, w_mlp2=v_w_mlp2)

    shards_bf16 = {n: weights[n].astype(BF16) for n in _GATHERED}
    full = [dict() for _ in range(L)]

    def gather_of(items):
        return _Exchange([(shards_bf16[n], l) for l, n in items], scatter=False)

    def gathered(items, arrs):
        for (l, n), a in zip(items, arrs):
            full[l][n] = _from_shards(n, a)

    gathered([(0, "w_in")], gather_of([(0, "w_in")]).alone("gather_first_weights"))

    def hosting(fn, *a, exch=None, done=None, **kw):
        if exch is None:
            return fn(*a, **kw)
        res = fn(*a, exch=exch, **kw)
        done(res[-exch.n:])
        own = res[:-exch.n]
        return own[0] if len(own) == 1 else own

    def gather_behind(l, names):
        if l >= L:
            return {}
        items = [(l, n) for n in names]
        return dict(exch=gather_of(items), done=functools.partial(gathered, items))

    cosf, sins = _rope_tables(cfg)
    xs = jnp.concatenate([ctx, x], axis=1)
    cc8 = _pad8(jnp.concatenate([c, c_ctx[None, :]], axis=0))
    va_blk, vc_blk = (QW + KVW) // KVW, (2 * QW + 3 * KVW) // KVW
    per_layer = lambda a: a[:, None, :]
    b_ada3, norm1_3, norm2_3, ps3 = per_layer(b_ada), per_layer(norm1), per_layer(norm2), per_layer(pool_scale)
    gvec = _gvec(q_norm_a, k_norm_a, q_norm_c, k_norm_c)
    sink8 = jnp.pad(sink_c[:, None, :], ((0, 0), (0, 7), (0, LANES - N_QHEADS)))
    wp = _block_diag(w_pool).astype(BF16)

    cc_all = _Exchange([cc8], scatter=False).alone("gather_cond")[0].reshape(8 * N_DEV, D)
    mod_cols = _Exchange([_adaln_fwd(cfg, cc_all, w_ada)], scatter=True).alone("scatter_mod")[0]
    mod_all = _adaln_join(cfg, mod_cols, b_ada3)

    saved = []
    for l in range(L):
        fw = full[l]
        ctx_active = l < L - 1
        mod = mod_all[l]
        z, h = hosting(_in_proj_fwd, cfg, l, xs, norm1_3, mod, fw["w_in"], **gather_behind(l, _MERGE_WEIGHTS if l == 0 else ("w_mlp2",)))
        qa, ka, qc, kc = _qknorm_fwd(cfg, l, z, gvec, cosf, sins)
        oa, lse_a = hosting(_attn_fwd, cfg, l, qa, ka, z, va_blk, sink8, window=False, sink=False, ctx_attend=ctx_active, name="attn_a_fwd",
                            **gather_behind(l, ("w_mlp1",)))
        oc, lse_c = hosting(_attn_fwd, cfg, l, qc, kc, z, vc_blk, sink8, window=True, sink=True, ctx_attend=ctx_active, name="attn_c_fwd",
                            **(gather_behind(0, ("w_mlp2",)) if l == 0 else {}))
        ob, pooled = _pool_fwd(cfg, l, z, wp, ps3)
        x1, mgo = hosting(_merge_fwd, cfg, xs, oa, ob, oc, z, mod, fw["w_br_a"], fw["w_br_b"], fw["w_br_c"], fw["w_out"],
                          ctx_active=ctx_active, **gather_behind(l + 1, _MERGE_WEIGHTS))
        if l < L - 1:
            x2, mo = hosting(_mlp_fwd, cfg, l, x1, norm2_3, mod, fw["w_mlp1"], fw["w_mlp2"], ctx_active=ctx_active,
                             **gather_behind(l + 1, ("w_in",)))
        else:
            x2, mo, sse = _mlp_fwd(cfg, l, x1, norm2_3, mod, fw["w_mlp1"], fw["w_mlp2"], ctx_active=ctx_active, target=loss_target)
        saved.append(dict(xs=xs, mod=mod, z=z, h=h, qa=qa, ka=ka, qc=qc, kc=kc, oa=oa, oc=oc, ob=ob, pooled=pooled, x1=x1, mgo=mgo, mo=mo,
                          lse_a=lse_a, lse_c=lse_c))
        xs = x2

    dxs = xs
    loss = lax.psum(0.5 * sse[0, 0] / D, ("x", "y", "c"))

    grads = [dict() for _ in range(L)]
    parts = {}
    small = {n: [None] * L for n in _SMALL if n != "c_ctx"}
    d_c_ctx = jnp.zeros((D,), F32)
    flat2 = lambda a: a.reshape(B * T, a.shape[-1])

    def scatter_of(l, names):
        return _Exchange([_to_shards(n, grads[l][n]) for n in names], scatter=True)

    def scattered(l, names, arrs):
        for n, a in zip(names, arrs):
            parts[(l, n)] = a

    def scatter_behind(l, names):
        if l >= L:
            return {}
        return dict(exch=scatter_of(l, names), done=functools.partial(scattered, l, names))

    for l in reversed(range(L)):
        fw, sv, g = full[l], saved[l], grads[l]
        ctx_active = l < L - 1
        mod = sv["mod"]
        dx1, h2, r, da, dout, dmod_mlp, dgn2 = hosting(_mlp_bwd, cfg, l, sv["x1"], dxs, sv["mo"], norm2_3, mod, fw["w_mlp1"], fw["w_mlp2"],
                                                       ctx_active=ctx_active, **scatter_behind(l + 1, ("w_in",)))
        g["w_mlp1"] = _matmul_tn(flat2(h2), flat2(da), "dw_mlp1", by_shard=True)
        g["w_mlp2"] = _matmul_tn(flat2(r), flat2(dout), "dw_mlp2", by_shard=False)
        doa, dob, doc, dpa, dpb, dpc, y, dmo, dzg, dg1 = _merge_bwd(
            cfg, dx1, sv["mgo"], sv["oa"], sv["ob"], sv["oc"], sv["z"], mod, fw["w_br_a"], fw["w_br_b"], fw["w_br_c"], fw["w_out"],
            ctx_active=ctx_active)
        g["w_out"] = _matmul_tn(flat2(y), flat2(dmo), "dw_out", by_shard=False)
        g["w_br_a"] = _matmul_tn(flat2(sv["oa"]), flat2(dpa), "dw_br_a", by_shard=True)
        g["w_br_b"] = _matmul_tn(flat2(sv["ob"]), flat2(dpb), "dw_br_b", by_shard=True)
        g["w_br_c"] = _matmul_tn(flat2(sv["oc"]), flat2(dpc), "dw_br_c", by_shard=True)
        z = sv["z"]
        dqa, dka, dva, _ = hosting(_attn_bwd, cfg, l, sv["qa"], sv["ka"], z, va_blk, sink8, doa, sv["lse_a"], window=False, sink=False,
                                   ctx_attend=ctx_active, name="attn_a_bwd", **scatter_behind(l, ("w_mlp1", "w_mlp2")))
        dqc, dkc, dvc, dsink = hosting(_attn_bwd, cfg, l, sv["qc"], sv["kc"], z, vc_blk, sink8, doc, sv["lse_c"], window=True, sink=True,
                                       ctx_attend=ctx_active, name="attn_c_bwd", **scatter_behind(l, _MERGE_WEIGHTS))
        dzq, dgvec = _qknorm_bwd(cfg, l, z, gvec, cosf, sins, dqa, dka, dva, dqc, dkc, dvc)
        du, dwp, dps = _pool_bwd(cfg, l, dob, sv["pooled"], wp, ps3)
        dxs, dz, dmod_in, dgn1 = _in_proj_bwd(cfg, l, dzq, du, dzg, fw["w_in"], sv["xs"], dx1, norm1_3, mod, latent_only=(l == 0))
        g["w_in"] = _matmul_tn(flat2(sv["h"]), flat2(dz), "dw_in", by_shard=False)
        dmod_cols, dbias = _dmod_pack(cfg, dmod_in, dg1, dmod_mlp)
        dm_all = _Exchange([dmod_cols], scatter=True).alone("scatter_dmod")[0].reshape(8 * N_DEV, -1)
        g["w_ada"], dcc = _adaln_bwd(cfg, l, cc_all, dm_all, w_ada)
        d_c_ctx = d_c_ctx + dcc[0]
        gv = dgvec[0]
        heads = lambda v, n: v.reshape(n, HEAD).sum(axis=0)
        small["b_ada"][l] = dbias[0]
        small["norm1"][l] = dgn1[0]
        small["norm2"][l] = dgn2[0]
        small["q_norm_a"][l] = heads(gv[0:QW], N_QHEADS)
        small["k_norm_a"][l] = heads(gv[QW:QW + KVW], N_KV)
        small["q_norm_c"][l] = heads(gv[QW + 2 * KVW:2 * QW + 2 * KVW], N_QHEADS)
        small["k_norm_c"][l] = heads(gv[2 * QW + 2 * KVW:2 * QW + 3 * KVW], N_KV)
        small["sink_c"][l] = dsink[0, :N_QHEADS]
        small["w_pool"][l] = jnp.stack([dwp[i * HEAD:(i + 1) * HEAD, i * HEAD:(i + 1) * HEAD] for i in range(len(POOL_WINDOWS))])
        small["pool_scale"][l] = dps[0]
    grad_x = dxs

    scattered(0, ("w_in",), scatter_of(0, ("w_in",)).alone("scatter_last_grads"))
    for l in range(L):
        parts[(l, "w_ada")] = grads[l]["w_ada"][None]
    stepped = {n: _adamw_sharded([parts[(l, n)] for l in range(L)], weights[n], mom_m[n], mom_v[n], "adamw_" + n) for n in _SHARDED}

    small_vals = {n: jnp.stack(v) for n, v in small.items()}
    small_vals["c_ctx"] = d_c_ctx
    small_parts = _Exchange([_pack_small(small_vals)], scatter=False).alone("gather_small_grads")[0]
    stepped_small = _adamw_packed(small_parts, _pack_small(weights), _pack_small(mom_m), _pack_small(mom_v), "adamw_small")

    outs = []
    for i in range(4):
        res = {n: stepped[n][i] for n in _SHARDED}
        res.update(_unpack_small(stepped_small[i], weights))
        outs.append(res)
    order = ("c_ctx", "w_ada", "b_ada", "norm1", "norm2", "w_in", "q_norm_a", "k_norm_a", "q_norm_c", "k_norm_c", "sink_c",
             "w_pool", "pool_scale", "w_br_a", "w_br_b", "w_br_c", "w_out", "w_mlp1", "w_mlp2")
    return (loss, grad_x, *[res[n] for res in outs for n in order])
```

```python
import functools

import jax
import jax.numpy as jnp
from jax import lax
from jax.experimental import pallas as pl
from jax.experimental.pallas import tpu as pltpu

F32 = jnp.float32
BF16 = jnp.bfloat16
HIGHEST = lax.Precision.HIGHEST

N_DEV = 8
HEAD = 64
N_QHEADS = 6
N_KV = 2
GROUP = 3
QW = N_QHEADS * HEAD
KVW = N_KV * HEAD
QKV_W = 2 * (QW + 2 * KVW)
POOL_W = 256
POOL_WINDOWS = (2, 4, 8, 16)
GATE0 = QKV_W + POOL_W
WINDOW = 128
GRID_W = 64
ROPE_THETA = 10000.0
EPS = 1e-6
NEG = -1e30
QSCALE = HEAD ** -0.5
LANES = 128
PACK_W = 1024
VMEM_LIMIT = 56 * 1024 * 1024

ADAM_LR = 0.001
ADAM_B1 = 0.9
ADAM_B2 = 0.999
ADAM_EPS = 1e-08
ADAM_WD = 0.01
ADAM_STEP = 10

NT_DIMS = (((1,), (1,)), ((), ()))
TN_DIMS = (((0,), (0,)), ((), ()))


def _dot(a, b):
    return jnp.dot(a, b, preferred_element_type=F32)


def _dot_nt(a, b):
    return lax.dot_general(a, b, NT_DIMS, preferred_element_type=F32)


def _dot_tn(a, b):
    return lax.dot_general(a, b, TN_DIMS, preferred_element_type=F32)


def _params(n_grid):
    return pltpu.CompilerParams(dimension_semantics=("arbitrary",) * n_grid, vmem_limit_bytes=VMEM_LIMIT)


def _full(shape):
    nd = len(shape)
    return pl.BlockSpec(shape, lambda *_: (0,) * nd)


def _layer(l, width):
    return pl.BlockSpec((1, 1, width), lambda *_: (l, 0, 0))


def _modulate(x, gn, shift, scale):
    rstd = lax.rsqrt(jnp.mean(x * x, axis=-1, keepdims=True) + EPS)
    xhat = x * rstd
    return xhat * gn * (1.0 + scale) + shift, xhat, rstd


def _modulate_bwd(dh, xhat, rstd, gn, scale):
    d_shift = jnp.sum(dh, axis=0, keepdims=True)
    d_scale = jnp.sum(dh * xhat * gn, axis=0, keepdims=True)
    dy = dh * (1.0 + scale)
    d_gn = jnp.sum(dy * xhat, axis=0, keepdims=True)
    dxh = dy * gn
    dx = rstd * (dxh - xhat * jnp.mean(dxh * xhat, axis=-1, keepdims=True))
    return dx, d_shift, d_scale, d_gn


def _mod_row(mod_ref, row, k, d):
    return mod_ref[pl.ds(row, 1), k * d:(k + 1) * d]


class _Cfg:
    def __init__(self, b, s, n, d):
        self.B, self.S, self.N, self.D = b, s, n, d
        self.T = n + s
        self.F = 4 * d
        self.IN = GATE0 + 3 * d
        self.tm = 256 if (n % 256 == 0 and s % 256 == 0) else 128
        self.nT = self.T // self.tm
        self.nC = n // self.tm
        self.gw = 512 if d % 512 == 0 else 256
        self.kw = self.tm + 2 * LANES
        assert GATE0 % self.gw == 0 and d % self.gw == 0 and b < 8 and self.T >= self.kw and max(POOL_WINDOWS) // 2 <= LANES
        assert s % GRID_W == 0 and n % self.tm == 0 and s % self.tm == 0 and s >= self.tm + 2 * WINDOW
        assert d % (N_DEV * LANES) == 0


def _peer(k):
    x, y, c = lax.axis_index("x"), lax.axis_index("y"), lax.axis_index("c")
    px = x ^ ((k >> 2) & 1)
    py = y ^ ((k >> 1) & 1)
    pc = c ^ (k & 1)
    return (px, py, pc), 4 * px + 2 * py + pc


class _Exchange:
    def __init__(self, arrays, scatter):
        self.arrays = [a if isinstance(a, tuple) else (a, None) for a in arrays]
        self.scatter = scatter
        self.n = len(self.arrays)

    def operands(self):
        return [a for a, _ in self.arrays]

    def out_shapes(self):
        res = []
        for a, layer in self.arrays:
            shape = a.shape[1:] if (self.scatter or layer is not None) else a.shape
            res.append(jax.ShapeDtypeStruct((N_DEV,) + tuple(shape), a.dtype))
        return res

    def scratch(self):
        n = self.n * (N_DEV - 1)
        return [pltpu.SemaphoreType.DMA((n,)), pltpu.SemaphoreType.DMA((n,)), pltpu.SemaphoreType.DMA((self.n,))]

    def _copies(self, x_refs, out_refs, send_sems, recv_sems, local_sems, want):
        _, me = _peer(0)
        res = []
        for i, ((_, layer), x_ref, out_ref) in enumerate(zip(self.arrays, x_refs, out_refs)):
            if self.scatter:
                src_of = lambda d, x_ref=x_ref: x_ref.at[d]
            elif layer is not None:
                src_of = lambda d, x_ref=x_ref, layer=layer: x_ref.at[layer]
            else:
                src_of = lambda d, x_ref=x_ref: x_ref
            if want == "local":
                res.append(pltpu.make_async_copy(src_of(me), out_ref.at[me], local_sems.at[i]))
                continue
            for k in range(1, N_DEV):
                pos, idx = _peer(k)
                j = i * (N_DEV - 1) + k - 1
                common = dict(send_sem=send_sems.at[j], recv_sem=recv_sems.at[j], device_id=pos, device_id_type=pl.DeviceIdType.MESH)
                if want == "send":
                    res.append(pltpu.make_async_remote_copy(src_ref=src_of(idx), dst_ref=out_ref.at[me], **common))
                else:
                    res.append(pltpu.make_async_remote_copy(src_ref=src_of(me), dst_ref=out_ref.at[idx], **common))
        return res

    def start(self, *refs):
        for cp in self._copies(*refs, "local") + self._copies(*refs, "send"):
            cp.start()

    def wait(self, *refs):
        for cp in self._copies(*refs, "recv"):
            cp.wait_recv()
        for cp in self._copies(*refs, "send"):
            cp.wait_send()
        for cp in self._copies(*refs, "local"):
            cp.wait()

    def alone(self, name):
        n = self.n

        def body(*refs):
            args = (refs[:n], refs[n:2 * n], *refs[2 * n:])
            self.start(*args)
            self.wait(*args)

        any_spec = pl.BlockSpec(memory_space=pl.ANY)
        return pl.pallas_call(body, name=name, in_specs=[any_spec] * n, out_specs=[any_spec] * n,
                              out_shape=self.out_shapes(), scratch_shapes=self.scratch())(*self.operands())


def _pcall(exch):
    if exch is None:
        return pl.pallas_call

    def make(body, *, name, grid, in_specs, out_specs, out_shape, compiler_params, scratch_shapes=()):
        multi = isinstance(out_shape, (list, tuple))
        out_specs_l = list(out_specs) if multi else [out_specs]
        out_shape_l = list(out_shape) if multi else [out_shape]
        n_in, n_out, n_x, n_s = len(in_specs), len(out_specs_l), exch.n, len(scratch_shapes)

        def hosted(*refs):
            ins, x_refs = refs[:n_in], refs[n_in:n_in + n_x]
            o0 = n_in + n_x
            outs, xo_refs = refs[o0:o0 + n_out], refs[o0 + n_out:o0 + n_out + n_x]
            s0 = o0 + n_out + n_x
            own_scratch, sems = refs[s0:s0 + n_s], refs[s0 + n_s:]
            ids = [pl.program_id(i) for i in range(len(grid))]
            first = functools.reduce(jnp.logical_and, [i == 0 for i in ids])
            last = functools.reduce(jnp.logical_and, [i == g - 1 for i, g in zip(ids, grid)])

            @pl.when(first)
            def _():
                exch.start(x_refs, xo_refs, *sems)

            body(*ins, *outs, *own_scratch)

            @pl.when(last)
            def _():
                exch.wait(x_refs, xo_refs, *sems)

        any_spec = pl.BlockSpec(memory_space=pl.ANY)
        call = pl.pallas_call(
            hosted, name=name, grid=grid, in_specs=list(in_specs) + [any_spec] * n_x, out_specs=out_specs_l + [any_spec] * n_x,
            out_shape=out_shape_l + exch.out_shapes(), scratch_shapes=list(scratch_shapes) + exch.scratch(),
            compiler_params=compiler_params)
        return lambda *args: call(*args, *exch.operands())

    return make


def _adaln_fwd(cfg, cc_all, w_ada):
    d = cfg.D
    L, _, wa = w_ada.shape

    def body(c_ref, w_ref, o_ref):
        c = c_ref[...]
        a = (c * jax.nn.sigmoid(c)).astype(BF16)
        for l in range(L):
            m = _dot(a, w_ref[l].astype(BF16))
            for p in range(N_DEV):
                o_ref[p, l] = m[8 * p:8 * (p + 1)]

    return pl.pallas_call(
        body, name="adaln_fwd", grid=(1,),
        in_specs=[_full((8 * N_DEV, d)), _full((L, d, wa))],
        out_specs=_full((N_DEV, L, 8, wa)),
        out_shape=jax.ShapeDtypeStruct((N_DEV, L, 8, wa), F32), compiler_params=_params(1),
    )(cc_all, w_ada)


def _adaln_join(cfg, parts, b_ada):
    d = cfg.D
    _, L, _, wa = parts.shape

    def body(p_ref, b_ref, o_ref):
        for l in range(L):
            for j in range(N_DEV):
                o_ref[l, :, j * wa:(j + 1) * wa] = p_ref[j, l] + b_ref[l, :, j * wa:(j + 1) * wa]

    return pl.pallas_call(
        body, name="adaln_join", grid=(1,),
        in_specs=[_full((N_DEV, L, 8, wa)), _full((L, 1, 6 * d))],
        out_specs=_full((L, 8, 6 * d)),
        out_shape=jax.ShapeDtypeStruct((L, 8, 6 * d), F32), compiler_params=_params(1),
    )(parts, b_ada)


def _in_proj_fwd(cfg, l, x, gn, mod, w_in, exch=None):
    B, T, D, IN, tm, nC = cfg.B, cfg.T, cfg.D, cfg.IN, cfg.tm, cfg.nC

    def body(x_ref, gn_ref, mod_ref, w_ref, z_ref, h_ref):
        b, t = pl.program_id(0), pl.program_id(1)
        row = jnp.where(t < nC, B, b)
        h, _, _ = _modulate(x_ref[0], gn_ref[0], _mod_row(mod_ref, row, 0, D), _mod_row(mod_ref, row, 1, D))
        hb = h.astype(BF16)
        h_ref[0] = hb
        z_ref[0] = _dot(hb, w_ref[...])

    return _pcall(exch)(
        body, name="in_proj_fwd", grid=(B, cfg.nT),
        in_specs=[pl.BlockSpec((1, tm, D), lambda b, t: (b, t, 0)), _layer(l, D), _full((8, 6 * D)), _full((D, IN))],
        out_specs=[pl.BlockSpec((1, tm, IN), lambda b, t: (b, t, 0)), pl.BlockSpec((1, tm, D), lambda b, t: (b, t, 0))],
        out_shape=[jax.ShapeDtypeStruct((B, T, IN), F32), jax.ShapeDtypeStruct((B, T, D), BF16)],
        compiler_params=_params(2),
    )(x, gn, mod, w_in)


def _head_indicator():
    r = lax.broadcasted_iota(jnp.int32, (LANES, LANES), 0) // HEAD
    c = lax.broadcasted_iota(jnp.int32, (LANES, LANES), 1) // HEAD
    return jnp.where(r == c, 1.0, 0.0).astype(BF16)


def _head_sum(x, ind):
    hi = x.astype(BF16)
    lo = (x - hi.astype(F32)).astype(BF16)
    return _dot(hi, ind) + _dot(lo, ind)


def _pair_swap(y):
    lane = lax.broadcasted_iota(jnp.int32, y.shape, 1)
    return jnp.where(lane % 2 == 0, pltpu.roll(y, LANES - 1, 1), pltpu.roll(y, 1, 1))


_QK_CHUNKS = (0, 1, 2, 3, 5, 6, 7, 8)
_Q_CHUNKS = (0, 1, 2, 5, 6, 7)


def _qknorm_fwd(cfg, l, z, gvec, cosf, sins):
    B, T, tm = cfg.B, cfg.T, cfg.tm

    def body(z_ref, g_ref, cos_ref, sin_ref, qa_ref, ka_ref, qc_ref, kc_ref):
        ind = _head_indicator()
        cos, sin = cos_ref[...], sin_ref[...]

        def chunk(c):
            x = z_ref[0, :, c * LANES:(c + 1) * LANES]
            ss = _head_sum(x * x, ind)
            y = x * lax.rsqrt(ss * (1.0 / HEAD) + EPS) * g_ref[0, :, c * LANES:(c + 1) * LANES]
            out = y * cos + _pair_swap(y) * sin
            return (out * QSCALE if c in _Q_CHUNKS else out).astype(BF16)

        qa_ref[0] = jnp.concatenate([chunk(0), chunk(1), chunk(2)], axis=-1)
        ka_ref[0] = chunk(3)
        qc_ref[0] = jnp.concatenate([chunk(5), chunk(6), chunk(7)], axis=-1)
        kc_ref[0] = chunk(8)

    row = lambda w: pl.BlockSpec((1, tm, w), lambda b, t: (b, t, 0))
    tab = pl.BlockSpec((tm, LANES), lambda b, t: (t, 0))
    return pl.pallas_call(
        body, name="qknorm_fwd", grid=(B, cfg.nT),
        in_specs=[row(QKV_W), _layer(l, QKV_W), tab, tab],
        out_specs=[row(QW), row(KVW), row(QW), row(KVW)],
        out_shape=[jax.ShapeDtypeStruct((B, T, w), BF16) for w in (QW, KVW, QW, KVW)],
        compiler_params=_params(2),
    )(z, gvec, cosf, sins)


def _attn_scores(cfg, tl, q, k_ref, v_ref, sink_ref, h, loc, window, sink, lse=None):
    S, N, tq = cfg.S, cfg.N, cfg.tm
    hs = slice(h * HEAD, (h + 1) * HEAD)
    qs = jnp.concatenate([q[:, (GROUP * h + g) * HEAD:(GROUP * h + g + 1) * HEAD] for g in range(GROUP)], axis=0)
    lo = None
    if not loc:
        kk = k_ref[0, 0:N, :][:, hs]
        vv = v_ref[0, 0:N, :].astype(BF16)[:, hs]
    elif not window:
        kk = k_ref[0][:, hs]
        vv = v_ref[0].astype(BF16)[:, hs]
    else:
        W = tq + 2 * WINDOW
        lo = pl.multiple_of(jnp.clip(tl * tq - WINDOW, 0, S - W), LANES)
        kk = jnp.concatenate([k_ref[0, 0:N, :], k_ref[0, pl.ds(N + lo, W), :]], axis=0)[:, hs]
        vv = jnp.concatenate([v_ref[0, 0:N, :], v_ref[0, pl.ds(N + lo, W), :]], axis=0).astype(BF16)[:, hs]
    st = _dot_nt(kk, qs)
    if window:
        krow = lax.broadcasted_iota(jnp.int32, st.shape, 0)
        qpos = tl * tq + lax.broadcasted_iota(jnp.int32, st.shape, 1) % tq
        st = jnp.where((krow < N) | (jnp.abs(qpos - (lo + krow - N)) <= WINDOW), st, NEG)
    sk = None
    if sink:
        colg = lax.broadcasted_iota(jnp.int32, (1, GROUP * tq), 1) // tq
        sk = jnp.zeros((1, GROUP * tq), F32)
        for g in range(GROUP):
            j = GROUP * h + g
            sk = jnp.where(colg == g, sink_ref[0, 0:1, j:j + 1], sk)
    if lse is not None:
        return qs, kk, vv, jnp.exp(st - lse), None, (jnp.exp(sk - lse) if sink else None), lo, lse
    m = jnp.max(st, axis=0, keepdims=True)
    if sink:
        m = jnp.maximum(m, sk)
    e = jnp.exp(st - m)
    l = jnp.sum(e, axis=0, keepdims=True)
    e_s = None
    if sink:
        e_s = jnp.exp(sk - m)
        l = l + e_s
    return qs, kk, vv, e, 1.0 / l, e_s, lo, m + jnp.log(l)


def _attn_fwd(cfg, l, q, k, z, vblock, sink8, *, window, sink, ctx_attend, name, exch=None):
    B, T, tq, nC = cfg.B, cfg.T, cfg.tm, cfg.nC

    def body(q_ref, k_ref, v_ref, sink_ref, o_ref, lse_ref):
        t = pl.program_id(1)

        def run(loc):
            q_t = q_ref[0]
            outs = [None] * N_QHEADS
            lses = [None] * N_QHEADS
            for h in range(N_KV):
                _, _, vv, e, inv, _, _, lse = _attn_scores(cfg, t - nC, q_t, k_ref, v_ref, sink_ref, h, loc, window and loc, sink)
                o = (_dot_tn(vv, e.astype(BF16)) * inv).T
                for g in range(GROUP):
                    outs[GROUP * h + g] = o[g * tq:(g + 1) * tq]
                    lses[GROUP * h + g] = lse[:, g * tq:(g + 1) * tq]
            o_ref[0] = jnp.concatenate(outs, axis=-1).astype(BF16)
            lse_ref[0] = jnp.concatenate(lses + [jnp.zeros((8 - N_QHEADS, tq), F32)], axis=0)

        pl.when(t >= nC)(functools.partial(run, True))
        if ctx_attend:
            pl.when(t < nC)(functools.partial(run, False))
        else:
            @pl.when(t < nC)
            def _():
                o_ref[0] = jnp.zeros((tq, QW), BF16)
                lse_ref[0] = jnp.zeros((8, tq), F32)

    return _pcall(exch)(
        body, name=name, grid=(B, cfg.nT),
        in_specs=[pl.BlockSpec((1, tq, QW), lambda b, t: (b, t, 0)),
                  pl.BlockSpec((1, T, KVW), lambda b, t: (b, 0, 0)),
                  pl.BlockSpec((1, T, KVW), lambda b, t: (b, 0, vblock)),
                  pl.BlockSpec((1, 8, LANES), lambda b, t: (l, 0, 0))],
        out_specs=[pl.BlockSpec((1, tq, QW), lambda b, t: (b, t, 0)), pl.BlockSpec((1, 8, tq), lambda b, t: (b, 0, t))],
        out_shape=[jax.ShapeDtypeStruct((B, T, QW), BF16), jax.ShapeDtypeStruct((B, 8, T), F32)], compiler_params=_params(2),
    )(q, k, z, sink8)


def _pool_geometry(cfg, t):
    tm, N, T, nC = cfg.tm, cfg.N, cfg.T, cfg.nC
    r0 = pl.multiple_of(t * tm, tm)
    isctx = t < nC
    seg_lo = jnp.where(isctx, 0, N)
    seg_hi = jnp.where(isctx, N, T)
    k0 = pl.multiple_of(jnp.clip(t * tm - LANES, 0, T - cfg.kw), LANES)
    return r0, seg_lo, seg_hi, k0


def _pool_count(pos, h, seg_lo, seg_hi):
    return jnp.maximum(jnp.minimum(pos + h, seg_hi) - jnp.maximum(pos - h, seg_lo), 1).astype(F32)


def _split_bf16(x):
    hi = x.astype(BF16)
    return hi, (x - hi.astype(F32)).astype(BF16)


def _pool_fwd(cfg, l, z, wp, ps):
    B, T, tm, kw = cfg.B, cfg.T, cfg.tm, cfg.kw

    def body(u_ref, wp_ref, ps_ref, ob_ref, pooled_ref):
        t = pl.program_id(1)
        r0, seg_lo, seg_hi, k0 = _pool_geometry(cfg, t)
        hi, lo = _split_bf16(u_ref[0, pl.ds(k0, kw), :])
        rr = r0 + lax.broadcasted_iota(jnp.int32, (tm, kw), 0)
        cc = k0 + lax.broadcasted_iota(jnp.int32, (tm, kw), 1)
        diff = cc - rr
        inseg = (cc >= seg_lo) & (cc < seg_hi)
        rcol = r0 + lax.broadcasted_iota(jnp.int32, (tm, 1), 0)
        group = lax.broadcasted_iota(jnp.int32, (tm, POOL_W), 1) // HEAD
        acc = jnp.zeros((tm, POOL_W), F32)
        for gi, w in enumerate(POOL_WINDOWS):
            h = w // 2
            band = jnp.where((diff >= -h) & (diff <= h - 1) & inseg, 1.0, 0.0).astype(BF16)
            tot = _dot(band, hi) + _dot(band, lo)
            acc = jnp.where(group == gi, tot / _pool_count(rcol, h, seg_lo, seg_hi), acc)
        pooled = (acc - u_ref[0, pl.ds(r0, tm), :]).astype(BF16)
        pooled_ref[0] = pooled
        ob_ref[0] = (_dot(pooled, wp_ref[0]) * ps_ref[0]).astype(BF16)

    row = pl.BlockSpec((1, tm, POOL_W), lambda b, t: (b, t, 0))
    return pl.pallas_call(
        body, name="pool_fwd", grid=(B, cfg.nT),
        in_specs=[pl.BlockSpec((1, T, POOL_W), lambda b, t: (b, 0, QKV_W // POOL_W)),
                  pl.BlockSpec((1, POOL_W, POOL_W), lambda b, t: (l, 0, 0)), _layer(l, POOL_W)],
        out_specs=[row, row],
        out_shape=[jax.ShapeDtypeStruct((B, T, POOL_W), BF16)] * 2, compiler_params=_params(2),
    )(z, wp, ps)


def _gate_specs(cfg):
    tm, gw = cfg.tm, cfg.gw
    first = GATE0 // gw
    return [pl.BlockSpec((1, tm, gw), functools.partial(lambda b, t, j: (b, t, j), j=first + i)) for i in range(3 * cfg.D // gw)]


def _read_gates(cfg, gate_refs):
    per = cfg.D // cfg.gw
    return [jnp.concatenate([gate_refs[k * per + i][0] for i in range(per)], axis=-1) for k in range(3)]


def _merge_fwd(cfg, x, oa, ob, oc, z, mod, wa, wb, wc, wo, *, ctx_active, exch=None):
    B, T, D, tm, nC = cfg.B, cfg.T, cfg.D, cfg.tm, cfg.nC
    ng = 3 * D // cfg.gw

    def body(x_ref, oa_ref, ob_ref, oc_ref, *rest):
        gate_refs = rest[:ng]
        mod_ref, wa_ref, wb_ref, wc_ref, wo_ref, x1_ref, mgo_ref = rest[ng:]
        b, t = pl.program_id(0), pl.program_id(1)

        def compute():
            row = jnp.where(t < nC, B, b)
            ga, gb, gc = _read_gates(cfg, gate_refs)
            y = (jax.nn.sigmoid(ga) * _dot(oa_ref[0], wa_ref[...])
                 + jax.nn.sigmoid(gb) * _dot(ob_ref[0], wb_ref[...])
                 + jax.nn.sigmoid(gc) * _dot(oc_ref[0], wc_ref[...]))
            mo = _dot(y.astype(BF16), wo_ref[...])
            mgo_ref[0] = mo.astype(BF16)
            x1_ref[0] = x_ref[0] + _mod_row(mod_ref, row, 2, D) * mo

        if ctx_active:
            compute()
        else:
            pl.when(t >= nC)(compute)

            @pl.when(t < nC)
            def _():
                mgo_ref[0] = jnp.zeros((tm, D), BF16)
                x1_ref[0] = x_ref[0]

    row = lambda w: pl.BlockSpec((1, tm, w), lambda b, t: (b, t, 0))
    return _pcall(exch)(
        body, name="merge_fwd", grid=(B, cfg.nT),
        in_specs=[row(D), row(QW), row(POOL_W), row(QW)] + _gate_specs(cfg)
        + [_full((8, 6 * D)), _full((QW, D)), _full((POOL_W, D)), _full((QW, D)), _full((D, D))],
        out_specs=[row(D), row(D)],
        out_shape=[jax.ShapeDtypeStruct((B, T, D), F32), jax.ShapeDtypeStruct((B, T, D), BF16)],
        compiler_params=_params(2),
    )(x, oa, ob, oc, *([z] * ng), mod, wa, wb, wc, wo)


def _w1_apply(hb, w1_ref):
    return jnp.concatenate([_dot(hb, w1_ref[d]) for d in range(N_DEV)], axis=-1)


def _mlp_fwd(cfg, l, x1, gn, mod, w1, w2, *, ctx_active, target=None, exch=None):
    B, T, D, F, tm, nC = cfg.B, cfg.T, cfg.D, cfg.F, cfg.tm, cfg.nC
    assert target is None or not ctx_active

    def body(x_ref, gn_ref, mod_ref, w1_ref, w2_ref, *rest):
        if target is None:
            x2_ref, mo_ref, r_ref = rest
        else:
            tgt_ref, x2_ref, mo_ref, r_ref, sse_ref = rest
            _acc_init([sse_ref])
        b, t = pl.program_id(0), pl.program_id(1)

        def compute():
            row = jnp.where(t < nC, B, b)
            x = x_ref[0]
            h, _, _ = _modulate(x, gn_ref[0], _mod_row(mod_ref, row, 3, D), _mod_row(mod_ref, row, 4, D))
            a = jnp.maximum(_w1_apply(h.astype(BF16), w1_ref), 0.0)
            rb = (a * a).astype(BF16)
            r_ref[0] = rb
            mo = _dot(rb, w2_ref[...])
            mo_ref[0] = mo.astype(BF16)
            x2 = x + _mod_row(mod_ref, row, 5, D) * mo
            if target is None:
                x2_ref[0] = x2
            else:
                err = x2 - tgt_ref[0]
                x2_ref[0] = err * (1.0 / D)
                sse_ref[...] += jnp.sum(err * err)

        if ctx_active:
            compute()
        else:
            pl.when(t >= nC)(compute)

            @pl.when(t < nC)
            def _():
                mo_ref[0] = jnp.zeros((tm, D), BF16)
                r_ref[0] = jnp.zeros((tm, F), BF16)
                x2_ref[0] = x_ref[0] if target is None else jnp.zeros((tm, D), F32)

    row = pl.BlockSpec((1, tm, D), lambda b, t: (b, t, 0))
    in_specs = [row, _layer(l, D), _full((8, 6 * D)), _full((N_DEV, D, F // N_DEV)), _full((F, D))]
    out_specs = [row, row, pl.BlockSpec((1, tm, F), lambda b, t: (b, t, 0))]
    out_shape = [jax.ShapeDtypeStruct((B, T, D), F32), jax.ShapeDtypeStruct((B, T, D), BF16), jax.ShapeDtypeStruct((B, T, F), BF16)]
    args = [x1, gn, mod, w1, w2]
    if target is not None:
        in_specs.append(pl.BlockSpec((1, tm, D), lambda b, t: (b, jnp.maximum(t - nC, 0), 0)))
        out_specs.append(_full((8, LANES)))
        out_shape.append(jax.ShapeDtypeStruct((8, LANES), F32))
        args.append(target)
    return _pcall(exch)(
        body, name="mlp_fwd", grid=(B, cfg.nT), in_specs=in_specs, out_specs=out_specs, out_shape=out_shape,
        compiler_params=_params(2),
    )(*args)


def _acc_init(refs):
    b, t = pl.program_id(0), pl.program_id(1)

    @pl.when((b == 0) & (t == 0))
    def _():
        for ref in refs:
            ref[...] = jnp.zeros(ref.shape, ref.dtype)


def _mlp_bwd(cfg, l, x1, dx2, mo, r, gn, mod, w1, w2, *, ctx_active, exch=None):
    B, T, D, F, tm, nC = cfg.B, cfg.T, cfg.D, cfg.F, cfg.tm, cfg.nC
    ws = F // N_DEV

    def body(x_ref, dx_ref, mo_ref, r_ref, gn_ref, mod_ref, w1_ref, w2_ref, dx1_ref, h_ref, da_ref, dout_ref, dmod_ref, dgn_ref):
        b, t = pl.program_id(0), pl.program_id(1)
        _acc_init([dmod_ref, dgn_ref])

        def compute():
            row = jnp.where(t < nC, B, b)
            gn = gn_ref[0]
            scale = _mod_row(mod_ref, row, 4, D)
            h, xhat, rstd = _modulate(x_ref[0], gn, _mod_row(mod_ref, row, 3, D), scale)
            hb = h.astype(BF16)
            dx = dx_ref[0]
            dout = (dx * _mod_row(mod_ref, row, 5, D)).astype(BF16)
            da = (_dot_nt(dout, w2_ref[...]) * (2.0 * jnp.sqrt(r_ref[0].astype(F32)))).astype(BF16)
            dh = _dot_nt(da[:, 0:ws], w1_ref[0])
            for d in range(1, N_DEV):
                dh = dh + _dot_nt(da[:, d * ws:(d + 1) * ws], w1_ref[d])
            dxn, d_shift, d_scale, d_gn = _modulate_bwd(dh, xhat, rstd, gn, scale)
            dx1_ref[0] = dx + dxn
            h_ref[0] = hb
            da_ref[0] = da
            dout_ref[0] = dout
            d_gate = jnp.sum(dx * mo_ref[0].astype(F32), axis=0, keepdims=True)
            dmod_ref[pl.ds(row, 1), :] += jnp.concatenate([d_shift, d_scale, d_gate], axis=-1)
            dgn_ref[0:1, :] += d_gn

        if ctx_active:
            compute()
        else:
            pl.when(t >= nC)(compute)

            @pl.when(t < nC)
            def _():
                dx1_ref[0] = dx_ref[0]
                h_ref[0] = jnp.zeros((tm, D), BF16)
                da_ref[0] = jnp.zeros((tm, F), BF16)
                dout_ref[0] = jnp.zeros((tm, D), BF16)

    row = lambda w: pl.BlockSpec((1, tm, w), lambda b, t: (b, t, 0))
    sds = lambda w, dt: jax.ShapeDtypeStruct((B, T, w), dt)
    return _pcall(exch)(
        body, name="mlp_bwd", grid=(B, cfg.nT),
        in_specs=[row(D), row(D), row(D), row(F), _layer(l, D), _full((8, 6 * D)), _full((N_DEV, D, ws)), _full((F, D))],
        out_specs=[row(D), row(D), row(F), row(D), _full((8, 3 * D)), _full((8, D))],
        out_shape=[sds(D, F32), sds(D, BF16), sds(F, BF16), sds(D, BF16),
                   jax.ShapeDtypeStruct((8, 3 * D), F32), jax.ShapeDtypeStruct((8, D), F32)],
        compiler_params=_params(2),
    )(x1, dx2, mo, r, gn, mod, w1, w2)


def _matmul_tn(a, g, name, *, by_shard):
    R, Ka = a.shape
    Ng = g.shape[1]
    tr = next(c for c in (2304, 1024, 512, 256, 128, 64, 32, 16, 8) if R % c == 0)
    tka = Ka if Ka <= 1024 else 1024
    if by_shard:
        ws = Ng // N_DEV
        per = next(c for c in (8, 4, 2, 1) if c * ws <= 1152 or c == 1)
        tn = per * ws
    else:
        tn = next(c for c in (1152, 1024, 768, 512, 384, 256, 128) if Ng % c == 0)
    assert Ka % tka == 0 and tn % LANES == 0
    nr = R // tr

    def body(a_ref, g_ref, o_ref, acc_ref):
        r = pl.program_id(2)

        @pl.when(r == 0)
        def _():
            acc_ref[...] = jnp.zeros(acc_ref.shape, F32)

        acc_ref[...] += _dot_tn(a_ref[...], g_ref[...])

        @pl.when(r == nr - 1)
        def _():
            if by_shard:
                for d in range(per):
                    o_ref[d] = acc_ref[:, d * ws:(d + 1) * ws].astype(BF16)
            else:
                o_ref[...] = acc_ref[...].astype(BF16)

    if by_shard:
        out_spec = pl.BlockSpec((per, tka, ws), lambda i, j, r: (j, i, 0))
        out_shape = jax.ShapeDtypeStruct((N_DEV, Ka, ws), BF16)
    else:
        out_spec = pl.BlockSpec((tka, tn), lambda i, j, r: (i, j))
        out_shape = jax.ShapeDtypeStruct((Ka, Ng), BF16)
    return pl.pallas_call(
        body, name=name, grid=(Ka // tka, Ng // tn, nr),
        in_specs=[pl.BlockSpec((tr, tka), lambda i, j, r: (r, i)), pl.BlockSpec((tr, tn), lambda i, j, r: (r, j))],
        out_specs=out_spec, out_shape=out_shape, scratch_shapes=[pltpu.VMEM((tka, tn), F32)], compiler_params=_params(3),
    )(a, g)


def _merge_bwd(cfg, dx1, mgo, oa, ob, oc, z, mod, wa, wb, wc, wo, *, ctx_active, exch=None):
    B, T, D, tm, nC = cfg.B, cfg.T, cfg.D, cfg.tm, cfg.nC
    ng = 3 * D // cfg.gw

    def body(dx_ref, mgo_ref, oa_ref, ob_ref, oc_ref, *rest):
        gate_refs = rest[:ng]
        (mod_ref, wa_ref, wb_ref, wc_ref, wo_ref,
         doa_ref, dob_ref, doc_ref, dpa_ref, dpb_ref, dpc_ref, y_ref, dmo_ref, dzg_ref, dg1_ref) = rest[ng:]
        b, t = pl.program_id(0), pl.program_id(1)
        _acc_init([dg1_ref])

        def compute():
            row = jnp.where(t < nC, B, b)
            dx = dx_ref[0]
            dg1_ref[pl.ds(row, 1), :] += jnp.sum(dx * mgo_ref[0].astype(F32), axis=0, keepdims=True)
            dmo = (dx * _mod_row(mod_ref, row, 2, D)).astype(BF16)
            dmo_ref[0] = dmo
            dy = _dot_nt(dmo, wo_ref[...])
            gates = _read_gates(cfg, gate_refs)
            y = jnp.zeros((tm, D), F32)
            dgs = []
            for gate, o_ref, w_ref, do_ref, dp_ref in ((gates[0], oa_ref, wa_ref, doa_ref, dpa_ref),
                                                      (gates[1], ob_ref, wb_ref, dob_ref, dpb_ref),
                                                      (gates[2], oc_ref, wc_ref, doc_ref, dpc_ref)):
                s = jax.nn.sigmoid(gate)
                p = _dot(o_ref[0], w_ref[...])
                y = y + s * p
                dp = (dy * s).astype(BF16)
                dp_ref[0] = dp
                do_ref[0] = _dot_nt(dp, w_ref[...]).astype(BF16)
                dgs.append((dy * p * s * (1.0 - s)).astype(BF16))
            y_ref[0] = y.astype(BF16)
            dzg_ref[0] = jnp.concatenate(dgs, axis=-1)

        if ctx_active:
            compute()
        else:
            pl.when(t >= nC)(compute)

            @pl.when(t < nC)
            def _():
                for ref in (doa_ref, dob_ref, doc_ref, dpa_ref, dpb_ref, dpc_ref, y_ref, dmo_ref, dzg_ref):
                    ref[...] = jnp.zeros(ref.shape, ref.dtype)

    row = lambda w: pl.BlockSpec((1, tm, w), lambda b, t: (b, t, 0))
    sds = lambda w: jax.ShapeDtypeStruct((B, T, w), BF16)
    return _pcall(exch)(
        body, name="merge_bwd", grid=(B, cfg.nT),
        in_specs=[row(D), row(D), row(QW), row(POOL_W), row(QW)] + _gate_specs(cfg)
        + [_full((8, 6 * D)), _full((QW, D)), _full((POOL_W, D)), _full((QW, D)), _full((D, D))],
        out_specs=[row(QW), row(POOL_W), row(QW), row(D), row(D), row(D), row(D), row(D), row(3 * D), _full((8, D))],
        out_shape=[sds(QW), sds(POOL_W), sds(QW), sds(D), sds(D), sds(D), sds(D), sds(D), sds(3 * D),
                   jax.ShapeDtypeStruct((8, D), F32)],
        compiler_params=_params(2),
    )(dx1, mgo, oa, ob, oc, *([z] * ng), mod, wa, wb, wc, wo)


def _attn_bwd(cfg, l, q, k, z, vblock, sink8, do, lse, *, window, sink, ctx_attend, name, exch=None):
    B, S, N, T, tq, nC = cfg.B, cfg.S, cfg.N, cfg.T, cfg.tm, cfg.nC

    def body(q_ref, k_ref, v_ref, sink_ref, do_ref, lse_ref, dq_ref, dk_ref, dv_ref, dsink_ref):
        b, t = pl.program_id(0), pl.program_id(1)
        _acc_init([dsink_ref])

        @pl.when(t == 0)
        def _():
            dk_ref[...] = jnp.zeros(dk_ref.shape, F32)
            dv_ref[...] = jnp.zeros(dv_ref.shape, F32)

        def run(loc):
            q_t = q_ref[0]
            do_t = do_ref[0]
            dqs = [None] * N_QHEADS
            dks, dvs = [], []
            dsink_row = jnp.zeros((1, LANES), F32)
            lane = lax.broadcasted_iota(jnp.int32, (1, LANES), 1)
            lo = None
            for h in range(N_KV):
                lse = jnp.concatenate([lse_ref[0, GROUP * h + g:GROUP * h + g + 1, :] for g in range(GROUP)], axis=1)
                qs, kk, vv, p, _, p_s, lo, _ = _attn_scores(cfg, t - nC, q_t, k_ref, v_ref, sink_ref, h, loc, window and loc, sink, lse=lse)
                dos = jnp.concatenate([do_t[:, (GROUP * h + g) * HEAD:(GROUP * h + g + 1) * HEAD] for g in range(GROUP)], axis=0)
                dp = _dot_nt(vv, dos)
                delta = jnp.sum(p * dp, axis=0, keepdims=True)
                ds = (p * (dp - delta)).astype(BF16)
                dq = _dot_tn(kk, ds).T
                dks.append(_dot(ds, qs))
                dvs.append(_dot(p.astype(BF16), dos))
                if sink:
                    dsk = -p_s * delta
                    for g in range(GROUP):
                        tot = jnp.sum(dsk[:, g * tq:(g + 1) * tq], axis=1, keepdims=True)
                        dsink_row = dsink_row + jnp.where(lane == GROUP * h + g, tot, 0.0)
                for g in range(GROUP):
                    dqs[GROUP * h + g] = dq[g * tq:(g + 1) * tq] * QSCALE
            dq_ref[0] = jnp.concatenate(dqs, axis=-1)
            dk = jnp.concatenate(dks, axis=-1)
            dv = jnp.concatenate(dvs, axis=-1)
            if loc and not window:
                dk_ref[0] += dk
                dv_ref[0] += dv
            else:
                dk_ref[0, 0:N, :] += dk[0:N]
                dv_ref[0, 0:N, :] += dv[0:N]
                if loc:
                    W = tq + 2 * WINDOW
                    dk_ref[0, pl.ds(N + lo, W), :] += dk[N:]
                    dv_ref[0, pl.ds(N + lo, W), :] += dv[N:]
            if sink:
                dsink_ref[0:1, :] += dsink_row

        pl.when(t >= nC)(functools.partial(run, True))
        if ctx_attend:
            pl.when(t < nC)(functools.partial(run, False))
        else:
            @pl.when(t < nC)
            def _():
                dq_ref[0] = jnp.zeros((tq, QW), F32)

    kv = pl.BlockSpec((1, T, KVW), lambda b, t: (b, 0, 0))
    qrow = pl.BlockSpec((1, tq, QW), lambda b, t: (b, t, 0))
    return _pcall(exch)(
        body, name=name, grid=(B, cfg.nT),
        in_specs=[qrow, kv, pl.BlockSpec((1, T, KVW), lambda b, t: (b, 0, vblock)),
                  pl.BlockSpec((1, 8, LANES), lambda b, t: (l, 0, 0)), qrow, pl.BlockSpec((1, 8, tq), lambda b, t: (b, 0, t))],
        out_specs=[qrow, kv, kv, _full((8, LANES))],
        out_shape=[jax.ShapeDtypeStruct((B, T, QW), F32), jax.ShapeDtypeStruct((B, T, KVW), F32),
                   jax.ShapeDtypeStruct((B, T, KVW), F32), jax.ShapeDtypeStruct((8, LANES), F32)],
        compiler_params=_params(2),
    )(q, k, z, sink8, do, lse)


def _qknorm_bwd(cfg, l, z, gvec, cosf, sins, dqa, dka, dva, dqc, dkc, dvc):
    B, T, tm = cfg.B, cfg.T, cfg.tm

    def body(z_ref, g_ref, cos_ref, sin_ref, dqa_ref, dka_ref, dva_ref, dqc_ref, dkc_ref, dvc_ref, dz_ref, dg_ref):
        _acc_init([dg_ref])
        ind = _head_indicator()
        cos, sin = cos_ref[...], sin_ref[...]
        dqa_t, dqc_t = dqa_ref[0], dqc_ref[0]
        douts = {0: dqa_t[:, 0:128], 1: dqa_t[:, 128:256], 2: dqa_t[:, 256:384], 3: dka_ref[0],
                 5: dqc_t[:, 0:128], 6: dqc_t[:, 128:256], 7: dqc_t[:, 256:384], 8: dkc_ref[0]}
        pieces = []
        dgs = []
        for c in range(QKV_W // LANES):
            if c not in douts:
                pieces.append(dva_ref[0] if c == 4 else dvc_ref[0])
                dgs.append(jnp.zeros((1, LANES), F32))
                continue
            x = z_ref[0, :, c * LANES:(c + 1) * LANES]
            g = g_ref[0, :, c * LANES:(c + 1) * LANES]
            ss = _head_sum(x * x, ind)
            rstd = lax.rsqrt(ss * (1.0 / HEAD) + EPS)
            n = x * rstd
            dout = douts[c]
            dy = dout * cos + _pair_swap(dout * sin)
            dgs.append(jnp.sum(dy * n, axis=0, keepdims=True))
            dn = dy * g
            mean = _head_sum(dn * n, ind) * (1.0 / HEAD)
            pieces.append(rstd * (dn - n * mean))
        dz_ref[0] = jnp.concatenate(pieces, axis=-1).astype(BF16)
        dg_ref[0:1, :] += jnp.concatenate(dgs, axis=-1)

    row = lambda w: pl.BlockSpec((1, tm, w), lambda b, t: (b, t, 0))
    tab = pl.BlockSpec((tm, LANES), lambda b, t: (t, 0))
    return pl.pallas_call(
        body, name="qknorm_bwd", grid=(B, cfg.nT),
        in_specs=[row(QKV_W), _layer(l, QKV_W), tab, tab, row(QW), row(KVW), row(KVW), row(QW), row(KVW), row(KVW)],
        out_specs=[row(QKV_W), _full((8, QKV_W))],
        out_shape=[jax.ShapeDtypeStruct((B, T, QKV_W), BF16), jax.ShapeDtypeStruct((8, QKV_W), F32)],
        compiler_params=_params(2),
    )(z, gvec, cosf, sins, dqa, dka, dva, dqc, dkc, dvc)


def _pool_bwd(cfg, l, dob, pooled, wp, ps):
    B, T, tm, kw = cfg.B, cfg.T, cfg.tm, cfg.kw

    def body(dob_ref, pooled_ref, wp_ref, ps_ref, du_ref, dwp_ref, dps_ref):
        t = pl.program_id(1)
        _acc_init([dwp_ref, dps_ref])
        r0, seg_lo, seg_hi, k0 = _pool_geometry(cfg, t)
        ps = ps_ref[0]
        wp = wp_ref[0]
        dmix = dob_ref[0, pl.ds(r0, tm), :].astype(F32)
        pooled = pooled_ref[0]
        dps_ref[0:1, :] += jnp.sum(dmix * _dot(pooled, wp), axis=0, keepdims=True)
        dpm = (dmix * ps).astype(BF16)
        dwp_ref[...] += _dot_tn(pooled, dpm)
        dpooled_t = _dot_nt(dpm, wp)
        dpm_w = (dob_ref[0, pl.ds(k0, kw), :].astype(F32) * ps).astype(BF16)
        dpooled_w = _dot_nt(dpm_w, wp)
        rr = r0 + lax.broadcasted_iota(jnp.int32, (tm, kw), 0)
        cc = k0 + lax.broadcasted_iota(jnp.int32, (tm, kw), 1)
        diff = rr - cc
        inseg = (cc >= seg_lo) & (cc < seg_hi)
        ccol = k0 + lax.broadcasted_iota(jnp.int32, (kw, 1), 0)
        group = lax.broadcasted_iota(jnp.int32, (tm, POOL_W), 1) // HEAD
        acc = jnp.zeros((tm, POOL_W), F32)
        for gi, w in enumerate(POOL_WINDOWS):
            h = w // 2
            band_t = jnp.where((diff >= -h) & (diff <= h - 1) & inseg, 1.0, 0.0).astype(BF16)
            hi, lo = _split_bf16(dpooled_w / _pool_count(ccol, h, seg_lo, seg_hi))
            acc = jnp.where(group == gi, _dot(band_t, hi) + _dot(band_t, lo), acc)
        du_ref[0] = (acc - dpooled_t).astype(BF16)

    row = pl.BlockSpec((1, tm, POOL_W), lambda b, t: (b, t, 0))
    return pl.pallas_call(
        body, name="pool_bwd", grid=(B, cfg.nT),
        in_specs=[pl.BlockSpec((1, T, POOL_W), lambda b, t: (b, 0, 0)), row,
                  pl.BlockSpec((1, POOL_W, POOL_W), lambda b, t: (l, 0, 0)), _layer(l, POOL_W)],
        out_specs=[row, _full((POOL_W, POOL_W)), _full((8, POOL_W))],
        out_shape=[jax.ShapeDtypeStruct((B, T, POOL_W), BF16), jax.ShapeDtypeStruct((POOL_W, POOL_W), F32),
                   jax.ShapeDtypeStruct((8, POOL_W), F32)],
        compiler_params=_params(2),
    )(dob, pooled, wp, ps)


def _in_proj_bwd(cfg, l, dzq, du, dzg, w_in, x, dx1, gn, mod, *, latent_only, exch=None):
    B, S, T, D, IN, tm, nC = cfg.B, cfg.S, cfg.T, cfg.D, cfg.IN, cfg.tm, cfg.nC

    def body(dzq_ref, du_ref, dzg_ref, w_ref, x_ref, dx1_ref, gn_ref, mod_ref, dx0_ref, dz_ref, dmod_ref, dgn_ref):
        b, t = pl.program_id(0), pl.program_id(1)
        _acc_init([dmod_ref, dgn_ref])
        row = jnp.where(t < nC, B, b)
        dz = jnp.concatenate([dzq_ref[0], du_ref[0], dzg_ref[0]], axis=-1)
        dz_ref[0] = dz
        dh = _dot_nt(dz, w_ref[...])
        gn = gn_ref[0]
        scale = _mod_row(mod_ref, row, 1, D)
        _, xhat, rstd = _modulate(x_ref[0], gn, _mod_row(mod_ref, row, 0, D), scale)
        dxn, d_shift, d_scale, d_gn = _modulate_bwd(dh, xhat, rstd, gn, scale)
        dx0_ref[0] = dx1_ref[0] + dxn
        dmod_ref[pl.ds(row, 1), :] += jnp.concatenate([d_shift, d_scale], axis=-1)
        dgn_ref[0:1, :] += d_gn

    row = lambda w: pl.BlockSpec((1, tm, w), lambda b, t: (b, t, 0))
    if latent_only:
        dx0_spec = pl.BlockSpec((1, tm, D), lambda b, t: (b, jnp.maximum(t - nC, 0), 0))
        dx0_shape = jax.ShapeDtypeStruct((B, S, D), F32)
    else:
        dx0_spec, dx0_shape = row(D), jax.ShapeDtypeStruct((B, T, D), F32)
    return _pcall(exch)(
        body, name="in_proj_bwd", grid=(B, cfg.nT),
        in_specs=[row(QKV_W), row(POOL_W), row(3 * D), _full((D, IN)), row(D), row(D), _layer(l, D), _full((8, 6 * D))],
        out_specs=[dx0_spec, row(IN), _full((8, 2 * D)), _full((8, D))],
        out_shape=[dx0_shape, jax.ShapeDtypeStruct((B, T, IN), BF16),
                   jax.ShapeDtypeStruct((8, 2 * D), F32), jax.ShapeDtypeStruct((8, D), F32)],
        compiler_params=_params(2),
    )(dzq, du, dzg, w_in, x, dx1, gn, mod)


def _adaln_bwd(cfg, l, cc_all, dm_all, w_ada):
    d, B = cfg.D, cfg.B
    wa = w_ada.shape[2]

    def body(c_ref, dm_ref, w_ref, dw_ref, dc_ref):
        c = c_ref[...]
        s = jax.nn.sigmoid(c)
        dmb = dm_ref[...].astype(BF16)
        dw_ref[...] = _dot_tn((c * s).astype(BF16), dmb)
        dc = _dot_nt(dmb, w_ref[0].astype(BF16)) * (s * (1.0 + c * (1.0 - s)))
        is_ctx = lax.broadcasted_iota(jnp.int32, (8 * N_DEV, 1), 0) % 8 == B
        dc_ref[...] = jnp.broadcast_to(jnp.sum(jnp.where(is_ctx, dc, 0.0), axis=0, keepdims=True), (8, d))

    return pl.pallas_call(
        body, name="adaln_bwd", grid=(1,),
        in_specs=[_full((8 * N_DEV, d)), _full((8 * N_DEV, wa)), pl.BlockSpec((1, d, wa), lambda *_: (l, 0, 0))],
        out_specs=[_full((d, wa)), _full((8, d))],
        out_shape=[jax.ShapeDtypeStruct((d, wa), F32), jax.ShapeDtypeStruct((8, d), F32)],
        compiler_params=_params(1),
    )(cc_all, dm_all, w_ada)


def _dmod_pack(cfg, dmod_in, dg1, dmod_mlp):
    d = cfg.D
    wa = 6 * d // N_DEV

    def body(din_ref, dg1_ref, dmlp_ref, o_ref, db_ref):
        dm = jnp.concatenate([din_ref[...], dg1_ref[...], dmlp_ref[...]], axis=-1)
        for j in range(N_DEV):
            o_ref[j] = dm[:, j * wa:(j + 1) * wa]
        db_ref[...] = jnp.broadcast_to(jnp.sum(dm, axis=0, keepdims=True), (8, 6 * d))

    return pl.pallas_call(
        body, name="dmod_pack", grid=(1,),
        in_specs=[_full((8, 2 * d)), _full((8, d)), _full((8, 3 * d))],
        out_specs=[_full((N_DEV, 8, wa)), _full((8, 6 * d))],
        out_shape=[jax.ShapeDtypeStruct((N_DEV, 8, wa), F32), jax.ShapeDtypeStruct((8, 6 * d), F32)],
        compiler_params=_params(1),
    )(dmod_in, dg1, dmod_mlp)


def _adam_update(g, w, m, v):
    bc1 = 1.0 - ADAM_B1 ** ADAM_STEP
    bc2 = 1.0 - ADAM_B2 ** ADAM_STEP
    m2 = ADAM_B1 * m + (1.0 - ADAM_B1) * g
    v2 = ADAM_B2 * v + (1.0 - ADAM_B2) * (g * g)
    delta = -ADAM_LR * ((m2 / bc1) / (jnp.sqrt(v2 / bc2) + ADAM_EPS) + ADAM_WD * w)
    return delta, m2, v2


def _sum_parts(p_ref):
    g = p_ref[0].astype(F32)
    for d in range(1, p_ref.shape[0]):
        g = g + p_ref[d].astype(F32)
    return g


def _adamw_sharded(parts, w, m, v, name):
    L, K, W = w.shape
    P = parts[0].shape[0]
    tk = next(c for c in (256, 128, 64, 32, 16, 8) if K % c == 0)

    def body(*refs):
        p_refs = refs[:L]
        w_ref, m_ref, v_ref, g_ref, d_ref, m2_ref, v2_ref = refs[L:]
        layer = pl.program_id(0)

        def run(p_ref):
            g = _sum_parts(p_ref)
            delta, m2, v2 = _adam_update(g, w_ref[0], m_ref[0], v_ref[0])
            g_ref[0] = g
            d_ref[0] = delta
            m2_ref[0] = m2
            v2_ref[0] = v2

        for li in range(L):
            pl.when(layer == li)(functools.partial(run, p_refs[li]))

    blk = pl.BlockSpec((1, tk, W), lambda l, i: (l, i, 0))
    part_spec = lambda li: pl.BlockSpec((P, tk, W), lambda l, i: (0, jnp.where(l == li, i, 0), 0))
    return pl.pallas_call(
        body, name=name, grid=(L, K // tk),
        in_specs=[part_spec(li) for li in range(L)] + [blk, blk, blk],
        out_specs=[blk] * 4, out_shape=[jax.ShapeDtypeStruct((L, K, W), F32)] * 4,
        compiler_params=_params(2),
    )(*parts, w, m, v)


def _adamw_packed(parts, w, m, v, name):
    rows = w.shape[0]
    tr = next(c for c in (256, 128, 64, 32, 16, 8) if rows % c == 0)

    def body(p_ref, w_ref, m_ref, v_ref, g_ref, d_ref, m2_ref, v2_ref):
        g = _sum_parts(p_ref)
        delta, m2, v2 = _adam_update(g, w_ref[...], m_ref[...], v_ref[...])
        g_ref[...] = g
        d_ref[...] = delta
        m2_ref[...] = m2
        v2_ref[...] = v2

    blk = pl.BlockSpec((tr, PACK_W), lambda i: (i, 0))
    return pl.pallas_call(
        body, name=name, grid=(rows // tr,),
        in_specs=[pl.BlockSpec((N_DEV, tr, PACK_W), lambda i: (0, i, 0)), blk, blk, blk],
        out_specs=[blk] * 4, out_shape=[jax.ShapeDtypeStruct((rows, PACK_W), F32)] * 4,
        compiler_params=_params(1),
    )(parts, w, m, v)


_SHARDED = dict(w_ada=True, w_in=True, w_br_a=True, w_br_b=True, w_br_c=True, w_out=False, w_mlp1=True, w_mlp2=False)
_MERGE_WEIGHTS = ("w_br_a", "w_br_b", "w_br_c", "w_out")
_GATHERED = ("w_in",) + _MERGE_WEIGHTS + ("w_mlp1", "w_mlp2")
_KEEP_SHARDS = ("w_mlp1",)
_SMALL = ("c_ctx", "b_ada", "norm1", "norm2", "q_norm_a", "k_norm_a", "q_norm_c", "k_norm_c", "sink_c", "w_pool", "pool_scale")


def _from_shards(name, g):
    n, k, w = g.shape
    if name in _KEEP_SHARDS:
        return g
    if _SHARDED[name]:
        return g.transpose(1, 0, 2).reshape(k, n * w)
    return g.reshape(n * k, w)


def _to_shards(name, g):
    if g.ndim == 3:
        return g
    if _SHARDED[name]:
        k, nw = g.shape
        return g.reshape(k, N_DEV, nw // N_DEV).transpose(1, 0, 2)
    nk, w = g.shape
    return g.reshape(N_DEV, nk // N_DEV, w)


def _pack_small(vals):
    flat = jnp.concatenate([vals[n].reshape(-1) for n in _SMALL])
    rows = -(-flat.shape[0] // (8 * PACK_W)) * 8
    return jnp.pad(flat, (0, rows * PACK_W - flat.shape[0])).reshape(rows, PACK_W)


def _unpack_small(packed, like):
    flat, out, r = packed.reshape(-1), {}, 0
    for n in _SMALL:
        sz = like[n].size
        out[n] = flat[r:r + sz].reshape(like[n].shape)
        r += sz
    return out


def _rope_tables(cfg):
    pos = jnp.arange(cfg.S, dtype=F32)
    r = jnp.floor(pos / GRID_W)
    col = pos - r * GRID_W
    inv = 1.0 / (ROPE_THETA ** (jnp.arange(0, HEAD // 2, 2, dtype=F32) / (HEAD // 2)))
    ang = jnp.concatenate([r[:, None] * inv, col[:, None] * inv], axis=-1)
    cos = jnp.repeat(jnp.cos(ang), 2, axis=-1)
    sin = jnp.repeat(jnp.sin(ang), 2, axis=-1) * jnp.tile(jnp.array([-1.0, 1.0], F32), HEAD // 2)
    cos = jnp.concatenate([jnp.ones((cfg.N, HEAD), F32), cos], axis=0)
    sin = jnp.concatenate([jnp.zeros((cfg.N, HEAD), F32), sin], axis=0)
    return jnp.tile(cos, (1, 2)), jnp.tile(sin, (1, 2))


def _gvec(qa, ka, qc, kc):
    one = jnp.ones((qa.shape[0], KVW), F32)
    t = lambda a, n: jnp.tile(a, (1, n))
    return jnp.concatenate([t(qa, N_QHEADS), t(ka, N_KV), one, t(qc, N_QHEADS), t(kc, N_KV), one], axis=-1)[:, None, :]


def _block_diag(wp):
    L, g, c, _ = wp.shape
    eye = jnp.eye(g, dtype=wp.dtype)
    return (wp[:, :, :, None, :] * eye[None, :, None, :, None]).reshape(L, g * c, g * c)


def _pad8(a):
    return jnp.pad(a, ((0, 8 - a.shape[0]), (0, 0)))


def kernel(x, c, ctx, c_ctx, w_ada, b_ada, norm1, norm2, w_in, q_norm_a, k_norm_a, q_norm_c, k_norm_c, sink_c, w_pool, pool_scale, w_br_a, w_br_b, w_br_c, w_out, w_mlp1, w_mlp2, loss_target, m_c_ctx, m_w_ada, m_b_ada, m_norm1, m_norm2, m_w_in, m_q_norm_a, m_k_norm_a, m_q_norm_c, m_k_norm_c, m_sink_c, m_w_pool, m_pool_scale, m_w_br_a, m_w_br_b, m_w_br_c, m_w_out, m_w_mlp1, m_w_mlp2, v_c_ctx, v_w_ada, v_b_ada, v_norm1, v_norm2, v_w_in, v_q_norm_a, v_k_norm_a, v_q_norm_c, v_k_norm_c, v_sink_c, v_w_pool, v_pool_scale, v_w_br_a, v_w_br_b, v_w_br_c, v_w_out, v_w_mlp1, v_w_mlp2):
    B, S, D = x.shape
    N = ctx.shape[1]
    L = w_ada.shape[0]
    cfg = _Cfg(B, S, N, D)
    T = cfg.T
    weights = dict(c_ctx=c_ctx, w_ada=w_ada, b_ada=b_ada, norm1=norm1, norm2=norm2, w_in=w_in, q_norm_a=q_norm_a,
                   k_norm_a=k_norm_a, q_norm_c=q_norm_c, k_norm_c=k_norm_c, sink_c=sink_c, w_pool=w_pool,
                   pool_scale=pool_scale, w_br_a=w_br_a, w_br_b=w_br_b, w_br_c=w_br_c, w_out=w_out, w_mlp1=w_mlp1, w_mlp2=w_mlp2)
    mom_m = dict(c_ctx=m_c_ctx, w_ada=m_w_ada, b_ada=m_b_ada, norm1=m_norm1, norm2=m_norm2, w_in=m_w_in, q_norm_a=m_q_norm_a,
                 k_norm_a=m_k_norm_a, q_norm_c=m_q_norm_c, k_norm_c=m_k_norm_c, sink_c=m_sink_c, w_pool=m_w_pool,
                 pool_scale=m_pool_scale, w_br_a=m_w_br_a, w_br_b=m_w_br_b, w_br_c=m_w_br_c, w_out=m_w_out, w_mlp1=m_w_mlp1, w_mlp2=m_w_mlp2)
    mom_v = dict(c_ctx=v_c_ctx, w_ada=v_w_ada, b_ada=v_b_ada, norm1=v_norm1, norm2=v_norm2, w_in=v_w_in, q_norm_a=v_q_norm_a,
                 k_norm_a=v_k_norm_a, q_norm_c=v_q_norm_c, k_norm_c=v_k_norm_c, sink_c=v_sink_c, w_pool=v_w_pool,
                 pool_scale=v_pool_scale, w_br_a=v_w_br_a, w_br_b=v_w_br_b, w_br_c=v_w_br_c, w_out=v_w_out, w_mlp1=v_w_mlp1, w_mlp2=v_w_mlp2)

    shards_bf16 = {n: weights[n].astype(BF16) for n in _GATHERED}
    full = [dict() for _ in range(L)]

    def gather_of(items):
        return _Exchange([(shards_bf16[n], l) for l, n in items], scatter=False)

    def gathered(items, arrs):
        for (l, n), a in zip(items, arrs):
            full[l][n] = _from_shards(n, a)

    gathered([(0, "w_in")], gather_of([(0, "w_in")]).alone("gather_first_weights"))

    def hosting(fn, *a, exch=None, done=None, **kw):
        if exch is None:
            return fn(*a, **kw)
        res = fn(*a, exch=exch, **kw)
        done(res[-exch.n:])
        own = res[:-exch.n]
        return own[0] if len(own) == 1 else own

    def gather_behind(l, names):
        if l >= L:
            return {}
        items = [(l, n) for n in names]
        return dict(exch=gather_of(items), done=functools.partial(gathered, items))

    cosf, sins = _rope_tables(cfg)
    xs = jnp.concatenate([ctx, x], axis=1)
    cc8 = _pad8(jnp.concatenate([c, c_ctx[None, :]], axis=0))
    va_blk, vc_blk = (QW + KVW) // KVW, (2 * QW + 3 * KVW) // KVW
    per_layer = lambda a: a[:, None, :]
    b_ada3, norm1_3, norm2_3, ps3 = per_layer(b_ada), per_layer(norm1), per_layer(norm2), per_layer(pool_scale)
    gvec = _gvec(q_norm_a, k_norm_a, q_norm_c, k_norm_c)
    sink8 = jnp.pad(sink_c[:, None, :], ((0, 0), (0, 7), (0, LANES - N_QHEADS)))
    wp = _block_diag(w_pool).astype(BF16)

    cc_all = _Exchange([cc8], scatter=False).alone("gather_cond")[0].reshape(8 * N_DEV, D)
    mod_cols = _Exchange([_adaln_fwd(cfg, cc_all, w_ada)], scatter=True).alone("scatter_mod")[0]
    mod_all = _adaln_join(cfg, mod_cols, b_ada3)

    saved = []
    for l in range(L):
        fw = full[l]
        ctx_active = l < L - 1
        mod = mod_all[l]
        z, h = hosting(_in_proj_fwd, cfg, l, xs, norm1_3, mod, fw["w_in"], **gather_behind(l, _MERGE_WEIGHTS if l == 0 else ("w_mlp2",)))
        qa, ka, qc, kc = _qknorm_fwd(cfg, l, z, gvec, cosf, sins)
        oa, lse_a = hosting(_attn_fwd, cfg, l, qa, ka, z, va_blk, sink8, window=False, sink=False, ctx_attend=ctx_active, name="attn_a_fwd",
                            **gather_behind(l, ("w_mlp1",)))
        oc, lse_c = hosting(_attn_fwd, cfg, l, qc, kc, z, vc_blk, sink8, window=True, sink=True, ctx_attend=ctx_active, name="attn_c_fwd",
                            **(gather_behind(l + 1, _MERGE_WEIGHTS)))
        ob, pooled = _pool_fwd(cfg, l, z, wp, ps3)
        x1, mgo = hosting(_merge_fwd, cfg, xs, oa, ob, oc, z, mod, fw["w_br_a"], fw["w_br_b"], fw["w_br_c"], fw["w_out"],
                          ctx_active=ctx_active, **(gather_behind(0, ("w_mlp2",)) if l == 0 else {}))
        if l < L - 1:
            x2, mo, r = hosting(_mlp_fwd, cfg, l, x1, norm2_3, mod, fw["w_mlp1"], fw["w_mlp2"], ctx_active=ctx_active,
                                **gather_behind(l + 1, ("w_in",)))
        else:
            x2, mo, r, sse = _mlp_fwd(cfg, l, x1, norm2_3, mod, fw["w_mlp1"], fw["w_mlp2"], ctx_active=ctx_active, target=loss_target)
        saved.append(dict(xs=xs, mod=mod, z=z, h=h, qa=qa, ka=ka, qc=qc, kc=kc, oa=oa, oc=oc, ob=ob, pooled=pooled, x1=x1, mgo=mgo, mo=mo,
                          lse_a=lse_a, lse_c=lse_c, r=r))
        xs = x2

    dxs = xs
    loss = lax.psum(0.5 * sse[0, 0] / D, ("x", "y", "c"))

    grads = [dict() for _ in range(L)]
    parts = {}
    small = {n: [None] * L for n in _SMALL if n != "c_ctx"}
    d_c_ctx = jnp.zeros((D,), F32)
    flat2 = lambda a: a.reshape(B * T, a.shape[-1])

    def scatter_of(l, names):
        return _Exchange([_to_shards(n, grads[l][n]) for n in names], scatter=True)

    def scattered(l, names, arrs):
        for n, a in zip(names, arrs):
            parts[(l, n)] = a

    def scatter_behind(l, names):
        if l >= L:
            return {}
        return dict(exch=scatter_of(l, names), done=functools.partial(scattered, l, names))

    for l in reversed(range(L)):
        fw, sv, g = full[l], saved[l], grads[l]
        ctx_active = l < L - 1
        mod = sv["mod"]
        dx1, h2, da, dout, dmod_mlp, dgn2 = hosting(_mlp_bwd, cfg, l, sv["x1"], dxs, sv["mo"], sv["r"], norm2_3, mod, fw["w_mlp1"], fw["w_mlp2"],
                                                    ctx_active=ctx_active, **scatter_behind(l + 1, ("w_in",)))
        g["w_mlp1"] = _matmul_tn(flat2(h2), flat2(da), "dw_mlp1", by_shard=True)
        g["w_mlp2"] = _matmul_tn(flat2(sv["r"]), flat2(dout), "dw_mlp2", by_shard=False)
        doa, dob, doc, dpa, dpb, dpc, y, dmo, dzg, dg1 = _merge_bwd(
            cfg, dx1, sv["mgo"], sv["oa"], sv["ob"], sv["oc"], sv["z"], mod, fw["w_br_a"], fw["w_br_b"], fw["w_br_c"], fw["w_out"],
            ctx_active=ctx_active)
        g["w_out"] = _matmul_tn(flat2(y), flat2(dmo), "dw_out", by_shard=False)
        g["w_br_a"] = _matmul_tn(flat2(sv["oa"]), flat2(dpa), "dw_br_a", by_shard=True)
        g["w_br_b"] = _matmul_tn(flat2(sv["ob"]), flat2(dpb), "dw_br_b", by_shard=True)
        g["w_br_c"] = _matmul_tn(flat2(sv["oc"]), flat2(dpc), "dw_br_c", by_shard=True)
        z = sv["z"]
        dqa, dka, dva, _ = hosting(_attn_bwd, cfg, l, sv["qa"], sv["ka"], z, va_blk, sink8, doa, sv["lse_a"], window=False, sink=False,
                                   ctx_attend=ctx_active, name="attn_a_bwd", **scatter_behind(l, ("w_mlp1", "w_mlp2")))
        dqc, dkc, dvc, dsink = hosting(_attn_bwd, cfg, l, sv["qc"], sv["kc"], z, vc_blk, sink8, doc, sv["lse_c"], window=True, sink=True,
                                       ctx_attend=ctx_active, name="attn_c_bwd", **scatter_behind(l, _MERGE_WEIGHTS))
        dzq, dgvec = _qknorm_bwd(cfg, l, z, gvec, cosf, sins, dqa, dka, dva, dqc, dkc, dvc)
        du, dwp, dps = _pool_bwd(cfg, l, dob, sv["pooled"], wp, ps3)
        dxs, dz, dmod_in, dgn1 = _in_proj_bwd(cfg, l, dzq, du, dzg, fw["w_in"], sv["xs"], dx1, norm1_3, mod, latent_only=(l == 0))
        g["w_in"] = _matmul_tn(flat2(sv["h"]), flat2(dz), "dw_in", by_shard=False)
        dmod_cols, dbias = _dmod_pack(cfg, dmod_in, dg1, dmod_mlp)
        dm_all = _Exchange([dmod_cols], scatter=True).alone("scatter_dmod")[0].reshape(8 * N_DEV, -1)
        g["w_ada"], dcc = _adaln_bwd(cfg, l, cc_all, dm_all, w_ada)
        d_c_ctx = d_c_ctx + dcc[0]
        gv = dgvec[0]
        heads = lambda v, n: v.reshape(n, HEAD).sum(axis=0)
        small["b_ada"][l] = dbias[0]
        small["norm1"][l] = dgn1[0]
        small["norm2"][l] = dgn2[0]
        small["q_norm_a"][l] = heads(gv[0:QW], N_QHEADS)
        small["k_norm_a"][l] = heads(gv[QW:QW + KVW], N_KV)
        small["q_norm_c"][l] = heads(gv[QW + 2 * KVW:2 * QW + 2 * KVW], N_QHEADS)
        small["k_norm_c"][l] = heads(gv[2 * QW + 2 * KVW:2 * QW + 3 * KVW], N_KV)
        small["sink_c"][l] = dsink[0, :N_QHEADS]
        small["w_pool"][l] = jnp.stack([dwp[i * HEAD:(i + 1) * HEAD, i * HEAD:(i + 1) * HEAD] for i in range(len(POOL_WINDOWS))])
        small["pool_scale"][l] = dps[0]
    grad_x = dxs

    scattered(0, ("w_in",), scatter_of(0, ("w_in",)).alone("scatter_last_grads"))
    for l in range(L):
        parts[(l, "w_ada")] = grads[l]["w_ada"][None]
    stepped = {n: _adamw_sharded([parts[(l, n)] for l in range(L)], weights[n], mom_m[n], mom_v[n], "adamw_" + n) for n in _SHARDED}

    small_vals = {n: jnp.stack(v) for n, v in small.items()}
    small_vals["c_ctx"] = d_c_ctx
    small_parts = _Exchange([_pack_small(small_vals)], scatter=False).alone("gather_small_grads")[0]
    stepped_small = _adamw_packed(small_parts, _pack_small(weights), _pack_small(mom_m), _pack_small(mom_v), "adamw_small")

    outs = []
    for i in range(4):
        res = {n: stepped[n][i] for n in _SHARDED}
        res.update(_unpack_small(stepped_small[i], weights))
        outs.append(res)
    order = ("c_ctx", "w_ada", "b_ada", "norm1", "norm2", "w_in", "q_norm_a", "k_norm_a", "q_norm_c", "k_norm_c", "sink_c",
             "w_pool", "pool_scale", "w_br_a", "w_br_b", "w_br_c", "w_out", "w_mlp1", "w_mlp2")
    return (loss, grad_x, *[res[n] for res in outs for n in order])
```

```python
import functools

import jax
import jax.numpy as jnp
from jax import lax
from jax.experimental import pallas as pl
from jax.experimental.pallas import tpu as pltpu

F32 = jnp.float32
BF16 = jnp.bfloat16
HIGHEST = lax.Precision.HIGHEST

N_DEV = 8
HEAD = 64
N_QHEADS = 6
N_KV = 2
GROUP = 3
QW = N_QHEADS * HEAD
KVW = N_KV * HEAD
QKV_W = 2 * (QW + 2 * KVW)
POOL_W = 256
POOL_WINDOWS = (2, 4, 8, 16)
POOL_HALO = 16
GATE0 = QKV_W + POOL_W
WINDOW = 128
GRID_W = 64
ROPE_THETA = 10000.0
EPS = 1e-6
NEG = -1e30
QSCALE = HEAD ** -0.5
LANES = 128
PACK_W = 1024
VMEM_LIMIT = 56 * 1024 * 1024

ADAM_LR = 0.001
ADAM_B1 = 0.9
ADAM_B2 = 0.999
ADAM_EPS = 1e-08
ADAM_WD = 0.01
ADAM_STEP = 10

NT_DIMS = (((1,), (1,)), ((), ()))
TN_DIMS = (((0,), (0,)), ((), ()))


def _dot(a, b):
    return jnp.dot(a, b, preferred_element_type=F32)


def _dot_nt(a, b):
    return lax.dot_general(a, b, NT_DIMS, preferred_element_type=F32)


def _dot_tn(a, b):
    return lax.dot_general(a, b, TN_DIMS, preferred_element_type=F32)


def _params(n_grid):
    return pltpu.CompilerParams(dimension_semantics=("arbitrary",) * n_grid, vmem_limit_bytes=VMEM_LIMIT)


def _full(shape):
    nd = len(shape)
    return pl.BlockSpec(shape, lambda *_: (0,) * nd)


def _layer(l, width):
    return pl.BlockSpec((1, 1, width), lambda *_: (l, 0, 0))


def _modulate(x, gn, shift, scale):
    rstd = lax.rsqrt(jnp.mean(x * x, axis=-1, keepdims=True) + EPS)
    xhat = x * rstd
    return xhat * gn * (1.0 + scale) + shift, xhat, rstd


def _modulate_bwd(dh, xhat, rstd, gn, scale):
    d_shift = jnp.sum(dh, axis=0, keepdims=True)
    d_scale = jnp.sum(dh * xhat * gn, axis=0, keepdims=True)
    dy = dh * (1.0 + scale)
    d_gn = jnp.sum(dy * xhat, axis=0, keepdims=True)
    dxh = dy * gn
    dx = rstd * (dxh - xhat * jnp.mean(dxh * xhat, axis=-1, keepdims=True))
    return dx, d_shift, d_scale, d_gn


def _mod_row(mod_ref, row, k, d):
    return mod_ref[pl.ds(row, 1), k * d:(k + 1) * d]


class _Cfg:
    def __init__(self, b, s, n, d):
        self.B, self.S, self.N, self.D = b, s, n, d
        self.T = n + s
        self.F = 4 * d
        self.IN = GATE0 + 3 * d
        self.tm = 256 if (n % 256 == 0 and s % 256 == 0) else 128
        self.nT = self.T // self.tm
        self.nC = n // self.tm
        self.gw = 512 if d % 512 == 0 else 256
        self.kw = self.tm + 2 * POOL_HALO
        assert GATE0 % self.gw == 0 and d % self.gw == 0 and b < 8 and self.T >= self.kw and max(POOL_WINDOWS) // 2 <= POOL_HALO
        assert s % GRID_W == 0 and n % self.tm == 0 and s % self.tm == 0 and s >= self.tm + 2 * WINDOW
        assert d % (N_DEV * LANES) == 0


def _peer(k):
    x, y, c = lax.axis_index("x"), lax.axis_index("y"), lax.axis_index("c")
    px = x ^ ((k >> 2) & 1)
    py = y ^ ((k >> 1) & 1)
    pc = c ^ (k & 1)
    return (px, py, pc), 4 * px + 2 * py + pc


class _Exchange:
    def __init__(self, arrays, scatter):
        self.arrays = [a if isinstance(a, tuple) else (a, None) for a in arrays]
        self.scatter = scatter
        self.n = len(self.arrays)

    def operands(self):
        return [a for a, _ in self.arrays]

    def out_shapes(self):
        res = []
        for a, layer in self.arrays:
            shape = a.shape[1:] if (self.scatter or layer is not None) else a.shape
            res.append(jax.ShapeDtypeStruct((N_DEV,) + tuple(shape), a.dtype))
        return res

    def scratch(self):
        n = self.n * (N_DEV - 1)
        return [pltpu.SemaphoreType.DMA((n,)), pltpu.SemaphoreType.DMA((n,)), pltpu.SemaphoreType.DMA((self.n,))]

    def _copies(self, x_refs, out_refs, send_sems, recv_sems, local_sems, want):
        _, me = _peer(0)
        res = []
        for i, ((_, layer), x_ref, out_ref) in enumerate(zip(self.arrays, x_refs, out_refs)):
            if self.scatter:
                src_of = lambda d, x_ref=x_ref: x_ref.at[d]
            elif layer is not None:
                src_of = lambda d, x_ref=x_ref, layer=layer: x_ref.at[layer]
            else:
                src_of = lambda d, x_ref=x_ref: x_ref
            if want == "local":
                res.append(pltpu.make_async_copy(src_of(me), out_ref.at[me], local_sems.at[i]))
                continue
            for k in range(1, N_DEV):
                pos, idx = _peer(k)
                j = i * (N_DEV - 1) + k - 1
                common = dict(send_sem=send_sems.at[j], recv_sem=recv_sems.at[j], device_id=pos, device_id_type=pl.DeviceIdType.MESH)
                if want == "send":
                    res.append(pltpu.make_async_remote_copy(src_ref=src_of(idx), dst_ref=out_ref.at[me], **common))
                else:
                    res.append(pltpu.make_async_remote_copy(src_ref=src_of(me), dst_ref=out_ref.at[idx], **common))
        return res

    def start(self, *refs):
        for cp in self._copies(*refs, "local") + self._copies(*refs, "send"):
            cp.start()

    def wait(self, *refs):
        for cp in self._copies(*refs, "recv"):
            cp.wait_recv()
        for cp in self._copies(*refs, "send"):
            cp.wait_send()
        for cp in self._copies(*refs, "local"):
            cp.wait()

    def alone(self, name):
        n = self.n

        def body(*refs):
            args = (refs[:n], refs[n:2 * n], *refs[2 * n:])
            self.start(*args)
            self.wait(*args)

        any_spec = pl.BlockSpec(memory_space=pl.ANY)
        return pl.pallas_call(body, name=name, in_specs=[any_spec] * n, out_specs=[any_spec] * n,
                              out_shape=self.out_shapes(), scratch_shapes=self.scratch())(*self.operands())


def _gather_two_level(x, layer, name):
    shape = x.shape[1:]

    def body(x_ref, out_ref, send_sems, recv_sems, local_sem):
        mx, my, mc = lax.axis_index("x"), lax.axis_index("y"), lax.axis_index("c")
        me, sibling = (mx, my, mc), (mx, my, 1 - mc)
        chips = [(1 - mx, my), (mx, 1 - my), (1 - mx, 1 - my)]
        src = x_ref.at[layer]

        def slot(px, py, pc):
            return out_ref.at[4 * px + 2 * py + pc]

        def copy(k, block, to, from_src=False):
            return pltpu.make_async_remote_copy(src_ref=src if from_src else slot(*block), dst_ref=slot(*block), send_sem=send_sems.at[k],
                                                recv_sem=recv_sems.at[k], device_id=to, device_id_type=pl.DeviceIdType.MESH)

        mine = pltpu.make_async_copy(src, slot(*me), local_sem)
        mine.start()
        first = [copy(0, me, sibling, True)] + [copy(1 + j, me, (*chip, mc), True) for j, chip in enumerate(chips)]
        for cp in first:
            cp.start()
        passed = [copy(4 + j, (*chip, mc), sibling) for j, chip in enumerate(chips)]
        for j, chip in enumerate(chips):
            copy(1 + j, (*chip, mc), me).wait_recv()
            passed[j].start()
        copy(0, sibling, me).wait_recv()
        for j, chip in enumerate(chips):
            copy(4 + j, (*chip, 1 - mc), me).wait_recv()
        for cp in first + passed:
            cp.wait_send()
        mine.wait()

    any_spec = pl.BlockSpec(memory_space=pl.ANY)
    return pl.pallas_call(
        body, name=name, in_specs=[any_spec], out_specs=any_spec,
        out_shape=jax.ShapeDtypeStruct((N_DEV,) + tuple(shape), x.dtype),
        scratch_shapes=[pltpu.SemaphoreType.DMA((N_DEV - 1,)), pltpu.SemaphoreType.DMA((N_DEV - 1,)), pltpu.SemaphoreType.DMA],
    )(x)


def _pcall(exch):
    if exch is None:
        return pl.pallas_call

    def make(body, *, name, grid, in_specs, out_specs, out_shape, compiler_params, scratch_shapes=()):
        multi = isinstance(out_shape, (list, tuple))
        out_specs_l = list(out_specs) if multi else [out_specs]
        out_shape_l = list(out_shape) if multi else [out_shape]
        n_in, n_out, n_x, n_s = len(in_specs), len(out_specs_l), exch.n, len(scratch_shapes)

        def hosted(*refs):
            ins, x_refs = refs[:n_in], refs[n_in:n_in + n_x]
            o0 = n_in + n_x
            outs, xo_refs = refs[o0:o0 + n_out], refs[o0 + n_out:o0 + n_out + n_x]
            s0 = o0 + n_out + n_x
            own_scratch, sems = refs[s0:s0 + n_s], refs[s0 + n_s:]
            ids = [pl.program_id(i) for i in range(len(grid))]
            first = functools.reduce(jnp.logical_and, [i == 0 for i in ids])
            last = functools.reduce(jnp.logical_and, [i == g - 1 for i, g in zip(ids, grid)])

            @pl.when(first)
            def _():
                exch.start(x_refs, xo_refs, *sems)

            body(*ins, *outs, *own_scratch)

            @pl.when(last)
            def _():
                exch.wait(x_refs, xo_refs, *sems)

        any_spec = pl.BlockSpec(memory_space=pl.ANY)
        call = pl.pallas_call(
            hosted, name=name, grid=grid, in_specs=list(in_specs) + [any_spec] * n_x, out_specs=out_specs_l + [any_spec] * n_x,
            out_shape=out_shape_l + exch.out_shapes(), scratch_shapes=list(scratch_shapes) + exch.scratch(),
            compiler_params=compiler_params)
        return lambda *args: call(*args, *exch.operands())

    return make


def _adaln_fwd(cfg, cc_all, w_ada):
    d = cfg.D
    L, _, wa = w_ada.shape

    def body(c_ref, w_ref, o_ref):
        c = c_ref[...]
        a = (c * jax.nn.sigmoid(c)).astype(BF16)
        for l in range(L):
            m = _dot(a, w_ref[l].astype(BF16))
            for p in range(N_DEV):
                o_ref[p, l] = m[8 * p:8 * (p + 1)]

    return pl.pallas_call(
        body, name="adaln_fwd", grid=(1,),
        in_specs=[_full((8 * N_DEV, d)), _full((L, d, wa))],
        out_specs=_full((N_DEV, L, 8, wa)),
        out_shape=jax.ShapeDtypeStruct((N_DEV, L, 8, wa), F32), compiler_params=_params(1),
    )(cc_all, w_ada)


def _adaln_join(cfg, parts, b_ada):
    d = cfg.D
    _, L, _, wa = parts.shape

    def body(p_ref, b_ref, o_ref):
        for l in range(L):
            for j in range(N_DEV):
                o_ref[l, :, j * wa:(j + 1) * wa] = p_ref[j, l] + b_ref[l, :, j * wa:(j + 1) * wa]

    return pl.pallas_call(
        body, name="adaln_join", grid=(1,),
        in_specs=[_full((N_DEV, L, 8, wa)), _full((L, 1, 6 * d))],
        out_specs=_full((L, 8, 6 * d)),
        out_shape=jax.ShapeDtypeStruct((L, 8, 6 * d), F32), compiler_params=_params(1),
    )(parts, b_ada)


def _in_proj_fwd(cfg, l, x, gn, mod, w_in, exch=None):
    B, T, D, IN, tm, nC = cfg.B, cfg.T, cfg.D, cfg.IN, cfg.tm, cfg.nC

    def body(x_ref, gn_ref, mod_ref, w_ref, z_ref, h_ref):
        b, t = pl.program_id(0), pl.program_id(1)
        row = jnp.where(t < nC, B, b)
        h, _, _ = _modulate(x_ref[0], gn_ref[0], _mod_row(mod_ref, row, 0, D), _mod_row(mod_ref, row, 1, D))
        hb = h.astype(BF16)
        h_ref[0] = hb
        z_ref[0] = _dot(hb, w_ref[...])

    return _pcall(exch)(
        body, name="in_proj_fwd", grid=(B, cfg.nT),
        in_specs=[pl.BlockSpec((1, tm, D), lambda b, t: (b, t, 0)), _layer(l, D), _full((8, 6 * D)), _full((D, IN))],
        out_specs=[pl.BlockSpec((1, tm, IN), lambda b, t: (b, t, 0)), pl.BlockSpec((1, tm, D), lambda b, t: (b, t, 0))],
        out_shape=[jax.ShapeDtypeStruct((B, T, IN), F32), jax.ShapeDtypeStruct((B, T, D), BF16)],
        compiler_params=_params(2),
    )(x, gn, mod, w_in)


def _head_indicator():
    r = lax.broadcasted_iota(jnp.int32, (LANES, LANES), 0) // HEAD
    c = lax.broadcasted_iota(jnp.int32, (LANES, LANES), 1) // HEAD
    return jnp.where(r == c, 1.0, 0.0).astype(BF16)


def _head_sum(x, ind):
    hi = x.astype(BF16)
    lo = (x - hi.astype(F32)).astype(BF16)
    return _dot(hi, ind) + _dot(lo, ind)


def _pair_swap(y):
    lane = lax.broadcasted_iota(jnp.int32, y.shape, 1)
    return jnp.where(lane % 2 == 0, pltpu.roll(y, LANES - 1, 1), pltpu.roll(y, 1, 1))


_QK_CHUNKS = (0, 1, 2, 3, 5, 6, 7, 8)
_Q_CHUNKS = (0, 1, 2, 5, 6, 7)


def _qknorm_fwd(cfg, l, z, gvec, cosf, sins):
    B, T, tm = cfg.B, cfg.T, cfg.tm

    def body(z_ref, g_ref, cos_ref, sin_ref, qa_ref, ka_ref, qc_ref, kc_ref):
        ind = _head_indicator()
        cos, sin = cos_ref[...], sin_ref[...]

        def chunk(c):
            x = z_ref[0, :, c * LANES:(c + 1) * LANES]
            ss = _head_sum(x * x, ind)
            y = x * lax.rsqrt(ss * (1.0 / HEAD) + EPS) * g_ref[0, :, c * LANES:(c + 1) * LANES]
            out = y * cos + _pair_swap(y) * sin
            return (out * QSCALE if c in _Q_CHUNKS else out).astype(BF16)

        qa_ref[0] = jnp.concatenate([chunk(0), chunk(1), chunk(2)], axis=-1)
        ka_ref[0] = chunk(3)
        qc_ref[0] = jnp.concatenate([chunk(5), chunk(6), chunk(7)], axis=-1)
        kc_ref[0] = chunk(8)

    row = lambda w: pl.BlockSpec((1, tm, w), lambda b, t: (b, t, 0))
    tab = pl.BlockSpec((tm, LANES), lambda b, t: (t, 0))
    return pl.pallas_call(
        body, name="qknorm_fwd", grid=(B, cfg.nT),
        in_specs=[row(QKV_W), _layer(l, QKV_W), tab, tab],
        out_specs=[row(QW), row(KVW), row(QW), row(KVW)],
        out_shape=[jax.ShapeDtypeStruct((B, T, w), BF16) for w in (QW, KVW, QW, KVW)],
        compiler_params=_params(2),
    )(z, gvec, cosf, sins)


def _attn_scores(cfg, tl, q, k_ref, v_ref, sink_ref, h, loc, window, sink, lse=None):
    S, N, tq = cfg.S, cfg.N, cfg.tm
    hs = slice(h * HEAD, (h + 1) * HEAD)
    qs = jnp.concatenate([q[:, (GROUP * h + g) * HEAD:(GROUP * h + g + 1) * HEAD] for g in range(GROUP)], axis=0)
    lo = None
    if not loc:
        kk = k_ref[0, 0:N, :][:, hs]
        vv = v_ref[0, 0:N, :].astype(BF16)[:, hs]
    elif not window:
        kk = k_ref[0][:, hs]
        vv = v_ref[0].astype(BF16)[:, hs]
    else:
        W = tq + 2 * WINDOW
        lo = pl.multiple_of(jnp.clip(tl * tq - WINDOW, 0, S - W), LANES)
        kk = jnp.concatenate([k_ref[0, 0:N, :], k_ref[0, pl.ds(N + lo, W), :]], axis=0)[:, hs]
        vv = jnp.concatenate([v_ref[0, 0:N, :], v_ref[0, pl.ds(N + lo, W), :]], axis=0).astype(BF16)[:, hs]
    st = _dot_nt(kk, qs)
    if window:
        krow = lax.broadcasted_iota(jnp.int32, st.shape, 0)
        qpos = tl * tq + lax.broadcasted_iota(jnp.int32, st.shape, 1) % tq
        st = jnp.where((krow < N) | (jnp.abs(qpos - (lo + krow - N)) <= WINDOW), st, NEG)
    sk = None
    if sink:
        colg = lax.broadcasted_iota(jnp.int32, (1, GROUP * tq), 1) // tq
        sk = jnp.zeros((1, GROUP * tq), F32)
        for g in range(GROUP):
            j = GROUP * h + g
            sk = jnp.where(colg == g, sink_ref[0, 0:1, j:j + 1], sk)
    if lse is not None:
        return qs, kk, vv, jnp.exp(st - lse), None, (jnp.exp(sk - lse) if sink else None), lo, lse
    m = jnp.max(st, axis=0, keepdims=True)
    if sink:
        m = jnp.maximum(m, sk)
    e = jnp.exp(st - m)
    l = jnp.sum(e, axis=0, keepdims=True)
    e_s = None
    if sink:
        e_s = jnp.exp(sk - m)
        l = l + e_s
    return qs, kk, vv, e, 1.0 / l, e_s, lo, m + jnp.log(l)


def _attn_fwd(cfg, l, q, k, z, vblock, sink8, *, window, sink, ctx_attend, name, exch=None):
    B, T, tq, nC = cfg.B, cfg.T, cfg.tm, cfg.nC

    def body(q_ref, k_ref, v_ref, sink_ref, o_ref, lse_ref):
        t = pl.program_id(1)

        def run(loc):
            q_t = q_ref[0]
            outs = [None] * N_QHEADS
            lses = [None] * N_QHEADS
            for h in range(N_KV):
                _, _, vv, e, inv, _, _, lse = _attn_scores(cfg, t - nC, q_t, k_ref, v_ref, sink_ref, h, loc, window and loc, sink)
                o = (_dot_tn(vv, e.astype(BF16)) * inv).T
                for g in range(GROUP):
                    outs[GROUP * h + g] = o[g * tq:(g + 1) * tq]
                    lses[GROUP * h + g] = lse[:, g * tq:(g + 1) * tq]
            o_ref[0] = jnp.concatenate(outs, axis=-1).astype(BF16)
            lse_ref[0] = jnp.concatenate(lses + [jnp.zeros((8 - N_QHEADS, tq), F32)], axis=0)

        pl.when(t >= nC)(functools.partial(run, True))
        if ctx_attend:
            pl.when(t < nC)(functools.partial(run, False))
        else:
            @pl.when(t < nC)
            def _():
                o_ref[0] = jnp.zeros((tq, QW), BF16)
                lse_ref[0] = jnp.zeros((8, tq), F32)

    return _pcall(exch)(
        body, name=name, grid=(B, cfg.nT),
        in_specs=[pl.BlockSpec((1, tq, QW), lambda b, t: (b, t, 0)),
                  pl.BlockSpec((1, T, KVW), lambda b, t: (b, 0, 0)),
                  pl.BlockSpec((1, T, KVW), lambda b, t: (b, 0, vblock)),
                  pl.BlockSpec((1, 8, LANES), lambda b, t: (l, 0, 0))],
        out_specs=[pl.BlockSpec((1, tq, QW), lambda b, t: (b, t, 0)), pl.BlockSpec((1, 8, tq), lambda b, t: (b, 0, t))],
        out_shape=[jax.ShapeDtypeStruct((B, T, QW), BF16), jax.ShapeDtypeStruct((B, 8, T), F32)], compiler_params=_params(2),
    )(q, k, z, sink8)


def _pool_geometry(cfg, t):
    tm, N, T, nC = cfg.tm, cfg.N, cfg.T, cfg.nC
    r0 = pl.multiple_of(t * tm, tm)
    isctx = t < nC
    seg_lo = jnp.where(isctx, 0, N)
    seg_hi = jnp.where(isctx, N, T)
    k0 = pl.multiple_of(jnp.clip(t * tm - POOL_HALO, 0, T - cfg.kw), POOL_HALO)
    return r0, seg_lo, seg_hi, k0


def _pool_count(pos, h, seg_lo, seg_hi):
    return jnp.maximum(jnp.minimum(pos + h, seg_hi) - jnp.maximum(pos - h, seg_lo), 1).astype(F32)


def _split_bf16(x):
    hi = x.astype(BF16)
    return hi, (x - hi.astype(F32)).astype(BF16)


def _pool_fwd(cfg, l, z, wp, ps):
    B, T, tm, kw = cfg.B, cfg.T, cfg.tm, cfg.kw

    def body(u_ref, wp_ref, ps_ref, ob_ref, pooled_ref):
        t = pl.program_id(1)
        r0, seg_lo, seg_hi, k0 = _pool_geometry(cfg, t)
        hi, lo = _split_bf16(u_ref[0, pl.ds(k0, kw), :])
        rr = r0 + lax.broadcasted_iota(jnp.int32, (tm, kw), 0)
        cc = k0 + lax.broadcasted_iota(jnp.int32, (tm, kw), 1)
        diff = cc - rr
        inseg = (cc >= seg_lo) & (cc < seg_hi)
        rcol = r0 + lax.broadcasted_iota(jnp.int32, (tm, 1), 0)
        group = lax.broadcasted_iota(jnp.int32, (tm, POOL_W), 1) // HEAD
        acc = jnp.zeros((tm, POOL_W), F32)
        for gi, w in enumerate(POOL_WINDOWS):
            h = w // 2
            band = jnp.where((diff >= -h) & (diff <= h - 1) & inseg, 1.0, 0.0).astype(BF16)
            tot = _dot(band, hi) + _dot(band, lo)
            acc = jnp.where(group == gi, tot / _pool_count(rcol, h, seg_lo, seg_hi), acc)
        pooled = (acc - u_ref[0, pl.ds(r0, tm), :]).astype(BF16)
        pooled_ref[0] = pooled
        ob_ref[0] = (_dot(pooled, wp_ref[0]) * ps_ref[0]).astype(BF16)

    row = pl.BlockSpec((1, tm, POOL_W), lambda b, t: (b, t, 0))
    return pl.pallas_call(
        body, name="pool_fwd", grid=(B, cfg.nT),
        in_specs=[pl.BlockSpec((1, T, POOL_W), lambda b, t: (b, 0, QKV_W // POOL_W)),
                  pl.BlockSpec((1, POOL_W, POOL_W), lambda b, t: (l, 0, 0)), _layer(l, POOL_W)],
        out_specs=[row, row],
        out_shape=[jax.ShapeDtypeStruct((B, T, POOL_W), BF16)] * 2, compiler_params=_params(2),
    )(z, wp, ps)


def _gate_specs(cfg):
    tm, gw = cfg.tm, cfg.gw
    first = GATE0 // gw
    return [pl.BlockSpec((1, tm, gw), functools.partial(lambda b, t, j: (b, t, j), j=first + i)) for i in range(3 * cfg.D // gw)]


def _read_gates(cfg, gate_refs):
    per = cfg.D // cfg.gw
    return [jnp.concatenate([gate_refs[k * per + i][0] for i in range(per)], axis=-1) for k in range(3)]


def _merge_fwd(cfg, x, oa, ob, oc, z, mod, wa, wb, wc, wo, *, ctx_active, exch=None):
    B, T, D, tm, nC = cfg.B, cfg.T, cfg.D, cfg.tm, cfg.nC
    ng = 3 * D // cfg.gw

    def body(x_ref, oa_ref, ob_ref, oc_ref, *rest):
        gate_refs = rest[:ng]
        mod_ref, wa_ref, wb_ref, wc_ref, wo_ref, x1_ref, mgo_ref = rest[ng:]
        b, t = pl.program_id(0), pl.program_id(1)

        def compute():
            row = jnp.where(t < nC, B, b)
            ga, gb, gc = _read_gates(cfg, gate_refs)
            y = (jax.nn.sigmoid(ga) * _dot(oa_ref[0], wa_ref[...])
                 + jax.nn.sigmoid(gb) * _dot(ob_ref[0], wb_ref[...])
                 + jax.nn.sigmoid(gc) * _dot(oc_ref[0], wc_ref[...]))
            mo = _dot(y.astype(BF16), wo_ref[...])
            mgo_ref[0] = mo.astype(BF16)
            x1_ref[0] = x_ref[0] + _mod_row(mod_ref, row, 2, D) * mo

        if ctx_active:
            compute()
        else:
            pl.when(t >= nC)(compute)

            @pl.when(t < nC)
            def _():
                mgo_ref[0] = jnp.zeros((tm, D), BF16)
                x1_ref[0] = x_ref[0]

    row = lambda w: pl.BlockSpec((1, tm, w), lambda b, t: (b, t, 0))
    return _pcall(exch)(
        body, name="merge_fwd", grid=(B, cfg.nT),
        in_specs=[row(D), row(QW), row(POOL_W), row(QW)] + _gate_specs(cfg)
        + [_full((8, 6 * D)), _full((QW, D)), _full((POOL_W, D)), _full((QW, D)), _full((D, D))],
        out_specs=[row(D), row(D)],
        out_shape=[jax.ShapeDtypeStruct((B, T, D), F32), jax.ShapeDtypeStruct((B, T, D), BF16)],
        compiler_params=_params(2),
    )(x, oa, ob, oc, *([z] * ng), mod, wa, wb, wc, wo)


def _w1_apply(hb, w1_ref):
    return jnp.concatenate([_dot(hb, w1_ref[d]) for d in range(N_DEV)], axis=-1)


def _mlp_fwd(cfg, l, x1, gn, mod, w1, w2, *, ctx_active, target=None, exch=None):
    B, T, D, F, tm, nC = cfg.B, cfg.T, cfg.D, cfg.F, cfg.tm, cfg.nC
    assert target is None or not ctx_active

    def body(x_ref, gn_ref, mod_ref, w1_ref, w2_ref, *rest):
        if target is None:
            x2_ref, mo_ref, r_ref = rest
        else:
            tgt_ref, x2_ref, mo_ref, r_ref, sse_ref = rest
            _acc_init([sse_ref])
        b, t = pl.program_id(0), pl.program_id(1)

        def compute():
            row = jnp.where(t < nC, B, b)
            x = x_ref[0]
            h, _, _ = _modulate(x, gn_ref[0], _mod_row(mod_ref, row, 3, D), _mod_row(mod_ref, row, 4, D))
            a = jnp.maximum(_w1_apply(h.astype(BF16), w1_ref), 0.0)
            rb = (a * a).astype(BF16)
            r_ref[0] = rb
            mo = _dot(rb, w2_ref[...])
            mo_ref[0] = mo.astype(BF16)
            x2 = x + _mod_row(mod_ref, row, 5, D) * mo
            if target is None:
                x2_ref[0] = x2
            else:
                err = x2 - tgt_ref[0]
                x2_ref[0] = err * (1.0 / D)
                sse_ref[...] += jnp.sum(err * err)

        if ctx_active:
            compute()
        else:
            pl.when(t >= nC)(compute)

            @pl.when(t < nC)
            def _():
                mo_ref[0] = jnp.zeros((tm, D), BF16)
                r_ref[0] = jnp.zeros((tm, F), BF16)
                x2_ref[0] = x_ref[0] if target is None else jnp.zeros((tm, D), F32)

    row = pl.BlockSpec((1, tm, D), lambda b, t: (b, t, 0))
    in_specs = [row, _layer(l, D), _full((8, 6 * D)), _full((N_DEV, D, F // N_DEV)), _full((F, D))]
    out_specs = [row, row, pl.BlockSpec((1, tm, F), lambda b, t: (b, t, 0))]
    out_shape = [jax.ShapeDtypeStruct((B, T, D), F32), jax.ShapeDtypeStruct((B, T, D), BF16), jax.ShapeDtypeStruct((B, T, F), BF16)]
    args = [x1, gn, mod, w1, w2]
    if target is not None:
        in_specs.append(pl.BlockSpec((1, tm, D), lambda b, t: (b, jnp.maximum(t - nC, 0), 0)))
        out_specs.append(_full((8, LANES)))
        out_shape.append(jax.ShapeDtypeStruct((8, LANES), F32))
        args.append(target)
    return _pcall(exch)(
        body, name="mlp_fwd", grid=(B, cfg.nT), in_specs=in_specs, out_specs=out_specs, out_shape=out_shape,
        compiler_params=_params(2),
    )(*args)


def _acc_init(refs):
    b, t = pl.program_id(0), pl.program_id(1)

    @pl.when((b == 0) & (t == 0))
    def _():
        for ref in refs:
            ref[...] = jnp.zeros(ref.shape, ref.dtype)


def _mlp_bwd(cfg, l, x1, dx2, mo, r, gn, mod, w1, w2, *, ctx_active, exch=None):
    B, T, D, F, tm, nC = cfg.B, cfg.T, cfg.D, cfg.F, cfg.tm, cfg.nC
    ws = F // N_DEV

    def body(x_ref, dx_ref, mo_ref, r_ref, gn_ref, mod_ref, w1_ref, w2_ref, dx1_ref, h_ref, da_ref, dout_ref, dmod_ref, dgn_ref):
        b, t = pl.program_id(0), pl.program_id(1)
        _acc_init([dmod_ref, dgn_ref])

        def compute():
            row = jnp.where(t < nC, B, b)
            gn = gn_ref[0]
            scale = _mod_row(mod_ref, row, 4, D)
            h, xhat, rstd = _modulate(x_ref[0], gn, _mod_row(mod_ref, row, 3, D), scale)
            hb = h.astype(BF16)
            dx = dx_ref[0]
            dout = (dx * _mod_row(mod_ref, row, 5, D)).astype(BF16)
            da = (_dot_nt(dout, w2_ref[...]) * (2.0 * jnp.sqrt(r_ref[0].astype(F32)))).astype(BF16)
            dh = _dot_nt(da[:, 0:ws], w1_ref[0])
            for d in range(1, N_DEV):
                dh = dh + _dot_nt(da[:, d * ws:(d + 1) * ws], w1_ref[d])
            dxn, d_shift, d_scale, d_gn = _modulate_bwd(dh, xhat, rstd, gn, scale)
            dx1_ref[0] = dx + dxn
            h_ref[0] = hb
            da_ref[0] = da
            dout_ref[0] = dout
            d_gate = jnp.sum(dx * mo_ref[0].astype(F32), axis=0, keepdims=True)
            dmod_ref[pl.ds(row, 1), :] += jnp.concatenate([d_shift, d_scale, d_gate], axis=-1)
            dgn_ref[0:1, :] += d_gn

        if ctx_active:
            compute()
        else:
            pl.when(t >= nC)(compute)

            @pl.when(t < nC)
            def _():
                dx1_ref[0] = dx_ref[0]
                h_ref[0] = jnp.zeros((tm, D), BF16)
                da_ref[0] = jnp.zeros((tm, F), BF16)
                dout_ref[0] = jnp.zeros((tm, D), BF16)

    row = lambda w: pl.BlockSpec((1, tm, w), lambda b, t: (b, t, 0))
    sds = lambda w, dt: jax.ShapeDtypeStruct((B, T, w), dt)
    return _pcall(exch)(
        body, name="mlp_bwd", grid=(B, cfg.nT),
        in_specs=[row(D), row(D), row(D), row(F), _layer(l, D), _full((8, 6 * D)), _full((N_DEV, D, ws)), _full((F, D))],
        out_specs=[row(D), row(D), row(F), row(D), _full((8, 3 * D)), _full((8, D))],
        out_shape=[sds(D, F32), sds(D, BF16), sds(F, BF16), sds(D, BF16),
                   jax.ShapeDtypeStruct((8, 3 * D), F32), jax.ShapeDtypeStruct((8, D), F32)],
        compiler_params=_params(2),
    )(x1, dx2, mo, r, gn, mod, w1, w2)


def _matmul_tn(a, g, name, *, by_shard):
    R, Ka = a.shape
    Ng = g.shape[1]
    tr = next(c for c in (2304, 1024, 512, 256, 128, 64, 32, 16, 8) if R % c == 0)
    tka = Ka if Ka <= 1024 else 1024
    if by_shard:
        ws = Ng // N_DEV
        per = next(c for c in (8, 4, 2, 1) if c * ws <= 1152 or c == 1)
        tn = per * ws
    else:
        tn = next(c for c in (1152, 1024, 768, 512, 384, 256, 128) if Ng % c == 0)
    assert Ka % tka == 0 and tn % LANES == 0
    nr = R // tr

    def body(a_ref, g_ref, o_ref, acc_ref):
        r = pl.program_id(2)

        @pl.when(r == 0)
        def _():
            acc_ref[...] = jnp.zeros(acc_ref.shape, F32)

        acc_ref[...] += _dot_tn(a_ref[...], g_ref[...])

        @pl.when(r == nr - 1)
        def _():
            if by_shard:
                for d in range(per):
                    o_ref[d] = acc_ref[:, d * ws:(d + 1) * ws].astype(BF16)
            else:
                o_ref[...] = acc_ref[...].astype(BF16)

    if by_shard:
        out_spec = pl.BlockSpec((per, tka, ws), lambda i, j, r: (j, i, 0))
        out_shape = jax.ShapeDtypeStruct((N_DEV, Ka, ws), BF16)
    else:
        out_spec = pl.BlockSpec((tka, tn), lambda i, j, r: (i, j))
        out_shape = jax.ShapeDtypeStruct((Ka, Ng), BF16)
    return pl.pallas_call(
        body, name=name, grid=(Ka // tka, Ng // tn, nr),
        in_specs=[pl.BlockSpec((tr, tka), lambda i, j, r: (r, i)), pl.BlockSpec((tr, tn), lambda i, j, r: (r, j))],
        out_specs=out_spec, out_shape=out_shape, scratch_shapes=[pltpu.VMEM((tka, tn), F32)], compiler_params=_params(3),
    )(a, g)


def _merge_bwd(cfg, dx1, mgo, oa, ob, oc, z, mod, wa, wb, wc, wo, *, ctx_active, exch=None):
    B, T, D, tm, nC = cfg.B, cfg.T, cfg.D, cfg.tm, cfg.nC
    ng = 3 * D // cfg.gw

    def body(dx_ref, mgo_ref, oa_ref, ob_ref, oc_ref, *rest):
        gate_refs = rest[:ng]
        (mod_ref, wa_ref, wb_ref, wc_ref, wo_ref,
         doa_ref, dob_ref, doc_ref, dpa_ref, dpb_ref, dpc_ref, y_ref, dmo_ref, dzg_ref, dg1_ref) = rest[ng:]
        b, t = pl.program_id(0), pl.program_id(1)
        _acc_init([dg1_ref])

        def compute():
            row = jnp.where(t < nC, B, b)
            dx = dx_ref[0]
            dg1_ref[pl.ds(row, 1), :] += jnp.sum(dx * mgo_ref[0].astype(F32), axis=0, keepdims=True)
            dmo = (dx * _mod_row(mod_ref, row, 2, D)).astype(BF16)
            dmo_ref[0] = dmo
            dy = _dot_nt(dmo, wo_ref[...])
            gates = _read_gates(cfg, gate_refs)
            y = jnp.zeros((tm, D), F32)
            dgs = []
            for gate, o_ref, w_ref, do_ref, dp_ref in ((gates[0], oa_ref, wa_ref, doa_ref, dpa_ref),
                                                      (gates[1], ob_ref, wb_ref, dob_ref, dpb_ref),
                                                      (gates[2], oc_ref, wc_ref, doc_ref, dpc_ref)):
                s = jax.nn.sigmoid(gate)
                p = _dot(o_ref[0], w_ref[...])
                y = y + s * p
                dp = (dy * s).astype(BF16)
                dp_ref[0] = dp
                do_ref[0] = _dot_nt(dp, w_ref[...]).astype(BF16)
                dgs.append((dy * p * s * (1.0 - s)).astype(BF16))
            y_ref[0] = y.astype(BF16)
            dzg_ref[0] = jnp.concatenate(dgs, axis=-1)

        if ctx_active:
            compute()
        else:
            pl.when(t >= nC)(compute)

            @pl.when(t < nC)
            def _():
                for ref in (doa_ref, dob_ref, doc_ref, dpa_ref, dpb_ref, dpc_ref, y_ref, dmo_ref, dzg_ref):
                    ref[...] = jnp.zeros(ref.shape, ref.dtype)

    row = lambda w: pl.BlockSpec((1, tm, w), lambda b, t: (b, t, 0))
    sds = lambda w: jax.ShapeDtypeStruct((B, T, w), BF16)
    return _pcall(exch)(
        body, name="merge_bwd", grid=(B, cfg.nT),
        in_specs=[row(D), row(D), row(QW), row(POOL_W), row(QW)] + _gate_specs(cfg)
        + [_full((8, 6 * D)), _full((QW, D)), _full((POOL_W, D)), _full((QW, D)), _full((D, D))],
        out_specs=[row(QW), row(POOL_W), row(QW), row(D), row(D), row(D), row(D), row(D), row(3 * D), _full((8, D))],
        out_shape=[sds(QW), sds(POOL_W), sds(QW), sds(D), sds(D), sds(D), sds(D), sds(D), sds(3 * D),
                   jax.ShapeDtypeStruct((8, D), F32)],
        compiler_params=_params(2),
    )(dx1, mgo, oa, ob, oc, *([z] * ng), mod, wa, wb, wc, wo)


def _attn_bwd(cfg, l, q, k, z, vblock, sink8, do, lse, *, window, sink, ctx_attend, name, exch=None):
    B, S, N, T, tq, nC = cfg.B, cfg.S, cfg.N, cfg.T, cfg.tm, cfg.nC

    def body(q_ref, k_ref, v_ref, sink_ref, do_ref, lse_ref, dq_ref, dk_ref, dv_ref, dsink_ref):
        b, t = pl.program_id(0), pl.program_id(1)
        _acc_init([dsink_ref])

        @pl.when(t == 0)
        def _():
            dk_ref[...] = jnp.zeros(dk_ref.shape, F32)
            dv_ref[...] = jnp.zeros(dv_ref.shape, F32)

        def run(loc):
            q_t = q_ref[0]
            do_t = do_ref[0]
            dqs = [None] * N_QHEADS
            dks, dvs = [], []
            dsink_row = jnp.zeros((1, LANES), F32)
            lane = lax.broadcasted_iota(jnp.int32, (1, LANES), 1)
            lo = None
            for h in range(N_KV):
                lse = jnp.concatenate([lse_ref[0, GROUP * h + g:GROUP * h + g + 1, :] for g in range(GROUP)], axis=1)
                qs, kk, vv, p, _, p_s, lo, _ = _attn_scores(cfg, t - nC, q_t, k_ref, v_ref, sink_ref, h, loc, window and loc, sink, lse=lse)
                dos = jnp.concatenate([do_t[:, (GROUP * h + g) * HEAD:(GROUP * h + g + 1) * HEAD] for g in range(GROUP)], axis=0)
                dp = _dot_nt(vv, dos)
                delta = jnp.sum(p * dp, axis=0, keepdims=True)
                ds = (p * (dp - delta)).astype(BF16)
                dq = _dot_tn(kk, ds).T
                dks.append(_dot(ds, qs))
                dvs.append(_dot(p.astype(BF16), dos))
                if sink:
                    dsk = -p_s * delta
                    for g in range(GROUP):
                        tot = jnp.sum(dsk[:, g * tq:(g + 1) * tq], axis=1, keepdims=True)
                        dsink_row = dsink_row + jnp.where(lane == GROUP * h + g, tot, 0.0)
                for g in range(GROUP):
                    dqs[GROUP * h + g] = dq[g * tq:(g + 1) * tq] * QSCALE
            dq_ref[0] = jnp.concatenate(dqs, axis=-1)
            dk = jnp.concatenate(dks, axis=-1)
            dv = jnp.concatenate(dvs, axis=-1)
            if loc and not window:
                dk_ref[0] += dk
                dv_ref[0] += dv
            else:
                dk_ref[0, 0:N, :] += dk[0:N]
                dv_ref[0, 0:N, :] += dv[0:N]
                if loc:
                    W = tq + 2 * WINDOW
                    dk_ref[0, pl.ds(N + lo, W), :] += dk[N:]
                    dv_ref[0, pl.ds(N + lo, W), :] += dv[N:]
            if sink:
                dsink_ref[0:1, :] += dsink_row

        pl.when(t >= nC)(functools.partial(run, True))
        if ctx_attend:
            pl.when(t < nC)(functools.partial(run, False))
        else:
            @pl.when(t < nC)
            def _():
                dq_ref[0] = jnp.zeros((tq, QW), F32)

    kv = pl.BlockSpec((1, T, KVW), lambda b, t: (b, 0, 0))
    qrow = pl.BlockSpec((1, tq, QW), lambda b, t: (b, t, 0))
    return _pcall(exch)(
        body, name=name, grid=(B, cfg.nT),
        in_specs=[qrow, kv, pl.BlockSpec((1, T, KVW), lambda b, t: (b, 0, vblock)),
                  pl.BlockSpec((1, 8, LANES), lambda b, t: (l, 0, 0)), qrow, pl.BlockSpec((1, 8, tq), lambda b, t: (b, 0, t))],
        out_specs=[qrow, kv, kv, _full((8, LANES))],
        out_shape=[jax.ShapeDtypeStruct((B, T, QW), F32), jax.ShapeDtypeStruct((B, T, KVW), F32),
                   jax.ShapeDtypeStruct((B, T, KVW), F32), jax.ShapeDtypeStruct((8, LANES), F32)],
        compiler_params=_params(2),
    )(q, k, z, sink8, do, lse)


def _qknorm_bwd(cfg, l, z, gvec, cosf, sins, dqa, dka, dva, dqc, dkc, dvc):
    B, T, tm = cfg.B, cfg.T, cfg.tm

    def body(z_ref, g_ref, cos_ref, sin_ref, dqa_ref, dka_ref, dva_ref, dqc_ref, dkc_ref, dvc_ref, dz_ref, dg_ref):
        _acc_init([dg_ref])
        ind = _head_indicator()
        cos, sin = cos_ref[...], sin_ref[...]
        dqa_t, dqc_t = dqa_ref[0], dqc_ref[0]
        douts = {0: dqa_t[:, 0:128], 1: dqa_t[:, 128:256], 2: dqa_t[:, 256:384], 3: dka_ref[0],
                 5: dqc_t[:, 0:128], 6: dqc_t[:, 128:256], 7: dqc_t[:, 256:384], 8: dkc_ref[0]}
        pieces = []
        dgs = []
        for c in range(QKV_W // LANES):
            if c not in douts:
                pieces.append(dva_ref[0] if c == 4 else dvc_ref[0])
                dgs.append(jnp.zeros((1, LANES), F32))
                continue
            x = z_ref[0, :, c * LANES:(c + 1) * LANES]
            g = g_ref[0, :, c * LANES:(c + 1) * LANES]
            ss = _head_sum(x * x, ind)
            rstd = lax.rsqrt(ss * (1.0 / HEAD) + EPS)
            n = x * rstd
            dout = douts[c]
            dy = dout * cos + _pair_swap(dout * sin)
            dgs.append(jnp.sum(dy * n, axis=0, keepdims=True))
            dn = dy * g
            mean = _head_sum(dn * n, ind) * (1.0 / HEAD)
            pieces.append(rstd * (dn - n * mean))
        dz_ref[0] = jnp.concatenate(pieces, axis=-1).astype(BF16)
        dg_ref[0:1, :] += jnp.concatenate(dgs, axis=-1)

    row = lambda w: pl.BlockSpec((1, tm, w), lambda b, t: (b, t, 0))
    tab = pl.BlockSpec((tm, LANES), lambda b, t: (t, 0))
    return pl.pallas_call(
        body, name="qknorm_bwd", grid=(B, cfg.nT),
        in_specs=[row(QKV_W), _layer(l, QKV_W), tab, tab, row(QW), row(KVW), row(KVW), row(QW), row(KVW), row(KVW)],
        out_specs=[row(QKV_W), _full((8, QKV_W))],
        out_shape=[jax.ShapeDtypeStruct((B, T, QKV_W), BF16), jax.ShapeDtypeStruct((8, QKV_W), F32)],
        compiler_params=_params(2),
    )(z, gvec, cosf, sins, dqa, dka, dva, dqc, dkc, dvc)


def _pool_bwd(cfg, l, dob, pooled, wp, ps):
    B, T, tm, kw = cfg.B, cfg.T, cfg.tm, cfg.kw

    def body(dob_ref, pooled_ref, wp_ref, ps_ref, du_ref, dwp_ref, dps_ref):
        t = pl.program_id(1)
        _acc_init([dwp_ref, dps_ref])
        r0, seg_lo, seg_hi, k0 = _pool_geometry(cfg, t)
        ps = ps_ref[0]
        wp = wp_ref[0]
        dmix = dob_ref[0, pl.ds(r0, tm), :].astype(F32)
        pooled = pooled_ref[0]
        dps_ref[0:1, :] += jnp.sum(dmix * _dot(pooled, wp), axis=0, keepdims=True)
        dpm = (dmix * ps).astype(BF16)
        dwp_ref[...] += _dot_tn(pooled, dpm)
        dpooled_t = _dot_nt(dpm, wp)
        dpm_w = (dob_ref[0, pl.ds(k0, kw), :].astype(F32) * ps).astype(BF16)
        dpooled_w = _dot_nt(dpm_w, wp)
        rr = r0 + lax.broadcasted_iota(jnp.int32, (tm, kw), 0)
        cc = k0 + lax.broadcasted_iota(jnp.int32, (tm, kw), 1)
        diff = rr - cc
        inseg = (cc >= seg_lo) & (cc < seg_hi)
        ccol = k0 + lax.broadcasted_iota(jnp.int32, (kw, 1), 0)
        group = lax.broadcasted_iota(jnp.int32, (tm, POOL_W), 1) // HEAD
        acc = jnp.zeros((tm, POOL_W), F32)
        for gi, w in enumerate(POOL_WINDOWS):
            h = w // 2
            band_t = jnp.where((diff >= -h) & (diff <= h - 1) & inseg, 1.0, 0.0).astype(BF16)
            hi, lo = _split_bf16(dpooled_w / _pool_count(ccol, h, seg_lo, seg_hi))
            acc = jnp.where(group == gi, _dot(band_t, hi) + _dot(band_t, lo), acc)
        du_ref[0] = (acc - dpooled_t).astype(BF16)

    row = pl.BlockSpec((1, tm, POOL_W), lambda b, t: (b, t, 0))
    return pl.pallas_call(
        body, name="pool_bwd", grid=(B, cfg.nT),
        in_specs=[pl.BlockSpec((1, T, POOL_W), lambda b, t: (b, 0, 0)), row,
                  pl.BlockSpec((1, POOL_W, POOL_W), lambda b, t: (l, 0, 0)), _layer(l, POOL_W)],
        out_specs=[row, _full((POOL_W, POOL_W)), _full((8, POOL_W))],
        out_shape=[jax.ShapeDtypeStruct((B, T, POOL_W), BF16), jax.ShapeDtypeStruct((POOL_W, POOL_W), F32),
                   jax.ShapeDtypeStruct((8, POOL_W), F32)],
        compiler_params=_params(2),
    )(dob, pooled, wp, ps)


def _in_proj_bwd(cfg, l, dzq, du, dzg, w_in, x, dx1, gn, mod, *, latent_only, exch=None):
    B, S, T, D, IN, tm, nC = cfg.B, cfg.S, cfg.T, cfg.D, cfg.IN, cfg.tm, cfg.nC

    def body(dzq_ref, du_ref, dzg_ref, w_ref, x_ref, dx1_ref, gn_ref, mod_ref, dx0_ref, dz_ref, dmod_ref, dgn_ref):
        b, t = pl.program_id(0), pl.program_id(1)
        _acc_init([dmod_ref, dgn_ref])
        row = jnp.where(t < nC, B, b)
        dz = jnp.concatenate([dzq_ref[0], du_ref[0], dzg_ref[0]], axis=-1)
        dz_ref[0] = dz
        dh = _dot_nt(dz, w_ref[...])
        gn = gn_ref[0]
        scale = _mod_row(mod_ref, row, 1, D)
        _, xhat, rstd = _modulate(x_ref[0], gn, _mod_row(mod_ref, row, 0, D), scale)
        dxn, d_shift, d_scale, d_gn = _modulate_bwd(dh, xhat, rstd, gn, scale)
        dx0_ref[0] = dx1_ref[0] + dxn
        dmod_ref[pl.ds(row, 1), :] += jnp.concatenate([d_shift, d_scale], axis=-1)
        dgn_ref[0:1, :] += d_gn

    row = lambda w: pl.BlockSpec((1, tm, w), lambda b, t: (b, t, 0))
    if latent_only:
        dx0_spec = pl.BlockSpec((1, tm, D), lambda b, t: (b, jnp.maximum(t - nC, 0), 0))
        dx0_shape = jax.ShapeDtypeStruct((B, S, D), F32)
    else:
        dx0_spec, dx0_shape = row(D), jax.ShapeDtypeStruct((B, T, D), F32)
    return _pcall(exch)(
        body, name="in_proj_bwd", grid=(B, cfg.nT),
        in_specs=[row(QKV_W), row(POOL_W), row(3 * D), _full((D, IN)), row(D), row(D), _layer(l, D), _full((8, 6 * D))],
        out_specs=[dx0_spec, row(IN), _full((8, 2 * D)), _full((8, D))],
        out_shape=[dx0_shape, jax.ShapeDtypeStruct((B, T, IN), BF16),
                   jax.ShapeDtypeStruct((8, 2 * D), F32), jax.ShapeDtypeStruct((8, D), F32)],
        compiler_params=_params(2),
    )(dzq, du, dzg, w_in, x, dx1, gn, mod)


def _adaln_bwd(cfg, l, cc_all, dm_all, w_ada):
    d, B = cfg.D, cfg.B
    wa = w_ada.shape[2]

    def body(c_ref, dm_ref, w_ref, dw_ref, dc_ref):
        c = c_ref[...]
        s = jax.nn.sigmoid(c)
        dmb = dm_ref[...].astype(BF16)
        dw_ref[...] = _dot_tn((c * s).astype(BF16), dmb)
        dc = _dot_nt(dmb, w_ref[0].astype(BF16)) * (s * (1.0 + c * (1.0 - s)))
        is_ctx = lax.broadcasted_iota(jnp.int32, (8 * N_DEV, 1), 0) % 8 == B
        dc_ref[...] = jnp.broadcast_to(jnp.sum(jnp.where(is_ctx, dc, 0.0), axis=0, keepdims=True), (8, d))

    return pl.pallas_call(
        body, name="adaln_bwd", grid=(1,),
        in_specs=[_full((8 * N_DEV, d)), _full((8 * N_DEV, wa)), pl.BlockSpec((1, d, wa), lambda *_: (l, 0, 0))],
        out_specs=[_full((d, wa)), _full((8, d))],
        out_shape=[jax.ShapeDtypeStruct((d, wa), F32), jax.ShapeDtypeStruct((8, d), F32)],
        compiler_params=_params(1),
    )(cc_all, dm_all, w_ada)


def _dmod_pack(cfg, dmod_in, dg1, dmod_mlp):
    d = cfg.D
    wa = 6 * d // N_DEV

    def body(din_ref, dg1_ref, dmlp_ref, o_ref, db_ref):
        dm = jnp.concatenate([din_ref[...], dg1_ref[...], dmlp_ref[...]], axis=-1)
        for j in range(N_DEV):
            o_ref[j] = dm[:, j * wa:(j + 1) * wa]
        db_ref[...] = jnp.broadcast_to(jnp.sum(dm, axis=0, keepdims=True), (8, 6 * d))

    return pl.pallas_call(
        body, name="dmod_pack", grid=(1,),
        in_specs=[_full((8, 2 * d)), _full((8, d)), _full((8, 3 * d))],
        out_specs=[_full((N_DEV, 8, wa)), _full((8, 6 * d))],
        out_shape=[jax.ShapeDtypeStruct((N_DEV, 8, wa), F32), jax.ShapeDtypeStruct((8, 6 * d), F32)],
        compiler_params=_params(1),
    )(dmod_in, dg1, dmod_mlp)


def _adam_update(g, w, m, v):
    bc1 = 1.0 - ADAM_B1 ** ADAM_STEP
    bc2 = 1.0 - ADAM_B2 ** ADAM_STEP
    m2 = ADAM_B1 * m + (1.0 - ADAM_B1) * g
    v2 = ADAM_B2 * v + (1.0 - ADAM_B2) * (g * g)
    delta = -ADAM_LR * ((m2 / bc1) / (jnp.sqrt(v2 / bc2) + ADAM_EPS) + ADAM_WD * w)
    return delta, m2, v2


def _sum_parts(p_ref):
    g = p_ref[0].astype(F32)
    for d in range(1, p_ref.shape[0]):
        g = g + p_ref[d].astype(F32)
    return g


def _adamw_sharded(parts, w, m, v, name):
    L, K, W = w.shape
    P = parts[0].shape[0]
    tk = next(c for c in (256, 128, 64, 32, 16, 8) if K % c == 0)

    def body(*refs):
        p_refs = refs[:L]
        w_ref, m_ref, v_ref, g_ref, d_ref, m2_ref, v2_ref = refs[L:]
        layer = pl.program_id(0)

        def run(p_ref):
            g = _sum_parts(p_ref)
            delta, m2, v2 = _adam_update(g, w_ref[0], m_ref[0], v_ref[0])
            g_ref[0] = g
            d_ref[0] = delta
            m2_ref[0] = m2
            v2_ref[0] = v2

        for li in range(L):
            pl.when(layer == li)(functools.partial(run, p_refs[li]))

    blk = pl.BlockSpec((1, tk, W), lambda l, i: (l, i, 0))
    part_spec = lambda li: pl.BlockSpec((P, tk, W), lambda l, i: (0, jnp.where(l == li, i, 0), 0))
    return pl.pallas_call(
        body, name=name, grid=(L, K // tk),
        in_specs=[part_spec(li) for li in range(L)] + [blk, blk, blk],
        out_specs=[blk] * 4, out_shape=[jax.ShapeDtypeStruct((L, K, W), F32)] * 4,
        compiler_params=_params(2),
    )(*parts, w, m, v)


def _adamw_packed(parts, w, m, v, name):
    rows = w.shape[0]
    tr = next(c for c in (256, 128, 64, 32, 16, 8) if rows % c == 0)

    def body(p_ref, w_ref, m_ref, v_ref, g_ref, d_ref, m2_ref, v2_ref):
        g = _sum_parts(p_ref)
        delta, m2, v2 = _adam_update(g, w_ref[...], m_ref[...], v_ref[...])
        g_ref[...] = g
        d_ref[...] = delta
        m2_ref[...] = m2
        v2_ref[...] = v2

    blk = pl.BlockSpec((tr, PACK_W), lambda i: (i, 0))
    return pl.pallas_call(
        body, name=name, grid=(rows // tr,),
        in_specs=[pl.BlockSpec((N_DEV, tr, PACK_W), lambda i: (0, i, 0)), blk, blk, blk],
        out_specs=[blk] * 4, out_shape=[jax.ShapeDtypeStruct((rows, PACK_W), F32)] * 4,
        compiler_params=_params(1),
    )(parts, w, m, v)


_SHARDED = dict(w_ada=True, w_in=True, w_br_a=True, w_br_b=True, w_br_c=True, w_out=False, w_mlp1=True, w_mlp2=False)
_MERGE_WEIGHTS = ("w_br_a", "w_br_b", "w_br_c", "w_out")
_GATHERED = ("w_in",) + _MERGE_WEIGHTS + ("w_mlp1", "w_mlp2")
_KEEP_SHARDS = ("w_mlp1",)
_SMALL = ("c_ctx", "b_ada", "norm1", "norm2", "q_norm_a", "k_norm_a", "q_norm_c", "k_norm_c", "sink_c", "w_pool", "pool_scale")


def _from_shards(name, g):
    n, k, w = g.shape
    if name in _KEEP_SHARDS:
        return g
    if _SHARDED[name]:
        return g.transpose(1, 0, 2).reshape(k, n * w)
    return g.reshape(n * k, w)


def _to_shards(name, g):
    if g.ndim == 3:
        return g
    if _SHARDED[name]:
        k, nw = g.shape
        return g.reshape(k, N_DEV, nw // N_DEV).transpose(1, 0, 2)
    nk, w = g.shape
    return g.reshape(N_DEV, nk // N_DEV, w)


def _pack_small(vals):
    flat = jnp.concatenate([vals[n].reshape(-1) for n in _SMALL])
    rows = -(-flat.shape[0] // (8 * PACK_W)) * 8
    return jnp.pad(flat, (0, rows * PACK_W - flat.shape[0])).reshape(rows, PACK_W)


def _unpack_small(packed, like):
    flat, out, r = packed.reshape(-1), {}, 0
    for n in _SMALL:
        sz = like[n].size
        out[n] = flat[r:r + sz].reshape(like[n].shape)
        r += sz
    return out


def _rope_tables(cfg):
    pos = jnp.arange(cfg.S, dtype=F32)
    r = jnp.floor(pos / GRID_W)
    col = pos - r * GRID_W
    inv = 1.0 / (ROPE_THETA ** (jnp.arange(0, HEAD // 2, 2, dtype=F32) / (HEAD // 2)))
    ang = jnp.concatenate([r[:, None] * inv, col[:, None] * inv], axis=-1)
    cos = jnp.repeat(jnp.cos(ang), 2, axis=-1)
    sin = jnp.repeat(jnp.sin(ang), 2, axis=-1) * jnp.tile(jnp.array([-1.0, 1.0], F32), HEAD // 2)
    cos = jnp.concatenate([jnp.ones((cfg.N, HEAD), F32), cos], axis=0)
    sin = jnp.concatenate([jnp.zeros((cfg.N, HEAD), F32), sin], axis=0)
    return jnp.tile(cos, (1, 2)), jnp.tile(sin, (1, 2))


def _gvec(qa, ka, qc, kc):
    one = jnp.ones((qa.shape[0], KVW), F32)
    t = lambda a, n: jnp.tile(a, (1, n))
    return jnp.concatenate([t(qa, N_QHEADS), t(ka, N_KV), one, t(qc, N_QHEADS), t(kc, N_KV), one], axis=-1)[:, None, :]


def _block_diag(wp):
    L, g, c, _ = wp.shape
    eye = jnp.eye(g, dtype=wp.dtype)
    return (wp[:, :, :, None, :] * eye[None, :, None, :, None]).reshape(L, g * c, g * c)


def _pad8(a):
    return jnp.pad(a, ((0, 8 - a.shape[0]), (0, 0)))


def kernel(x, c, ctx, c_ctx, w_ada, b_ada, norm1, norm2, w_in, q_norm_a, k_norm_a, q_norm_c, k_norm_c, sink_c, w_pool, pool_scale, w_br_a, w_br_b, w_br_c, w_out, w_mlp1, w_mlp2, loss_target, m_c_ctx, m_w_ada, m_b_ada, m_norm1, m_norm2, m_w_in, m_q_norm_a, m_k_norm_a, m_q_norm_c, m_k_norm_c, m_sink_c, m_w_pool, m_pool_scale, m_w_br_a, m_w_br_b, m_w_br_c, m_w_out, m_w_mlp1, m_w_mlp2, v_c_ctx, v_w_ada, v_b_ada, v_norm1, v_norm2, v_w_in, v_q_norm_a, v_k_norm_a, v_q_norm_c, v_k_norm_c, v_sink_c, v_w_pool, v_pool_scale, v_w_br_a, v_w_br_b, v_w_br_c, v_w_out, v_w_mlp1, v_w_mlp2):
    B, S, D = x.shape
    N = ctx.shape[1]
    L = w_ada.shape[0]
    cfg = _Cfg(B, S, N, D)
    T = cfg.T
    weights = dict(c_ctx=c_ctx, w_ada=w_ada, b_ada=b_ada, norm1=norm1, norm2=norm2, w_in=w_in, q_norm_a=q_norm_a,
                   k_norm_a=k_norm_a, q_norm_c=q_norm_c, k_norm_c=k_norm_c, sink_c=sink_c, w_pool=w_pool,
                   pool_scale=pool_scale, w_br_a=w_br_a, w_br_b=w_br_b, w_br_c=w_br_c, w_out=w_out, w_mlp1=w_mlp1, w_mlp2=w_mlp2)
    mom_m = dict(c_ctx=m_c_ctx, w_ada=m_w_ada, b_ada=m_b_ada, norm1=m_norm1, norm2=m_norm2, w_in=m_w_in, q_norm_a=m_q_norm_a,
                 k_norm_a=m_k_norm_a, q_norm_c=m_q_norm_c, k_norm_c=m_k_norm_c, sink_c=m_sink_c, w_pool=m_w_pool,
                 pool_scale=m_pool_scale, w_br_a=m_w_br_a, w_br_b=m_w_br_b, w_br_c=m_w_br_c, w_out=m_w_out, w_mlp1=m_w_mlp1, w_mlp2=m_w_mlp2)
    mom_v = dict(c_ctx=v_c_ctx, w_ada=v_w_ada, b_ada=v_b_ada, norm1=v_norm1, norm2=v_norm2, w_in=v_w_in, q_norm_a=v_q_norm_a,
                 k_norm_a=v_k_norm_a, q_norm_c=v_q_norm_c, k_norm_c=v_k_norm_c, sink_c=v_sink_c, w_pool=v_w_pool,
                 pool_scale=v_pool_scale, w_br_a=v_w_br_a, w_br_b=v_w_br_b, w_br_c=v_w_br_c, w_out=v_w_out, w_mlp1=v_w_mlp1, w_mlp2=v_w_mlp2)

    shards_bf16 = {n: weights[n].astype(BF16) for n in _GATHERED}
    full = [dict() for _ in range(L)]

    def gather_of(items):
        return _Exchange([(shards_bf16[n], l) for l, n in items], scatter=False)

    def gathered(items, arrs):
        for (l, n), a in zip(items, arrs):
            full[l][n] = _from_shards(n, a)

    gathered([(0, "w_in")], [_gather_two_level(shards_bf16["w_in"], 0, "gather_first_weights")])

    def hosting(fn, *a, exch=None, done=None, **kw):
        if exch is None:
            return fn(*a, **kw)
        res = fn(*a, exch=exch, **kw)
        done(res[-exch.n:])
        own = res[:-exch.n]
        return own[0] if len(own) == 1 else own

    def gather_behind(l, names):
        if l >= L:
            return {}
        items = [(l, n) for n in names]
        return dict(exch=gather_of(items), done=functools.partial(gathered, items))

    cosf, sins = _rope_tables(cfg)
    xs = jnp.concatenate([ctx, x], axis=1)
    cc8 = _pad8(jnp.concatenate([c, c_ctx[None, :]], axis=0))
    va_blk, vc_blk = (QW + KVW) // KVW, (2 * QW + 3 * KVW) // KVW
    per_layer = lambda a: a[:, None, :]
    b_ada3, norm1_3, norm2_3, ps3 = per_layer(b_ada), per_layer(norm1), per_layer(norm2), per_layer(pool_scale)
    gvec = _gvec(q_norm_a, k_norm_a, q_norm_c, k_norm_c)
    sink8 = jnp.pad(sink_c[:, None, :], ((0, 0), (0, 7), (0, LANES - N_QHEADS)))
    wp = _block_diag(w_pool).astype(BF16)

    cc_all = _Exchange([cc8], scatter=False).alone("gather_cond")[0].reshape(8 * N_DEV, D)
    mod_cols = _Exchange([_adaln_fwd(cfg, cc_all, w_ada)], scatter=True).alone("scatter_mod")[0]
    mod_all = _adaln_join(cfg, mod_cols, b_ada3)

    saved = []
    for l in range(L):
        fw = full[l]
        ctx_active = l < L - 1
        mod = mod_all[l]
        z, h = hosting(_in_proj_fwd, cfg, l, xs, norm1_3, mod, fw["w_in"], **gather_behind(l, _MERGE_WEIGHTS if l == 0 else ("w_mlp2",)))
        qa, ka, qc, kc = _qknorm_fwd(cfg, l, z, gvec, cosf, sins)
        oa, lse_a = hosting(_attn_fwd, cfg, l, qa, ka, z, va_blk, sink8, window=False, sink=False, ctx_attend=ctx_active, name="attn_a_fwd",
                            **gather_behind(l, ("w_mlp1",)))
        oc, lse_c = hosting(_attn_fwd, cfg, l, qc, kc, z, vc_blk, sink8, window=True, sink=True, ctx_attend=ctx_active, name="attn_c_fwd",
                            **(gather_behind(l + 1, _MERGE_WEIGHTS)))
        ob, pooled = _pool_fwd(cfg, l, z, wp, ps3)
        x1, mgo = hosting(_merge_fwd, cfg, xs, oa, ob, oc, z, mod, fw["w_br_a"], fw["w_br_b"], fw["w_br_c"], fw["w_out"],
                          ctx_active=ctx_active, **(gather_behind(0, ("w_mlp2",)) if l == 0 else {}))
        if l < L - 1:
            x2, mo, r = hosting(_mlp_fwd, cfg, l, x1, norm2_3, mod, fw["w_mlp1"], fw["w_mlp2"], ctx_active=ctx_active,
                                **gather_behind(l + 1, ("w_in",)))
        else:
            x2, mo, r, sse = _mlp_fwd(cfg, l, x1, norm2_3, mod, fw["w_mlp1"], fw["w_mlp2"], ctx_active=ctx_active, target=loss_target)
        saved.append(dict(xs=xs, mod=mod, z=z, h=h, qa=qa, ka=ka, qc=qc, kc=kc, oa=oa, oc=oc, ob=ob, pooled=pooled, x1=x1, mgo=mgo, mo=mo,
                          lse_a=lse_a, lse_c=lse_c, r=r))
        xs = x2

    dxs = xs
    loss = lax.psum(0.5 * sse[0, 0] / D, ("x", "y", "c"))

    grads = [dict() for _ in range(L)]
    parts = {}
    small = {n: [None] * L for n in _SMALL if n != "c_ctx"}
    d_c_ctx = jnp.zeros((D,), F32)
    flat2 = lambda a: a.reshape(B * T, a.shape[-1])

    def scatter_of(l, names):
        return _Exchange([_to_shards(n, grads[l][n]) for n in names], scatter=True)

    def scattered(l, names, arrs):
        for n, a in zip(names, arrs):
            parts[(l, n)] = a

    def scatter_behind(l, names):
        if l >= L:
            return {}
        return dict(exch=scatter_of(l, names), done=functools.partial(scattered, l, names))

    for l in reversed(range(L)):
        fw, sv, g = full[l], saved[l], grads[l]
        ctx_active = l < L - 1
        mod = sv["mod"]
        dx1, h2, da, dout, dmod_mlp, dgn2 = hosting(_mlp_bwd, cfg, l, sv["x1"], dxs, sv["mo"], sv["r"], norm2_3, mod, fw["w_mlp1"], fw["w_mlp2"],
                                                    ctx_active=ctx_active, **scatter_behind(l + 1, ("w_in",)))
        g["w_mlp1"] = _matmul_tn(flat2(h2), flat2(da), "dw_mlp1", by_shard=True)
        g["w_mlp2"] = _matmul_tn(flat2(sv["r"]), flat2(dout), "dw_mlp2", by_shard=False)
        doa, dob, doc, dpa, dpb, dpc, y, dmo, dzg, dg1 = _merge_bwd(
            cfg, dx1, sv["mgo"], sv["oa"], sv["ob"], sv["oc"], sv["z"], mod, fw["w_br_a"], fw["w_br_b"], fw["w_br_c"], fw["w_out"],
            ctx_active=ctx_active)
        g["w_out"] = _matmul_tn(flat2(y), flat2(dmo), "dw_out", by_shard=False)
        g["w_br_a"] = _matmul_tn(flat2(sv["oa"]), flat2(dpa), "dw_br_a", by_shard=True)
        g["w_br_b"] = _matmul_tn(flat2(sv["ob"]), flat2(dpb), "dw_br_b", by_shard=True)
        g["w_br_c"] = _matmul_tn(flat2(sv["oc"]), flat2(dpc), "dw_br_c", by_shard=True)
        z = sv["z"]
        dqa, dka, dva, _ = hosting(_attn_bwd, cfg, l, sv["qa"], sv["ka"], z, va_blk, sink8, doa, sv["lse_a"], window=False, sink=False,
                                   ctx_attend=ctx_active, name="attn_a_bwd", **scatter_behind(l, ("w_mlp1", "w_mlp2")))
        dqc, dkc, dvc, dsink = hosting(_attn_bwd, cfg, l, sv["qc"], sv["kc"], z, vc_blk, sink8, doc, sv["lse_c"], window=True, sink=True,
                                       ctx_attend=ctx_active, name="attn_c_bwd", **scatter_behind(l, _MERGE_WEIGHTS))
        dzq, dgvec = _qknorm_bwd(cfg, l, z, gvec, cosf, sins, dqa, dka, dva, dqc, dkc, dvc)
        du, dwp, dps = _pool_bwd(cfg, l, dob, sv["pooled"], wp, ps3)
        dxs, dz, dmod_in, dgn1 = _in_proj_bwd(cfg, l, dzq, du, dzg, fw["w_in"], sv["xs"], dx1, norm1_3, mod, latent_only=(l == 0))
        g["w_in"] = _matmul_tn(flat2(sv["h"]), flat2(dz), "dw_in", by_shard=False)
        dmod_cols, dbias = _dmod_pack(cfg, dmod_in, dg1, dmod_mlp)
        dm_all = _Exchange([dmod_cols], scatter=True).alone("scatter_dmod")[0].reshape(8 * N_DEV, -1)
        g["w_ada"], dcc = _adaln_bwd(cfg, l, cc_all, dm_all, w_ada)
        d_c_ctx = d_c_ctx + dcc[0]
        gv = dgvec[0]
        heads = lambda v, n: v.reshape(n, HEAD).sum(axis=0)
        small["b_ada"][l] = dbias[0]
        small["norm1"][l] = dgn1[0]
        small["norm2"][l] = dgn2[0]
        small["q_norm_a"][l] = heads(gv[0:QW], N_QHEADS)
        small["k_norm_a"][l] = heads(gv[QW:QW + KVW], N_KV)
        small["q_norm_c"][l] = heads(gv[QW + 2 * KVW:2 * QW + 2 * KVW], N_QHEADS)
        small["k_norm_c"][l] = heads(gv[2 * QW + 2 * KVW:2 * QW + 3 * KVW], N_KV)
        small["sink_c"][l] = dsink[0, :N_QHEADS]
        small["w_pool"][l] = jnp.stack([dwp[i * HEAD:(i + 1) * HEAD, i * HEAD:(i + 1) * HEAD] for i in range(len(POOL_WINDOWS))])
        small["pool_scale"][l] = dps[0]
    grad_x = dxs

    scattered(0, ("w_in",), scatter_of(0, ("w_in",)).alone("scatter_last_grads"))
    for l in range(L):
        parts[(l, "w_ada")] = grads[l]["w_ada"][None]
    stepped = {n: _adamw_sharded([parts[(l, n)] for l in range(L)], weights[n], mom_m[n], mom_v[n], "adamw_" + n) for n in _SHARDED}

    small_vals = {n: jnp.stack(v) for n, v in small.items()}
    small_vals["c_ctx"] = d_c_ctx
    small_parts = _Exchange([_pack_small(small_vals)], scatter=False).alone("gather_small_grads")[0]
    stepped_small = _adamw_packed(small_parts, _pack_small(weights), _pack_small(mom_m), _pack_small(mom_v), "adamw_small")

    outs = []
    for i in range(4):
        res = {n: stepped[n][i] for n in _SHARDED}
        res.update(_unpack_small(stepped_small[i], weights))
        outs.append(res)
    order = ("c_ctx", "w_ada", "b_ada", "norm1", "norm2", "w_in", "q_norm_a", "k_norm_a", "q_norm_c", "k_norm_c", "sink_c",
             "w_pool", "pool_scale", "w_br_a", "w_br_b", "w_br_c", "w_out", "w_mlp1", "w_mlp2")
    return (loss, grad_x, *[res[n] for res in outs for n in order])
```

```python
import functools

import jax
import jax.numpy as jnp
from jax import lax
from jax.experimental import pallas as pl
from jax.experimental.pallas import tpu as pltpu

F32 = jnp.float32
BF16 = jnp.bfloat16
HIGHEST = lax.Precision.HIGHEST

N_DEV = 8
HEAD = 64
N_QHEADS = 6
N_KV = 2
GROUP = 3
QW = N_QHEADS * HEAD
KVW = N_KV * HEAD
QKV_W = 2 * (QW + 2 * KVW)
POOL_W = 256
POOL_WINDOWS = (2, 4, 8, 16)
POOL_HALO = 16
GATE0 = QKV_W + POOL_W
WINDOW = 128
GRID_W = 64
ROPE_THETA = 10000.0
EPS = 1e-6
NEG = -1e30
QSCALE = HEAD ** -0.5
LANES = 128
PACK_W = 1024
VMEM_LIMIT = 56 * 1024 * 1024

ADAM_LR = 0.001
ADAM_B1 = 0.9
ADAM_B2 = 0.999
ADAM_EPS = 1e-08
ADAM_WD = 0.01
ADAM_STEP = 10

NT_DIMS = (((1,), (1,)), ((), ()))
TN_DIMS = (((0,), (0,)), ((), ()))


def _dot(a, b):
    return jnp.dot(a, b, preferred_element_type=F32)


def _dot_nt(a, b):
    return lax.dot_general(a, b, NT_DIMS, preferred_element_type=F32)


def _dot_tn(a, b):
    return lax.dot_general(a, b, TN_DIMS, preferred_element_type=F32)


def _params(n_grid):
    return pltpu.CompilerParams(dimension_semantics=("arbitrary",) * n_grid, vmem_limit_bytes=VMEM_LIMIT)


def _full(shape):
    nd = len(shape)
    return pl.BlockSpec(shape, lambda *_: (0,) * nd)


def _layer(l, width):
    return pl.BlockSpec((1, 1, width), lambda *_: (l, 0, 0))


def _modulate(x, gn, shift, scale):
    rstd = lax.rsqrt(jnp.mean(x * x, axis=-1, keepdims=True) + EPS)
    xhat = x * rstd
    return xhat * gn * (1.0 + scale) + shift, xhat, rstd


def _modulate_bwd(dh, xhat, rstd, gn, scale):
    d_shift = jnp.sum(dh, axis=0, keepdims=True)
    d_scale = jnp.sum(dh * xhat * gn, axis=0, keepdims=True)
    dy = dh * (1.0 + scale)
    d_gn = jnp.sum(dy * xhat, axis=0, keepdims=True)
    dxh = dy * gn
    dx = rstd * (dxh - xhat * jnp.mean(dxh * xhat, axis=-1, keepdims=True))
    return dx, d_shift, d_scale, d_gn


def _mod_row(mod_ref, row, k, d):
    return mod_ref[pl.ds(row, 1), k * d:(k + 1) * d]


class _Cfg:
    def __init__(self, b, s, n, d):
        self.B, self.S, self.N, self.D = b, s, n, d
        self.T = n + s
        self.F = 4 * d
        self.IN = GATE0 + 3 * d
        self.tm = 256 if (n % 256 == 0 and s % 256 == 0) else 128
        self.nT = self.T // self.tm
        self.nC = n // self.tm
        self.gw = 512 if d % 512 == 0 else 256
        self.kw = self.tm + 2 * POOL_HALO
        assert GATE0 % self.gw == 0 and d % self.gw == 0 and b < 8 and self.T >= self.kw and max(POOL_WINDOWS) // 2 <= POOL_HALO
        assert s % GRID_W == 0 and n % self.tm == 0 and s % self.tm == 0 and s >= self.tm + 2 * WINDOW
        assert d % (N_DEV * LANES) == 0


def _peer(k):
    x, y, c = lax.axis_index("x"), lax.axis_index("y"), lax.axis_index("c")
    px = x ^ ((k >> 2) & 1)
    py = y ^ ((k >> 1) & 1)
    pc = c ^ (k & 1)
    return (px, py, pc), 4 * px + 2 * py + pc


class _Exchange:
    def __init__(self, arrays, scatter):
        self.arrays = [a if isinstance(a, tuple) else (a, None) for a in arrays]
        self.scatter = scatter
        self.n = len(self.arrays)

    def operands(self):
        return [a for a, _ in self.arrays]

    def out_shapes(self):
        res = []
        for a, layer in self.arrays:
            shape = a.shape[1:] if (self.scatter or layer is not None) else a.shape
            res.append(jax.ShapeDtypeStruct((N_DEV,) + tuple(shape), a.dtype))
        return res

    def scratch(self):
        n = self.n * (N_DEV - 1)
        return [pltpu.SemaphoreType.DMA((n,)), pltpu.SemaphoreType.DMA((n,)), pltpu.SemaphoreType.DMA((self.n,))]

    def _copies(self, x_refs, out_refs, send_sems, recv_sems, local_sems, want):
        _, me = _peer(0)
        res = []
        for i, ((_, layer), x_ref, out_ref) in enumerate(zip(self.arrays, x_refs, out_refs)):
            if self.scatter:
                src_of = lambda d, x_ref=x_ref: x_ref.at[d]
            elif layer is not None:
                src_of = lambda d, x_ref=x_ref, layer=layer: x_ref.at[layer]
            else:
                src_of = lambda d, x_ref=x_ref: x_ref
            if want == "local":
                res.append(pltpu.make_async_copy(src_of(me), out_ref.at[me], local_sems.at[i]))
                continue
            for k in range(1, N_DEV):
                pos, idx = _peer(k)
                j = i * (N_DEV - 1) + k - 1
                common = dict(send_sem=send_sems.at[j], recv_sem=recv_sems.at[j], device_id=pos, device_id_type=pl.DeviceIdType.MESH)
                if want == "send":
                    res.append(pltpu.make_async_remote_copy(src_ref=src_of(idx), dst_ref=out_ref.at[me], **common))
                else:
                    res.append(pltpu.make_async_remote_copy(src_ref=src_of(me), dst_ref=out_ref.at[idx], **common))
        return res

    def start(self, *refs):
        for cp in self._copies(*refs, "local") + self._copies(*refs, "send"):
            cp.start()

    def wait(self, *refs):
        for cp in self._copies(*refs, "recv"):
            cp.wait_recv()
        for cp in self._copies(*refs, "send"):
            cp.wait_send()
        for cp in self._copies(*refs, "local"):
            cp.wait()

    def alone(self, name):
        n = self.n

        def body(*refs):
            args = (refs[:n], refs[n:2 * n], *refs[2 * n:])
            self.start(*args)
            self.wait(*args)

        any_spec = pl.BlockSpec(memory_space=pl.ANY)
        return pl.pallas_call(body, name=name, in_specs=[any_spec] * n, out_specs=[any_spec] * n,
                              out_shape=self.out_shapes(), scratch_shapes=self.scratch())(*self.operands())


def _gather_two_level(x, layer, name):
    shape = x.shape[1:]

    def body(x_ref, out_ref, send_sems, recv_sems, local_sem):
        mx, my, mc = lax.axis_index("x"), lax.axis_index("y"), lax.axis_index("c")
        me, sibling = (mx, my, mc), (mx, my, 1 - mc)
        chips = [(1 - mx, my), (mx, 1 - my), (1 - mx, 1 - my)]
        src = x_ref.at[layer]

        def slot(px, py, pc):
            return out_ref.at[4 * px + 2 * py + pc]

        def copy(k, block, to, from_src=False):
            return pltpu.make_async_remote_copy(src_ref=src if from_src else slot(*block), dst_ref=slot(*block), send_sem=send_sems.at[k],
                                                recv_sem=recv_sems.at[k], device_id=to, device_id_type=pl.DeviceIdType.MESH)

        mine = pltpu.make_async_copy(src, slot(*me), local_sem)
        mine.start()
        first = [copy(0, me, sibling, True)] + [copy(1 + j, me, (*chip, mc), True) for j, chip in enumerate(chips)]
        for cp in first:
            cp.start()
        passed = [copy(4 + j, (*chip, mc), sibling) for j, chip in enumerate(chips)]
        for j, chip in enumerate(chips):
            copy(1 + j, (*chip, mc), me).wait_recv()
            passed[j].start()
        copy(0, sibling, me).wait_recv()
        for j, chip in enumerate(chips):
            copy(4 + j, (*chip, 1 - mc), me).wait_recv()
        for cp in first + passed:
            cp.wait_send()
        mine.wait()

    any_spec = pl.BlockSpec(memory_space=pl.ANY)
    return pl.pallas_call(
        body, name=name, in_specs=[any_spec], out_specs=any_spec,
        out_shape=jax.ShapeDtypeStruct((N_DEV,) + tuple(shape), x.dtype),
        scratch_shapes=[pltpu.SemaphoreType.DMA((N_DEV - 1,)), pltpu.SemaphoreType.DMA((N_DEV - 1,)), pltpu.SemaphoreType.DMA],
    )(x)


def _pcall(exch):
    if exch is None:
        return pl.pallas_call

    def make(body, *, name, grid, in_specs, out_specs, out_shape, compiler_params, scratch_shapes=()):
        multi = isinstance(out_shape, (list, tuple))
        out_specs_l = list(out_specs) if multi else [out_specs]
        out_shape_l = list(out_shape) if multi else [out_shape]
        n_in, n_out, n_x, n_s = len(in_specs), len(out_specs_l), exch.n, len(scratch_shapes)

        def hosted(*refs):
            ins, x_refs = refs[:n_in], refs[n_in:n_in + n_x]
            o0 = n_in + n_x
            outs, xo_refs = refs[o0:o0 + n_out], refs[o0 + n_out:o0 + n_out + n_x]
            s0 = o0 + n_out + n_x
            own_scratch, sems = refs[s0:s0 + n_s], refs[s0 + n_s:]
            ids = [pl.program_id(i) for i in range(len(grid))]
            first = functools.reduce(jnp.logical_and, [i == 0 for i in ids])
            last = functools.reduce(jnp.logical_and, [i == g - 1 for i, g in zip(ids, grid)])

            @pl.when(first)
            def _():
                exch.start(x_refs, xo_refs, *sems)

            body(*ins, *outs, *own_scratch)

            @pl.when(last)
            def _():
                exch.wait(x_refs, xo_refs, *sems)

        any_spec = pl.BlockSpec(memory_space=pl.ANY)
        call = pl.pallas_call(
            hosted, name=name, grid=grid, in_specs=list(in_specs) + [any_spec] * n_x, out_specs=out_specs_l + [any_spec] * n_x,
            out_shape=out_shape_l + exch.out_shapes(), scratch_shapes=list(scratch_shapes) + exch.scratch(),
            compiler_params=compiler_params)
        return lambda *args: call(*args, *exch.operands())

    return make


def _adaln_fwd(cfg, cc_all, w_ada):
    d = cfg.D
    L, _, wa = w_ada.shape

    def body(c_ref, w_ref, o_ref):
        c = c_ref[...]
        a = (c * jax.nn.sigmoid(c)).astype(BF16)
        for l in range(L):
            m = _dot(a, w_ref[l].astype(BF16))
            for p in range(N_DEV):
                o_ref[p, l] = m[8 * p:8 * (p + 1)]

    return pl.pallas_call(
        body, name="adaln_fwd", grid=(1,),
        in_specs=[_full((8 * N_DEV, d)), _full((L, d, wa))],
        out_specs=_full((N_DEV, L, 8, wa)),
        out_shape=jax.ShapeDtypeStruct((N_DEV, L, 8, wa), F32), compiler_params=_params(1),
    )(cc_all, w_ada)


def _adaln_join(cfg, parts, b_ada):
    d = cfg.D
    _, L, _, wa = parts.shape

    def body(p_ref, b_ref, o_ref):
        for l in range(L):
            for j in range(N_DEV):
                o_ref[l, :, j * wa:(j + 1) * wa] = p_ref[j, l] + b_ref[l, :, j * wa:(j + 1) * wa]

    return pl.pallas_call(
        body, name="adaln_join", grid=(1,),
        in_specs=[_full((N_DEV, L, 8, wa)), _full((L, 1, 6 * d))],
        out_specs=_full((L, 8, 6 * d)),
        out_shape=jax.ShapeDtypeStruct((L, 8, 6 * d), F32), compiler_params=_params(1),
    )(parts, b_ada)


def _in_proj_fwd(cfg, l, x, gn, mod, w_in, exch=None):
    B, T, D, IN, tm, nC = cfg.B, cfg.T, cfg.D, cfg.IN, cfg.tm, cfg.nC

    def body(x_ref, gn_ref, mod_ref, w_ref, z_ref, h_ref):
        b, t = pl.program_id(0), pl.program_id(1)
        row = jnp.where(t < nC, B, b)
        h, _, _ = _modulate(x_ref[0], gn_ref[0], _mod_row(mod_ref, row, 0, D), _mod_row(mod_ref, row, 1, D))
        hb = h.astype(BF16)
        h_ref[0] = hb
        z_ref[0] = _dot(hb, w_ref[...])

    return _pcall(exch)(
        body, name="in_proj_fwd", grid=(B, cfg.nT),
        in_specs=[pl.BlockSpec((1, tm, D), lambda b, t: (b, t, 0)), _layer(l, D), _full((8, 6 * D)), _full((D, IN))],
        out_specs=[pl.BlockSpec((1, tm, IN), lambda b, t: (b, t, 0)), pl.BlockSpec((1, tm, D), lambda b, t: (b, t, 0))],
        out_shape=[jax.ShapeDtypeStruct((B, T, IN), F32), jax.ShapeDtypeStruct((B, T, D), BF16)],
        compiler_params=_params(2),
    )(x, gn, mod, w_in)


def _head_indicator():
    r = lax.broadcasted_iota(jnp.int32, (LANES, LANES), 0) // HEAD
    c = lax.broadcasted_iota(jnp.int32, (LANES, LANES), 1) // HEAD
    return jnp.where(r == c, 1.0, 0.0).astype(BF16)


def _head_sum(x, ind):
    hi = x.astype(BF16)
    lo = (x - hi.astype(F32)).astype(BF16)
    return _dot(hi, ind) + _dot(lo, ind)


def _pair_swap(y):
    lane = lax.broadcasted_iota(jnp.int32, y.shape, 1)
    return jnp.where(lane % 2 == 0, pltpu.roll(y, LANES - 1, 1), pltpu.roll(y, 1, 1))


_QK_CHUNKS = (0, 1, 2, 3, 5, 6, 7, 8)
_Q_CHUNKS = (0, 1, 2, 5, 6, 7)


def _qknorm_fwd(cfg, l, z, gvec, cosf, sins):
    B, T, tm = cfg.B, cfg.T, cfg.tm

    def body(z_ref, g_ref, cos_ref, sin_ref, qa_ref, ka_ref, qc_ref, kc_ref):
        ind = _head_indicator()
        cos, sin = cos_ref[...], sin_ref[...]

        def chunk(c):
            x = z_ref[0, :, c * LANES:(c + 1) * LANES]
            ss = _head_sum(x * x, ind)
            y = x * lax.rsqrt(ss * (1.0 / HEAD) + EPS) * g_ref[0, :, c * LANES:(c + 1) * LANES]
            out = y * cos + _pair_swap(y) * sin
            return (out * QSCALE if c in _Q_CHUNKS else out).astype(BF16)

        qa_ref[0] = jnp.concatenate([chunk(0), chunk(1), chunk(2)], axis=-1)
        ka_ref[0] = chunk(3)
        qc_ref[0] = jnp.concatenate([chunk(5), chunk(6), chunk(7)], axis=-1)
        kc_ref[0] = chunk(8)

    row = lambda w: pl.BlockSpec((1, tm, w), lambda b, t: (b, t, 0))
    tab = pl.BlockSpec((tm, LANES), lambda b, t: (t, 0))
    return pl.pallas_call(
        body, name="qknorm_fwd", grid=(B, cfg.nT),
        in_specs=[row(QKV_W), _layer(l, QKV_W), tab, tab],
        out_specs=[row(QW), row(KVW), row(QW), row(KVW)],
        out_shape=[jax.ShapeDtypeStruct((B, T, w), BF16) for w in (QW, KVW, QW, KVW)],
        compiler_params=_params(2),
    )(z, gvec, cosf, sins)


def _attn_scores(cfg, tl, q, k_ref, v_ref, sink_ref, h, loc, window, sink, lse=None):
    S, N, tq = cfg.S, cfg.N, cfg.tm
    hs = slice(h * HEAD, (h + 1) * HEAD)
    qs = jnp.concatenate([q[:, (GROUP * h + g) * HEAD:(GROUP * h + g + 1) * HEAD] for g in range(GROUP)], axis=0)
    lo = None
    if not loc:
        kk = k_ref[0, 0:N, :][:, hs]
        vv = v_ref[0, 0:N, :].astype(BF16)[:, hs]
    elif not window:
        kk = k_ref[0][:, hs]
        vv = v_ref[0].astype(BF16)[:, hs]
    else:
        W = tq + 2 * WINDOW
        lo = pl.multiple_of(jnp.clip(tl * tq - WINDOW, 0, S - W), LANES)
        kk = jnp.concatenate([k_ref[0, 0:N, :], k_ref[0, pl.ds(N + lo, W), :]], axis=0)[:, hs]
        vv = jnp.concatenate([v_ref[0, 0:N, :], v_ref[0, pl.ds(N + lo, W), :]], axis=0).astype(BF16)[:, hs]
    st = _dot_nt(kk, qs)
    if window:
        krow = lax.broadcasted_iota(jnp.int32, st.shape, 0)
        qpos = tl * tq + lax.broadcasted_iota(jnp.int32, st.shape, 1) % tq
        st = jnp.where((krow < N) | (jnp.abs(qpos - (lo + krow - N)) <= WINDOW), st, NEG)
    sk = None
    if sink:
        colg = lax.broadcasted_iota(jnp.int32, (1, GROUP * tq), 1) // tq
        sk = jnp.zeros((1, GROUP * tq), F32)
        for g in range(GROUP):
            j = GROUP * h + g
            sk = jnp.where(colg == g, sink_ref[0, 0:1, j:j + 1], sk)
    if lse is not None:
        return qs, kk, vv, jnp.exp(st - lse), None, (jnp.exp(sk - lse) if sink else None), lo, lse
    m = jnp.max(st, axis=0, keepdims=True)
    if sink:
        m = jnp.maximum(m, sk)
    e = jnp.exp(st - m)
    l = jnp.sum(e, axis=0, keepdims=True)
    e_s = None
    if sink:
        e_s = jnp.exp(sk - m)
        l = l + e_s
    return qs, kk, vv, e, 1.0 / l, e_s, lo, m + jnp.log(l)


def _attn_fwd(cfg, l, q, k, z, vblock, sink8, *, window, sink, ctx_attend, name, exch=None):
    B, T, tq, nC = cfg.B, cfg.T, cfg.tm, cfg.nC

    def body(q_ref, k_ref, v_ref, sink_ref, o_ref, lse_ref):
        t = pl.program_id(1)

        def run(loc):
            q_t = q_ref[0]
            outs = [None] * N_QHEADS
            lses = [None] * N_QHEADS
            for h in range(N_KV):
                _, _, vv, e, inv, _, _, lse = _attn_scores(cfg, t - nC, q_t, k_ref, v_ref, sink_ref, h, loc, window and loc, sink)
                o = (_dot_tn(vv, e.astype(BF16)) * inv).T
                for g in range(GROUP):
                    outs[GROUP * h + g] = o[g * tq:(g + 1) * tq]
                    lses[GROUP * h + g] = lse[:, g * tq:(g + 1) * tq]
            o_ref[0] = jnp.concatenate(outs, axis=-1).astype(BF16)
            lse_ref[0] = jnp.concatenate(lses + [jnp.zeros((8 - N_QHEADS, tq), F32)], axis=0)

        pl.when(t >= nC)(functools.partial(run, True))
        if ctx_attend:
            pl.when(t < nC)(functools.partial(run, False))
        else:
            @pl.when(t < nC)
            def _():
                o_ref[0] = jnp.zeros((tq, QW), BF16)
                lse_ref[0] = jnp.zeros((8, tq), F32)

    return _pcall(exch)(
        body, name=name, grid=(B, cfg.nT),
        in_specs=[pl.BlockSpec((1, tq, QW), lambda b, t: (b, t, 0)),
                  pl.BlockSpec((1, T, KVW), lambda b, t: (b, 0, 0)),
                  pl.BlockSpec((1, T, KVW), lambda b, t: (b, 0, vblock)),
                  pl.BlockSpec((1, 8, LANES), lambda b, t: (l, 0, 0))],
        out_specs=[pl.BlockSpec((1, tq, QW), lambda b, t: (b, t, 0)), pl.BlockSpec((1, 8, tq), lambda b, t: (b, 0, t))],
        out_shape=[jax.ShapeDtypeStruct((B, T, QW), BF16), jax.ShapeDtypeStruct((B, 8, T), F32)], compiler_params=_params(2),
    )(q, k, z, sink8)


def _pool_geometry(cfg, t):
    tm, N, T, nC = cfg.tm, cfg.N, cfg.T, cfg.nC
    r0 = pl.multiple_of(t * tm, tm)
    isctx = t < nC
    seg_lo = jnp.where(isctx, 0, N)
    seg_hi = jnp.where(isctx, N, T)
    k0 = pl.multiple_of(jnp.clip(t * tm - POOL_HALO, 0, T - cfg.kw), POOL_HALO)
    return r0, seg_lo, seg_hi, k0


def _pool_count(pos, h, seg_lo, seg_hi):
    return jnp.maximum(jnp.minimum(pos + h, seg_hi) - jnp.maximum(pos - h, seg_lo), 1).astype(F32)


def _split_bf16(x):
    hi = x.astype(BF16)
    return hi, (x - hi.astype(F32)).astype(BF16)


def _pool_fwd(cfg, l, z, wp, ps):
    B, T, tm, kw = cfg.B, cfg.T, cfg.tm, cfg.kw

    def body(u_ref, wp_ref, ps_ref, ob_ref, pooled_ref):
        t = pl.program_id(1)
        r0, seg_lo, seg_hi, k0 = _pool_geometry(cfg, t)
        hi, lo = _split_bf16(u_ref[0, pl.ds(k0, kw), :])
        rr = r0 + lax.broadcasted_iota(jnp.int32, (tm, kw), 0)
        cc = k0 + lax.broadcasted_iota(jnp.int32, (tm, kw), 1)
        diff = cc - rr
        inseg = (cc >= seg_lo) & (cc < seg_hi)
        rcol = r0 + lax.broadcasted_iota(jnp.int32, (tm, 1), 0)
        group = lax.broadcasted_iota(jnp.int32, (tm, POOL_W), 1) // HEAD
        acc = jnp.zeros((tm, POOL_W), F32)
        for gi, w in enumerate(POOL_WINDOWS):
            h = w // 2
            band = jnp.where((diff >= -h) & (diff <= h - 1) & inseg, 1.0, 0.0).astype(BF16)
            tot = _dot(band, hi) + _dot(band, lo)
            acc = jnp.where(group == gi, tot / _pool_count(rcol, h, seg_lo, seg_hi), acc)
        pooled = (acc - u_ref[0, pl.ds(r0, tm), :]).astype(BF16)
        pooled_ref[0] = pooled
        ob_ref[0] = (_dot(pooled, wp_ref[0]) * ps_ref[0]).astype(BF16)

    row = pl.BlockSpec((1, tm, POOL_W), lambda b, t: (b, t, 0))
    return pl.pallas_call(
        body, name="pool_fwd", grid=(B, cfg.nT),
        in_specs=[pl.BlockSpec((1, T, POOL_W), lambda b, t: (b, 0, QKV_W // POOL_W)),
                  pl.BlockSpec((1, POOL_W, POOL_W), lambda b, t: (l, 0, 0)), _layer(l, POOL_W)],
        out_specs=[row, row],
        out_shape=[jax.ShapeDtypeStruct((B, T, POOL_W), BF16)] * 2, compiler_params=_params(2),
    )(z, wp, ps)


def _gate_specs(cfg):
    tm, gw = cfg.tm, cfg.gw
    first = GATE0 // gw
    return [pl.BlockSpec((1, tm, gw), functools.partial(lambda b, t, j: (b, t, j), j=first + i)) for i in range(3 * cfg.D // gw)]


def _read_gates(cfg, gate_refs):
    per = cfg.D // cfg.gw
    return [jnp.concatenate([gate_refs[k * per + i][0] for i in range(per)], axis=-1) for k in range(3)]


def _merge_fwd(cfg, x, oa, ob, oc, z, mod, wa, wb, wc, wo, *, ctx_active, exch=None):
    B, T, D, tm, nC = cfg.B, cfg.T, cfg.D, cfg.tm, cfg.nC
    ng = 3 * D // cfg.gw

    def body(x_ref, oa_ref, ob_ref, oc_ref, *rest):
        gate_refs = rest[:ng]
        mod_ref, wa_ref, wb_ref, wc_ref, wo_ref, x1_ref, mgo_ref = rest[ng:]
        b, t = pl.program_id(0), pl.program_id(1)

        def compute():
            row = jnp.where(t < nC, B, b)
            ga, gb, gc = _read_gates(cfg, gate_refs)
            y = (jax.nn.sigmoid(ga) * _dot(oa_ref[0], wa_ref[...])
                 + jax.nn.sigmoid(gb) * _dot(ob_ref[0], wb_ref[...])
                 + jax.nn.sigmoid(gc) * _dot(oc_ref[0], wc_ref[...]))
            mo = _dot(y.astype(BF16), wo_ref[...])
            mgo_ref[0] = mo.astype(BF16)
            x1_ref[0] = x_ref[0] + _mod_row(mod_ref, row, 2, D) * mo

        if ctx_active:
            compute()
        else:
            pl.when(t >= nC)(compute)

            @pl.when(t < nC)
            def _():
                mgo_ref[0] = jnp.zeros((tm, D), BF16)
                x1_ref[0] = x_ref[0]

    row = lambda w: pl.BlockSpec((1, tm, w), lambda b, t: (b, t, 0))
    return _pcall(exch)(
        body, name="merge_fwd", grid=(B, cfg.nT),
        in_specs=[row(D), row(QW), row(POOL_W), row(QW)] + _gate_specs(cfg)
        + [_full((8, 6 * D)), _full((QW, D)), _full((POOL_W, D)), _full((QW, D)), _full((D, D))],
        out_specs=[row(D), row(D)],
        out_shape=[jax.ShapeDtypeStruct((B, T, D), F32), jax.ShapeDtypeStruct((B, T, D), BF16)],
        compiler_params=_params(2),
    )(x, oa, ob, oc, *([z] * ng), mod, wa, wb, wc, wo)


def _w1_apply(hb, w1_ref):
    return jnp.concatenate([_dot(hb, w1_ref[d]) for d in range(N_DEV)], axis=-1)


def _mlp_fwd(cfg, l, x1, gn, mod, w1, w2, *, ctx_active, target=None, exch=None):
    B, T, D, F, tm, nC = cfg.B, cfg.T, cfg.D, cfg.F, cfg.tm, cfg.nC
    assert target is None or not ctx_active

    def body(x_ref, gn_ref, mod_ref, w1_ref, w2_ref, *rest):
        if target is None:
            x2_ref, mo_ref, r_ref = rest
        else:
            tgt_ref, x2_ref, mo_ref, r_ref, sse_ref = rest
            _acc_init([sse_ref])
        b, t = pl.program_id(0), pl.program_id(1)

        def compute():
            row = jnp.where(t < nC, B, b)
            x = x_ref[0]
            h, _, _ = _modulate(x, gn_ref[0], _mod_row(mod_ref, row, 3, D), _mod_row(mod_ref, row, 4, D))
            a = jnp.maximum(_w1_apply(h.astype(BF16), w1_ref), 0.0)
            rb = (a * a).astype(BF16)
            r_ref[0] = rb
            mo = _dot(rb, w2_ref[...])
            mo_ref[0] = mo.astype(BF16)
            x2 = x + _mod_row(mod_ref, row, 5, D) * mo
            if target is None:
                x2_ref[0] = x2
            else:
                err = x2 - tgt_ref[0]
                x2_ref[0] = err * (1.0 / D)
                sse_ref[...] += jnp.sum(err * err)

        if ctx_active:
            compute()
        else:
            pl.when(t >= nC)(compute)

            @pl.when(t < nC)
            def _():
                mo_ref[0] = jnp.zeros((tm, D), BF16)
                r_ref[0] = jnp.zeros((tm, F), BF16)
                x2_ref[0] = x_ref[0] if target is None else jnp.zeros((tm, D), F32)

    row = pl.BlockSpec((1, tm, D), lambda b, t: (b, t, 0))
    in_specs = [row, _layer(l, D), _full((8, 6 * D)), _full((N_DEV, D, F // N_DEV)), _full((F, D))]
    out_specs = [row, row, pl.BlockSpec((1, tm, F), lambda b, t: (b, t, 0))]
    out_shape = [jax.ShapeDtypeStruct((B, T, D), F32), jax.ShapeDtypeStruct((B, T, D), BF16), jax.ShapeDtypeStruct((B, T, F), BF16)]
    args = [x1, gn, mod, w1, w2]
    if target is not None:
        in_specs.append(pl.BlockSpec((1, tm, D), lambda b, t: (b, jnp.maximum(t - nC, 0), 0)))
        out_specs.append(_full((8, LANES)))
        out_shape.append(jax.ShapeDtypeStruct((8, LANES), F32))
        args.append(target)
    return _pcall(exch)(
        body, name="mlp_fwd", grid=(B, cfg.nT), in_specs=in_specs, out_specs=out_specs, out_shape=out_shape,
        compiler_params=_params(2),
    )(*args)


def _acc_init(refs):
    b, t = pl.program_id(0), pl.program_id(1)

    @pl.when((b == 0) & (t == 0))
    def _():
        for ref in refs:
            ref[...] = jnp.zeros(ref.shape, ref.dtype)


def _mlp_bwd(cfg, l, x1, dx2, mo, r, gn, mod, w1, w2, *, ctx_active, exch=None):
    B, T, D, F, tm, nC = cfg.B, cfg.T, cfg.D, cfg.F, cfg.tm, cfg.nC
    ws = F // N_DEV

    def body(x_ref, dx_ref, mo_ref, r_ref, gn_ref, mod_ref, w1_ref, w2_ref, dx1_ref, h_ref, da_ref, dout_ref, dmod_ref, dgn_ref):
        b, t = pl.program_id(0), pl.program_id(1)
        _acc_init([dmod_ref, dgn_ref])

        def compute():
            row = jnp.where(t < nC, B, b)
            gn = gn_ref[0]
            scale = _mod_row(mod_ref, row, 4, D)
            h, xhat, rstd = _modulate(x_ref[0], gn, _mod_row(mod_ref, row, 3, D), scale)
            hb = h.astype(BF16)
            dx = dx_ref[0]
            dout = (dx * _mod_row(mod_ref, row, 5, D)).astype(BF16)
            da = (_dot_nt(dout, w2_ref[...]) * (2.0 * jnp.sqrt(r_ref[0].astype(F32)))).astype(BF16)
            dh = _dot_nt(da[:, 0:ws], w1_ref[0])
            for d in range(1, N_DEV):
                dh = dh + _dot_nt(da[:, d * ws:(d + 1) * ws], w1_ref[d])
            dxn, d_shift, d_scale, d_gn = _modulate_bwd(dh, xhat, rstd, gn, scale)
            dx1_ref[0] = dx + dxn
            h_ref[0] = hb
            da_ref[0] = da
            dout_ref[0] = dout
            d_gate = jnp.sum(dx * mo_ref[0].astype(F32), axis=0, keepdims=True)
            dmod_ref[pl.ds(row, 1), :] += jnp.concatenate([d_shift, d_scale, d_gate], axis=-1)
            dgn_ref[0:1, :] += d_gn

        if ctx_active:
            compute()
        else:
            pl.when(t >= nC)(compute)

            @pl.when(t < nC)
            def _():
                dx1_ref[0] = dx_ref[0]
                h_ref[0] = jnp.zeros((tm, D), BF16)
                da_ref[0] = jnp.zeros((tm, F), BF16)
                dout_ref[0] = jnp.zeros((tm, D), BF16)

    row = lambda w: pl.BlockSpec((1, tm, w), lambda b, t: (b, t, 0))
    sds = lambda w, dt: jax.ShapeDtypeStruct((B, T, w), dt)
    return _pcall(exch)(
        body, name="mlp_bwd", grid=(B, cfg.nT),
        in_specs=[row(D), row(D), row(D), row(F), _layer(l, D), _full((8, 6 * D)), _full((N_DEV, D, ws)), _full((F, D))],
        out_specs=[row(D), row(D), row(F), row(D), _full((8, 3 * D)), _full((8, D))],
        out_shape=[sds(D, F32), sds(D, BF16), sds(F, BF16), sds(D, BF16),
                   jax.ShapeDtypeStruct((8, 3 * D), F32), jax.ShapeDtypeStruct((8, D), F32)],
        compiler_params=_params(2),
    )(x1, dx2, mo, r, gn, mod, w1, w2)


def _matmul_tn(a, g, name, *, by_shard, a_cols=None, exch=None):
    R = a.shape[0]
    Ng = g.shape[1]
    tr = next(c for c in (2304, 1024, 512, 256, 128, 64, 32, 16, 8) if R % c == 0)
    if a_cols is None:
        Ka, a_blk = a.shape[1], 0
        tka = Ka if Ka <= 1024 else 1024
    else:
        a_start, Ka = a_cols
        tka = Ka
        assert a_start % Ka == 0 and Ka % LANES == 0
        a_blk = a_start // Ka
    if by_shard:
        ws = Ng // N_DEV
        per = next(c for c in (8, 4, 2, 1) if c * ws <= 1152 or c == 1)
        tn = per * ws
    else:
        tn = next(c for c in (1152, 1024, 768, 512, 384, 256, 128) if Ng % c == 0)
    assert Ka % tka == 0 and tn % LANES == 0
    nr = R // tr

    def body(a_ref, g_ref, o_ref, acc_ref):
        r = pl.program_id(2)

        @pl.when(r == 0)
        def _():
            acc_ref[...] = jnp.zeros(acc_ref.shape, F32)

        acc_ref[...] += _dot_tn(a_ref[...], g_ref[...])

        @pl.when(r == nr - 1)
        def _():
            if by_shard:
                for d in range(per):
                    o_ref[d] = acc_ref[:, d * ws:(d + 1) * ws].astype(BF16)
            else:
                o_ref[...] = acc_ref[...].astype(BF16)

    if by_shard:
        out_spec = pl.BlockSpec((per, tka, ws), lambda i, j, r: (j, i, 0))
        out_shape = jax.ShapeDtypeStruct((N_DEV, Ka, ws), BF16)
    else:
        out_spec = pl.BlockSpec((tka, tn), lambda i, j, r: (i, j))
        out_shape = jax.ShapeDtypeStruct((Ka, Ng), BF16)
    return _pcall(exch)(
        body, name=name, grid=(Ka // tka, Ng // tn, nr),
        in_specs=[pl.BlockSpec((tr, tka), lambda i, j, r: (r, i + a_blk)), pl.BlockSpec((tr, tn), lambda i, j, r: (r, j))],
        out_specs=out_spec, out_shape=out_shape, scratch_shapes=[pltpu.VMEM((tka, tn), F32)], compiler_params=_params(3),
    )(a, g)


def _merge_bwd(cfg, dx1, mgo, oa, ob, oc, z, mod, wa, wb, wc, wo, *, ctx_active, exch=None):
    B, T, D, tm, nC = cfg.B, cfg.T, cfg.D, cfg.tm, cfg.nC
    ng = 3 * D // cfg.gw

    def body(dx_ref, mgo_ref, oa_ref, ob_ref, oc_ref, *rest):
        gate_refs = rest[:ng]
        (mod_ref, wa_ref, wb_ref, wc_ref, wo_ref,
         doa_ref, dob_ref, doc_ref, dpa_ref, dpb_ref, dpc_ref, y_ref, dmo_ref, dzg_ref, dg1_ref) = rest[ng:]
        b, t = pl.program_id(0), pl.program_id(1)
        _acc_init([dg1_ref])

        def compute():
            row = jnp.where(t < nC, B, b)
            dx = dx_ref[0]
            dg1_ref[pl.ds(row, 1), :] += jnp.sum(dx * mgo_ref[0].astype(F32), axis=0, keepdims=True)
            dmo = (dx * _mod_row(mod_ref, row, 2, D)).astype(BF16)
            dmo_ref[0] = dmo
            dy = _dot_nt(dmo, wo_ref[...])
            gates = _read_gates(cfg, gate_refs)
            y = jnp.zeros((tm, D), F32)
            dgs = []
            for gate, o_ref, w_ref, do_ref, dp_ref in ((gates[0], oa_ref, wa_ref, doa_ref, dpa_ref),
                                                      (gates[1], ob_ref, wb_ref, dob_ref, dpb_ref),
                                                      (gates[2], oc_ref, wc_ref, doc_ref, dpc_ref)):
                s = jax.nn.sigmoid(gate)
                p = _dot(o_ref[0], w_ref[...])
                y = y + s * p
                dp = (dy * s).astype(BF16)
                dp_ref[0] = dp
                do_ref[0] = _dot_nt(dp, w_ref[...]).astype(BF16)
                dgs.append((dy * p * s * (1.0 - s)).astype(BF16))
            y_ref[0] = y.astype(BF16)
            dzg_ref[0] = jnp.concatenate(dgs, axis=-1)

        if ctx_active:
            compute()
        else:
            pl.when(t >= nC)(compute)

            @pl.when(t < nC)
            def _():
                for ref in (doa_ref, dob_ref, doc_ref, dpa_ref, dpb_ref, dpc_ref, y_ref, dmo_ref, dzg_ref):
                    ref[...] = jnp.zeros(ref.shape, ref.dtype)

    row = lambda w: pl.BlockSpec((1, tm, w), lambda b, t: (b, t, 0))
    sds = lambda w: jax.ShapeDtypeStruct((B, T, w), BF16)
    return _pcall(exch)(
        body, name="merge_bwd", grid=(B, cfg.nT),
        in_specs=[row(D), row(D), row(QW), row(POOL_W), row(QW)] + _gate_specs(cfg)
        + [_full((8, 6 * D)), _full((QW, D)), _full((POOL_W, D)), _full((QW, D)), _full((D, D))],
        out_specs=[row(QW), row(POOL_W), row(QW), row(D), row(D), row(D), row(D), row(D), row(3 * D), _full((8, D))],
        out_shape=[sds(QW), sds(POOL_W), sds(QW), sds(D), sds(D), sds(D), sds(D), sds(D), sds(3 * D),
                   jax.ShapeDtypeStruct((8, D), F32)],
        compiler_params=_params(2),
    )(dx1, mgo, oa, ob, oc, *([z] * ng), mod, wa, wb, wc, wo)


def _attn_bwd(cfg, l, q, k, z, vblock, sink8, do, lse, *, window, sink, ctx_attend, name, exch=None):
    B, S, N, T, tq, nC = cfg.B, cfg.S, cfg.N, cfg.T, cfg.tm, cfg.nC

    def body(q_ref, k_ref, v_ref, sink_ref, do_ref, lse_ref, dq_ref, dk_ref, dv_ref, dsink_ref):
        b, t = pl.program_id(0), pl.program_id(1)
        _acc_init([dsink_ref])

        @pl.when(t == 0)
        def _():
            dk_ref[...] = jnp.zeros(dk_ref.shape, F32)
            dv_ref[...] = jnp.zeros(dv_ref.shape, F32)

        def run(loc):
            q_t = q_ref[0]
            do_t = do_ref[0]
            dqs = [None] * N_QHEADS
            dks, dvs = [], []
            dsink_row = jnp.zeros((1, LANES), F32)
            lane = lax.broadcasted_iota(jnp.int32, (1, LANES), 1)
            lo = None
            for h in range(N_KV):
                lse = jnp.concatenate([lse_ref[0, GROUP * h + g:GROUP * h + g + 1, :] for g in range(GROUP)], axis=1)
                qs, kk, vv, p, _, p_s, lo, _ = _attn_scores(cfg, t - nC, q_t, k_ref, v_ref, sink_ref, h, loc, window and loc, sink, lse=lse)
                dos = jnp.concatenate([do_t[:, (GROUP * h + g) * HEAD:(GROUP * h + g + 1) * HEAD] for g in range(GROUP)], axis=0)
                dp = _dot_nt(vv, dos)
                delta = jnp.sum(p * dp, axis=0, keepdims=True)
                ds = (p * (dp - delta)).astype(BF16)
                dq = _dot_tn(kk, ds).T
                dks.append(_dot(ds, qs))
                dvs.append(_dot(p.astype(BF16), dos))
                if sink:
                    dsk = -p_s * delta
                    for g in range(GROUP):
                        tot = jnp.sum(dsk[:, g * tq:(g + 1) * tq], axis=1, keepdims=True)
                        dsink_row = dsink_row + jnp.where(lane == GROUP * h + g, tot, 0.0)
                for g in range(GROUP):
                    dqs[GROUP * h + g] = dq[g * tq:(g + 1) * tq] * QSCALE
            dq_ref[0] = jnp.concatenate(dqs, axis=-1)
            dk = jnp.concatenate(dks, axis=-1)
            dv = jnp.concatenate(dvs, axis=-1)
            if loc and not window:
                dk_ref[0] += dk
                dv_ref[0] += dv
            else:
                dk_ref[0, 0:N, :] += dk[0:N]
                dv_ref[0, 0:N, :] += dv[0:N]
                if loc:
                    W = tq + 2 * WINDOW
                    dk_ref[0, pl.ds(N + lo, W), :] += dk[N:]
                    dv_ref[0, pl.ds(N + lo, W), :] += dv[N:]
            if sink:
                dsink_ref[0:1, :] += dsink_row

        pl.when(t >= nC)(functools.partial(run, True))
        if ctx_attend:
            pl.when(t < nC)(functools.partial(run, False))
        else:
            @pl.when(t < nC)
            def _():
                dq_ref[0] = jnp.zeros((tq, QW), F32)

    kv = pl.BlockSpec((1, T, KVW), lambda b, t: (b, 0, 0))
    qrow = pl.BlockSpec((1, tq, QW), lambda b, t: (b, t, 0))
    return _pcall(exch)(
        body, name=name, grid=(B, cfg.nT),
        in_specs=[qrow, kv, pl.BlockSpec((1, T, KVW), lambda b, t: (b, 0, vblock)),
                  pl.BlockSpec((1, 8, LANES), lambda b, t: (l, 0, 0)), qrow, pl.BlockSpec((1, 8, tq), lambda b, t: (b, 0, t))],
        out_specs=[qrow, kv, kv, _full((8, LANES))],
        out_shape=[jax.ShapeDtypeStruct((B, T, QW), F32), jax.ShapeDtypeStruct((B, T, KVW), F32),
                   jax.ShapeDtypeStruct((B, T, KVW), F32), jax.ShapeDtypeStruct((8, LANES), F32)],
        compiler_params=_params(2),
    )(q, k, z, sink8, do, lse)


def _qknorm_bwd(cfg, l, z, gvec, cosf, sins, dqa, dka, dva, dqc, dkc, dvc):
    B, T, tm = cfg.B, cfg.T, cfg.tm

    def body(z_ref, g_ref, cos_ref, sin_ref, dqa_ref, dka_ref, dva_ref, dqc_ref, dkc_ref, dvc_ref, dz_ref, dg_ref):
        _acc_init([dg_ref])
        ind = _head_indicator()
        cos, sin = cos_ref[...], sin_ref[...]
        dqa_t, dqc_t = dqa_ref[0], dqc_ref[0]
        douts = {0: dqa_t[:, 0:128], 1: dqa_t[:, 128:256], 2: dqa_t[:, 256:384], 3: dka_ref[0],
                 5: dqc_t[:, 0:128], 6: dqc_t[:, 128:256], 7: dqc_t[:, 256:384], 8: dkc_ref[0]}
        pieces = []
        dgs = []
        for c in range(QKV_W // LANES):
            if c not in douts:
                pieces.append(dva_ref[0] if c == 4 else dvc_ref[0])
                dgs.append(jnp.zeros((1, LANES), F32))
                continue
            x = z_ref[0, :, c * LANES:(c + 1) * LANES]
            g = g_ref[0, :, c * LANES:(c + 1) * LANES]
            ss = _head_sum(x * x, ind)
            rstd = lax.rsqrt(ss * (1.0 / HEAD) + EPS)
            n = x * rstd
            dout = douts[c]
            dy = dout * cos + _pair_swap(dout * sin)
            dgs.append(jnp.sum(dy * n, axis=0, keepdims=True))
            dn = dy * g
            mean = _head_sum(dn * n, ind) * (1.0 / HEAD)
            pieces.append(rstd * (dn - n * mean))
        dz_ref[0] = jnp.concatenate(pieces, axis=-1).astype(BF16)
        dg_ref[0:1, :] += jnp.concatenate(dgs, axis=-1)

    row = lambda w: pl.BlockSpec((1, tm, w), lambda b, t: (b, t, 0))
    tab = pl.BlockSpec((tm, LANES), lambda b, t: (t, 0))
    return pl.pallas_call(
        body, name="qknorm_bwd", grid=(B, cfg.nT),
        in_specs=[row(QKV_W), _layer(l, QKV_W), tab, tab, row(QW), row(KVW), row(KVW), row(QW), row(KVW), row(KVW)],
        out_specs=[row(QKV_W), _full((8, QKV_W))],
        out_shape=[jax.ShapeDtypeStruct((B, T, QKV_W), BF16), jax.ShapeDtypeStruct((8, QKV_W), F32)],
        compiler_params=_params(2),
    )(z, gvec, cosf, sins, dqa, dka, dva, dqc, dkc, dvc)


def _pool_bwd(cfg, l, dob, pooled, wp, ps):
    B, T, tm, kw = cfg.B, cfg.T, cfg.tm, cfg.kw

    def body(dob_ref, pooled_ref, wp_ref, ps_ref, du_ref, dwp_ref, dps_ref):
        t = pl.program_id(1)
        _acc_init([dwp_ref, dps_ref])
        r0, seg_lo, seg_hi, k0 = _pool_geometry(cfg, t)
        ps = ps_ref[0]
        wp = wp_ref[0]
        dmix = dob_ref[0, pl.ds(r0, tm), :].astype(F32)
        pooled = pooled_ref[0]
        dps_ref[0:1, :] += jnp.sum(dmix * _dot(pooled, wp), axis=0, keepdims=True)
        dpm = (dmix * ps).astype(BF16)
        dwp_ref[...] += _dot_tn(pooled, dpm)
        dpooled_t = _dot_nt(dpm, wp)
        dpm_w = (dob_ref[0, pl.ds(k0, kw), :].astype(F32) * ps).astype(BF16)
        dpooled_w = _dot_nt(dpm_w, wp)
        rr = r0 + lax.broadcasted_iota(jnp.int32, (tm, kw), 0)
        cc = k0 + lax.broadcasted_iota(jnp.int32, (tm, kw), 1)
        diff = rr - cc
        inseg = (cc >= seg_lo) & (cc < seg_hi)
        ccol = k0 + lax.broadcasted_iota(jnp.int32, (kw, 1), 0)
        group = lax.broadcasted_iota(jnp.int32, (tm, POOL_W), 1) // HEAD
        acc = jnp.zeros((tm, POOL_W), F32)
        for gi, w in enumerate(POOL_WINDOWS):
            h = w // 2
            band_t = jnp.where((diff >= -h) & (diff <= h - 1) & inseg, 1.0, 0.0).astype(BF16)
            hi, lo = _split_bf16(dpooled_w / _pool_count(ccol, h, seg_lo, seg_hi))
            acc = jnp.where(group == gi, _dot(band_t, hi) + _dot(band_t, lo), acc)
        du_ref[0] = (acc - dpooled_t).astype(BF16)

    row = pl.BlockSpec((1, tm, POOL_W), lambda b, t: (b, t, 0))
    return pl.pallas_call(
        body, name="pool_bwd", grid=(B, cfg.nT),
        in_specs=[pl.BlockSpec((1, T, POOL_W), lambda b, t: (b, 0, 0)), row,
                  pl.BlockSpec((1, POOL_W, POOL_W), lambda b, t: (l, 0, 0)), _layer(l, POOL_W)],
        out_specs=[row, _full((POOL_W, POOL_W)), _full((8, POOL_W))],
        out_shape=[jax.ShapeDtypeStruct((B, T, POOL_W), BF16), jax.ShapeDtypeStruct((POOL_W, POOL_W), F32),
                   jax.ShapeDtypeStruct((8, POOL_W), F32)],
        compiler_params=_params(2),
    )(dob, pooled, wp, ps)


def _in_proj_bwd(cfg, l, dzq, du, dzg, w_in, x, dx1, gn, mod, *, latent_only, exch=None):
    B, S, T, D, IN, tm, nC = cfg.B, cfg.S, cfg.T, cfg.D, cfg.IN, cfg.tm, cfg.nC

    def body(dzq_ref, du_ref, dzg_ref, w_ref, x_ref, dx1_ref, gn_ref, mod_ref, dx0_ref, dz_ref, dmod_ref, dgn_ref):
        b, t = pl.program_id(0), pl.program_id(1)
        _acc_init([dmod_ref, dgn_ref])
        row = jnp.where(t < nC, B, b)
        dz = jnp.concatenate([dzq_ref[0], du_ref[0], dzg_ref[0]], axis=-1)
        dz_ref[0] = dz
        dh = _dot_nt(dz, w_ref[...])
        gn = gn_ref[0]
        scale = _mod_row(mod_ref, row, 1, D)
        _, xhat, rstd = _modulate(x_ref[0], gn, _mod_row(mod_ref, row, 0, D), scale)
        dxn, d_shift, d_scale, d_gn = _modulate_bwd(dh, xhat, rstd, gn, scale)
        dx0_ref[0] = dx1_ref[0] + dxn
        dmod_ref[pl.ds(row, 1), :] += jnp.concatenate([d_shift, d_scale], axis=-1)
        dgn_ref[0:1, :] += d_gn

    row = lambda w: pl.BlockSpec((1, tm, w), lambda b, t: (b, t, 0))
    if latent_only:
        dx0_spec = pl.BlockSpec((1, tm, D), lambda b, t: (b, jnp.maximum(t - nC, 0), 0))
        dx0_shape = jax.ShapeDtypeStruct((B, S, D), F32)
    else:
        dx0_spec, dx0_shape = row(D), jax.ShapeDtypeStruct((B, T, D), F32)
    return _pcall(exch)(
        body, name="in_proj_bwd", grid=(B, cfg.nT),
        in_specs=[row(QKV_W), row(POOL_W), row(3 * D), _full((D, IN)), row(D), row(D), _layer(l, D), _full((8, 6 * D))],
        out_specs=[dx0_spec, row(IN), _full((8, 2 * D)), _full((8, D))],
        out_shape=[dx0_shape, jax.ShapeDtypeStruct((B, T, IN), BF16),
                   jax.ShapeDtypeStruct((8, 2 * D), F32), jax.ShapeDtypeStruct((8, D), F32)],
        compiler_params=_params(2),
    )(dzq, du, dzg, w_in, x, dx1, gn, mod)


def _adaln_bwd(cfg, l, cc_all, dm_all, w_ada):
    d, B = cfg.D, cfg.B
    wa = w_ada.shape[2]

    def body(c_ref, dm_ref, w_ref, dw_ref, dc_ref):
        c = c_ref[...]
        s = jax.nn.sigmoid(c)
        dmb = dm_ref[...].astype(BF16)
        dw_ref[...] = _dot_tn((c * s).astype(BF16), dmb)
        dc = _dot_nt(dmb, w_ref[0].astype(BF16)) * (s * (1.0 + c * (1.0 - s)))
        is_ctx = lax.broadcasted_iota(jnp.int32, (8 * N_DEV, 1), 0) % 8 == B
        dc_ref[...] = jnp.broadcast_to(jnp.sum(jnp.where(is_ctx, dc, 0.0), axis=0, keepdims=True), (8, d))

    return pl.pallas_call(
        body, name="adaln_bwd", grid=(1,),
        in_specs=[_full((8 * N_DEV, d)), _full((8 * N_DEV, wa)), pl.BlockSpec((1, d, wa), lambda *_: (l, 0, 0))],
        out_specs=[_full((d, wa)), _full((8, d))],
        out_shape=[jax.ShapeDtypeStruct((d, wa), F32), jax.ShapeDtypeStruct((8, d), F32)],
        compiler_params=_params(1),
    )(cc_all, dm_all, w_ada)


def _dmod_pack(cfg, dmod_in, dg1, dmod_mlp):
    d = cfg.D
    wa = 6 * d // N_DEV

    def body(din_ref, dg1_ref, dmlp_ref, o_ref, db_ref):
        dm = jnp.concatenate([din_ref[...], dg1_ref[...], dmlp_ref[...]], axis=-1)
        for j in range(N_DEV):
            o_ref[j] = dm[:, j * wa:(j + 1) * wa]
        db_ref[...] = jnp.broadcast_to(jnp.sum(dm, axis=0, keepdims=True), (8, 6 * d))

    return pl.pallas_call(
        body, name="dmod_pack", grid=(1,),
        in_specs=[_full((8, 2 * d)), _full((8, d)), _full((8, 3 * d))],
        out_specs=[_full((N_DEV, 8, wa)), _full((8, 6 * d))],
        out_shape=[jax.ShapeDtypeStruct((N_DEV, 8, wa), F32), jax.ShapeDtypeStruct((8, 6 * d), F32)],
        compiler_params=_params(1),
    )(dmod_in, dg1, dmod_mlp)


def _adam_update(g, w, m, v):
    bc1 = 1.0 - ADAM_B1 ** ADAM_STEP
    bc2 = 1.0 - ADAM_B2 ** ADAM_STEP
    m2 = ADAM_B1 * m + (1.0 - ADAM_B1) * g
    v2 = ADAM_B2 * v + (1.0 - ADAM_B2) * (g * g)
    delta = -ADAM_LR * ((m2 / bc1) / (jnp.sqrt(v2 / bc2) + ADAM_EPS) + ADAM_WD * w)
    return delta, m2, v2


def _sum_parts(p_ref):
    g = p_ref[0].astype(F32)
    for d in range(1, p_ref.shape[0]):
        g = g + p_ref[d].astype(F32)
    return g


def _adamw_sharded(parts, w, m, v, name):
    L, K, W = w.shape
    min_rows = min(c.shape[1] for chunks in parts for c in chunks)
    tk = next(c for c in (256, 128, 64, 32, 16, 8) if K % c == 0 and min_rows % c == 0)
    spans, flat = [], []
    for li, chunks in enumerate(parts):
        row = 0
        for c in chunks:
            assert c.shape[1] % tk == 0
            spans.append((li, row // tk, (row + c.shape[1]) // tk))
            flat.append(c)
            row += c.shape[1]
        assert row == K

    def body(*refs):
        p_refs = refs[:len(flat)]
        w_ref, m_ref, v_ref, g_ref, d_ref, m2_ref, v2_ref = refs[len(flat):]
        layer, i = pl.program_id(0), pl.program_id(1)

        def run(p_ref):
            g = _sum_parts(p_ref)
            delta, m2, v2 = _adam_update(g, w_ref[0], m_ref[0], v_ref[0])
            g_ref[0] = g
            d_ref[0] = delta
            m2_ref[0] = m2
            v2_ref[0] = v2

        for (li, lo, hi), p_ref in zip(spans, p_refs):
            pl.when((layer == li) & (i >= lo) & (i < hi))(functools.partial(run, p_ref))

    blk = pl.BlockSpec((1, tk, W), lambda l, i: (l, i, 0))

    def part_spec(span, arr):
        li, lo, hi = span
        return pl.BlockSpec((arr.shape[0], tk, W), lambda l, i: (0, jnp.where((l == li) & (i >= lo) & (i < hi), i - lo, 0), 0))

    return pl.pallas_call(
        body, name=name, grid=(L, K // tk),
        in_specs=[part_spec(sp, arr) for sp, arr in zip(spans, flat)] + [blk, blk, blk],
        out_specs=[blk] * 4, out_shape=[jax.ShapeDtypeStruct((L, K, W), F32)] * 4,
        compiler_params=_params(2),
    )(*flat, w, m, v)


def _adamw_packed(parts, w, m, v, name):
    rows = w.shape[0]
    tr = next(c for c in (256, 128, 64, 32, 16, 8) if rows % c == 0)

    def body(p_ref, w_ref, m_ref, v_ref, g_ref, d_ref, m2_ref, v2_ref):
        g = _sum_parts(p_ref)
        delta, m2, v2 = _adam_update(g, w_ref[...], m_ref[...], v_ref[...])
        g_ref[...] = g
        d_ref[...] = delta
        m2_ref[...] = m2
        v2_ref[...] = v2

    blk = pl.BlockSpec((tr, PACK_W), lambda i: (i, 0))
    return pl.pallas_call(
        body, name=name, grid=(rows // tr,),
        in_specs=[pl.BlockSpec((N_DEV, tr, PACK_W), lambda i: (0, i, 0)), blk, blk, blk],
        out_specs=[blk] * 4, out_shape=[jax.ShapeDtypeStruct((rows, PACK_W), F32)] * 4,
        compiler_params=_params(1),
    )(parts, w, m, v)


_SHARDED = dict(w_ada=True, w_in=True, w_br_a=True, w_br_b=True, w_br_c=True, w_out=False, w_mlp1=True, w_mlp2=False)
_MERGE_WEIGHTS = ("w_br_a", "w_br_b", "w_br_c", "w_out")
_GATHERED = ("w_in",) + _MERGE_WEIGHTS + ("w_mlp1", "w_mlp2")
_KEEP_SHARDS = ("w_mlp1",)
_SMALL = ("c_ctx", "b_ada", "norm1", "norm2", "q_norm_a", "k_norm_a", "q_norm_c", "k_norm_c", "sink_c", "w_pool", "pool_scale")


def _from_shards(name, g):
    n, k, w = g.shape
    if name in _KEEP_SHARDS:
        return g
    if _SHARDED[name]:
        return g.transpose(1, 0, 2).reshape(k, n * w)
    return g.reshape(n * k, w)


def _to_shards(name, g):
    if g.ndim == 3:
        return g
    if _SHARDED[name]:
        k, nw = g.shape
        return g.reshape(k, N_DEV, nw // N_DEV).transpose(1, 0, 2)
    nk, w = g.shape
    return g.reshape(N_DEV, nk // N_DEV, w)


def _pack_small(vals):
    flat = jnp.concatenate([vals[n].reshape(-1) for n in _SMALL])
    rows = -(-flat.shape[0] // (8 * PACK_W)) * 8
    return jnp.pad(flat, (0, rows * PACK_W - flat.shape[0])).reshape(rows, PACK_W)


def _unpack_small(packed, like):
    flat, out, r = packed.reshape(-1), {}, 0
    for n in _SMALL:
        sz = like[n].size
        out[n] = flat[r:r + sz].reshape(like[n].shape)
        r += sz
    return out


def _rope_tables(cfg):
    pos = jnp.arange(cfg.S, dtype=F32)
    r = jnp.floor(pos / GRID_W)
    col = pos - r * GRID_W
    inv = 1.0 / (ROPE_THETA ** (jnp.arange(0, HEAD // 2, 2, dtype=F32) / (HEAD // 2)))
    ang = jnp.concatenate([r[:, None] * inv, col[:, None] * inv], axis=-1)
    cos = jnp.repeat(jnp.cos(ang), 2, axis=-1)
    sin = jnp.repeat(jnp.sin(ang), 2, axis=-1) * jnp.tile(jnp.array([-1.0, 1.0], F32), HEAD // 2)
    cos = jnp.concatenate([jnp.ones((cfg.N, HEAD), F32), cos], axis=0)
    sin = jnp.concatenate([jnp.zeros((cfg.N, HEAD), F32), sin], axis=0)
    return jnp.tile(cos, (1, 2)), jnp.tile(sin, (1, 2))


def _gvec(qa, ka, qc, kc):
    one = jnp.ones((qa.shape[0], KVW), F32)
    t = lambda a, n: jnp.tile(a, (1, n))
    return jnp.concatenate([t(qa, N_QHEADS), t(ka, N_KV), one, t(qc, N_QHEADS), t(kc, N_KV), one], axis=-1)[:, None, :]


def _block_diag(wp):
    L, g, c, _ = wp.shape
    eye = jnp.eye(g, dtype=wp.dtype)
    return (wp[:, :, :, None, :] * eye[None, :, None, :, None]).reshape(L, g * c, g * c)


def _pad8(a):
    return jnp.pad(a, ((0, 8 - a.shape[0]), (0, 0)))


def kernel(x, c, ctx, c_ctx, w_ada, b_ada, norm1, norm2, w_in, q_norm_a, k_norm_a, q_norm_c, k_norm_c, sink_c, w_pool, pool_scale, w_br_a, w_br_b, w_br_c, w_out, w_mlp1, w_mlp2, loss_target, m_c_ctx, m_w_ada, m_b_ada, m_norm1, m_norm2, m_w_in, m_q_norm_a, m_k_norm_a, m_q_norm_c, m_k_norm_c, m_sink_c, m_w_pool, m_pool_scale, m_w_br_a, m_w_br_b, m_w_br_c, m_w_out, m_w_mlp1, m_w_mlp2, v_c_ctx, v_w_ada, v_b_ada, v_norm1, v_norm2, v_w_in, v_q_norm_a, v_k_norm_a, v_q_norm_c, v_k_norm_c, v_sink_c, v_w_pool, v_pool_scale, v_w_br_a, v_w_br_b, v_w_br_c, v_w_out, v_w_mlp1, v_w_mlp2):
    B, S, D = x.shape
    N = ctx.shape[1]
    L = w_ada.shape[0]
    cfg = _Cfg(B, S, N, D)
    T = cfg.T
    weights = dict(c_ctx=c_ctx, w_ada=w_ada, b_ada=b_ada, norm1=norm1, norm2=norm2, w_in=w_in, q_norm_a=q_norm_a,
                   k_norm_a=k_norm_a, q_norm_c=q_norm_c, k_norm_c=k_norm_c, sink_c=sink_c, w_pool=w_pool,
                   pool_scale=pool_scale, w_br_a=w_br_a, w_br_b=w_br_b, w_br_c=w_br_c, w_out=w_out, w_mlp1=w_mlp1, w_mlp2=w_mlp2)
    mom_m = dict(c_ctx=m_c_ctx, w_ada=m_w_ada, b_ada=m_b_ada, norm1=m_norm1, norm2=m_norm2, w_in=m_w_in, q_norm_a=m_q_norm_a,
                 k_norm_a=m_k_norm_a, q_norm_c=m_q_norm_c, k_norm_c=m_k_norm_c, sink_c=m_sink_c, w_pool=m_w_pool,
                 pool_scale=m_pool_scale, w_br_a=m_w_br_a, w_br_b=m_w_br_b, w_br_c=m_w_br_c, w_out=m_w_out, w_mlp1=m_w_mlp1, w_mlp2=m_w_mlp2)
    mom_v = dict(c_ctx=v_c_ctx, w_ada=v_w_ada, b_ada=v_b_ada, norm1=v_norm1, norm2=v_norm2, w_in=v_w_in, q_norm_a=v_q_norm_a,
                 k_norm_a=v_k_norm_a, q_norm_c=v_q_norm_c, k_norm_c=v_k_norm_c, sink_c=v_sink_c, w_pool=v_w_pool,
                 pool_scale=v_pool_scale, w_br_a=v_w_br_a, w_br_b=v_w_br_b, w_br_c=v_w_br_c, w_out=v_w_out, w_mlp1=v_w_mlp1, w_mlp2=v_w_mlp2)

    shards_bf16 = {n: weights[n].astype(BF16) for n in _GATHERED}
    full = [dict() for _ in range(L)]

    def gather_of(items):
        return _Exchange([(shards_bf16[n], l) for l, n in items], scatter=False)

    def gathered(items, arrs):
        for (l, n), a in zip(items, arrs):
            full[l][n] = _from_shards(n, a)

    gathered([(0, "w_in")], [_gather_two_level(shards_bf16["w_in"], 0, "gather_first_weights")])

    def hosting(fn, *a, exch=None, done=None, **kw):
        if exch is None:
            return fn(*a, **kw)
        res = fn(*a, exch=exch, **kw)
        done(res[-exch.n:])
        own = res[:-exch.n]
        return own[0] if len(own) == 1 else own

    def gather_behind(l, names):
        if l >= L:
            return {}
        items = [(l, n) for n in names]
        return dict(exch=gather_of(items), done=functools.partial(gathered, items))

    cosf, sins = _rope_tables(cfg)
    xs = jnp.concatenate([ctx, x], axis=1)
    cc8 = _pad8(jnp.concatenate([c, c_ctx[None, :]], axis=0))
    va_blk, vc_blk = (QW + KVW) // KVW, (2 * QW + 3 * KVW) // KVW
    per_layer = lambda a: a[:, None, :]
    b_ada3, norm1_3, norm2_3, ps3 = per_layer(b_ada), per_layer(norm1), per_layer(norm2), per_layer(pool_scale)
    gvec = _gvec(q_norm_a, k_norm_a, q_norm_c, k_norm_c)
    sink8 = jnp.pad(sink_c[:, None, :], ((0, 0), (0, 7), (0, LANES - N_QHEADS)))
    wp = _block_diag(w_pool).astype(BF16)

    cc_all = _Exchange([cc8], scatter=False).alone("gather_cond")[0].reshape(8 * N_DEV, D)
    mod_cols = _Exchange([_adaln_fwd(cfg, cc_all, w_ada)], scatter=True).alone("scatter_mod")[0]
    mod_all = _adaln_join(cfg, mod_cols, b_ada3)

    saved = []
    for l in range(L):
        fw = full[l]
        ctx_active = l < L - 1
        mod = mod_all[l]
        z, h = hosting(_in_proj_fwd, cfg, l, xs, norm1_3, mod, fw["w_in"], **gather_behind(l, _MERGE_WEIGHTS if l == 0 else ("w_mlp2",)))
        qa, ka, qc, kc = _qknorm_fwd(cfg, l, z, gvec, cosf, sins)
        oa, lse_a = hosting(_attn_fwd, cfg, l, qa, ka, z, va_blk, sink8, window=False, sink=False, ctx_attend=ctx_active, name="attn_a_fwd",
                            **gather_behind(l, ("w_mlp1",)))
        oc, lse_c = hosting(_attn_fwd, cfg, l, qc, kc, z, vc_blk, sink8, window=True, sink=True, ctx_attend=ctx_active, name="attn_c_fwd",
                            **(gather_behind(l + 1, _MERGE_WEIGHTS)))
        ob, pooled = _pool_fwd(cfg, l, z, wp, ps3)
        x1, mgo = hosting(_merge_fwd, cfg, xs, oa, ob, oc, z, mod, fw["w_br_a"], fw["w_br_b"], fw["w_br_c"], fw["w_out"],
                          ctx_active=ctx_active, **(gather_behind(0, ("w_mlp2",)) if l == 0 else {}))
        if l < L - 1:
            x2, mo, r = hosting(_mlp_fwd, cfg, l, x1, norm2_3, mod, fw["w_mlp1"], fw["w_mlp2"], ctx_active=ctx_active,
                                **gather_behind(l + 1, ("w_in",)))
        else:
            x2, mo, r, sse = _mlp_fwd(cfg, l, x1, norm2_3, mod, fw["w_mlp1"], fw["w_mlp2"], ctx_active=ctx_active, target=loss_target)
        saved.append(dict(xs=xs, mod=mod, z=z, h=h, qa=qa, ka=ka, qc=qc, kc=kc, oa=oa, oc=oc, ob=ob, pooled=pooled, x1=x1, mgo=mgo, mo=mo,
                          lse_a=lse_a, lse_c=lse_c, r=r))
        xs = x2

    dxs = xs
    loss = lax.psum(0.5 * sse[0, 0] / D, ("x", "y", "c"))

    grads = [dict() for _ in range(L)]
    parts = {}
    small = {n: [None] * L for n in _SMALL if n != "c_ctx"}
    d_c_ctx = jnp.zeros((D,), F32)
    flat2 = lambda a: a.reshape(B * T, a.shape[-1])

    def scatter_of(l, names):
        return _Exchange([_to_shards(n, grads[l][n]) for n in names], scatter=True)

    def scattered(l, names, arrs):
        for n, a in zip(names, arrs):
            parts[(l, n)] = a

    def scatter_behind(l, names):
        if l >= L:
            return {}
        return dict(exch=scatter_of(l, names), done=functools.partial(scattered, l, names))

    for l in reversed(range(L)):
        fw, sv, g = full[l], saved[l], grads[l]
        ctx_active = l < L - 1
        mod = sv["mod"]
        dx1, h2, da, dout, dmod_mlp, dgn2 = hosting(_mlp_bwd, cfg, l, sv["x1"], dxs, sv["mo"], sv["r"], norm2_3, mod, fw["w_mlp1"], fw["w_mlp2"],
                                                    ctx_active=ctx_active, **scatter_behind(l + 1, ("w_in",)))
        g["w_mlp1"] = _matmul_tn(flat2(h2), flat2(da), "dw_mlp1", by_shard=True)
        g["w_mlp2"] = _matmul_tn(flat2(sv["r"]), flat2(dout), "dw_mlp2", by_shard=False)
        doa, dob, doc, dpa, dpb, dpc, y, dmo, dzg, dg1 = _merge_bwd(
            cfg, dx1, sv["mgo"], sv["oa"], sv["ob"], sv["oc"], sv["z"], mod, fw["w_br_a"], fw["w_br_b"], fw["w_br_c"], fw["w_out"],
            ctx_active=ctx_active)
        g["w_out"] = _matmul_tn(flat2(y), flat2(dmo), "dw_out", by_shard=False)
        g["w_br_a"] = _matmul_tn(flat2(sv["oa"]), flat2(dpa), "dw_br_a", by_shard=True)
        g["w_br_b"] = _matmul_tn(flat2(sv["ob"]), flat2(dpb), "dw_br_b", by_shard=True)
        g["w_br_c"] = _matmul_tn(flat2(sv["oc"]), flat2(dpc), "dw_br_c", by_shard=True)
        z = sv["z"]
        dqa, dka, dva, _ = hosting(_attn_bwd, cfg, l, sv["qa"], sv["ka"], z, va_blk, sink8, doa, sv["lse_a"], window=False, sink=False,
                                   ctx_attend=ctx_active, name="attn_a_bwd", **scatter_behind(l, ("w_mlp1", "w_mlp2")))
        dqc, dkc, dvc, dsink = hosting(_attn_bwd, cfg, l, sv["qc"], sv["kc"], z, vc_blk, sink8, doc, sv["lse_c"], window=True, sink=True,
                                       ctx_attend=ctx_active, name="attn_c_bwd", **scatter_behind(l, _MERGE_WEIGHTS))
        dzq, dgvec = _qknorm_bwd(cfg, l, z, gvec, cosf, sins, dqa, dka, dva, dqc, dkc, dvc)
        du, dwp, dps = _pool_bwd(cfg, l, dob, sv["pooled"], wp, ps3)
        dxs, dz, dmod_in, dgn1 = _in_proj_bwd(cfg, l, dzq, du, dzg, fw["w_in"], sv["xs"], dx1, norm1_3, mod, latent_only=(l == 0))
        dmod_cols, dbias = _dmod_pack(cfg, dmod_in, dg1, dmod_mlp)
        dmod_exchange = _Exchange([dmod_cols], scatter=True)
        if l > 0:
            g["w_in"], dm_all = _matmul_tn(flat2(sv["h"]), flat2(dz), "dw_in", by_shard=False, exch=dmod_exchange)
        else:
            half = D // 2
            g_lo, dm_all = _matmul_tn(flat2(sv["h"]), flat2(dz), "dw_in_lo", by_shard=False, a_cols=(0, half), exch=dmod_exchange)
            g_hi, parts_lo = _matmul_tn(flat2(sv["h"]), flat2(dz), "dw_in_hi", by_shard=False, a_cols=(half, half),
                                        exch=_Exchange([_to_shards("w_in", g_lo)], scatter=True))
        g["w_ada"], dcc = _adaln_bwd(cfg, l, cc_all, dm_all.reshape(8 * N_DEV, -1), w_ada)
        d_c_ctx = d_c_ctx + dcc[0]
        gv = dgvec[0]
        heads = lambda v, n: v.reshape(n, HEAD).sum(axis=0)
        small["b_ada"][l] = dbias[0]
        small["norm1"][l] = dgn1[0]
        small["norm2"][l] = dgn2[0]
        small["q_norm_a"][l] = heads(gv[0:QW], N_QHEADS)
        small["k_norm_a"][l] = heads(gv[QW:QW + KVW], N_KV)
        small["q_norm_c"][l] = heads(gv[QW + 2 * KVW:2 * QW + 2 * KVW], N_QHEADS)
        small["k_norm_c"][l] = heads(gv[2 * QW + 2 * KVW:2 * QW + 3 * KVW], N_KV)
        small["sink_c"][l] = dsink[0, :N_QHEADS]
        small["w_pool"][l] = jnp.stack([dwp[i * HEAD:(i + 1) * HEAD, i * HEAD:(i + 1) * HEAD] for i in range(len(POOL_WINDOWS))])
        small["pool_scale"][l] = dps[0]
    grad_x = dxs

    small_vals = {n: jnp.stack(v) for n, v in small.items()}
    small_vals["c_ctx"] = d_c_ctx
    small_packed = _pack_small(small_vals)
    parts_hi, small_parts = _Exchange([_to_shards("w_in", g_hi), jnp.broadcast_to(small_packed[None], (N_DEV,) + small_packed.shape)],
                                      scatter=True).alone("scatter_last_grads")
    chunks = {(l, n): [parts[(l, n)]] for l in range(L) for n in _GATHERED if (l, n) in parts}
    chunks[(0, "w_in")] = [parts_lo, parts_hi]
    for l in range(L):
        chunks[(l, "w_ada")] = [grads[l]["w_ada"][None]]
    stepped = {n: _adamw_sharded([chunks[(l, n)] for l in range(L)], weights[n], mom_m[n], mom_v[n], "adamw_" + n) for n in _SHARDED}
    stepped_small = _adamw_packed(small_parts, _pack_small(weights), _pack_small(mom_m), _pack_small(mom_v), "adamw_small")

    outs = []
    for i in range(4):
        res = {n: stepped[n][i] for n in _SHARDED}
        res.update(_unpack_small(stepped_small[i], weights))
        outs.append(res)
    order = ("c_ctx", "w_ada", "b_ada", "norm1", "norm2", "w_in", "q_norm_a", "k_norm_a", "q_norm_c", "k_norm_c", "sink_c",
             "w_pool", "pool_scale", "w_br_a", "w_br_b", "w_br_c", "w_out", "w_mlp1", "w_mlp2")
    return (loss, grad_x, *[res[n] for res in outs for n in order])
```

```python
import functools

import jax
import jax.numpy as jnp
from jax import lax
from jax.experimental import pallas as pl
from jax.experimental.pallas import tpu as pltpu

F32 = jnp.float32
BF16 = jnp.bfloat16

N_DEV = 8
HEAD = 64
N_QHEADS = 6
N_KV = 2
GROUP = 3
QW = N_QHEADS * HEAD
KVW = N_KV * HEAD
QKV_W = 2 * (QW + 2 * KVW)
POOL_W = 256
POOL_WINDOWS = (2, 4, 8, 16)
POOL_HALO = 16
GATE0 = QKV_W + POOL_W
WINDOW = 128
GRID_W = 64
ROPE_THETA = 10000.0
EPS = 1e-6
NEG = -1e30
QSCALE = HEAD ** -0.5
LANES = 128
PACK_W = 1024
VMEM_LIMIT = 56 * 1024 * 1024

ADAM_LR = 0.001
ADAM_B1 = 0.9
ADAM_B2 = 0.999
ADAM_EPS = 1e-08
ADAM_WD = 0.01
ADAM_STEP = 10

NT_DIMS = (((1,), (1,)), ((), ()))
TN_DIMS = (((0,), (0,)), ((), ()))


def _dot(a, b):
    return jnp.dot(a, b, preferred_element_type=F32)


def _dot_nt(a, b):
    return lax.dot_general(a, b, NT_DIMS, preferred_element_type=F32)


def _dot_tn(a, b):
    return lax.dot_general(a, b, TN_DIMS, preferred_element_type=F32)


def _params(n_grid):
    return pltpu.CompilerParams(dimension_semantics=("arbitrary",) * n_grid, vmem_limit_bytes=VMEM_LIMIT)


def _full(shape):
    nd = len(shape)
    return pl.BlockSpec(shape, lambda *_: (0,) * nd)


def _layer(l, width):
    return pl.BlockSpec((1, 1, width), lambda *_: (l, 0, 0))


def _modulate(x, gn, shift, scale):
    rstd = lax.rsqrt(jnp.mean(x * x, axis=-1, keepdims=True) + EPS)
    xhat = x * rstd
    return xhat * gn * (1.0 + scale) + shift, xhat, rstd


def _modulate_bwd(dh, xhat, rstd, gn, scale):
    d_shift = jnp.sum(dh, axis=0, keepdims=True)
    d_scale = jnp.sum(dh * xhat * gn, axis=0, keepdims=True)
    dy = dh * (1.0 + scale)
    d_gn = jnp.sum(dy * xhat, axis=0, keepdims=True)
    dxh = dy * gn
    dx = rstd * (dxh - xhat * jnp.mean(dxh * xhat, axis=-1, keepdims=True))
    return dx, d_shift, d_scale, d_gn


def _mod_row(mod_ref, row, k, d):
    return mod_ref[pl.ds(row, 1), k * d:(k + 1) * d]


class _Cfg:
    def __init__(self, b, s, n, d):
        self.B, self.S, self.N, self.D = b, s, n, d
        self.T = n + s
        self.F = 4 * d
        self.IN = GATE0 + 3 * d
        self.tm = 256 if (n % 256 == 0 and s % 256 == 0) else 128
        self.nT = self.T // self.tm
        self.nC = n // self.tm
        self.gw = 512 if d % 512 == 0 else 256
        self.kw = self.tm + 2 * POOL_HALO
        assert GATE0 % self.gw == 0 and d % self.gw == 0 and b < 8 and self.T >= self.kw and max(POOL_WINDOWS) // 2 <= POOL_HALO
        assert s % GRID_W == 0 and n % self.tm == 0 and s % self.tm == 0 and s >= self.tm + 2 * WINDOW
        assert d % (N_DEV * LANES) == 0


def _peer(k):
    x, y, c = lax.axis_index("x"), lax.axis_index("y"), lax.axis_index("c")
    px = x ^ ((k >> 2) & 1)
    py = y ^ ((k >> 1) & 1)
    pc = c ^ (k & 1)
    return (px, py, pc), 4 * px + 2 * py + pc


class _Exchange:
    def __init__(self, arrays, scatter):
        self.arrays = [a if isinstance(a, tuple) else (a, None) for a in arrays]
        self.scatter = scatter
        self.n = len(self.arrays)

    def operands(self):
        return [a for a, _ in self.arrays]

    def out_shapes(self):
        res = []
        for a, layer in self.arrays:
            shape = a.shape[1:] if (self.scatter or layer is not None) else a.shape
            res.append(jax.ShapeDtypeStruct((N_DEV,) + tuple(shape), a.dtype))
        return res

    def scratch(self):
        n = self.n * (N_DEV - 1)
        return [pltpu.SemaphoreType.DMA((n,)), pltpu.SemaphoreType.DMA((n,)), pltpu.SemaphoreType.DMA((self.n,))]

    def _copies(self, x_refs, out_refs, send_sems, recv_sems, local_sems, want):
        _, me = _peer(0)
        res = []
        for i, ((_, layer), x_ref, out_ref) in enumerate(zip(self.arrays, x_refs, out_refs)):
            if self.scatter:
                src_of = lambda d, x_ref=x_ref: x_ref.at[d]
            elif layer is not None:
                src_of = lambda d, x_ref=x_ref, layer=layer: x_ref.at[layer]
            else:
                src_of = lambda d, x_ref=x_ref: x_ref
            if want == "local":
                res.append(pltpu.make_async_copy(src_of(me), out_ref.at[me], local_sems.at[i]))
                continue
            for k in range(1, N_DEV):
                pos, idx = _peer(k)
                j = i * (N_DEV - 1) + k - 1
                common = dict(send_sem=send_sems.at[j], recv_sem=recv_sems.at[j], device_id=pos, device_id_type=pl.DeviceIdType.MESH)
                if want == "send":
                    res.append(pltpu.make_async_remote_copy(src_ref=src_of(idx), dst_ref=out_ref.at[me], **common))
                else:
                    res.append(pltpu.make_async_remote_copy(src_ref=src_of(me), dst_ref=out_ref.at[idx], **common))
        return res

    def start(self, *refs):
        for cp in self._copies(*refs, "local") + self._copies(*refs, "send"):
            cp.start()

    def wait(self, *refs):
        for cp in self._copies(*refs, "recv"):
            cp.wait_recv()
        for cp in self._copies(*refs, "send"):
            cp.wait_send()
        for cp in self._copies(*refs, "local"):
            cp.wait()

    def alone(self, name):
        n = self.n

        def body(*refs):
            args = (refs[:n], refs[n:2 * n], *refs[2 * n:])
            self.start(*args)
            self.wait(*args)

        any_spec = pl.BlockSpec(memory_space=pl.ANY)
        return pl.pallas_call(body, name=name, in_specs=[any_spec] * n, out_specs=[any_spec] * n,
                              out_shape=self.out_shapes(), scratch_shapes=self.scratch())(*self.operands())


def _gather_two_level(x, layer, name):
    shape = x.shape[1:]

    def body(x_ref, out_ref, send_sems, recv_sems, local_sem):
        mx, my, mc = lax.axis_index("x"), lax.axis_index("y"), lax.axis_index("c")
        me, sibling = (mx, my, mc), (mx, my, 1 - mc)
        chips = [(1 - mx, my), (mx, 1 - my), (1 - mx, 1 - my)]
        src = x_ref.at[layer]

        def slot(px, py, pc):
            return out_ref.at[4 * px + 2 * py + pc]

        def copy(k, block, to, from_src=False):
            return pltpu.make_async_remote_copy(src_ref=src if from_src else slot(*block), dst_ref=slot(*block), send_sem=send_sems.at[k],
                                                recv_sem=recv_sems.at[k], device_id=to, device_id_type=pl.DeviceIdType.MESH)

        mine = pltpu.make_async_copy(src, slot(*me), local_sem)
        mine.start()
        first = [copy(0, me, sibling, True)] + [copy(1 + j, me, (*chip, mc), True) for j, chip in enumerate(chips)]
        for cp in first:
            cp.start()
        passed = [copy(4 + j, (*chip, mc), sibling) for j, chip in enumerate(chips)]
        for j, chip in enumerate(chips):
            copy(1 + j, (*chip, mc), me).wait_recv()
            passed[j].start()
        copy(0, sibling, me).wait_recv()
        for j, chip in enumerate(chips):
            copy(4 + j, (*chip, 1 - mc), me).wait_recv()
        for cp in first + passed:
            cp.wait_send()
        mine.wait()

    any_spec = pl.BlockSpec(memory_space=pl.ANY)
    return pl.pallas_call(
        body, name=name, in_specs=[any_spec], out_specs=any_spec,
        out_shape=jax.ShapeDtypeStruct((N_DEV,) + tuple(shape), x.dtype),
        scratch_shapes=[pltpu.SemaphoreType.DMA((N_DEV - 1,)), pltpu.SemaphoreType.DMA((N_DEV - 1,)), pltpu.SemaphoreType.DMA],
    )(x)


def _pcall(exch):
    if exch is None:
        return pl.pallas_call

    def make(body, *, name, grid, in_specs, out_specs, out_shape, compiler_params, scratch_shapes=()):
        multi = isinstance(out_shape, (list, tuple))
        out_specs_l = list(out_specs) if multi else [out_specs]
        out_shape_l = list(out_shape) if multi else [out_shape]
        n_in, n_out, n_x, n_s = len(in_specs), len(out_specs_l), exch.n, len(scratch_shapes)

        def hosted(*refs):
            ins, x_refs = refs[:n_in], refs[n_in:n_in + n_x]
            o0 = n_in + n_x
            outs, xo_refs = refs[o0:o0 + n_out], refs[o0 + n_out:o0 + n_out + n_x]
            s0 = o0 + n_out + n_x
            own_scratch, sems = refs[s0:s0 + n_s], refs[s0 + n_s:]
            ids = [pl.program_id(i) for i in range(len(grid))]
            first = functools.reduce(jnp.logical_and, [i == 0 for i in ids])
            last = functools.reduce(jnp.logical_and, [i == g - 1 for i, g in zip(ids, grid)])

            @pl.when(first)
            def _():
                exch.start(x_refs, xo_refs, *sems)

            body(*ins, *outs, *own_scratch)

            @pl.when(last)
            def _():
                exch.wait(x_refs, xo_refs, *sems)

        any_spec = pl.BlockSpec(memory_space=pl.ANY)
        call = pl.pallas_call(
            hosted, name=name, grid=grid, in_specs=list(in_specs) + [any_spec] * n_x, out_specs=out_specs_l + [any_spec] * n_x,
            out_shape=out_shape_l + exch.out_shapes(), scratch_shapes=list(scratch_shapes) + exch.scratch(),
            compiler_params=compiler_params)
        return lambda *args: call(*args, *exch.operands())

    return make


def _adaln_fwd(cfg, cc_all, w_ada):
    d = cfg.D
    L, _, wa = w_ada.shape

    def body(c_ref, w_ref, o_ref):
        c = c_ref[...]
        a = (c * jax.nn.sigmoid(c)).astype(BF16)
        for l in range(L):
            m = _dot(a, w_ref[l].astype(BF16))
            for p in range(N_DEV):
                o_ref[p, l] = m[8 * p:8 * (p + 1)]

    return pl.pallas_call(
        body, name="adaln_fwd", grid=(1,),
        in_specs=[_full((8 * N_DEV, d)), _full((L, d, wa))],
        out_specs=_full((N_DEV, L, 8, wa)),
        out_shape=jax.ShapeDtypeStruct((N_DEV, L, 8, wa), F32), compiler_params=_params(1),
    )(cc_all, w_ada)


def _adaln_join(cfg, parts, b_ada):
    d = cfg.D
    _, L, _, wa = parts.shape

    def body(p_ref, b_ref, o_ref):
        for l in range(L):
            for j in range(N_DEV):
                o_ref[l, :, j * wa:(j + 1) * wa] = p_ref[j, l] + b_ref[l, :, j * wa:(j + 1) * wa]

    return pl.pallas_call(
        body, name="adaln_join", grid=(1,),
        in_specs=[_full((N_DEV, L, 8, wa)), _full((L, 1, 6 * d))],
        out_specs=_full((L, 8, 6 * d)),
        out_shape=jax.ShapeDtypeStruct((L, 8, 6 * d), F32), compiler_params=_params(1),
    )(parts, b_ada)


def _in_proj_fwd(cfg, l, x, gn, mod, w_in, exch=None):
    B, T, D, IN, tm, nC = cfg.B, cfg.T, cfg.D, cfg.IN, cfg.tm, cfg.nC

    def body(x_ref, gn_ref, mod_ref, w_ref, z_ref, h_ref):
        b, t = pl.program_id(0), pl.program_id(1)
        row = jnp.where(t < nC, B, b)
        h, _, _ = _modulate(x_ref[0], gn_ref[0], _mod_row(mod_ref, row, 0, D), _mod_row(mod_ref, row, 1, D))
        hb = h.astype(BF16)
        h_ref[0] = hb
        z_ref[0] = _dot(hb, w_ref[...])

    return _pcall(exch)(
        body, name="in_proj_fwd", grid=(B, cfg.nT),
        in_specs=[pl.BlockSpec((1, tm, D), lambda b, t: (b, t, 0)), _layer(l, D), _full((8, 6 * D)), _full((D, IN))],
        out_specs=[pl.BlockSpec((1, tm, IN), lambda b, t: (b, t, 0)), pl.BlockSpec((1, tm, D), lambda b, t: (b, t, 0))],
        out_shape=[jax.ShapeDtypeStruct((B, T, IN), F32), jax.ShapeDtypeStruct((B, T, D), BF16)],
        compiler_params=_params(2),
    )(x, gn, mod, w_in)


def _head_indicator():
    r = lax.broadcasted_iota(jnp.int32, (LANES, LANES), 0) // HEAD
    c = lax.broadcasted_iota(jnp.int32, (LANES, LANES), 1) // HEAD
    return jnp.where(r == c, 1.0, 0.0).astype(BF16)


def _head_sum(x, ind):
    hi = x.astype(BF16)
    lo = (x - hi.astype(F32)).astype(BF16)
    return _dot(hi, ind) + _dot(lo, ind)


def _pair_swap(y):
    lane = lax.broadcasted_iota(jnp.int32, y.shape, 1)
    return jnp.where(lane % 2 == 0, pltpu.roll(y, LANES - 1, 1), pltpu.roll(y, 1, 1))


_Q_CHUNKS = (0, 1, 2, 5, 6, 7)


def _qknorm_fwd(cfg, l, z, gvec, cosf, sins):
    B, T, tm = cfg.B, cfg.T, cfg.tm

    def body(z_ref, g_ref, cos_ref, sin_ref, qa_ref, ka_ref, qc_ref, kc_ref):
        ind = _head_indicator()
        cos, sin = cos_ref[...], sin_ref[...]

        def chunk(c):
            x = z_ref[0, :, c * LANES:(c + 1) * LANES]
            ss = _head_sum(x * x, ind)
            y = x * lax.rsqrt(ss * (1.0 / HEAD) + EPS) * g_ref[0, :, c * LANES:(c + 1) * LANES]
            out = y * cos + _pair_swap(y) * sin
            return (out * QSCALE if c in _Q_CHUNKS else out).astype(BF16)

        qa_ref[0] = jnp.concatenate([chunk(0), chunk(1), chunk(2)], axis=-1)
        ka_ref[0] = chunk(3)
        qc_ref[0] = jnp.concatenate([chunk(5), chunk(6), chunk(7)], axis=-1)
        kc_ref[0] = chunk(8)

    row = lambda w: pl.BlockSpec((1, tm, w), lambda b, t: (b, t, 0))
    tab = pl.BlockSpec((tm, LANES), lambda b, t: (t, 0))
    return pl.pallas_call(
        body, name="qknorm_fwd", grid=(B, cfg.nT),
        in_specs=[row(QKV_W), _layer(l, QKV_W), tab, tab],
        out_specs=[row(QW), row(KVW), row(QW), row(KVW)],
        out_shape=[jax.ShapeDtypeStruct((B, T, w), BF16) for w in (QW, KVW, QW, KVW)],
        compiler_params=_params(2),
    )(z, gvec, cosf, sins)


def _attn_scores(cfg, tl, q, k_ref, v_ref, sink_ref, h, loc, window, sink, lse=None):
    S, N, tq = cfg.S, cfg.N, cfg.tm
    hs = slice(h * HEAD, (h + 1) * HEAD)
    qs = jnp.concatenate([q[:, (GROUP * h + g) * HEAD:(GROUP * h + g + 1) * HEAD] for g in range(GROUP)], axis=0)
    lo = None
    if not loc:
        kk = k_ref[0, 0:N, :][:, hs]
        vv = v_ref[0, 0:N, :].astype(BF16)[:, hs]
    elif not window:
        kk = k_ref[0][:, hs]
        vv = v_ref[0].astype(BF16)[:, hs]
    else:
        W = tq + 2 * WINDOW
        lo = pl.multiple_of(jnp.clip(tl * tq - WINDOW, 0, S - W), LANES)
        kk = jnp.concatenate([k_ref[0, 0:N, :], k_ref[0, pl.ds(N + lo, W), :]], axis=0)[:, hs]
        vv = jnp.concatenate([v_ref[0, 0:N, :], v_ref[0, pl.ds(N + lo, W), :]], axis=0).astype(BF16)[:, hs]
    st = _dot_nt(kk, qs)
    if window:
        krow = lax.broadcasted_iota(jnp.int32, st.shape, 0)
        qpos = tl * tq + lax.broadcasted_iota(jnp.int32, st.shape, 1) % tq
        st = jnp.where((krow < N) | (jnp.abs(qpos - (lo + krow - N)) <= WINDOW), st, NEG)
    sk = None
    if sink:
        colg = lax.broadcasted_iota(jnp.int32, (1, GROUP * tq), 1) // tq
        sk = jnp.zeros((1, GROUP * tq), F32)
        for g in range(GROUP):
            j = GROUP * h + g
            sk = jnp.where(colg == g, sink_ref[0, 0:1, j:j + 1], sk)
    if lse is not None:
        return qs, kk, vv, jnp.exp(st - lse), None, (jnp.exp(sk - lse) if sink else None), lo, lse
    m = jnp.max(st, axis=0, keepdims=True)
    if sink:
        m = jnp.maximum(m, sk)
    e = jnp.exp(st - m)
    l = jnp.sum(e, axis=0, keepdims=True)
    e_s = None
    if sink:
        e_s = jnp.exp(sk - m)
        l = l + e_s
    return qs, kk, vv, e, 1.0 / l, e_s, lo, m + jnp.log(l)


def _attn_fwd(cfg, l, q, k, z, vblock, sink8, *, window, sink, ctx_attend, name, exch=None):
    B, T, tq, nC = cfg.B, cfg.T, cfg.tm, cfg.nC

    def body(q_ref, k_ref, v_ref, sink_ref, o_ref, lse_ref):
        t = pl.program_id(1)

        def run(loc):
            q_t = q_ref[0]
            outs = [None] * N_QHEADS
            lses = [None] * N_QHEADS
            for h in range(N_KV):
                _, _, vv, e, inv, _, _, lse = _attn_scores(cfg, t - nC, q_t, k_ref, v_ref, sink_ref, h, loc, window and loc, sink)
                o = (_dot_tn(vv, e.astype(BF16)) * inv).T
                for g in range(GROUP):
                    outs[GROUP * h + g] = o[g * tq:(g + 1) * tq]
                    lses[GROUP * h + g] = lse[:, g * tq:(g + 1) * tq]
            o_ref[0] = jnp.concatenate(outs, axis=-1).astype(BF16)
            lse_ref[0] = jnp.concatenate(lses + [jnp.zeros((8 - N_QHEADS, tq), F32)], axis=0)

        pl.when(t >= nC)(functools.partial(run, True))
        if ctx_attend:
            pl.when(t < nC)(functools.partial(run, False))
        else:
            @pl.when(t < nC)
            def _():
                o_ref[0] = jnp.zeros((tq, QW), BF16)
                lse_ref[0] = jnp.zeros((8, tq), F32)

    return _pcall(exch)(
        body, name=name, grid=(B, cfg.nT),
        in_specs=[pl.BlockSpec((1, tq, QW), lambda b, t: (b, t, 0)),
                  pl.BlockSpec((1, T, KVW), lambda b, t: (b, 0, 0)),
                  pl.BlockSpec((1, T, KVW), lambda b, t: (b, 0, vblock)),
                  pl.BlockSpec((1, 8, LANES), lambda b, t: (l, 0, 0))],
        out_specs=[pl.BlockSpec((1, tq, QW), lambda b, t: (b, t, 0)), pl.BlockSpec((1, 8, tq), lambda b, t: (b, 0, t))],
        out_shape=[jax.ShapeDtypeStruct((B, T, QW), BF16), jax.ShapeDtypeStruct((B, 8, T), F32)], compiler_params=_params(2),
    )(q, k, z, sink8)


def _pool_geometry(cfg, t):
    tm, N, T, nC = cfg.tm, cfg.N, cfg.T, cfg.nC
    r0 = pl.multiple_of(t * tm, tm)
    isctx = t < nC
    seg_lo = jnp.where(isctx, 0, N)
    seg_hi = jnp.where(isctx, N, T)
    k0 = pl.multiple_of(jnp.clip(t * tm - POOL_HALO, 0, T - cfg.kw), POOL_HALO)
    return r0, seg_lo, seg_hi, k0


def _pool_count(pos, h, seg_lo, seg_hi):
    return jnp.maximum(jnp.minimum(pos + h, seg_hi) - jnp.maximum(pos - h, seg_lo), 1).astype(F32)


def _split_bf16(x):
    hi = x.astype(BF16)
    return hi, (x - hi.astype(F32)).astype(BF16)


def _pool_fwd(cfg, l, z, wp, ps):
    B, T, tm, kw = cfg.B, cfg.T, cfg.tm, cfg.kw

    def body(u_ref, wp_ref, ps_ref, ob_ref, pooled_ref):
        t = pl.program_id(1)
        r0, seg_lo, seg_hi, k0 = _pool_geometry(cfg, t)
        hi, lo = _split_bf16(u_ref[0, pl.ds(k0, kw), :])
        rr = r0 + lax.broadcasted_iota(jnp.int32, (tm, kw), 0)
        cc = k0 + lax.broadcasted_iota(jnp.int32, (tm, kw), 1)
        diff = cc - rr
        inseg = (cc >= seg_lo) & (cc < seg_hi)
        rcol = r0 + lax.broadcasted_iota(jnp.int32, (tm, 1), 0)
        group = lax.broadcasted_iota(jnp.int32, (tm, POOL_W), 1) // HEAD
        acc = jnp.zeros((tm, POOL_W), F32)
        for gi, w in enumerate(POOL_WINDOWS):
            h = w // 2
            band = jnp.where((diff >= -h) & (diff <= h - 1) & inseg, 1.0, 0.0).astype(BF16)
            tot = _dot(band, hi) + _dot(band, lo)
            acc = jnp.where(group == gi, tot / _pool_count(rcol, h, seg_lo, seg_hi), acc)
        pooled = (acc - u_ref[0, pl.ds(r0, tm), :]).astype(BF16)
        pooled_ref[0] = pooled
        ob_ref[0] = (_dot(pooled, wp_ref[0]) * ps_ref[0]).astype(BF16)

    row = pl.BlockSpec((1, tm, POOL_W), lambda b, t: (b, t, 0))
    return pl.pallas_call(
        body, name="pool_fwd", grid=(B, cfg.nT),
        in_specs=[pl.BlockSpec((1, T, POOL_W), lambda b, t: (b, 0, QKV_W // POOL_W)),
                  pl.BlockSpec((1, POOL_W, POOL_W), lambda b, t: (l, 0, 0)), _layer(l, POOL_W)],
        out_specs=[row, row],
        out_shape=[jax.ShapeDtypeStruct((B, T, POOL_W), BF16)] * 2, compiler_params=_params(2),
    )(z, wp, ps)


def _gate_specs(cfg):
    tm, gw = cfg.tm, cfg.gw
    first = GATE0 // gw
    return [pl.BlockSpec((1, tm, gw), functools.partial(lambda b, t, j: (b, t, j), j=first + i)) for i in range(3 * cfg.D // gw)]


def _read_gates(cfg, gate_refs):
    per = cfg.D // cfg.gw
    return [jnp.concatenate([gate_refs[k * per + i][0] for i in range(per)], axis=-1) for k in range(3)]


def _merge_fwd(cfg, x, oa, ob, oc, z, mod, wa, wb, wc, wo, *, ctx_active, exch=None):
    B, T, D, tm, nC = cfg.B, cfg.T, cfg.D, cfg.tm, cfg.nC
    ng = 3 * D // cfg.gw

    def body(x_ref, oa_ref, ob_ref, oc_ref, *rest):
        gate_refs = rest[:ng]
        mod_ref, wa_ref, wb_ref, wc_ref, wo_ref, x1_ref, mgo_ref = rest[ng:]
        b, t = pl.program_id(0), pl.program_id(1)

        def compute():
            row = jnp.where(t < nC, B, b)
            ga, gb, gc = _read_gates(cfg, gate_refs)
            y = (jax.nn.sigmoid(ga) * _dot(oa_ref[0], wa_ref[...])
                 + jax.nn.sigmoid(gb) * _dot(ob_ref[0], wb_ref[...])
                 + jax.nn.sigmoid(gc) * _dot(oc_ref[0], wc_ref[...]))
            mo = _dot(y.astype(BF16), wo_ref[...])
            mgo_ref[0] = mo.astype(BF16)
            x1_ref[0] = x_ref[0] + _mod_row(mod_ref, row, 2, D) * mo

        if ctx_active:
            compute()
        else:
            pl.when(t >= nC)(compute)

            @pl.when(t < nC)
            def _():
                mgo_ref[0] = jnp.zeros((tm, D), BF16)
                x1_ref[0] = x_ref[0]

    row = lambda w: pl.BlockSpec((1, tm, w), lambda b, t: (b, t, 0))
    return _pcall(exch)(
        body, name="merge_fwd", grid=(B, cfg.nT),
        in_specs=[row(D), row(QW), row(POOL_W), row(QW)] + _gate_specs(cfg)
        + [_full((8, 6 * D)), _full((QW, D)), _full((POOL_W, D)), _full((QW, D)), _full((D, D))],
        out_specs=[row(D), row(D)],
        out_shape=[jax.ShapeDtypeStruct((B, T, D), F32), jax.ShapeDtypeStruct((B, T, D), BF16)],
        compiler_params=_params(2),
    )(x, oa, ob, oc, *([z] * ng), mod, wa, wb, wc, wo)


def _w1_apply(hb, w1_ref):
    return jnp.concatenate([_dot(hb, w1_ref[d]) for d in range(N_DEV)], axis=-1)


def _mlp_fwd(cfg, l, x1, gn, mod, w1, w2, *, ctx_active, target=None, exch=None):
    B, T, D, F, tm, nC = cfg.B, cfg.T, cfg.D, cfg.F, cfg.tm, cfg.nC
    assert target is None or not ctx_active

    def body(x_ref, gn_ref, mod_ref, w1_ref, w2_ref, *rest):
        if target is None:
            x2_ref, mo_ref, r_ref = rest
        else:
            tgt_ref, x2_ref, mo_ref, r_ref, sse_ref = rest
            _acc_init([sse_ref])
        b, t = pl.program_id(0), pl.program_id(1)

        def compute():
            row = jnp.where(t < nC, B, b)
            x = x_ref[0]
            h, _, _ = _modulate(x, gn_ref[0], _mod_row(mod_ref, row, 3, D), _mod_row(mod_ref, row, 4, D))
            a = jnp.maximum(_w1_apply(h.astype(BF16), w1_ref), 0.0)
            rb = (a * a).astype(BF16)
            r_ref[0] = rb
            mo = _dot(rb, w2_ref[...])
            mo_ref[0] = mo.astype(BF16)
            x2 = x + _mod_row(mod_ref, row, 5, D) * mo
            if target is None:
                x2_ref[0] = x2
            else:
                err = x2 - tgt_ref[0]
                x2_ref[0] = err * (1.0 / D)
                sse_ref[...] += jnp.sum(err * err)

        if ctx_active:
            compute()
        else:
            pl.when(t >= nC)(compute)

            @pl.when(t < nC)
            def _():
                mo_ref[0] = jnp.zeros((tm, D), BF16)
                r_ref[0] = jnp.zeros((tm, F), BF16)
                x2_ref[0] = x_ref[0] if target is None else jnp.zeros((tm, D), F32)

    row = pl.BlockSpec((1, tm, D), lambda b, t: (b, t, 0))
    in_specs = [row, _layer(l, D), _full((8, 6 * D)), _full((N_DEV, D, F // N_DEV)), _full((F, D))]
    out_specs = [row, row, pl.BlockSpec((1, tm, F), lambda b, t: (b, t, 0))]
    out_shape = [jax.ShapeDtypeStruct((B, T, D), F32), jax.ShapeDtypeStruct((B, T, D), BF16), jax.ShapeDtypeStruct((B, T, F), BF16)]
    args = [x1, gn, mod, w1, w2]
    if target is not None:
        in_specs.append(pl.BlockSpec((1, tm, D), lambda b, t: (b, jnp.maximum(t - nC, 0), 0)))
        out_specs.append(_full((8, LANES)))
        out_shape.append(jax.ShapeDtypeStruct((8, LANES), F32))
        args.append(target)
    return _pcall(exch)(
        body, name="mlp_fwd", grid=(B, cfg.nT), in_specs=in_specs, out_specs=out_specs, out_shape=out_shape,
        compiler_params=_params(2),
    )(*args)


def _acc_init(refs):
    b, t = pl.program_id(0), pl.program_id(1)

    @pl.when((b == 0) & (t == 0))
    def _():
        for ref in refs:
            ref[...] = jnp.zeros(ref.shape, ref.dtype)


def _mlp_bwd(cfg, l, x1, dx2, mo, r, gn, mod, w1, w2, *, ctx_active, exch=None):
    B, T, D, F, tm, nC = cfg.B, cfg.T, cfg.D, cfg.F, cfg.tm, cfg.nC
    ws = F // N_DEV

    def body(x_ref, dx_ref, mo_ref, r_ref, gn_ref, mod_ref, w1_ref, w2_ref, dx1_ref, h_ref, da_ref, dout_ref, dmod_ref, dgn_ref):
        b, t = pl.program_id(0), pl.program_id(1)
        _acc_init([dmod_ref, dgn_ref])

        def compute():
            row = jnp.where(t < nC, B, b)
            gn = gn_ref[0]
            scale = _mod_row(mod_ref, row, 4, D)
            h, xhat, rstd = _modulate(x_ref[0], gn, _mod_row(mod_ref, row, 3, D), scale)
            hb = h.astype(BF16)
            dx = dx_ref[0]
            dout = (dx * _mod_row(mod_ref, row, 5, D)).astype(BF16)
            da = (_dot_nt(dout, w2_ref[...]) * (2.0 * jnp.sqrt(r_ref[0].astype(F32)))).astype(BF16)
            dh = _dot_nt(da[:, 0:ws], w1_ref[0])
            for d in range(1, N_DEV):
                dh = dh + _dot_nt(da[:, d * ws:(d + 1) * ws], w1_ref[d])
            dxn, d_shift, d_scale, d_gn = _modulate_bwd(dh, xhat, rstd, gn, scale)
            dx1_ref[0] = dx + dxn
            h_ref[0] = hb
            da_ref[0] = da
            dout_ref[0] = dout
            d_gate = jnp.sum(dx * mo_ref[0].astype(F32), axis=0, keepdims=True)
            dmod_ref[pl.ds(row, 1), :] += jnp.concatenate([d_shift, d_scale, d_gate], axis=-1)
            dgn_ref[0:1, :] += d_gn

        if ctx_active:
            compute()
        else:
            pl.when(t >= nC)(compute)

            @pl.when(t < nC)
            def _():
                dx1_ref[0] = dx_ref[0]
                h_ref[0] = jnp.zeros((tm, D), BF16)
                da_ref[0] = jnp.zeros((tm, F), BF16)
                dout_ref[0] = jnp.zeros((tm, D), BF16)

    row = lambda w: pl.BlockSpec((1, tm, w), lambda b, t: (b, t, 0))
    sds = lambda w, dt: jax.ShapeDtypeStruct((B, T, w), dt)
    return _pcall(exch)(
        body, name="mlp_bwd", grid=(B, cfg.nT),
        in_specs=[row(D), row(D), row(D), row(F), _layer(l, D), _full((8, 6 * D)), _full((N_DEV, D, ws)), _full((F, D))],
        out_specs=[row(D), row(D), row(F), row(D), _full((8, 3 * D)), _full((8, D))],
        out_shape=[sds(D, F32), sds(D, BF16), sds(F, BF16), sds(D, BF16),
                   jax.ShapeDtypeStruct((8, 3 * D), F32), jax.ShapeDtypeStruct((8, D), F32)],
        compiler_params=_params(2),
    )(x1, dx2, mo, r, gn, mod, w1, w2)


def _matmul_tn(a, g, name, *, by_shard, a_cols=None, exch=None):
    R = a.shape[0]
    Ng = g.shape[1]
    tr = next(c for c in (2304, 1024, 512, 256, 128, 64, 32, 16, 8) if R % c == 0)
    if a_cols is None:
        Ka, a_blk = a.shape[1], 0
        tka = Ka if Ka <= 1024 else 1024
    else:
        a_start, Ka = a_cols
        tka = Ka
        assert a_start % Ka == 0 and Ka % LANES == 0
        a_blk = a_start // Ka
    if by_shard:
        ws = Ng // N_DEV
        per = next(c for c in (8, 4, 2, 1) if c * ws <= 1152 or c == 1)
        tn = per * ws
    else:
        tn = next(c for c in (1152, 1024, 768, 512, 384, 256, 128) if Ng % c == 0)
    assert Ka % tka == 0 and tn % LANES == 0
    nr = R // tr

    def body(a_ref, g_ref, o_ref, acc_ref):
        r = pl.program_id(2)

        @pl.when(r == 0)
        def _():
            acc_ref[...] = jnp.zeros(acc_ref.shape, F32)

        acc_ref[...] += _dot_tn(a_ref[...], g_ref[...])

        @pl.when(r == nr - 1)
        def _():
            if by_shard:
                for d in range(per):
                    o_ref[d] = acc_ref[:, d * ws:(d + 1) * ws].astype(BF16)
            else:
                o_ref[...] = acc_ref[...].astype(BF16)

    if by_shard:
        out_spec = pl.BlockSpec((per, tka, ws), lambda i, j, r: (j, i, 0))
        out_shape = jax.ShapeDtypeStruct((N_DEV, Ka, ws), BF16)
    else:
        out_spec = pl.BlockSpec((tka, tn), lambda i, j, r: (i, j))
        out_shape = jax.ShapeDtypeStruct((Ka, Ng), BF16)
    return _pcall(exch)(
        body, name=name, grid=(Ka // tka, Ng // tn, nr),
        in_specs=[pl.BlockSpec((tr, tka), lambda i, j, r: (r, i + a_blk)), pl.BlockSpec((tr, tn), lambda i, j, r: (r, j))],
        out_specs=out_spec, out_shape=out_shape, scratch_shapes=[pltpu.VMEM((tka, tn), F32)], compiler_params=_params(3),
    )(a, g)


def _merge_bwd(cfg, dx1, mgo, oa, ob, oc, z, mod, wa, wb, wc, wo, *, ctx_active, exch=None):
    B, T, D, tm, nC = cfg.B, cfg.T, cfg.D, cfg.tm, cfg.nC
    ng = 3 * D // cfg.gw

    def body(dx_ref, mgo_ref, oa_ref, ob_ref, oc_ref, *rest):
        gate_refs = rest[:ng]
        (mod_ref, wa_ref, wb_ref, wc_ref, wo_ref,
         doa_ref, dob_ref, doc_ref, dpa_ref, dpb_ref, dpc_ref, y_ref, dmo_ref, dzg_ref, dg1_ref) = rest[ng:]
        b, t = pl.program_id(0), pl.program_id(1)
        _acc_init([dg1_ref])

        def compute():
            row = jnp.where(t < nC, B, b)
            dx = dx_ref[0]
            dg1_ref[pl.ds(row, 1), :] += jnp.sum(dx * mgo_ref[0].astype(F32), axis=0, keepdims=True)
            dmo = (dx * _mod_row(mod_ref, row, 2, D)).astype(BF16)
            dmo_ref[0] = dmo
            dy = _dot_nt(dmo, wo_ref[...])
            gates = _read_gates(cfg, gate_refs)
            y = jnp.zeros((tm, D), F32)
            dgs = []
            for gate, o_ref, w_ref, do_ref, dp_ref in ((gates[0], oa_ref, wa_ref, doa_ref, dpa_ref),
                                                      (gates[1], ob_ref, wb_ref, dob_ref, dpb_ref),
                                                      (gates[2], oc_ref, wc_ref, doc_ref, dpc_ref)):
                s = jax.nn.sigmoid(gate)
                p = _dot(o_ref[0], w_ref[...])
                y = y + s * p
                dp = (dy * s).astype(BF16)
                dp_ref[0] = dp
                do_ref[0] = _dot_nt(dp, w_ref[...]).astype(BF16)
                dgs.append((dy * p * s * (1.0 - s)).astype(BF16))
            y_ref[0] = y.astype(BF16)
            dzg_ref[0] = jnp.concatenate(dgs, axis=-1)

        if ctx_active:
            compute()
        else:
            pl.when(t >= nC)(compute)

            @pl.when(t < nC)
            def _():
                for ref in (doa_ref, dob_ref, doc_ref, dpa_ref, dpb_ref, dpc_ref, y_ref, dmo_ref, dzg_ref):
                    ref[...] = jnp.zeros(ref.shape, ref.dtype)

    row = lambda w: pl.BlockSpec((1, tm, w), lambda b, t: (b, t, 0))
    sds = lambda w: jax.ShapeDtypeStruct((B, T, w), BF16)
    return _pcall(exch)(
        body, name="merge_bwd", grid=(B, cfg.nT),
        in_specs=[row(D), row(D), row(QW), row(POOL_W), row(QW)] + _gate_specs(cfg)
        + [_full((8, 6 * D)), _full((QW, D)), _full((POOL_W, D)), _full((QW, D)), _full((D, D))],
        out_specs=[row(QW), row(POOL_W), row(QW), row(D), row(D), row(D), row(D), row(D), row(3 * D), _full((8, D))],
        out_shape=[sds(QW), sds(POOL_W), sds(QW), sds(D), sds(D), sds(D), sds(D), sds(D), sds(3 * D),
                   jax.ShapeDtypeStruct((8, D), F32)],
        compiler_params=_params(2),
    )(dx1, mgo, oa, ob, oc, *([z] * ng), mod, wa, wb, wc, wo)


def _attn_bwd(cfg, l, q, k, z, vblock, sink8, do, lse, *, window, sink, ctx_attend, name, exch=None):
    B, S, N, T, tq, nC = cfg.B, cfg.S, cfg.N, cfg.T, cfg.tm, cfg.nC

    def body(q_ref, k_ref, v_ref, sink_ref, do_ref, lse_ref, dq_ref, dk_ref, dv_ref, dsink_ref):
        b, t = pl.program_id(0), pl.program_id(1)
        _acc_init([dsink_ref])

        @pl.when(t == 0)
        def _():
            dk_ref[...] = jnp.zeros(dk_ref.shape, F32)
            dv_ref[...] = jnp.zeros(dv_ref.shape, F32)

        def run(loc):
            q_t = q_ref[0]
            do_t = do_ref[0]
            dqs = [None] * N_QHEADS
            dks, dvs = [], []
            dsink_row = jnp.zeros((1, LANES), F32)
            lane = lax.broadcasted_iota(jnp.int32, (1, LANES), 1)
            lo = None
            for h in range(N_KV):
                lse = jnp.concatenate([lse_ref[0, GROUP * h + g:GROUP * h + g + 1, :] for g in range(GROUP)], axis=1)
                qs, kk, vv, p, _, p_s, lo, _ = _attn_scores(cfg, t - nC, q_t, k_ref, v_ref, sink_ref, h, loc, window and loc, sink, lse=lse)
                dos = jnp.concatenate([do_t[:, (GROUP * h + g) * HEAD:(GROUP * h + g + 1) * HEAD] for g in range(GROUP)], axis=0)
                dp = _dot_nt(vv, dos)
                delta = jnp.sum(p * dp, axis=0, keepdims=True)
                ds = (p * (dp - delta)).astype(BF16)
                dq = _dot_tn(kk, ds).T
                dks.append(_dot(ds, qs))
                dvs.append(_dot(p.astype(BF16), dos))
                if sink:
                    dsk = -p_s * delta
                    for g in range(GROUP):
                        tot = jnp.sum(dsk[:, g * tq:(g + 1) * tq], axis=1, keepdims=True)
                        dsink_row = dsink_row + jnp.where(lane == GROUP * h + g, tot, 0.0)
                for g in range(GROUP):
                    dqs[GROUP * h + g] = dq[g * tq:(g + 1) * tq] * QSCALE
            dq_ref[0] = jnp.concatenate(dqs, axis=-1)
            dk = jnp.concatenate(dks, axis=-1)
            dv = jnp.concatenate(dvs, axis=-1)
            if loc and not window:
                dk_ref[0] += dk
                dv_ref[0] += dv
            else:
                dk_ref[0, 0:N, :] += dk[0:N]
                dv_ref[0, 0:N, :] += dv[0:N]
                if loc:
                    W = tq + 2 * WINDOW
                    dk_ref[0, pl.ds(N + lo, W), :] += dk[N:]
                    dv_ref[0, pl.ds(N + lo, W), :] += dv[N:]
            if sink:
                dsink_ref[0:1, :] += dsink_row

        pl.when(t >= nC)(functools.partial(run, True))
        if ctx_attend:
            pl.when(t < nC)(functools.partial(run, False))
        else:
            @pl.when(t < nC)
            def _():
                dq_ref[0] = jnp.zeros((tq, QW), F32)

    kv = pl.BlockSpec((1, T, KVW), lambda b, t: (b, 0, 0))
    qrow = pl.BlockSpec((1, tq, QW), lambda b, t: (b, t, 0))
    return _pcall(exch)(
        body, name=name, grid=(B, cfg.nT),
        in_specs=[qrow, kv, pl.BlockSpec((1, T, KVW), lambda b, t: (b, 0, vblock)),
                  pl.BlockSpec((1, 8, LANES), lambda b, t: (l, 0, 0)), qrow, pl.BlockSpec((1, 8, tq), lambda b, t: (b, 0, t))],
        out_specs=[qrow, kv, kv, _full((8, LANES))],
        out_shape=[jax.ShapeDtypeStruct((B, T, QW), F32), jax.ShapeDtypeStruct((B, T, KVW), F32),
                   jax.ShapeDtypeStruct((B, T, KVW), F32), jax.ShapeDtypeStruct((8, LANES), F32)],
        compiler_params=_params(2),
    )(q, k, z, sink8, do, lse)


def _qknorm_bwd(cfg, l, z, gvec, cosf, sins, dqa, dka, dva, dqc, dkc, dvc):
    B, T, tm = cfg.B, cfg.T, cfg.tm

    def body(z_ref, g_ref, cos_ref, sin_ref, dqa_ref, dka_ref, dva_ref, dqc_ref, dkc_ref, dvc_ref, dz_ref, dg_ref):
        _acc_init([dg_ref])
        ind = _head_indicator()
        cos, sin = cos_ref[...], sin_ref[...]
        dqa_t, dqc_t = dqa_ref[0], dqc_ref[0]
        douts = {0: dqa_t[:, 0:128], 1: dqa_t[:, 128:256], 2: dqa_t[:, 256:384], 3: dka_ref[0],
                 5: dqc_t[:, 0:128], 6: dqc_t[:, 128:256], 7: dqc_t[:, 256:384], 8: dkc_ref[0]}
        pieces = []
        dgs = []
        for c in range(QKV_W // LANES):
            if c not in douts:
                pieces.append(dva_ref[0] if c == 4 else dvc_ref[0])
                dgs.append(jnp.zeros((1, LANES), F32))
                continue
            x = z_ref[0, :, c * LANES:(c + 1) * LANES]
            g = g_ref[0, :, c * LANES:(c + 1) * LANES]
            ss = _head_sum(x * x, ind)
            rstd = lax.rsqrt(ss * (1.0 / HEAD) + EPS)
            n = x * rstd
            dout = douts[c]
            dy = dout * cos + _pair_swap(dout * sin)
            dgs.append(jnp.sum(dy * n, axis=0, keepdims=True))
            dn = dy * g
            mean = _head_sum(dn * n, ind) * (1.0 / HEAD)
            pieces.append(rstd * (dn - n * mean))
        dz_ref[0] = jnp.concatenate(pieces, axis=-1).astype(BF16)
        dg_ref[0:1, :] += jnp.concatenate(dgs, axis=-1)

    row = lambda w: pl.BlockSpec((1, tm, w), lambda b, t: (b, t, 0))
    tab = pl.BlockSpec((tm, LANES), lambda b, t: (t, 0))
    return pl.pallas_call(
        body, name="qknorm_bwd", grid=(B, cfg.nT),
        in_specs=[row(QKV_W), _layer(l, QKV_W), tab, tab, row(QW), row(KVW), row(KVW), row(QW), row(KVW), row(KVW)],
        out_specs=[row(QKV_W), _full((8, QKV_W))],
        out_shape=[jax.ShapeDtypeStruct((B, T, QKV_W), BF16), jax.ShapeDtypeStruct((8, QKV_W), F32)],
        compiler_params=_params(2),
    )(z, gvec, cosf, sins, dqa, dka, dva, dqc, dkc, dvc)


def _pool_bwd(cfg, l, dob, pooled, wp, ps):
    B, T, tm, kw = cfg.B, cfg.T, cfg.tm, cfg.kw

    def body(dob_ref, pooled_ref, wp_ref, ps_ref, du_ref, dwp_ref, dps_ref):
        t = pl.program_id(1)
        _acc_init([dwp_ref, dps_ref])
        r0, seg_lo, seg_hi, k0 = _pool_geometry(cfg, t)
        ps = ps_ref[0]
        wp = wp_ref[0]
        dmix = dob_ref[0, pl.ds(r0, tm), :].astype(F32)
        pooled = pooled_ref[0]
        dps_ref[0:1, :] += jnp.sum(dmix * _dot(pooled, wp), axis=0, keepdims=True)
        dpm = (dmix * ps).astype(BF16)
        dwp_ref[...] += _dot_tn(pooled, dpm)
        dpooled_t = _dot_nt(dpm, wp)
        dpm_w = (dob_ref[0, pl.ds(k0, kw), :].astype(F32) * ps).astype(BF16)
        dpooled_w = _dot_nt(dpm_w, wp)
        rr = r0 + lax.broadcasted_iota(jnp.int32, (tm, kw), 0)
        cc = k0 + lax.broadcasted_iota(jnp.int32, (tm, kw), 1)
        diff = rr - cc
        inseg = (cc >= seg_lo) & (cc < seg_hi)
        ccol = k0 + lax.broadcasted_iota(jnp.int32, (kw, 1), 0)
        group = lax.broadcasted_iota(jnp.int32, (tm, POOL_W), 1) // HEAD
        acc = jnp.zeros((tm, POOL_W), F32)
        for gi, w in enumerate(POOL_WINDOWS):
            h = w // 2
            band_t = jnp.where((diff >= -h) & (diff <= h - 1) & inseg, 1.0, 0.0).astype(BF16)
            hi, lo = _split_bf16(dpooled_w / _pool_count(ccol, h, seg_lo, seg_hi))
            acc = jnp.where(group == gi, _dot(band_t, hi) + _dot(band_t, lo), acc)
        du_ref[0] = (acc - dpooled_t).astype(BF16)

    row = pl.BlockSpec((1, tm, POOL_W), lambda b, t: (b, t, 0))
    return pl.pallas_call(
        body, name="pool_bwd", grid=(B, cfg.nT),
        in_specs=[pl.BlockSpec((1, T, POOL_W), lambda b, t: (b, 0, 0)), row,
                  pl.BlockSpec((1, POOL_W, POOL_W), lambda b, t: (l, 0, 0)), _layer(l, POOL_W)],
        out_specs=[row, _full((POOL_W, POOL_W)), _full((8, POOL_W))],
        out_shape=[jax.ShapeDtypeStruct((B, T, POOL_W), BF16), jax.ShapeDtypeStruct((POOL_W, POOL_W), F32),
                   jax.ShapeDtypeStruct((8, POOL_W), F32)],
        compiler_params=_params(2),
    )(dob, pooled, wp, ps)


def _in_proj_bwd(cfg, l, dzq, du, dzg, w_in, x, dx1, gn, mod, *, latent_only, exch=None):
    B, S, T, D, IN, tm, nC = cfg.B, cfg.S, cfg.T, cfg.D, cfg.IN, cfg.tm, cfg.nC

    def body(dzq_ref, du_ref, dzg_ref, w_ref, x_ref, dx1_ref, gn_ref, mod_ref, dx0_ref, dz_ref, dmod_ref, dgn_ref):
        b, t = pl.program_id(0), pl.program_id(1)
        _acc_init([dmod_ref, dgn_ref])
        row = jnp.where(t < nC, B, b)
        dz = jnp.concatenate([dzq_ref[0], du_ref[0], dzg_ref[0]], axis=-1)
        dz_ref[0] = dz
        dh = _dot_nt(dz, w_ref[...])
        gn = gn_ref[0]
        scale = _mod_row(mod_ref, row, 1, D)
        _, xhat, rstd = _modulate(x_ref[0], gn, _mod_row(mod_ref, row, 0, D), scale)
        dxn, d_shift, d_scale, d_gn = _modulate_bwd(dh, xhat, rstd, gn, scale)
        dx0_ref[0] = dx1_ref[0] + dxn
        dmod_ref[pl.ds(row, 1), :] += jnp.concatenate([d_shift, d_scale], axis=-1)
        dgn_ref[0:1, :] += d_gn

    row = lambda w: pl.BlockSpec((1, tm, w), lambda b, t: (b, t, 0))
    if latent_only:
        dx0_spec = pl.BlockSpec((1, tm, D), lambda b, t: (b, jnp.maximum(t - nC, 0), 0))
        dx0_shape = jax.ShapeDtypeStruct((B, S, D), F32)
    else:
        dx0_spec, dx0_shape = row(D), jax.ShapeDtypeStruct((B, T, D), F32)
    return _pcall(exch)(
        body, name="in_proj_bwd", grid=(B, cfg.nT),
        in_specs=[row(QKV_W), row(POOL_W), row(3 * D), _full((D, IN)), row(D), row(D), _layer(l, D), _full((8, 6 * D))],
        out_specs=[dx0_spec, row(IN), _full((8, 2 * D)), _full((8, D))],
        out_shape=[dx0_shape, jax.ShapeDtypeStruct((B, T, IN), BF16),
                   jax.ShapeDtypeStruct((8, 2 * D), F32), jax.ShapeDtypeStruct((8, D), F32)],
        compiler_params=_params(2),
    )(dzq, du, dzg, w_in, x, dx1, gn, mod)


def _adaln_bwd(cfg, l, cc_all, dm_all, w_ada):
    d, B = cfg.D, cfg.B
    wa = w_ada.shape[2]

    def body(c_ref, dm_ref, w_ref, dw_ref, dc_ref):
        c = c_ref[...]
        s = jax.nn.sigmoid(c)
        dmb = dm_ref[...].astype(BF16)
        dw_ref[...] = _dot_tn((c * s).astype(BF16), dmb)
        dc = _dot_nt(dmb, w_ref[0].astype(BF16)) * (s * (1.0 + c * (1.0 - s)))
        is_ctx = lax.broadcasted_iota(jnp.int32, (8 * N_DEV, 1), 0) % 8 == B
        dc_ref[...] = jnp.broadcast_to(jnp.sum(jnp.where(is_ctx, dc, 0.0), axis=0, keepdims=True), (8, d))

    return pl.pallas_call(
        body, name="adaln_bwd", grid=(1,),
        in_specs=[_full((8 * N_DEV, d)), _full((8 * N_DEV, wa)), pl.BlockSpec((1, d, wa), lambda *_: (l, 0, 0))],
        out_specs=[_full((d, wa)), _full((8, d))],
        out_shape=[jax.ShapeDtypeStruct((d, wa), F32), jax.ShapeDtypeStruct((8, d), F32)],
        compiler_params=_params(1),
    )(cc_all, dm_all, w_ada)


def _dmod_pack(cfg, dmod_in, dg1, dmod_mlp):
    d = cfg.D
    wa = 6 * d // N_DEV

    def body(din_ref, dg1_ref, dmlp_ref, o_ref, db_ref):
        dm = jnp.concatenate([din_ref[...], dg1_ref[...], dmlp_ref[...]], axis=-1)
        for j in range(N_DEV):
            o_ref[j] = dm[:, j * wa:(j + 1) * wa]
        db_ref[...] = jnp.broadcast_to(jnp.sum(dm, axis=0, keepdims=True), (8, 6 * d))

    return pl.pallas_call(
        body, name="dmod_pack", grid=(1,),
        in_specs=[_full((8, 2 * d)), _full((8, d)), _full((8, 3 * d))],
        out_specs=[_full((N_DEV, 8, wa)), _full((8, 6 * d))],
        out_shape=[jax.ShapeDtypeStruct((N_DEV, 8, wa), F32), jax.ShapeDtypeStruct((8, 6 * d), F32)],
        compiler_params=_params(1),
    )(dmod_in, dg1, dmod_mlp)


def _adam_update(g, w, m, v):
    bc1 = 1.0 - ADAM_B1 ** ADAM_STEP
    bc2 = 1.0 - ADAM_B2 ** ADAM_STEP
    m2 = ADAM_B1 * m + (1.0 - ADAM_B1) * g
    v2 = ADAM_B2 * v + (1.0 - ADAM_B2) * (g * g)
    delta = -ADAM_LR * ((m2 / bc1) / (jnp.sqrt(v2 / bc2) + ADAM_EPS) + ADAM_WD * w)
    return delta, m2, v2


def _sum_parts(p_ref):
    g = p_ref[0].astype(F32)
    for d in range(1, p_ref.shape[0]):
        g = g + p_ref[d].astype(F32)
    return g


def _adamw_sharded(parts, w, m, v, name):
    L, K, W = w.shape
    min_rows = min(c.shape[1] for chunks in parts for c in chunks)
    tk = next(c for c in (256, 128, 64, 32, 16, 8) if K % c == 0 and min_rows % c == 0)
    spans, flat = [], []
    for li, chunks in enumerate(parts):
        row = 0
        for c in chunks:
            assert c.shape[1] % tk == 0
            spans.append((li, row // tk, (row + c.shape[1]) // tk))
            flat.append(c)
            row += c.shape[1]
        assert row == K

    def body(*refs):
        p_refs = refs[:len(flat)]
        w_ref, m_ref, v_ref, g_ref, d_ref, m2_ref, v2_ref = refs[len(flat):]
        layer, i = pl.program_id(0), pl.program_id(1)

        def run(p_ref):
            g = _sum_parts(p_ref)
            delta, m2, v2 = _adam_update(g, w_ref[0], m_ref[0], v_ref[0])
            g_ref[0] = g
            d_ref[0] = delta
            m2_ref[0] = m2
            v2_ref[0] = v2

        for (li, lo, hi), p_ref in zip(spans, p_refs):
            pl.when((layer == li) & (i >= lo) & (i < hi))(functools.partial(run, p_ref))

    blk = pl.BlockSpec((1, tk, W), lambda l, i: (l, i, 0))

    def part_spec(span, arr):
        li, lo, hi = span
        return pl.BlockSpec((arr.shape[0], tk, W), lambda l, i: (0, jnp.where((l == li) & (i >= lo) & (i < hi), i - lo, 0), 0))

    return pl.pallas_call(
        body, name=name, grid=(L, K // tk),
        in_specs=[part_spec(sp, arr) for sp, arr in zip(spans, flat)] + [blk, blk, blk],
        out_specs=[blk] * 4, out_shape=[jax.ShapeDtypeStruct((L, K, W), F32)] * 4,
        compiler_params=_params(2),
    )(*flat, w, m, v)


def _adamw_packed(parts, w, m, v, name):
    rows = w.shape[0]
    tr = next(c for c in (256, 128, 64, 32, 16, 8) if rows % c == 0)

    def body(p_ref, w_ref, m_ref, v_ref, g_ref, d_ref, m2_ref, v2_ref):
        g = _sum_parts(p_ref)
        delta, m2, v2 = _adam_update(g, w_ref[...], m_ref[...], v_ref[...])
        g_ref[...] = g
        d_ref[...] = delta
        m2_ref[...] = m2
        v2_ref[...] = v2

    blk = pl.BlockSpec((tr, PACK_W), lambda i: (i, 0))
    return pl.pallas_call(
        body, name=name, grid=(rows // tr,),
        in_specs=[pl.BlockSpec((N_DEV, tr, PACK_W), lambda i: (0, i, 0)), blk, blk, blk],
        out_specs=[blk] * 4, out_shape=[jax.ShapeDtypeStruct((rows, PACK_W), F32)] * 4,
        compiler_params=_params(1),
    )(parts, w, m, v)


_SHARDED = dict(w_ada=True, w_in=True, w_br_a=True, w_br_b=True, w_br_c=True, w_out=False, w_mlp1=True, w_mlp2=False)
_MERGE_WEIGHTS = ("w_br_a", "w_br_b", "w_br_c", "w_out")
_GATHERED = ("w_in",) + _MERGE_WEIGHTS + ("w_mlp1", "w_mlp2")
_KEEP_SHARDS = ("w_mlp1",)
_SMALL = ("c_ctx", "b_ada", "norm1", "norm2", "q_norm_a", "k_norm_a", "q_norm_c", "k_norm_c", "sink_c", "w_pool", "pool_scale")


def _from_shards(name, g):
    n, k, w = g.shape
    if name in _KEEP_SHARDS:
        return g
    if _SHARDED[name]:
        return g.transpose(1, 0, 2).reshape(k, n * w)
    return g.reshape(n * k, w)


def _to_shards(name, g):
    if g.ndim == 3:
        return g
    if _SHARDED[name]:
        k, nw = g.shape
        return g.reshape(k, N_DEV, nw // N_DEV).transpose(1, 0, 2)
    nk, w = g.shape
    return g.reshape(N_DEV, nk // N_DEV, w)


def _pack_small(vals):
    flat = jnp.concatenate([vals[n].reshape(-1) for n in _SMALL])
    rows = -(-flat.shape[0] // (8 * PACK_W)) * 8
    return jnp.pad(flat, (0, rows * PACK_W - flat.shape[0])).reshape(rows, PACK_W)


def _unpack_small(packed, like):
    flat, out, r = packed.reshape(-1), {}, 0
    for n in _SMALL:
        sz = like[n].size
        out[n] = flat[r:r + sz].reshape(like[n].shape)
        r += sz
    return out


def _rope_tables(cfg):
    pos = jnp.arange(cfg.S, dtype=F32)
    r = jnp.floor(pos / GRID_W)
    col = pos - r * GRID_W
    inv = 1.0 / (ROPE_THETA ** (jnp.arange(0, HEAD // 2, 2, dtype=F32) / (HEAD // 2)))
    ang = jnp.concatenate([r[:, None] * inv, col[:, None] * inv], axis=-1)
    cos = jnp.repeat(jnp.cos(ang), 2, axis=-1)
    sin = jnp.repeat(jnp.sin(ang), 2, axis=-1) * jnp.tile(jnp.array([-1.0, 1.0], F32), HEAD // 2)
    cos = jnp.concatenate([jnp.ones((cfg.N, HEAD), F32), cos], axis=0)
    sin = jnp.concatenate([jnp.zeros((cfg.N, HEAD), F32), sin], axis=0)
    return jnp.tile(cos, (1, 2)), jnp.tile(sin, (1, 2))


def _gvec(qa, ka, qc, kc):
    one = jnp.ones((qa.shape[0], KVW), F32)
    t = lambda a, n: jnp.tile(a, (1, n))
    return jnp.concatenate([t(qa, N_QHEADS), t(ka, N_KV), one, t(qc, N_QHEADS), t(kc, N_KV), one], axis=-1)[:, None, :]


def _block_diag(wp):
    L, g, c, _ = wp.shape
    eye = jnp.eye(g, dtype=wp.dtype)
    return (wp[:, :, :, None, :] * eye[None, :, None, :, None]).reshape(L, g * c, g * c)


def _pad8(a):
    return jnp.pad(a, ((0, 8 - a.shape[0]), (0, 0)))


def kernel(x, c, ctx, c_ctx, w_ada, b_ada, norm1, norm2, w_in, q_norm_a, k_norm_a, q_norm_c, k_norm_c, sink_c, w_pool, pool_scale, w_br_a, w_br_b, w_br_c, w_out, w_mlp1, w_mlp2, loss_target, m_c_ctx, m_w_ada, m_b_ada, m_norm1, m_norm2, m_w_in, m_q_norm_a, m_k_norm_a, m_q_norm_c, m_k_norm_c, m_sink_c, m_w_pool, m_pool_scale, m_w_br_a, m_w_br_b, m_w_br_c, m_w_out, m_w_mlp1, m_w_mlp2, v_c_ctx, v_w_ada, v_b_ada, v_norm1, v_norm2, v_w_in, v_q_norm_a, v_k_norm_a, v_q_norm_c, v_k_norm_c, v_sink_c, v_w_pool, v_pool_scale, v_w_br_a, v_w_br_b, v_w_br_c, v_w_out, v_w_mlp1, v_w_mlp2):
    B, S, D = x.shape
    N = ctx.shape[1]
    L = w_ada.shape[0]
    cfg = _Cfg(B, S, N, D)
    T = cfg.T
    weights = dict(c_ctx=c_ctx, w_ada=w_ada, b_ada=b_ada, norm1=norm1, norm2=norm2, w_in=w_in, q_norm_a=q_norm_a,
                   k_norm_a=k_norm_a, q_norm_c=q_norm_c, k_norm_c=k_norm_c, sink_c=sink_c, w_pool=w_pool,
                   pool_scale=pool_scale, w_br_a=w_br_a, w_br_b=w_br_b, w_br_c=w_br_c, w_out=w_out, w_mlp1=w_mlp1, w_mlp2=w_mlp2)
    mom_m = dict(c_ctx=m_c_ctx, w_ada=m_w_ada, b_ada=m_b_ada, norm1=m_norm1, norm2=m_norm2, w_in=m_w_in, q_norm_a=m_q_norm_a,
                 k_norm_a=m_k_norm_a, q_norm_c=m_q_norm_c, k_norm_c=m_k_norm_c, sink_c=m_sink_c, w_pool=m_w_pool,
                 pool_scale=m_pool_scale, w_br_a=m_w_br_a, w_br_b=m_w_br_b, w_br_c=m_w_br_c, w_out=m_w_out, w_mlp1=m_w_mlp1, w_mlp2=m_w_mlp2)
    mom_v = dict(c_ctx=v_c_ctx, w_ada=v_w_ada, b_ada=v_b_ada, norm1=v_norm1, norm2=v_norm2, w_in=v_w_in, q_norm_a=v_q_norm_a,
                 k_norm_a=v_k_norm_a, q_norm_c=v_q_norm_c, k_norm_c=v_k_norm_c, sink_c=v_sink_c, w_pool=v_w_pool,
                 pool_scale=v_pool_scale, w_br_a=v_w_br_a, w_br_b=v_w_br_b, w_br_c=v_w_br_c, w_out=v_w_out, w_mlp1=v_w_mlp1, w_mlp2=v_w_mlp2)

    shards_bf16 = {n: weights[n].astype(BF16) for n in _GATHERED}
    full = [dict() for _ in range(L)]

    def gather_of(items):
        return _Exchange([(shards_bf16[n], l) for l, n in items], scatter=False)

    def gathered(items, arrs):
        for (l, n), a in zip(items, arrs):
            full[l][n] = _from_shards(n, a)

    gathered([(0, "w_in")], [_gather_two_level(shards_bf16["w_in"], 0, "gather_first_weights")])

    def hosting(fn, *a, exch=None, done=None, **kw):
        if exch is None:
            return fn(*a, **kw)
        res = fn(*a, exch=exch, **kw)
        done(res[-exch.n:])
        own = res[:-exch.n]
        return own[0] if len(own) == 1 else own

    def gather_behind(l, names):
        if l >= L:
            return {}
        items = [(l, n) for n in names]
        return dict(exch=gather_of(items), done=functools.partial(gathered, items))

    cosf, sins = _rope_tables(cfg)
    xs = jnp.concatenate([ctx, x], axis=1)
    cc8 = _pad8(jnp.concatenate([c, c_ctx[None, :]], axis=0))
    va_blk, vc_blk = (QW + KVW) // KVW, (2 * QW + 3 * KVW) // KVW
    per_layer = lambda a: a[:, None, :]
    b_ada3, norm1_3, norm2_3, ps3 = per_layer(b_ada), per_layer(norm1), per_layer(norm2), per_layer(pool_scale)
    gvec = _gvec(q_norm_a, k_norm_a, q_norm_c, k_norm_c)
    sink8 = jnp.pad(sink_c[:, None, :], ((0, 0), (0, 7), (0, LANES - N_QHEADS)))
    wp = _block_diag(w_pool).astype(BF16)

    cc_all = _Exchange([cc8], scatter=False).alone("gather_cond")[0].reshape(8 * N_DEV, D)
    mod_cols = _Exchange([_adaln_fwd(cfg, cc_all, w_ada)], scatter=True).alone("scatter_mod")[0]
    mod_all = _adaln_join(cfg, mod_cols, b_ada3)

    saved = []
    for l in range(L):
        fw = full[l]
        ctx_active = l < L - 1
        mod = mod_all[l]
        z, h = hosting(_in_proj_fwd, cfg, l, xs, norm1_3, mod, fw["w_in"], **gather_behind(l, _MERGE_WEIGHTS if l == 0 else ("w_mlp2",)))
        qa, ka, qc, kc = _qknorm_fwd(cfg, l, z, gvec, cosf, sins)
        oa, lse_a = hosting(_attn_fwd, cfg, l, qa, ka, z, va_blk, sink8, window=False, sink=False, ctx_attend=ctx_active, name="attn_a_fwd",
                            **gather_behind(l, ("w_mlp1",)))
        oc, lse_c = hosting(_attn_fwd, cfg, l, qc, kc, z, vc_blk, sink8, window=True, sink=True, ctx_attend=ctx_active, name="attn_c_fwd",
                            **(gather_behind(l + 1, _MERGE_WEIGHTS)))
        ob, pooled = _pool_fwd(cfg, l, z, wp, ps3)
        x1, mgo = hosting(_merge_fwd, cfg, xs, oa, ob, oc, z, mod, fw["w_br_a"], fw["w_br_b"], fw["w_br_c"], fw["w_out"],
                          ctx_active=ctx_active, **(gather_behind(0, ("w_mlp2",)) if l == 0 else {}))
        if l < L - 1:
            x2, mo, r = hosting(_mlp_fwd, cfg, l, x1, norm2_3, mod, fw["w_mlp1"], fw["w_mlp2"], ctx_active=ctx_active,
                                **gather_behind(l + 1, ("w_in",)))
        else:
            x2, mo, r, sse = _mlp_fwd(cfg, l, x1, norm2_3, mod, fw["w_mlp1"], fw["w_mlp2"], ctx_active=ctx_active, target=loss_target)
        saved.append(dict(xs=xs, mod=mod, z=z, h=h, qa=qa, ka=ka, qc=qc, kc=kc, oa=oa, oc=oc, ob=ob, pooled=pooled, x1=x1, mgo=mgo, mo=mo,
                          lse_a=lse_a, lse_c=lse_c, r=r))
        xs = x2

    dxs = xs
    loss = lax.psum(0.5 * sse[0, 0] / D, ("x", "y", "c"))

    grads = [dict() for _ in range(L)]
    parts = {}
    small = {n: [None] * L for n in _SMALL if n != "c_ctx"}
    d_c_ctx = jnp.zeros((D,), F32)
    flat2 = lambda a: a.reshape(B * T, a.shape[-1])

    def scatter_of(l, names):
        return _Exchange([_to_shards(n, grads[l][n]) for n in names], scatter=True)

    def scattered(l, names, arrs):
        for n, a in zip(names, arrs):
            parts[(l, n)] = a

    def scatter_behind(l, names):
        if l >= L:
            return {}
        return dict(exch=scatter_of(l, names), done=functools.partial(scattered, l, names))

    for l in reversed(range(L)):
        fw, sv, g = full[l], saved[l], grads[l]
        ctx_active = l < L - 1
        mod = sv["mod"]
        dx1, h2, da, dout, dmod_mlp, dgn2 = hosting(_mlp_bwd, cfg, l, sv["x1"], dxs, sv["mo"], sv["r"], norm2_3, mod, fw["w_mlp1"], fw["w_mlp2"],
                                                    ctx_active=ctx_active, **scatter_behind(l + 1, ("w_in",)))
        g["w_mlp1"] = _matmul_tn(flat2(h2), flat2(da), "dw_mlp1", by_shard=True)
        g["w_mlp2"] = _matmul_tn(flat2(sv["r"]), flat2(dout), "dw_mlp2", by_shard=False)
        doa, dob, doc, dpa, dpb, dpc, y, dmo, dzg, dg1 = _merge_bwd(
            cfg, dx1, sv["mgo"], sv["oa"], sv["ob"], sv["oc"], sv["z"], mod, fw["w_br_a"], fw["w_br_b"], fw["w_br_c"], fw["w_out"],
            ctx_active=ctx_active)
        g["w_out"] = _matmul_tn(flat2(y), flat2(dmo), "dw_out", by_shard=False)
        g["w_br_a"] = _matmul_tn(flat2(sv["oa"]), flat2(dpa), "dw_br_a", by_shard=True)
        g["w_br_b"] = _matmul_tn(flat2(sv["ob"]), flat2(dpb), "dw_br_b", by_shard=True)
        g["w_br_c"] = _matmul_tn(flat2(sv["oc"]), flat2(dpc), "dw_br_c", by_shard=True)
        z = sv["z"]
        dqa, dka, dva, _ = hosting(_attn_bwd, cfg, l, sv["qa"], sv["ka"], z, va_blk, sink8, doa, sv["lse_a"], window=False, sink=False,
                                   ctx_attend=ctx_active, name="attn_a_bwd", **scatter_behind(l, ("w_mlp1", "w_mlp2")))
        dqc, dkc, dvc, dsink = hosting(_attn_bwd, cfg, l, sv["qc"], sv["kc"], z, vc_blk, sink8, doc, sv["lse_c"], window=True, sink=True,
                                       ctx_attend=ctx_active, name="attn_c_bwd", **scatter_behind(l, _MERGE_WEIGHTS))
        dzq, dgvec = _qknorm_bwd(cfg, l, z, gvec, cosf, sins, dqa, dka, dva, dqc, dkc, dvc)
        du, dwp, dps = _pool_bwd(cfg, l, dob, sv["pooled"], wp, ps3)
        dxs, dz, dmod_in, dgn1 = _in_proj_bwd(cfg, l, dzq, du, dzg, fw["w_in"], sv["xs"], dx1, norm1_3, mod, latent_only=(l == 0))
        dmod_cols, dbias = _dmod_pack(cfg, dmod_in, dg1, dmod_mlp)
        dmod_exchange = _Exchange([dmod_cols], scatter=True)
        half = D // 2
        if l > 0:
            g["w_in"], dm_all = _matmul_tn(flat2(sv["h"]), flat2(dz), "dw_in", by_shard=True, exch=dmod_exchange)
        else:
            g_lo, dm_all = _matmul_tn(flat2(sv["h"]), flat2(dz), "dw_in_lo", by_shard=True, a_cols=(0, half), exch=dmod_exchange)
        g["w_ada"], dcc = _adaln_bwd(cfg, l, cc_all, dm_all.reshape(8 * N_DEV, -1), w_ada)
        d_c_ctx = d_c_ctx + dcc[0]
        gv = dgvec[0]
        heads = lambda v, n: v.reshape(n, HEAD).sum(axis=0)
        small["b_ada"][l] = dbias[0]
        small["norm1"][l] = dgn1[0]
        small["norm2"][l] = dgn2[0]
        small["q_norm_a"][l] = heads(gv[0:QW], N_QHEADS)
        small["k_norm_a"][l] = heads(gv[QW:QW + KVW], N_KV)
        small["q_norm_c"][l] = heads(gv[QW + 2 * KVW:2 * QW + 2 * KVW], N_QHEADS)
        small["k_norm_c"][l] = heads(gv[2 * QW + 2 * KVW:2 * QW + 3 * KVW], N_KV)
        small["sink_c"][l] = dsink[0, :N_QHEADS]
        small["w_pool"][l] = jnp.stack([dwp[i * HEAD:(i + 1) * HEAD, i * HEAD:(i + 1) * HEAD] for i in range(len(POOL_WINDOWS))])
        small["pool_scale"][l] = dps[0]
    grad_x = dxs

    small_vals = {n: jnp.stack(v) for n, v in small.items()}
    small_vals["c_ctx"] = d_c_ctx
    small_packed = _pack_small(small_vals)
    g_hi, parts_lo, small_parts = _matmul_tn(
        flat2(saved[0]["h"]), flat2(dz), "dw_in_hi", by_shard=True, a_cols=(half, half),
        exch=_Exchange([g_lo, jnp.broadcast_to(small_packed[None], (N_DEV,) + small_packed.shape)], scatter=True))
    parts_hi = _Exchange([g_hi], scatter=True).alone("scatter_last_grads")[0]
    chunks = {(l, n): [parts[(l, n)]] for l in range(L) for n in _GATHERED if (l, n) in parts}
    chunks[(0, "w_in")] = [parts_lo, parts_hi]
    for l in range(L):
        chunks[(l, "w_ada")] = [grads[l]["w_ada"][None]]
    stepped = {n: _adamw_sharded([chunks[(l, n)] for l in range(L)], weights[n], mom_m[n], mom_v[n], "adamw_" + n) for n in _SHARDED}
    stepped_small = _adamw_packed(small_parts, _pack_small(weights), _pack_small(mom_m), _pack_small(mom_v), "adamw_small")

    outs = []
    for i in range(4):
        res = {n: stepped[n][i] for n in _SHARDED}
        res.update(_unpack_small(stepped_small[i], weights))
        outs.append(res)
    order = ("c_ctx", "w_ada", "b_ada", "norm1", "norm2", "w_in", "q_norm_a", "k_norm_a", "q_norm_c", "k_norm_c", "sink_c",
             "w_pool", "pool_scale", "w_br_a", "w_br_b", "w_br_c", "w_out", "w_mlp1", "w_mlp2")
    return (loss, grad_x, *[res[n] for res in outs for n in order])
```

```python
import functools

import jax
import jax.numpy as jnp
from jax import lax
from jax.experimental import pallas as pl
from jax.experimental.pallas import tpu as pltpu

F32 = jnp.float32
BF16 = jnp.bfloat16

N_DEV = 8
HEAD = 64
N_QHEADS = 6
N_KV = 2
GROUP = 3
QW = N_QHEADS * HEAD
KVW = N_KV * HEAD
QKV_W = 2 * (QW + 2 * KVW)
POOL_W = 256
POOL_WINDOWS = (2, 4, 8, 16)
POOL_HALO = 16
GATE0 = QKV_W + POOL_W
WINDOW = 128
GRID_W = 64
ROPE_THETA = 10000.0
EPS = 1e-6
NEG = -1e30
QSCALE = HEAD ** -0.5
LANES = 128
PACK_W = 1024
VMEM_LIMIT = 56 * 1024 * 1024

ADAM_LR = 0.001
ADAM_B1 = 0.9
ADAM_B2 = 0.999
ADAM_EPS = 1e-08
ADAM_WD = 0.01
ADAM_STEP = 10

NT_DIMS = (((1,), (1,)), ((), ()))
TN_DIMS = (((0,), (0,)), ((), ()))


def _dot(a, b):
    return jnp.dot(a, b, preferred_element_type=F32)


def _dot_nt(a, b):
    return lax.dot_general(a, b, NT_DIMS, preferred_element_type=F32)


def _dot_tn(a, b):
    return lax.dot_general(a, b, TN_DIMS, preferred_element_type=F32)


def _params(n_grid):
    return pltpu.CompilerParams(dimension_semantics=("arbitrary",) * n_grid, vmem_limit_bytes=VMEM_LIMIT)


def _full(shape):
    nd = len(shape)
    return pl.BlockSpec(shape, lambda *_: (0,) * nd)


def _layer(l, width):
    return pl.BlockSpec((1, 1, width), lambda *_: (l, 0, 0))


def _modulate(x, gn, shift, scale):
    rstd = lax.rsqrt(jnp.mean(x * x, axis=-1, keepdims=True) + EPS)
    xhat = x * rstd
    return xhat * gn * (1.0 + scale) + shift, xhat, rstd


def _modulate_bwd(dh, xhat, rstd, gn, scale):
    d_shift = jnp.sum(dh, axis=0, keepdims=True)
    d_scale = jnp.sum(dh * xhat * gn, axis=0, keepdims=True)
    dy = dh * (1.0 + scale)
    d_gn = jnp.sum(dy * xhat, axis=0, keepdims=True)
    dxh = dy * gn
    dx = rstd * (dxh - xhat * jnp.mean(dxh * xhat, axis=-1, keepdims=True))
    return dx, d_shift, d_scale, d_gn


def _mod_row(mod_ref, row, k, d):
    return mod_ref[pl.ds(row, 1), k * d:(k + 1) * d]


class _Cfg:
    def __init__(self, b, s, n, d):
        self.B, self.S, self.N, self.D = b, s, n, d
        self.T = n + s
        self.F = 4 * d
        self.IN = GATE0 + 3 * d
        self.tm = 256 if (n % 256 == 0 and s % 256 == 0) else 128
        self.nT = self.T // self.tm
        self.nC = n // self.tm
        self.gw = 512 if d % 512 == 0 else 256
        self.kw = self.tm + 2 * POOL_HALO
        assert GATE0 % self.gw == 0 and d % self.gw == 0 and b < 8 and self.T >= self.kw and max(POOL_WINDOWS) // 2 <= POOL_HALO
        assert s % GRID_W == 0 and n % self.tm == 0 and s % self.tm == 0 and s >= self.tm + 2 * WINDOW
        assert d % (N_DEV * LANES) == 0


def _peer(k):
    x, y, c = lax.axis_index("x"), lax.axis_index("y"), lax.axis_index("c")
    px = x ^ ((k >> 2) & 1)
    py = y ^ ((k >> 1) & 1)
    pc = c ^ (k & 1)
    return (px, py, pc), 4 * px + 2 * py + pc


class _Exchange:
    def __init__(self, arrays, scatter):
        self.arrays = [a if isinstance(a, tuple) else (a, None) for a in arrays]
        self.scatter = scatter
        self.n = len(self.arrays)

    def operands(self):
        return [a for a, _ in self.arrays]

    def out_shapes(self):
        res = []
        for a, layer in self.arrays:
            shape = a.shape[1:] if (self.scatter or layer is not None) else a.shape
            res.append(jax.ShapeDtypeStruct((N_DEV,) + tuple(shape), a.dtype))
        return res

    def scratch(self):
        n = self.n * (N_DEV - 1)
        return [pltpu.SemaphoreType.DMA((n,)), pltpu.SemaphoreType.DMA((n,)), pltpu.SemaphoreType.DMA((self.n,))]

    def _copies(self, x_refs, out_refs, send_sems, recv_sems, local_sems, want):
        _, me = _peer(0)
        res = []
        for i, ((_, layer), x_ref, out_ref) in enumerate(zip(self.arrays, x_refs, out_refs)):
            if self.scatter:
                src_of = lambda d, x_ref=x_ref: x_ref.at[d]
            elif layer is not None:
                src_of = lambda d, x_ref=x_ref, layer=layer: x_ref.at[layer]
            else:
                src_of = lambda d, x_ref=x_ref: x_ref
            if want == "local":
                res.append(pltpu.make_async_copy(src_of(me), out_ref.at[me], local_sems.at[i]))
                continue
            for k in range(1, N_DEV):
                pos, idx = _peer(k)
                j = i * (N_DEV - 1) + k - 1
                common = dict(send_sem=send_sems.at[j], recv_sem=recv_sems.at[j], device_id=pos, device_id_type=pl.DeviceIdType.MESH)
                if want == "send":
                    res.append(pltpu.make_async_remote_copy(src_ref=src_of(idx), dst_ref=out_ref.at[me], **common))
                else:
                    res.append(pltpu.make_async_remote_copy(src_ref=src_of(me), dst_ref=out_ref.at[idx], **common))
        return res

    def start(self, *refs):
        for cp in self._copies(*refs, "local") + self._copies(*refs, "send"):
            cp.start()

    def wait(self, *refs):
        for cp in self._copies(*refs, "recv"):
            cp.wait_recv()
        for cp in self._copies(*refs, "send"):
            cp.wait_send()
        for cp in self._copies(*refs, "local"):
            cp.wait()

    def alone(self, name):
        n = self.n

        def body(*refs):
            args = (refs[:n], refs[n:2 * n], *refs[2 * n:])
            self.start(*args)
            self.wait(*args)

        any_spec = pl.BlockSpec(memory_space=pl.ANY)
        return pl.pallas_call(body, name=name, in_specs=[any_spec] * n, out_specs=[any_spec] * n,
                              out_shape=self.out_shapes(), scratch_shapes=self.scratch())(*self.operands())


def _gather_two_level(x, layer, name):
    shape = x.shape[1:]

    def body(x_ref, out_ref, send_sems, recv_sems, local_sem):
        mx, my, mc = lax.axis_index("x"), lax.axis_index("y"), lax.axis_index("c")
        me, sibling = (mx, my, mc), (mx, my, 1 - mc)
        chips = [(1 - mx, my), (mx, 1 - my), (1 - mx, 1 - my)]
        src = x_ref.at[layer]

        def slot(px, py, pc):
            return out_ref.at[4 * px + 2 * py + pc]

        def copy(k, block, to, from_src=False):
            return pltpu.make_async_remote_copy(src_ref=src if from_src else slot(*block), dst_ref=slot(*block), send_sem=send_sems.at[k],
                                                recv_sem=recv_sems.at[k], device_id=to, device_id_type=pl.DeviceIdType.MESH)

        mine = pltpu.make_async_copy(src, slot(*me), local_sem)
        mine.start()
        first = [copy(0, me, sibling, True)] + [copy(1 + j, me, (*chip, mc), True) for j, chip in enumerate(chips)]
        for cp in first:
            cp.start()
        passed = [copy(4 + j, (*chip, mc), sibling) for j, chip in enumerate(chips)]
        for j, chip in enumerate(chips):
            copy(1 + j, (*chip, mc), me).wait_recv()
            passed[j].start()
        copy(0, sibling, me).wait_recv()
        for j, chip in enumerate(chips):
            copy(4 + j, (*chip, 1 - mc), me).wait_recv()
        for cp in first + passed:
            cp.wait_send()
        mine.wait()

    any_spec = pl.BlockSpec(memory_space=pl.ANY)
    return pl.pallas_call(
        body, name=name, in_specs=[any_spec], out_specs=any_spec,
        out_shape=jax.ShapeDtypeStruct((N_DEV,) + tuple(shape), x.dtype),
        scratch_shapes=[pltpu.SemaphoreType.DMA((N_DEV - 1,)), pltpu.SemaphoreType.DMA((N_DEV - 1,)), pltpu.SemaphoreType.DMA],
    )(x)


def _pcall(exch):
    if exch is None:
        return pl.pallas_call

    def make(body, *, name, grid, in_specs, out_specs, out_shape, compiler_params, scratch_shapes=()):
        multi = isinstance(out_shape, (list, tuple))
        out_specs_l = list(out_specs) if multi else [out_specs]
        out_shape_l = list(out_shape) if multi else [out_shape]
        n_in, n_out, n_x, n_s = len(in_specs), len(out_specs_l), exch.n, len(scratch_shapes)

        def hosted(*refs):
            ins, x_refs = refs[:n_in], refs[n_in:n_in + n_x]
            o0 = n_in + n_x
            outs, xo_refs = refs[o0:o0 + n_out], refs[o0 + n_out:o0 + n_out + n_x]
            s0 = o0 + n_out + n_x
            own_scratch, sems = refs[s0:s0 + n_s], refs[s0 + n_s:]
            ids = [pl.program_id(i) for i in range(len(grid))]
            first = functools.reduce(jnp.logical_and, [i == 0 for i in ids])
            last = functools.reduce(jnp.logical_and, [i == g - 1 for i, g in zip(ids, grid)])

            @pl.when(first)
            def _():
                exch.start(x_refs, xo_refs, *sems)

            body(*ins, *outs, *own_scratch)

            @pl.when(last)
            def _():
                exch.wait(x_refs, xo_refs, *sems)

        any_spec = pl.BlockSpec(memory_space=pl.ANY)
        call = pl.pallas_call(
            hosted, name=name, grid=grid, in_specs=list(in_specs) + [any_spec] * n_x, out_specs=out_specs_l + [any_spec] * n_x,
            out_shape=out_shape_l + exch.out_shapes(), scratch_shapes=list(scratch_shapes) + exch.scratch(),
            compiler_params=compiler_params)
        return lambda *args: call(*args, *exch.operands())

    return make


def _adaln_fwd(cfg, cc_all, w_ada):
    d = cfg.D
    L, _, wa = w_ada.shape

    def body(c_ref, w_ref, o_ref):
        c = c_ref[...]
        a = (c * jax.nn.sigmoid(c)).astype(BF16)
        for l in range(L):
            m = _dot(a, w_ref[l].astype(BF16))
            for p in range(N_DEV):
                o_ref[p, l] = m[8 * p:8 * (p + 1)]

    return pl.pallas_call(
        body, name="adaln_fwd", grid=(1,),
        in_specs=[_full((8 * N_DEV, d)), _full((L, d, wa))],
        out_specs=_full((N_DEV, L, 8, wa)),
        out_shape=jax.ShapeDtypeStruct((N_DEV, L, 8, wa), F32), compiler_params=_params(1),
    )(cc_all, w_ada)


def _adaln_join(cfg, parts, b_ada):
    d = cfg.D
    _, L, _, wa = parts.shape

    def body(p_ref, b_ref, o_ref):
        for l in range(L):
            for j in range(N_DEV):
                o_ref[l, :, j * wa:(j + 1) * wa] = p_ref[j, l] + b_ref[l, :, j * wa:(j + 1) * wa]

    return pl.pallas_call(
        body, name="adaln_join", grid=(1,),
        in_specs=[_full((N_DEV, L, 8, wa)), _full((L, 1, 6 * d))],
        out_specs=_full((L, 8, 6 * d)),
        out_shape=jax.ShapeDtypeStruct((L, 8, 6 * d), F32), compiler_params=_params(1),
    )(parts, b_ada)


def _in_proj_fwd(cfg, l, x, gn, mod, w_in, exch=None):
    B, T, D, IN, tm, nC = cfg.B, cfg.T, cfg.D, cfg.IN, cfg.tm, cfg.nC

    def body(x_ref, gn_ref, mod_ref, w_ref, z_ref, u_ref, h_ref):
        b, t = pl.program_id(0), pl.program_id(1)
        row = jnp.where(t < nC, B, b)
        h, _, _ = _modulate(x_ref[0], gn_ref[0], _mod_row(mod_ref, row, 0, D), _mod_row(mod_ref, row, 1, D))
        hb = h.astype(BF16)
        h_ref[0] = hb
        z = _dot(hb, w_ref[...])
        z_ref[0] = z.astype(BF16)
        u_ref[0] = z[:, QKV_W:QKV_W + POOL_W]

    row = lambda w: pl.BlockSpec((1, tm, w), lambda b, t: (b, t, 0))
    return _pcall(exch)(
        body, name="in_proj_fwd", grid=(B, cfg.nT),
        in_specs=[row(D), _layer(l, D), _full((8, 6 * D)), _full((D, IN))],
        out_specs=[row(IN), row(POOL_W), row(D)],
        out_shape=[jax.ShapeDtypeStruct((B, T, IN), BF16), jax.ShapeDtypeStruct((B, T, POOL_W), F32), jax.ShapeDtypeStruct((B, T, D), BF16)],
        compiler_params=_params(2),
    )(x, gn, mod, w_in)


def _head_indicator():
    r = lax.broadcasted_iota(jnp.int32, (LANES, LANES), 0) // HEAD
    c = lax.broadcasted_iota(jnp.int32, (LANES, LANES), 1) // HEAD
    return jnp.where(r == c, 1.0, 0.0).astype(BF16)


def _head_sum(x, ind):
    hi = x.astype(BF16)
    lo = (x - hi.astype(F32)).astype(BF16)
    return _dot(hi, ind) + _dot(lo, ind)


def _pair_swap(y):
    lane = lax.broadcasted_iota(jnp.int32, y.shape, 1)
    return jnp.where(lane % 2 == 0, pltpu.roll(y, LANES - 1, 1), pltpu.roll(y, 1, 1))


_Q_CHUNKS = (0, 1, 2, 5, 6, 7)


def _qknorm_fwd(cfg, l, z, gvec, cosf, sins):
    B, T, tm = cfg.B, cfg.T, cfg.tm

    def body(z_ref, g_ref, cos_ref, sin_ref, qa_ref, ka_ref, qc_ref, kc_ref):
        ind = _head_indicator()
        cos, sin = cos_ref[...], sin_ref[...]

        def chunk(c):
            x = z_ref[0, :, c * LANES:(c + 1) * LANES].astype(F32)
            ss = _head_sum(x * x, ind)
            y = x * lax.rsqrt(ss * (1.0 / HEAD) + EPS) * g_ref[0, :, c * LANES:(c + 1) * LANES]
            out = y * cos + _pair_swap(y) * sin
            return (out * QSCALE if c in _Q_CHUNKS else out).astype(BF16)

        qa_ref[0] = jnp.concatenate([chunk(0), chunk(1), chunk(2)], axis=-1)
        ka_ref[0] = chunk(3)
        qc_ref[0] = jnp.concatenate([chunk(5), chunk(6), chunk(7)], axis=-1)
        kc_ref[0] = chunk(8)

    row = lambda w: pl.BlockSpec((1, tm, w), lambda b, t: (b, t, 0))
    tab = pl.BlockSpec((tm, LANES), lambda b, t: (t, 0))
    return pl.pallas_call(
        body, name="qknorm_fwd", grid=(B, cfg.nT),
        in_specs=[row(QKV_W), _layer(l, QKV_W), tab, tab],
        out_specs=[row(QW), row(KVW), row(QW), row(KVW)],
        out_shape=[jax.ShapeDtypeStruct((B, T, w), BF16) for w in (QW, KVW, QW, KVW)],
        compiler_params=_params(2),
    )(z, gvec, cosf, sins)


def _attn_scores(cfg, tl, q, k_ref, v_ref, sink_ref, h, loc, window, sink, lse=None):
    S, N, tq = cfg.S, cfg.N, cfg.tm
    hs = slice(h * HEAD, (h + 1) * HEAD)
    qs = jnp.concatenate([q[:, (GROUP * h + g) * HEAD:(GROUP * h + g + 1) * HEAD] for g in range(GROUP)], axis=0)
    lo = None
    if not loc:
        kk = k_ref[0, 0:N, :][:, hs]
        vv = v_ref[0, 0:N, :].astype(BF16)[:, hs]
    elif not window:
        kk = k_ref[0][:, hs]
        vv = v_ref[0].astype(BF16)[:, hs]
    else:
        W = tq + 2 * WINDOW
        lo = pl.multiple_of(jnp.clip(tl * tq - WINDOW, 0, S - W), LANES)
        kk = jnp.concatenate([k_ref[0, 0:N, :], k_ref[0, pl.ds(N + lo, W), :]], axis=0)[:, hs]
        vv = jnp.concatenate([v_ref[0, 0:N, :], v_ref[0, pl.ds(N + lo, W), :]], axis=0).astype(BF16)[:, hs]
    st = _dot_nt(kk, qs)
    if window:
        krow = lax.broadcasted_iota(jnp.int32, st.shape, 0)
        qpos = tl * tq + lax.broadcasted_iota(jnp.int32, st.shape, 1) % tq
        st = jnp.where((krow < N) | (jnp.abs(qpos - (lo + krow - N)) <= WINDOW), st, NEG)
    sk = None
    if sink:
        colg = lax.broadcasted_iota(jnp.int32, (1, GROUP * tq), 1) // tq
        sk = jnp.zeros((1, GROUP * tq), F32)
        for g in range(GROUP):
            j = GROUP * h + g
            sk = jnp.where(colg == g, sink_ref[0, 0:1, j:j + 1], sk)
    if lse is not None:
        return qs, kk, vv, jnp.exp(st - lse), None, (jnp.exp(sk - lse) if sink else None), lo, lse
    m = jnp.max(st, axis=0, keepdims=True)
    if sink:
        m = jnp.maximum(m, sk)
    e = jnp.exp(st - m)
    l = jnp.sum(e, axis=0, keepdims=True)
    e_s = None
    if sink:
        e_s = jnp.exp(sk - m)
        l = l + e_s
    return qs, kk, vv, e, 1.0 / l, e_s, lo, m + jnp.log(l)


def _attn_fwd(cfg, l, q, k, z, vblock, sink8, *, window, sink, ctx_attend, name, exch=None):
    B, T, tq, nC = cfg.B, cfg.T, cfg.tm, cfg.nC

    def body(q_ref, k_ref, v_ref, sink_ref, o_ref, lse_ref):
        t = pl.program_id(1)

        def run(loc):
            q_t = q_ref[0]
            outs = [None] * N_QHEADS
            lses = [None] * N_QHEADS
            for h in range(N_KV):
                _, _, vv, e, inv, _, _, lse = _attn_scores(cfg, t - nC, q_t, k_ref, v_ref, sink_ref, h, loc, window and loc, sink)
                o = (_dot_tn(vv, e.astype(BF16)) * inv).T
                for g in range(GROUP):
                    outs[GROUP * h + g] = o[g * tq:(g + 1) * tq]
                    lses[GROUP * h + g] = lse[:, g * tq:(g + 1) * tq]
            o_ref[0] = jnp.concatenate(outs, axis=-1).astype(BF16)
            lse_ref[0] = jnp.concatenate(lses + [jnp.zeros((8 - N_QHEADS, tq), F32)], axis=0)

        pl.when(t >= nC)(functools.partial(run, True))
        if ctx_attend:
            pl.when(t < nC)(functools.partial(run, False))
        else:
            @pl.when(t < nC)
            def _():
                o_ref[0] = jnp.zeros((tq, QW), BF16)
                lse_ref[0] = jnp.zeros((8, tq), F32)

    return _pcall(exch)(
        body, name=name, grid=(B, cfg.nT),
        in_specs=[pl.BlockSpec((1, tq, QW), lambda b, t: (b, t, 0)),
                  pl.BlockSpec((1, T, KVW), lambda b, t: (b, 0, 0)),
                  pl.BlockSpec((1, T, KVW), lambda b, t: (b, 0, vblock)),
                  pl.BlockSpec((1, 8, LANES), lambda b, t: (l, 0, 0))],
        out_specs=[pl.BlockSpec((1, tq, QW), lambda b, t: (b, t, 0)), pl.BlockSpec((1, 8, tq), lambda b, t: (b, 0, t))],
        out_shape=[jax.ShapeDtypeStruct((B, T, QW), BF16), jax.ShapeDtypeStruct((B, 8, T), F32)], compiler_params=_params(2),
    )(q, k, z, sink8)


def _pool_geometry(cfg, t):
    tm, N, T, nC = cfg.tm, cfg.N, cfg.T, cfg.nC
    r0 = pl.multiple_of(t * tm, tm)
    isctx = t < nC
    seg_lo = jnp.where(isctx, 0, N)
    seg_hi = jnp.where(isctx, N, T)
    k0 = pl.multiple_of(jnp.clip(t * tm - POOL_HALO, 0, T - cfg.kw), POOL_HALO)
    return r0, seg_lo, seg_hi, k0


def _pool_count(pos, h, seg_lo, seg_hi):
    return jnp.maximum(jnp.minimum(pos + h, seg_hi) - jnp.maximum(pos - h, seg_lo), 1).astype(F32)


def _split_bf16(x):
    hi = x.astype(BF16)
    return hi, (x - hi.astype(F32)).astype(BF16)


def _pool_fwd(cfg, l, u, wp, ps):
    B, T, tm, kw = cfg.B, cfg.T, cfg.tm, cfg.kw

    def body(u_ref, wp_ref, ps_ref, ob_ref, pooled_ref):
        t = pl.program_id(1)
        r0, seg_lo, seg_hi, k0 = _pool_geometry(cfg, t)
        hi, lo = _split_bf16(u_ref[0, pl.ds(k0, kw), :])
        rr = r0 + lax.broadcasted_iota(jnp.int32, (tm, kw), 0)
        cc = k0 + lax.broadcasted_iota(jnp.int32, (tm, kw), 1)
        diff = cc - rr
        inseg = (cc >= seg_lo) & (cc < seg_hi)
        rcol = r0 + lax.broadcasted_iota(jnp.int32, (tm, 1), 0)
        group = lax.broadcasted_iota(jnp.int32, (tm, POOL_W), 1) // HEAD
        acc = jnp.zeros((tm, POOL_W), F32)
        for gi, w in enumerate(POOL_WINDOWS):
            h = w // 2
            band = jnp.where((diff >= -h) & (diff <= h - 1) & inseg, 1.0, 0.0).astype(BF16)
            tot = _dot(band, hi) + _dot(band, lo)
            acc = jnp.where(group == gi, tot / _pool_count(rcol, h, seg_lo, seg_hi), acc)
        pooled = (acc - u_ref[0, pl.ds(r0, tm), :]).astype(BF16)
        pooled_ref[0] = pooled
        ob_ref[0] = (_dot(pooled, wp_ref[0]) * ps_ref[0]).astype(BF16)

    row = pl.BlockSpec((1, tm, POOL_W), lambda b, t: (b, t, 0))
    return pl.pallas_call(
        body, name="pool_fwd", grid=(B, cfg.nT),
        in_specs=[pl.BlockSpec((1, T, POOL_W), lambda b, t: (b, 0, 0)),
                  pl.BlockSpec((1, POOL_W, POOL_W), lambda b, t: (l, 0, 0)), _layer(l, POOL_W)],
        out_specs=[row, row],
        out_shape=[jax.ShapeDtypeStruct((B, T, POOL_W), BF16)] * 2, compiler_params=_params(2),
    )(u, wp, ps)


def _gate_specs(cfg):
    tm, gw = cfg.tm, cfg.gw
    first = GATE0 // gw
    return [pl.BlockSpec((1, tm, gw), functools.partial(lambda b, t, j: (b, t, j), j=first + i)) for i in range(3 * cfg.D // gw)]


def _read_gates(cfg, gate_refs):
    per = cfg.D // cfg.gw
    return [jnp.concatenate([gate_refs[k * per + i][0] for i in range(per)], axis=-1).astype(F32) for k in range(3)]


def _merge_fwd(cfg, x, oa, ob, oc, z, mod, wa, wb, wc, wo, *, ctx_active, exch=None):
    B, T, D, tm, nC = cfg.B, cfg.T, cfg.D, cfg.tm, cfg.nC
    ng = 3 * D // cfg.gw

    def body(x_ref, oa_ref, ob_ref, oc_ref, *rest):
        gate_refs = rest[:ng]
        mod_ref, wa_ref, wb_ref, wc_ref, wo_ref, x1_ref, mgo_ref = rest[ng:]
        b, t = pl.program_id(0), pl.program_id(1)

        def compute():
            row = jnp.where(t < nC, B, b)
            ga, gb, gc = _read_gates(cfg, gate_refs)
            y = (jax.nn.sigmoid(ga) * _dot(oa_ref[0], wa_ref[...])
                 + jax.nn.sigmoid(gb) * _dot(ob_ref[0], wb_ref[...])
                 + jax.nn.sigmoid(gc) * _dot(oc_ref[0], wc_ref[...]))
            mo = _dot(y.astype(BF16), wo_ref[...])
            mgo_ref[0] = mo.astype(BF16)
            x1_ref[0] = x_ref[0] + _mod_row(mod_ref, row, 2, D) * mo

        if ctx_active:
            compute()
        else:
            pl.when(t >= nC)(compute)

            @pl.when(t < nC)
            def _():
                mgo_ref[0] = jnp.zeros((tm, D), BF16)
                x1_ref[0] = x_ref[0]

    row = lambda w: pl.BlockSpec((1, tm, w), lambda b, t: (b, t, 0))
    return _pcall(exch)(
        body, name="merge_fwd", grid=(B, cfg.nT),
        in_specs=[row(D), row(QW), row(POOL_W), row(QW)] + _gate_specs(cfg)
        + [_full((8, 6 * D)), _full((QW, D)), _full((POOL_W, D)), _full((QW, D)), _full((D, D))],
        out_specs=[row(D), row(D)],
        out_shape=[jax.ShapeDtypeStruct((B, T, D), F32), jax.ShapeDtypeStruct((B, T, D), BF16)],
        compiler_params=_params(2),
    )(x, oa, ob, oc, *([z] * ng), mod, wa, wb, wc, wo)


def _w1_apply(hb, w1_ref):
    return jnp.concatenate([_dot(hb, w1_ref[d]) for d in range(N_DEV)], axis=-1)


def _mlp_fwd(cfg, l, x1, gn, mod, w1, w2, *, ctx_active, target=None, exch=None):
    B, T, D, F, tm, nC = cfg.B, cfg.T, cfg.D, cfg.F, cfg.tm, cfg.nC
    assert target is None or not ctx_active

    def body(x_ref, gn_ref, mod_ref, w1_ref, w2_ref, *rest):
        if target is None:
            x2_ref, mo_ref, r_ref = rest
        else:
            tgt_ref, x2_ref, mo_ref, r_ref, sse_ref = rest
            _acc_init([sse_ref])
        b, t = pl.program_id(0), pl.program_id(1)

        def compute():
            row = jnp.where(t < nC, B, b)
            x = x_ref[0]
            h, _, _ = _modulate(x, gn_ref[0], _mod_row(mod_ref, row, 3, D), _mod_row(mod_ref, row, 4, D))
            a = jnp.maximum(_w1_apply(h.astype(BF16), w1_ref), 0.0)
            rb = (a * a).astype(BF16)
            r_ref[0] = rb
            mo = _dot(rb, w2_ref[...])
            mo_ref[0] = mo.astype(BF16)
            x2 = x + _mod_row(mod_ref, row, 5, D) * mo
            if target is None:
                x2_ref[0] = x2
            else:
                err = x2 - tgt_ref[0]
                x2_ref[0] = err * (1.0 / D)
                sse_ref[...] += jnp.sum(err * err)

        if ctx_active:
            compute()
        else:
            pl.when(t >= nC)(compute)

            @pl.when(t < nC)
            def _():
                mo_ref[0] = jnp.zeros((tm, D), BF16)
                r_ref[0] = jnp.zeros((tm, F), BF16)
                x2_ref[0] = x_ref[0] if target is None else jnp.zeros((tm, D), F32)

    row = pl.BlockSpec((1, tm, D), lambda b, t: (b, t, 0))
    in_specs = [row, _layer(l, D), _full((8, 6 * D)), _full((N_DEV, D, F // N_DEV)), _full((F, D))]
    out_specs = [row, row, pl.BlockSpec((1, tm, F), lambda b, t: (b, t, 0))]
    out_shape = [jax.ShapeDtypeStruct((B, T, D), F32), jax.ShapeDtypeStruct((B, T, D), BF16), jax.ShapeDtypeStruct((B, T, F), BF16)]
    args = [x1, gn, mod, w1, w2]
    if target is not None:
        in_specs.append(pl.BlockSpec((1, tm, D), lambda b, t: (b, jnp.maximum(t - nC, 0), 0)))
        out_specs.append(_full((8, LANES)))
        out_shape.append(jax.ShapeDtypeStruct((8, LANES), F32))
        args.append(target)
    return _pcall(exch)(
        body, name="mlp_fwd", grid=(B, cfg.nT), in_specs=in_specs, out_specs=out_specs, out_shape=out_shape,
        compiler_params=_params(2),
    )(*args)


def _acc_init(refs):
    b, t = pl.program_id(0), pl.program_id(1)

    @pl.when((b == 0) & (t == 0))
    def _():
        for ref in refs:
            ref[...] = jnp.zeros(ref.shape, ref.dtype)


def _mlp_bwd(cfg, l, x1, dx2, mo, r, gn, mod, w1, w2, *, ctx_active, exch=None):
    B, T, D, F, tm, nC = cfg.B, cfg.T, cfg.D, cfg.F, cfg.tm, cfg.nC
    ws = F // N_DEV

    def body(x_ref, dx_ref, mo_ref, r_ref, gn_ref, mod_ref, w1_ref, w2_ref, dx1_ref, h_ref, da_ref, dout_ref, dmod_ref, dgn_ref):
        b, t = pl.program_id(0), pl.program_id(1)
        _acc_init([dmod_ref, dgn_ref])

        def compute():
            row = jnp.where(t < nC, B, b)
            gn = gn_ref[0]
            scale = _mod_row(mod_ref, row, 4, D)
            h, xhat, rstd = _modulate(x_ref[0], gn, _mod_row(mod_ref, row, 3, D), scale)
            hb = h.astype(BF16)
            dx = dx_ref[0]
            dout = (dx * _mod_row(mod_ref, row, 5, D)).astype(BF16)
            da = (_dot_nt(dout, w2_ref[...]) * (2.0 * jnp.sqrt(r_ref[0].astype(F32)))).astype(BF16)
            dh = _dot_nt(da[:, 0:ws], w1_ref[0])
            for d in range(1, N_DEV):
                dh = dh + _dot_nt(da[:, d * ws:(d + 1) * ws], w1_ref[d])
            dxn, d_shift, d_scale, d_gn = _modulate_bwd(dh, xhat, rstd, gn, scale)
            dx1_ref[0] = dx + dxn
            h_ref[0] = hb
            da_ref[0] = da
            dout_ref[0] = dout
            d_gate = jnp.sum(dx * mo_ref[0].astype(F32), axis=0, keepdims=True)
            dmod_ref[pl.ds(row, 1), :] += jnp.concatenate([d_shift, d_scale, d_gate], axis=-1)
            dgn_ref[0:1, :] += d_gn

        if ctx_active:
            compute()
        else:
            pl.when(t >= nC)(compute)

            @pl.when(t < nC)
            def _():
                dx1_ref[0] = dx_ref[0]
                h_ref[0] = jnp.zeros((tm, D), BF16)
                da_ref[0] = jnp.zeros((tm, F), BF16)
                dout_ref[0] = jnp.zeros((tm, D), BF16)

    row = lambda w: pl.BlockSpec((1, tm, w), lambda b, t: (b, t, 0))
    sds = lambda w, dt: jax.ShapeDtypeStruct((B, T, w), dt)
    return _pcall(exch)(
        body, name="mlp_bwd", grid=(B, cfg.nT),
        in_specs=[row(D), row(D), row(D), row(F), _layer(l, D), _full((8, 6 * D)), _full((N_DEV, D, ws)), _full((F, D))],
        out_specs=[row(D), row(D), row(F), row(D), _full((8, 3 * D)), _full((8, D))],
        out_shape=[sds(D, F32), sds(D, BF16), sds(F, BF16), sds(D, BF16),
                   jax.ShapeDtypeStruct((8, 3 * D), F32), jax.ShapeDtypeStruct((8, D), F32)],
        compiler_params=_params(2),
    )(x1, dx2, mo, r, gn, mod, w1, w2)


def _matmul_tn(a, g, name, *, by_shard, a_cols=None, exch=None):
    R = a.shape[0]
    Ng = g.shape[1]
    tr = next(c for c in (2304, 1024, 512, 256, 128, 64, 32, 16, 8) if R % c == 0)
    if a_cols is None:
        Ka, a_blk = a.shape[1], 0
        tka = Ka if Ka <= 1024 else 1024
    else:
        a_start, Ka = a_cols
        tka = Ka
        assert a_start % Ka == 0 and Ka % LANES == 0
        a_blk = a_start // Ka
    if by_shard:
        ws = Ng // N_DEV
        per = next(c for c in (8, 4, 2, 1) if c * ws <= 1152 or c == 1)
        tn = per * ws
    else:
        tn = next(c for c in (1152, 1024, 768, 512, 384, 256, 128) if Ng % c == 0)
    assert Ka % tka == 0 and tn % LANES == 0
    nr = R // tr

    def body(a_ref, g_ref, o_ref, acc_ref):
        r = pl.program_id(2)

        @pl.when(r == 0)
        def _():
            acc_ref[...] = jnp.zeros(acc_ref.shape, F32)

        acc_ref[...] += _dot_tn(a_ref[...], g_ref[...])

        @pl.when(r == nr - 1)
        def _():
            if by_shard:
                for d in range(per):
                    o_ref[d] = acc_ref[:, d * ws:(d + 1) * ws].astype(BF16)
            else:
                o_ref[...] = acc_ref[...].astype(BF16)

    if by_shard:
        out_spec = pl.BlockSpec((per, tka, ws), lambda i, j, r: (j, i, 0))
        out_shape = jax.ShapeDtypeStruct((N_DEV, Ka, ws), BF16)
    else:
        out_spec = pl.BlockSpec((tka, tn), lambda i, j, r: (i, j))
        out_shape = jax.ShapeDtypeStruct((Ka, Ng), BF16)
    return _pcall(exch)(
        body, name=name, grid=(Ka // tka, Ng // tn, nr),
        in_specs=[pl.BlockSpec((tr, tka), lambda i, j, r: (r, i + a_blk)), pl.BlockSpec((tr, tn), lambda i, j, r: (r, j))],
        out_specs=out_spec, out_shape=out_shape, scratch_shapes=[pltpu.VMEM((tka, tn), F32)], compiler_params=_params(3),
    )(a, g)


def _merge_bwd(cfg, dx1, mgo, oa, ob, oc, z, mod, wa, wb, wc, wo, *, ctx_active, exch=None):
    B, T, D, tm, nC = cfg.B, cfg.T, cfg.D, cfg.tm, cfg.nC
    ng = 3 * D // cfg.gw

    def body(dx_ref, mgo_ref, oa_ref, ob_ref, oc_ref, *rest):
        gate_refs = rest[:ng]
        (mod_ref, wa_ref, wb_ref, wc_ref, wo_ref,
         doa_ref, dob_ref, doc_ref, dpa_ref, dpb_ref, dpc_ref, y_ref, dmo_ref, dzg_ref, dg1_ref) = rest[ng:]
        b, t = pl.program_id(0), pl.program_id(1)
        _acc_init([dg1_ref])

        def compute():
            row = jnp.where(t < nC, B, b)
            dx = dx_ref[0]
            dg1_ref[pl.ds(row, 1), :] += jnp.sum(dx * mgo_ref[0].astype(F32), axis=0, keepdims=True)
            dmo = (dx * _mod_row(mod_ref, row, 2, D)).astype(BF16)
            dmo_ref[0] = dmo
            dy = _dot_nt(dmo, wo_ref[...])
            gates = _read_gates(cfg, gate_refs)
            y = jnp.zeros((tm, D), F32)
            dgs = []
            for gate, o_ref, w_ref, do_ref, dp_ref in ((gates[0], oa_ref, wa_ref, doa_ref, dpa_ref),
                                                      (gates[1], ob_ref, wb_ref, dob_ref, dpb_ref),
                                                      (gates[2], oc_ref, wc_ref, doc_ref, dpc_ref)):
                s = jax.nn.sigmoid(gate)
                p = _dot(o_ref[0], w_ref[...])
                y = y + s * p
                dp = (dy * s).astype(BF16)
                dp_ref[0] = dp
                do_ref[0] = _dot_nt(dp, w_ref[...]).astype(BF16)
                dgs.append((dy * p * s * (1.0 - s)).astype(BF16))
            y_ref[0] = y.astype(BF16)
            dzg_ref[0] = jnp.concatenate(dgs, axis=-1)

        if ctx_active:
            compute()
        else:
            pl.when(t >= nC)(compute)

            @pl.when(t < nC)
            def _():
                for ref in (doa_ref, dob_ref, doc_ref, dpa_ref, dpb_ref, dpc_ref, y_ref, dmo_ref, dzg_ref):
                    ref[...] = jnp.zeros(ref.shape, ref.dtype)

    row = lambda w: pl.BlockSpec((1, tm, w), lambda b, t: (b, t, 0))
    sds = lambda w: jax.ShapeDtypeStruct((B, T, w), BF16)
    return _pcall(exch)(
        body, name="merge_bwd", grid=(B, cfg.nT),
        in_specs=[row(D), row(D), row(QW), row(POOL_W), row(QW)] + _gate_specs(cfg)
        + [_full((8, 6 * D)), _full((QW, D)), _full((POOL_W, D)), _full((QW, D)), _full((D, D))],
        out_specs=[row(QW), row(POOL_W), row(QW), row(D), row(D), row(D), row(D), row(D), row(3 * D), _full((8, D))],
        out_shape=[sds(QW), sds(POOL_W), sds(QW), sds(D), sds(D), sds(D), sds(D), sds(D), sds(3 * D),
                   jax.ShapeDtypeStruct((8, D), F32)],
        compiler_params=_params(2),
    )(dx1, mgo, oa, ob, oc, *([z] * ng), mod, wa, wb, wc, wo)


def _attn_bwd(cfg, l, q, k, z, vblock, sink8, do, lse, *, window, sink, ctx_attend, name, exch=None):
    B, S, N, T, tq, nC = cfg.B, cfg.S, cfg.N, cfg.T, cfg.tm, cfg.nC

    def body(q_ref, k_ref, v_ref, sink_ref, do_ref, lse_ref, dq_ref, dk_ref, dv_ref, dsink_ref):
        b, t = pl.program_id(0), pl.program_id(1)
        _acc_init([dsink_ref])

        @pl.when(t == 0)
        def _():
            dk_ref[...] = jnp.zeros(dk_ref.shape, F32)
            dv_ref[...] = jnp.zeros(dv_ref.shape, F32)

        def run(loc):
            q_t = q_ref[0]
            do_t = do_ref[0]
            dqs = [None] * N_QHEADS
            dks, dvs = [], []
            dsink_row = jnp.zeros((1, LANES), F32)
            lane = lax.broadcasted_iota(jnp.int32, (1, LANES), 1)
            lo = None
            for h in range(N_KV):
                lse = jnp.concatenate([lse_ref[0, GROUP * h + g:GROUP * h + g + 1, :] for g in range(GROUP)], axis=1)
                qs, kk, vv, p, _, p_s, lo, _ = _attn_scores(cfg, t - nC, q_t, k_ref, v_ref, sink_ref, h, loc, window and loc, sink, lse=lse)
                dos = jnp.concatenate([do_t[:, (GROUP * h + g) * HEAD:(GROUP * h + g + 1) * HEAD] for g in range(GROUP)], axis=0)
                dp = _dot_nt(vv, dos)
                delta = jnp.sum(p * dp, axis=0, keepdims=True)
                ds = (p * (dp - delta)).astype(BF16)
                dq = _dot_tn(kk, ds).T
                dks.append(_dot(ds, qs))
                dvs.append(_dot(p.astype(BF16), dos))
                if sink:
                    dsk = -p_s * delta
                    for g in range(GROUP):
                        tot = jnp.sum(dsk[:, g * tq:(g + 1) * tq], axis=1, keepdims=True)
                        dsink_row = dsink_row + jnp.where(lane == GROUP * h + g, tot, 0.0)
                for g in range(GROUP):
                    dqs[GROUP * h + g] = dq[g * tq:(g + 1) * tq] * QSCALE
            dq_ref[0] = jnp.concatenate(dqs, axis=-1)
            dk = jnp.concatenate(dks, axis=-1)
            dv = jnp.concatenate(dvs, axis=-1)
            if loc and not window:
                dk_ref[0] += dk
                dv_ref[0] += dv
            else:
                dk_ref[0, 0:N, :] += dk[0:N]
                dv_ref[0, 0:N, :] += dv[0:N]
                if loc:
                    W = tq + 2 * WINDOW
                    dk_ref[0, pl.ds(N + lo, W), :] += dk[N:]
                    dv_ref[0, pl.ds(N + lo, W), :] += dv[N:]
            if sink:
                dsink_ref[0:1, :] += dsink_row

        pl.when(t >= nC)(functools.partial(run, True))
        if ctx_attend:
            pl.when(t < nC)(functools.partial(run, False))
        else:
            @pl.when(t < nC)
            def _():
                dq_ref[0] = jnp.zeros((tq, QW), F32)

    kv = pl.BlockSpec((1, T, KVW), lambda b, t: (b, 0, 0))
    qrow = pl.BlockSpec((1, tq, QW), lambda b, t: (b, t, 0))
    return _pcall(exch)(
        body, name=name, grid=(B, cfg.nT),
        in_specs=[qrow, kv, pl.BlockSpec((1, T, KVW), lambda b, t: (b, 0, vblock)),
                  pl.BlockSpec((1, 8, LANES), lambda b, t: (l, 0, 0)), qrow, pl.BlockSpec((1, 8, tq), lambda b, t: (b, 0, t))],
        out_specs=[qrow, kv, kv, _full((8, LANES))],
        out_shape=[jax.ShapeDtypeStruct((B, T, QW), F32), jax.ShapeDtypeStruct((B, T, KVW), F32),
                   jax.ShapeDtypeStruct((B, T, KVW), F32), jax.ShapeDtypeStruct((8, LANES), F32)],
        compiler_params=_params(2),
    )(q, k, z, sink8, do, lse)


def _qknorm_bwd(cfg, l, z, gvec, cosf, sins, dqa, dka, dva, dqc, dkc, dvc):
    B, T, tm = cfg.B, cfg.T, cfg.tm

    def body(z_ref, g_ref, cos_ref, sin_ref, dqa_ref, dka_ref, dva_ref, dqc_ref, dkc_ref, dvc_ref, dz_ref, dg_ref):
        _acc_init([dg_ref])
        ind = _head_indicator()
        cos, sin = cos_ref[...], sin_ref[...]
        dqa_t, dqc_t = dqa_ref[0], dqc_ref[0]
        douts = {0: dqa_t[:, 0:128], 1: dqa_t[:, 128:256], 2: dqa_t[:, 256:384], 3: dka_ref[0],
                 5: dqc_t[:, 0:128], 6: dqc_t[:, 128:256], 7: dqc_t[:, 256:384], 8: dkc_ref[0]}
        pieces = []
        dgs = []
        for c in range(QKV_W // LANES):
            if c not in douts:
                pieces.append(dva_ref[0] if c == 4 else dvc_ref[0])
                dgs.append(jnp.zeros((1, LANES), F32))
                continue
            x = z_ref[0, :, c * LANES:(c + 1) * LANES].astype(F32)
            g = g_ref[0, :, c * LANES:(c + 1) * LANES]
            ss = _head_sum(x * x, ind)
            rstd = lax.rsqrt(ss * (1.0 / HEAD) + EPS)
            n = x * rstd
            dout = douts[c]
            dy = dout * cos + _pair_swap(dout * sin)
            dgs.append(jnp.sum(dy * n, axis=0, keepdims=True))
            dn = dy * g
            mean = _head_sum(dn * n, ind) * (1.0 / HEAD)
            pieces.append(rstd * (dn - n * mean))
        dz_ref[0] = jnp.concatenate(pieces, axis=-1).astype(BF16)
        dg_ref[0:1, :] += jnp.concatenate(dgs, axis=-1)

    row = lambda w: pl.BlockSpec((1, tm, w), lambda b, t: (b, t, 0))
    tab = pl.BlockSpec((tm, LANES), lambda b, t: (t, 0))
    return pl.pallas_call(
        body, name="qknorm_bwd", grid=(B, cfg.nT),
        in_specs=[row(QKV_W), _layer(l, QKV_W), tab, tab, row(QW), row(KVW), row(KVW), row(QW), row(KVW), row(KVW)],
        out_specs=[row(QKV_W), _full((8, QKV_W))],
        out_shape=[jax.ShapeDtypeStruct((B, T, QKV_W), BF16), jax.ShapeDtypeStruct((8, QKV_W), F32)],
        compiler_params=_params(2),
    )(z, gvec, cosf, sins, dqa, dka, dva, dqc, dkc, dvc)


def _pool_bwd(cfg, l, dob, pooled, wp, ps):
    B, T, tm, kw = cfg.B, cfg.T, cfg.tm, cfg.kw

    def body(dob_ref, pooled_ref, wp_ref, ps_ref, du_ref, dwp_ref, dps_ref):
        t = pl.program_id(1)
        _acc_init([dwp_ref, dps_ref])
        r0, seg_lo, seg_hi, k0 = _pool_geometry(cfg, t)
        ps = ps_ref[0]
        wp = wp_ref[0]
        dmix = dob_ref[0, pl.ds(r0, tm), :].astype(F32)
        pooled = pooled_ref[0]
        dps_ref[0:1, :] += jnp.sum(dmix * _dot(pooled, wp), axis=0, keepdims=True)
        dpm = (dmix * ps).astype(BF16)
        dwp_ref[...] += _dot_tn(pooled, dpm)
        dpooled_t = _dot_nt(dpm, wp)
        dpm_w = (dob_ref[0, pl.ds(k0, kw), :].astype(F32) * ps).astype(BF16)
        dpooled_w = _dot_nt(dpm_w, wp)
        rr = r0 + lax.broadcasted_iota(jnp.int32, (tm, kw), 0)
        cc = k0 + lax.broadcasted_iota(jnp.int32, (tm, kw), 1)
        diff = rr - cc
        inseg = (cc >= seg_lo) & (cc < seg_hi)
        ccol = k0 + lax.broadcasted_iota(jnp.int32, (kw, 1), 0)
        group = lax.broadcasted_iota(jnp.int32, (tm, POOL_W), 1) // HEAD
        acc = jnp.zeros((tm, POOL_W), F32)
        for gi, w in enumerate(POOL_WINDOWS):
            h = w // 2
            band_t = jnp.where((diff >= -h) & (diff <= h - 1) & inseg, 1.0, 0.0).astype(BF16)
            hi, lo = _split_bf16(dpooled_w / _pool_count(ccol, h, seg_lo, seg_hi))
            acc = jnp.where(group == gi, _dot(band_t, hi) + _dot(band_t, lo), acc)
        du_ref[0] = (acc - dpooled_t).astype(BF16)

    row = pl.BlockSpec((1, tm, POOL_W), lambda b, t: (b, t, 0))
    return pl.pallas_call(
        body, name="pool_bwd", grid=(B, cfg.nT),
        in_specs=[pl.BlockSpec((1, T, POOL_W), lambda b, t: (b, 0, 0)), row,
                  pl.BlockSpec((1, POOL_W, POOL_W), lambda b, t: (l, 0, 0)), _layer(l, POOL_W)],
        out_specs=[row, _full((POOL_W, POOL_W)), _full((8, POOL_W))],
        out_shape=[jax.ShapeDtypeStruct((B, T, POOL_W), BF16), jax.ShapeDtypeStruct((POOL_W, POOL_W), F32),
                   jax.ShapeDtypeStruct((8, POOL_W), F32)],
        compiler_params=_params(2),
    )(dob, pooled, wp, ps)


def _in_proj_bwd(cfg, l, dzq, du, dzg, w_in, x, dx1, gn, mod, *, latent_only, exch=None):
    B, S, T, D, IN, tm, nC = cfg.B, cfg.S, cfg.T, cfg.D, cfg.IN, cfg.tm, cfg.nC

    def body(dzq_ref, du_ref, dzg_ref, w_ref, x_ref, dx1_ref, gn_ref, mod_ref, dx0_ref, dz_ref, dmod_ref, dgn_ref):
        b, t = pl.program_id(0), pl.program_id(1)
        _acc_init([dmod_ref, dgn_ref])
        row = jnp.where(t < nC, B, b)
        dz = jnp.concatenate([dzq_ref[0], du_ref[0], dzg_ref[0]], axis=-1)
        dz_ref[0] = dz
        dh = _dot_nt(dz, w_ref[...])
        gn = gn_ref[0]
        scale = _mod_row(mod_ref, row, 1, D)
        _, xhat, rstd = _modulate(x_ref[0], gn, _mod_row(mod_ref, row, 0, D), scale)
        dxn, d_shift, d_scale, d_gn = _modulate_bwd(dh, xhat, rstd, gn, scale)
        dx0_ref[0] = dx1_ref[0] + dxn
        dmod_ref[pl.ds(row, 1), :] += jnp.concatenate([d_shift, d_scale], axis=-1)
        dgn_ref[0:1, :] += d_gn

    row = lambda w: pl.BlockSpec((1, tm, w), lambda b, t: (b, t, 0))
    if latent_only:
        dx0_spec = pl.BlockSpec((1, tm, D), lambda b, t: (b, jnp.maximum(t - nC, 0), 0))
        dx0_shape = jax.ShapeDtypeStruct((B, S, D), F32)
    else:
        dx0_spec, dx0_shape = row(D), jax.ShapeDtypeStruct((B, T, D), F32)
    return _pcall(exch)(
        body, name="in_proj_bwd", grid=(B, cfg.nT),
        in_specs=[row(QKV_W), row(POOL_W), row(3 * D), _full((D, IN)), row(D), row(D), _layer(l, D), _full((8, 6 * D))],
        out_specs=[dx0_spec, row(IN), _full((8, 2 * D)), _full((8, D))],
        out_shape=[dx0_shape, jax.ShapeDtypeStruct((B, T, IN), BF16),
                   jax.ShapeDtypeStruct((8, 2 * D), F32), jax.ShapeDtypeStruct((8, D), F32)],
        compiler_params=_params(2),
    )(dzq, du, dzg, w_in, x, dx1, gn, mod)


def _adaln_bwd(cfg, l, cc_all, dm_all, w_ada):
    d, B = cfg.D, cfg.B
    wa = w_ada.shape[2]

    def body(c_ref, dm_ref, w_ref, dw_ref, dc_ref):
        c = c_ref[...]
        s = jax.nn.sigmoid(c)
        dmb = dm_ref[...].astype(BF16)
        dw_ref[...] = _dot_tn((c * s).astype(BF16), dmb)
        dc = _dot_nt(dmb, w_ref[0].astype(BF16)) * (s * (1.0 + c * (1.0 - s)))
        is_ctx = lax.broadcasted_iota(jnp.int32, (8 * N_DEV, 1), 0) % 8 == B
        dc_ref[...] = jnp.broadcast_to(jnp.sum(jnp.where(is_ctx, dc, 0.0), axis=0, keepdims=True), (8, d))

    return pl.pallas_call(
        body, name="adaln_bwd", grid=(1,),
        in_specs=[_full((8 * N_DEV, d)), _full((8 * N_DEV, wa)), pl.BlockSpec((1, d, wa), lambda *_: (l, 0, 0))],
        out_specs=[_full((d, wa)), _full((8, d))],
        out_shape=[jax.ShapeDtypeStruct((d, wa), F32), jax.ShapeDtypeStruct((8, d), F32)],
        compiler_params=_params(1),
    )(cc_all, dm_all, w_ada)


def _dmod_pack(cfg, dmod_in, dg1, dmod_mlp):
    d = cfg.D
    wa = 6 * d // N_DEV

    def body(din_ref, dg1_ref, dmlp_ref, o_ref, db_ref):
        dm = jnp.concatenate([din_ref[...], dg1_ref[...], dmlp_ref[...]], axis=-1)
        for j in range(N_DEV):
            o_ref[j] = dm[:, j * wa:(j + 1) * wa]
        db_ref[...] = jnp.broadcast_to(jnp.sum(dm, axis=0, keepdims=True), (8, 6 * d))

    return pl.pallas_call(
        body, name="dmod_pack", grid=(1,),
        in_specs=[_full((8, 2 * d)), _full((8, d)), _full((8, 3 * d))],
        out_specs=[_full((N_DEV, 8, wa)), _full((8, 6 * d))],
        out_shape=[jax.ShapeDtypeStruct((N_DEV, 8, wa), F32), jax.ShapeDtypeStruct((8, 6 * d), F32)],
        compiler_params=_params(1),
    )(dmod_in, dg1, dmod_mlp)


def _adam_update(g, w, m, v):
    bc1 = 1.0 - ADAM_B1 ** ADAM_STEP
    bc2 = 1.0 - ADAM_B2 ** ADAM_STEP
    m2 = ADAM_B1 * m + (1.0 - ADAM_B1) * g
    v2 = ADAM_B2 * v + (1.0 - ADAM_B2) * (g * g)
    delta = -ADAM_LR * ((m2 / bc1) / (jnp.sqrt(v2 / bc2) + ADAM_EPS) + ADAM_WD * w)
    return delta, m2, v2


def _sum_parts(p_ref):
    g = p_ref[0].astype(F32)
    for d in range(1, p_ref.shape[0]):
        g = g + p_ref[d].astype(F32)
    return g


def _adamw_sharded(parts, w, m, v, name):
    L, K, W = w.shape
    min_rows = min(c.shape[1] for chunks in parts for c in chunks)
    tk = next(c for c in (256, 128, 64, 32, 16, 8) if K % c == 0 and min_rows % c == 0)
    spans, flat = [], []
    for li, chunks in enumerate(parts):
        row = 0
        for c in chunks:
            assert c.shape[1] % tk == 0
            spans.append((li, row // tk, (row + c.shape[1]) // tk))
            flat.append(c)
            row += c.shape[1]
        assert row == K

    def body(*refs):
        p_refs = refs[:len(flat)]
        w_ref, m_ref, v_ref, g_ref, d_ref, m2_ref, v2_ref = refs[len(flat):]
        layer, i = pl.program_id(0), pl.program_id(1)

        def run(p_ref):
            g = _sum_parts(p_ref)
            delta, m2, v2 = _adam_update(g, w_ref[0], m_ref[0], v_ref[0])
            g_ref[0] = g
            d_ref[0] = delta
            m2_ref[0] = m2
            v2_ref[0] = v2

        for (li, lo, hi), p_ref in zip(spans, p_refs):
            pl.when((layer == li) & (i >= lo) & (i < hi))(functools.partial(run, p_ref))

    blk = pl.BlockSpec((1, tk, W), lambda l, i: (l, i, 0))

    def part_spec(span, arr):
        li, lo, hi = span
        return pl.BlockSpec((arr.shape[0], tk, W), lambda l, i: (0, jnp.where((l == li) & (i >= lo) & (i < hi), i - lo, 0), 0))

    return pl.pallas_call(
        body, name=name, grid=(L, K // tk),
        in_specs=[part_spec(sp, arr) for sp, arr in zip(spans, flat)] + [blk, blk, blk],
        out_specs=[blk] * 4, out_shape=[jax.ShapeDtypeStruct((L, K, W), F32)] * 4,
        compiler_params=_params(2),
    )(*flat, w, m, v)


def _adamw_packed(parts, w, m, v, name):
    rows = w.shape[0]
    tr = next(c for c in (256, 128, 64, 32, 16, 8) if rows % c == 0)

    def body(p_ref, w_ref, m_ref, v_ref, g_ref, d_ref, m2_ref, v2_ref):
        g = _sum_parts(p_ref)
        delta, m2, v2 = _adam_update(g, w_ref[...], m_ref[...], v_ref[...])
        g_ref[...] = g
        d_ref[...] = delta
        m2_ref[...] = m2
        v2_ref[...] = v2

    blk = pl.BlockSpec((tr, PACK_W), lambda i: (i, 0))
    return pl.pallas_call(
        body, name=name, grid=(rows // tr,),
        in_specs=[pl.BlockSpec((N_DEV, tr, PACK_W), lambda i: (0, i, 0)), blk, blk, blk],
        out_specs=[blk] * 4, out_shape=[jax.ShapeDtypeStruct((rows, PACK_W), F32)] * 4,
        compiler_params=_params(1),
    )(parts, w, m, v)


_SHARDED = dict(w_ada=True, w_in=True, w_br_a=True, w_br_b=True, w_br_c=True, w_out=False, w_mlp1=True, w_mlp2=False)
_MERGE_WEIGHTS = ("w_br_a", "w_br_b", "w_br_c", "w_out")
_GATHERED = ("w_in",) + _MERGE_WEIGHTS + ("w_mlp1", "w_mlp2")
_KEEP_SHARDS = ("w_mlp1",)
_SMALL = ("c_ctx", "b_ada", "norm1", "norm2", "q_norm_a", "k_norm_a", "q_norm_c", "k_norm_c", "sink_c", "w_pool", "pool_scale")


def _from_shards(name, g):
    n, k, w = g.shape
    if name in _KEEP_SHARDS:
        return g
    if _SHARDED[name]:
        return g.transpose(1, 0, 2).reshape(k, n * w)
    return g.reshape(n * k, w)


def _to_shards(name, g):
    if g.ndim == 3:
        return g
    if _SHARDED[name]:
        k, nw = g.shape
        return g.reshape(k, N_DEV, nw // N_DEV).transpose(1, 0, 2)
    nk, w = g.shape
    return g.reshape(N_DEV, nk // N_DEV, w)


def _pack_small(vals):
    flat = jnp.concatenate([vals[n].reshape(-1) for n in _SMALL])
    rows = -(-flat.shape[0] // (8 * PACK_W)) * 8
    return jnp.pad(flat, (0, rows * PACK_W - flat.shape[0])).reshape(rows, PACK_W)


def _unpack_small(packed, like):
    flat, out, r = packed.reshape(-1), {}, 0
    for n in _SMALL:
        sz = like[n].size
        out[n] = flat[r:r + sz].reshape(like[n].shape)
        r += sz
    return out


def _rope_tables(cfg):
    pos = jnp.arange(cfg.S, dtype=F32)
    r = jnp.floor(pos / GRID_W)
    col = pos - r * GRID_W
    inv = 1.0 / (ROPE_THETA ** (jnp.arange(0, HEAD // 2, 2, dtype=F32) / (HEAD // 2)))
    ang = jnp.concatenate([r[:, None] * inv, col[:, None] * inv], axis=-1)
    cos = jnp.repeat(jnp.cos(ang), 2, axis=-1)
    sin = jnp.repeat(jnp.sin(ang), 2, axis=-1) * jnp.tile(jnp.array([-1.0, 1.0], F32), HEAD // 2)
    cos = jnp.concatenate([jnp.ones((cfg.N, HEAD), F32), cos], axis=0)
    sin = jnp.concatenate([jnp.zeros((cfg.N, HEAD), F32), sin], axis=0)
    return jnp.tile(cos, (1, 2)), jnp.tile(sin, (1, 2))


def _gvec(qa, ka, qc, kc):
    one = jnp.ones((qa.shape[0], KVW), F32)
    t = lambda a, n: jnp.tile(a, (1, n))
    return jnp.concatenate([t(qa, N_QHEADS), t(ka, N_KV), one, t(qc, N_QHEADS), t(kc, N_KV), one], axis=-1)[:, None, :]


def _block_diag(wp):
    L, g, c, _ = wp.shape
    eye = jnp.eye(g, dtype=wp.dtype)
    return (wp[:, :, :, None, :] * eye[None, :, None, :, None]).reshape(L, g * c, g * c)


def _pad8(a):
    return jnp.pad(a, ((0, 8 - a.shape[0]), (0, 0)))


def kernel(x, c, ctx, c_ctx, w_ada, b_ada, norm1, norm2, w_in, q_norm_a, k_norm_a, q_norm_c, k_norm_c, sink_c, w_pool, pool_scale, w_br_a, w_br_b, w_br_c, w_out, w_mlp1, w_mlp2, loss_target, m_c_ctx, m_w_ada, m_b_ada, m_norm1, m_norm2, m_w_in, m_q_norm_a, m_k_norm_a, m_q_norm_c, m_k_norm_c, m_sink_c, m_w_pool, m_pool_scale, m_w_br_a, m_w_br_b, m_w_br_c, m_w_out, m_w_mlp1, m_w_mlp2, v_c_ctx, v_w_ada, v_b_ada, v_norm1, v_norm2, v_w_in, v_q_norm_a, v_k_norm_a, v_q_norm_c, v_k_norm_c, v_sink_c, v_w_pool, v_pool_scale, v_w_br_a, v_w_br_b, v_w_br_c, v_w_out, v_w_mlp1, v_w_mlp2):
    B, S, D = x.shape
    N = ctx.shape[1]
    L = w_ada.shape[0]
    cfg = _Cfg(B, S, N, D)
    T = cfg.T
    weights = dict(c_ctx=c_ctx, w_ada=w_ada, b_ada=b_ada, norm1=norm1, norm2=norm2, w_in=w_in, q_norm_a=q_norm_a,
                   k_norm_a=k_norm_a, q_norm_c=q_norm_c, k_norm_c=k_norm_c, sink_c=sink_c, w_pool=w_pool,
                   pool_scale=pool_scale, w_br_a=w_br_a, w_br_b=w_br_b, w_br_c=w_br_c, w_out=w_out, w_mlp1=w_mlp1, w_mlp2=w_mlp2)
    mom_m = dict(c_ctx=m_c_ctx, w_ada=m_w_ada, b_ada=m_b_ada, norm1=m_norm1, norm2=m_norm2, w_in=m_w_in, q_norm_a=m_q_norm_a,
                 k_norm_a=m_k_norm_a, q_norm_c=m_q_norm_c, k_norm_c=m_k_norm_c, sink_c=m_sink_c, w_pool=m_w_pool,
                 pool_scale=m_pool_scale, w_br_a=m_w_br_a, w_br_b=m_w_br_b, w_br_c=m_w_br_c, w_out=m_w_out, w_mlp1=m_w_mlp1, w_mlp2=m_w_mlp2)
    mom_v = dict(c_ctx=v_c_ctx, w_ada=v_w_ada, b_ada=v_b_ada, norm1=v_norm1, norm2=v_norm2, w_in=v_w_in, q_norm_a=v_q_norm_a,
                 k_norm_a=v_k_norm_a, q_norm_c=v_q_norm_c, k_norm_c=v_k_norm_c, sink_c=v_sink_c, w_pool=v_w_pool,
                 pool_scale=v_pool_scale, w_br_a=v_w_br_a, w_br_b=v_w_br_b, w_br_c=v_w_br_c, w_out=v_w_out, w_mlp1=v_w_mlp1, w_mlp2=v_w_mlp2)

    shards_bf16 = {n: weights[n].astype(BF16) for n in _GATHERED}
    full = [dict() for _ in range(L)]

    def gather_of(items):
        return _Exchange([(shards_bf16[n], l) for l, n in items], scatter=False)

    def gathered(items, arrs):
        for (l, n), a in zip(items, arrs):
            full[l][n] = _from_shards(n, a)

    gathered([(0, "w_in")], [_gather_two_level(shards_bf16["w_in"], 0, "gather_first_weights")])

    def hosting(fn, *a, exch=None, done=None, **kw):
        if exch is None:
            return fn(*a, **kw)
        res = fn(*a, exch=exch, **kw)
        done(res[-exch.n:])
        own = res[:-exch.n]
        return own[0] if len(own) == 1 else own

    def gather_behind(l, names):
        if l >= L:
            return {}
        items = [(l, n) for n in names]
        return dict(exch=gather_of(items), done=functools.partial(gathered, items))

    cosf, sins = _rope_tables(cfg)
    xs = jnp.concatenate([ctx, x], axis=1)
    cc8 = _pad8(jnp.concatenate([c, c_ctx[None, :]], axis=0))
    va_blk, vc_blk = (QW + KVW) // KVW, (2 * QW + 3 * KVW) // KVW
    per_layer = lambda a: a[:, None, :]
    b_ada3, norm1_3, norm2_3, ps3 = per_layer(b_ada), per_layer(norm1), per_layer(norm2), per_layer(pool_scale)
    gvec = _gvec(q_norm_a, k_norm_a, q_norm_c, k_norm_c)
    sink8 = jnp.pad(sink_c[:, None, :], ((0, 0), (0, 7), (0, LANES - N_QHEADS)))
    wp = _block_diag(w_pool).astype(BF16)

    cc_all = _Exchange([cc8], scatter=False).alone("gather_cond")[0].reshape(8 * N_DEV, D)
    mod_cols = _Exchange([_adaln_fwd(cfg, cc_all, w_ada)], scatter=True).alone("scatter_mod")[0]
    mod_all = _adaln_join(cfg, mod_cols, b_ada3)

    saved = []
    for l in range(L):
        fw = full[l]
        ctx_active = l < L - 1
        mod = mod_all[l]
        z, u, h = hosting(_in_proj_fwd, cfg, l, xs, norm1_3, mod, fw["w_in"], **gather_behind(l, _MERGE_WEIGHTS if l == 0 else ("w_mlp2",)))
        qa, ka, qc, kc = _qknorm_fwd(cfg, l, z, gvec, cosf, sins)
        oa, lse_a = hosting(_attn_fwd, cfg, l, qa, ka, z, va_blk, sink8, window=False, sink=False, ctx_attend=ctx_active, name="attn_a_fwd",
                            **gather_behind(l, ("w_mlp1",)))
        oc, lse_c = hosting(_attn_fwd, cfg, l, qc, kc, z, vc_blk, sink8, window=True, sink=True, ctx_attend=ctx_active, name="attn_c_fwd",
                            **(gather_behind(l + 1, _MERGE_WEIGHTS)))
        ob, pooled = _pool_fwd(cfg, l, u, wp, ps3)
        x1, mgo = hosting(_merge_fwd, cfg, xs, oa, ob, oc, z, mod, fw["w_br_a"], fw["w_br_b"], fw["w_br_c"], fw["w_out"],
                          ctx_active=ctx_active, **(gather_behind(0, ("w_mlp2",)) if l == 0 else {}))
        if l < L - 1:
            x2, mo, r = hosting(_mlp_fwd, cfg, l, x1, norm2_3, mod, fw["w_mlp1"], fw["w_mlp2"], ctx_active=ctx_active,
                                **gather_behind(l + 1, ("w_in",)))
        else:
            x2, mo, r, sse = _mlp_fwd(cfg, l, x1, norm2_3, mod, fw["w_mlp1"], fw["w_mlp2"], ctx_active=ctx_active, target=loss_target)
        saved.append(dict(xs=xs, mod=mod, z=z, h=h, qa=qa, ka=ka, qc=qc, kc=kc, oa=oa, oc=oc, ob=ob, pooled=pooled, x1=x1, mgo=mgo, mo=mo,
                          lse_a=lse_a, lse_c=lse_c, r=r))
        xs = x2

    dxs = xs
    loss = lax.psum(0.5 * sse[0, 0] / D, ("x", "y", "c"))

    grads = [dict() for _ in range(L)]
    parts = {}
    small = {n: [None] * L for n in _SMALL if n != "c_ctx"}
    d_c_ctx = jnp.zeros((D,), F32)
    flat2 = lambda a: a.reshape(B * T, a.shape[-1])

    def scatter_of(l, names):
        return _Exchange([_to_shards(n, grads[l][n]) for n in names], scatter=True)

    def scattered(l, names, arrs):
        for n, a in zip(names, arrs):
            parts[(l, n)] = a

    def scatter_behind(l, names):
        if l >= L:
            return {}
        return dict(exch=scatter_of(l, names), done=functools.partial(scattered, l, names))

    for l in reversed(range(L)):
        fw, sv, g = full[l], saved[l], grads[l]
        ctx_active = l < L - 1
        mod = sv["mod"]
        dx1, h2, da, dout, dmod_mlp, dgn2 = hosting(_mlp_bwd, cfg, l, sv["x1"], dxs, sv["mo"], sv["r"], norm2_3, mod, fw["w_mlp1"], fw["w_mlp2"],
                                                    ctx_active=ctx_active, **scatter_behind(l + 1, ("w_in",)))
        g["w_mlp1"] = _matmul_tn(flat2(h2), flat2(da), "dw_mlp1", by_shard=True)
        g["w_mlp2"] = _matmul_tn(flat2(sv["r"]), flat2(dout), "dw_mlp2", by_shard=False)
        doa, dob, doc, dpa, dpb, dpc, y, dmo, dzg, dg1 = _merge_bwd(
            cfg, dx1, sv["mgo"], sv["oa"], sv["ob"], sv["oc"], sv["z"], mod, fw["w_br_a"], fw["w_br_b"], fw["w_br_c"], fw["w_out"],
            ctx_active=ctx_active)
        g["w_out"] = _matmul_tn(flat2(y), flat2(dmo), "dw_out", by_shard=False)
        g["w_br_a"] = _matmul_tn(flat2(sv["oa"]), flat2(dpa), "dw_br_a", by_shard=True)
        g["w_br_b"] = _matmul_tn(flat2(sv["ob"]), flat2(dpb), "dw_br_b", by_shard=True)
        g["w_br_c"] = _matmul_tn(flat2(sv["oc"]), flat2(dpc), "dw_br_c", by_shard=True)
        z = sv["z"]
        dqa, dka, dva, _ = hosting(_attn_bwd, cfg, l, sv["qa"], sv["ka"], z, va_blk, sink8, doa, sv["lse_a"], window=False, sink=False,
                                   ctx_attend=ctx_active, name="attn_a_bwd", **scatter_behind(l, ("w_mlp1", "w_mlp2")))
        dqc, dkc, dvc, dsink = hosting(_attn_bwd, cfg, l, sv["qc"], sv["kc"], z, vc_blk, sink8, doc, sv["lse_c"], window=True, sink=True,
                                       ctx_attend=ctx_active, name="attn_c_bwd", **scatter_behind(l, _MERGE_WEIGHTS))
        dzq, dgvec = _qknorm_bwd(cfg, l, z, gvec, cosf, sins, dqa, dka, dva, dqc, dkc, dvc)
        du, dwp, dps = _pool_bwd(cfg, l, dob, sv["pooled"], wp, ps3)
        dxs, dz, dmod_in, dgn1 = _in_proj_bwd(cfg, l, dzq, du, dzg, fw["w_in"], sv["xs"], dx1, norm1_3, mod, latent_only=(l == 0))
        dmod_cols, dbias = _dmod_pack(cfg, dmod_in, dg1, dmod_mlp)
        dmod_exchange = _Exchange([dmod_cols], scatter=True)
        half = D // 2
        if l > 0:
            g["w_in"], dm_all = _matmul_tn(flat2(sv["h"]), flat2(dz), "dw_in", by_shard=True, exch=dmod_exchange)
        else:
            g_lo, dm_all = _matmul_tn(flat2(sv["h"]), flat2(dz), "dw_in_lo", by_shard=True, a_cols=(0, half), exch=dmod_exchange)
        g["w_ada"], dcc = _adaln_bwd(cfg, l, cc_all, dm_all.reshape(8 * N_DEV, -1), w_ada)
        d_c_ctx = d_c_ctx + dcc[0]
        gv = dgvec[0]
        heads = lambda v, n: v.reshape(n, HEAD).sum(axis=0)
        small["b_ada"][l] = dbias[0]
        small["norm1"][l] = dgn1[0]
        small["norm2"][l] = dgn2[0]
        small["q_norm_a"][l] = heads(gv[0:QW], N_QHEADS)
        small["k_norm_a"][l] = heads(gv[QW:QW + KVW], N_KV)
        small["q_norm_c"][l] = heads(gv[QW + 2 * KVW:2 * QW + 2 * KVW], N_QHEADS)
        small["k_norm_c"][l] = heads(gv[2 * QW + 2 * KVW:2 * QW + 3 * KVW], N_KV)
        small["sink_c"][l] = dsink[0, :N_QHEADS]
        small["w_pool"][l] = jnp.stack([dwp[i * HEAD:(i + 1) * HEAD, i * HEAD:(i + 1) * HEAD] for i in range(len(POOL_WINDOWS))])
        small["pool_scale"][l] = dps[0]
    grad_x = dxs

    small_vals = {n: jnp.stack(v) for n, v in small.items()}
    small_vals["c_ctx"] = d_c_ctx
    small_packed = _pack_small(small_vals)
    g_hi, parts_lo, small_parts = _matmul_tn(
        flat2(saved[0]["h"]), flat2(dz), "dw_in_hi", by_shard=True, a_cols=(half, half),
        exch=_Exchange([g_lo, jnp.broadcast_to(small_packed[None], (N_DEV,) + small_packed.shape)], scatter=True))
    parts_hi = _Exchange([g_hi], scatter=True).alone("scatter_last_grads")[0]
    chunks = {(l, n): [parts[(l, n)]] for l in range(L) for n in _GATHERED if (l, n) in parts}
    chunks[(0, "w_in")] = [parts_lo, parts_hi]
    for l in range(L):
        chunks[(l, "w_ada")] = [grads[l]["w_ada"][None]]
    stepped = {n: _adamw_sharded([chunks[(l, n)] for l in range(L)], weights[n], mom_m[n], mom_v[n], "adamw_" + n) for n in _SHARDED}
    stepped_small = _adamw_packed(small_parts, _pack_small(weights), _pack_small(mom_m), _pack_small(mom_v), "adamw_small")

    outs = []
    for i in range(4):
        res = {n: stepped[n][i] for n in _SHARDED}
        res.update(_unpack_small(stepped_small[i], weights))
        outs.append(res)
    order = ("c_ctx", "w_ada", "b_ada", "norm1", "norm2", "w_in", "q_norm_a", "k_norm_a", "q_norm_c", "k_norm_c", "sink_c",
             "w_pool", "pool_scale", "w_br_a", "w_br_b", "w_br_c", "w_out", "w_mlp1", "w_mlp2")
    return (loss, grad_x, *[res[n] for res in outs for n in order])
```

```python
import functools

import jax
import jax.numpy as jnp
from jax import lax
from jax.experimental import pallas as pl
from jax.experimental.pallas import tpu as pltpu

F32 = jnp.float32
BF16 = jnp.bfloat16

N_DEV = 8
HEAD = 64
N_QHEADS = 6
N_KV = 2
GROUP = 3
QW = N_QHEADS * HEAD
KVW = N_KV * HEAD
QKV_W = 2 * (QW + 2 * KVW)
POOL_W = 256
POOL_WINDOWS = (2, 4, 8, 16)
POOL_HALO = 16
GATE0 = QKV_W + POOL_W
WINDOW = 128
GRID_W = 64
ROPE_THETA = 10000.0
EPS = 1e-6
NEG = -1e30
QSCALE = HEAD ** -0.5
LANES = 128
PACK_W = 1024
VMEM_LIMIT = 56 * 1024 * 1024

ADAM_LR = 0.001
ADAM_B1 = 0.9
ADAM_B2 = 0.999
ADAM_EPS = 1e-08
ADAM_WD = 0.01
ADAM_STEP = 10

NT_DIMS = (((1,), (1,)), ((), ()))
TN_DIMS = (((0,), (0,)), ((), ()))


def _dot(a, b):
    return jnp.dot(a, b, preferred_element_type=F32)


def _dot_nt(a, b):
    return lax.dot_general(a, b, NT_DIMS, preferred_element_type=F32)


def _dot_tn(a, b):
    return lax.dot_general(a, b, TN_DIMS, preferred_element_type=F32)


def _params(n_grid):
    return pltpu.CompilerParams(dimension_semantics=("arbitrary",) * n_grid, vmem_limit_bytes=VMEM_LIMIT)


def _full(shape):
    nd = len(shape)
    return pl.BlockSpec(shape, lambda *_: (0,) * nd)


def _layer(l, width):
    return pl.BlockSpec((1, 1, width), lambda *_: (l, 0, 0))


def _modulate(x, gn, shift, scale):
    rstd = lax.rsqrt(jnp.mean(x * x, axis=-1, keepdims=True) + EPS)
    xhat = x * rstd
    return xhat * gn * (1.0 + scale) + shift, xhat, rstd


def _modulate_bwd(dh, xhat, rstd, gn, scale):
    d_shift = jnp.sum(dh, axis=0, keepdims=True)
    d_scale = jnp.sum(dh * xhat * gn, axis=0, keepdims=True)
    dy = dh * (1.0 + scale)
    d_gn = jnp.sum(dy * xhat, axis=0, keepdims=True)
    dxh = dy * gn
    dx = rstd * (dxh - xhat * jnp.mean(dxh * xhat, axis=-1, keepdims=True))
    return dx, d_shift, d_scale, d_gn


def _mod_row(mod_ref, row, k, d):
    return mod_ref[pl.ds(row, 1), k * d:(k + 1) * d]


class _Cfg:
    def __init__(self, b, s, n, d):
        self.B, self.S, self.N, self.D = b, s, n, d
        self.T = n + s
        self.F = 4 * d
        self.IN = GATE0 + 3 * d
        self.tm = 256 if (n % 256 == 0 and s % 256 == 0) else 128
        self.nT = self.T // self.tm
        self.nC = n // self.tm
        self.gw = 512 if d % 512 == 0 else 256
        self.kw = self.tm + 2 * POOL_HALO
        assert GATE0 % self.gw == 0 and d % self.gw == 0 and b < 8 and self.T >= self.kw and max(POOL_WINDOWS) // 2 <= POOL_HALO
        assert s % GRID_W == 0 and n % self.tm == 0 and s % self.tm == 0 and s >= self.tm + 2 * WINDOW
        assert d % (N_DEV * LANES) == 0


def _peer(k):
    x, y, c = lax.axis_index("x"), lax.axis_index("y"), lax.axis_index("c")
    px = x ^ ((k >> 2) & 1)
    py = y ^ ((k >> 1) & 1)
    pc = c ^ (k & 1)
    return (px, py, pc), 4 * px + 2 * py + pc


class _Exchange:
    def __init__(self, arrays, scatter):
        self.arrays = [a if isinstance(a, tuple) else (a, None) for a in arrays]
        self.scatter = scatter
        self.n = len(self.arrays)

    def operands(self):
        return [a for a, _ in self.arrays]

    def out_shapes(self):
        res = []
        for a, layer in self.arrays:
            shape = a.shape[1:] if (self.scatter or layer is not None) else a.shape
            res.append(jax.ShapeDtypeStruct((N_DEV,) + tuple(shape), a.dtype))
        return res

    def scratch(self):
        n = self.n * (N_DEV - 1)
        return [pltpu.SemaphoreType.DMA((n,)), pltpu.SemaphoreType.DMA((n,)), pltpu.SemaphoreType.DMA((self.n,))]

    def _copies(self, x_refs, out_refs, send_sems, recv_sems, local_sems, want):
        _, me = _peer(0)
        res = []
        for i, ((_, layer), x_ref, out_ref) in enumerate(zip(self.arrays, x_refs, out_refs)):
            if self.scatter:
                src_of = lambda d, x_ref=x_ref: x_ref.at[d]
            elif layer is not None:
                src_of = lambda d, x_ref=x_ref, layer=layer: x_ref.at[layer]
            else:
                src_of = lambda d, x_ref=x_ref: x_ref
            if want == "local":
                res.append(pltpu.make_async_copy(src_of(me), out_ref.at[me], local_sems.at[i]))
                continue
            for k in range(1, N_DEV):
                pos, idx = _peer(k)
                j = i * (N_DEV - 1) + k - 1
                common = dict(send_sem=send_sems.at[j], recv_sem=recv_sems.at[j], device_id=pos, device_id_type=pl.DeviceIdType.MESH)
                if want == "send":
                    res.append(pltpu.make_async_remote_copy(src_ref=src_of(idx), dst_ref=out_ref.at[me], **common))
                else:
                    res.append(pltpu.make_async_remote_copy(src_ref=src_of(me), dst_ref=out_ref.at[idx], **common))
        return res

    def start(self, *refs):
        for cp in self._copies(*refs, "local") + self._copies(*refs, "send"):
            cp.start()

    def wait(self, *refs):
        for cp in self._copies(*refs, "recv"):
            cp.wait_recv()
        for cp in self._copies(*refs, "send"):
            cp.wait_send()
        for cp in self._copies(*refs, "local"):
            cp.wait()

    def alone(self, name):
        n = self.n

        def body(*refs):
            args = (refs[:n], refs[n:2 * n], *refs[2 * n:])
            self.start(*args)
            self.wait(*args)

        any_spec = pl.BlockSpec(memory_space=pl.ANY)
        return pl.pallas_call(body, name=name, in_specs=[any_spec] * n, out_specs=[any_spec] * n,
                              out_shape=self.out_shapes(), scratch_shapes=self.scratch())(*self.operands())


def _gather_two_level(x, layer, name):
    shape = x.shape[1:]

    def body(x_ref, out_ref, send_sems, recv_sems, local_sem):
        mx, my, mc = lax.axis_index("x"), lax.axis_index("y"), lax.axis_index("c")
        me, sibling = (mx, my, mc), (mx, my, 1 - mc)
        chips = [(1 - mx, my), (mx, 1 - my), (1 - mx, 1 - my)]
        src = x_ref.at[layer]

        def slot(px, py, pc):
            return out_ref.at[4 * px + 2 * py + pc]

        def copy(k, block, to, from_src=False):
            return pltpu.make_async_remote_copy(src_ref=src if from_src else slot(*block), dst_ref=slot(*block), send_sem=send_sems.at[k],
                                                recv_sem=recv_sems.at[k], device_id=to, device_id_type=pl.DeviceIdType.MESH)

        mine = pltpu.make_async_copy(src, slot(*me), local_sem)
        mine.start()
        first = [copy(0, me, sibling, True)] + [copy(1 + j, me, (*chip, mc), True) for j, chip in enumerate(chips)]
        for cp in first:
            cp.start()
        passed = [copy(4 + j, (*chip, mc), sibling) for j, chip in enumerate(chips)]
        for j, chip in enumerate(chips):
            copy(1 + j, (*chip, mc), me).wait_recv()
            passed[j].start()
        copy(0, sibling, me).wait_recv()
        for j, chip in enumerate(chips):
            copy(4 + j, (*chip, 1 - mc), me).wait_recv()
        for cp in first + passed:
            cp.wait_send()
        mine.wait()

    any_spec = pl.BlockSpec(memory_space=pl.ANY)
    return pl.pallas_call(
        body, name=name, in_specs=[any_spec], out_specs=any_spec,
        out_shape=jax.ShapeDtypeStruct((N_DEV,) + tuple(shape), x.dtype),
        scratch_shapes=[pltpu.SemaphoreType.DMA((N_DEV - 1,)), pltpu.SemaphoreType.DMA((N_DEV - 1,)), pltpu.SemaphoreType.DMA],
    )(x)


def _pcall(exch):
    if exch is None:
        return pl.pallas_call

    def make(body, *, name, grid, in_specs, out_specs, out_shape, compiler_params, scratch_shapes=()):
        multi = isinstance(out_shape, (list, tuple))
        out_specs_l = list(out_specs) if multi else [out_specs]
        out_shape_l = list(out_shape) if multi else [out_shape]
        n_in, n_out, n_x, n_s = len(in_specs), len(out_specs_l), exch.n, len(scratch_shapes)

        def hosted(*refs):
            ins, x_refs = refs[:n_in], refs[n_in:n_in + n_x]
            o0 = n_in + n_x
            outs, xo_refs = refs[o0:o0 + n_out], refs[o0 + n_out:o0 + n_out + n_x]
            s0 = o0 + n_out + n_x
            own_scratch, sems = refs[s0:s0 + n_s], refs[s0 + n_s:]
            ids = [pl.program_id(i) for i in range(len(grid))]
            first = functools.reduce(jnp.logical_and, [i == 0 for i in ids])
            last = functools.reduce(jnp.logical_and, [i == g - 1 for i, g in zip(ids, grid)])

            @pl.when(first)
            def _():
                exch.start(x_refs, xo_refs, *sems)

            body(*ins, *outs, *own_scratch)

            @pl.when(last)
            def _():
                exch.wait(x_refs, xo_refs, *sems)

        any_spec = pl.BlockSpec(memory_space=pl.ANY)
        call = pl.pallas_call(
            hosted, name=name, grid=grid, in_specs=list(in_specs) + [any_spec] * n_x, out_specs=out_specs_l + [any_spec] * n_x,
            out_shape=out_shape_l + exch.out_shapes(), scratch_shapes=list(scratch_shapes) + exch.scratch(),
            compiler_params=compiler_params)
        return lambda *args: call(*args, *exch.operands())

    return make


def _adaln_fwd(cfg, cc_all, w_ada):
    d = cfg.D
    L, _, wa = w_ada.shape

    def body(c_ref, w_ref, o_ref):
        c = c_ref[...]
        a = (c * jax.nn.sigmoid(c)).astype(BF16)
        for l in range(L):
            m = _dot(a, w_ref[l].astype(BF16))
            for p in range(N_DEV):
                o_ref[p, l] = m[8 * p:8 * (p + 1)]

    return pl.pallas_call(
        body, name="adaln_fwd", grid=(1,),
        in_specs=[_full((8 * N_DEV, d)), _full((L, d, wa))],
        out_specs=_full((N_DEV, L, 8, wa)),
        out_shape=jax.ShapeDtypeStruct((N_DEV, L, 8, wa), F32), compiler_params=_params(1),
    )(cc_all, w_ada)


def _adaln_join(cfg, parts, b_ada):
    d = cfg.D
    _, L, _, wa = parts.shape

    def body(p_ref, b_ref, o_ref):
        for l in range(L):
            for j in range(N_DEV):
                o_ref[l, :, j * wa:(j + 1) * wa] = p_ref[j, l] + b_ref[l, :, j * wa:(j + 1) * wa]

    return pl.pallas_call(
        body, name="adaln_join", grid=(1,),
        in_specs=[_full((N_DEV, L, 8, wa)), _full((L, 1, 6 * d))],
        out_specs=_full((L, 8, 6 * d)),
        out_shape=jax.ShapeDtypeStruct((L, 8, 6 * d), F32), compiler_params=_params(1),
    )(parts, b_ada)


def _in_proj_fwd(cfg, l, x, gn, mod, w_in, exch=None):
    B, T, D, IN, tm, nC = cfg.B, cfg.T, cfg.D, cfg.IN, cfg.tm, cfg.nC

    def body(x_ref, gn_ref, mod_ref, w_ref, z_ref, u_ref, h_ref):
        b, t = pl.program_id(0), pl.program_id(1)
        row = jnp.where(t < nC, B, b)
        h, _, _ = _modulate(x_ref[0], gn_ref[0], _mod_row(mod_ref, row, 0, D), _mod_row(mod_ref, row, 1, D))
        hb = h.astype(BF16)
        h_ref[0] = hb
        z = _dot(hb, w_ref[...])
        z_ref[0] = z.astype(BF16)
        u_ref[0] = z[:, QKV_W:QKV_W + POOL_W]

    row = lambda w: pl.BlockSpec((1, tm, w), lambda b, t: (b, t, 0))
    return _pcall(exch)(
        body, name="in_proj_fwd", grid=(B, cfg.nT),
        in_specs=[row(D), _layer(l, D), _full((8, 6 * D)), _full((D, IN))],
        out_specs=[row(IN), row(POOL_W), row(D)],
        out_shape=[jax.ShapeDtypeStruct((B, T, IN), BF16), jax.ShapeDtypeStruct((B, T, POOL_W), F32), jax.ShapeDtypeStruct((B, T, D), BF16)],
        compiler_params=_params(2),
    )(x, gn, mod, w_in)


def _head_indicator():
    r = lax.broadcasted_iota(jnp.int32, (LANES, LANES), 0) // HEAD
    c = lax.broadcasted_iota(jnp.int32, (LANES, LANES), 1) // HEAD
    return jnp.where(r == c, 1.0, 0.0).astype(BF16)


def _head_sum(x, ind):
    hi = x.astype(BF16)
    lo = (x - hi.astype(F32)).astype(BF16)
    return _dot(hi, ind) + _dot(lo, ind)


def _pair_swap(y):
    lane = lax.broadcasted_iota(jnp.int32, y.shape, 1)
    return jnp.where(lane % 2 == 0, pltpu.roll(y, LANES - 1, 1), pltpu.roll(y, 1, 1))


_Q_CHUNKS = (0, 1, 2, 5, 6, 7)


def _qknorm_fwd(cfg, l, z, gvec, cosf, sins):
    B, T, tm = cfg.B, cfg.T, cfg.tm

    def body(z_ref, g_ref, cos_ref, sin_ref, qa_ref, ka_ref, qc_ref, kc_ref):
        ind = _head_indicator()
        cos, sin = cos_ref[...], sin_ref[...]

        def chunk(c):
            x = z_ref[0, :, c * LANES:(c + 1) * LANES].astype(F32)
            ss = _head_sum(x * x, ind)
            y = x * lax.rsqrt(ss * (1.0 / HEAD) + EPS) * g_ref[0, :, c * LANES:(c + 1) * LANES]
            out = y * cos + _pair_swap(y) * sin
            return (out * QSCALE if c in _Q_CHUNKS else out).astype(BF16)

        qa_ref[0] = jnp.concatenate([chunk(0), chunk(1), chunk(2)], axis=-1)
        ka_ref[0] = chunk(3)
        qc_ref[0] = jnp.concatenate([chunk(5), chunk(6), chunk(7)], axis=-1)
        kc_ref[0] = chunk(8)

    row = lambda w: pl.BlockSpec((1, tm, w), lambda b, t: (b, t, 0))
    tab = pl.BlockSpec((tm, LANES), lambda b, t: (t, 0))
    return pl.pallas_call(
        body, name="qknorm_fwd", grid=(B, cfg.nT),
        in_specs=[row(QKV_W), _layer(l, QKV_W), tab, tab],
        out_specs=[row(QW), row(KVW), row(QW), row(KVW)],
        out_shape=[jax.ShapeDtypeStruct((B, T, w), BF16) for w in (QW, KVW, QW, KVW)],
        compiler_params=_params(2),
    )(z, gvec, cosf, sins)


def _attn_scores(cfg, tl, q, k_ref, v_ref, sink_ref, h, loc, window, sink, lse=None):
    S, N, tq = cfg.S, cfg.N, cfg.tm
    hs = slice(h * HEAD, (h + 1) * HEAD)
    qs = jnp.concatenate([q[:, (GROUP * h + g) * HEAD:(GROUP * h + g + 1) * HEAD] for g in range(GROUP)], axis=0)
    lo = None
    if not loc:
        kk = k_ref[0, 0:N, :][:, hs]
        vv = v_ref[0, 0:N, :].astype(BF16)[:, hs]
    elif not window:
        kk = k_ref[0][:, hs]
        vv = v_ref[0].astype(BF16)[:, hs]
    else:
        W = tq + 2 * WINDOW
        lo = pl.multiple_of(jnp.clip(tl * tq - WINDOW, 0, S - W), LANES)
        kk = jnp.concatenate([k_ref[0, 0:N, :], k_ref[0, pl.ds(N + lo, W), :]], axis=0)[:, hs]
        vv = jnp.concatenate([v_ref[0, 0:N, :], v_ref[0, pl.ds(N + lo, W), :]], axis=0).astype(BF16)[:, hs]
    st = _dot_nt(kk, qs)
    if window:
        krow = lax.broadcasted_iota(jnp.int32, st.shape, 0)
        qpos = tl * tq + lax.broadcasted_iota(jnp.int32, st.shape, 1) % tq
        st = jnp.where((krow < N) | (jnp.abs(qpos - (lo + krow - N)) <= WINDOW), st, NEG)
    sk = None
    if sink:
        colg = lax.broadcasted_iota(jnp.int32, (1, GROUP * tq), 1) // tq
        sk = jnp.zeros((1, GROUP * tq), F32)
        for g in range(GROUP):
            j = GROUP * h + g
            sk = jnp.where(colg == g, sink_ref[0, 0:1, j:j + 1], sk)
    if lse is not None:
        return qs, kk, vv, jnp.exp(st - lse), None, (jnp.exp(sk - lse) if sink else None), lo, lse
    m = jnp.max(st, axis=0, keepdims=True)
    if sink:
        m = jnp.maximum(m, sk)
    e = jnp.exp(st - m)
    l = jnp.sum(e, axis=0, keepdims=True)
    e_s = None
    if sink:
        e_s = jnp.exp(sk - m)
        l = l + e_s
    return qs, kk, vv, e, 1.0 / l, e_s, lo, m + jnp.log(l)


def _attn_fwd(cfg, l, q, k, z, vblock, sink8, *, window, sink, ctx_attend, name, exch=None):
    B, T, tq, nC = cfg.B, cfg.T, cfg.tm, cfg.nC

    def body(q_ref, k_ref, v_ref, sink_ref, o_ref, lse_ref):
        t = pl.program_id(1)

        def run(loc):
            q_t = q_ref[0]
            outs = [None] * N_QHEADS
            lses = [None] * N_QHEADS
            for h in range(N_KV):
                _, _, vv, e, inv, _, _, lse = _attn_scores(cfg, t - nC, q_t, k_ref, v_ref, sink_ref, h, loc, window and loc, sink)
                o = (_dot_tn(vv, e.astype(BF16)) * inv).T
                for g in range(GROUP):
                    outs[GROUP * h + g] = o[g * tq:(g + 1) * tq]
                    lses[GROUP * h + g] = lse[:, g * tq:(g + 1) * tq]
            o_ref[0] = jnp.concatenate(outs, axis=-1).astype(BF16)
            lse_ref[0] = jnp.concatenate(lses + [jnp.zeros((8 - N_QHEADS, tq), F32)], axis=0)

        pl.when(t >= nC)(functools.partial(run, True))
        if ctx_attend:
            pl.when(t < nC)(functools.partial(run, False))
        else:
            @pl.when(t < nC)
            def _():
                o_ref[0] = jnp.zeros((tq, QW), BF16)
                lse_ref[0] = jnp.zeros((8, tq), F32)

    return _pcall(exch)(
        body, name=name, grid=(B, cfg.nT),
        in_specs=[pl.BlockSpec((1, tq, QW), lambda b, t: (b, t, 0)),
                  pl.BlockSpec((1, T, KVW), lambda b, t: (b, 0, 0)),
                  pl.BlockSpec((1, T, KVW), lambda b, t: (b, 0, vblock)),
                  pl.BlockSpec((1, 8, LANES), lambda b, t: (l, 0, 0))],
        out_specs=[pl.BlockSpec((1, tq, QW), lambda b, t: (b, t, 0)), pl.BlockSpec((1, 8, tq), lambda b, t: (b, 0, t))],
        out_shape=[jax.ShapeDtypeStruct((B, T, QW), BF16), jax.ShapeDtypeStruct((B, 8, T), F32)], compiler_params=_params(2),
    )(q, k, z, sink8)


def _pool_geometry(cfg, t):
    tm, N, T, nC = cfg.tm, cfg.N, cfg.T, cfg.nC
    r0 = pl.multiple_of(t * tm, tm)
    isctx = t < nC
    seg_lo = jnp.where(isctx, 0, N)
    seg_hi = jnp.where(isctx, N, T)
    k0 = pl.multiple_of(jnp.clip(t * tm - POOL_HALO, 0, T - cfg.kw), POOL_HALO)
    return r0, seg_lo, seg_hi, k0


def _pool_count(pos, h, seg_lo, seg_hi):
    return jnp.maximum(jnp.minimum(pos + h, seg_hi) - jnp.maximum(pos - h, seg_lo), 1).astype(F32)


def _split_bf16(x):
    hi = x.astype(BF16)
    return hi, (x - hi.astype(F32)).astype(BF16)


def _pool_fwd(cfg, l, u, wp, ps):
    B, T, tm, kw = cfg.B, cfg.T, cfg.tm, cfg.kw

    def body(u_ref, wp_ref, ps_ref, ob_ref, pooled_ref):
        t = pl.program_id(1)
        r0, seg_lo, seg_hi, k0 = _pool_geometry(cfg, t)
        hi, lo = _split_bf16(u_ref[0, pl.ds(k0, kw), :])
        rr = r0 + lax.broadcasted_iota(jnp.int32, (tm, kw), 0)
        cc = k0 + lax.broadcasted_iota(jnp.int32, (tm, kw), 1)
        diff = cc - rr
        inseg = (cc >= seg_lo) & (cc < seg_hi)
        rcol = r0 + lax.broadcasted_iota(jnp.int32, (tm, 1), 0)
        group = lax.broadcasted_iota(jnp.int32, (tm, POOL_W), 1) // HEAD
        acc = jnp.zeros((tm, POOL_W), F32)
        for gi, w in enumerate(POOL_WINDOWS):
            h = w // 2
            band = jnp.where((diff >= -h) & (diff <= h - 1) & inseg, 1.0, 0.0).astype(BF16)
            tot = _dot(band, hi) + _dot(band, lo)
            acc = jnp.where(group == gi, tot / _pool_count(rcol, h, seg_lo, seg_hi), acc)
        pooled = (acc - u_ref[0, pl.ds(r0, tm), :]).astype(BF16)
        pooled_ref[0] = pooled
        ob_ref[0] = (_dot(pooled, wp_ref[0]) * ps_ref[0]).astype(BF16)

    row = pl.BlockSpec((1, tm, POOL_W), lambda b, t: (b, t, 0))
    return pl.pallas_call(
        body, name="pool_fwd", grid=(B, cfg.nT),
        in_specs=[pl.BlockSpec((1, T, POOL_W), lambda b, t: (b, 0, 0)),
                  pl.BlockSpec((1, POOL_W, POOL_W), lambda b, t: (l, 0, 0)), _layer(l, POOL_W)],
        out_specs=[row, row],
        out_shape=[jax.ShapeDtypeStruct((B, T, POOL_W), BF16)] * 2, compiler_params=_params(2),
    )(u, wp, ps)


def _gate_specs(cfg):
    tm, gw = cfg.tm, cfg.gw
    first = GATE0 // gw
    return [pl.BlockSpec((1, tm, gw), functools.partial(lambda b, t, j: (b, t, j), j=first + i)) for i in range(3 * cfg.D // gw)]


def _read_gates(cfg, gate_refs):
    per = cfg.D // cfg.gw
    return [jnp.concatenate([gate_refs[k * per + i][0] for i in range(per)], axis=-1).astype(F32) for k in range(3)]


def _merge_fwd(cfg, x, oa, ob, oc, z, mod, wa, wb, wc, wo, *, ctx_active, exch=None):
    B, T, D, tm, nC = cfg.B, cfg.T, cfg.D, cfg.tm, cfg.nC
    ng = 3 * D // cfg.gw

    def body(x_ref, oa_ref, ob_ref, oc_ref, *rest):
        gate_refs = rest[:ng]
        mod_ref, wa_ref, wb_ref, wc_ref, wo_ref, x1_ref, mgo_ref = rest[ng:]
        b, t = pl.program_id(0), pl.program_id(1)

        def compute():
            row = jnp.where(t < nC, B, b)
            ga, gb, gc = _read_gates(cfg, gate_refs)
            y = (jax.nn.sigmoid(ga) * _dot(oa_ref[0], wa_ref[...])
                 + jax.nn.sigmoid(gb) * _dot(ob_ref[0], wb_ref[...])
                 + jax.nn.sigmoid(gc) * _dot(oc_ref[0], wc_ref[...]))
            mo = _dot(y.astype(BF16), wo_ref[...])
            mgo_ref[0] = mo.astype(BF16)
            x1_ref[0] = x_ref[0] + _mod_row(mod_ref, row, 2, D) * mo

        if ctx_active:
            compute()
        else:
            pl.when(t >= nC)(compute)

            @pl.when(t < nC)
            def _():
                mgo_ref[0] = jnp.zeros((tm, D), BF16)
                x1_ref[0] = x_ref[0]

    row = lambda w: pl.BlockSpec((1, tm, w), lambda b, t: (b, t, 0))
    return _pcall(exch)(
        body, name="merge_fwd", grid=(B, cfg.nT),
        in_specs=[row(D), row(QW), row(POOL_W), row(QW)] + _gate_specs(cfg)
        + [_full((8, 6 * D)), _full((QW, D)), _full((POOL_W, D)), _full((QW, D)), _full((D, D))],
        out_specs=[row(D), row(D)],
        out_shape=[jax.ShapeDtypeStruct((B, T, D), F32), jax.ShapeDtypeStruct((B, T, D), BF16)],
        compiler_params=_params(2),
    )(x, oa, ob, oc, *([z] * ng), mod, wa, wb, wc, wo)


def _w1_apply(hb, w1_ref):
    return jnp.concatenate([_dot(hb, w1_ref[d]) for d in range(N_DEV)], axis=-1)


def _mlp_fwd(cfg, l, x1, gn, mod, w1, w2, *, ctx_active, target=None, exch=None):
    B, T, D, F, tm, nC = cfg.B, cfg.T, cfg.D, cfg.F, cfg.tm, cfg.nC
    assert target is None or not ctx_active

    def body(x_ref, gn_ref, mod_ref, w1_ref, w2_ref, *rest):
        if target is None:
            x2_ref, mo_ref, r_ref = rest
        else:
            tgt_ref, x2_ref, mo_ref, r_ref, sse_ref = rest
            _acc_init([sse_ref])
        b, t = pl.program_id(0), pl.program_id(1)

        def compute():
            row = jnp.where(t < nC, B, b)
            x = x_ref[0]
            h, _, _ = _modulate(x, gn_ref[0], _mod_row(mod_ref, row, 3, D), _mod_row(mod_ref, row, 4, D))
            a = jnp.maximum(_w1_apply(h.astype(BF16), w1_ref), 0.0)
            rb = (a * a).astype(BF16)
            r_ref[0] = rb
            mo = _dot(rb, w2_ref[...])
            mo_ref[0] = mo.astype(BF16)
            x2 = x + _mod_row(mod_ref, row, 5, D) * mo
            if target is None:
                x2_ref[0] = x2
            else:
                err = x2 - tgt_ref[0]
                x2_ref[0] = err * (1.0 / D)
                sse_ref[...] += jnp.sum(err * err)

        if ctx_active:
            compute()
        else:
            pl.when(t >= nC)(compute)

            @pl.when(t < nC)
            def _():
                mo_ref[0] = jnp.zeros((tm, D), BF16)
                r_ref[0] = jnp.zeros((tm, F), BF16)
                x2_ref[0] = x_ref[0] if target is None else jnp.zeros((tm, D), F32)

    row = pl.BlockSpec((1, tm, D), lambda b, t: (b, t, 0))
    in_specs = [row, _layer(l, D), _full((8, 6 * D)), _full((N_DEV, D, F // N_DEV)), _full((F, D))]
    out_specs = [row, row, pl.BlockSpec((1, tm, F), lambda b, t: (b, t, 0))]
    out_shape = [jax.ShapeDtypeStruct((B, T, D), F32), jax.ShapeDtypeStruct((B, T, D), BF16), jax.ShapeDtypeStruct((B, T, F), BF16)]
    args = [x1, gn, mod, w1, w2]
    if target is not None:
        in_specs.append(pl.BlockSpec((1, tm, D), lambda b, t: (b, jnp.maximum(t - nC, 0), 0)))
        out_specs.append(_full((8, LANES)))
        out_shape.append(jax.ShapeDtypeStruct((8, LANES), F32))
        args.append(target)
    return _pcall(exch)(
        body, name="mlp_fwd", grid=(B, cfg.nT), in_specs=in_specs, out_specs=out_specs, out_shape=out_shape,
        compiler_params=_params(2),
    )(*args)


def _acc_init(refs):
    b, t = pl.program_id(0), pl.program_id(1)

    @pl.when((b == 0) & (t == 0))
    def _():
        for ref in refs:
            ref[...] = jnp.zeros(ref.shape, ref.dtype)


def _mlp_bwd(cfg, l, x1, dx2, mo, r, gn, mod, w1, w2, *, ctx_active, exch=None):
    B, T, D, F, tm, nC = cfg.B, cfg.T, cfg.D, cfg.F, cfg.tm, cfg.nC
    ws = F // N_DEV

    def body(x_ref, dx_ref, mo_ref, r_ref, gn_ref, mod_ref, w1_ref, w2_ref, dx1_ref, h_ref, da_ref, dout_ref, dmod_ref, dgn_ref):
        b, t = pl.program_id(0), pl.program_id(1)
        _acc_init([dmod_ref, dgn_ref])

        def compute():
            row = jnp.where(t < nC, B, b)
            gn = gn_ref[0]
            scale = _mod_row(mod_ref, row, 4, D)
            h, xhat, rstd = _modulate(x_ref[0], gn, _mod_row(mod_ref, row, 3, D), scale)
            hb = h.astype(BF16)
            dx = dx_ref[0]
            dout = (dx * _mod_row(mod_ref, row, 5, D)).astype(BF16)
            da = (_dot_nt(dout, w2_ref[...]) * (2.0 * jnp.sqrt(r_ref[0].astype(F32)))).astype(BF16)
            dh = _dot_nt(da[:, 0:ws], w1_ref[0])
            for d in range(1, N_DEV):
                dh = dh + _dot_nt(da[:, d * ws:(d + 1) * ws], w1_ref[d])
            dxn, d_shift, d_scale, d_gn = _modulate_bwd(dh, xhat, rstd, gn, scale)
            dx1_ref[0] = dx + dxn
            h_ref[0] = hb
            da_ref[0] = da
            dout_ref[0] = dout
            d_gate = jnp.sum(dx * mo_ref[0].astype(F32), axis=0, keepdims=True)
            dmod_ref[pl.ds(row, 1), :] += jnp.concatenate([d_shift, d_scale, d_gate], axis=-1)
            dgn_ref[0:1, :] += d_gn

        if ctx_active:
            compute()
        else:
            pl.when(t >= nC)(compute)

            @pl.when(t < nC)
            def _():
                dx1_ref[0] = dx_ref[0]
                h_ref[0] = jnp.zeros((tm, D), BF16)
                da_ref[0] = jnp.zeros((tm, F), BF16)
                dout_ref[0] = jnp.zeros((tm, D), BF16)

    row = lambda w: pl.BlockSpec((1, tm, w), lambda b, t: (b, t, 0))
    sds = lambda w, dt: jax.ShapeDtypeStruct((B, T, w), dt)
    return _pcall(exch)(
        body, name="mlp_bwd", grid=(B, cfg.nT),
        in_specs=[row(D), row(D), row(D), row(F), _layer(l, D), _full((8, 6 * D)), _full((N_DEV, D, ws)), _full((F, D))],
        out_specs=[row(D), row(D), row(F), row(D), _full((8, 3 * D)), _full((8, D))],
        out_shape=[sds(D, F32), sds(D, BF16), sds(F, BF16), sds(D, BF16),
                   jax.ShapeDtypeStruct((8, 3 * D), F32), jax.ShapeDtypeStruct((8, D), F32)],
        compiler_params=_params(2),
    )(x1, dx2, mo, r, gn, mod, w1, w2)


def _matmul_tn(a, g, name, *, by_shard, a_cols=None, exch=None):
    R = a.shape[0]
    Ng = g.shape[1]
    tr = next(c for c in (2304, 1024, 512, 256, 128, 64, 32, 16, 8) if R % c == 0)
    if a_cols is None:
        Ka, a_blk = a.shape[1], 0
        tka = Ka if Ka <= 1024 else 1024
    else:
        a_start, Ka = a_cols
        tka = Ka
        assert a_start % Ka == 0 and Ka % LANES == 0
        a_blk = a_start // Ka
    if by_shard:
        ws = Ng // N_DEV
        per = next(c for c in (8, 4, 2, 1) if c * ws <= 1152 or c == 1)
        tn = per * ws
    else:
        tn = next(c for c in (1152, 1024, 768, 512, 384, 256, 128) if Ng % c == 0)
    assert Ka % tka == 0 and tn % LANES == 0
    nr = R // tr

    def body(a_ref, g_ref, o_ref, acc_ref):
        r = pl.program_id(2)

        @pl.when(r == 0)
        def _():
            acc_ref[...] = jnp.zeros(acc_ref.shape, F32)

        acc_ref[...] += _dot_tn(a_ref[...], g_ref[...])

        @pl.when(r == nr - 1)
        def _():
            if by_shard:
                for d in range(per):
                    o_ref[d] = acc_ref[:, d * ws:(d + 1) * ws].astype(BF16)
            else:
                o_ref[...] = acc_ref[...].astype(BF16)

    if by_shard:
        out_spec = pl.BlockSpec((per, tka, ws), lambda i, j, r: (j, i, 0))
        out_shape = jax.ShapeDtypeStruct((N_DEV, Ka, ws), BF16)
    else:
        out_spec = pl.BlockSpec((tka, tn), lambda i, j, r: (i, j))
        out_shape = jax.ShapeDtypeStruct((Ka, Ng), BF16)
    return _pcall(exch)(
        body, name=name, grid=(Ka // tka, Ng // tn, nr),
        in_specs=[pl.BlockSpec((tr, tka), lambda i, j, r: (r, i + a_blk)), pl.BlockSpec((tr, tn), lambda i, j, r: (r, j))],
        out_specs=out_spec, out_shape=out_shape, scratch_shapes=[pltpu.VMEM((tka, tn), F32)], compiler_params=_params(3),
    )(a, g)


def _merge_bwd(cfg, dx1, mgo, oa, ob, oc, z, mod, wa, wb, wc, wo, *, ctx_active, exch=None):
    B, T, D, tm, nC = cfg.B, cfg.T, cfg.D, cfg.tm, cfg.nC
    ng = 3 * D // cfg.gw

    def body(dx_ref, mgo_ref, oa_ref, ob_ref, oc_ref, *rest):
        gate_refs = rest[:ng]
        (mod_ref, wa_ref, wb_ref, wc_ref, wo_ref,
         doa_ref, dob_ref, doc_ref, dpa_ref, dpb_ref, dpc_ref, y_ref, dmo_ref, dzg_ref, dg1_ref) = rest[ng:]
        b, t = pl.program_id(0), pl.program_id(1)
        _acc_init([dg1_ref])

        def compute():
            row = jnp.where(t < nC, B, b)
            dx = dx_ref[0]
            dg1_ref[pl.ds(row, 1), :] += jnp.sum(dx * mgo_ref[0].astype(F32), axis=0, keepdims=True)
            dmo = (dx * _mod_row(mod_ref, row, 2, D)).astype(BF16)
            dmo_ref[0] = dmo
            dy = _dot_nt(dmo, wo_ref[...])
            gates = _read_gates(cfg, gate_refs)
            y = jnp.zeros((tm, D), F32)
            dgs = []
            for gate, o_ref, w_ref, do_ref, dp_ref in ((gates[0], oa_ref, wa_ref, doa_ref, dpa_ref),
                                                      (gates[1], ob_ref, wb_ref, dob_ref, dpb_ref),
                                                      (gates[2], oc_ref, wc_ref, doc_ref, dpc_ref)):
                s = jax.nn.sigmoid(gate)
                p = _dot(o_ref[0], w_ref[...])
                y = y + s * p
                dp = (dy * s).astype(BF16)
                dp_ref[0] = dp
                do_ref[0] = _dot_nt(dp, w_ref[...]).astype(BF16)
                dgs.append((dy * p * s * (1.0 - s)).astype(BF16))
            y_ref[0] = y.astype(BF16)
            dzg_ref[0] = jnp.concatenate(dgs, axis=-1)

        if ctx_active:
            compute()
        else:
            pl.when(t >= nC)(compute)

            @pl.when(t < nC)
            def _():
                for ref in (doa_ref, dob_ref, doc_ref, dpa_ref, dpb_ref, dpc_ref, y_ref, dmo_ref, dzg_ref):
                    ref[...] = jnp.zeros(ref.shape, ref.dtype)

    row = lambda w: pl.BlockSpec((1, tm, w), lambda b, t: (b, t, 0))
    sds = lambda w: jax.ShapeDtypeStruct((B, T, w), BF16)
    return _pcall(exch)(
        body, name="merge_bwd", grid=(B, cfg.nT),
        in_specs=[row(D), row(D), row(QW), row(POOL_W), row(QW)] + _gate_specs(cfg)
        + [_full((8, 6 * D)), _full((QW, D)), _full((POOL_W, D)), _full((QW, D)), _full((D, D))],
        out_specs=[row(QW), row(POOL_W), row(QW), row(D), row(D), row(D), row(D), row(D), row(3 * D), _full((8, D))],
        out_shape=[sds(QW), sds(POOL_W), sds(QW), sds(D), sds(D), sds(D), sds(D), sds(D), sds(3 * D),
                   jax.ShapeDtypeStruct((8, D), F32)],
        compiler_params=_params(2),
    )(dx1, mgo, oa, ob, oc, *([z] * ng), mod, wa, wb, wc, wo)


def _attn_bwd(cfg, l, q, k, z, vblock, sink8, do, lse, *, window, sink, ctx_attend, name, exch=None):
    B, S, N, T, tq, nC = cfg.B, cfg.S, cfg.N, cfg.T, cfg.tm, cfg.nC

    def body(q_ref, k_ref, v_ref, sink_ref, do_ref, lse_ref, dq_ref, dk_ref, dv_ref, dsink_ref):
        b, t = pl.program_id(0), pl.program_id(1)
        _acc_init([dsink_ref])

        @pl.when(t == 0)
        def _():
            dk_ref[...] = jnp.zeros(dk_ref.shape, F32)
            dv_ref[...] = jnp.zeros(dv_ref.shape, F32)

        def run(loc):
            q_t = q_ref[0]
            do_t = do_ref[0]
            dqs = [None] * N_QHEADS
            dks, dvs = [], []
            dsink_row = jnp.zeros((1, LANES), F32)
            lane = lax.broadcasted_iota(jnp.int32, (1, LANES), 1)
            lo = None
            for h in range(N_KV):
                lse = jnp.concatenate([lse_ref[0, GROUP * h + g:GROUP * h + g + 1, :] for g in range(GROUP)], axis=1)
                qs, kk, vv, p, _, p_s, lo, _ = _attn_scores(cfg, t - nC, q_t, k_ref, v_ref, sink_ref, h, loc, window and loc, sink, lse=lse)
                dos = jnp.concatenate([do_t[:, (GROUP * h + g) * HEAD:(GROUP * h + g + 1) * HEAD] for g in range(GROUP)], axis=0)
                dp = _dot_nt(vv, dos)
                delta = jnp.sum(p * dp, axis=0, keepdims=True)
                ds = (p * (dp - delta)).astype(BF16)
                dq = _dot_tn(kk, ds).T
                dks.append(_dot(ds, qs))
                dvs.append(_dot(p.astype(BF16), dos))
                if sink:
                    dsk = -p_s * delta
                    for g in range(GROUP):
                        tot = jnp.sum(dsk[:, g * tq:(g + 1) * tq], axis=1, keepdims=True)
                        dsink_row = dsink_row + jnp.where(lane == GROUP * h + g, tot, 0.0)
                for g in range(GROUP):
                    dqs[GROUP * h + g] = dq[g * tq:(g + 1) * tq] * QSCALE
            dq_ref[0] = jnp.concatenate(dqs, axis=-1)
            dk = jnp.concatenate(dks, axis=-1)
            dv = jnp.concatenate(dvs, axis=-1)
            if loc and not window:
                dk_ref[0] += dk
                dv_ref[0] += dv
            else:
                dk_ref[0, 0:N, :] += dk[0:N]
                dv_ref[0, 0:N, :] += dv[0:N]
                if loc:
                    W = tq + 2 * WINDOW
                    dk_ref[0, pl.ds(N + lo, W), :] += dk[N:]
                    dv_ref[0, pl.ds(N + lo, W), :] += dv[N:]
            if sink:
                dsink_ref[0:1, :] += dsink_row

        pl.when(t >= nC)(functools.partial(run, True))
        if ctx_attend:
            pl.when(t < nC)(functools.partial(run, False))
        else:
            @pl.when(t < nC)
            def _():
                dq_ref[0] = jnp.zeros((tq, QW), F32)

    kv = pl.BlockSpec((1, T, KVW), lambda b, t: (b, 0, 0))
    qrow = pl.BlockSpec((1, tq, QW), lambda b, t: (b, t, 0))
    return _pcall(exch)(
        body, name=name, grid=(B, cfg.nT),
        in_specs=[qrow, kv, pl.BlockSpec((1, T, KVW), lambda b, t: (b, 0, vblock)),
                  pl.BlockSpec((1, 8, LANES), lambda b, t: (l, 0, 0)), qrow, pl.BlockSpec((1, 8, tq), lambda b, t: (b, 0, t))],
        out_specs=[qrow, kv, kv, _full((8, LANES))],
        out_shape=[jax.ShapeDtypeStruct((B, T, QW), F32), jax.ShapeDtypeStruct((B, T, KVW), F32),
                   jax.ShapeDtypeStruct((B, T, KVW), F32), jax.ShapeDtypeStruct((8, LANES), F32)],
        compiler_params=_params(2),
    )(q, k, z, sink8, do, lse)


def _qknorm_bwd(cfg, l, z, gvec, cosf, sins, dqa, dka, dva, dqc, dkc, dvc):
    B, T, tm = cfg.B, cfg.T, cfg.tm

    def body(z_ref, g_ref, cos_ref, sin_ref, dqa_ref, dka_ref, dva_ref, dqc_ref, dkc_ref, dvc_ref, dz_ref, dg_ref):
        _acc_init([dg_ref])
        ind = _head_indicator()
        cos, sin = cos_ref[...], sin_ref[...]
        dqa_t, dqc_t = dqa_ref[0], dqc_ref[0]
        douts = {0: dqa_t[:, 0:128], 1: dqa_t[:, 128:256], 2: dqa_t[:, 256:384], 3: dka_ref[0],
                 5: dqc_t[:, 0:128], 6: dqc_t[:, 128:256], 7: dqc_t[:, 256:384], 8: dkc_ref[0]}
        pieces = []
        dgs = []
        for c in range(QKV_W // LANES):
            if c not in douts:
                pieces.append(dva_ref[0] if c == 4 else dvc_ref[0])
                dgs.append(jnp.zeros((1, LANES), F32))
                continue
            x = z_ref[0, :, c * LANES:(c + 1) * LANES].astype(F32)
            g = g_ref[0, :, c * LANES:(c + 1) * LANES]
            ss = _head_sum(x * x, ind)
            rstd = lax.rsqrt(ss * (1.0 / HEAD) + EPS)
            n = x * rstd
            dout = douts[c]
            dy = dout * cos + _pair_swap(dout * sin)
            dgs.append(jnp.sum(dy * n, axis=0, keepdims=True))
            dn = dy * g
            mean = _head_sum(dn * n, ind) * (1.0 / HEAD)
            pieces.append(rstd * (dn - n * mean))
        dz_ref[0] = jnp.concatenate(pieces, axis=-1).astype(BF16)
        dg_ref[0:1, :] += jnp.concatenate(dgs, axis=-1)

    row = lambda w: pl.BlockSpec((1, tm, w), lambda b, t: (b, t, 0))
    tab = pl.BlockSpec((tm, LANES), lambda b, t: (t, 0))
    return pl.pallas_call(
        body, name="qknorm_bwd", grid=(B, cfg.nT),
        in_specs=[row(QKV_W), _layer(l, QKV_W), tab, tab, row(QW), row(KVW), row(KVW), row(QW), row(KVW), row(KVW)],
        out_specs=[row(QKV_W), _full((8, QKV_W))],
        out_shape=[jax.ShapeDtypeStruct((B, T, QKV_W), BF16), jax.ShapeDtypeStruct((8, QKV_W), F32)],
        compiler_params=_params(2),
    )(z, gvec, cosf, sins, dqa, dka, dva, dqc, dkc, dvc)


def _pool_bwd(cfg, l, dob, pooled, wp, ps):
    B, T, tm, kw = cfg.B, cfg.T, cfg.tm, cfg.kw

    def body(dob_ref, pooled_ref, wp_ref, ps_ref, du_ref, dwp_ref, dps_ref):
        t = pl.program_id(1)
        _acc_init([dwp_ref, dps_ref])
        r0, seg_lo, seg_hi, k0 = _pool_geometry(cfg, t)
        ps = ps_ref[0]
        wp = wp_ref[0]
        dmix = dob_ref[0, pl.ds(r0, tm), :].astype(F32)
        pooled = pooled_ref[0]
        dps_ref[0:1, :] += jnp.sum(dmix * _dot(pooled, wp), axis=0, keepdims=True)
        dpm = (dmix * ps).astype(BF16)
        dwp_ref[...] += _dot_tn(pooled, dpm)
        dpooled_t = _dot_nt(dpm, wp)
        dpm_w = (dob_ref[0, pl.ds(k0, kw), :].astype(F32) * ps).astype(BF16)
        dpooled_w = _dot_nt(dpm_w, wp)
        rr = r0 + lax.broadcasted_iota(jnp.int32, (tm, kw), 0)
        cc = k0 + lax.broadcasted_iota(jnp.int32, (tm, kw), 1)
        diff = rr - cc
        inseg = (cc >= seg_lo) & (cc < seg_hi)
        ccol = k0 + lax.broadcasted_iota(jnp.int32, (kw, 1), 0)
        group = lax.broadcasted_iota(jnp.int32, (tm, POOL_W), 1) // HEAD
        acc = jnp.zeros((tm, POOL_W), F32)
        for gi, w in enumerate(POOL_WINDOWS):
            h = w // 2
            band_t = jnp.where((diff >= -h) & (diff <= h - 1) & inseg, 1.0, 0.0).astype(BF16)
            hi, lo = _split_bf16(dpooled_w / _pool_count(ccol, h, seg_lo, seg_hi))
            acc = jnp.where(group == gi, _dot(band_t, hi) + _dot(band_t, lo), acc)
        du_ref[0] = (acc - dpooled_t).astype(BF16)

    row = pl.BlockSpec((1, tm, POOL_W), lambda b, t: (b, t, 0))
    return pl.pallas_call(
        body, name="pool_bwd", grid=(B, cfg.nT),
        in_specs=[pl.BlockSpec((1, T, POOL_W), lambda b, t: (b, 0, 0)), row,
                  pl.BlockSpec((1, POOL_W, POOL_W), lambda b, t: (l, 0, 0)), _layer(l, POOL_W)],
        out_specs=[row, _full((POOL_W, POOL_W)), _full((8, POOL_W))],
        out_shape=[jax.ShapeDtypeStruct((B, T, POOL_W), BF16), jax.ShapeDtypeStruct((POOL_W, POOL_W), F32),
                   jax.ShapeDtypeStruct((8, POOL_W), F32)],
        compiler_params=_params(2),
    )(dob, pooled, wp, ps)


def _in_proj_bwd(cfg, l, dzq, du, dzg, w_in, x, dx1, gn, mod, *, latent_only, exch=None):
    B, S, T, D, IN, tm, nC = cfg.B, cfg.S, cfg.T, cfg.D, cfg.IN, cfg.tm, cfg.nC

    def body(dzq_ref, du_ref, dzg_ref, w_ref, x_ref, dx1_ref, gn_ref, mod_ref, dx0_ref, dz_ref, dmod_ref, dgn_ref):
        b, t = pl.program_id(0), pl.program_id(1)
        _acc_init([dmod_ref, dgn_ref])
        row = jnp.where(t < nC, B, b)
        dz = jnp.concatenate([dzq_ref[0], du_ref[0], dzg_ref[0]], axis=-1)
        dz_ref[0] = dz
        dh = _dot_nt(dz, w_ref[...])
        gn = gn_ref[0]
        scale = _mod_row(mod_ref, row, 1, D)
        _, xhat, rstd = _modulate(x_ref[0], gn, _mod_row(mod_ref, row, 0, D), scale)
        dxn, d_shift, d_scale, d_gn = _modulate_bwd(dh, xhat, rstd, gn, scale)
        dx0_ref[0] = dx1_ref[0] + dxn
        dmod_ref[pl.ds(row, 1), :] += jnp.concatenate([d_shift, d_scale], axis=-1)
        dgn_ref[0:1, :] += d_gn

    row = lambda w: pl.BlockSpec((1, tm, w), lambda b, t: (b, t, 0))
    if latent_only:
        dx0_spec = pl.BlockSpec((1, tm, D), lambda b, t: (b, jnp.maximum(t - nC, 0), 0))
        dx0_shape = jax.ShapeDtypeStruct((B, S, D), F32)
    else:
        dx0_spec, dx0_shape = row(D), jax.ShapeDtypeStruct((B, T, D), F32)
    return _pcall(exch)(
        body, name="in_proj_bwd", grid=(B, cfg.nT),
        in_specs=[row(QKV_W), row(POOL_W), row(3 * D), _full((D, IN)), row(D), row(D), _layer(l, D), _full((8, 6 * D))],
        out_specs=[dx0_spec, row(IN), _full((8, 2 * D)), _full((8, D))],
        out_shape=[dx0_shape, jax.ShapeDtypeStruct((B, T, IN), BF16),
                   jax.ShapeDtypeStruct((8, 2 * D), F32), jax.ShapeDtypeStruct((8, D), F32)],
        compiler_params=_params(2),
    )(dzq, du, dzg, w_in, x, dx1, gn, mod)


def _adaln_bwd(cfg, l, cc_all, dm_all, w_ada):
    d, B = cfg.D, cfg.B
    wa = w_ada.shape[2]

    def body(c_ref, dm_ref, w_ref, dw_ref, dc_ref):
        c = c_ref[...]
        s = jax.nn.sigmoid(c)
        dmb = dm_ref[...].astype(BF16)
        dw_ref[...] = _dot_tn((c * s).astype(BF16), dmb)
        dc = _dot_nt(dmb, w_ref[0].astype(BF16)) * (s * (1.0 + c * (1.0 - s)))
        is_ctx = lax.broadcasted_iota(jnp.int32, (8 * N_DEV, 1), 0) % 8 == B
        dc_ref[...] = jnp.broadcast_to(jnp.sum(jnp.where(is_ctx, dc, 0.0), axis=0, keepdims=True), (8, d))

    return pl.pallas_call(
        body, name="adaln_bwd", grid=(1,),
        in_specs=[_full((8 * N_DEV, d)), _full((8 * N_DEV, wa)), pl.BlockSpec((1, d, wa), lambda *_: (l, 0, 0))],
        out_specs=[_full((d, wa)), _full((8, d))],
        out_shape=[jax.ShapeDtypeStruct((d, wa), F32), jax.ShapeDtypeStruct((8, d), F32)],
        compiler_params=_params(1),
    )(cc_all, dm_all, w_ada)


def _dmod_pack(cfg, dmod_in, dg1, dmod_mlp):
    d = cfg.D
    wa = 6 * d // N_DEV

    def body(din_ref, dg1_ref, dmlp_ref, o_ref, db_ref):
        dm = jnp.concatenate([din_ref[...], dg1_ref[...], dmlp_ref[...]], axis=-1)
        for j in range(N_DEV):
            o_ref[j] = dm[:, j * wa:(j + 1) * wa]
        db_ref[...] = jnp.broadcast_to(jnp.sum(dm, axis=0, keepdims=True), (8, 6 * d))

    return pl.pallas_call(
        body, name="dmod_pack", grid=(1,),
        in_specs=[_full((8, 2 * d)), _full((8, d)), _full((8, 3 * d))],
        out_specs=[_full((N_DEV, 8, wa)), _full((8, 6 * d))],
        out_shape=[jax.ShapeDtypeStruct((N_DEV, 8, wa), F32), jax.ShapeDtypeStruct((8, 6 * d), F32)],
        compiler_params=_params(1),
    )(dmod_in, dg1, dmod_mlp)


def _adam_update(g, w, m, v):
    bc1 = 1.0 - ADAM_B1 ** ADAM_STEP
    bc2 = 1.0 - ADAM_B2 ** ADAM_STEP
    m2 = ADAM_B1 * m + (1.0 - ADAM_B1) * g
    v2 = ADAM_B2 * v + (1.0 - ADAM_B2) * (g * g)
    delta = -ADAM_LR * ((m2 / bc1) / (jnp.sqrt(v2 / bc2) + ADAM_EPS) + ADAM_WD * w)
    return delta, m2, v2


def _sum_parts(p_ref):
    g = p_ref[0].astype(F32)
    for d in range(1, p_ref.shape[0]):
        g = g + p_ref[d].astype(F32)
    return g


def _adamw_sharded(parts, w, m, v, name):
    L, K, W = w.shape
    min_rows = min(c.shape[1] for chunks in parts for c in chunks)
    tk = next(c for c in (256, 128, 64, 32, 16, 8) if K % c == 0 and min_rows % c == 0)
    spans, flat = [], []
    for li, chunks in enumerate(parts):
        row = 0
        for c in chunks:
            assert c.shape[1] % tk == 0
            spans.append((li, row // tk, (row + c.shape[1]) // tk))
            flat.append(c)
            row += c.shape[1]
        assert row == K

    def body(*refs):
        p_refs = refs[:len(flat)]
        w_ref, m_ref, v_ref, g_ref, d_ref, m2_ref, v2_ref = refs[len(flat):]
        layer, i = pl.program_id(0), pl.program_id(1)

        def run(p_ref):
            g = _sum_parts(p_ref)
            delta, m2, v2 = _adam_update(g, w_ref[0], m_ref[0], v_ref[0])
            g_ref[0] = g
            d_ref[0] = delta
            m2_ref[0] = m2
            v2_ref[0] = v2

        for (li, lo, hi), p_ref in zip(spans, p_refs):
            pl.when((layer == li) & (i >= lo) & (i < hi))(functools.partial(run, p_ref))

    blk = pl.BlockSpec((1, tk, W), lambda l, i: (l, i, 0))

    def part_spec(span, arr):
        li, lo, hi = span
        return pl.BlockSpec((arr.shape[0], tk, W), lambda l, i: (0, jnp.where((l == li) & (i >= lo) & (i < hi), i - lo, 0), 0))

    return pl.pallas_call(
        body, name=name, grid=(L, K // tk),
        in_specs=[part_spec(sp, arr) for sp, arr in zip(spans, flat)] + [blk, blk, blk],
        out_specs=[blk] * 4, out_shape=[jax.ShapeDtypeStruct((L, K, W), F32)] * 4,
        compiler_params=_params(2),
    )(*flat, w, m, v)


def _adamw_packed(parts, w, m, v, name):
    rows = w.shape[0]
    tr = next(c for c in (256, 128, 64, 32, 16, 8) if rows % c == 0)

    def body(p_ref, w_ref, m_ref, v_ref, g_ref, d_ref, m2_ref, v2_ref):
        g = _sum_parts(p_ref)
        delta, m2, v2 = _adam_update(g, w_ref[...], m_ref[...], v_ref[...])
        g_ref[...] = g
        d_ref[...] = delta
        m2_ref[...] = m2
        v2_ref[...] = v2

    blk = pl.BlockSpec((tr, PACK_W), lambda i: (i, 0))
    return pl.pallas_call(
        body, name=name, grid=(rows // tr,),
        in_specs=[pl.BlockSpec((N_DEV, tr, PACK_W), lambda i: (0, i, 0)), blk, blk, blk],
        out_specs=[blk] * 4, out_shape=[jax.ShapeDtypeStruct((rows, PACK_W), F32)] * 4,
        compiler_params=_params(1),
    )(parts, w, m, v)


_SHARDED = dict(w_ada=True, w_in=True, w_br_a=True, w_br_b=True, w_br_c=True, w_out=False, w_mlp1=True, w_mlp2=False)
_MERGE_WEIGHTS = ("w_br_a", "w_br_b", "w_br_c", "w_out")
_GATHERED = ("w_in",) + _MERGE_WEIGHTS + ("w_mlp1", "w_mlp2")
_KEEP_SHARDS = ("w_mlp1",)
_SMALL = ("c_ctx", "b_ada", "norm1", "norm2", "q_norm_a", "k_norm_a", "q_norm_c", "k_norm_c", "sink_c", "w_pool", "pool_scale")


def _from_shards(name, g):
    n, k, w = g.shape
    if name in _KEEP_SHARDS:
        return g
    if _SHARDED[name]:
        return g.transpose(1, 0, 2).reshape(k, n * w)
    return g.reshape(n * k, w)


def _to_shards(name, g):
    if g.ndim == 3:
        return g
    if _SHARDED[name]:
        k, nw = g.shape
        return g.reshape(k, N_DEV, nw // N_DEV).transpose(1, 0, 2)
    nk, w = g.shape
    return g.reshape(N_DEV, nk // N_DEV, w)


def _pack_small(vals):
    flat = jnp.concatenate([vals[n].reshape(-1) for n in _SMALL])
    rows = -(-flat.shape[0] // (8 * PACK_W)) * 8
    return jnp.pad(flat, (0, rows * PACK_W - flat.shape[0])).reshape(rows, PACK_W)


def _unpack_small(packed, like):
    flat, out, r = packed.reshape(-1), {}, 0
    for n in _SMALL:
        sz = like[n].size
        out[n] = flat[r:r + sz].reshape(like[n].shape)
        r += sz
    return out


def _rope_tables(cfg):
    pos = jnp.arange(cfg.S, dtype=F32)
    r = jnp.floor(pos / GRID_W)
    col = pos - r * GRID_W
    inv = 1.0 / (ROPE_THETA ** (jnp.arange(0, HEAD // 2, 2, dtype=F32) / (HEAD // 2)))
    ang = jnp.concatenate([r[:, None] * inv, col[:, None] * inv], axis=-1)
    cos = jnp.repeat(jnp.cos(ang), 2, axis=-1)
    sin = jnp.repeat(jnp.sin(ang), 2, axis=-1) * jnp.tile(jnp.array([-1.0, 1.0], F32), HEAD // 2)
    cos = jnp.concatenate([jnp.ones((cfg.N, HEAD), F32), cos], axis=0)
    sin = jnp.concatenate([jnp.zeros((cfg.N, HEAD), F32), sin], axis=0)
    return jnp.tile(cos, (1, 2)), jnp.tile(sin, (1, 2))


def _gvec(qa, ka, qc, kc):
    one = jnp.ones((qa.shape[0], KVW), F32)
    t = lambda a, n: jnp.tile(a, (1, n))
    return jnp.concatenate([t(qa, N_QHEADS), t(ka, N_KV), one, t(qc, N_QHEADS), t(kc, N_KV), one], axis=-1)[:, None, :]


def _block_diag(wp):
    L, g, c, _ = wp.shape
    eye = jnp.eye(g, dtype=wp.dtype)
    return (wp[:, :, :, None, :] * eye[None, :, None, :, None]).reshape(L, g * c, g * c)


def _pad8(a):
    return jnp.pad(a, ((0, 8 - a.shape[0]), (0, 0)))


def kernel(x, c, ctx, c_ctx, w_ada, b_ada, norm1, norm2, w_in, q_norm_a, k_norm_a, q_norm_c, k_norm_c, sink_c, w_pool, pool_scale, w_br_a, w_br_b, w_br_c, w_out, w_mlp1, w_mlp2, loss_target, m_c_ctx, m_w_ada, m_b_ada, m_norm1, m_norm2, m_w_in, m_q_norm_a, m_k_norm_a, m_q_norm_c, m_k_norm_c, m_sink_c, m_w_pool, m_pool_scale, m_w_br_a, m_w_br_b, m_w_br_c, m_w_out, m_w_mlp1, m_w_mlp2, v_c_ctx, v_w_ada, v_b_ada, v_norm1, v_norm2, v_w_in, v_q_norm_a, v_k_norm_a, v_q_norm_c, v_k_norm_c, v_sink_c, v_w_pool, v_pool_scale, v_w_br_a, v_w_br_b, v_w_br_c, v_w_out, v_w_mlp1, v_w_mlp2):
    B, S, D = x.shape
    N = ctx.shape[1]
    L = w_ada.shape[0]
    cfg = _Cfg(B, S, N, D)
    T = cfg.T
    weights = dict(c_ctx=c_ctx, w_ada=w_ada, b_ada=b_ada, norm1=norm1, norm2=norm2, w_in=w_in, q_norm_a=q_norm_a,
                   k_norm_a=k_norm_a, q_norm_c=q_norm_c, k_norm_c=k_norm_c, sink_c=sink_c, w_pool=w_pool,
                   pool_scale=pool_scale, w_br_a=w_br_a, w_br_b=w_br_b, w_br_c=w_br_c, w_out=w_out, w_mlp1=w_mlp1, w_mlp2=w_mlp2)
    mom_m = dict(c_ctx=m_c_ctx, w_ada=m_w_ada, b_ada=m_b_ada, norm1=m_norm1, norm2=m_norm2, w_in=m_w_in, q_norm_a=m_q_norm_a,
                 k_norm_a=m_k_norm_a, q_norm_c=m_q_norm_c, k_norm_c=m_k_norm_c, sink_c=m_sink_c, w_pool=m_w_pool,
                 pool_scale=m_pool_scale, w_br_a=m_w_br_a, w_br_b=m_w_br_b, w_br_c=m_w_br_c, w_out=m_w_out, w_mlp1=m_w_mlp1, w_mlp2=m_w_mlp2)
    mom_v = dict(c_ctx=v_c_ctx, w_ada=v_w_ada, b_ada=v_b_ada, norm1=v_norm1, norm2=v_norm2, w_in=v_w_in, q_norm_a=v_q_norm_a,
                 k_norm_a=v_k_norm_a, q_norm_c=v_q_norm_c, k_norm_c=v_k_norm_c, sink_c=v_sink_c, w_pool=v_w_pool,
                 pool_scale=v_pool_scale, w_br_a=v_w_br_a, w_br_b=v_w_br_b, w_br_c=v_w_br_c, w_out=v_w_out, w_mlp1=v_w_mlp1, w_mlp2=v_w_mlp2)

    shards_bf16 = {n: weights[n].astype(BF16) for n in _GATHERED}
    full = [dict() for _ in range(L)]

    def gather_of(items):
        return _Exchange([(shards_bf16[n], l) for l, n in items], scatter=False)

    def gathered(items, arrs):
        for (l, n), a in zip(items, arrs):
            full[l][n] = _from_shards(n, a)

    gathered([(0, "w_in")], [_gather_two_level(shards_bf16["w_in"], 0, "gather_first_weights")])

    def hosting(fn, *a, exch=None, done=None, **kw):
        if exch is None:
            return fn(*a, **kw)
        res = fn(*a, exch=exch, **kw)
        done(res[-exch.n:])
        own = res[:-exch.n]
        return own[0] if len(own) == 1 else own

    def gather_behind(l, names):
        if l >= L:
            return {}
        items = [(l, n) for n in names]
        return dict(exch=gather_of(items), done=functools.partial(gathered, items))

    cosf, sins = _rope_tables(cfg)
    xs = jnp.concatenate([ctx, x], axis=1)
    cc8 = _pad8(jnp.concatenate([c, c_ctx[None, :]], axis=0))
    va_blk, vc_blk = (QW + KVW) // KVW, (2 * QW + 3 * KVW) // KVW
    per_layer = lambda a: a[:, None, :]
    b_ada3, norm1_3, norm2_3, ps3 = per_layer(b_ada), per_layer(norm1), per_layer(norm2), per_layer(pool_scale)
    gvec = _gvec(q_norm_a, k_norm_a, q_norm_c, k_norm_c)
    sink8 = jnp.pad(sink_c[:, None, :], ((0, 0), (0, 7), (0, LANES - N_QHEADS)))
    wp = _block_diag(w_pool).astype(BF16)

    cc_all = _Exchange([cc8], scatter=False).alone("gather_cond")[0].reshape(8 * N_DEV, D)
    mod_cols = _Exchange([_adaln_fwd(cfg, cc_all, w_ada)], scatter=True).alone("scatter_mod")[0]
    mod_all = _adaln_join(cfg, mod_cols, b_ada3)

    saved = []
    for l in range(L):
        fw = full[l]
        ctx_active = l < L - 1
        mod = mod_all[l]
        z, u, h = hosting(_in_proj_fwd, cfg, l, xs, norm1_3, mod, fw["w_in"], **gather_behind(l, _MERGE_WEIGHTS if l == 0 else ("w_mlp2",)))
        qa, ka, qc, kc = _qknorm_fwd(cfg, l, z, gvec, cosf, sins)
        oa, lse_a = hosting(_attn_fwd, cfg, l, qa, ka, z, va_blk, sink8, window=False, sink=False, ctx_attend=ctx_active, name="attn_a_fwd",
                            **gather_behind(l, ("w_mlp1", "w_mlp2") if l == 0 else ("w_mlp1",)))
        oc, lse_c = hosting(_attn_fwd, cfg, l, qc, kc, z, vc_blk, sink8, window=True, sink=True, ctx_attend=ctx_active, name="attn_c_fwd",
                            **(gather_behind(l + 1, _MERGE_WEIGHTS)))
        ob, pooled = _pool_fwd(cfg, l, u, wp, ps3)
        x1, mgo = hosting(_merge_fwd, cfg, xs, oa, ob, oc, z, mod, fw["w_br_a"], fw["w_br_b"], fw["w_br_c"], fw["w_out"],
                          ctx_active=ctx_active)
        if l < L - 1:
            x2, mo, r = hosting(_mlp_fwd, cfg, l, x1, norm2_3, mod, fw["w_mlp1"], fw["w_mlp2"], ctx_active=ctx_active,
                                **gather_behind(l + 1, ("w_in",)))
        else:
            x2, mo, r, sse = _mlp_fwd(cfg, l, x1, norm2_3, mod, fw["w_mlp1"], fw["w_mlp2"], ctx_active=ctx_active, target=loss_target)
        saved.append(dict(xs=xs, mod=mod, z=z, h=h, qa=qa, ka=ka, qc=qc, kc=kc, oa=oa, oc=oc, ob=ob, pooled=pooled, x1=x1, mgo=mgo, mo=mo,
                          lse_a=lse_a, lse_c=lse_c, r=r))
        xs = x2

    dxs = xs
    loss = lax.psum(0.5 * sse[0, 0] / D, ("x", "y", "c"))

    grads = [dict() for _ in range(L)]
    parts = {}
    small = {n: [None] * L for n in _SMALL if n != "c_ctx"}
    d_c_ctx = jnp.zeros((D,), F32)
    flat2 = lambda a: a.reshape(B * T, a.shape[-1])

    def scatter_of(l, names):
        return _Exchange([_to_shards(n, grads[l][n]) for n in names], scatter=True)

    def scattered(l, names, arrs):
        for n, a in zip(names, arrs):
            parts[(l, n)] = a

    def scatter_behind(l, names):
        if l >= L:
            return {}
        return dict(exch=scatter_of(l, names), done=functools.partial(scattered, l, names))

    for l in reversed(range(L)):
        fw, sv, g = full[l], saved[l], grads[l]
        ctx_active = l < L - 1
        mod = sv["mod"]
        dx1, h2, da, dout, dmod_mlp, dgn2 = hosting(_mlp_bwd, cfg, l, sv["x1"], dxs, sv["mo"], sv["r"], norm2_3, mod, fw["w_mlp1"], fw["w_mlp2"],
                                                    ctx_active=ctx_active, **scatter_behind(l + 1, ("w_in",)))
        g["w_mlp1"] = _matmul_tn(flat2(h2), flat2(da), "dw_mlp1", by_shard=True)
        g["w_mlp2"] = _matmul_tn(flat2(sv["r"]), flat2(dout), "dw_mlp2", by_shard=False)
        doa, dob, doc, dpa, dpb, dpc, y, dmo, dzg, dg1 = _merge_bwd(
            cfg, dx1, sv["mgo"], sv["oa"], sv["ob"], sv["oc"], sv["z"], mod, fw["w_br_a"], fw["w_br_b"], fw["w_br_c"], fw["w_out"],
            ctx_active=ctx_active)
        g["w_out"] = _matmul_tn(flat2(y), flat2(dmo), "dw_out", by_shard=False)
        g["w_br_a"] = _matmul_tn(flat2(sv["oa"]), flat2(dpa), "dw_br_a", by_shard=True)
        g["w_br_b"] = _matmul_tn(flat2(sv["ob"]), flat2(dpb), "dw_br_b", by_shard=True)
        g["w_br_c"] = _matmul_tn(flat2(sv["oc"]), flat2(dpc), "dw_br_c", by_shard=True)
        z = sv["z"]
        dqa, dka, dva, _ = hosting(_attn_bwd, cfg, l, sv["qa"], sv["ka"], z, va_blk, sink8, doa, sv["lse_a"], window=False, sink=False,
                                   ctx_attend=ctx_active, name="attn_a_bwd", **scatter_behind(l, ("w_mlp1", "w_mlp2")))
        dqc, dkc, dvc, dsink = hosting(_attn_bwd, cfg, l, sv["qc"], sv["kc"], z, vc_blk, sink8, doc, sv["lse_c"], window=True, sink=True,
                                       ctx_attend=ctx_active, name="attn_c_bwd", **scatter_behind(l, _MERGE_WEIGHTS))
        dzq, dgvec = _qknorm_bwd(cfg, l, z, gvec, cosf, sins, dqa, dka, dva, dqc, dkc, dvc)
        du, dwp, dps = _pool_bwd(cfg, l, dob, sv["pooled"], wp, ps3)
        dxs, dz, dmod_in, dgn1 = _in_proj_bwd(cfg, l, dzq, du, dzg, fw["w_in"], sv["xs"], dx1, norm1_3, mod, latent_only=(l == 0))
        dmod_cols, dbias = _dmod_pack(cfg, dmod_in, dg1, dmod_mlp)
        dmod_exchange = _Exchange([dmod_cols], scatter=True)
        half = D // 2
        if l > 0:
            g["w_in"], dm_all = _matmul_tn(flat2(sv["h"]), flat2(dz), "dw_in", by_shard=True, exch=dmod_exchange)
        else:
            g_lo, dm_all = _matmul_tn(flat2(sv["h"]), flat2(dz), "dw_in_lo", by_shard=True, a_cols=(0, half), exch=dmod_exchange)
        g["w_ada"], dcc = _adaln_bwd(cfg, l, cc_all, dm_all.reshape(8 * N_DEV, -1), w_ada)
        d_c_ctx = d_c_ctx + dcc[0]
        gv = dgvec[0]
        heads = lambda v, n: v.reshape(n, HEAD).sum(axis=0)
        small["b_ada"][l] = dbias[0]
        small["norm1"][l] = dgn1[0]
        small["norm2"][l] = dgn2[0]
        small["q_norm_a"][l] = heads(gv[0:QW], N_QHEADS)
        small["k_norm_a"][l] = heads(gv[QW:QW + KVW], N_KV)
        small["q_norm_c"][l] = heads(gv[QW + 2 * KVW:2 * QW + 2 * KVW], N_QHEADS)
        small["k_norm_c"][l] = heads(gv[2 * QW + 2 * KVW:2 * QW + 3 * KVW], N_KV)
        small["sink_c"][l] = dsink[0, :N_QHEADS]
        small["w_pool"][l] = jnp.stack([dwp[i * HEAD:(i + 1) * HEAD, i * HEAD:(i + 1) * HEAD] for i in range(len(POOL_WINDOWS))])
        small["pool_scale"][l] = dps[0]
    grad_x = dxs

    small_vals = {n: jnp.stack(v) for n, v in small.items()}
    small_vals["c_ctx"] = d_c_ctx
    small_packed = _pack_small(small_vals)
    g_hi, parts_lo, small_parts = _matmul_tn(
        flat2(saved[0]["h"]), flat2(dz), "dw_in_hi", by_shard=True, a_cols=(half, half),
        exch=_Exchange([g_lo, jnp.broadcast_to(small_packed[None], (N_DEV,) + small_packed.shape)], scatter=True))
    parts_hi = _Exchange([g_hi], scatter=True).alone("scatter_last_grads")[0]
    chunks = {(l, n): [parts[(l, n)]] for l in range(L) for n in _GATHERED if (l, n) in parts}
    chunks[(0, "w_in")] = [parts_lo, parts_hi]
    for l in range(L):
        chunks[(l, "w_ada")] = [grads[l]["w_ada"][None]]
    stepped = {n: _adamw_sharded([chunks[(l, n)] for l in range(L)], weights[n], mom_m[n], mom_v[n], "adamw_" + n) for n in _SHARDED}
    stepped_small = _adamw_packed(small_parts, _pack_small(weights), _pack_small(mom_m), _pack_small(mom_v), "adamw_small")

    outs = []
    for i in range(4):
        res = {n: stepped[n][i] for n in _SHARDED}
        res.update(_unpack_small(stepped_small[i], weights))
        outs.append(res)
    order = ("c_ctx", "w_ada", "b_ada", "norm1", "norm2", "w_in", "q_norm_a", "k_norm_a", "q_norm_c", "k_norm_c", "sink_c",
             "w_pool", "pool_scale", "w_br_a", "w_br_b", "w_br_c", "w_out", "w_mlp1", "w_mlp2")
    return (loss, grad_x, *[res[n] for res in outs for n in order])
```

```python
import functools
import math

import jax
import jax.numpy as jnp
from jax import lax
from jax.experimental import pallas as pl
from jax.experimental.pallas import tpu as pltpu

F32 = jnp.float32
BF16 = jnp.bfloat16

N_DEV = 8
HEAD = 64
N_QHEADS = 6
N_KV = 2
GROUP = 3
QW = N_QHEADS * HEAD
KVW = N_KV * HEAD
QKV_W = 2 * (QW + 2 * KVW)
POOL_W = 256
POOL_WINDOWS = (2, 4, 8, 16)
POOL_HALO = 16
GATE0 = QKV_W + POOL_W
WINDOW = 128
GRID_W = 64
ROPE_THETA = 10000.0
EPS = 1e-6
NEG = -1e30
QSCALE = HEAD ** -0.5
LANES = 128
PACK_W = 1024
VMEM_LIMIT = 56 * 1024 * 1024

ADAM_LR = 0.001
ADAM_B1 = 0.9
ADAM_B2 = 0.999
ADAM_EPS = 1e-08
ADAM_WD = 0.01
ADAM_STEP = 10

NT_DIMS = (((1,), (1,)), ((), ()))
TN_DIMS = (((0,), (0,)), ((), ()))


def _dot(a, b):
    return jnp.dot(a, b, preferred_element_type=F32)


def _dot_nt(a, b):
    return lax.dot_general(a, b, NT_DIMS, preferred_element_type=F32)


def _dot_tn(a, b):
    return lax.dot_general(a, b, TN_DIMS, preferred_element_type=F32)


def _params(n_grid):
    return pltpu.CompilerParams(dimension_semantics=("arbitrary",) * n_grid, vmem_limit_bytes=VMEM_LIMIT)


def _full(shape):
    nd = len(shape)
    return pl.BlockSpec(shape, lambda *_: (0,) * nd)


def _layer(l, width):
    return pl.BlockSpec((1, 1, width), lambda *_: (l, 0, 0))


def _modulate(x, gn, shift, scale):
    rstd = lax.rsqrt(jnp.mean(x * x, axis=-1, keepdims=True) + EPS)
    xhat = x * rstd
    return xhat * gn * (1.0 + scale) + shift, xhat, rstd


def _modulate_bwd(dh, xhat, rstd, gn, scale):
    d_shift = jnp.sum(dh, axis=0, keepdims=True)
    d_scale = jnp.sum(dh * xhat * gn, axis=0, keepdims=True)
    dy = dh * (1.0 + scale)
    d_gn = jnp.sum(dy * xhat, axis=0, keepdims=True)
    dxh = dy * gn
    dx = rstd * (dxh - xhat * jnp.mean(dxh * xhat, axis=-1, keepdims=True))
    return dx, d_shift, d_scale, d_gn


def _mod_row(mod_ref, row, k, d):
    return mod_ref[pl.ds(row, 1), k * d:(k + 1) * d]


class _Cfg:
    def __init__(self, b, s, n, d):
        self.B, self.S, self.N, self.D = b, s, n, d
        self.T = n + s
        self.F = 4 * d
        self.IN = GATE0 + 3 * d
        self.tm = 256 if (n % 256 == 0 and s % 256 == 0) else 128
        self.nT = self.T // self.tm
        self.nC = n // self.tm
        self.gw = math.gcd(GATE0, 3 * d)
        self.kw = self.tm + 2 * POOL_HALO
        assert self.gw % LANES == 0 and b < 8 and self.T >= self.kw and max(POOL_WINDOWS) // 2 <= POOL_HALO
        assert s % GRID_W == 0 and n % self.tm == 0 and s % self.tm == 0 and s >= self.tm + 2 * WINDOW
        assert d % (N_DEV * LANES) == 0


def _peer(k):
    x, y, c = lax.axis_index("x"), lax.axis_index("y"), lax.axis_index("c")
    px = x ^ ((k >> 2) & 1)
    py = y ^ ((k >> 1) & 1)
    pc = c ^ (k & 1)
    return (px, py, pc), 4 * px + 2 * py + pc


class _Exchange:
    def __init__(self, arrays, scatter):
        self.arrays = [a if isinstance(a, tuple) else (a, None) for a in arrays]
        self.scatter = scatter
        self.n = len(self.arrays)

    def operands(self):
        return [a for a, _ in self.arrays]

    def out_shapes(self):
        res = []
        for a, layer in self.arrays:
            shape = a.shape[1:] if (self.scatter or layer is not None) else a.shape
            res.append(jax.ShapeDtypeStruct((N_DEV,) + tuple(shape), a.dtype))
        return res

    def scratch(self):
        n = self.n * (N_DEV - 1)
        return [pltpu.SemaphoreType.DMA((n,)), pltpu.SemaphoreType.DMA((n,)), pltpu.SemaphoreType.DMA((self.n,))]

    def _copies(self, x_refs, out_refs, send_sems, recv_sems, local_sems, want):
        _, me = _peer(0)
        res = []
        for i, ((_, layer), x_ref, out_ref) in enumerate(zip(self.arrays, x_refs, out_refs)):
            if self.scatter:
                src_of = lambda d, x_ref=x_ref: x_ref.at[d]
            elif layer is not None:
                src_of = lambda d, x_ref=x_ref, layer=layer: x_ref.at[layer]
            else:
                src_of = lambda d, x_ref=x_ref: x_ref
            if want == "local":
                res.append(pltpu.make_async_copy(src_of(me), out_ref.at[me], local_sems.at[i]))
                continue
            for k in range(1, N_DEV):
                pos, idx = _peer(k)
                j = i * (N_DEV - 1) + k - 1
                common = dict(send_sem=send_sems.at[j], recv_sem=recv_sems.at[j], device_id=pos, device_id_type=pl.DeviceIdType.MESH)
                if want == "send":
                    res.append(pltpu.make_async_remote_copy(src_ref=src_of(idx), dst_ref=out_ref.at[me], **common))
                else:
                    res.append(pltpu.make_async_remote_copy(src_ref=src_of(me), dst_ref=out_ref.at[idx], **common))
        return res

    def start(self, *refs):
        for cp in self._copies(*refs, "local") + self._copies(*refs, "send"):
            cp.start()

    def wait(self, *refs):
        for cp in self._copies(*refs, "recv"):
            cp.wait_recv()
        for cp in self._copies(*refs, "send"):
            cp.wait_send()
        for cp in self._copies(*refs, "local"):
            cp.wait()

    def alone(self, name):
        n = self.n

        def body(*refs):
            args = (refs[:n], refs[n:2 * n], *refs[2 * n:])
            self.start(*args)
            self.wait(*args)

        any_spec = pl.BlockSpec(memory_space=pl.ANY)
        return pl.pallas_call(body, name=name, in_specs=[any_spec] * n, out_specs=[any_spec] * n,
                              out_shape=self.out_shapes(), scratch_shapes=self.scratch())(*self.operands())


def _gather_two_level(x, layer, name):
    shape = x.shape[1:]

    def body(x_ref, out_ref, send_sems, recv_sems, local_sem):
        mx, my, mc = lax.axis_index("x"), lax.axis_index("y"), lax.axis_index("c")
        me, sibling = (mx, my, mc), (mx, my, 1 - mc)
        chips = [(1 - mx, my), (mx, 1 - my), (1 - mx, 1 - my)]
        src = x_ref.at[layer]

        def slot(px, py, pc):
            return out_ref.at[4 * px + 2 * py + pc]

        def copy(k, block, to, from_src=False):
            return pltpu.make_async_remote_copy(src_ref=src if from_src else slot(*block), dst_ref=slot(*block), send_sem=send_sems.at[k],
                                                recv_sem=recv_sems.at[k], device_id=to, device_id_type=pl.DeviceIdType.MESH)

        mine = pltpu.make_async_copy(src, slot(*me), local_sem)
        mine.start()
        first = [copy(0, me, sibling, True)] + [copy(1 + j, me, (*chip, mc), True) for j, chip in enumerate(chips)]
        for cp in first:
            cp.start()
        passed = [copy(4 + j, (*chip, mc), sibling) for j, chip in enumerate(chips)]
        for j, chip in enumerate(chips):
            copy(1 + j, (*chip, mc), me).wait_recv()
            passed[j].start()
        copy(0, sibling, me).wait_recv()
        for j, chip in enumerate(chips):
            copy(4 + j, (*chip, 1 - mc), me).wait_recv()
        for cp in first + passed:
            cp.wait_send()
        mine.wait()

    any_spec = pl.BlockSpec(memory_space=pl.ANY)
    return pl.pallas_call(
        body, name=name, in_specs=[any_spec], out_specs=any_spec,
        out_shape=jax.ShapeDtypeStruct((N_DEV,) + tuple(shape), x.dtype),
        scratch_shapes=[pltpu.SemaphoreType.DMA((N_DEV - 1,)), pltpu.SemaphoreType.DMA((N_DEV - 1,)), pltpu.SemaphoreType.DMA],
    )(x)


def _pcall(exch):
    if exch is None:
        return pl.pallas_call

    def make(body, *, name, grid, in_specs, out_specs, out_shape, compiler_params, scratch_shapes=()):
        multi = isinstance(out_shape, (list, tuple))
        out_specs_l = list(out_specs) if multi else [out_specs]
        out_shape_l = list(out_shape) if multi else [out_shape]
        n_in, n_out, n_x, n_s = len(in_specs), len(out_specs_l), exch.n, len(scratch_shapes)

        def hosted(*refs):
            ins, x_refs = refs[:n_in], refs[n_in:n_in + n_x]
            o0 = n_in + n_x
            outs, xo_refs = refs[o0:o0 + n_out], refs[o0 + n_out:o0 + n_out + n_x]
            s0 = o0 + n_out + n_x
            own_scratch, sems = refs[s0:s0 + n_s], refs[s0 + n_s:]
            ids = [pl.program_id(i) for i in range(len(grid))]
            first = functools.reduce(jnp.logical_and, [i == 0 for i in ids])
            last = functools.reduce(jnp.logical_and, [i == g - 1 for i, g in zip(ids, grid)])

            @pl.when(first)
            def _():
                exch.start(x_refs, xo_refs, *sems)

            body(*ins, *outs, *own_scratch)

            @pl.when(last)
            def _():
                exch.wait(x_refs, xo_refs, *sems)

        any_spec = pl.BlockSpec(memory_space=pl.ANY)
        call = pl.pallas_call(
            hosted, name=name, grid=grid, in_specs=list(in_specs) + [any_spec] * n_x, out_specs=out_specs_l + [any_spec] * n_x,
            out_shape=out_shape_l + exch.out_shapes(), scratch_shapes=list(scratch_shapes) + exch.scratch(),
            compiler_params=compiler_params)
        return lambda *args: call(*args, *exch.operands())

    return make


def _adaln_fwd(cfg, cc_all, w_ada):
    d = cfg.D
    L, _, wa = w_ada.shape

    def body(c_ref, w_ref, o_ref):
        c = c_ref[...]
        a = (c * jax.nn.sigmoid(c)).astype(BF16)
        for l in range(L):
            m = _dot(a, w_ref[l].astype(BF16))
            for p in range(N_DEV):
                o_ref[p, l] = m[8 * p:8 * (p + 1)]

    return pl.pallas_call(
        body, name="adaln_fwd", grid=(1,),
        in_specs=[_full((8 * N_DEV, d)), _full((L, d, wa))],
        out_specs=_full((N_DEV, L, 8, wa)),
        out_shape=jax.ShapeDtypeStruct((N_DEV, L, 8, wa), F32), compiler_params=_params(1),
    )(cc_all, w_ada)


def _adaln_join(cfg, parts, b_ada):
    d = cfg.D
    _, L, _, wa = parts.shape

    def body(p_ref, b_ref, o_ref):
        for l in range(L):
            for j in range(N_DEV):
                o_ref[l, :, j * wa:(j + 1) * wa] = p_ref[j, l] + b_ref[l, :, j * wa:(j + 1) * wa]

    return pl.pallas_call(
        body, name="adaln_join", grid=(1,),
        in_specs=[_full((N_DEV, L, 8, wa)), _full((L, 1, 6 * d))],
        out_specs=_full((L, 8, 6 * d)),
        out_shape=jax.ShapeDtypeStruct((L, 8, 6 * d), F32), compiler_params=_params(1),
    )(parts, b_ada)


def _in_proj_fwd(cfg, l, x, gn, mod, w_in, exch=None):
    B, T, D, IN, tm, nC = cfg.B, cfg.T, cfg.D, cfg.IN, cfg.tm, cfg.nC

    def body(x_ref, gn_ref, mod_ref, w_ref, z_ref, u_ref, h_ref):
        b, t = pl.program_id(0), pl.program_id(1)
        row = jnp.where(t < nC, B, b)
        h, _, _ = _modulate(x_ref[0], gn_ref[0], _mod_row(mod_ref, row, 0, D), _mod_row(mod_ref, row, 1, D))
        hb = h.astype(BF16)
        h_ref[0] = hb
        z = _dot(hb, w_ref[...])
        z_ref[0] = z.astype(BF16)
        u_ref[0] = z[:, QKV_W:QKV_W + POOL_W]

    row = lambda w: pl.BlockSpec((1, tm, w), lambda b, t: (b, t, 0))
    return _pcall(exch)(
        body, name="in_proj_fwd", grid=(B, cfg.nT),
        in_specs=[row(D), _layer(l, D), _full((8, 6 * D)), _full((D, IN))],
        out_specs=[row(IN), row(POOL_W), row(D)],
        out_shape=[jax.ShapeDtypeStruct((B, T, IN), BF16), jax.ShapeDtypeStruct((B, T, POOL_W), F32), jax.ShapeDtypeStruct((B, T, D), BF16)],
        compiler_params=_params(2),
    )(x, gn, mod, w_in)


def _head_indicator():
    r = lax.broadcasted_iota(jnp.int32, (LANES, LANES), 0) // HEAD
    c = lax.broadcasted_iota(jnp.int32, (LANES, LANES), 1) // HEAD
    return jnp.where(r == c, 1.0, 0.0).astype(BF16)


def _head_sum(x, ind):
    hi = x.astype(BF16)
    lo = (x - hi.astype(F32)).astype(BF16)
    return _dot(hi, ind) + _dot(lo, ind)


def _pair_swap(y):
    lane = lax.broadcasted_iota(jnp.int32, y.shape, 1)
    return jnp.where(lane % 2 == 0, pltpu.roll(y, LANES - 1, 1), pltpu.roll(y, 1, 1))


_Q_CHUNKS = (0, 1, 2, 5, 6, 7)


def _qknorm_fwd(cfg, l, z, gvec, cosf, sins):
    B, T, tm = cfg.B, cfg.T, cfg.tm

    def body(z_ref, g_ref, cos_ref, sin_ref, qa_ref, ka_ref, qc_ref, kc_ref):
        ind = _head_indicator()
        cos, sin = cos_ref[...], sin_ref[...]

        def chunk(c):
            x = z_ref[0, :, c * LANES:(c + 1) * LANES].astype(F32)
            ss = _head_sum(x * x, ind)
            y = x * lax.rsqrt(ss * (1.0 / HEAD) + EPS) * g_ref[0, :, c * LANES:(c + 1) * LANES]
            out = y * cos + _pair_swap(y) * sin
            return (out * QSCALE if c in _Q_CHUNKS else out).astype(BF16)

        qa_ref[0] = jnp.concatenate([chunk(0), chunk(1), chunk(2)], axis=-1)
        ka_ref[0] = chunk(3)
        qc_ref[0] = jnp.concatenate([chunk(5), chunk(6), chunk(7)], axis=-1)
        kc_ref[0] = chunk(8)

    row = lambda w: pl.BlockSpec((1, tm, w), lambda b, t: (b, t, 0))
    tab = pl.BlockSpec((tm, LANES), lambda b, t: (t, 0))
    return pl.pallas_call(
        body, name="qknorm_fwd", grid=(B, cfg.nT),
        in_specs=[row(QKV_W), _layer(l, QKV_W), tab, tab],
        out_specs=[row(QW), row(KVW), row(QW), row(KVW)],
        out_shape=[jax.ShapeDtypeStruct((B, T, w), BF16) for w in (QW, KVW, QW, KVW)],
        compiler_params=_params(2),
    )(z, gvec, cosf, sins)


def _attn_scores(cfg, tl, q, k_ref, v_ref, sink_ref, h, loc, window, sink, lse=None):
    S, N, tq = cfg.S, cfg.N, cfg.tm
    hs = slice(h * HEAD, (h + 1) * HEAD)
    qs = jnp.concatenate([q[:, (GROUP * h + g) * HEAD:(GROUP * h + g + 1) * HEAD] for g in range(GROUP)], axis=0)
    lo = None
    if not loc:
        kk = k_ref[0, 0:N, :][:, hs]
        vv = v_ref[0, 0:N, :].astype(BF16)[:, hs]
    elif not window:
        kk = k_ref[0][:, hs]
        vv = v_ref[0].astype(BF16)[:, hs]
    else:
        W = tq + 2 * WINDOW
        lo = pl.multiple_of(jnp.clip(tl * tq - WINDOW, 0, S - W), LANES)
        kk = jnp.concatenate([k_ref[0, 0:N, :], k_ref[0, pl.ds(N + lo, W), :]], axis=0)[:, hs]
        vv = jnp.concatenate([v_ref[0, 0:N, :], v_ref[0, pl.ds(N + lo, W), :]], axis=0).astype(BF16)[:, hs]
    st = _dot_nt(kk, qs)
    if window:
        krow = lax.broadcasted_iota(jnp.int32, st.shape, 0)
        qpos = tl * tq + lax.broadcasted_iota(jnp.int32, st.shape, 1) % tq
        st = jnp.where((krow < N) | (jnp.abs(qpos - (lo + krow - N)) <= WINDOW), st, NEG)
    sk = None
    if sink:
        colg = lax.broadcasted_iota(jnp.int32, (1, GROUP * tq), 1) // tq
        sk = jnp.zeros((1, GROUP * tq), F32)
        for g in range(GROUP):
            j = GROUP * h + g
            sk = jnp.where(colg == g, sink_ref[0, 0:1, j:j + 1], sk)
    if lse is not None:
        return qs, kk, vv, jnp.exp(st - lse), None, (jnp.exp(sk - lse) if sink else None), lo, lse
    m = jnp.max(st, axis=0, keepdims=True)
    if sink:
        m = jnp.maximum(m, sk)
    e = jnp.exp(st - m)
    l = jnp.sum(e, axis=0, keepdims=True)
    e_s = None
    if sink:
        e_s = jnp.exp(sk - m)
        l = l + e_s
    return qs, kk, vv, e, 1.0 / l, e_s, lo, m + jnp.log(l)


def _attn_fwd(cfg, l, q, k, z, vblock, sink8, *, window, sink, ctx_attend, name, exch=None):
    B, T, tq, nC = cfg.B, cfg.T, cfg.tm, cfg.nC

    def body(q_ref, k_ref, v_ref, sink_ref, o_ref, lse_ref):
        t = pl.program_id(1)

        def run(loc):
            q_t = q_ref[0]
            outs = [None] * N_QHEADS
            lses = [None] * N_QHEADS
            for h in range(N_KV):
                _, _, vv, e, inv, _, _, lse = _attn_scores(cfg, t - nC, q_t, k_ref, v_ref, sink_ref, h, loc, window and loc, sink)
                o = (_dot_tn(vv, e.astype(BF16)) * inv).T
                for g in range(GROUP):
                    outs[GROUP * h + g] = o[g * tq:(g + 1) * tq]
                    lses[GROUP * h + g] = lse[:, g * tq:(g + 1) * tq]
            o_ref[0] = jnp.concatenate(outs, axis=-1).astype(BF16)
            lse_ref[0] = jnp.concatenate(lses + [jnp.zeros((8 - N_QHEADS, tq), F32)], axis=0)

        pl.when(t >= nC)(functools.partial(run, True))
        if ctx_attend:
            pl.when(t < nC)(functools.partial(run, False))
        else:
            @pl.when(t < nC)
            def _():
                o_ref[0] = jnp.zeros((tq, QW), BF16)
                lse_ref[0] = jnp.zeros((8, tq), F32)

    return _pcall(exch)(
        body, name=name, grid=(B, cfg.nT),
        in_specs=[pl.BlockSpec((1, tq, QW), lambda b, t: (b, t, 0)),
                  pl.BlockSpec((1, T, KVW), lambda b, t: (b, 0, 0)),
                  pl.BlockSpec((1, T, KVW), lambda b, t: (b, 0, vblock)),
                  pl.BlockSpec((1, 8, LANES), lambda b, t: (l, 0, 0))],
        out_specs=[pl.BlockSpec((1, tq, QW), lambda b, t: (b, t, 0)), pl.BlockSpec((1, 8, tq), lambda b, t: (b, 0, t))],
        out_shape=[jax.ShapeDtypeStruct((B, T, QW), BF16), jax.ShapeDtypeStruct((B, 8, T), F32)], compiler_params=_params(2),
    )(q, k, z, sink8)


def _pool_geometry(cfg, t):
    tm, N, T, nC = cfg.tm, cfg.N, cfg.T, cfg.nC
    r0 = pl.multiple_of(t * tm, tm)
    isctx = t < nC
    seg_lo = jnp.where(isctx, 0, N)
    seg_hi = jnp.where(isctx, N, T)
    k0 = pl.multiple_of(jnp.clip(t * tm - POOL_HALO, 0, T - cfg.kw), POOL_HALO)
    return r0, seg_lo, seg_hi, k0


def _pool_count(pos, h, seg_lo, seg_hi):
    return jnp.maximum(jnp.minimum(pos + h, seg_hi) - jnp.maximum(pos - h, seg_lo), 1).astype(F32)


def _split_bf16(x):
    hi = x.astype(BF16)
    return hi, (x - hi.astype(F32)).astype(BF16)


def _pool_fwd(cfg, l, u, wp, ps):
    B, T, tm, kw = cfg.B, cfg.T, cfg.tm, cfg.kw

    def body(u_ref, wp_ref, ps_ref, ob_ref, pooled_ref):
        t = pl.program_id(1)
        r0, seg_lo, seg_hi, k0 = _pool_geometry(cfg, t)
        hi, lo = _split_bf16(u_ref[0, pl.ds(k0, kw), :])
        rr = r0 + lax.broadcasted_iota(jnp.int32, (tm, kw), 0)
        cc = k0 + lax.broadcasted_iota(jnp.int32, (tm, kw), 1)
        diff = cc - rr
        inseg = (cc >= seg_lo) & (cc < seg_hi)
        rcol = r0 + lax.broadcasted_iota(jnp.int32, (tm, 1), 0)
        group = lax.broadcasted_iota(jnp.int32, (tm, POOL_W), 1) // HEAD
        acc = jnp.zeros((tm, POOL_W), F32)
        for gi, w in enumerate(POOL_WINDOWS):
            h = w // 2
            band = jnp.where((diff >= -h) & (diff <= h - 1) & inseg, 1.0, 0.0).astype(BF16)
            tot = _dot(band, hi) + _dot(band, lo)
            acc = jnp.where(group == gi, tot / _pool_count(rcol, h, seg_lo, seg_hi), acc)
        pooled = (acc - u_ref[0, pl.ds(r0, tm), :]).astype(BF16)
        pooled_ref[0] = pooled
        ob_ref[0] = (_dot(pooled, wp_ref[0]) * ps_ref[0]).astype(BF16)

    row = pl.BlockSpec((1, tm, POOL_W), lambda b, t: (b, t, 0))
    return pl.pallas_call(
        body, name="pool_fwd", grid=(B, cfg.nT),
        in_specs=[pl.BlockSpec((1, T, POOL_W), lambda b, t: (b, 0, 0)),
                  pl.BlockSpec((1, POOL_W, POOL_W), lambda b, t: (l, 0, 0)), _layer(l, POOL_W)],
        out_specs=[row, row],
        out_shape=[jax.ShapeDtypeStruct((B, T, POOL_W), BF16)] * 2, compiler_params=_params(2),
    )(u, wp, ps)


def _gate_specs(cfg):
    tm, gw = cfg.tm, cfg.gw
    first = GATE0 // gw
    return [pl.BlockSpec((1, tm, gw), functools.partial(lambda b, t, j: (b, t, j), j=first + i)) for i in range(3 * cfg.D // gw)]


def _read_gates(cfg, gate_refs):
    gates = jnp.concatenate([ref[0] for ref in gate_refs], axis=-1).astype(F32)
    return [gates[:, k * cfg.D:(k + 1) * cfg.D] for k in range(3)]


def _merge_fwd(cfg, x, oa, ob, oc, z, mod, wa, wb, wc, wo, *, ctx_active, exch=None):
    B, T, D, tm, nC = cfg.B, cfg.T, cfg.D, cfg.tm, cfg.nC
    ng = 3 * D // cfg.gw

    def body(x_ref, oa_ref, ob_ref, oc_ref, *rest):
        gate_refs = rest[:ng]
        mod_ref, wa_ref, wb_ref, wc_ref, wo_ref, x1_ref, mgo_ref = rest[ng:]
        b, t = pl.program_id(0), pl.program_id(1)

        def compute():
            row = jnp.where(t < nC, B, b)
            ga, gb, gc = _read_gates(cfg, gate_refs)
            y = (jax.nn.sigmoid(ga) * _dot(oa_ref[0], wa_ref[...])
                 + jax.nn.sigmoid(gb) * _dot(ob_ref[0], wb_ref[...])
                 + jax.nn.sigmoid(gc) * _dot(oc_ref[0], wc_ref[...]))
            mo = _dot(y.astype(BF16), wo_ref[...])
            mgo_ref[0] = mo.astype(BF16)
            x1_ref[0] = x_ref[0] + _mod_row(mod_ref, row, 2, D) * mo

        if ctx_active:
            compute()
        else:
            pl.when(t >= nC)(compute)

            @pl.when(t < nC)
            def _():
                mgo_ref[0] = jnp.zeros((tm, D), BF16)
                x1_ref[0] = x_ref[0]

    row = lambda w: pl.BlockSpec((1, tm, w), lambda b, t: (b, t, 0))
    return _pcall(exch)(
        body, name="merge_fwd", grid=(B, cfg.nT),
        in_specs=[row(D), row(QW), row(POOL_W), row(QW)] + _gate_specs(cfg)
        + [_full((8, 6 * D)), _full((QW, D)), _full((POOL_W, D)), _full((QW, D)), _full((D, D))],
        out_specs=[row(D), row(D)],
        out_shape=[jax.ShapeDtypeStruct((B, T, D), F32), jax.ShapeDtypeStruct((B, T, D), BF16)],
        compiler_params=_params(2),
    )(x, oa, ob, oc, *([z] * ng), mod, wa, wb, wc, wo)


def _w1_apply(hb, w1_ref):
    return jnp.concatenate([_dot(hb, w1_ref[d]) for d in range(N_DEV)], axis=-1)


def _mlp_fwd(cfg, l, x1, gn, mod, w1, w2, *, ctx_active, target=None, exch=None):
    B, T, D, F, tm, nC = cfg.B, cfg.T, cfg.D, cfg.F, cfg.tm, cfg.nC
    assert target is None or not ctx_active

    def body(x_ref, gn_ref, mod_ref, w1_ref, w2_ref, *rest):
        if target is None:
            x2_ref, mo_ref, r_ref = rest
        else:
            tgt_ref, x2_ref, mo_ref, r_ref, sse_ref = rest
            _acc_init([sse_ref])
        b, t = pl.program_id(0), pl.program_id(1)

        def compute():
            row = jnp.where(t < nC, B, b)
            x = x_ref[0]
            h, _, _ = _modulate(x, gn_ref[0], _mod_row(mod_ref, row, 3, D), _mod_row(mod_ref, row, 4, D))
            a = jnp.maximum(_w1_apply(h.astype(BF16), w1_ref), 0.0)
            rb = (a * a).astype(BF16)
            r_ref[0] = rb
            mo = _dot(rb, w2_ref[...])
            mo_ref[0] = mo.astype(BF16)
            x2 = x + _mod_row(mod_ref, row, 5, D) * mo
            if target is None:
                x2_ref[0] = x2
            else:
                err = x2 - tgt_ref[0]
                x2_ref[0] = err * (1.0 / D)
                sse_ref[...] += jnp.sum(err * err)

        if ctx_active:
            compute()
        else:
            pl.when(t >= nC)(compute)

            @pl.when(t < nC)
            def _():
                mo_ref[0] = jnp.zeros((tm, D), BF16)
                r_ref[0] = jnp.zeros((tm, F), BF16)
                x2_ref[0] = x_ref[0] if target is None else jnp.zeros((tm, D), F32)

    row = pl.BlockSpec((1, tm, D), lambda b, t: (b, t, 0))
    in_specs = [row, _layer(l, D), _full((8, 6 * D)), _full((N_DEV, D, F // N_DEV)), _full((F, D))]
    out_specs = [row, row, pl.BlockSpec((1, tm, F), lambda b, t: (b, t, 0))]
    out_shape = [jax.ShapeDtypeStruct((B, T, D), F32), jax.ShapeDtypeStruct((B, T, D), BF16), jax.ShapeDtypeStruct((B, T, F), BF16)]
    args = [x1, gn, mod, w1, w2]
    if target is not None:
        in_specs.append(pl.BlockSpec((1, tm, D), lambda b, t: (b, jnp.maximum(t - nC, 0), 0)))
        out_specs.append(_full((8, LANES)))
        out_shape.append(jax.ShapeDtypeStruct((8, LANES), F32))
        args.append(target)
    return _pcall(exch)(
        body, name="mlp_fwd", grid=(B, cfg.nT), in_specs=in_specs, out_specs=out_specs, out_shape=out_shape,
        compiler_params=_params(2),
    )(*args)


def _acc_init(refs):
    b, t = pl.program_id(0), pl.program_id(1)

    @pl.when((b == 0) & (t == 0))
    def _():
        for ref in refs:
            ref[...] = jnp.zeros(ref.shape, ref.dtype)


def _mlp_bwd(cfg, l, x1, dx2, mo, r, gn, mod, w1, w2, *, ctx_active, exch=None):
    B, T, D, F, tm, nC = cfg.B, cfg.T, cfg.D, cfg.F, cfg.tm, cfg.nC
    ws = F // N_DEV

    def body(x_ref, dx_ref, mo_ref, r_ref, gn_ref, mod_ref, w1_ref, w2_ref, dx1_ref, h_ref, da_ref, dout_ref, dmod_ref, dgn_ref):
        b, t = pl.program_id(0), pl.program_id(1)
        _acc_init([dmod_ref, dgn_ref])

        def compute():
            row = jnp.where(t < nC, B, b)
            gn = gn_ref[0]
            scale = _mod_row(mod_ref, row, 4, D)
            h, xhat, rstd = _modulate(x_ref[0], gn, _mod_row(mod_ref, row, 3, D), scale)
            hb = h.astype(BF16)
            dx = dx_ref[0]
            dout = (dx * _mod_row(mod_ref, row, 5, D)).astype(BF16)
            da = (_dot_nt(dout, w2_ref[...]) * (2.0 * jnp.sqrt(r_ref[0].astype(F32)))).astype(BF16)
            dh = _dot_nt(da[:, 0:ws], w1_ref[0])
            for d in range(1, N_DEV):
                dh = dh + _dot_nt(da[:, d * ws:(d + 1) * ws], w1_ref[d])
            dxn, d_shift, d_scale, d_gn = _modulate_bwd(dh, xhat, rstd, gn, scale)
            dx1_ref[0] = dx + dxn
            h_ref[0] = hb
            da_ref[0] = da
            dout_ref[0] = dout
            d_gate = jnp.sum(dx * mo_ref[0].astype(F32), axis=0, keepdims=True)
            dmod_ref[pl.ds(row, 1), :] += jnp.concatenate([d_shift, d_scale, d_gate], axis=-1)
            dgn_ref[0:1, :] += d_gn

        if ctx_active:
            compute()
        else:
            pl.when(t >= nC)(compute)

            @pl.when(t < nC)
            def _():
                dx1_ref[0] = dx_ref[0]
                h_ref[0] = jnp.zeros((tm, D), BF16)
                da_ref[0] = jnp.zeros((tm, F), BF16)
                dout_ref[0] = jnp.zeros((tm, D), BF16)

    row = lambda w: pl.BlockSpec((1, tm, w), lambda b, t: (b, t, 0))
    sds = lambda w, dt: jax.ShapeDtypeStruct((B, T, w), dt)
    return _pcall(exch)(
        body, name="mlp_bwd", grid=(B, cfg.nT),
        in_specs=[row(D), row(D), row(D), row(F), _layer(l, D), _full((8, 6 * D)), _full((N_DEV, D, ws)), _full((F, D))],
        out_specs=[row(D), row(D), row(F), row(D), _full((8, 3 * D)), _full((8, D))],
        out_shape=[sds(D, F32), sds(D, BF16), sds(F, BF16), sds(D, BF16),
                   jax.ShapeDtypeStruct((8, 3 * D), F32), jax.ShapeDtypeStruct((8, D), F32)],
        compiler_params=_params(2),
    )(x1, dx2, mo, r, gn, mod, w1, w2)


def _matmul_tn(a, g, name, *, by_shard, a_cols=None, exch=None):
    R = a.shape[0]
    Ng = g.shape[1]
    tr = next(c for c in (2304, 1024, 512, 256, 128, 64, 32, 16, 8) if R % c == 0)
    if a_cols is None:
        Ka, a_blk = a.shape[1], 0
        tka = Ka if Ka <= 1024 else 1024
    else:
        a_start, Ka = a_cols
        tka = Ka
        assert a_start % Ka == 0 and Ka % LANES == 0
        a_blk = a_start // Ka
    if by_shard:
        ws = Ng // N_DEV
        per = next(c for c in (8, 4, 2, 1) if c * ws <= 1152 or c == 1)
        tn = per * ws
    else:
        tn = next(c for c in (1152, 1024, 768, 512, 384, 256, 128) if Ng % c == 0)
    assert Ka % tka == 0 and tn % LANES == 0
    nr = R // tr

    def body(a_ref, g_ref, o_ref, acc_ref):
        r = pl.program_id(2)

        @pl.when(r == 0)
        def _():
            acc_ref[...] = jnp.zeros(acc_ref.shape, F32)

        acc_ref[...] += _dot_tn(a_ref[...], g_ref[...])

        @pl.when(r == nr - 1)
        def _():
            if by_shard:
                for d in range(per):
                    o_ref[d] = acc_ref[:, d * ws:(d + 1) * ws].astype(BF16)
            else:
                o_ref[...] = acc_ref[...].astype(BF16)

    if by_shard:
        out_spec = pl.BlockSpec((per, tka, ws), lambda i, j, r: (j, i, 0))
        out_shape = jax.ShapeDtypeStruct((N_DEV, Ka, ws), BF16)
    else:
        out_spec = pl.BlockSpec((tka, tn), lambda i, j, r: (i, j))
        out_shape = jax.ShapeDtypeStruct((Ka, Ng), BF16)
    return _pcall(exch)(
        body, name=name, grid=(Ka // tka, Ng // tn, nr),
        in_specs=[pl.BlockSpec((tr, tka), lambda i, j, r: (r, i + a_blk)), pl.BlockSpec((tr, tn), lambda i, j, r: (r, j))],
        out_specs=out_spec, out_shape=out_shape, scratch_shapes=[pltpu.VMEM((tka, tn), F32)], compiler_params=_params(3),
    )(a, g)


def _merge_bwd(cfg, dx1, mgo, oa, ob, oc, z, mod, wa, wb, wc, wo, *, ctx_active, exch=None):
    B, T, D, tm, nC = cfg.B, cfg.T, cfg.D, cfg.tm, cfg.nC
    ng = 3 * D // cfg.gw

    def body(dx_ref, mgo_ref, oa_ref, ob_ref, oc_ref, *rest):
        gate_refs = rest[:ng]
        (mod_ref, wa_ref, wb_ref, wc_ref, wo_ref,
         doa_ref, dob_ref, doc_ref, dpa_ref, dpb_ref, dpc_ref, y_ref, dmo_ref, dzg_ref, dg1_ref) = rest[ng:]
        b, t = pl.program_id(0), pl.program_id(1)
        _acc_init([dg1_ref])

        def compute():
            row = jnp.where(t < nC, B, b)
            dx = dx_ref[0]
            dg1_ref[pl.ds(row, 1), :] += jnp.sum(dx * mgo_ref[0].astype(F32), axis=0, keepdims=True)
            dmo = (dx * _mod_row(mod_ref, row, 2, D)).astype(BF16)
            dmo_ref[0] = dmo
            dy = _dot_nt(dmo, wo_ref[...])
            gates = _read_gates(cfg, gate_refs)
            y = jnp.zeros((tm, D), F32)
            dgs = []
            for gate, o_ref, w_ref, do_ref, dp_ref in ((gates[0], oa_ref, wa_ref, doa_ref, dpa_ref),
                                                      (gates[1], ob_ref, wb_ref, dob_ref, dpb_ref),
                                                      (gates[2], oc_ref, wc_ref, doc_ref, dpc_ref)):
                s = jax.nn.sigmoid(gate)
                p = _dot(o_ref[0], w_ref[...])
                y = y + s * p
                dp = (dy * s).astype(BF16)
                dp_ref[0] = dp
                do_ref[0] = _dot_nt(dp, w_ref[...]).astype(BF16)
                dgs.append((dy * p * s * (1.0 - s)).astype(BF16))
            y_ref[0] = y.astype(BF16)
            dzg_ref[0] = jnp.concatenate(dgs, axis=-1)

        if ctx_active:
            compute()
        else:
            pl.when(t >= nC)(compute)

            @pl.when(t < nC)
            def _():
                for ref in (doa_ref, dob_ref, doc_ref, dpa_ref, dpb_ref, dpc_ref, y_ref, dmo_ref, dzg_ref):
                    ref[...] = jnp.zeros(ref.shape, ref.dtype)

    row = lambda w: pl.BlockSpec((1, tm, w), lambda b, t: (b, t, 0))
    sds = lambda w: jax.ShapeDtypeStruct((B, T, w), BF16)
    return _pcall(exch)(
        body, name="merge_bwd", grid=(B, cfg.nT),
        in_specs=[row(D), row(D), row(QW), row(POOL_W), row(QW)] + _gate_specs(cfg)
        + [_full((8, 6 * D)), _full((QW, D)), _full((POOL_W, D)), _full((QW, D)), _full((D, D))],
        out_specs=[row(QW), row(POOL_W), row(QW), row(D), row(D), row(D), row(D), row(D), row(3 * D), _full((8, D))],
        out_shape=[sds(QW), sds(POOL_W), sds(QW), sds(D), sds(D), sds(D), sds(D), sds(D), sds(3 * D),
                   jax.ShapeDtypeStruct((8, D), F32)],
        compiler_params=_params(2),
    )(dx1, mgo, oa, ob, oc, *([z] * ng), mod, wa, wb, wc, wo)


def _attn_bwd(cfg, l, q, k, z, vblock, sink8, do, lse, *, window, sink, ctx_attend, name, exch=None):
    B, S, N, T, tq, nC = cfg.B, cfg.S, cfg.N, cfg.T, cfg.tm, cfg.nC

    def body(q_ref, k_ref, v_ref, sink_ref, do_ref, lse_ref, dq_ref, dk_ref, dv_ref, dsink_ref):
        b, t = pl.program_id(0), pl.program_id(1)
        _acc_init([dsink_ref])

        @pl.when(t == 0)
        def _():
            dk_ref[...] = jnp.zeros(dk_ref.shape, F32)
            dv_ref[...] = jnp.zeros(dv_ref.shape, F32)

        def run(loc):
            q_t = q_ref[0]
            do_t = do_ref[0]
            dqs = [None] * N_QHEADS
            dks, dvs = [], []
            dsink_row = jnp.zeros((1, LANES), F32)
            lane = lax.broadcasted_iota(jnp.int32, (1, LANES), 1)
            lo = None
            for h in range(N_KV):
                lse = jnp.concatenate([lse_ref[0, GROUP * h + g:GROUP * h + g + 1, :] for g in range(GROUP)], axis=1)
                qs, kk, vv, p, _, p_s, lo, _ = _attn_scores(cfg, t - nC, q_t, k_ref, v_ref, sink_ref, h, loc, window and loc, sink, lse=lse)
                dos = jnp.concatenate([do_t[:, (GROUP * h + g) * HEAD:(GROUP * h + g + 1) * HEAD] for g in range(GROUP)], axis=0)
                dp = _dot_nt(vv, dos)
                delta = jnp.sum(p * dp, axis=0, keepdims=True)
                ds = (p * (dp - delta)).astype(BF16)
                dq = _dot_tn(kk, ds).T
                dks.append(_dot(ds, qs))
                dvs.append(_dot(p.astype(BF16), dos))
                if sink:
                    dsk = -p_s * delta
                    for g in range(GROUP):
                        tot = jnp.sum(dsk[:, g * tq:(g + 1) * tq], axis=1, keepdims=True)
                        dsink_row = dsink_row + jnp.where(lane == GROUP * h + g, tot, 0.0)
                for g in range(GROUP):
                    dqs[GROUP * h + g] = dq[g * tq:(g + 1) * tq] * QSCALE
            dq_ref[0] = jnp.concatenate(dqs, axis=-1)
            dk = jnp.concatenate(dks, axis=-1)
            dv = jnp.concatenate(dvs, axis=-1)
            if loc and not window:
                dk_ref[0] += dk
                dv_ref[0] += dv
            else:
                dk_ref[0, 0:N, :] += dk[0:N]
                dv_ref[0, 0:N, :] += dv[0:N]
                if loc:
                    W = tq + 2 * WINDOW
                    dk_ref[0, pl.ds(N + lo, W), :] += dk[N:]
                    dv_ref[0, pl.ds(N + lo, W), :] += dv[N:]
            if sink:
                dsink_ref[0:1, :] += dsink_row

        pl.when(t >= nC)(functools.partial(run, True))
        if ctx_attend:
            pl.when(t < nC)(functools.partial(run, False))
        else:
            @pl.when(t < nC)
            def _():
                dq_ref[0] = jnp.zeros((tq, QW), F32)

    kv = pl.BlockSpec((1, T, KVW), lambda b, t: (b, 0, 0))
    qrow = pl.BlockSpec((1, tq, QW), lambda b, t: (b, t, 0))
    return _pcall(exch)(
        body, name=name, grid=(B, cfg.nT),
        in_specs=[qrow, kv, pl.BlockSpec((1, T, KVW), lambda b, t: (b, 0, vblock)),
                  pl.BlockSpec((1, 8, LANES), lambda b, t: (l, 0, 0)), qrow, pl.BlockSpec((1, 8, tq), lambda b, t: (b, 0, t))],
        out_specs=[qrow, kv, kv, _full((8, LANES))],
        out_shape=[jax.ShapeDtypeStruct((B, T, QW), F32), jax.ShapeDtypeStruct((B, T, KVW), F32),
                   jax.ShapeDtypeStruct((B, T, KVW), F32), jax.ShapeDtypeStruct((8, LANES), F32)],
        compiler_params=_params(2),
    )(q, k, z, sink8, do, lse)


def _qknorm_bwd(cfg, l, z, gvec, cosf, sins, dqa, dka, dva, dqc, dkc, dvc):
    B, T, tm = cfg.B, cfg.T, cfg.tm

    def body(z_ref, g_ref, cos_ref, sin_ref, dqa_ref, dka_ref, dva_ref, dqc_ref, dkc_ref, dvc_ref, dz_ref, dg_ref):
        _acc_init([dg_ref])
        ind = _head_indicator()
        cos, sin = cos_ref[...], sin_ref[...]
        dqa_t, dqc_t = dqa_ref[0], dqc_ref[0]
        douts = {0: dqa_t[:, 0:128], 1: dqa_t[:, 128:256], 2: dqa_t[:, 256:384], 3: dka_ref[0],
                 5: dqc_t[:, 0:128], 6: dqc_t[:, 128:256], 7: dqc_t[:, 256:384], 8: dkc_ref[0]}
        pieces = []
        dgs = []
        for c in range(QKV_W // LANES):
            if c not in douts:
                pieces.append(dva_ref[0] if c == 4 else dvc_ref[0])
                dgs.append(jnp.zeros((1, LANES), F32))
                continue
            x = z_ref[0, :, c * LANES:(c + 1) * LANES].astype(F32)
            g = g_ref[0, :, c * LANES:(c + 1) * LANES]
            ss = _head_sum(x * x, ind)
            rstd = lax.rsqrt(ss * (1.0 / HEAD) + EPS)
            n = x * rstd
            dout = douts[c]
            dy = dout * cos + _pair_swap(dout * sin)
            dgs.append(jnp.sum(dy * n, axis=0, keepdims=True))
            dn = dy * g
            mean = _head_sum(dn * n, ind) * (1.0 / HEAD)
            pieces.append(rstd * (dn - n * mean))
        dz_ref[0] = jnp.concatenate(pieces, axis=-1).astype(BF16)
        dg_ref[0:1, :] += jnp.concatenate(dgs, axis=-1)

    row = lambda w: pl.BlockSpec((1, tm, w), lambda b, t: (b, t, 0))
    tab = pl.BlockSpec((tm, LANES), lambda b, t: (t, 0))
    return pl.pallas_call(
        body, name="qknorm_bwd", grid=(B, cfg.nT),
        in_specs=[row(QKV_W), _layer(l, QKV_W), tab, tab, row(QW), row(KVW), row(KVW), row(QW), row(KVW), row(KVW)],
        out_specs=[row(QKV_W), _full((8, QKV_W))],
        out_shape=[jax.ShapeDtypeStruct((B, T, QKV_W), BF16), jax.ShapeDtypeStruct((8, QKV_W), F32)],
        compiler_params=_params(2),
    )(z, gvec, cosf, sins, dqa, dka, dva, dqc, dkc, dvc)


def _pool_bwd(cfg, l, dob, pooled, wp, ps):
    B, T, tm, kw = cfg.B, cfg.T, cfg.tm, cfg.kw

    def body(dob_ref, pooled_ref, wp_ref, ps_ref, du_ref, dwp_ref, dps_ref):
        t = pl.program_id(1)
        _acc_init([dwp_ref, dps_ref])
        r0, seg_lo, seg_hi, k0 = _pool_geometry(cfg, t)
        ps = ps_ref[0]
        wp = wp_ref[0]
        dmix = dob_ref[0, pl.ds(r0, tm), :].astype(F32)
        pooled = pooled_ref[0]
        dps_ref[0:1, :] += jnp.sum(dmix * _dot(pooled, wp), axis=0, keepdims=True)
        dpm = (dmix * ps).astype(BF16)
        dwp_ref[...] += _dot_tn(pooled, dpm)
        dpooled_t = _dot_nt(dpm, wp)
        dpm_w = (dob_ref[0, pl.ds(k0, kw), :].astype(F32) * ps).astype(BF16)
        dpooled_w = _dot_nt(dpm_w, wp)
        rr = r0 + lax.broadcasted_iota(jnp.int32, (tm, kw), 0)
        cc = k0 + lax.broadcasted_iota(jnp.int32, (tm, kw), 1)
        diff = rr - cc
        inseg = (cc >= seg_lo) & (cc < seg_hi)
        ccol = k0 + lax.broadcasted_iota(jnp.int32, (kw, 1), 0)
        group = lax.broadcasted_iota(jnp.int32, (tm, POOL_W), 1) // HEAD
        acc = jnp.zeros((tm, POOL_W), F32)
        for gi, w in enumerate(POOL_WINDOWS):
            h = w // 2
            band_t = jnp.where((diff >= -h) & (diff <= h - 1) & inseg, 1.0, 0.0).astype(BF16)
            hi, lo = _split_bf16(dpooled_w / _pool_count(ccol, h, seg_lo, seg_hi))
            acc = jnp.where(group == gi, _dot(band_t, hi) + _dot(band_t, lo), acc)
        du_ref[0] = (acc - dpooled_t).astype(BF16)

    row = pl.BlockSpec((1, tm, POOL_W), lambda b, t: (b, t, 0))
    return pl.pallas_call(
        body, name="pool_bwd", grid=(B, cfg.nT),
        in_specs=[pl.BlockSpec((1, T, POOL_W), lambda b, t: (b, 0, 0)), row,
                  pl.BlockSpec((1, POOL_W, POOL_W), lambda b, t: (l, 0, 0)), _layer(l, POOL_W)],
        out_specs=[row, _full((POOL_W, POOL_W)), _full((8, POOL_W))],
        out_shape=[jax.ShapeDtypeStruct((B, T, POOL_W), BF16), jax.ShapeDtypeStruct((POOL_W, POOL_W), F32),
                   jax.ShapeDtypeStruct((8, POOL_W), F32)],
        compiler_params=_params(2),
    )(dob, pooled, wp, ps)


def _in_proj_bwd(cfg, l, dzq, du, dzg, w_in, x, dx1, gn, mod, *, latent_only, exch=None):
    B, S, T, D, IN, tm, nC = cfg.B, cfg.S, cfg.T, cfg.D, cfg.IN, cfg.tm, cfg.nC

    def body(dzq_ref, du_ref, dzg_ref, w_ref, x_ref, dx1_ref, gn_ref, mod_ref, dx0_ref, dz_ref, dmod_ref, dgn_ref):
        b, t = pl.program_id(0), pl.program_id(1)
        _acc_init([dmod_ref, dgn_ref])
        row = jnp.where(t < nC, B, b)
        dz = jnp.concatenate([dzq_ref[0], du_ref[0], dzg_ref[0]], axis=-1)
        dz_ref[0] = dz
        dh = _dot_nt(dz, w_ref[...])
        gn = gn_ref[0]
        scale = _mod_row(mod_ref, row, 1, D)
        _, xhat, rstd = _modulate(x_ref[0], gn, _mod_row(mod_ref, row, 0, D), scale)
        dxn, d_shift, d_scale, d_gn = _modulate_bwd(dh, xhat, rstd, gn, scale)
        dx0_ref[0] = dx1_ref[0] + dxn
        dmod_ref[pl.ds(row, 1), :] += jnp.concatenate([d_shift, d_scale], axis=-1)
        dgn_ref[0:1, :] += d_gn

    row = lambda w: pl.BlockSpec((1, tm, w), lambda b, t: (b, t, 0))
    if latent_only:
        dx0_spec = pl.BlockSpec((1, tm, D), lambda b, t: (b, jnp.maximum(t - nC, 0), 0))
        dx0_shape = jax.ShapeDtypeStruct((B, S, D), F32)
    else:
        dx0_spec, dx0_shape = row(D), jax.ShapeDtypeStruct((B, T, D), F32)
    return _pcall(exch)(
        body, name="in_proj_bwd", grid=(B, cfg.nT),
        in_specs=[row(QKV_W), row(POOL_W), row(3 * D), _full((D, IN)), row(D), row(D), _layer(l, D), _full((8, 6 * D))],
        out_specs=[dx0_spec, row(IN), _full((8, 2 * D)), _full((8, D))],
        out_shape=[dx0_shape, jax.ShapeDtypeStruct((B, T, IN), BF16),
                   jax.ShapeDtypeStruct((8, 2 * D), F32), jax.ShapeDtypeStruct((8, D), F32)],
        compiler_params=_params(2),
    )(dzq, du, dzg, w_in, x, dx1, gn, mod)


def _adaln_bwd(cfg, l, cc_all, dm_all, w_ada):
    d, B = cfg.D, cfg.B
    wa = w_ada.shape[2]

    def body(c_ref, dm_ref, w_ref, dw_ref, dc_ref):
        c = c_ref[...]
        s = jax.nn.sigmoid(c)
        dmb = dm_ref[...].astype(BF16)
        dw_ref[...] = _dot_tn((c * s).astype(BF16), dmb)
        dc = _dot_nt(dmb, w_ref[0].astype(BF16)) * (s * (1.0 + c * (1.0 - s)))
        is_ctx = lax.broadcasted_iota(jnp.int32, (8 * N_DEV, 1), 0) % 8 == B
        dc_ref[...] = jnp.broadcast_to(jnp.sum(jnp.where(is_ctx, dc, 0.0), axis=0, keepdims=True), (8, d))

    return pl.pallas_call(
        body, name="adaln_bwd", grid=(1,),
        in_specs=[_full((8 * N_DEV, d)), _full((8 * N_DEV, wa)), pl.BlockSpec((1, d, wa), lambda *_: (l, 0, 0))],
        out_specs=[_full((d, wa)), _full((8, d))],
        out_shape=[jax.ShapeDtypeStruct((d, wa), F32), jax.ShapeDtypeStruct((8, d), F32)],
        compiler_params=_params(1),
    )(cc_all, dm_all, w_ada)


def _dmod_pack(cfg, dmod_in, dg1, dmod_mlp):
    d = cfg.D
    wa = 6 * d // N_DEV

    def body(din_ref, dg1_ref, dmlp_ref, o_ref, db_ref):
        dm = jnp.concatenate([din_ref[...], dg1_ref[...], dmlp_ref[...]], axis=-1)
        for j in range(N_DEV):
            o_ref[j] = dm[:, j * wa:(j + 1) * wa]
        db_ref[...] = jnp.broadcast_to(jnp.sum(dm, axis=0, keepdims=True), (8, 6 * d))

    return pl.pallas_call(
        body, name="dmod_pack", grid=(1,),
        in_specs=[_full((8, 2 * d)), _full((8, d)), _full((8, 3 * d))],
        out_specs=[_full((N_DEV, 8, wa)), _full((8, 6 * d))],
        out_shape=[jax.ShapeDtypeStruct((N_DEV, 8, wa), F32), jax.ShapeDtypeStruct((8, 6 * d), F32)],
        compiler_params=_params(1),
    )(dmod_in, dg1, dmod_mlp)


def _adam_update(g, w, m, v):
    bc1 = 1.0 - ADAM_B1 ** ADAM_STEP
    bc2 = 1.0 - ADAM_B2 ** ADAM_STEP
    m2 = ADAM_B1 * m + (1.0 - ADAM_B1) * g
    v2 = ADAM_B2 * v + (1.0 - ADAM_B2) * (g * g)
    delta = -ADAM_LR * ((m2 / bc1) / (jnp.sqrt(v2 / bc2) + ADAM_EPS) + ADAM_WD * w)
    return delta, m2, v2


def _sum_parts(p_ref):
    g = p_ref[0].astype(F32)
    for d in range(1, p_ref.shape[0]):
        g = g + p_ref[d].astype(F32)
    return g


def _adamw_sharded(parts, w, m, v, name):
    L, K, W = w.shape
    min_rows = min(c.shape[1] for chunks in parts for c in chunks)
    tk = next(c for c in (256, 128, 64, 32, 16, 8) if K % c == 0 and min_rows % c == 0)
    spans, flat = [], []
    for li, chunks in enumerate(parts):
        row = 0
        for c in chunks:
            assert c.shape[1] % tk == 0
            spans.append((li, row // tk, (row + c.shape[1]) // tk))
            flat.append(c)
            row += c.shape[1]
        assert row == K

    def body(*refs):
        p_refs = refs[:len(flat)]
        w_ref, m_ref, v_ref, g_ref, d_ref, m2_ref, v2_ref = refs[len(flat):]
        layer, i = pl.program_id(0), pl.program_id(1)

        def run(p_ref):
            g = _sum_parts(p_ref)
            delta, m2, v2 = _adam_update(g, w_ref[0], m_ref[0], v_ref[0])
            g_ref[0] = g
            d_ref[0] = delta
            m2_ref[0] = m2
            v2_ref[0] = v2

        for (li, lo, hi), p_ref in zip(spans, p_refs):
            pl.when((layer == li) & (i >= lo) & (i < hi))(functools.partial(run, p_ref))

    blk = pl.BlockSpec((1, tk, W), lambda l, i: (l, i, 0))

    def part_spec(span, arr):
        li, lo, hi = span
        return pl.BlockSpec((arr.shape[0], tk, W), lambda l, i: (0, jnp.where((l == li) & (i >= lo) & (i < hi), i - lo, 0), 0))

    return pl.pallas_call(
        body, name=name, grid=(L, K // tk),
        in_specs=[part_spec(sp, arr) for sp, arr in zip(spans, flat)] + [blk, blk, blk],
        out_specs=[blk] * 4, out_shape=[jax.ShapeDtypeStruct((L, K, W), F32)] * 4,
        compiler_params=_params(2),
    )(*flat, w, m, v)


def _adamw_packed(parts, w, m, v, name):
    rows = w.shape[0]
    tr = next(c for c in (256, 128, 64, 32, 16, 8) if rows % c == 0)

    def body(p_ref, w_ref, m_ref, v_ref, g_ref, d_ref, m2_ref, v2_ref):
        g = _sum_parts(p_ref)
        delta, m2, v2 = _adam_update(g, w_ref[...], m_ref[...], v_ref[...])
        g_ref[...] = g
        d_ref[...] = delta
        m2_ref[...] = m2
        v2_ref[...] = v2

    blk = pl.BlockSpec((tr, PACK_W), lambda i: (i, 0))
    return pl.pallas_call(
        body, name=name, grid=(rows // tr,),
        in_specs=[pl.BlockSpec((N_DEV, tr, PACK_W), lambda i: (0, i, 0)), blk, blk, blk],
        out_specs=[blk] * 4, out_shape=[jax.ShapeDtypeStruct((rows, PACK_W), F32)] * 4,
        compiler_params=_params(1),
    )(parts, w, m, v)


_SHARDED = dict(w_ada=True, w_in=True, w_br_a=True, w_br_b=True, w_br_c=True, w_out=False, w_mlp1=True, w_mlp2=False)
_MERGE_WEIGHTS = ("w_br_a", "w_br_b", "w_br_c", "w_out")
_GATHERED = ("w_in",) + _MERGE_WEIGHTS + ("w_mlp1", "w_mlp2")
_KEEP_SHARDS = ("w_mlp1",)
_SMALL = ("c_ctx", "b_ada", "norm1", "norm2", "q_norm_a", "k_norm_a", "q_norm_c", "k_norm_c", "sink_c", "w_pool", "pool_scale")


def _from_shards(name, g):
    n, k, w = g.shape
    if name in _KEEP_SHARDS:
        return g
    if _SHARDED[name]:
        return g.transpose(1, 0, 2).reshape(k, n * w)
    return g.reshape(n * k, w)


def _to_shards(name, g):
    if g.ndim == 3:
        return g
    if _SHARDED[name]:
        k, nw = g.shape
        return g.reshape(k, N_DEV, nw // N_DEV).transpose(1, 0, 2)
    nk, w = g.shape
    return g.reshape(N_DEV, nk // N_DEV, w)


def _pack_small(vals):
    flat = jnp.concatenate([vals[n].reshape(-1) for n in _SMALL])
    rows = -(-flat.shape[0] // (8 * PACK_W)) * 8
    return jnp.pad(flat, (0, rows * PACK_W - flat.shape[0])).reshape(rows, PACK_W)


def _unpack_small(packed, like):
    flat, out, r = packed.reshape(-1), {}, 0
    for n in _SMALL:
        sz = like[n].size
        out[n] = flat[r:r + sz].reshape(like[n].shape)
        r += sz
    return out


def _rope_tables(cfg):
    pos = jnp.arange(cfg.S, dtype=F32)
    r = jnp.floor(pos / GRID_W)
    col = pos - r * GRID_W
    inv = 1.0 / (ROPE_THETA ** (jnp.arange(0, HEAD // 2, 2, dtype=F32) / (HEAD // 2)))
    ang = jnp.concatenate([r[:, None] * inv, col[:, None] * inv], axis=-1)
    cos = jnp.repeat(jnp.cos(ang), 2, axis=-1)
    sin = jnp.repeat(jnp.sin(ang), 2, axis=-1) * jnp.tile(jnp.array([-1.0, 1.0], F32), HEAD // 2)
    cos = jnp.concatenate([jnp.ones((cfg.N, HEAD), F32), cos], axis=0)
    sin = jnp.concatenate([jnp.zeros((cfg.N, HEAD), F32), sin], axis=0)
    return jnp.tile(cos, (1, 2)), jnp.tile(sin, (1, 2))


def _gvec(qa, ka, qc, kc):
    one = jnp.ones((qa.shape[0], KVW), F32)
    t = lambda a, n: jnp.tile(a, (1, n))
    return jnp.concatenate([t(qa, N_QHEADS), t(ka, N_KV), one, t(qc, N_QHEADS), t(kc, N_KV), one], axis=-1)[:, None, :]


def _block_diag(wp):
    L, g, c, _ = wp.shape
    eye = jnp.eye(g, dtype=wp.dtype)
    return (wp[:, :, :, None, :] * eye[None, :, None, :, None]).reshape(L, g * c, g * c)


def _pad8(a):
    return jnp.pad(a, ((0, 8 - a.shape[0]), (0, 0)))


def kernel(x, c, ctx, c_ctx, w_ada, b_ada, norm1, norm2, w_in, q_norm_a, k_norm_a, q_norm_c, k_norm_c, sink_c, w_pool, pool_scale, w_br_a, w_br_b, w_br_c, w_out, w_mlp1, w_mlp2, loss_target, m_c_ctx, m_w_ada, m_b_ada, m_norm1, m_norm2, m_w_in, m_q_norm_a, m_k_norm_a, m_q_norm_c, m_k_norm_c, m_sink_c, m_w_pool, m_pool_scale, m_w_br_a, m_w_br_b, m_w_br_c, m_w_out, m_w_mlp1, m_w_mlp2, v_c_ctx, v_w_ada, v_b_ada, v_norm1, v_norm2, v_w_in, v_q_norm_a, v_k_norm_a, v_q_norm_c, v_k_norm_c, v_sink_c, v_w_pool, v_pool_scale, v_w_br_a, v_w_br_b, v_w_br_c, v_w_out, v_w_mlp1, v_w_mlp2):
    B, S, D = x.shape
    N = ctx.shape[1]
    L = w_ada.shape[0]
    cfg = _Cfg(B, S, N, D)
    T = cfg.T
    weights = dict(c_ctx=c_ctx, w_ada=w_ada, b_ada=b_ada, norm1=norm1, norm2=norm2, w_in=w_in, q_norm_a=q_norm_a,
                   k_norm_a=k_norm_a, q_norm_c=q_norm_c, k_norm_c=k_norm_c, sink_c=sink_c, w_pool=w_pool,
                   pool_scale=pool_scale, w_br_a=w_br_a, w_br_b=w_br_b, w_br_c=w_br_c, w_out=w_out, w_mlp1=w_mlp1, w_mlp2=w_mlp2)
    mom_m = dict(c_ctx=m_c_ctx, w_ada=m_w_ada, b_ada=m_b_ada, norm1=m_norm1, norm2=m_norm2, w_in=m_w_in, q_norm_a=m_q_norm_a,
                 k_norm_a=m_k_norm_a, q_norm_c=m_q_norm_c, k_norm_c=m_k_norm_c, sink_c=m_sink_c, w_pool=m_w_pool,
                 pool_scale=m_pool_scale, w_br_a=m_w_br_a, w_br_b=m_w_br_b, w_br_c=m_w_br_c, w_out=m_w_out, w_mlp1=m_w_mlp1, w_mlp2=m_w_mlp2)
    mom_v = dict(c_ctx=v_c_ctx, w_ada=v_w_ada, b_ada=v_b_ada, norm1=v_norm1, norm2=v_norm2, w_in=v_w_in, q_norm_a=v_q_norm_a,
                 k_norm_a=v_k_norm_a, q_norm_c=v_q_norm_c, k_norm_c=v_k_norm_c, sink_c=v_sink_c, w_pool=v_w_pool,
                 pool_scale=v_pool_scale, w_br_a=v_w_br_a, w_br_b=v_w_br_b, w_br_c=v_w_br_c, w_out=v_w_out, w_mlp1=v_w_mlp1, w_mlp2=v_w_mlp2)

    shards_bf16 = {n: weights[n].astype(BF16) for n in _GATHERED}
    full = [dict() for _ in range(L)]

    def gather_of(items):
        return _Exchange([(shards_bf16[n], l) for l, n in items], scatter=False)

    def gathered(items, arrs):
        for (l, n), a in zip(items, arrs):
            full[l][n] = _from_shards(n, a)

    gathered([(0, "w_in")], [_gather_two_level(shards_bf16["w_in"], 0, "gather_first_weights")])

    def hosting(fn, *a, exch=None, done=None, **kw):
        if exch is None:
            return fn(*a, **kw)
        res = fn(*a, exch=exch, **kw)
        done(res[-exch.n:])
        own = res[:-exch.n]
        return own[0] if len(own) == 1 else own

    def gather_behind(l, names):
        if l >= L:
            return {}
        items = [(l, n) for n in names]
        return dict(exch=gather_of(items), done=functools.partial(gathered, items))

    cosf, sins = _rope_tables(cfg)
    xs = jnp.concatenate([ctx, x], axis=1)
    cc8 = _pad8(jnp.concatenate([c, c_ctx[None, :]], axis=0))
    va_blk, vc_blk = (QW + KVW) // KVW, (2 * QW + 3 * KVW) // KVW
    per_layer = lambda a: a[:, None, :]
    b_ada3, norm1_3, norm2_3, ps3 = per_layer(b_ada), per_layer(norm1), per_layer(norm2), per_layer(pool_scale)
    gvec = _gvec(q_norm_a, k_norm_a, q_norm_c, k_norm_c)
    sink8 = jnp.pad(sink_c[:, None, :], ((0, 0), (0, 7), (0, LANES - N_QHEADS)))
    wp = _block_diag(w_pool).astype(BF16)

    cc_all = _Exchange([cc8], scatter=False).alone("gather_cond")[0].reshape(8 * N_DEV, D)
    mod_cols = _Exchange([_adaln_fwd(cfg, cc_all, w_ada)], scatter=True).alone("scatter_mod")[0]
    mod_all = _adaln_join(cfg, mod_cols, b_ada3)

    saved = []
    for l in range(L):
        fw = full[l]
        ctx_active = l < L - 1
        mod = mod_all[l]
        z, u, h = hosting(_in_proj_fwd, cfg, l, xs, norm1_3, mod, fw["w_in"], **gather_behind(l, _MERGE_WEIGHTS if l == 0 else ("w_mlp2",)))
        qa, ka, qc, kc = _qknorm_fwd(cfg, l, z, gvec, cosf, sins)
        oa, lse_a = hosting(_attn_fwd, cfg, l, qa, ka, z, va_blk, sink8, window=False, sink=False, ctx_attend=ctx_active, name="attn_a_fwd",
                            **gather_behind(l, ("w_mlp1", "w_mlp2") if l == 0 else ("w_mlp1",)))
        oc, lse_c = hosting(_attn_fwd, cfg, l, qc, kc, z, vc_blk, sink8, window=True, sink=True, ctx_attend=ctx_active, name="attn_c_fwd",
                            **(gather_behind(l + 1, _MERGE_WEIGHTS)))
        ob, pooled = _pool_fwd(cfg, l, u, wp, ps3)
        x1, mgo = hosting(_merge_fwd, cfg, xs, oa, ob, oc, z, mod, fw["w_br_a"], fw["w_br_b"], fw["w_br_c"], fw["w_out"],
                          ctx_active=ctx_active)
        if l < L - 1:
            x2, mo, r = hosting(_mlp_fwd, cfg, l, x1, norm2_3, mod, fw["w_mlp1"], fw["w_mlp2"], ctx_active=ctx_active,
                                **gather_behind(l + 1, ("w_in",)))
        else:
            x2, mo, r, sse = _mlp_fwd(cfg, l, x1, norm2_3, mod, fw["w_mlp1"], fw["w_mlp2"], ctx_active=ctx_active, target=loss_target)
        saved.append(dict(xs=xs, mod=mod, z=z, h=h, qa=qa, ka=ka, qc=qc, kc=kc, oa=oa, oc=oc, ob=ob, pooled=pooled, x1=x1, mgo=mgo, mo=mo,
                          lse_a=lse_a, lse_c=lse_c, r=r))
        xs = x2

    dxs = xs
    loss = lax.psum(0.5 * sse[0, 0] / D, ("x", "y", "c"))

    grads = [dict() for _ in range(L)]
    parts = {}
    small = {n: [None] * L for n in _SMALL if n != "c_ctx"}
    d_c_ctx = jnp.zeros((D,), F32)
    flat2 = lambda a: a.reshape(B * T, a.shape[-1])

    def scatter_of(l, names):
        return _Exchange([_to_shards(n, grads[l][n]) for n in names], scatter=True)

    def scattered(l, names, arrs):
        for n, a in zip(names, arrs):
            parts[(l, n)] = a

    def scatter_behind(l, names):
        if l >= L:
            return {}
        return dict(exch=scatter_of(l, names), done=functools.partial(scattered, l, names))

    for l in reversed(range(L)):
        fw, sv, g = full[l], saved[l], grads[l]
        ctx_active = l < L - 1
        mod = sv["mod"]
        dx1, h2, da, dout, dmod_mlp, dgn2 = hosting(_mlp_bwd, cfg, l, sv["x1"], dxs, sv["mo"], sv["r"], norm2_3, mod, fw["w_mlp1"], fw["w_mlp2"],
                                                    ctx_active=ctx_active, **scatter_behind(l + 1, ("w_in",)))
        g["w_mlp1"] = _matmul_tn(flat2(h2), flat2(da), "dw_mlp1", by_shard=True)
        g["w_mlp2"] = _matmul_tn(flat2(sv["r"]), flat2(dout), "dw_mlp2", by_shard=False)
        doa, dob, doc, dpa, dpb, dpc, y, dmo, dzg, dg1 = _merge_bwd(
            cfg, dx1, sv["mgo"], sv["oa"], sv["ob"], sv["oc"], sv["z"], mod, fw["w_br_a"], fw["w_br_b"], fw["w_br_c"], fw["w_out"],
            ctx_active=ctx_active)
        g["w_out"] = _matmul_tn(flat2(y), flat2(dmo), "dw_out", by_shard=False)
        g["w_br_a"] = _matmul_tn(flat2(sv["oa"]), flat2(dpa), "dw_br_a", by_shard=True)
        g["w_br_b"] = _matmul_tn(flat2(sv["ob"]), flat2(dpb), "dw_br_b", by_shard=True)
        g["w_br_c"] = _matmul_tn(flat2(sv["oc"]), flat2(dpc), "dw_br_c", by_shard=True)
        z = sv["z"]
        dqa, dka, dva, _ = hosting(_attn_bwd, cfg, l, sv["qa"], sv["ka"], z, va_blk, sink8, doa, sv["lse_a"], window=False, sink=False,
                                   ctx_attend=ctx_active, name="attn_a_bwd", **scatter_behind(l, ("w_mlp1", "w_mlp2")))
        dqc, dkc, dvc, dsink = hosting(_attn_bwd, cfg, l, sv["qc"], sv["kc"], z, vc_blk, sink8, doc, sv["lse_c"], window=True, sink=True,
                                       ctx_attend=ctx_active, name="attn_c_bwd", **scatter_behind(l, _MERGE_WEIGHTS))
        dzq, dgvec = _qknorm_bwd(cfg, l, z, gvec, cosf, sins, dqa, dka, dva, dqc, dkc, dvc)
        du, dwp, dps = _pool_bwd(cfg, l, dob, sv["pooled"], wp, ps3)
        dxs, dz, dmod_in, dgn1 = _in_proj_bwd(cfg, l, dzq, du, dzg, fw["w_in"], sv["xs"], dx1, norm1_3, mod, latent_only=(l == 0))
        dmod_cols, dbias = _dmod_pack(cfg, dmod_in, dg1, dmod_mlp)
        dmod_exchange = _Exchange([dmod_cols], scatter=True)
        half = D // 2
        if l > 0:
            g["w_in"], dm_all = _matmul_tn(flat2(sv["h"]), flat2(dz), "dw_in", by_shard=True, exch=dmod_exchange)
        else:
            g_lo, dm_all = _matmul_tn(flat2(sv["h"]), flat2(dz), "dw_in_lo", by_shard=True, a_cols=(0, half), exch=dmod_exchange)
        g["w_ada"], dcc = _adaln_bwd(cfg, l, cc_all, dm_all.reshape(8 * N_DEV, -1), w_ada)
        d_c_ctx = d_c_ctx + dcc[0]
        gv = dgvec[0]
        heads = lambda v, n: v.reshape(n, HEAD).sum(axis=0)
        small["b_ada"][l] = dbias[0]
        small["norm1"][l] = dgn1[0]
        small["norm2"][l] = dgn2[0]
        small["q_norm_a"][l] = heads(gv[0:QW], N_QHEADS)
        small["k_norm_a"][l] = heads(gv[QW:QW + KVW], N_KV)
        small["q_norm_c"][l] = heads(gv[QW + 2 * KVW:2 * QW + 2 * KVW], N_QHEADS)
        small["k_norm_c"][l] = heads(gv[2 * QW + 2 * KVW:2 * QW + 3 * KVW], N_KV)
        small["sink_c"][l] = dsink[0, :N_QHEADS]
        small["w_pool"][l] = jnp.stack([dwp[i * HEAD:(i + 1) * HEAD, i * HEAD:(i + 1) * HEAD] for i in range(len(POOL_WINDOWS))])
        small["pool_scale"][l] = dps[0]
    grad_x = dxs

    small_vals = {n: jnp.stack(v) for n, v in small.items()}
    small_vals["c_ctx"] = d_c_ctx
    small_packed = _pack_small(small_vals)
    g_hi, parts_lo, small_parts = _matmul_tn(
        flat2(saved[0]["h"]), flat2(dz), "dw_in_hi", by_shard=True, a_cols=(half, half),
        exch=_Exchange([g_lo, jnp.broadcast_to(small_packed[None], (N_DEV,) + small_packed.shape)], scatter=True))
    parts_hi = _Exchange([g_hi], scatter=True).alone("scatter_last_grads")[0]
    chunks = {(l, n): [parts[(l, n)]] for l in range(L) for n in _GATHERED if (l, n) in parts}
    chunks[(0, "w_in")] = [parts_lo, parts_hi]
    for l in range(L):
        chunks[(l, "w_ada")] = [grads[l]["w_ada"][None]]
    stepped = {n: _adamw_sharded([chunks[(l, n)] for l in range(L)], weights[n], mom_m[n], mom_v[n], "adamw_" + n) for n in _SHARDED}
    stepped_small = _adamw_packed(small_parts, _pack_small(weights), _pack_small(mom_m), _pack_small(mom_v), "adamw_small")

    outs = []
    for i in range(4):
        res = {n: stepped[n][i] for n in _SHARDED}
        res.update(_unpack_small(stepped_small[i], weights))
        outs.append(res)
    order = ("c_ctx", "w_ada", "b_ada", "norm1", "norm2", "w_in", "q_norm_a", "k_norm_a", "q_norm_c", "k_norm_c", "sink_c",
             "w_pool", "pool_scale", "w_br_a", "w_br_b", "w_br_c", "w_out", "w_mlp1", "w_mlp2")
    return (loss, grad_x, *[res[n] for res in outs for n in order])
```

```python
import functools

import jax
import jax.numpy as jnp
from jax import lax
from jax.experimental import pallas as pl
from jax.experimental.pallas import tpu as pltpu

F32 = jnp.float32
BF16 = jnp.bfloat16

N_DEV = 8
HEAD = 64
N_QHEADS = 6
N_KV = 2
GROUP = 3
QW = N_QHEADS * HEAD
KVW = N_KV * HEAD
QKV_W = 2 * (QW + 2 * KVW)
POOL_W = 256
POOL_WINDOWS = (2, 4, 8, 16)
POOL_HALO = 16
GATE0 = QKV_W + POOL_W
WINDOW = 128
GRID_W = 64
ROPE_THETA = 10000.0
EPS = 1e-6
NEG = -1e30
QSCALE = HEAD ** -0.5
LANES = 128
PACK_W = 1024
VMEM_LIMIT = 56 * 1024 * 1024

ADAM_LR = 0.001
ADAM_B1 = 0.9
ADAM_B2 = 0.999
ADAM_EPS = 1e-08
ADAM_WD = 0.01
ADAM_STEP = 10

NT_DIMS = (((1,), (1,)), ((), ()))
TN_DIMS = (((0,), (0,)), ((), ()))


def _dot(a, b):
    return jnp.dot(a, b, preferred_element_type=F32)


def _dot_nt(a, b):
    return lax.dot_general(a, b, NT_DIMS, preferred_element_type=F32)


def _dot_tn(a, b):
    return lax.dot_general(a, b, TN_DIMS, preferred_element_type=F32)


def _params(n_grid):
    return pltpu.CompilerParams(dimension_semantics=("arbitrary",) * n_grid, vmem_limit_bytes=VMEM_LIMIT)


def _full(shape):
    nd = len(shape)
    return pl.BlockSpec(shape, lambda *_: (0,) * nd)


def _layer(l, width):
    return pl.BlockSpec((1, 1, width), lambda *_: (l, 0, 0))


def _modulate(x, gn, shift, scale):
    rstd = lax.rsqrt(jnp.mean(x * x, axis=-1, keepdims=True) + EPS)
    xhat = x * rstd
    return xhat * gn * (1.0 + scale) + shift, xhat, rstd


def _modulate_bwd(dh, xhat, rstd, gn, scale):
    d_shift = jnp.sum(dh, axis=0, keepdims=True)
    d_scale = jnp.sum(dh * xhat * gn, axis=0, keepdims=True)
    dy = dh * (1.0 + scale)
    d_gn = jnp.sum(dy * xhat, axis=0, keepdims=True)
    dxh = dy * gn
    dx = rstd * (dxh - xhat * jnp.mean(dxh * xhat, axis=-1, keepdims=True))
    return dx, d_shift, d_scale, d_gn


def _mod_row(mod_ref, row, k, d):
    return mod_ref[pl.ds(row, 1), k * d:(k + 1) * d]


class _Cfg:
    def __init__(self, b, s, n, d):
        self.B, self.S, self.N, self.D = b, s, n, d
        self.T = n + s
        self.F = 4 * d
        self.IN = GATE0 + 3 * d
        self.tm = 256 if (n % 256 == 0 and s % 256 == 0) else 128
        self.nT = self.T // self.tm
        self.nC = n // self.tm
        self.gw = 512 if d % 512 == 0 else 256
        self.kw = self.tm + 2 * POOL_HALO
        assert GATE0 % self.gw == 0 and d % self.gw == 0 and b < 8 and self.T >= self.kw and max(POOL_WINDOWS) // 2 <= POOL_HALO
        assert s % GRID_W == 0 and n % self.tm == 0 and s % self.tm == 0 and s >= self.tm + 2 * WINDOW
        assert d % (N_DEV * LANES) == 0


def _peer(k):
    x, y, c = lax.axis_index("x"), lax.axis_index("y"), lax.axis_index("c")
    px = x ^ ((k >> 2) & 1)
    py = y ^ ((k >> 1) & 1)
    pc = c ^ (k & 1)
    return (px, py, pc), 4 * px + 2 * py + pc


class _Exchange:
    def __init__(self, arrays, scatter):
        self.arrays = [a if isinstance(a, tuple) else (a, None) for a in arrays]
        self.scatter = scatter
        self.n = len(self.arrays)

    def operands(self):
        return [a for a, _ in self.arrays]

    def out_shapes(self):
        res = []
        for a, layer in self.arrays:
            shape = a.shape[1:] if (self.scatter or layer is not None) else a.shape
            res.append(jax.ShapeDtypeStruct((N_DEV,) + tuple(shape), a.dtype))
        return res

    def scratch(self):
        n = self.n * (N_DEV - 1)
        return [pltpu.SemaphoreType.DMA((n,)), pltpu.SemaphoreType.DMA((n,)), pltpu.SemaphoreType.DMA((self.n,))]

    def _copies(self, x_refs, out_refs, send_sems, recv_sems, local_sems, want):
        _, me = _peer(0)
        res = []
        for i, ((_, layer), x_ref, out_ref) in enumerate(zip(self.arrays, x_refs, out_refs)):
            if self.scatter:
                src_of = lambda d, x_ref=x_ref: x_ref.at[d]
            elif layer is not None:
                src_of = lambda d, x_ref=x_ref, layer=layer: x_ref.at[layer]
            else:
                src_of = lambda d, x_ref=x_ref: x_ref
            if want == "local":
                res.append(pltpu.make_async_copy(src_of(me), out_ref.at[me], local_sems.at[i]))
                continue
            for k in range(1, N_DEV):
                pos, idx = _peer(k)
                j = i * (N_DEV - 1) + k - 1
                common = dict(send_sem=send_sems.at[j], recv_sem=recv_sems.at[j], device_id=pos, device_id_type=pl.DeviceIdType.MESH)
                if want == "send":
                    res.append(pltpu.make_async_remote_copy(src_ref=src_of(idx), dst_ref=out_ref.at[me], **common))
                else:
                    res.append(pltpu.make_async_remote_copy(src_ref=src_of(me), dst_ref=out_ref.at[idx], **common))
        return res

    def start(self, *refs):
        for cp in self._copies(*refs, "local") + self._copies(*refs, "send"):
            cp.start()

    def wait(self, *refs):
        for cp in self._copies(*refs, "recv"):
            cp.wait_recv()
        for cp in self._copies(*refs, "send"):
            cp.wait_send()
        for cp in self._copies(*refs, "local"):
            cp.wait()

    def alone(self, name):
        n = self.n

        def body(*refs):
            args = (refs[:n], refs[n:2 * n], *refs[2 * n:])
            self.start(*args)
            self.wait(*args)

        any_spec = pl.BlockSpec(memory_space=pl.ANY)
        return pl.pallas_call(body, name=name, in_specs=[any_spec] * n, out_specs=[any_spec] * n,
                              out_shape=self.out_shapes(), scratch_shapes=self.scratch())(*self.operands())


def _gather_two_level(x, layer, name):
    shape = x.shape[1:]

    def body(x_ref, out_ref, send_sems, recv_sems, local_sem):
        mx, my, mc = lax.axis_index("x"), lax.axis_index("y"), lax.axis_index("c")
        me, sibling = (mx, my, mc), (mx, my, 1 - mc)
        chips = [(1 - mx, my), (mx, 1 - my), (1 - mx, 1 - my)]
        src = x_ref.at[layer]

        def slot(px, py, pc):
            return out_ref.at[4 * px + 2 * py + pc]

        def copy(k, block, to, from_src=False):
            return pltpu.make_async_remote_copy(src_ref=src if from_src else slot(*block), dst_ref=slot(*block), send_sem=send_sems.at[k],
                                                recv_sem=recv_sems.at[k], device_id=to, device_id_type=pl.DeviceIdType.MESH)

        mine = pltpu.make_async_copy(src, slot(*me), local_sem)
        mine.start()
        first = [copy(0, me, sibling, True)] + [copy(1 + j, me, (*chip, mc), True) for j, chip in enumerate(chips)]
        for cp in first:
            cp.start()
        passed = [copy(4 + j, (*chip, mc), sibling) for j, chip in enumerate(chips)]
        for j, chip in enumerate(chips):
            copy(1 + j, (*chip, mc), me).wait_recv()
            passed[j].start()
        copy(0, sibling, me).wait_recv()
        for j, chip in enumerate(chips):
            copy(4 + j, (*chip, 1 - mc), me).wait_recv()
        for cp in first + passed:
            cp.wait_send()
        mine.wait()

    any_spec = pl.BlockSpec(memory_space=pl.ANY)
    return pl.pallas_call(
        body, name=name, in_specs=[any_spec], out_specs=any_spec,
        out_shape=jax.ShapeDtypeStruct((N_DEV,) + tuple(shape), x.dtype),
        scratch_shapes=[pltpu.SemaphoreType.DMA((N_DEV - 1,)), pltpu.SemaphoreType.DMA((N_DEV - 1,)), pltpu.SemaphoreType.DMA],
    )(x)


def _pcall(exch):
    if exch is None:
        return pl.pallas_call

    def make(body, *, name, grid, in_specs, out_specs, out_shape, compiler_params, scratch_shapes=()):
        multi = isinstance(out_shape, (list, tuple))
        out_specs_l = list(out_specs) if multi else [out_specs]
        out_shape_l = list(out_shape) if multi else [out_shape]
        n_in, n_out, n_x, n_s = len(in_specs), len(out_specs_l), exch.n, len(scratch_shapes)

        def hosted(*refs):
            ins, x_refs = refs[:n_in], refs[n_in:n_in + n_x]
            o0 = n_in + n_x
            outs, xo_refs = refs[o0:o0 + n_out], refs[o0 + n_out:o0 + n_out + n_x]
            s0 = o0 + n_out + n_x
            own_scratch, sems = refs[s0:s0 + n_s], refs[s0 + n_s:]
            ids = [pl.program_id(i) for i in range(len(grid))]
            first = functools.reduce(jnp.logical_and, [i == 0 for i in ids])
            last = functools.reduce(jnp.logical_and, [i == g - 1 for i, g in zip(ids, grid)])

            @pl.when(first)
            def _():
                exch.start(x_refs, xo_refs, *sems)

            body(*ins, *outs, *own_scratch)

            @pl.when(last)
            def _():
                exch.wait(x_refs, xo_refs, *sems)

        any_spec = pl.BlockSpec(memory_space=pl.ANY)
        call = pl.pallas_call(
            hosted, name=name, grid=grid, in_specs=list(in_specs) + [any_spec] * n_x, out_specs=out_specs_l + [any_spec] * n_x,
            out_shape=out_shape_l + exch.out_shapes(), scratch_shapes=list(scratch_shapes) + exch.scratch(),
            compiler_params=compiler_params)
        return lambda *args: call(*args, *exch.operands())

    return make


def _adaln_fwd(cfg, cc_all, w_ada):
    d = cfg.D
    L, _, wa = w_ada.shape

    def body(c_ref, w_ref, o_ref):
        c = c_ref[...]
        a = (c * jax.nn.sigmoid(c)).astype(BF16)
        for l in range(L):
            m = _dot(a, w_ref[l].astype(BF16))
            for p in range(N_DEV):
                o_ref[p, l] = m[8 * p:8 * (p + 1)]

    return pl.pallas_call(
        body, name="adaln_fwd", grid=(1,),
        in_specs=[_full((8 * N_DEV, d)), _full((L, d, wa))],
        out_specs=_full((N_DEV, L, 8, wa)),
        out_shape=jax.ShapeDtypeStruct((N_DEV, L, 8, wa), F32), compiler_params=_params(1),
    )(cc_all, w_ada)


def _adaln_join(cfg, parts, b_ada):
    d = cfg.D
    _, L, _, wa = parts.shape

    def body(p_ref, b_ref, o_ref):
        for l in range(L):
            for j in range(N_DEV):
                o_ref[l, :, j * wa:(j + 1) * wa] = p_ref[j, l] + b_ref[l, :, j * wa:(j + 1) * wa]

    return pl.pallas_call(
        body, name="adaln_join", grid=(1,),
        in_specs=[_full((N_DEV, L, 8, wa)), _full((L, 1, 6 * d))],
        out_specs=_full((L, 8, 6 * d)),
        out_shape=jax.ShapeDtypeStruct((L, 8, 6 * d), F32), compiler_params=_params(1),
    )(parts, b_ada)


def _in_proj_fwd(cfg, l, x, gn, mod, w_in, exch=None):
    B, T, D, IN, tm, nC = cfg.B, cfg.T, cfg.D, cfg.IN, cfg.tm, cfg.nC

    def body(x_ref, gn_ref, mod_ref, w_ref, z_ref, u_ref, h_ref):
        b, t = pl.program_id(0), pl.program_id(1)
        row = jnp.where(t < nC, B, b)
        h, _, _ = _modulate(x_ref[0], gn_ref[0], _mod_row(mod_ref, row, 0, D), _mod_row(mod_ref, row, 1, D))
        hb = h.astype(BF16)
        h_ref[0] = hb
        z = _dot(hb, w_ref[...])
        z_ref[0] = z.astype(BF16)
        u_ref[0] = z[:, QKV_W:QKV_W + POOL_W]

    row = lambda w: pl.BlockSpec((1, tm, w), lambda b, t: (b, t, 0))
    return _pcall(exch)(
        body, name="in_proj_fwd", grid=(B, cfg.nT),
        in_specs=[row(D), _layer(l, D), _full((8, 6 * D)), _full((D, IN))],
        out_specs=[row(IN), row(POOL_W), row(D)],
        out_shape=[jax.ShapeDtypeStruct((B, T, IN), BF16), jax.ShapeDtypeStruct((B, T, POOL_W), F32), jax.ShapeDtypeStruct((B, T, D), BF16)],
        compiler_params=_params(2),
    )(x, gn, mod, w_in)


def _head_indicator():
    r = lax.broadcasted_iota(jnp.int32, (LANES, LANES), 0) // HEAD
    c = lax.broadcasted_iota(jnp.int32, (LANES, LANES), 1) // HEAD
    return jnp.where(r == c, 1.0, 0.0).astype(BF16)


def _head_sum(x, ind):
    hi = x.astype(BF16)
    lo = (x - hi.astype(F32)).astype(BF16)
    return _dot(hi, ind) + _dot(lo, ind)


def _pair_swap(y):
    lane = lax.broadcasted_iota(jnp.int32, y.shape, 1)
    return jnp.where(lane % 2 == 0, pltpu.roll(y, LANES - 1, 1), pltpu.roll(y, 1, 1))


_Q_CHUNKS = (0, 1, 2, 5, 6, 7)


def _qknorm_fwd(cfg, l, z, gvec, cosf, sins):
    B, T, tm = cfg.B, cfg.T, cfg.tm

    def body(z_ref, g_ref, cos_ref, sin_ref, qa_ref, ka_ref, qc_ref, kc_ref):
        ind = _head_indicator()
        cos, sin = cos_ref[...], sin_ref[...]

        def chunk(c):
            x = z_ref[0, :, c * LANES:(c + 1) * LANES].astype(F32)
            ss = _head_sum(x * x, ind)
            y = x * lax.rsqrt(ss * (1.0 / HEAD) + EPS) * g_ref[0, :, c * LANES:(c + 1) * LANES]
            out = y * cos + _pair_swap(y) * sin
            return (out * QSCALE if c in _Q_CHUNKS else out).astype(BF16)

        qa_ref[0] = jnp.concatenate([chunk(0), chunk(1), chunk(2)], axis=-1)
        ka_ref[0] = chunk(3)
        qc_ref[0] = jnp.concatenate([chunk(5), chunk(6), chunk(7)], axis=-1)
        kc_ref[0] = chunk(8)

    row = lambda w: pl.BlockSpec((1, tm, w), lambda b, t: (b, t, 0))
    tab = pl.BlockSpec((tm, LANES), lambda b, t: (t, 0))
    return pl.pallas_call(
        body, name="qknorm_fwd", grid=(B, cfg.nT),
        in_specs=[row(QKV_W), _layer(l, QKV_W), tab, tab],
        out_specs=[row(QW), row(KVW), row(QW), row(KVW)],
        out_shape=[jax.ShapeDtypeStruct((B, T, w), BF16) for w in (QW, KVW, QW, KVW)],
        compiler_params=_params(2),
    )(z, gvec, cosf, sins)


def _attn_scores(cfg, tl, q, k_ref, v_ref, sink_ref, h, loc, window, sink, lse=None):
    S, N, tq = cfg.S, cfg.N, cfg.tm
    hs = slice(h * HEAD, (h + 1) * HEAD)
    qs = jnp.concatenate([q[:, (GROUP * h + g) * HEAD:(GROUP * h + g + 1) * HEAD] for g in range(GROUP)], axis=0)
    lo = None
    if not loc:
        kk = k_ref[0, 0:N, :][:, hs]
        vv = v_ref[0, 0:N, :].astype(BF16)[:, hs]
    elif not window:
        kk = k_ref[0][:, hs]
        vv = v_ref[0].astype(BF16)[:, hs]
    else:
        W = tq + 2 * WINDOW
        lo = pl.multiple_of(jnp.clip(tl * tq - WINDOW, 0, S - W), LANES)
        kk = jnp.concatenate([k_ref[0, 0:N, :], k_ref[0, pl.ds(N + lo, W), :]], axis=0)[:, hs]
        vv = jnp.concatenate([v_ref[0, 0:N, :], v_ref[0, pl.ds(N + lo, W), :]], axis=0).astype(BF16)[:, hs]
    st = _dot_nt(kk, qs)
    if window:
        krow = lax.broadcasted_iota(jnp.int32, st.shape, 0)
        qpos = tl * tq + lax.broadcasted_iota(jnp.int32, st.shape, 1) % tq
        st = jnp.where((krow < N) | (jnp.abs(qpos - (lo + krow - N)) <= WINDOW), st, NEG)
    sk = None
    if sink:
        colg = lax.broadcasted_iota(jnp.int32, (1, GROUP * tq), 1) // tq
        sk = jnp.zeros((1, GROUP * tq), F32)
        for g in range(GROUP):
            j = GROUP * h + g
            sk = jnp.where(colg == g, sink_ref[0, 0:1, j:j + 1], sk)
    if lse is not None:
        return qs, kk, vv, jnp.exp(st - lse), None, (jnp.exp(sk - lse) if sink else None), lo, lse
    m = jnp.max(st, axis=0, keepdims=True)
    if sink:
        m = jnp.maximum(m, sk)
    e = jnp.exp(st - m)
    l = jnp.sum(e, axis=0, keepdims=True)
    e_s = None
    if sink:
        e_s = jnp.exp(sk - m)
        l = l + e_s
    return qs, kk, vv, e, 1.0 / l, e_s, lo, m + jnp.log(l)


def _attn_fwd(cfg, l, q, k, z, vblock, sink8, *, window, sink, ctx_attend, name, exch=None):
    B, T, tq, nC = cfg.B, cfg.T, cfg.tm, cfg.nC

    def body(q_ref, k_ref, v_ref, sink_ref, o_ref, lse_ref):
        t = pl.program_id(1)

        def run(loc):
            q_t = q_ref[0]
            outs = [None] * N_QHEADS
            lses = [None] * N_QHEADS
            for h in range(N_KV):
                _, _, vv, e, inv, _, _, lse = _attn_scores(cfg, t - nC, q_t, k_ref, v_ref, sink_ref, h, loc, window and loc, sink)
                o = (_dot_tn(vv, e.astype(BF16)) * inv).T
                for g in range(GROUP):
                    outs[GROUP * h + g] = o[g * tq:(g + 1) * tq]
                    lses[GROUP * h + g] = lse[:, g * tq:(g + 1) * tq]
            o_ref[0] = jnp.concatenate(outs, axis=-1).astype(BF16)
            lse_ref[0] = jnp.concatenate(lses + [jnp.zeros((8 - N_QHEADS, tq), F32)], axis=0)

        pl.when(t >= nC)(functools.partial(run, True))
        if ctx_attend:
            pl.when(t < nC)(functools.partial(run, False))
        else:
            @pl.when(t < nC)
            def _():
                o_ref[0] = jnp.zeros((tq, QW), BF16)
                lse_ref[0] = jnp.zeros((8, tq), F32)

    return _pcall(exch)(
        body, name=name, grid=(B, cfg.nT),
        in_specs=[pl.BlockSpec((1, tq, QW), lambda b, t: (b, t, 0)),
                  pl.BlockSpec((1, T, KVW), lambda b, t: (b, 0, 0)),
                  pl.BlockSpec((1, T, KVW), lambda b, t: (b, 0, vblock)),
                  pl.BlockSpec((1, 8, LANES), lambda b, t: (l, 0, 0))],
        out_specs=[pl.BlockSpec((1, tq, QW), lambda b, t: (b, t, 0)), pl.BlockSpec((1, 8, tq), lambda b, t: (b, 0, t))],
        out_shape=[jax.ShapeDtypeStruct((B, T, QW), BF16), jax.ShapeDtypeStruct((B, 8, T), F32)], compiler_params=_params(2),
    )(q, k, z, sink8)


def _pool_geometry(cfg, t):
    tm, N, T, nC = cfg.tm, cfg.N, cfg.T, cfg.nC
    r0 = pl.multiple_of(t * tm, tm)
    isctx = t < nC
    seg_lo = jnp.where(isctx, 0, N)
    seg_hi = jnp.where(isctx, N, T)
    k0 = pl.multiple_of(jnp.clip(t * tm - POOL_HALO, 0, T - cfg.kw), POOL_HALO)
    return r0, seg_lo, seg_hi, k0


def _pool_count(pos, h, seg_lo, seg_hi):
    return jnp.maximum(jnp.minimum(pos + h, seg_hi) - jnp.maximum(pos - h, seg_lo), 1).astype(F32)


def _split_bf16(x):
    hi = x.astype(BF16)
    return hi, (x - hi.astype(F32)).astype(BF16)


def _pool_fwd(cfg, l, u, wp, ps):
    B, T, tm, kw = cfg.B, cfg.T, cfg.tm, cfg.kw

    def body(u_ref, wp_ref, ps_ref, ob_ref, pooled_ref):
        t = pl.program_id(1)
        r0, seg_lo, seg_hi, k0 = _pool_geometry(cfg, t)
        hi, lo = _split_bf16(u_ref[0, pl.ds(k0, kw), :])
        rr = r0 + lax.broadcasted_iota(jnp.int32, (tm, kw), 0)
        cc = k0 + lax.broadcasted_iota(jnp.int32, (tm, kw), 1)
        diff = cc - rr
        inseg = (cc >= seg_lo) & (cc < seg_hi)
        rcol = r0 + lax.broadcasted_iota(jnp.int32, (tm, 1), 0)
        group = lax.broadcasted_iota(jnp.int32, (tm, POOL_W), 1) // HEAD
        acc = jnp.zeros((tm, POOL_W), F32)
        for gi, w in enumerate(POOL_WINDOWS):
            h = w // 2
            band = jnp.where((diff >= -h) & (diff <= h - 1) & inseg, 1.0, 0.0).astype(BF16)
            tot = _dot(band, hi) + _dot(band, lo)
            acc = jnp.where(group == gi, tot / _pool_count(rcol, h, seg_lo, seg_hi), acc)
        pooled = (acc - u_ref[0, pl.ds(r0, tm), :]).astype(BF16)
        pooled_ref[0] = pooled
        ob_ref[0] = (_dot(pooled, wp_ref[0]) * ps_ref[0]).astype(BF16)

    row = pl.BlockSpec((1, tm, POOL_W), lambda b, t: (b, t, 0))
    return pl.pallas_call(
        body, name="pool_fwd", grid=(B, cfg.nT),
        in_specs=[pl.BlockSpec((1, T, POOL_W), lambda b, t: (b, 0, 0)),
                  pl.BlockSpec((1, POOL_W, POOL_W), lambda b, t: (l, 0, 0)), _layer(l, POOL_W)],
        out_specs=[row, row],
        out_shape=[jax.ShapeDtypeStruct((B, T, POOL_W), BF16)] * 2, compiler_params=_params(2),
    )(u, wp, ps)


def _gate_specs(cfg):
    tm, gw = cfg.tm, cfg.gw
    first = GATE0 // gw
    return [pl.BlockSpec((1, tm, gw), functools.partial(lambda b, t, j: (b, t, j), j=first + i)) for i in range(3 * cfg.D // gw)]


def _read_gates(cfg, gate_refs):
    per = cfg.D // cfg.gw
    return [jnp.concatenate([gate_refs[k * per + i][0] for i in range(per)], axis=-1).astype(F32) for k in range(3)]


def _merge_fwd(cfg, x, oa, ob, oc, z, mod, wa, wb, wc, wo, *, ctx_active, exch=None):
    B, T, D, tm, nC = cfg.B, cfg.T, cfg.D, cfg.tm, cfg.nC
    ng = 3 * D // cfg.gw

    def body(x_ref, oa_ref, ob_ref, oc_ref, *rest):
        gate_refs = rest[:ng]
        mod_ref, wa_ref, wb_ref, wc_ref, wo_ref, x1_ref, mgo_ref = rest[ng:]
        b, t = pl.program_id(0), pl.program_id(1)

        def compute():
            row = jnp.where(t < nC, B, b)
            ga, gb, gc = _read_gates(cfg, gate_refs)
            y = (jax.nn.sigmoid(ga) * _dot(oa_ref[0], wa_ref[...])
                 + jax.nn.sigmoid(gb) * _dot(ob_ref[0], wb_ref[...])
                 + jax.nn.sigmoid(gc) * _dot(oc_ref[0], wc_ref[...]))
            mo = _dot(y.astype(BF16), wo_ref[...])
            mgo_ref[0] = mo.astype(BF16)
            x1_ref[0] = x_ref[0] + _mod_row(mod_ref, row, 2, D) * mo

        if ctx_active:
            compute()
        else:
            pl.when(t >= nC)(compute)

            @pl.when(t < nC)
            def _():
                mgo_ref[0] = jnp.zeros((tm, D), BF16)
                x1_ref[0] = x_ref[0]

    row = lambda w: pl.BlockSpec((1, tm, w), lambda b, t: (b, t, 0))
    return _pcall(exch)(
        body, name="merge_fwd", grid=(B, cfg.nT),
        in_specs=[row(D), row(QW), row(POOL_W), row(QW)] + _gate_specs(cfg)
        + [_full((8, 6 * D)), _full((QW, D)), _full((POOL_W, D)), _full((QW, D)), _full((D, D))],
        out_specs=[row(D), row(D)],
        out_shape=[jax.ShapeDtypeStruct((B, T, D), F32), jax.ShapeDtypeStruct((B, T, D), BF16)],
        compiler_params=_params(2),
    )(x, oa, ob, oc, *([z] * ng), mod, wa, wb, wc, wo)


def _w1_apply(hb, w1_ref):
    return jnp.concatenate([_dot(hb, w1_ref[d]) for d in range(N_DEV)], axis=-1)


def _mlp_fwd(cfg, l, x1, gn, mod, w1, w2, *, ctx_active, target=None, exch=None):
    B, T, D, F, tm, nC = cfg.B, cfg.T, cfg.D, cfg.F, cfg.tm, cfg.nC
    assert target is None or not ctx_active

    def body(x_ref, gn_ref, mod_ref, w1_ref, w2_ref, *rest):
        if target is None:
            x2_ref, mo_ref, r_ref = rest
        else:
            tgt_ref, x2_ref, mo_ref, r_ref, sse_ref = rest
            _acc_init([sse_ref])
        b, t = pl.program_id(0), pl.program_id(1)

        def compute():
            row = jnp.where(t < nC, B, b)
            x = x_ref[0]
            h, _, _ = _modulate(x, gn_ref[0], _mod_row(mod_ref, row, 3, D), _mod_row(mod_ref, row, 4, D))
            a = jnp.maximum(_w1_apply(h.astype(BF16), w1_ref), 0.0)
            rb = (a * a).astype(BF16)
            r_ref[0] = rb
            mo = _dot(rb, w2_ref[...])
            mo_ref[0] = mo.astype(BF16)
            x2 = x + _mod_row(mod_ref, row, 5, D) * mo
            if target is None:
                x2_ref[0] = x2
            else:
                err = x2 - tgt_ref[0]
                x2_ref[0] = err * (1.0 / D)
                sse_ref[...] += jnp.sum(err * err)

        if ctx_active:
            compute()
        else:
            pl.when(t >= nC)(compute)

            @pl.when(t < nC)
            def _():
                mo_ref[0] = jnp.zeros((tm, D), BF16)
                r_ref[0] = jnp.zeros((tm, F), BF16)
                x2_ref[0] = x_ref[0] if target is None else jnp.zeros((tm, D), F32)

    row = pl.BlockSpec((1, tm, D), lambda b, t: (b, t, 0))
    in_specs = [row, _layer(l, D), _full((8, 6 * D)), _full((N_DEV, D, F // N_DEV)), _full((F, D))]
    out_specs = [row, row, pl.BlockSpec((1, tm, F), lambda b, t: (b, t, 0))]
    out_shape = [jax.ShapeDtypeStruct((B, T, D), F32), jax.ShapeDtypeStruct((B, T, D), BF16), jax.ShapeDtypeStruct((B, T, F), BF16)]
    args = [x1, gn, mod, w1, w2]
    if target is not None:
        in_specs.append(pl.BlockSpec((1, tm, D), lambda b, t: (b, jnp.maximum(t - nC, 0), 0)))
        out_specs.append(_full((8, LANES)))
        out_shape.append(jax.ShapeDtypeStruct((8, LANES), F32))
        args.append(target)
    return _pcall(exch)(
        body, name="mlp_fwd", grid=(B, cfg.nT), in_specs=in_specs, out_specs=out_specs, out_shape=out_shape,
        compiler_params=_params(2),
    )(*args)


def _acc_init(refs):
    b, t = pl.program_id(0), pl.program_id(1)

    @pl.when((b == 0) & (t == 0))
    def _():
        for ref in refs:
            ref[...] = jnp.zeros(ref.shape, ref.dtype)


def _mlp_bwd(cfg, l, x1, dx2, mo, r, gn, mod, w1, w2, *, ctx_active, exch=None):
    B, T, D, F, tm, nC = cfg.B, cfg.T, cfg.D, cfg.F, cfg.tm, cfg.nC
    ws = F // N_DEV

    def body(x_ref, dx_ref, mo_ref, r_ref, gn_ref, mod_ref, w1_ref, w2_ref, dx1_ref, h_ref, da_ref, dout_ref, dmod_ref, dgn_ref):
        b, t = pl.program_id(0), pl.program_id(1)
        _acc_init([dmod_ref, dgn_ref])

        def compute():
            row = jnp.where(t < nC, B, b)
            gn = gn_ref[0]
            scale = _mod_row(mod_ref, row, 4, D)
            h, xhat, rstd = _modulate(x_ref[0], gn, _mod_row(mod_ref, row, 3, D), scale)
            hb = h.astype(BF16)
            dx = dx_ref[0]
            dout = (dx * _mod_row(mod_ref, row, 5, D)).astype(BF16)
            da = (_dot_nt(dout, w2_ref[...]) * (2.0 * jnp.sqrt(r_ref[0].astype(F32)))).astype(BF16)
            dh = _dot_nt(da[:, 0:ws], w1_ref[0])
            for d in range(1, N_DEV):
                dh = dh + _dot_nt(da[:, d * ws:(d + 1) * ws], w1_ref[d])
            dxn, d_shift, d_scale, d_gn = _modulate_bwd(dh, xhat, rstd, gn, scale)
            dx1_ref[0] = dx + dxn
            h_ref[0] = hb
            da_ref[0] = da
            dout_ref[0] = dout
            d_gate = jnp.sum(dx * mo_ref[0].astype(F32), axis=0, keepdims=True)
            dmod_ref[pl.ds(row, 1), :] += jnp.concatenate([d_shift, d_scale, d_gate], axis=-1)
            dgn_ref[0:1, :] += d_gn

        if ctx_active:
            compute()
        else:
            pl.when(t >= nC)(compute)

            @pl.when(t < nC)
            def _():
                dx1_ref[0] = dx_ref[0]
                h_ref[0] = jnp.zeros((tm, D), BF16)
                da_ref[0] = jnp.zeros((tm, F), BF16)
                dout_ref[0] = jnp.zeros((tm, D), BF16)

    row = lambda w: pl.BlockSpec((1, tm, w), lambda b, t: (b, t, 0))
    sds = lambda w, dt: jax.ShapeDtypeStruct((B, T, w), dt)
    return _pcall(exch)(
        body, name="mlp_bwd", grid=(B, cfg.nT),
        in_specs=[row(D), row(D), row(D), row(F), _layer(l, D), _full((8, 6 * D)), _full((N_DEV, D, ws)), _full((F, D))],
        out_specs=[row(D), row(D), row(F), row(D), _full((8, 3 * D)), _full((8, D))],
        out_shape=[sds(D, F32), sds(D, BF16), sds(F, BF16), sds(D, BF16),
                   jax.ShapeDtypeStruct((8, 3 * D), F32), jax.ShapeDtypeStruct((8, D), F32)],
        compiler_params=_params(2),
    )(x1, dx2, mo, r, gn, mod, w1, w2)


def _matmul_tn(a, g, name, *, by_shard, a_cols=None, exch=None):
    R = a.shape[0]
    Ng = g.shape[1]
    tr = next(c for c in (2304, 1024, 512, 256, 128, 64, 32, 16, 8) if R % c == 0)
    if a_cols is None:
        Ka, a_blk = a.shape[1], 0
        tka = Ka if Ka <= 1024 else 1024
    else:
        a_start, Ka = a_cols
        tka = Ka
        assert a_start % Ka == 0 and Ka % LANES == 0
        a_blk = a_start // Ka
    if by_shard:
        ws = Ng // N_DEV
        per = next(c for c in (8, 4, 2, 1) if c * ws <= 1152 or c == 1)
        tn = per * ws
    else:
        tn = next(c for c in (1152, 1024, 768, 512, 384, 256, 128) if Ng % c == 0)
    assert Ka % tka == 0 and tn % LANES == 0
    nr = R // tr

    def body(a_ref, g_ref, o_ref, acc_ref):
        r = pl.program_id(2)

        @pl.when(r == 0)
        def _():
            acc_ref[...] = jnp.zeros(acc_ref.shape, F32)

        acc_ref[...] += _dot_tn(a_ref[...], g_ref[...])

        @pl.when(r == nr - 1)
        def _():
            if by_shard:
                for d in range(per):
                    o_ref[d] = acc_ref[:, d * ws:(d + 1) * ws].astype(BF16)
            else:
                o_ref[...] = acc_ref[...].astype(BF16)

    if by_shard:
        out_spec = pl.BlockSpec((per, tka, ws), lambda i, j, r: (j, i, 0))
        out_shape = jax.ShapeDtypeStruct((N_DEV, Ka, ws), BF16)
    else:
        out_spec = pl.BlockSpec((tka, tn), lambda i, j, r: (i, j))
        out_shape = jax.ShapeDtypeStruct((Ka, Ng), BF16)
    return _pcall(exch)(
        body, name=name, grid=(Ka // tka, Ng // tn, nr),
        in_specs=[pl.BlockSpec((tr, tka), lambda i, j, r: (r, i + a_blk)), pl.BlockSpec((tr, tn), lambda i, j, r: (r, j))],
        out_specs=out_spec, out_shape=out_shape, scratch_shapes=[pltpu.VMEM((tka, tn), F32)], compiler_params=_params(3),
    )(a, g)


def _merge_bwd(cfg, dx1, mgo, oa, ob, oc, z, mod, wa, wb, wc, wo, *, ctx_active, exch=None):
    B, T, D, tm, nC = cfg.B, cfg.T, cfg.D, cfg.tm, cfg.nC
    ng = 3 * D // cfg.gw

    def body(dx_ref, mgo_ref, oa_ref, ob_ref, oc_ref, *rest):
        gate_refs = rest[:ng]
        (mod_ref, wa_ref, wb_ref, wc_ref, wo_ref,
         doa_ref, dob_ref, doc_ref, dpa_ref, dpb_ref, dpc_ref, y_ref, dmo_ref, dzg_ref, dg1_ref) = rest[ng:]
        b, t = pl.program_id(0), pl.program_id(1)
        _acc_init([dg1_ref])

        def compute():
            row = jnp.where(t < nC, B, b)
            dx = dx_ref[0]
            dg1_ref[pl.ds(row, 1), :] += jnp.sum(dx * mgo_ref[0].astype(F32), axis=0, keepdims=True)
            dmo = (dx * _mod_row(mod_ref, row, 2, D)).astype(BF16)
            dmo_ref[0] = dmo
            dy = _dot_nt(dmo, wo_ref[...])
            gates = _read_gates(cfg, gate_refs)
            y = jnp.zeros((tm, D), F32)
            dgs = []
            for gate, o_ref, w_ref, do_ref, dp_ref in ((gates[0], oa_ref, wa_ref, doa_ref, dpa_ref),
                                                      (gates[1], ob_ref, wb_ref, dob_ref, dpb_ref),
                                                      (gates[2], oc_ref, wc_ref, doc_ref, dpc_ref)):
                s = jax.nn.sigmoid(gate)
                p = _dot(o_ref[0], w_ref[...])
                y = y + s * p
                dp = (dy * s).astype(BF16)
                dp_ref[0] = dp
                do_ref[0] = _dot_nt(dp, w_ref[...]).astype(BF16)
                dgs.append((dy * p * s * (1.0 - s)).astype(BF16))
            y_ref[0] = y.astype(BF16)
            dzg_ref[0] = jnp.concatenate(dgs, axis=-1)

        if ctx_active:
            compute()
        else:
            pl.when(t >= nC)(compute)

            @pl.when(t < nC)
            def _():
                for ref in (doa_ref, dob_ref, doc_ref, dpa_ref, dpb_ref, dpc_ref, y_ref, dmo_ref, dzg_ref):
                    ref[...] = jnp.zeros(ref.shape, ref.dtype)

    row = lambda w: pl.BlockSpec((1, tm, w), lambda b, t: (b, t, 0))
    sds = lambda w: jax.ShapeDtypeStruct((B, T, w), BF16)
    return _pcall(exch)(
        body, name="merge_bwd", grid=(B, cfg.nT),
        in_specs=[row(D), row(D), row(QW), row(POOL_W), row(QW)] + _gate_specs(cfg)
        + [_full((8, 6 * D)), _full((QW, D)), _full((POOL_W, D)), _full((QW, D)), _full((D, D))],
        out_specs=[row(QW), row(POOL_W), row(QW), row(D), row(D), row(D), row(D), row(D), row(3 * D), _full((8, D))],
        out_shape=[sds(QW), sds(POOL_W), sds(QW), sds(D), sds(D), sds(D), sds(D), sds(D), sds(3 * D),
                   jax.ShapeDtypeStruct((8, D), F32)],
        compiler_params=_params(2),
    )(dx1, mgo, oa, ob, oc, *([z] * ng), mod, wa, wb, wc, wo)


def _attn_bwd(cfg, l, q, k, z, vblock, sink8, do, lse, *, window, sink, ctx_attend, name, exch=None):
    B, S, N, T, tq, nC = cfg.B, cfg.S, cfg.N, cfg.T, cfg.tm, cfg.nC

    def body(q_ref, k_ref, v_ref, sink_ref, do_ref, lse_ref, dq_ref, dk_ref, dv_ref, dsink_ref):
        b, t = pl.program_id(0), pl.program_id(1)
        _acc_init([dsink_ref])

        @pl.when(t == 0)
        def _():
            dk_ref[...] = jnp.zeros(dk_ref.shape, F32)
            dv_ref[...] = jnp.zeros(dv_ref.shape, F32)

        def run(loc):
            q_t = q_ref[0]
            do_t = do_ref[0]
            dqs = [None] * N_QHEADS
            dks, dvs = [], []
            dsink_row = jnp.zeros((1, LANES), F32)
            lane = lax.broadcasted_iota(jnp.int32, (1, LANES), 1)
            lo = None
            for h in range(N_KV):
                lse = jnp.concatenate([lse_ref[0, GROUP * h + g:GROUP * h + g + 1, :] for g in range(GROUP)], axis=1)
                qs, kk, vv, p, _, p_s, lo, _ = _attn_scores(cfg, t - nC, q_t, k_ref, v_ref, sink_ref, h, loc, window and loc, sink, lse=lse)
                dos = jnp.concatenate([do_t[:, (GROUP * h + g) * HEAD:(GROUP * h + g + 1) * HEAD] for g in range(GROUP)], axis=0)
                dp = _dot_nt(vv, dos)
                delta = jnp.sum(p * dp, axis=0, keepdims=True)
                ds = (p * (dp - delta)).astype(BF16)
                dq = _dot_tn(kk, ds).T
                dks.append(_dot(ds, qs))
                dvs.append(_dot(p.astype(BF16), dos))
                if sink:
                    dsk = -p_s * delta
                    for g in range(GROUP):
                        tot = jnp.sum(dsk[:, g * tq:(g + 1) * tq], axis=1, keepdims=True)
                        dsink_row = dsink_row + jnp.where(lane == GROUP * h + g, tot, 0.0)
                for g in range(GROUP):
                    dqs[GROUP * h + g] = dq[g * tq:(g + 1) * tq] * QSCALE
            dq_ref[0] = jnp.concatenate(dqs, axis=-1)
            dk = jnp.concatenate(dks, axis=-1)
            dv = jnp.concatenate(dvs, axis=-1)
            if loc and not window:
                dk_ref[0] += dk
                dv_ref[0] += dv
            else:
                dk_ref[0, 0:N, :] += dk[0:N]
                dv_ref[0, 0:N, :] += dv[0:N]
                if loc:
                    W = tq + 2 * WINDOW
                    dk_ref[0, pl.ds(N + lo, W), :] += dk[N:]
                    dv_ref[0, pl.ds(N + lo, W), :] += dv[N:]
            if sink:
                dsink_ref[0:1, :] += dsink_row

        pl.when(t >= nC)(functools.partial(run, True))
        if ctx_attend:
            pl.when(t < nC)(functools.partial(run, False))
        else:
            @pl.when(t < nC)
            def _():
                dq_ref[0] = jnp.zeros((tq, QW), F32)

    kv = pl.BlockSpec((1, T, KVW), lambda b, t: (b, 0, 0))
    qrow = pl.BlockSpec((1, tq, QW), lambda b, t: (b, t, 0))
    return _pcall(exch)(
        body, name=name, grid=(B, cfg.nT),
        in_specs=[qrow, kv, pl.BlockSpec((1, T, KVW), lambda b, t: (b, 0, vblock)),
                  pl.BlockSpec((1, 8, LANES), lambda b, t: (l, 0, 0)), qrow, pl.BlockSpec((1, 8, tq), lambda b, t: (b, 0, t))],
        out_specs=[qrow, kv, kv, _full((8, LANES))],
        out_shape=[jax.ShapeDtypeStruct((B, T, QW), F32), jax.ShapeDtypeStruct((B, T, KVW), F32),
                   jax.ShapeDtypeStruct((B, T, KVW), F32), jax.ShapeDtypeStruct((8, LANES), F32)],
        compiler_params=_params(2),
    )(q, k, z, sink8, do, lse)


def _qknorm_bwd(cfg, l, z, gvec, cosf, sins, dqa, dka, dva, dqc, dkc, dvc):
    B, T, tm = cfg.B, cfg.T, cfg.tm

    def body(z_ref, g_ref, cos_ref, sin_ref, dqa_ref, dka_ref, dva_ref, dqc_ref, dkc_ref, dvc_ref, dz_ref, dg_ref):
        _acc_init([dg_ref])
        ind = _head_indicator()
        cos, sin = cos_ref[...], sin_ref[...]
        dqa_t, dqc_t = dqa_ref[0], dqc_ref[0]
        douts = {0: dqa_t[:, 0:128], 1: dqa_t[:, 128:256], 2: dqa_t[:, 256:384], 3: dka_ref[0],
                 5: dqc_t[:, 0:128], 6: dqc_t[:, 128:256], 7: dqc_t[:, 256:384], 8: dkc_ref[0]}
        pieces = []
        dgs = []
        for c in range(QKV_W // LANES):
            if c not in douts:
                pieces.append(dva_ref[0] if c == 4 else dvc_ref[0])
                dgs.append(jnp.zeros((1, LANES), F32))
                continue
            x = z_ref[0, :, c * LANES:(c + 1) * LANES].astype(F32)
            g = g_ref[0, :, c * LANES:(c + 1) * LANES]
            ss = _head_sum(x * x, ind)
            rstd = lax.rsqrt(ss * (1.0 / HEAD) + EPS)
            n = x * rstd
            dout = douts[c]
            dy = dout * cos + _pair_swap(dout * sin)
            dgs.append(jnp.sum(dy * n, axis=0, keepdims=True))
            dn = dy * g
            mean = _head_sum(dn * n, ind) * (1.0 / HEAD)
            pieces.append(rstd * (dn - n * mean))
        dz_ref[0] = jnp.concatenate(pieces, axis=-1).astype(BF16)
        dg_ref[0:1, :] += jnp.concatenate(dgs, axis=-1)

    row = lambda w: pl.BlockSpec((1, tm, w), lambda b, t: (b, t, 0))
    tab = pl.BlockSpec((tm, LANES), lambda b, t: (t, 0))
    return pl.pallas_call(
        body, name="qknorm_bwd", grid=(B, cfg.nT),
        in_specs=[row(QKV_W), _layer(l, QKV_W), tab, tab, row(QW), row(KVW), row(KVW), row(QW), row(KVW), row(KVW)],
        out_specs=[row(QKV_W), _full((8, QKV_W))],
        out_shape=[jax.ShapeDtypeStruct((B, T, QKV_W), BF16), jax.ShapeDtypeStruct((8, QKV_W), F32)],
        compiler_params=_params(2),
    )(z, gvec, cosf, sins, dqa, dka, dva, dqc, dkc, dvc)


def _pool_bwd(cfg, l, dob, pooled, wp, ps):
    B, T, tm, kw = cfg.B, cfg.T, cfg.tm, cfg.kw

    def body(dob_ref, pooled_ref, wp_ref, ps_ref, du_ref, dwp_ref, dps_ref):
        t = pl.program_id(1)
        _acc_init([dwp_ref, dps_ref])
        r0, seg_lo, seg_hi, k0 = _pool_geometry(cfg, t)
        ps = ps_ref[0]
        wp = wp_ref[0]
        dmix = dob_ref[0, pl.ds(r0, tm), :].astype(F32)
        pooled = pooled_ref[0]
        dps_ref[0:1, :] += jnp.sum(dmix * _dot(pooled, wp), axis=0, keepdims=True)
        dpm = (dmix * ps).astype(BF16)
        dwp_ref[...] += _dot_tn(pooled, dpm)
        dpooled_t = _dot_nt(dpm, wp)
        dpm_w = (dob_ref[0, pl.ds(k0, kw), :].astype(F32) * ps).astype(BF16)
        dpooled_w = _dot_nt(dpm_w, wp)
        rr = r0 + lax.broadcasted_iota(jnp.int32, (tm, kw), 0)
        cc = k0 + lax.broadcasted_iota(jnp.int32, (tm, kw), 1)
        diff = rr - cc
        inseg = (cc >= seg_lo) & (cc < seg_hi)
        ccol = k0 + lax.broadcasted_iota(jnp.int32, (kw, 1), 0)
        group = lax.broadcasted_iota(jnp.int32, (tm, POOL_W), 1) // HEAD
        acc = jnp.zeros((tm, POOL_W), F32)
        for gi, w in enumerate(POOL_WINDOWS):
            h = w // 2
            band_t = jnp.where((diff >= -h) & (diff <= h - 1) & inseg, 1.0, 0.0).astype(BF16)
            hi, lo = _split_bf16(dpooled_w / _pool_count(ccol, h, seg_lo, seg_hi))
            acc = jnp.where(group == gi, _dot(band_t, hi) + _dot(band_t, lo), acc)
        du_ref[0] = (acc - dpooled_t).astype(BF16)

    row = pl.BlockSpec((1, tm, POOL_W), lambda b, t: (b, t, 0))
    return pl.pallas_call(
        body, name="pool_bwd", grid=(B, cfg.nT),
        in_specs=[pl.BlockSpec((1, T, POOL_W), lambda b, t: (b, 0, 0)), row,
                  pl.BlockSpec((1, POOL_W, POOL_W), lambda b, t: (l, 0, 0)), _layer(l, POOL_W)],
        out_specs=[row, _full((POOL_W, POOL_W)), _full((8, POOL_W))],
        out_shape=[jax.ShapeDtypeStruct((B, T, POOL_W), BF16), jax.ShapeDtypeStruct((POOL_W, POOL_W), F32),
                   jax.ShapeDtypeStruct((8, POOL_W), F32)],
        compiler_params=_params(2),
    )(dob, pooled, wp, ps)


def _in_proj_bwd(cfg, l, dzq, du, dzg, w_in, x, dx1, gn, mod, *, latent_only, exch=None):
    B, S, T, D, IN, tm, nC = cfg.B, cfg.S, cfg.T, cfg.D, cfg.IN, cfg.tm, cfg.nC

    def body(dzq_ref, du_ref, dzg_ref, w_ref, x_ref, dx1_ref, gn_ref, mod_ref, dx0_ref, dz_ref, dmod_ref, dgn_ref):
        b, t = pl.program_id(0), pl.program_id(1)
        _acc_init([dmod_ref, dgn_ref])
        row = jnp.where(t < nC, B, b)
        dz = jnp.concatenate([dzq_ref[0], du_ref[0], dzg_ref[0]], axis=-1)
        dz_ref[0] = dz
        dh = _dot_nt(dz, w_ref[...])
        gn = gn_ref[0]
        scale = _mod_row(mod_ref, row, 1, D)
        _, xhat, rstd = _modulate(x_ref[0], gn, _mod_row(mod_ref, row, 0, D), scale)
        dxn, d_shift, d_scale, d_gn = _modulate_bwd(dh, xhat, rstd, gn, scale)
        dx0_ref[0] = dx1_ref[0] + dxn
        dmod_ref[pl.ds(row, 1), :] += jnp.concatenate([d_shift, d_scale], axis=-1)
        dgn_ref[0:1, :] += d_gn

    row = lambda w: pl.BlockSpec((1, tm, w), lambda b, t: (b, t, 0))
    if latent_only:
        dx0_spec = pl.BlockSpec((1, tm, D), lambda b, t: (b, jnp.maximum(t - nC, 0), 0))
        dx0_shape = jax.ShapeDtypeStruct((B, S, D), F32)
    else:
        dx0_spec, dx0_shape = row(D), jax.ShapeDtypeStruct((B, T, D), F32)
    return _pcall(exch)(
        body, name="in_proj_bwd", grid=(B, cfg.nT),
        in_specs=[row(QKV_W), row(POOL_W), row(3 * D), _full((D, IN)), row(D), row(D), _layer(l, D), _full((8, 6 * D))],
        out_specs=[dx0_spec, row(IN), _full((8, 2 * D)), _full((8, D))],
        out_shape=[dx0_shape, jax.ShapeDtypeStruct((B, T, IN), BF16),
                   jax.ShapeDtypeStruct((8, 2 * D), F32), jax.ShapeDtypeStruct((8, D), F32)],
        compiler_params=_params(2),
    )(dzq, du, dzg, w_in, x, dx1, gn, mod)


def _adaln_bwd(cfg, l, cc_all, dm_all, w_ada):
    d, B = cfg.D, cfg.B
    wa = w_ada.shape[2]

    def body(c_ref, dm_ref, w_ref, dw_ref, dc_ref):
        c = c_ref[...]
        s = jax.nn.sigmoid(c)
        dmb = dm_ref[...].astype(BF16)
        dw_ref[...] = _dot_tn((c * s).astype(BF16), dmb)
        dc = _dot_nt(dmb, w_ref[0].astype(BF16)) * (s * (1.0 + c * (1.0 - s)))
        is_ctx = lax.broadcasted_iota(jnp.int32, (8 * N_DEV, 1), 0) % 8 == B
        dc_ref[...] = jnp.broadcast_to(jnp.sum(jnp.where(is_ctx, dc, 0.0), axis=0, keepdims=True), (8, d))

    return pl.pallas_call(
        body, name="adaln_bwd", grid=(1,),
        in_specs=[_full((8 * N_DEV, d)), _full((8 * N_DEV, wa)), pl.BlockSpec((1, d, wa), lambda *_: (l, 0, 0))],
        out_specs=[_full((d, wa)), _full((8, d))],
        out_shape=[jax.ShapeDtypeStruct((d, wa), F32), jax.ShapeDtypeStruct((8, d), F32)],
        compiler_params=_params(1),
    )(cc_all, dm_all, w_ada)


def _dmod_pack(cfg, dmod_in, dg1, dmod_mlp):
    d = cfg.D
    wa = 6 * d // N_DEV

    def body(din_ref, dg1_ref, dmlp_ref, o_ref, db_ref):
        dm = jnp.concatenate([din_ref[...], dg1_ref[...], dmlp_ref[...]], axis=-1)
        for j in range(N_DEV):
            o_ref[j] = dm[:, j * wa:(j + 1) * wa]
        db_ref[...] = jnp.broadcast_to(jnp.sum(dm, axis=0, keepdims=True), (8, 6 * d))

    return pl.pallas_call(
        body, name="dmod_pack", grid=(1,),
        in_specs=[_full((8, 2 * d)), _full((8, d)), _full((8, 3 * d))],
        out_specs=[_full((N_DEV, 8, wa)), _full((8, 6 * d))],
        out_shape=[jax.ShapeDtypeStruct((N_DEV, 8, wa), F32), jax.ShapeDtypeStruct((8, 6 * d), F32)],
        compiler_params=_params(1),
    )(dmod_in, dg1, dmod_mlp)


def _adam_update(g, w, m, v):
    bc1 = 1.0 - ADAM_B1 ** ADAM_STEP
    bc2 = 1.0 - ADAM_B2 ** ADAM_STEP
    m2 = ADAM_B1 * m + (1.0 - ADAM_B1) * g
    v2 = ADAM_B2 * v + (1.0 - ADAM_B2) * (g * g)
    delta = -ADAM_LR * ((m2 / bc1) / (jnp.sqrt(v2 / bc2) + ADAM_EPS) + ADAM_WD * w)
    return delta, m2, v2


def _sum_parts(p_ref):
    g = p_ref[0].astype(F32)
    for d in range(1, p_ref.shape[0]):
        g = g + p_ref[d].astype(F32)
    return g


def _adamw_sharded(parts, w, m, v, name):
    L, K, W = w.shape
    min_rows = min(c.shape[1] for chunks in parts for c in chunks)
    tk = next(c for c in (256, 128, 64, 32, 16, 8) if K % c == 0 and min_rows % c == 0)
    spans, flat = [], []
    for li, chunks in enumerate(parts):
        row = 0
        for c in chunks:
            assert c.shape[1] % tk == 0
            spans.append((li, row // tk, (row + c.shape[1]) // tk))
            flat.append(c)
            row += c.shape[1]
        assert row == K

    def body(*refs):
        p_refs = refs[:len(flat)]
        w_ref, m_ref, v_ref, g_ref, d_ref, m2_ref, v2_ref = refs[len(flat):]
        layer, i = pl.program_id(0), pl.program_id(1)

        def run(p_ref):
            g = _sum_parts(p_ref)
            delta, m2, v2 = _adam_update(g, w_ref[0], m_ref[0], v_ref[0])
            g_ref[0] = g
            d_ref[0] = delta
            m2_ref[0] = m2
            v2_ref[0] = v2

        for (li, lo, hi), p_ref in zip(spans, p_refs):
            pl.when((layer == li) & (i >= lo) & (i < hi))(functools.partial(run, p_ref))

    blk = pl.BlockSpec((1, tk, W), lambda l, i: (l, i, 0))

    def part_spec(span, arr):
        li, lo, hi = span
        return pl.BlockSpec((arr.shape[0], tk, W), lambda l, i: (0, jnp.where((l == li) & (i >= lo) & (i < hi), i - lo, 0), 0))

    return pl.pallas_call(
        body, name=name, grid=(L, K // tk),
        in_specs=[part_spec(sp, arr) for sp, arr in zip(spans, flat)] + [blk, blk, blk],
        out_specs=[blk] * 4, out_shape=[jax.ShapeDtypeStruct((L, K, W), F32)] * 4,
        compiler_params=_params(2),
    )(*flat, w, m, v)


def _adamw_packed(parts, w, m, v, name):
    rows = w.shape[0]
    tr = next(c for c in (256, 128, 64, 32, 16, 8) if rows % c == 0)

    def body(p_ref, w_ref, m_ref, v_ref, g_ref, d_ref, m2_ref, v2_ref):
        g = _sum_parts(p_ref)
        delta, m2, v2 = _adam_update(g, w_ref[...], m_ref[...], v_ref[...])
        g_ref[...] = g
        d_ref[...] = delta
        m2_ref[...] = m2
        v2_ref[...] = v2

    blk = pl.BlockSpec((tr, PACK_W), lambda i: (i, 0))
    return pl.pallas_call(
        body, name=name, grid=(rows // tr,),
        in_specs=[pl.BlockSpec((N_DEV, tr, PACK_W), lambda i: (0, i, 0)), blk, blk, blk],
        out_specs=[blk] * 4, out_shape=[jax.ShapeDtypeStruct((rows, PACK_W), F32)] * 4,
        compiler_params=_params(1),
    )(parts, w, m, v)


_SHARDED = dict(w_ada=True, w_in=True, w_br_a=True, w_br_b=True, w_br_c=True, w_out=False, w_mlp1=True, w_mlp2=False)
_MERGE_WEIGHTS = ("w_br_a", "w_br_b", "w_br_c", "w_out")
_GATHERED = ("w_in",) + _MERGE_WEIGHTS + ("w_mlp1", "w_mlp2")
_KEEP_SHARDS = ("w_mlp1",)
_SMALL = ("c_ctx", "b_ada", "norm1", "norm2", "q_norm_a", "k_norm_a", "q_norm_c", "k_norm_c", "sink_c", "w_pool", "pool_scale")


def _from_shards(name, g):
    n, k, w = g.shape
    if name in _KEEP_SHARDS:
        return g
    if _SHARDED[name]:
        return g.transpose(1, 0, 2).reshape(k, n * w)
    return g.reshape(n * k, w)


def _to_shards(name, g):
    if g.ndim == 3:
        return g
    if _SHARDED[name]:
        k, nw = g.shape
        return g.reshape(k, N_DEV, nw // N_DEV).transpose(1, 0, 2)
    nk, w = g.shape
    return g.reshape(N_DEV, nk // N_DEV, w)


def _pack_small(vals):
    flat = jnp.concatenate([vals[n].reshape(-1) for n in _SMALL])
    rows = -(-flat.shape[0] // (8 * PACK_W)) * 8
    return jnp.pad(flat, (0, rows * PACK_W - flat.shape[0])).reshape(rows, PACK_W)


def _unpack_small(packed, like):
    flat, out, r = packed.reshape(-1), {}, 0
    for n in _SMALL:
        sz = like[n].size
        out[n] = flat[r:r + sz].reshape(like[n].shape)
        r += sz
    return out


def _rope_tables(cfg):
    pos = jnp.arange(cfg.S, dtype=F32)
    r = jnp.floor(pos / GRID_W)
    col = pos - r * GRID_W
    inv = 1.0 / (ROPE_THETA ** (jnp.arange(0, HEAD // 2, 2, dtype=F32) / (HEAD // 2)))
    ang = jnp.concatenate([r[:, None] * inv, col[:, None] * inv], axis=-1)
    cos = jnp.repeat(jnp.cos(ang), 2, axis=-1)
    sin = jnp.repeat(jnp.sin(ang), 2, axis=-1) * jnp.tile(jnp.array([-1.0, 1.0], F32), HEAD // 2)
    cos = jnp.concatenate([jnp.ones((cfg.N, HEAD), F32), cos], axis=0)
    sin = jnp.concatenate([jnp.zeros((cfg.N, HEAD), F32), sin], axis=0)
    return jnp.tile(cos, (1, 2)), jnp.tile(sin, (1, 2))


def _gvec(qa, ka, qc, kc):
    one = jnp.ones((qa.shape[0], KVW), F32)
    t = lambda a, n: jnp.tile(a, (1, n))
    return jnp.concatenate([t(qa, N_QHEADS), t(ka, N_KV), one, t(qc, N_QHEADS), t(kc, N_KV), one], axis=-1)[:, None, :]


def _block_diag(wp):
    L, g, c, _ = wp.shape
    eye = jnp.eye(g, dtype=wp.dtype)
    return (wp[:, :, :, None, :] * eye[None, :, None, :, None]).reshape(L, g * c, g * c)


def _pad8(a):
    return jnp.pad(a, ((0, 8 - a.shape[0]), (0, 0)))


def kernel(x, c, ctx, c_ctx, w_ada, b_ada, norm1, norm2, w_in, q_norm_a, k_norm_a, q_norm_c, k_norm_c, sink_c, w_pool, pool_scale, w_br_a, w_br_b, w_br_c, w_out, w_mlp1, w_mlp2, loss_target, m_c_ctx, m_w_ada, m_b_ada, m_norm1, m_norm2, m_w_in, m_q_norm_a, m_k_norm_a, m_q_norm_c, m_k_norm_c, m_sink_c, m_w_pool, m_pool_scale, m_w_br_a, m_w_br_b, m_w_br_c, m_w_out, m_w_mlp1, m_w_mlp2, v_c_ctx, v_w_ada, v_b_ada, v_norm1, v_norm2, v_w_in, v_q_norm_a, v_k_norm_a, v_q_norm_c, v_k_norm_c, v_sink_c, v_w_pool, v_pool_scale, v_w_br_a, v_w_br_b, v_w_br_c, v_w_out, v_w_mlp1, v_w_mlp2):
    B, S, D = x.shape
    N = ctx.shape[1]
    L = w_ada.shape[0]
    cfg = _Cfg(B, S, N, D)
    T = cfg.T
    weights = dict(c_ctx=c_ctx, w_ada=w_ada, b_ada=b_ada, norm1=norm1, norm2=norm2, w_in=w_in, q_norm_a=q_norm_a,
                   k_norm_a=k_norm_a, q_norm_c=q_norm_c, k_norm_c=k_norm_c, sink_c=sink_c, w_pool=w_pool,
                   pool_scale=pool_scale, w_br_a=w_br_a, w_br_b=w_br_b, w_br_c=w_br_c, w_out=w_out, w_mlp1=w_mlp1, w_mlp2=w_mlp2)
    mom_m = dict(c_ctx=m_c_ctx, w_ada=m_w_ada, b_ada=m_b_ada, norm1=m_norm1, norm2=m_norm2, w_in=m_w_in, q_norm_a=m_q_norm_a,
                 k_norm_a=m_k_norm_a, q_norm_c=m_q_norm_c, k_norm_c=m_k_norm_c, sink_c=m_sink_c, w_pool=m_w_pool,
                 pool_scale=m_pool_scale, w_br_a=m_w_br_a, w_br_b=m_w_br_b, w_br_c=m_w_br_c, w_out=m_w_out, w_mlp1=m_w_mlp1, w_mlp2=m_w_mlp2)
    mom_v = dict(c_ctx=v_c_ctx, w_ada=v_w_ada, b_ada=v_b_ada, norm1=v_norm1, norm2=v_norm2, w_in=v_w_in, q_norm_a=v_q_norm_a,
                 k_norm_a=v_k_norm_a, q_norm_c=v_q_norm_c, k_norm_c=v_k_norm_c, sink_c=v_sink_c, w_pool=v_w_pool,
                 pool_scale=v_pool_scale, w_br_a=v_w_br_a, w_br_b=v_w_br_b, w_br_c=v_w_br_c, w_out=v_w_out, w_mlp1=v_w_mlp1, w_mlp2=v_w_mlp2)

    shards_bf16 = {n: weights[n].astype(BF16) for n in _GATHERED}
    full = [dict() for _ in range(L)]

    def gather_of(items):
        return _Exchange([(shards_bf16[n], l) for l, n in items], scatter=False)

    def gathered(items, arrs):
        for (l, n), a in zip(items, arrs):
            full[l][n] = _from_shards(n, a)

    gathered([(0, "w_in")], [_gather_two_level(shards_bf16["w_in"], 0, "gather_first_weights")])

    def hosting(fn, *a, exch=None, done=None, **kw):
        if exch is None:
            return fn(*a, **kw)
        res = fn(*a, exch=exch, **kw)
        done(res[-exch.n:])
        own = res[:-exch.n]
        return own[0] if len(own) == 1 else own

    def gather_behind(l, names):
        if l >= L:
            return {}
        items = [(l, n) for n in names]
        return dict(exch=gather_of(items), done=functools.partial(gathered, items))

    cosf, sins = _rope_tables(cfg)
    xs = jnp.concatenate([ctx, x], axis=1)
    cc8 = _pad8(jnp.concatenate([c, c_ctx[None, :]], axis=0))
    va_blk, vc_blk = (QW + KVW) // KVW, (2 * QW + 3 * KVW) // KVW
    per_layer = lambda a: a[:, None, :]
    b_ada3, norm1_3, norm2_3, ps3 = per_layer(b_ada), per_layer(norm1), per_layer(norm2), per_layer(pool_scale)
    gvec = _gvec(q_norm_a, k_norm_a, q_norm_c, k_norm_c)
    sink8 = jnp.pad(sink_c[:, None, :], ((0, 0), (0, 7), (0, LANES - N_QHEADS)))
    wp = _block_diag(w_pool).astype(BF16)

    cc_all = _Exchange([cc8], scatter=False).alone("gather_cond")[0].reshape(8 * N_DEV, D)
    mod_cols = _Exchange([_adaln_fwd(cfg, cc_all, w_ada)], scatter=True).alone("scatter_mod")[0]
    mod_all = _adaln_join(cfg, mod_cols, b_ada3)

    saved = []
    for l in range(L):
        fw = full[l]
        ctx_active = l < L - 1
        mod = mod_all[l]
        z, u, h = hosting(_in_proj_fwd, cfg, l, xs, norm1_3, mod, fw["w_in"], **gather_behind(l, _MERGE_WEIGHTS if l == 0 else ("w_mlp2",)))
        qa, ka, qc, kc = _qknorm_fwd(cfg, l, z, gvec, cosf, sins)
        oa, lse_a = hosting(_attn_fwd, cfg, l, qa, ka, z, va_blk, sink8, window=False, sink=False, ctx_attend=ctx_active, name="attn_a_fwd",
                            **gather_behind(l, ("w_mlp1", "w_mlp2") if l == 0 else ("w_mlp1",)))
        oc, lse_c = hosting(_attn_fwd, cfg, l, qc, kc, z, vc_blk, sink8, window=True, sink=True, ctx_attend=ctx_active, name="attn_c_fwd",
                            **(gather_behind(l + 1, _MERGE_WEIGHTS)))
        ob, pooled = _pool_fwd(cfg, l, u, wp, ps3)
        x1, mgo = hosting(_merge_fwd, cfg, xs, oa, ob, oc, z, mod, fw["w_br_a"], fw["w_br_b"], fw["w_br_c"], fw["w_out"],
                          ctx_active=ctx_active)
        if l < L - 1:
            x2, mo, r = hosting(_mlp_fwd, cfg, l, x1, norm2_3, mod, fw["w_mlp1"], fw["w_mlp2"], ctx_active=ctx_active,
                                **gather_behind(l + 1, ("w_in",)))
        else:
            x2, mo, r, sse = _mlp_fwd(cfg, l, x1, norm2_3, mod, fw["w_mlp1"], fw["w_mlp2"], ctx_active=ctx_active, target=loss_target)
        saved.append(dict(xs=xs, mod=mod, z=z, h=h, qa=qa, ka=ka, qc=qc, kc=kc, oa=oa, oc=oc, ob=ob, pooled=pooled, x1=x1, mgo=mgo, mo=mo,
                          lse_a=lse_a, lse_c=lse_c, r=r))
        xs = x2

    dxs = xs
    loss = lax.psum(0.5 * sse[0, 0] / D, ("x", "y", "c"))

    grads = [dict() for _ in range(L)]
    parts = {}
    small = {n: [None] * L for n in _SMALL if n != "c_ctx"}
    d_c_ctx = jnp.zeros((D,), F32)
    flat2 = lambda a: a.reshape(B * T, a.shape[-1])

    def scatter_of(l, names):
        return _Exchange([_to_shards(n, grads[l][n]) for n in names], scatter=True)

    def scattered(l, names, arrs):
        for n, a in zip(names, arrs):
            parts[(l, n)] = a

    def scatter_behind(l, names):
        if l >= L:
            return {}
        return dict(exch=scatter_of(l, names), done=functools.partial(scattered, l, names))

    for l in reversed(range(L)):
        fw, sv, g = full[l], saved[l], grads[l]
        ctx_active = l < L - 1
        mod = sv["mod"]
        dx1, h2, da, dout, dmod_mlp, dgn2 = hosting(_mlp_bwd, cfg, l, sv["x1"], dxs, sv["mo"], sv["r"], norm2_3, mod, fw["w_mlp1"], fw["w_mlp2"],
                                                    ctx_active=ctx_active, **scatter_behind(l + 1, ("w_in",)))
        g["w_mlp1"] = _matmul_tn(flat2(h2), flat2(da), "dw_mlp1", by_shard=True)
        g["w_mlp2"] = _matmul_tn(flat2(sv["r"]), flat2(dout), "dw_mlp2", by_shard=False)
        doa, dob, doc, dpa, dpb, dpc, y, dmo, dzg, dg1 = _merge_bwd(
            cfg, dx1, sv["mgo"], sv["oa"], sv["ob"], sv["oc"], sv["z"], mod, fw["w_br_a"], fw["w_br_b"], fw["w_br_c"], fw["w_out"],
            ctx_active=ctx_active)
        g["w_out"] = _matmul_tn(flat2(y), flat2(dmo), "dw_out", by_shard=False)
        g["w_br_a"] = _matmul_tn(flat2(sv["oa"]), flat2(dpa), "dw_br_a", by_shard=True)
        g["w_br_b"] = _matmul_tn(flat2(sv["ob"]), flat2(dpb), "dw_br_b", by_shard=True)
        g["w_br_c"] = _matmul_tn(flat2(sv["oc"]), flat2(dpc), "dw_br_c", by_shard=True)
        z = sv["z"]
        dqa, dka, dva, _ = hosting(_attn_bwd, cfg, l, sv["qa"], sv["ka"], z, va_blk, sink8, doa, sv["lse_a"], window=False, sink=False,
                                   ctx_attend=ctx_active, name="attn_a_bwd", **scatter_behind(l, ("w_mlp1", "w_mlp2")))
        dqc, dkc, dvc, dsink = hosting(_attn_bwd, cfg, l, sv["qc"], sv["kc"], z, vc_blk, sink8, doc, sv["lse_c"], window=True, sink=True,
                                       ctx_attend=ctx_active, name="attn_c_bwd", **scatter_behind(l, _MERGE_WEIGHTS))
        dzq, dgvec = _qknorm_bwd(cfg, l, z, gvec, cosf, sins, dqa, dka, dva, dqc, dkc, dvc)
        du, dwp, dps = _pool_bwd(cfg, l, dob, sv["pooled"], wp, ps3)
        dxs, dz, dmod_in, dgn1 = _in_proj_bwd(cfg, l, dzq, du, dzg, fw["w_in"], sv["xs"], dx1, norm1_3, mod, latent_only=(l == 0))
        dmod_cols, dbias = _dmod_pack(cfg, dmod_in, dg1, dmod_mlp)
        dmod_exchange = _Exchange([dmod_cols], scatter=True)
        n_last, w_last = 4, D // 4
        if l > 0:
            g["w_in"], dm_all = _matmul_tn(flat2(sv["h"]), flat2(dz), "dw_in", by_shard=True, exch=dmod_exchange)
        else:
            g_chunk, dm_all = _matmul_tn(flat2(sv["h"]), flat2(dz), "dw_in_c0", by_shard=True, a_cols=(0, w_last), exch=dmod_exchange)
        g["w_ada"], dcc = _adaln_bwd(cfg, l, cc_all, dm_all.reshape(8 * N_DEV, -1), w_ada)
        d_c_ctx = d_c_ctx + dcc[0]
        gv = dgvec[0]
        heads = lambda v, n: v.reshape(n, HEAD).sum(axis=0)
        small["b_ada"][l] = dbias[0]
        small["norm1"][l] = dgn1[0]
        small["norm2"][l] = dgn2[0]
        small["q_norm_a"][l] = heads(gv[0:QW], N_QHEADS)
        small["k_norm_a"][l] = heads(gv[QW:QW + KVW], N_KV)
        small["q_norm_c"][l] = heads(gv[QW + 2 * KVW:2 * QW + 2 * KVW], N_QHEADS)
        small["k_norm_c"][l] = heads(gv[2 * QW + 2 * KVW:2 * QW + 3 * KVW], N_KV)
        small["sink_c"][l] = dsink[0, :N_QHEADS]
        small["w_pool"][l] = jnp.stack([dwp[i * HEAD:(i + 1) * HEAD, i * HEAD:(i + 1) * HEAD] for i in range(len(POOL_WINDOWS))])
        small["pool_scale"][l] = dps[0]
    grad_x = dxs

    small_vals = {n: jnp.stack(v) for n, v in small.items()}
    small_vals["c_ctx"] = d_c_ctx
    small_packed = _pack_small(small_vals)
    w_in_parts = []
    for ci in range(1, n_last):
        riding = [g_chunk] + ([jnp.broadcast_to(small_packed[None], (N_DEV,) + small_packed.shape)] if ci == n_last - 1 else [])
        res = _matmul_tn(flat2(saved[0]["h"]), flat2(dz), "dw_in_c%d" % ci, by_shard=True, a_cols=(ci * w_last, w_last),
                         exch=_Exchange(riding, scatter=True))
        g_chunk = res[0]
        w_in_parts.append(res[1])
    small_parts = res[2]
    w_in_parts.append(_Exchange([g_chunk], scatter=True).alone("scatter_last_grads")[0])
    chunks = {(l, n): [parts[(l, n)]] for l in range(L) for n in _GATHERED if (l, n) in parts}
    chunks[(0, "w_in")] = w_in_parts
    for l in range(L):
        chunks[(l, "w_ada")] = [grads[l]["w_ada"][None]]
    stepped = {n: _adamw_sharded([chunks[(l, n)] for l in range(L)], weights[n], mom_m[n], mom_v[n], "adamw_" + n) for n in _SHARDED}
    stepped_small = _adamw_packed(small_parts, _pack_small(weights), _pack_small(mom_m), _pack_small(mom_v), "adamw_small")

    outs = []
    for i in range(4):
        res = {n: stepped[n][i] for n in _SHARDED}
        res.update(_unpack_small(stepped_small[i], weights))
        outs.append(res)
    order = ("c_ctx", "w_ada", "b_ada", "norm1", "norm2", "w_in", "q_norm_a", "k_norm_a", "q_norm_c", "k_norm_c", "sink_c",
             "w_pool", "pool_scale", "w_br_a", "w_br_b", "w_br_c", "w_out", "w_mlp1", "w_mlp2")
    return (loss, grad_x, *[res[n] for res in outs for n in order])
```
